```python
import math
import jax, jax.numpy as jnp
from jax import lax
import numpy as np

D_MODEL = 1024
BATCH = 8
SEQ = 2048
DEPTH = 1

ROPE_THETA = 500000.0
POS_OFFSET_MAX = 4096
MLA_HEADS = 8
MLA_NOPE = 64
MLA_ROPE = 32
MLA_V = 64
Q_LORA = 256
KV_LORA = 128
ATTN_QBLOCK = 128
MOBA_HEADS = 8
MOBA_HD = 64
MOBA_ROT = MOBA_HD // 4
MOBA_BLOCK = 256
MOBA_TOPK = 3
MOBA_QCHUNK = 128
MIX_WIDTH = MLA_HEADS * MLA_V + MOBA_HEADS * MOBA_HD
IN_COLS = Q_LORA + KV_LORA + MLA_ROPE + 3 * MOBA_HEADS * MOBA_HD
N_EXPERTS = 32
MOE_TOPK = 4
D_FF = 1024
SWIGLU_LIMIT = 7.0
SWIGLU_ALPHA = 1.702
MOE_BLOCK = 128
DN_ALPHA = (2.0 * DEPTH) ** 0.25
DN_BETA = (8.0 * DEPTH) ** -0.25

kernel_name = 'hymba_mla_moba_gptoss_deepnorm'


def _rms_norm(x, g, eps=1e-6):
    xf = x.astype(jnp.float32)
    y = xf * lax.rsqrt(jnp.mean(xf * xf, axis=-1, keepdims=True) + eps)
    return (y * g.astype(jnp.float32)).astype(x.dtype)


def _layer_norm(x, g, b, eps=1e-5):
    xf = x.astype(jnp.float32)
    mu = jnp.mean(xf, axis=-1, keepdims=True)
    var = jnp.mean(jnp.square(xf - mu), axis=-1, keepdims=True)
    y = (xf - mu) * lax.rsqrt(var + eps)
    return (y * g.astype(jnp.float32) + b.astype(jnp.float32)).astype(x.dtype)


def _rope_tables(positions, d_rot):
    inv_freq = ROPE_THETA ** (-jnp.arange(0, d_rot, 2, dtype=jnp.float32) / d_rot)
    ang = positions.astype(jnp.float32)[..., None] * inv_freq
    return jnp.cos(ang), jnp.sin(ang)


def _apply_rope(x, cos, sin):
    half = x.shape[-1] // 2
    x1, x2 = x[..., :half], x[..., half:]
    c = cos.astype(x.dtype)
    s = sin.astype(x.dtype)
    return jnp.concatenate([x1 * c - x2 * s, x2 * c + x1 * s], axis=-1)


def _mla(q_lat, kv_lat, k_rope_raw, cos, sin, q_a_norm, w_q_b, kv_a_norm, w_kv_b):
    B, T = q_lat.shape[:2]
    q = (_rms_norm(q_lat, q_a_norm) @ w_q_b).reshape(B, T, MLA_HEADS, MLA_NOPE + MLA_ROPE)
    q_nope = q[..., :MLA_NOPE]
    q_rope = _apply_rope(q[..., MLA_NOPE:], cos[:, :, None, :], sin[:, :, None, :])
    kv = (_rms_norm(kv_lat, kv_a_norm) @ w_kv_b).reshape(B, T, MLA_HEADS, MLA_NOPE + MLA_V)
    k_nope = kv[..., :MLA_NOPE].transpose(0, 2, 1, 3)
    v = kv[..., MLA_NOPE:].transpose(0, 2, 1, 3)
    k_rope = _apply_rope(k_rope_raw, cos, sin)
    nqb = T // ATTN_QBLOCK
    qn_b = q_nope.reshape(B, nqb, ATTN_QBLOCK, MLA_HEADS, MLA_NOPE).transpose(1, 0, 3, 2, 4)
    qr_b = q_rope.reshape(B, nqb, ATTN_QBLOCK, MLA_HEADS, MLA_ROPE).transpose(1, 0, 3, 2, 4)
    scale = 1.0 / math.sqrt(MLA_NOPE + MLA_ROPE)
    kpos = jnp.arange(T)

    def q_block(args):
        i, qn_i, qr_i = args
        s = (jnp.einsum('bhqd,bhkd->bhqk', qn_i, k_nope)
             + jnp.einsum('bhqd,bkd->bhqk', qr_i, k_rope)).astype(jnp.float32) * scale
        qpos = i * ATTN_QBLOCK + jnp.arange(ATTN_QBLOCK)
        s = jnp.where(kpos[None, :] <= qpos[:, None], s, -jnp.inf)
        p = jax.nn.softmax(s, axis=-1).astype(v.dtype)
        return jnp.einsum('bhqk,bhkd->bhqd', p, v)

    out = lax.map(q_block, (jnp.arange(nqb), qn_b, qr_b))
    return out.transpose(1, 0, 3, 2, 4).reshape(B, T, MLA_HEADS * MLA_V)


def _moba(q, k, v, cos, sin):
    B, T = q.shape[:2]
    H, d = MOBA_HEADS, MOBA_HD
    q = q.reshape(B, T, H, d)
    k = k.reshape(B, T, H, d)
    v = v.reshape(B, T, H, d)
    c4, s4 = cos[:, :, None, :], sin[:, :, None, :]
    q = jnp.concatenate([_apply_rope(q[..., :MOBA_ROT], c4, s4), q[..., MOBA_ROT:]], axis=-1)
    k = jnp.concatenate([_apply_rope(k[..., :MOBA_ROT], c4, s4), k[..., MOBA_ROT:]], axis=-1)
    q, k, v = (t.transpose(0, 2, 1, 3) for t in (q, k, v))
    nb = -(-T // MOBA_BLOCK)
    lp = nb * MOBA_BLOCK
    pad = ((0, 0), (0, 0), (0, lp - T), (0, 0))
    kblk = jnp.pad(k, pad).reshape(B, H, nb, MOBA_BLOCK, d)
    vblk = jnp.pad(v, pad).reshape(B, H, nb, MOBA_BLOCK, d)
    kmean = jnp.mean(kblk.astype(jnp.float32), axis=3)
    gate = jnp.einsum('bhtd,bhnd->bhtn', q.astype(jnp.float32), kmean)
    cur = jnp.arange(T) // MOBA_BLOCK
    gate = jnp.where(jnp.arange(nb)[None, :] < cur[:, None], gate, -jnp.inf)
    k_sel = min(MOBA_TOPK, nb)
    _, sel = lax.top_k(gate, k_sel)
    nqc = T // MOBA_QCHUNK
    q_items = q.reshape(B, H, nqc, MOBA_QCHUNK, d).transpose(0, 2, 1, 3, 4).reshape(B * nqc, H, MOBA_QCHUNK, d)
    sel_items = sel.reshape(B, H, nqc, MOBA_QCHUNK, k_sel).transpose(0, 2, 1, 3, 4).reshape(B * nqc, H, MOBA_QCHUNK, k_sel)
    scale = 1.0 / math.sqrt(d)
    hidx = jnp.arange(H)[:, None, None]

    def chunk(args):
        j, qc, selc = args
        b = j // nqc
        c = j % nqc
        kb = kblk[b]
        vb = vblk[b]
        ks = kb[hidx, selc]
        vs = vb[hidx, selc]
        qpos = c * MOBA_QCHUNK + jnp.arange(MOBA_QCHUNK)
        valid = jnp.arange(k_sel)[None, :] < (qpos // MOBA_BLOCK)[:, None]
        s_sel = jnp.einsum('hqd,hqkpd->hqkp', qc, ks).astype(jnp.float32) * scale
        s_sel = jnp.where(valid[None, :, :, None], s_sel, -jnp.inf).reshape(H, MOBA_QCHUNK, k_sel * MOBA_BLOCK)
        own = (c * MOBA_QCHUNK) // MOBA_BLOCK
        kown = lax.dynamic_index_in_dim(kb, own, axis=1, keepdims=False)
        vown = lax.dynamic_index_in_dim(vb, own, axis=1, keepdims=False)
        kpos = own * MOBA_BLOCK + jnp.arange(MOBA_BLOCK)
        s_own = jnp.einsum('hqd,hpd->hqp', qc, kown).astype(jnp.float32) * scale
        s_own = jnp.where(kpos[None, :] <= qpos[:, None], s_own, -jnp.inf)
        p = jax.nn.softmax(jnp.concatenate([s_sel, s_own], axis=-1), axis=-1).astype(vs.dtype)
        p_sel = p[..., :k_sel * MOBA_BLOCK].reshape(H, MOBA_QCHUNK, k_sel, MOBA_BLOCK)
        p_own = p[..., k_sel * MOBA_BLOCK:]
        return (jnp.einsum('hqkp,hqkpd->hqd', p_sel, vs)
                + jnp.einsum('hqp,hpd->hqd', p_own, vown))

    out = lax.map(chunk, (jnp.arange(B * nqc), q_items, sel_items))
    return out.reshape(B, nqc, H, MOBA_QCHUNK, d).transpose(0, 1, 3, 2, 4).reshape(B, T, H * d)


def _moe(x, w_router, b_router, w_gate, b_gate, w_up, b_up, w_down, b_down):
    B, T, D = x.shape
    n_tok = B * T
    xf = x.reshape(n_tok, D)
    logits = (xf @ w_router + b_router).astype(jnp.float32)
    top_vals, top_idx = lax.top_k(logits, MOE_TOPK)
    gates = jax.nn.softmax(top_vals, axis=-1)
    n_asg = n_tok * MOE_TOPK
    e_flat = top_idx.reshape(n_asg).astype(jnp.int32)
    order = jnp.argsort(e_flat, stable=True)
    sorted_e = e_flat[order]
    counts = jnp.bincount(e_flat, length=N_EXPERTS).astype(jnp.int32)
    padded = (counts + MOE_BLOCK - 1) // MOE_BLOCK * MOE_BLOCK
    pad_end = jnp.cumsum(padded).astype(jnp.int32)
    pad_start = pad_end - padded
    grp_start = (jnp.cumsum(counts) - counts).astype(jnp.int32)
    rank = jnp.arange(n_asg, dtype=jnp.int32) - grp_start[sorted_e]
    dest_sorted = (pad_start[sorted_e] + rank).astype(jnp.int32)
    dest = jnp.zeros((n_asg,), jnp.int32).at[order].set(dest_sorted)
    n_rows = n_asg + N_EXPERTS * MOE_BLOCK
    n_blk = n_rows // MOE_BLOCK
    row_tok = jnp.full((n_rows,), n_tok, jnp.int32).at[dest_sorted].set((order // MOE_TOPK).astype(jnp.int32))
    blk_start = jnp.arange(n_blk, dtype=jnp.int32) * MOE_BLOCK
    blk_expert = jnp.clip(jnp.searchsorted(pad_end, blk_start, side='right'), 0, N_EXPERTS - 1)
    x_pad = jnp.concatenate([xf, jnp.zeros((1, D), xf.dtype)], axis=0)

    def expert_block(args):
        e, rows = args
        xb = x_pad[rows]
        g = jnp.minimum(xb @ w_gate[e] + b_gate[e], SWIGLU_LIMIT)
        u = jnp.clip(xb @ w_up[e] + b_up[e], -SWIGLU_LIMIT, SWIGLU_LIMIT)
        h = g * jax.nn.sigmoid(SWIGLU_ALPHA * g) * (u + 1.0)
        return h @ w_down[e] + b_down[e]

    y_rows = lax.map(expert_block, (blk_expert, row_tok.reshape(n_blk, MOE_BLOCK))).reshape(n_rows, D)
    y = jnp.einsum('nk,nkd->nd', gates.astype(x.dtype), y_rows[dest.reshape(n_tok, MOE_TOPK)])
    return y.reshape(B, T, D)


def setup_inputs(seed: int = 0) -> dict:
    key = jax.random.key(seed)
    ks = jax.random.split(key, 32)
    L, D = DEPTH, D_MODEL
    nrm = jax.random.normal
    sD = D ** -0.5
    x = nrm(ks[0], (BATCH, SEQ, D), jnp.float32)
    offs = jax.random.randint(ks[1], (BATCH, 1), 0, POS_OFFSET_MAX, dtype=jnp.int32)
    positions = (offs + jnp.arange(SEQ, dtype=jnp.int32)[None, :]).astype(jnp.int32)
    mh = MOBA_HEADS * MOBA_HD
    w_in = jnp.concatenate([
        nrm(ks[2], (L, D, Q_LORA)) * sD,
        nrm(ks[3], (L, D, KV_LORA)) * sD,
        nrm(ks[4], (L, D, MLA_ROPE)) * sD,
        nrm(ks[5], (L, D, mh)) * sD,
        nrm(ks[6], (L, D, mh)) * sD,
        nrm(ks[7], (L, D, mh)) * sD * DN_BETA,
    ], axis=-1)
    q_a_norm = 1.0 + 0.01 * nrm(ks[8], (L, Q_LORA))
    w_q_b = nrm(ks[9], (L, Q_LORA, MLA_HEADS * (MLA_NOPE + MLA_ROPE))) * Q_LORA ** -0.5
    kv_a_norm = 1.0 + 0.01 * nrm(ks[10], (L, KV_LORA))
    wk_b = nrm(ks[11], (L, KV_LORA, MLA_HEADS, MLA_NOPE)) * KV_LORA ** -0.5
    wv_b = nrm(ks[12], (L, KV_LORA, MLA_HEADS, MLA_V)) * KV_LORA ** -0.5 * DN_BETA
    w_kv_b = jnp.concatenate([wk_b, wv_b], axis=-1).reshape(L, KV_LORA, MLA_HEADS * (MLA_NOPE + MLA_V))
    w_o = nrm(ks[13], (L, MIX_WIDTH, D)) * MIX_WIDTH ** -0.5 * DN_BETA
    ln1_g = 1.0 + 0.01 * nrm(ks[14], (L, D))
    ln1_b = 0.01 * nrm(ks[15], (L, D))
    w_router = nrm(ks[16], (L, D, N_EXPERTS)) * sD
    b_router = 0.01 * nrm(ks[17], (L, N_EXPERTS))
    w_gate = nrm(ks[18], (L, N_EXPERTS, D, D_FF)) * sD
    b_gate = 0.01 * nrm(ks[19], (L, N_EXPERTS, D_FF))
    w_up = nrm(ks[20], (L, N_EXPERTS, D, D_FF)) * sD
    b_up = 0.01 * nrm(ks[21], (L, N_EXPERTS, D_FF))
    w_down = nrm(ks[22], (L, N_EXPERTS, D_FF, D)) * D_FF ** -0.5 * DN_BETA
    b_down = 0.01 * nrm(ks[23], (L, N_EXPERTS, D))
    ln2_g = 1.0 + 0.01 * nrm(ks[24], (L, D))
    ln2_b = 0.01 * nrm(ks[25], (L, D))
    return {'x': x, 'positions': positions, 'w_in': w_in, 'q_a_norm': q_a_norm, 'w_q_b': w_q_b,
            'kv_a_norm': kv_a_norm, 'w_kv_b': w_kv_b, 'w_o': w_o, 'ln1_g': ln1_g, 'ln1_b': ln1_b,
            'w_router': w_router, 'b_router': b_router, 'w_gate': w_gate, 'b_gate': b_gate,
            'w_up': w_up, 'b_up': b_up, 'w_down': w_down, 'b_down': b_down,
            'ln2_g': ln2_g, 'ln2_b': ln2_b}


def reference(x, positions, w_in, q_a_norm, w_q_b, kv_a_norm, w_kv_b, w_o, ln1_g, ln1_b,
              w_router, b_router, w_gate, b_gate, w_up, b_up, w_down, b_down, ln2_g, ln2_b):
    cos_a, sin_a = _rope_tables(positions, MLA_ROPE)
    cos_b, sin_b = _rope_tables(positions, MOBA_ROT)
    o1 = Q_LORA
    o2 = o1 + KV_LORA
    o3 = o2 + MLA_ROPE
    mh = MOBA_HEADS * MOBA_HD
    h = x
    for l in range(DEPTH):
        proj = h @ w_in[l]
        a = _mla(proj[..., :o1], proj[..., o1:o2], proj[..., o2:o3], cos_a, sin_a,
                 q_a_norm[l], w_q_b[l], kv_a_norm[l], w_kv_b[l])
        m = _moba(proj[..., o3:o3 + mh], proj[..., o3 + mh:o3 + 2 * mh], proj[..., o3 + 2 * mh:],
                  cos_b, sin_b)
        mix = jnp.concatenate([a, m], axis=-1) @ w_o[l]
        h = _layer_norm(DN_ALPHA * h + mix, ln1_g[l], ln1_b[l])
        ffn = _moe(h, w_router[l], b_router[l], w_gate[l], b_gate[l], w_up[l], b_up[l], w_down[l], b_down[l])
        h = _layer_norm(DN_ALPHA * h + ffn, ln2_g[l], ln2_b[l])
    return h
```

```python
import functools
import math

import jax
import jax.numpy as jnp
from jax import lax
from jax.experimental import pallas as pl
from jax.experimental.pallas import tpu as pltpu

ROPE_THETA = 500000.0
MLA_HEADS = 8
MLA_NOPE = 64
MLA_ROPE = 32
MLA_V = 64
Q_LORA = 256
KV_LORA = 128
MOBA_HEADS = 8
MOBA_HD = 64
MOBA_ROT = MOBA_HD // 4
MOBA_BLOCK = 256
MOBA_TOPK = 3
N_EXPERTS = 32
MOE_TOPK = 4
SWIGLU_LIMIT = 7.0
SWIGLU_ALPHA = 1.702
RMS_EPS = 1e-6
LN_EPS = 1e-5

LANES = 128
SUBLANES = 8
VMEM_LIMIT_BYTES = 56 * 1024 * 1024

SLOT = LANES
TQ = MOBA_BLOCK
ROW_TILE = 256
MOE_GROUP = 256
DISPATCH_TILE = 256
COMBINE_TILE = 128
NEG_BIG = -(2.0 ** 100)

F32 = jnp.float32
BF16 = jnp.bfloat16
NT_DIMS = (((1,), (1,)), ((), ()))


def _dot(a, b, precision=None):
    return jnp.dot(a, b, preferred_element_type=F32, precision=precision)


def _dot_nt(a, b, precision=None):
    return lax.dot_general(a, b, NT_DIMS, preferred_element_type=F32, precision=precision)


def _rms(x, g):
    return x * lax.rsqrt(jnp.mean(x * x, axis=-1, keepdims=True) + RMS_EPS) * g


def _layer_norm(x, g, b):
    mu = jnp.mean(x, axis=-1, keepdims=True)
    xc = x - mu
    var = jnp.mean(xc * xc, axis=-1, keepdims=True)
    return xc * lax.rsqrt(var + LN_EPS) * g + b


_C_QL = 0
_C_KVL = _C_QL + Q_LORA
_C_KR = _C_KVL + KV_LORA
_C_KRS = _C_KR + SLOT
_C_MQ = _C_KRS + SLOT
_MH = MOBA_HEADS * MOBA_HD
_C_MQS = _C_MQ + _MH
_C_MK = _C_MQS + _MH
_C_MKS = _C_MK + _MH
_C_MV = _C_MKS + _MH
_C_END = _C_MV + _MH


def _prep_kernel(x_ref, win_ref, wq_ref, wqs_ref, wk_ref, wv_ref, qg_ref, kvg_ref,
                 ca_ref, sa_ref, cb_ref, sb_ref,
                 qa_ref, ka_ref, va_ref, mq_ref, mk_ref, mv_ref, selb_ref, kmean_scr):
    c = pl.program_id(1)
    xb = x_ref[...].astype(BF16)
    ca = ca_ref[...]
    sa = sa_ref[...]
    cb = cb_ref[...]
    sb = sb_ref[...]

    ql = _dot(xb, win_ref[:, _C_QL:_C_KVL])
    kvl = _dot(xb, win_ref[:, _C_KVL:_C_KR])
    kr = _dot(xb, win_ref[:, _C_KR:_C_KRS])
    krs = _dot(xb, win_ref[:, _C_KRS:_C_MQ])
    qn = _rms(ql, qg_ref[...]).astype(BF16)
    kvn = _rms(kvl, kvg_ref[...]).astype(BF16)
    q = _dot(qn, wq_ref[...])
    qs = _dot(qn, wqs_ref[...])
    kn = _dot(kvn, wk_ref[...])
    v = _dot(kvn, wv_ref[...])
    scale_a = 1.0 / math.sqrt(MLA_NOPE + MLA_ROPE)
    kro = kr * ca + krs * sa
    for h in range(MLA_HEADS):
        sl = slice(h * SLOT, (h + 1) * SLOT)
        qa_ref[:, sl] = ((q[:, sl] * ca + qs[:, sl] * sa) * scale_a).astype(BF16)
        ka_ref[:, sl] = (kn[:, sl] + kro).astype(BF16)
    va_ref[...] = v.astype(BF16)

    mq = _dot(xb, win_ref[:, _C_MQ:_C_MQS])
    mqs = _dot(xb, win_ref[:, _C_MQS:_C_MK])
    mk = _dot(xb, win_ref[:, _C_MK:_C_MKS])
    mks = _dot(xb, win_ref[:, _C_MKS:_C_MV])
    mv = _dot(xb, win_ref[:, _C_MV:_C_END])
    npair = _MH // LANES
    mq_rot = jnp.concatenate(
        [mq[:, j * LANES:(j + 1) * LANES] * cb + mqs[:, j * LANES:(j + 1) * LANES] * sb for j in range(npair)], axis=1)
    mk_rot = jnp.concatenate(
        [mk[:, j * LANES:(j + 1) * LANES] * cb + mks[:, j * LANES:(j + 1) * LANES] * sb for j in range(npair)], axis=1)
    scale_b = 1.0 / math.sqrt(MOBA_HD)
    mq_ref[...] = (mq_rot * scale_b).astype(BF16)
    mk_ref[...] = mk_rot.astype(BF16)
    mv_ref[...] = mv.astype(BF16)

    nrow = MOBA_HEADS * SUBLANES
    @pl.when(c == 0)
    def _():
        kmean_scr[...] = jnp.zeros_like(kmean_scr)

    row_i = lax.broadcasted_iota(jnp.int32, (nrow, _MH), 0)
    lane_i = lax.broadcasted_iota(jnp.int32, (nrow, _MH), 1)
    kmean_c = jnp.mean(mk_rot, axis=0, keepdims=True)
    put = ((row_i % SUBLANES) == c) & ((lane_i // MOBA_HD) == (row_i // SUBLANES))
    table = kmean_scr[...]
    gate_t = _dot_nt(table, mq_rot, precision=lax.Precision.HIGHEST)
    kmean_scr[...] = jnp.where(put, jnp.broadcast_to(kmean_c, (nrow, _MH)), table)

    n_idx = lax.broadcasted_iota(jnp.int32, (SUBLANES, ROW_TILE), 0)
    valid = n_idx < c
    rows = []
    for h in range(MOBA_HEADS):
        g = jnp.where(valid, gate_t[h * SUBLANES:(h + 1) * SUBLANES, :], -jnp.inf)
        rank = jnp.zeros((SUBLANES, ROW_TILE), jnp.int32)
        for k in range(1, SUBLANES):
            other = pltpu.roll(g, k, axis=0)
            other_n = pltpu.roll(n_idx, k, axis=0)
            beats = (other > g) | ((other == g) & (other_n < n_idx))
            rank = rank + beats.astype(jnp.int32)
        sel = valid & (rank < MOBA_TOPK)
        rows.append(jnp.where(sel, 0.0, NEG_BIG))
    rows.append(jnp.zeros((LANES - nrow, ROW_TILE), F32))
    selb_ref[...] = jnp.concatenate(rows, axis=0).T


def _prep(x2, win, wq, wqs, wk, wv, qg, kvg, ca, sa, cb, sb, B, T):
    N, D = x2.shape
    nt = T // ROW_TILE
    row = lambda b, c: (b * nt + c, 0)
    full = lambda b, c: (0, 0)

    def rows(w):
        return pl.BlockSpec((ROW_TILE, w), row)

    def whole(a):
        return pl.BlockSpec(a.shape, full)

    out_shapes = (
        jax.ShapeDtypeStruct((N, MLA_HEADS * SLOT), BF16),
        jax.ShapeDtypeStruct((N, MLA_HEADS * SLOT), BF16),
        jax.ShapeDtypeStruct((N, MLA_HEADS * MLA_V), BF16),
        jax.ShapeDtypeStruct((N, _MH), BF16),
        jax.ShapeDtypeStruct((N, _MH), BF16),
        jax.ShapeDtypeStruct((N, _MH), BF16),
        jax.ShapeDtypeStruct((N, LANES), F32),
    )
    return pl.pallas_call(
        _prep_kernel,
        out_shape=out_shapes,
        grid=(B, nt),
        in_specs=[rows(D), whole(win), whole(wq), whole(wqs), whole(wk), whole(wv), whole(qg), whole(kvg),
                  rows(LANES), rows(LANES), rows(LANES), rows(LANES)],
        out_specs=(rows(MLA_HEADS * SLOT), rows(MLA_HEADS * SLOT), rows(MLA_HEADS * MLA_V),
                   rows(_MH), rows(_MH), rows(_MH), rows(LANES)),
        scratch_shapes=[pltpu.VMEM((MOBA_HEADS * SUBLANES, _MH), F32)],
        compiler_params=pltpu.CompilerParams(dimension_semantics=("arbitrary", "arbitrary"),
                                             vmem_limit_bytes=VMEM_LIMIT_BYTES),
        name="prep",
    )(x2, win, wq, wqs, wk, wv, qg, kvg, ca, sa, cb, sb)


def _flash_first(q, k, v):
    s = _dot_nt(q, k)
    r = lax.broadcasted_iota(jnp.int32, s.shape, 0)
    cidx = lax.broadcasted_iota(jnp.int32, s.shape, 1)
    s = jnp.where(cidx <= r, s, -jnp.inf)
    m = jnp.max(s, axis=-1, keepdims=True)
    p = jnp.exp(s - m)
    l = jnp.sum(p, axis=-1, keepdims=True)
    acc = _dot(p.astype(BF16), v)
    return m, l, acc


def _flash_step(carry, s, v):
    m, l, acc = carry
    m_new = jnp.maximum(m, jnp.max(s, axis=-1, keepdims=True))
    alpha = jnp.exp(m - m_new)
    p = jnp.exp(s - m_new)
    l = alpha * l + jnp.sum(p, axis=-1, keepdims=True)
    acc = alpha * acc + _dot(p.astype(BF16), v)
    return m_new, l, acc


def _mla_kernel(q_ref, k_ref, v_ref, o_ref):
    i = pl.program_id(2)
    d0 = pl.multiple_of(i * TQ, TQ)
    outs = []
    for hh in range(2):
        sl = slice(hh * SLOT, (hh + 1) * SLOT)
        q = q_ref[:, sl]
        carry = _flash_first(q, k_ref[pl.ds(d0, TQ), sl], v_ref[pl.ds(d0, TQ), :])

        def body(j, carry, q=q, sl=sl):
            j0 = pl.multiple_of(j * TQ, TQ)
            s = _dot_nt(q, k_ref[pl.ds(j0, TQ), sl])
            return _flash_step(carry, s, v_ref[pl.ds(j0, TQ), :])

        m, l, acc = lax.fori_loop(0, i, body, carry)
        outs.append(acc / l)
    lane = lax.broadcasted_iota(jnp.int32, outs[0].shape, 1)
    o_ref[...] = jnp.where(lane < MLA_V, outs[0], outs[1]).astype(o_ref.dtype)


def _mla(qa, ka, va, B, T):
    N = qa.shape[0]
    nq = T // TQ
    npair = MLA_HEADS // 2
    return pl.pallas_call(
        _mla_kernel,
        out_shape=jax.ShapeDtypeStruct((N, MLA_HEADS * MLA_V), BF16),
        grid=(B, npair, nq),
        in_specs=[pl.BlockSpec((TQ, 2 * SLOT), lambda b, p, i: (b * nq + i, p)),
                  pl.BlockSpec((T, 2 * SLOT), lambda b, p, i: (b, p)),
                  pl.BlockSpec((T, LANES), lambda b, p, i: (b, p))],
        out_specs=pl.BlockSpec((TQ, LANES), lambda b, p, i: (b * nq + i, p)),
        compiler_params=pltpu.CompilerParams(dimension_semantics=("arbitrary", "arbitrary", "arbitrary"),
                                             vmem_limit_bytes=VMEM_LIMIT_BYTES),
        name="mla_attention",
    )(qa, ka, va)


def _moba_kernel(q_ref, k_ref, v_ref, sb_ref, o_ref):
    p = pl.program_id(1)
    i = pl.program_id(2)
    d0 = pl.multiple_of(i * TQ, TQ)
    q2 = q_ref[...]
    selb = sb_ref[...]
    lane = lax.broadcasted_iota(jnp.int32, q2.shape, 1)
    outs = []
    for hh in range(2):
        in_head = (lane >= hh * MOBA_HD) & (lane < (hh + 1) * MOBA_HD)
        q = jnp.where(in_head, q2, jnp.zeros_like(q2))
        carry = _flash_first(q, k_ref[pl.ds(d0, TQ), :], v_ref[pl.ds(d0, TQ), :])
        head = p * 2 + hh

        def body(n, carry, q=q, head=head):
            n0 = pl.multiple_of(n * TQ, TQ)
            bias = jnp.sum(jnp.where(lane == head * SUBLANES + n, selb, 0.0), axis=-1, keepdims=True)
            s = _dot_nt(q, k_ref[pl.ds(n0, TQ), :]) + bias
            return _flash_step(carry, s, v_ref[pl.ds(n0, TQ), :])

        m, l, acc = lax.fori_loop(0, i, body, carry)
        outs.append(acc / l)
    o_ref[...] = jnp.where(lane < MOBA_HD, outs[0], outs[1]).astype(o_ref.dtype)


def _moba(mq, mk, mv, selb, B, T):
    N = mq.shape[0]
    nq = T // TQ
    npair = MOBA_HEADS // 2
    return pl.pallas_call(
        _moba_kernel,
        out_shape=jax.ShapeDtypeStruct((N, _MH), BF16),
        grid=(B, npair, nq),
        in_specs=[pl.BlockSpec((TQ, LANES), lambda b, p, i: (b * nq + i, p)),
                  pl.BlockSpec((T, LANES), lambda b, p, i: (b, p)),
                  pl.BlockSpec((T, LANES), lambda b, p, i: (b, p)),
                  pl.BlockSpec((TQ, LANES), lambda b, p, i: (b * nq + i, 0))],
        out_specs=pl.BlockSpec((TQ, LANES), lambda b, p, i: (b * nq + i, p)),
        compiler_params=pltpu.CompilerParams(dimension_semantics=("arbitrary", "arbitrary", "arbitrary"),
                                             vmem_limit_bytes=VMEM_LIMIT_BYTES),
        name="moba_attention",
    )(mq, mk, mv, selb)


def _oproj_kernel(alpha, a_ref, m_ref, x_ref, wo_ref, g_ref, b_ref, wr_ref, br_ref,
                  x1_ref, route_ref, gates_ref, cnt_ref, carry_scr):
    i = pl.program_id(0)

    @pl.when(i == 0)
    def _():
        carry_scr[...] = jnp.zeros_like(carry_scr)

    wa = a_ref.shape[1]
    mix = _dot(a_ref[...], wo_ref[:wa, :]) + _dot(m_ref[...], wo_ref[wa:, :])
    x1 = _layer_norm(alpha * x_ref[...] + mix, g_ref[...], b_ref[...])
    x1_ref[...] = x1

    logits = _dot(x1, wr_ref[...], precision=lax.Precision.HIGHEST) + br_ref[...]
    lane = lax.broadcasted_iota(jnp.int32, logits.shape, 1)
    lane_f = lane.astype(F32)
    vals, idxs = [], []
    work = logits
    for _ in range(MOE_TOPK):
        mx = jnp.max(work, axis=-1, keepdims=True)
        ix = jnp.min(jnp.where(work == mx, lane_f, float(LANES)), axis=-1, keepdims=True).astype(jnp.int32)
        vals.append(mx)
        idxs.append(ix)
        work = jnp.where(lane == ix, -jnp.inf, work)
    exps = [jnp.exp(v - vals[0]) for v in vals]
    den = exps[0]
    for e in exps[1:]:
        den = den + e

    onehot = jnp.zeros(logits.shape, F32)
    for ix in idxs:
        onehot = onehot + (lane == ix).astype(F32)
    rows = logits.shape[0]
    r = lax.broadcasted_iota(jnp.int32, (rows, rows), 0)
    cidx = lax.broadcasted_iota(jnp.int32, (rows, rows), 1)
    lower = (cidx < r).astype(BF16)
    carry = carry_scr[0:1, :]
    before = _dot(lower, onehot.astype(BF16)) + carry
    route = jnp.zeros(logits.shape, jnp.int32)
    gates = jnp.zeros(logits.shape, F32)
    for k in range(MOE_TOPK):
        rank = jnp.sum(jnp.where(lane == idxs[k], before, 0.0), axis=-1, keepdims=True).astype(jnp.int32)
        route = jnp.where(lane == k, idxs[k], route)
        route = jnp.where(lane == MOE_TOPK + k, rank, route)
        gates = jnp.where(lane == k, exps[k] / den, gates)
    route_ref[...] = route
    gates_ref[...] = gates
    new_carry = carry + jnp.sum(onehot, axis=0, keepdims=True)
    carry_scr[...] = jnp.broadcast_to(new_carry, carry_scr.shape)
    cnt_ref[...] = jnp.broadcast_to(new_carry, cnt_ref.shape)


def _oproj(a, m, x2, wo, g1, b1, wr, br, alpha):
    N, D = x2.shape
    nt = N // ROW_TILE
    row = lambda i: (i, 0)
    full = lambda i: (0, 0)
    return pl.pallas_call(
        functools.partial(_oproj_kernel, alpha),
        out_shape=(jax.ShapeDtypeStruct((N, D), F32),
                   jax.ShapeDtypeStruct((N, LANES), jnp.int32),
                   jax.ShapeDtypeStruct((N, LANES), F32),
                   jax.ShapeDtypeStruct((SUBLANES, LANES), F32)),
        grid=(nt,),
        in_specs=[pl.BlockSpec((ROW_TILE, a.shape[1]), row), pl.BlockSpec((ROW_TILE, m.shape[1]), row),
                  pl.BlockSpec((ROW_TILE, D), row), pl.BlockSpec(wo.shape, full),
                  pl.BlockSpec(g1.shape, full), pl.BlockSpec(b1.shape, full),
                  pl.BlockSpec(wr.shape, full), pl.BlockSpec(br.shape, full)],
        out_specs=(pl.BlockSpec((ROW_TILE, D), row), pl.BlockSpec((ROW_TILE, LANES), row),
                   pl.BlockSpec((ROW_TILE, LANES), row), pl.BlockSpec((SUBLANES, LANES), full)),
        scratch_shapes=[pltpu.VMEM((SUBLANES, LANES), F32)],
        compiler_params=pltpu.CompilerParams(dimension_semantics=("arbitrary",),
                                             vmem_limit_bytes=VMEM_LIMIT_BYTES),
        name="oproj_router",
    )(a, m, x2, wo, g1, b1, wr, br)


def _dispatch_kernel(dest_ref, tail_ref, x1_ref, xs_ref, zero_scr, sem):
    i = pl.program_id(0)

    @pl.when(i == 0)
    def _():
        zero_scr[...] = jnp.zeros_like(zero_scr)

        def tail_copy(e):
            return pltpu.make_async_copy(zero_scr, xs_ref.at[pl.ds(pl.multiple_of(tail_ref[e], MOE_GROUP), MOE_GROUP)], sem)

        def start(e, _):
            @pl.when(tail_ref[e] >= 0)
            def _():
                tail_copy(e).start()
            return 0

        def wait(e, _):
            @pl.when(tail_ref[e] >= 0)
            def _():
                tail_copy(e).wait()
            return 0

        lax.fori_loop(0, N_EXPERTS, start, 0)
        lax.fori_loop(0, N_EXPERTS, wait, 0)

        def spare_copy(blk):
            return pltpu.make_async_copy(zero_scr, xs_ref.at[pl.ds(pl.multiple_of(blk * MOE_GROUP, MOE_GROUP), MOE_GROUP)], sem)

        def start_spare(blk, _):
            spare_copy(blk).start()
            return 0

        def wait_spare(blk, _):
            spare_copy(blk).wait()
            return 0

        nblk = xs_ref.shape[0] // MOE_GROUP
        lax.fori_loop(tail_ref[N_EXPERTS], nblk, start_spare, 0)
        lax.fori_loop(tail_ref[N_EXPERTS], nblk, wait_spare, 0)

    base = i * DISPATCH_TILE

    def row_copy(t, k):
        d = dest_ref[(base + t) * MOE_TOPK + k]
        return pltpu.make_async_copy(x1_ref.at[pl.ds(base + t, 1)], xs_ref.at[pl.ds(d, 1)], sem)

    def start_rows(t, _):
        for k in range(MOE_TOPK):
            row_copy(t, k).start()
        return 0

    def wait_rows(t, _):
        for k in range(MOE_TOPK):
            row_copy(t, k).wait()
        return 0

    lax.fori_loop(0, DISPATCH_TILE, start_rows, 0)
    lax.fori_loop(0, DISPATCH_TILE, wait_rows, 0)


def _dispatch(dest, tail, x1, n_rows):
    N, D = x1.shape
    return pl.pallas_call(
        _dispatch_kernel,
        out_shape=jax.ShapeDtypeStruct((n_rows, D), F32),
        grid_spec=pltpu.PrefetchScalarGridSpec(
            num_scalar_prefetch=2,
            grid=(N // DISPATCH_TILE,),
            in_specs=[pl.BlockSpec(memory_space=pl.ANY)],
            out_specs=pl.BlockSpec(memory_space=pl.ANY),
            scratch_shapes=[pltpu.VMEM((MOE_GROUP, D), F32), pltpu.SemaphoreType.DMA(())],
        ),
        compiler_params=pltpu.CompilerParams(dimension_semantics=("arbitrary",),
                                             vmem_limit_bytes=VMEM_LIMIT_BYTES),
        name="dispatch",
    )(dest, tail, x1)


def _experts_kernel(be_ref, nused_ref, x_ref, wg_ref, bg_ref, wu_ref, bu_ref, wd_ref, bd_ref,
                    y_ref, wg_bf, wu_bf, wd_bf):
    i = pl.program_id(0)
    prev = be_ref[jnp.maximum(i - 1, 0)]
    changed = (i == 0) | (be_ref[i] != prev)
    active = i < nused_ref[0]

    @pl.when(active & changed)
    def _():
        wg_bf[...] = wg_ref[0].astype(BF16)
        wu_bf[...] = wu_ref[0].astype(BF16)
        wd_bf[...] = wd_ref[0].astype(BF16)

    @pl.when(active)
    def _():
        xb = x_ref[...].astype(BF16)
        g = jnp.minimum(_dot(xb, wg_bf[...]) + bg_ref[0], SWIGLU_LIMIT)
        u = jnp.clip(_dot(xb, wu_bf[...]) + bu_ref[0], -SWIGLU_LIMIT, SWIGLU_LIMIT)
        h = g * (1.0 / (1.0 + jnp.exp(-SWIGLU_ALPHA * g))) * (u + 1.0)
        y_ref[...] = _dot(h.astype(BF16), wd_bf[...]) + bd_ref[0]

    @pl.when(jnp.logical_not(active))
    def _():
        y_ref[...] = jnp.zeros_like(y_ref)


def _experts(blk_expert, n_used, xs, wg, bg, wu, bu, wd, bd):
    n_rows, D = xs.shape
    E, _, F = wg.shape
    nblk = n_rows // MOE_GROUP

    def rowmap(i, be, nu):
        return (jnp.minimum(i, nu[0] - 1), 0)

    def wmap(i, be, nu):
        return (be[i], 0, 0)

    return pl.pallas_call(
        _experts_kernel,
        out_shape=jax.ShapeDtypeStruct((n_rows, D), F32),
        grid_spec=pltpu.PrefetchScalarGridSpec(
            num_scalar_prefetch=2,
            grid=(nblk,),
            in_specs=[pl.BlockSpec((MOE_GROUP, D), rowmap),
                      pl.BlockSpec((1, D, F), wmap), pl.BlockSpec((1, 1, F), wmap),
                      pl.BlockSpec((1, D, F), wmap), pl.BlockSpec((1, 1, F), wmap),
                      pl.BlockSpec((1, F, D), wmap), pl.BlockSpec((1, 1, D), wmap)],
            out_specs=pl.BlockSpec((MOE_GROUP, D), lambda i, be, nu: (i, 0)),
            scratch_shapes=[pltpu.VMEM((D, F), BF16), pltpu.VMEM((D, F), BF16), pltpu.VMEM((F, D), BF16)],
        ),
        compiler_params=pltpu.CompilerParams(dimension_semantics=("arbitrary",),
                                             vmem_limit_bytes=VMEM_LIMIT_BYTES),
        name="experts",
    )(blk_expert, n_used, xs, wg, bg, wu, bu, wd, bd)


def _combine_kernel(alpha, dest_ref, gates_ref, x1_ref, g_ref, b_ref, y_ref, o_ref, ybuf, sem):
    i = pl.program_id(0)
    base = i * COMBINE_TILE

    def row_copy(t, k):
        d = dest_ref[(base + t) * MOE_TOPK + k]
        return pltpu.make_async_copy(y_ref.at[pl.ds(d, 1)], ybuf.at[k, pl.ds(t, 1)], sem)

    def start_rows(t, _):
        for k in range(MOE_TOPK):
            row_copy(t, k).start()
        return 0

    def wait_rows(t, _):
        for k in range(MOE_TOPK):
            row_copy(t, k).wait()
        return 0

    lax.fori_loop(0, COMBINE_TILE, start_rows, 0)
    lax.fori_loop(0, COMBINE_TILE, wait_rows, 0)

    gates = gates_ref[...]
    ffn = gates[:, 0:1] * ybuf[0]
    for k in range(1, MOE_TOPK):
        ffn = ffn + gates[:, k:k + 1] * ybuf[k]
    o_ref[...] = _layer_norm(alpha * x1_ref[...] + ffn, g_ref[...], b_ref[...])


def _combine(dest, gates, x1, g2, b2, y_rows, alpha):
    N, D = x1.shape
    row = lambda i, d: (i, 0)
    full = lambda i, d: (0, 0)
    return pl.pallas_call(
        functools.partial(_combine_kernel, alpha),
        out_shape=jax.ShapeDtypeStruct((N, D), F32),
        grid_spec=pltpu.PrefetchScalarGridSpec(
            num_scalar_prefetch=1,
            grid=(N // COMBINE_TILE,),
            in_specs=[pl.BlockSpec((COMBINE_TILE, LANES), row), pl.BlockSpec((COMBINE_TILE, D), row),
                      pl.BlockSpec(g2.shape, full), pl.BlockSpec(b2.shape, full),
                      pl.BlockSpec(memory_space=pl.ANY)],
            out_specs=pl.BlockSpec((COMBINE_TILE, D), row),
            scratch_shapes=[pltpu.VMEM((MOE_TOPK, COMBINE_TILE, D), F32), pltpu.SemaphoreType.DMA(())],
        ),
        compiler_params=pltpu.CompilerParams(dimension_semantics=("arbitrary",),
                                             vmem_limit_bytes=VMEM_LIMIT_BYTES),
        name="combine",
    )(dest, gates, x1, g2, b2, y_rows)


def _rot_partner(w, half):
    return jnp.concatenate([-w[..., half:2 * half], w[..., :half]], axis=-1)


def _layer_weights(w_in, w_q_b, w_kv_b):
    D = w_in.shape[0]
    o1 = Q_LORA
    o2 = o1 + KV_LORA
    o3 = o2 + MLA_ROPE
    w_ql, w_kvl, w_kr = w_in[:, :o1], w_in[:, o1:o2], w_in[:, o2:o3]
    w_mq, w_mk, w_mv = w_in[:, o3:o3 + _MH], w_in[:, o3 + _MH:o3 + 2 * _MH], w_in[:, o3 + 2 * _MH:]
    zpad = lambda n: jnp.zeros((D, n), w_in.dtype)
    tail = SLOT - MLA_NOPE - MLA_ROPE
    kr_slot = jnp.concatenate([zpad(MLA_NOPE), w_kr, zpad(tail)], axis=1)
    krs_slot = jnp.concatenate([zpad(MLA_NOPE), _rot_partner(w_kr, MLA_ROPE // 2), zpad(tail)], axis=1)

    def moba_partner(w):
        w3 = w.reshape(D, MOBA_HEADS, MOBA_HD)
        part = jnp.concatenate([_rot_partner(w3[..., :MOBA_ROT], MOBA_ROT // 2),
                                jnp.zeros((D, MOBA_HEADS, MOBA_HD - MOBA_ROT), w.dtype)], axis=-1)
        return part.reshape(D, _MH)

    win = jnp.concatenate([w_ql, w_kvl, kr_slot, krs_slot, w_mq, moba_partner(w_mq), w_mk, moba_partner(w_mk), w_mv],
                          axis=1).astype(BF16)

    wq3 = w_q_b.reshape(Q_LORA, MLA_HEADS, MLA_NOPE + MLA_ROPE)
    zq = jnp.zeros((Q_LORA, MLA_HEADS, tail), w_q_b.dtype)
    wq = jnp.concatenate([wq3, zq], axis=-1).reshape(Q_LORA, MLA_HEADS * SLOT).astype(BF16)
    wqs = jnp.concatenate([jnp.zeros((Q_LORA, MLA_HEADS, MLA_NOPE), w_q_b.dtype),
                           _rot_partner(wq3[..., MLA_NOPE:], MLA_ROPE // 2), zq],
                          axis=-1).reshape(Q_LORA, MLA_HEADS * SLOT).astype(BF16)
    wkv3 = w_kv_b.reshape(KV_LORA, MLA_HEADS, MLA_NOPE + MLA_V)
    wk = jnp.concatenate([wkv3[..., :MLA_NOPE], jnp.zeros((KV_LORA, MLA_HEADS, SLOT - MLA_NOPE), w_kv_b.dtype)],
                         axis=-1).reshape(KV_LORA, MLA_HEADS * SLOT).astype(BF16)
    wv = wkv3[..., MLA_NOPE:].reshape(KV_LORA, MLA_HEADS * MLA_V).astype(BF16)
    return win, wq, wqs, wk, wv


def _rope_tables(positions):
    pos = positions.astype(F32).reshape(-1, 1)
    n = pos.shape[0]

    def cs(d_rot):
        inv_freq = ROPE_THETA ** (-jnp.arange(0, d_rot, 2, dtype=F32) / d_rot)
        ang = pos * inv_freq
        return jnp.cos(ang), jnp.sin(ang)

    ca, sa = cs(MLA_ROPE)
    tail = SLOT - MLA_NOPE - MLA_ROPE
    ca_t = jnp.concatenate([jnp.ones((n, MLA_NOPE), F32), ca, ca, jnp.zeros((n, tail), F32)], axis=1)
    sa_t = jnp.concatenate([jnp.zeros((n, MLA_NOPE), F32), sa, sa, jnp.zeros((n, tail), F32)], axis=1)
    cb, sb = cs(MOBA_ROT)
    rest = MOBA_HD - MOBA_ROT
    cb_h = jnp.concatenate([cb, cb, jnp.ones((n, rest), F32)], axis=1)
    sb_h = jnp.concatenate([sb, sb, jnp.zeros((n, rest), F32)], axis=1)
    return ca_t, sa_t, jnp.concatenate([cb_h, cb_h], axis=1), jnp.concatenate([sb_h, sb_h], axis=1)


def kernel(x, positions, w_in, q_a_norm, w_q_b, kv_a_norm, w_kv_b, w_o, ln1_g, ln1_b, w_router, b_router,
           w_gate, b_gate, w_up, b_up, w_down, b_down, ln2_g, ln2_b):
    B, T, D = x.shape
    depth = w_in.shape[0]
    alpha = (2.0 * depth) ** 0.25
    N = B * T
    assert T % MOBA_BLOCK == 0 and T // MOBA_BLOCK <= SUBLANES and N % ROW_TILE == 0
    n_asg = N * MOE_TOPK
    n_rows = n_asg + N_EXPERTS * MOE_GROUP
    ca, sa, cb, sb = _rope_tables(positions)
    h = x.reshape(N, D)
    for l in range(depth):
        win, wq, wqs, wk, wv = _layer_weights(w_in[l], w_q_b[l], w_kv_b[l])
        qa, ka, va, mq, mk, mv, selb = _prep(h, win, wq, wqs, wk, wv, q_a_norm[l].reshape(1, -1),
                                             kv_a_norm[l].reshape(1, -1), ca, sa, cb, sb, B, T)
        a = _mla(qa, ka, va, B, T)
        m = _moba(mq, mk, mv, selb, B, T)
        wr = jnp.concatenate([w_router[l], jnp.zeros((D, LANES - N_EXPERTS), F32)], axis=1)
        br = jnp.concatenate([b_router[l], jnp.full((LANES - N_EXPERTS,), -jnp.inf, F32)]).reshape(1, LANES)
        x1, route, gates, cnt = _oproj(a, m, h, w_o[l].astype(BF16), ln1_g[l].reshape(1, D), ln1_b[l].reshape(1, D),
                                       wr, br, alpha)
        counts = cnt[0, :N_EXPERTS].astype(jnp.int32)
        padded = (counts + MOE_GROUP - 1) // MOE_GROUP * MOE_GROUP
        pad_end = jnp.cumsum(padded).astype(jnp.int32)
        pad_start = pad_end - padded
        e_idx = route[:, :MOE_TOPK]
        dest = (pad_start[e_idx] + route[:, MOE_TOPK:2 * MOE_TOPK]).reshape(n_asg)
        n_used = (pad_end[-1:] // MOE_GROUP).astype(jnp.int32)
        tail = jnp.concatenate([jnp.where(padded > 0, pad_end - MOE_GROUP, -1).astype(jnp.int32), n_used])
        blk_start = jnp.arange(n_rows // MOE_GROUP, dtype=jnp.int32) * MOE_GROUP
        blk_expert = jnp.clip(jnp.searchsorted(pad_end, blk_start, side='right'), 0, N_EXPERTS - 1).astype(jnp.int32)
        xs = _dispatch(dest, tail, x1, n_rows)
        y_rows = _experts(blk_expert, n_used, xs, w_gate[l], b_gate[l].reshape(N_EXPERTS, 1, -1),
                          w_up[l], b_up[l].reshape(N_EXPERTS, 1, -1), w_down[l], b_down[l].reshape(N_EXPERTS, 1, -1))
        h = _combine(dest, gates, x1, ln2_g[l].reshape(1, D), ln2_b[l].reshape(1, D), y_rows, alpha)
    return h.reshape(B, T, D)
```

```python
import functools
import math

import jax
import jax.numpy as jnp
from jax import lax
from jax.experimental import pallas as pl
from jax.experimental.pallas import tpu as pltpu

ROPE_THETA = 500000.0
MLA_HEADS = 8
MLA_NOPE = 64
MLA_ROPE = 32
MLA_V = 64
Q_LORA = 256
KV_LORA = 128
MOBA_HEADS = 8
MOBA_HD = 64
MOBA_ROT = MOBA_HD // 4
MOBA_BLOCK = 256
MOBA_TOPK = 3
N_EXPERTS = 32
MOE_TOPK = 4
SWIGLU_LIMIT = 7.0
SWIGLU_ALPHA = 1.702
RMS_EPS = 1e-6
LN_EPS = 1e-5

LANES = 128
SUBLANES = 8
VMEM_LIMIT_BYTES = 56 * 1024 * 1024

SLOT = LANES
TQ = MOBA_BLOCK
ROW_TILE = 256
MOE_GROUP = 256
DISPATCH_TILE = 256
COMBINE_TILE = 128
NEG_BIG = -(2.0 ** 100)

F32 = jnp.float32
BF16 = jnp.bfloat16
NT_DIMS = (((1,), (1,)), ((), ()))


def _dot(a, b, precision=None):
    return jnp.dot(a, b, preferred_element_type=F32, precision=precision)


def _dot_nt(a, b, precision=None):
    return lax.dot_general(a, b, NT_DIMS, preferred_element_type=F32, precision=precision)


def _rms(x, g):
    return x * lax.rsqrt(jnp.mean(x * x, axis=-1, keepdims=True) + RMS_EPS) * g


def _layer_norm(x, g, b):
    mu = jnp.mean(x, axis=-1, keepdims=True)
    xc = x - mu
    var = jnp.mean(xc * xc, axis=-1, keepdims=True)
    return xc * lax.rsqrt(var + LN_EPS) * g + b


_C_QL = 0
_C_KVL = _C_QL + Q_LORA
_C_KR = _C_KVL + KV_LORA
_C_KRS = _C_KR + SLOT
_C_MQ = _C_KRS + SLOT
_MH = MOBA_HEADS * MOBA_HD
_C_MQS = _C_MQ + _MH
_C_MK = _C_MQS + _MH
_C_MKS = _C_MK + _MH
_C_MV = _C_MKS + _MH
_C_END = _C_MV + _MH


def _prep_kernel(x_ref, win_ref, wq_ref, wqs_ref, wk_ref, wv_ref, qg_ref, kvg_ref,
                 ca_ref, sa_ref, cb_ref, sb_ref,
                 qa_ref, ka_ref, va_ref, mq_ref, mk_ref, mv_ref, selb_ref, kmean_scr):
    c = pl.program_id(1)
    xb = x_ref[...].astype(BF16)
    ca = ca_ref[...]
    sa = sa_ref[...]
    cb = cb_ref[...]
    sb = sb_ref[...]

    ql = _dot(xb, win_ref[:, _C_QL:_C_KVL])
    kvl = _dot(xb, win_ref[:, _C_KVL:_C_KR])
    kr = _dot(xb, win_ref[:, _C_KR:_C_KRS])
    krs = _dot(xb, win_ref[:, _C_KRS:_C_MQ])
    qn = _rms(ql, qg_ref[...]).astype(BF16)
    kvn = _rms(kvl, kvg_ref[...]).astype(BF16)
    q = _dot(qn, wq_ref[...])
    qs = _dot(qn, wqs_ref[...])
    kn = _dot(kvn, wk_ref[...])
    v = _dot(kvn, wv_ref[...])
    scale_a = 1.0 / math.sqrt(MLA_NOPE + MLA_ROPE)
    kro = kr * ca + krs * sa
    for h in range(MLA_HEADS):
        sl = slice(h * SLOT, (h + 1) * SLOT)
        qa_ref[:, sl] = ((q[:, sl] * ca + qs[:, sl] * sa) * scale_a).astype(BF16)
        ka_ref[:, sl] = (kn[:, sl] + kro).astype(BF16)
    va_ref[...] = v.astype(BF16)

    mq = _dot(xb, win_ref[:, _C_MQ:_C_MQS])
    mqs = _dot(xb, win_ref[:, _C_MQS:_C_MK])
    mk = _dot(xb, win_ref[:, _C_MK:_C_MKS])
    mks = _dot(xb, win_ref[:, _C_MKS:_C_MV])
    mv = _dot(xb, win_ref[:, _C_MV:_C_END])
    npair = _MH // LANES
    mq_rot = jnp.concatenate(
        [mq[:, j * LANES:(j + 1) * LANES] * cb + mqs[:, j * LANES:(j + 1) * LANES] * sb for j in range(npair)], axis=1)
    mk_rot = jnp.concatenate(
        [mk[:, j * LANES:(j + 1) * LANES] * cb + mks[:, j * LANES:(j + 1) * LANES] * sb for j in range(npair)], axis=1)
    scale_b = 1.0 / math.sqrt(MOBA_HD)
    mq_ref[...] = (mq_rot * scale_b).astype(BF16)
    mk_ref[...] = mk_rot.astype(BF16)
    mv_ref[...] = mv.astype(BF16)

    nrow = MOBA_HEADS * SUBLANES
    @pl.when(c == 0)
    def _():
        kmean_scr[...] = jnp.zeros_like(kmean_scr)

    row_i = lax.broadcasted_iota(jnp.int32, (nrow, _MH), 0)
    lane_i = lax.broadcasted_iota(jnp.int32, (nrow, _MH), 1)
    kmean_c = jnp.mean(mk_rot, axis=0, keepdims=True)
    put = ((row_i % SUBLANES) == c) & ((lane_i // MOBA_HD) == (row_i // SUBLANES))
    table = kmean_scr[...]
    gate_t = _dot_nt(table, mq_rot, precision=lax.Precision.HIGHEST)
    kmean_scr[...] = jnp.where(put, jnp.broadcast_to(kmean_c, (nrow, _MH)), table)

    n_idx = lax.broadcasted_iota(jnp.int32, (SUBLANES, ROW_TILE), 0)
    valid = n_idx < c
    rows = []
    for h in range(MOBA_HEADS):
        g = jnp.where(valid, gate_t[h * SUBLANES:(h + 1) * SUBLANES, :], -jnp.inf)
        rank = jnp.zeros((SUBLANES, ROW_TILE), jnp.int32)
        for k in range(1, SUBLANES):
            other = pltpu.roll(g, k, axis=0)
            other_n = pltpu.roll(n_idx, k, axis=0)
            beats = (other > g) | ((other == g) & (other_n < n_idx))
            rank = rank + beats.astype(jnp.int32)
        sel = valid & (rank < MOBA_TOPK)
        rows.append(jnp.where(sel, 0.0, NEG_BIG))
    rows.append(jnp.zeros((LANES - nrow, ROW_TILE), F32))
    selb_ref[...] = jnp.concatenate(rows, axis=0).T


def _prep(x2, win, wq, wqs, wk, wv, qg, kvg, ca, sa, cb, sb, B, T):
    N, D = x2.shape
    nt = T // ROW_TILE
    row = lambda b, c: (b * nt + c, 0)
    full = lambda b, c: (0, 0)

    def rows(w):
        return pl.BlockSpec((ROW_TILE, w), row)

    def whole(a):
        return pl.BlockSpec(a.shape, full)

    out_shapes = (
        jax.ShapeDtypeStruct((N, MLA_HEADS * SLOT), BF16),
        jax.ShapeDtypeStruct((N, MLA_HEADS * SLOT), BF16),
        jax.ShapeDtypeStruct((N, MLA_HEADS * MLA_V), BF16),
        jax.ShapeDtypeStruct((N, _MH), BF16),
        jax.ShapeDtypeStruct((N, _MH), BF16),
        jax.ShapeDtypeStruct((N, _MH), BF16),
        jax.ShapeDtypeStruct((N, LANES), F32),
    )
    return pl.pallas_call(
        _prep_kernel,
        out_shape=out_shapes,
        grid=(B, nt),
        in_specs=[rows(D), whole(win), whole(wq), whole(wqs), whole(wk), whole(wv), whole(qg), whole(kvg),
                  rows(LANES), rows(LANES), rows(LANES), rows(LANES)],
        out_specs=(rows(MLA_HEADS * SLOT), rows(MLA_HEADS * SLOT), rows(MLA_HEADS * MLA_V),
                   rows(_MH), rows(_MH), rows(_MH), rows(LANES)),
        scratch_shapes=[pltpu.VMEM((MOBA_HEADS * SUBLANES, _MH), F32)],
        compiler_params=pltpu.CompilerParams(dimension_semantics=("arbitrary", "arbitrary"),
                                             vmem_limit_bytes=VMEM_LIMIT_BYTES),
        name="prep",
    )(x2, win, wq, wqs, wk, wv, qg, kvg, ca, sa, cb, sb)


def _flash_first(q, k, v):
    s = _dot_nt(q, k)
    r = lax.broadcasted_iota(jnp.int32, s.shape, 0)
    cidx = lax.broadcasted_iota(jnp.int32, s.shape, 1)
    s = jnp.where(cidx <= r, s, -jnp.inf)
    m = jnp.max(s, axis=-1, keepdims=True)
    p = jnp.exp(s - m)
    l = jnp.sum(p, axis=-1, keepdims=True)
    acc = _dot(p.astype(BF16), v)
    return m, l, acc


def _flash_step(carry, s, v):
    m, l, acc = carry
    m_new = jnp.maximum(m, jnp.max(s, axis=-1, keepdims=True))
    alpha = jnp.exp(m - m_new)
    p = jnp.exp(s - m_new)
    l = alpha * l + jnp.sum(p, axis=-1, keepdims=True)
    acc = alpha * acc + _dot(p.astype(BF16), v)
    return m_new, l, acc


def _mla_kernel(q_ref, k_ref, v_ref, o_ref):
    i = pl.program_id(2)
    d0 = pl.multiple_of(i * TQ, TQ)
    outs = []
    for hh in range(2):
        sl = slice(hh * SLOT, (hh + 1) * SLOT)
        q = q_ref[:, sl]
        carry = _flash_first(q, k_ref[pl.ds(d0, TQ), sl], v_ref[pl.ds(d0, TQ), :])

        def body(j, carry, q=q, sl=sl):
            j0 = pl.multiple_of(j * TQ, TQ)
            s = _dot_nt(q, k_ref[pl.ds(j0, TQ), sl])
            return _flash_step(carry, s, v_ref[pl.ds(j0, TQ), :])

        m, l, acc = lax.fori_loop(0, i, body, carry)
        outs.append(acc / l)
    lane = lax.broadcasted_iota(jnp.int32, outs[0].shape, 1)
    o_ref[...] = jnp.where(lane < MLA_V, outs[0], outs[1]).astype(o_ref.dtype)


def _mla(qa, ka, va, B, T):
    N = qa.shape[0]
    nq = T // TQ
    npair = MLA_HEADS // 2
    return pl.pallas_call(
        _mla_kernel,
        out_shape=jax.ShapeDtypeStruct((N, MLA_HEADS * MLA_V), BF16),
        grid=(B, npair, nq),
        in_specs=[pl.BlockSpec((TQ, 2 * SLOT), lambda b, p, i: (b * nq + i, p)),
                  pl.BlockSpec((T, 2 * SLOT), lambda b, p, i: (b, p)),
                  pl.BlockSpec((T, LANES), lambda b, p, i: (b, p))],
        out_specs=pl.BlockSpec((TQ, LANES), lambda b, p, i: (b * nq + i, p)),
        compiler_params=pltpu.CompilerParams(dimension_semantics=("arbitrary", "arbitrary", "arbitrary"),
                                             vmem_limit_bytes=VMEM_LIMIT_BYTES),
        name="mla_attention",
    )(qa, ka, va)


def _moba_kernel(q_ref, k_ref, v_ref, sb_ref, o_ref):
    p = pl.program_id(1)
    i = pl.program_id(2)
    d0 = pl.multiple_of(i * TQ, TQ)
    q2 = q_ref[...]
    selb = sb_ref[...]
    lane = lax.broadcasted_iota(jnp.int32, q2.shape, 1)
    outs = []
    for hh in range(2):
        in_head = (lane >= hh * MOBA_HD) & (lane < (hh + 1) * MOBA_HD)
        q = jnp.where(in_head, q2, jnp.zeros_like(q2))
        carry = _flash_first(q, k_ref[pl.ds(d0, TQ), :], v_ref[pl.ds(d0, TQ), :])
        head = p * 2 + hh

        def body(n, carry, q=q, head=head):
            n0 = pl.multiple_of(n * TQ, TQ)
            bias = jnp.sum(jnp.where(lane == head * SUBLANES + n, selb, 0.0), axis=-1, keepdims=True)
            s = _dot_nt(q, k_ref[pl.ds(n0, TQ), :]) + bias
            return _flash_step(carry, s, v_ref[pl.ds(n0, TQ), :])

        m, l, acc = lax.fori_loop(0, i, body, carry)
        outs.append(acc / l)
    o_ref[...] = jnp.where(lane < MOBA_HD, outs[0], outs[1]).astype(o_ref.dtype)


def _moba(mq, mk, mv, selb, B, T):
    N = mq.shape[0]
    nq = T // TQ
    npair = MOBA_HEADS // 2
    return pl.pallas_call(
        _moba_kernel,
        out_shape=jax.ShapeDtypeStruct((N, _MH), BF16),
        grid=(B, npair, nq),
        in_specs=[pl.BlockSpec((TQ, LANES), lambda b, p, i: (b * nq + i, p)),
                  pl.BlockSpec((T, LANES), lambda b, p, i: (b, p)),
                  pl.BlockSpec((T, LANES), lambda b, p, i: (b, p)),
                  pl.BlockSpec((TQ, LANES), lambda b, p, i: (b * nq + i, 0))],
        out_specs=pl.BlockSpec((TQ, LANES), lambda b, p, i: (b * nq + i, p)),
        compiler_params=pltpu.CompilerParams(dimension_semantics=("arbitrary", "arbitrary", "arbitrary"),
                                             vmem_limit_bytes=VMEM_LIMIT_BYTES),
        name="moba_attention",
    )(mq, mk, mv, selb)


def _oproj_kernel(alpha, a_ref, m_ref, x_ref, wo_ref, g_ref, b_ref, wr_ref, br_ref,
                  x1_ref, route_ref, gates_ref, cnt_ref, carry_scr):
    i = pl.program_id(0)

    @pl.when(i == 0)
    def _():
        carry_scr[...] = jnp.zeros_like(carry_scr)

    wa = a_ref.shape[1]
    mix = _dot(a_ref[...], wo_ref[:wa, :]) + _dot(m_ref[...], wo_ref[wa:, :])
    x1 = _layer_norm(alpha * x_ref[...] + mix, g_ref[...], b_ref[...])
    x1_ref[...] = x1

    logits = _dot(x1, wr_ref[...], precision=lax.Precision.HIGHEST) + br_ref[...]
    lane = lax.broadcasted_iota(jnp.int32, logits.shape, 1)
    lane_f = lane.astype(F32)
    vals, idxs = [], []
    work = logits
    for _ in range(MOE_TOPK):
        mx = jnp.max(work, axis=-1, keepdims=True)
        ix = jnp.min(jnp.where(work == mx, lane_f, float(LANES)), axis=-1, keepdims=True).astype(jnp.int32)
        vals.append(mx)
        idxs.append(ix)
        work = jnp.where(lane == ix, -jnp.inf, work)
    exps = [jnp.exp(v - vals[0]) for v in vals]
    den = exps[0]
    for e in exps[1:]:
        den = den + e

    onehot = jnp.zeros(logits.shape, F32)
    for ix in idxs:
        onehot = onehot + (lane == ix).astype(F32)
    rows = logits.shape[0]
    r = lax.broadcasted_iota(jnp.int32, (rows, rows), 0)
    cidx = lax.broadcasted_iota(jnp.int32, (rows, rows), 1)
    lower = (cidx < r).astype(BF16)
    carry = carry_scr[0:1, :]
    before = _dot(lower, onehot.astype(BF16)) + carry
    route = jnp.zeros(logits.shape, jnp.int32)
    gates = jnp.zeros(logits.shape, F32)
    for k in range(MOE_TOPK):
        rank = jnp.sum(jnp.where(lane == idxs[k], before, 0.0), axis=-1, keepdims=True).astype(jnp.int32)
        route = jnp.where(lane == k, idxs[k], route)
        route = jnp.where(lane == MOE_TOPK + k, rank, route)
        gates = jnp.where(lane == k, exps[k] / den, gates)
    route_ref[...] = route
    gates_ref[...] = gates
    new_carry = carry + jnp.sum(onehot, axis=0, keepdims=True)
    carry_scr[...] = jnp.broadcast_to(new_carry, carry_scr.shape)
    cnt_ref[...] = jnp.broadcast_to(new_carry, cnt_ref.shape)


def _oproj(a, m, x2, wo, g1, b1, wr, br, alpha):
    N, D = x2.shape
    nt = N // ROW_TILE
    row = lambda i: (i, 0)
    full = lambda i: (0, 0)
    return pl.pallas_call(
        functools.partial(_oproj_kernel, alpha),
        out_shape=(jax.ShapeDtypeStruct((N, D), F32),
                   jax.ShapeDtypeStruct((N, LANES), jnp.int32),
                   jax.ShapeDtypeStruct((N, LANES), F32),
                   jax.ShapeDtypeStruct((SUBLANES, LANES), F32)),
        grid=(nt,),
        in_specs=[pl.BlockSpec((ROW_TILE, a.shape[1]), row), pl.BlockSpec((ROW_TILE, m.shape[1]), row),
                  pl.BlockSpec((ROW_TILE, D), row), pl.BlockSpec(wo.shape, full),
                  pl.BlockSpec(g1.shape, full), pl.BlockSpec(b1.shape, full),
                  pl.BlockSpec(wr.shape, full), pl.BlockSpec(br.shape, full)],
        out_specs=(pl.BlockSpec((ROW_TILE, D), row), pl.BlockSpec((ROW_TILE, LANES), row),
                   pl.BlockSpec((ROW_TILE, LANES), row), pl.BlockSpec((SUBLANES, LANES), full)),
        scratch_shapes=[pltpu.VMEM((SUBLANES, LANES), F32)],
        compiler_params=pltpu.CompilerParams(dimension_semantics=("arbitrary",),
                                             vmem_limit_bytes=VMEM_LIMIT_BYTES),
        name="oproj_router",
    )(a, m, x2, wo, g1, b1, wr, br)


def _dispatch_kernel(dest_ref, tail_ref, x1_ref, xs_ref, zero_scr, sem):
    i = pl.program_id(0)

    @pl.when(i == 0)
    def _():
        zero_scr[...] = jnp.zeros_like(zero_scr)

        def tail_copy(e):
            return pltpu.make_async_copy(zero_scr, xs_ref.at[pl.ds(pl.multiple_of(tail_ref[e], MOE_GROUP), MOE_GROUP)], sem)

        def start(e, _):
            @pl.when(tail_ref[e] >= 0)
            def _():
                tail_copy(e).start()
            return 0

        def wait(e, _):
            @pl.when(tail_ref[e] >= 0)
            def _():
                tail_copy(e).wait()
            return 0

        lax.fori_loop(0, N_EXPERTS, start, 0)
        lax.fori_loop(0, N_EXPERTS, wait, 0)

        def spare_copy(blk):
            return pltpu.make_async_copy(zero_scr, xs_ref.at[pl.ds(pl.multiple_of(blk * MOE_GROUP, MOE_GROUP), MOE_GROUP)], sem)

        def start_spare(blk, _):
            spare_copy(blk).start()
            return 0

        def wait_spare(blk, _):
            spare_copy(blk).wait()
            return 0

        nblk = xs_ref.shape[0] // MOE_GROUP
        lax.fori_loop(tail_ref[N_EXPERTS], nblk, start_spare, 0)
        lax.fori_loop(tail_ref[N_EXPERTS], nblk, wait_spare, 0)

    base = i * DISPATCH_TILE

    def row_copy(t, k):
        d = dest_ref[(base + t) * MOE_TOPK + k]
        return pltpu.make_async_copy(x1_ref.at[pl.ds(t, 1)], xs_ref.at[pl.ds(d, 1)], sem)

    def start_rows(t, _):
        for k in range(MOE_TOPK):
            row_copy(t, k).start()
        return 0

    def wait_rows(t, _):
        for k in range(MOE_TOPK):
            row_copy(t, k).wait()
        return 0

    lax.fori_loop(0, DISPATCH_TILE, start_rows, 0)
    lax.fori_loop(0, DISPATCH_TILE, wait_rows, 0)


def _dispatch(dest, tail, x1, n_rows):
    N, D = x1.shape
    return pl.pallas_call(
        _dispatch_kernel,
        out_shape=jax.ShapeDtypeStruct((n_rows, D), F32),
        grid_spec=pltpu.PrefetchScalarGridSpec(
            num_scalar_prefetch=2,
            grid=(N // DISPATCH_TILE,),
            in_specs=[pl.BlockSpec((DISPATCH_TILE, D), lambda i, d, t: (i, 0))],
            out_specs=pl.BlockSpec(memory_space=pl.ANY),
            scratch_shapes=[pltpu.VMEM((MOE_GROUP, D), F32), pltpu.SemaphoreType.DMA(())],
        ),
        compiler_params=pltpu.CompilerParams(dimension_semantics=("arbitrary",),
                                             vmem_limit_bytes=VMEM_LIMIT_BYTES),
        name="dispatch",
    )(dest, tail, x1)


def _experts_kernel(be_ref, nused_ref, x_ref, wg_ref, bg_ref, wu_ref, bu_ref, wd_ref, bd_ref,
                    y_ref, wg_bf, wu_bf, wd_bf):
    i = pl.program_id(0)
    prev = be_ref[jnp.maximum(i - 1, 0)]
    changed = (i == 0) | (be_ref[i] != prev)
    active = i < nused_ref[0]

    @pl.when(active & changed)
    def _():
        wg_bf[...] = wg_ref[0].astype(BF16)
        wu_bf[...] = wu_ref[0].astype(BF16)
        wd_bf[...] = wd_ref[0].astype(BF16)

    @pl.when(active)
    def _():
        xb = x_ref[...].astype(BF16)
        g = jnp.minimum(_dot(xb, wg_bf[...]) + bg_ref[0], SWIGLU_LIMIT)
        u = jnp.clip(_dot(xb, wu_bf[...]) + bu_ref[0], -SWIGLU_LIMIT, SWIGLU_LIMIT)
        h = g * (1.0 / (1.0 + jnp.exp(-SWIGLU_ALPHA * g))) * (u + 1.0)
        y_ref[...] = _dot(h.astype(BF16), wd_bf[...]) + bd_ref[0]

    @pl.when(jnp.logical_not(active))
    def _():
        y_ref[...] = jnp.zeros_like(y_ref)


def _experts(blk_expert, n_used, xs, wg, bg, wu, bu, wd, bd):
    n_rows, D = xs.shape
    E, _, F = wg.shape
    nblk = n_rows // MOE_GROUP

    def rowmap(i, be, nu):
        return (jnp.minimum(i, nu[0] - 1), 0)

    def wmap(i, be, nu):
        return (be[i], 0, 0)

    return pl.pallas_call(
        _experts_kernel,
        out_shape=jax.ShapeDtypeStruct((n_rows, D), F32),
        grid_spec=pltpu.PrefetchScalarGridSpec(
            num_scalar_prefetch=2,
            grid=(nblk,),
            in_specs=[pl.BlockSpec((MOE_GROUP, D), rowmap),
                      pl.BlockSpec((1, D, F), wmap), pl.BlockSpec((1, 1, F), wmap),
                      pl.BlockSpec((1, D, F), wmap), pl.BlockSpec((1, 1, F), wmap),
                      pl.BlockSpec((1, F, D), wmap), pl.BlockSpec((1, 1, D), wmap)],
            out_specs=pl.BlockSpec((MOE_GROUP, D), lambda i, be, nu: (i, 0)),
            scratch_shapes=[pltpu.VMEM((D, F), BF16), pltpu.VMEM((D, F), BF16), pltpu.VMEM((F, D), BF16)],
        ),
        compiler_params=pltpu.CompilerParams(dimension_semantics=("arbitrary",),
                                             vmem_limit_bytes=VMEM_LIMIT_BYTES),
        name="experts",
    )(blk_expert, n_used, xs, wg, bg, wu, bu, wd, bd)


def _combine_kernel(alpha, dest_ref, gates_ref, x1_ref, g_ref, b_ref, y_ref, o_ref, ybuf, sem):
    i = pl.program_id(0)
    base = i * COMBINE_TILE

    def row_copy(t, k):
        d = dest_ref[(base + t) * MOE_TOPK + k]
        return pltpu.make_async_copy(y_ref.at[pl.ds(d, 1)], ybuf.at[k, pl.ds(t, 1)], sem)

    def start_rows(t, _):
        for k in range(MOE_TOPK):
            row_copy(t, k).start()
        return 0

    def wait_rows(t, _):
        for k in range(MOE_TOPK):
            row_copy(t, k).wait()
        return 0

    lax.fori_loop(0, COMBINE_TILE, start_rows, 0)
    lax.fori_loop(0, COMBINE_TILE, wait_rows, 0)

    gates = gates_ref[...]
    ffn = gates[:, 0:1] * ybuf[0]
    for k in range(1, MOE_TOPK):
        ffn = ffn + gates[:, k:k + 1] * ybuf[k]
    o_ref[...] = _layer_norm(alpha * x1_ref[...] + ffn, g_ref[...], b_ref[...])


def _combine(dest, gates, x1, g2, b2, y_rows, alpha):
    N, D = x1.shape
    row = lambda i, d: (i, 0)
    full = lambda i, d: (0, 0)
    return pl.pallas_call(
        functools.partial(_combine_kernel, alpha),
        out_shape=jax.ShapeDtypeStruct((N, D), F32),
        grid_spec=pltpu.PrefetchScalarGridSpec(
            num_scalar_prefetch=1,
            grid=(N // COMBINE_TILE,),
            in_specs=[pl.BlockSpec((COMBINE_TILE, LANES), row), pl.BlockSpec((COMBINE_TILE, D), row),
                      pl.BlockSpec(g2.shape, full), pl.BlockSpec(b2.shape, full),
                      pl.BlockSpec(memory_space=pl.ANY)],
            out_specs=pl.BlockSpec((COMBINE_TILE, D), row),
            scratch_shapes=[pltpu.VMEM((MOE_TOPK, COMBINE_TILE, D), F32), pltpu.SemaphoreType.DMA(())],
        ),
        compiler_params=pltpu.CompilerParams(dimension_semantics=("arbitrary",),
                                             vmem_limit_bytes=VMEM_LIMIT_BYTES),
        name="combine",
    )(dest, gates, x1, g2, b2, y_rows)


def _rot_partner(w, half):
    return jnp.concatenate([-w[..., half:2 * half], w[..., :half]], axis=-1)


def _layer_weights(w_in, w_q_b, w_kv_b):
    D = w_in.shape[0]
    o1 = Q_LORA
    o2 = o1 + KV_LORA
    o3 = o2 + MLA_ROPE
    w_ql, w_kvl, w_kr = w_in[:, :o1], w_in[:, o1:o2], w_in[:, o2:o3]
    w_mq, w_mk, w_mv = w_in[:, o3:o3 + _MH], w_in[:, o3 + _MH:o3 + 2 * _MH], w_in[:, o3 + 2 * _MH:]
    zpad = lambda n: jnp.zeros((D, n), w_in.dtype)
    tail = SLOT - MLA_NOPE - MLA_ROPE
    kr_slot = jnp.concatenate([zpad(MLA_NOPE), w_kr, zpad(tail)], axis=1)
    krs_slot = jnp.concatenate([zpad(MLA_NOPE), _rot_partner(w_kr, MLA_ROPE // 2), zpad(tail)], axis=1)

    def moba_partner(w):
        w3 = w.reshape(D, MOBA_HEADS, MOBA_HD)
        part = jnp.concatenate([_rot_partner(w3[..., :MOBA_ROT], MOBA_ROT // 2),
                                jnp.zeros((D, MOBA_HEADS, MOBA_HD - MOBA_ROT), w.dtype)], axis=-1)
        return part.reshape(D, _MH)

    win = jnp.concatenate([w_ql, w_kvl, kr_slot, krs_slot, w_mq, moba_partner(w_mq), w_mk, moba_partner(w_mk), w_mv],
                          axis=1).astype(BF16)

    wq3 = w_q_b.reshape(Q_LORA, MLA_HEADS, MLA_NOPE + MLA_ROPE)
    zq = jnp.zeros((Q_LORA, MLA_HEADS, tail), w_q_b.dtype)
    wq = jnp.concatenate([wq3, zq], axis=-1).reshape(Q_LORA, MLA_HEADS * SLOT).astype(BF16)
    wqs = jnp.concatenate([jnp.zeros((Q_LORA, MLA_HEADS, MLA_NOPE), w_q_b.dtype),
                           _rot_partner(wq3[..., MLA_NOPE:], MLA_ROPE // 2), zq],
                          axis=-1).reshape(Q_LORA, MLA_HEADS * SLOT).astype(BF16)
    wkv3 = w_kv_b.reshape(KV_LORA, MLA_HEADS, MLA_NOPE + MLA_V)
    wk = jnp.concatenate([wkv3[..., :MLA_NOPE], jnp.zeros((KV_LORA, MLA_HEADS, SLOT - MLA_NOPE), w_kv_b.dtype)],
                         axis=-1).reshape(KV_LORA, MLA_HEADS * SLOT).astype(BF16)
    wv = wkv3[..., MLA_NOPE:].reshape(KV_LORA, MLA_HEADS * MLA_V).astype(BF16)
    return win, wq, wqs, wk, wv


def _rope_tables(positions):
    pos = positions.astype(F32).reshape(-1, 1)
    n = pos.shape[0]

    def cs(d_rot):
        inv_freq = ROPE_THETA ** (-jnp.arange(0, d_rot, 2, dtype=F32) / d_rot)
        ang = pos * inv_freq
        return jnp.cos(ang), jnp.sin(ang)

    ca, sa = cs(MLA_ROPE)
    tail = SLOT - MLA_NOPE - MLA_ROPE
    ca_t = jnp.concatenate([jnp.ones((n, MLA_NOPE), F32), ca, ca, jnp.zeros((n, tail), F32)], axis=1)
    sa_t = jnp.concatenate([jnp.zeros((n, MLA_NOPE), F32), sa, sa, jnp.zeros((n, tail), F32)], axis=1)
    cb, sb = cs(MOBA_ROT)
    rest = MOBA_HD - MOBA_ROT
    cb_h = jnp.concatenate([cb, cb, jnp.ones((n, rest), F32)], axis=1)
    sb_h = jnp.concatenate([sb, sb, jnp.zeros((n, rest), F32)], axis=1)
    return ca_t, sa_t, jnp.concatenate([cb_h, cb_h], axis=1), jnp.concatenate([sb_h, sb_h], axis=1)


def kernel(x, positions, w_in, q_a_norm, w_q_b, kv_a_norm, w_kv_b, w_o, ln1_g, ln1_b, w_router, b_router,
           w_gate, b_gate, w_up, b_up, w_down, b_down, ln2_g, ln2_b):
    B, T, D = x.shape
    depth = w_in.shape[0]
    alpha = (2.0 * depth) ** 0.25
    N = B * T
    assert T % MOBA_BLOCK == 0 and T // MOBA_BLOCK <= SUBLANES and N % ROW_TILE == 0
    n_asg = N * MOE_TOPK
    n_rows = n_asg + N_EXPERTS * MOE_GROUP
    ca, sa, cb, sb = _rope_tables(positions)
    h = x.reshape(N, D)
    for l in range(depth):
        win, wq, wqs, wk, wv = _layer_weights(w_in[l], w_q_b[l], w_kv_b[l])
        qa, ka, va, mq, mk, mv, selb = _prep(h, win, wq, wqs, wk, wv, q_a_norm[l].reshape(1, -1),
                                             kv_a_norm[l].reshape(1, -1), ca, sa, cb, sb, B, T)
        a = _mla(qa, ka, va, B, T)
        m = _moba(mq, mk, mv, selb, B, T)
        wr = jnp.concatenate([w_router[l], jnp.zeros((D, LANES - N_EXPERTS), F32)], axis=1)
        br = jnp.concatenate([b_router[l], jnp.full((LANES - N_EXPERTS,), -jnp.inf, F32)]).reshape(1, LANES)
        x1, route, gates, cnt = _oproj(a, m, h, w_o[l].astype(BF16), ln1_g[l].reshape(1, D), ln1_b[l].reshape(1, D),
                                       wr, br, alpha)
        counts = cnt[0, :N_EXPERTS].astype(jnp.int32)
        padded = (counts + MOE_GROUP - 1) // MOE_GROUP * MOE_GROUP
        pad_end = jnp.cumsum(padded).astype(jnp.int32)
        pad_start = pad_end - padded
        e_idx = route[:, :MOE_TOPK]
        dest = (pad_start[e_idx] + route[:, MOE_TOPK:2 * MOE_TOPK]).reshape(n_asg)
        n_used = (pad_end[-1:] // MOE_GROUP).astype(jnp.int32)
        tail = jnp.concatenate([jnp.where(padded > 0, pad_end - MOE_GROUP, -1).astype(jnp.int32), n_used])
        blk_start = jnp.arange(n_rows // MOE_GROUP, dtype=jnp.int32) * MOE_GROUP
        blk_expert = jnp.minimum(jnp.sum((blk_start[:, None] >= pad_end[None, :]).astype(jnp.int32), axis=1),
                                 N_EXPERTS - 1)
        xs = _dispatch(dest, tail, x1, n_rows)
        y_rows = _experts(blk_expert, n_used, xs, w_gate[l], b_gate[l].reshape(N_EXPERTS, 1, -1),
                          w_up[l], b_up[l].reshape(N_EXPERTS, 1, -1), w_down[l], b_down[l].reshape(N_EXPERTS, 1, -1))
        h = _combine(dest, gates, x1, ln2_g[l].reshape(1, D), ln2_b[l].reshape(1, D), y_rows, alpha)
    return h.reshape(B, T, D)
```

```python
import functools
import math

import jax
import jax.numpy as jnp
from jax import lax
from jax.experimental import pallas as pl
from jax.experimental.pallas import tpu as pltpu

ROPE_THETA = 500000.0
MLA_HEADS = 8
MLA_NOPE = 64
MLA_ROPE = 32
MLA_V = 64
Q_LORA = 256
KV_LORA = 128
MOBA_HEADS = 8
MOBA_HD = 64
MOBA_ROT = MOBA_HD // 4
MOBA_BLOCK = 256
MOBA_TOPK = 3
N_EXPERTS = 32
MOE_TOPK = 4
SWIGLU_LIMIT = 7.0
SWIGLU_ALPHA = 1.702
RMS_EPS = 1e-6
LN_EPS = 1e-5

LANES = 128
SUBLANES = 8
VMEM_LIMIT_BYTES = 56 * 1024 * 1024

SLOT = LANES
TQ = MOBA_BLOCK
ROW_TILE = 256
MOE_GROUP = 256
DISPATCH_TILE = 256
COMBINE_TILE = 128
HEAD_LANES = 64
ONES_LANE = 64
BIAS_LANE = 64
ATTN_HEADS_PER_STEP = 4
NEG_BIG = -(2.0 ** 100)

F32 = jnp.float32
BF16 = jnp.bfloat16
NT_DIMS = (((1,), (1,)), ((), ()))


def _dot(a, b, precision=None):
    return jnp.dot(a, b, preferred_element_type=F32, precision=precision)


def _dot_nt(a, b, precision=None):
    return lax.dot_general(a, b, NT_DIMS, preferred_element_type=F32, precision=precision)


def _rms(x, g):
    return x * lax.rsqrt(jnp.mean(x * x, axis=-1, keepdims=True) + RMS_EPS) * g


def _layer_norm(x, g, b):
    mu = jnp.mean(x, axis=-1, keepdims=True)
    xc = x - mu
    var = jnp.mean(xc * xc, axis=-1, keepdims=True)
    return xc * lax.rsqrt(var + LN_EPS) * g + b


_C_QL = 0
_C_KVL = _C_QL + Q_LORA
_C_KR = _C_KVL + KV_LORA
_C_KRS = _C_KR + SLOT
_C_MQ = _C_KRS + SLOT
_MH = MOBA_HEADS * MOBA_HD
_C_MQS = _C_MQ + _MH
_C_MK = _C_MQS + _MH
_C_MKS = _C_MK + _MH
_C_MV = _C_MKS + _MH
_C_END = _C_MV + _MH


def _prep_kernel(x_ref, win_ref, wq_ref, wqs_ref, wk_ref, wv_ref, qg_ref, kvg_ref,
                 ca_ref, sa_ref, cb_ref, sb_ref,
                 qa_ref, ka_ref, va_ref, mq_ref, mk_ref, mv_ref, kmean_scr):
    c = pl.program_id(1)
    xb = x_ref[...].astype(BF16)
    ca = ca_ref[...]
    sa = sa_ref[...]
    cb = cb_ref[...]
    sb = sb_ref[...]
    lane = lax.broadcasted_iota(jnp.int32, (ROW_TILE, SLOT), 1)
    head_lanes = lane < HEAD_LANES
    ones_col = (lane == ONES_LANE).astype(F32)

    ql = _dot(xb, win_ref[:, _C_QL:_C_KVL])
    kvl = _dot(xb, win_ref[:, _C_KVL:_C_KR])
    kr = _dot(xb, win_ref[:, _C_KR:_C_KRS])
    krs = _dot(xb, win_ref[:, _C_KRS:_C_MQ])
    qn = _rms(ql, qg_ref[...]).astype(BF16)
    kvn = _rms(kvl, kvg_ref[...]).astype(BF16)
    q = _dot(qn, wq_ref[...])
    qs = _dot(qn, wqs_ref[...])
    kn = _dot(kvn, wk_ref[...])
    v = _dot(kvn, wv_ref[...])
    scale_a = 1.0 / math.sqrt(MLA_NOPE + MLA_ROPE)
    kro = kr * ca + krs * sa
    for h in range(MLA_HEADS):
        sl = slice(h * SLOT, (h + 1) * SLOT)
        qa_ref[:, sl] = ((q[:, sl] * ca + qs[:, sl] * sa) * scale_a).astype(BF16)
        ka_ref[:, sl] = (kn[:, sl] + kro).astype(BF16)
        va_ref[:, sl] = (v[:, sl] + ones_col).astype(BF16)

    mq = _dot(xb, win_ref[:, _C_MQ:_C_MQS])
    mqs = _dot(xb, win_ref[:, _C_MQS:_C_MK])
    mk = _dot(xb, win_ref[:, _C_MK:_C_MKS])
    mks = _dot(xb, win_ref[:, _C_MKS:_C_MV])
    mv = _dot(xb, win_ref[:, _C_MV:_C_END])
    npair = _MH // LANES
    pair = lambda a, j: a[:, j * LANES:(j + 1) * LANES]
    mq_rot = [pair(mq, j) * cb + pair(mqs, j) * sb for j in range(npair)]
    mk_rot = [pair(mk, j) * cb + pair(mks, j) * sb for j in range(npair)]
    mq_all = jnp.concatenate(mq_rot, axis=1)
    mk_all = jnp.concatenate(mk_rot, axis=1)

    nrow = MOBA_HEADS * SUBLANES
    @pl.when(c == 0)
    def _():
        kmean_scr[...] = jnp.zeros_like(kmean_scr)

    row_i = lax.broadcasted_iota(jnp.int32, (nrow, _MH), 0)
    lane_i = lax.broadcasted_iota(jnp.int32, (nrow, _MH), 1)
    kmean_c = jnp.mean(mk_all, axis=0, keepdims=True)
    put = ((row_i % SUBLANES) == c) & ((lane_i // MOBA_HD) == (row_i // SUBLANES))
    table = kmean_scr[...]
    gate_t = _dot_nt(table, mq_all, precision=lax.Precision.HIGHEST)
    kmean_scr[...] = jnp.where(put, jnp.broadcast_to(kmean_c, (nrow, _MH)), table)

    n_idx = lax.broadcasted_iota(jnp.int32, (SUBLANES, ROW_TILE), 0)
    valid = n_idx < c
    rows = []
    for h in range(MOBA_HEADS):
        g = jnp.where(valid, gate_t[h * SUBLANES:(h + 1) * SUBLANES, :], -jnp.inf)
        rank = jnp.zeros((SUBLANES, ROW_TILE), jnp.int32)
        for k in range(1, SUBLANES):
            other = pltpu.roll(g, k, axis=0)
            other_n = pltpu.roll(n_idx, k, axis=0)
            beats = (other > g) | ((other == g) & (other_n < n_idx))
            rank = rank + beats.astype(jnp.int32)
        keep = (valid & (rank < MOBA_TOPK)) | (n_idx == c)
        rows.append(jnp.where(keep, 0.0, NEG_BIG))
    rows.append(jnp.zeros((LANES - nrow, ROW_TILE), F32))
    selb = jnp.concatenate(rows, axis=0).T

    scale_b = 1.0 / math.sqrt(MOBA_HD)
    bias_lanes = (lane >= BIAS_LANE) & (lane < BIAS_LANE + SUBLANES)
    block_onehot = (lane == BIAS_LANE + c).astype(F32)
    for h in range(MOBA_HEADS):
        j, hh = divmod(h, 2)
        sl = slice(h * SLOT, (h + 1) * SLOT)
        to_slot = (lambda a: a) if hh == 0 else (lambda a: pltpu.roll(a, HEAD_LANES, axis=1))
        bias = pltpu.roll(selb, (BIAS_LANE - SUBLANES * h) % LANES, axis=1)
        mq_ref[:, sl] = jnp.where(head_lanes, to_slot(mq_rot[j]) * scale_b, jnp.where(bias_lanes, bias, 0.0)).astype(BF16)
        mk_ref[:, sl] = jnp.where(head_lanes, to_slot(mk_rot[j]), block_onehot).astype(BF16)
        mv_ref[:, sl] = jnp.where(head_lanes, to_slot(pair(mv, j)), ones_col).astype(BF16)


def _prep(x2, win, wq, wqs, wk, wv, qg, kvg, ca, sa, cb, sb, B, T):
    N, D = x2.shape
    nt = T // ROW_TILE
    row = lambda b, c: (b * nt + c, 0)
    full = lambda b, c: (0, 0)

    def rows(w):
        return pl.BlockSpec((ROW_TILE, w), row)

    def whole(a):
        return pl.BlockSpec(a.shape, full)

    slots = jax.ShapeDtypeStruct((N, MLA_HEADS * SLOT), BF16)
    return pl.pallas_call(
        _prep_kernel,
        out_shape=(slots,) * 6,
        grid=(B, nt),
        in_specs=[rows(D), whole(win), whole(wq), whole(wqs), whole(wk), whole(wv), whole(qg), whole(kvg),
                  rows(LANES), rows(LANES), rows(LANES), rows(LANES)],
        out_specs=(rows(MLA_HEADS * SLOT),) * 6,
        scratch_shapes=[pltpu.VMEM((MOBA_HEADS * SUBLANES, _MH), F32)],
        compiler_params=pltpu.CompilerParams(dimension_semantics=("arbitrary", "arbitrary"),
                                             vmem_limit_bytes=VMEM_LIMIT_BYTES),
        name="prep",
    )(x2, win, wq, wqs, wk, wv, qg, kvg, ca, sa, cb, sb)


def _attn_kernel(q_ref, k_ref, v_ref, o_ref, s_scr):
    i = pl.program_id(2)
    nh = ATTN_HEADS_PER_STEP
    slot = lambda h: slice(h * SLOT, (h + 1) * SLOT)
    qs = [q_ref[:, slot(h)] for h in range(nh)]

    def scores(h, j0):
        return _dot_nt(qs[h], k_ref[pl.ds(j0, TQ), slot(h)])

    def fold(s):
        out = s[:, :LANES]
        for t in range(1, TQ // LANES):
            out = jnp.maximum(out, s[:, t * LANES:(t + 1) * LANES])
        return out

    def pass1(j, mrun):
        j0 = pl.multiple_of(j * TQ, TQ)
        out = []
        for h in range(nh):
            s = scores(h, j0)
            s_scr[h, j] = s
            out.append(jnp.maximum(mrun[h], fold(s)))
        return tuple(out)

    mrun = lax.fori_loop(0, i, pass1, tuple(jnp.full((TQ, LANES), -jnp.inf, F32) for _ in range(nh)))
    d0 = pl.multiple_of(i * TQ, TQ)
    r = lax.broadcasted_iota(jnp.int32, (TQ, TQ), 0)
    cidx = lax.broadcasted_iota(jnp.int32, (TQ, TQ), 1)
    ms = []
    for h in range(nh):
        s = jnp.where(cidx <= r, scores(h, d0), -jnp.inf)
        s_scr[h, i] = s
        ms.append(jnp.max(jnp.maximum(mrun[h], fold(s)), axis=-1, keepdims=True))

    def pass2(j, acc):
        j0 = pl.multiple_of(j * TQ, TQ)
        out = []
        for h in range(nh):
            p = jnp.exp(s_scr[h, j] - ms[h]).astype(BF16)
            out.append(acc[h] + _dot(p, v_ref[pl.ds(j0, TQ), slot(h)]))
        return tuple(out)

    acc = lax.fori_loop(0, i + 1, pass2, tuple(jnp.zeros((TQ, SLOT), F32) for _ in range(nh)))
    lane = lax.broadcasted_iota(jnp.int32, (TQ, SLOT), 1)
    outs = [a / a[:, ONES_LANE:ONES_LANE + 1] for a in acc]
    for jj in range(nh // 2):
        both = jnp.where(lane < HEAD_LANES, outs[2 * jj], pltpu.roll(outs[2 * jj + 1], HEAD_LANES, axis=1))
        o_ref[:, jj * LANES:(jj + 1) * LANES] = both.astype(o_ref.dtype)


def _attention(q, k, v, B, T, name):
    N = q.shape[0]
    nq = T // TQ
    nh = ATTN_HEADS_PER_STEP
    heads = q.shape[1] // SLOT
    return pl.pallas_call(
        _attn_kernel,
        out_shape=jax.ShapeDtypeStruct((N, heads * HEAD_LANES), BF16),
        grid=(B, heads // nh, nq),
        in_specs=[pl.BlockSpec((TQ, nh * SLOT), lambda b, g, i: (b * nq + i, g)),
                  pl.BlockSpec((T, nh * SLOT), lambda b, g, i: (b, g)),
                  pl.BlockSpec((T, nh * SLOT), lambda b, g, i: (b, g))],
        out_specs=pl.BlockSpec((TQ, nh * HEAD_LANES), lambda b, g, i: (b * nq + i, g)),
        scratch_shapes=[pltpu.VMEM((nh, nq, TQ, TQ), F32)],
        compiler_params=pltpu.CompilerParams(dimension_semantics=("arbitrary", "arbitrary", "arbitrary"),
                                             vmem_limit_bytes=VMEM_LIMIT_BYTES),
        name=name,
    )(q, k, v)


def _oproj_kernel(alpha, a_ref, m_ref, x_ref, wo_ref, g_ref, b_ref, wr_ref, br_ref,
                  x1_ref, route_ref, gates_ref, cnt_ref, carry_scr):
    i = pl.program_id(0)

    @pl.when(i == 0)
    def _():
        carry_scr[...] = jnp.zeros_like(carry_scr)

    wa = a_ref.shape[1]
    mix = _dot(a_ref[...], wo_ref[:wa, :]) + _dot(m_ref[...], wo_ref[wa:, :])
    x1 = _layer_norm(alpha * x_ref[...] + mix, g_ref[...], b_ref[...])
    x1_ref[...] = x1

    logits = _dot(x1, wr_ref[...], precision=lax.Precision.HIGHEST) + br_ref[...]
    lane = lax.broadcasted_iota(jnp.int32, logits.shape, 1)
    lane_f = lane.astype(F32)
    vals, idxs = [], []
    work = logits
    for _ in range(MOE_TOPK):
        mx = jnp.max(work, axis=-1, keepdims=True)
        ix = jnp.min(jnp.where(work == mx, lane_f, float(LANES)), axis=-1, keepdims=True).astype(jnp.int32)
        vals.append(mx)
        idxs.append(ix)
        work = jnp.where(lane == ix, -jnp.inf, work)
    exps = [jnp.exp(v - vals[0]) for v in vals]
    den = exps[0]
    for e in exps[1:]:
        den = den + e

    onehot = jnp.zeros(logits.shape, F32)
    for ix in idxs:
        onehot = onehot + (lane == ix).astype(F32)
    rows = logits.shape[0]
    r = lax.broadcasted_iota(jnp.int32, (rows, rows), 0)
    cidx = lax.broadcasted_iota(jnp.int32, (rows, rows), 1)
    lower = (cidx < r).astype(BF16)
    carry = carry_scr[0:1, :]
    before = _dot(lower, onehot.astype(BF16)) + carry
    route = jnp.zeros(logits.shape, jnp.int32)
    gates = jnp.zeros(logits.shape, F32)
    for k in range(MOE_TOPK):
        rank = jnp.sum(jnp.where(lane == idxs[k], before, 0.0), axis=-1, keepdims=True).astype(jnp.int32)
        route = jnp.where(lane == k, idxs[k], route)
        route = jnp.where(lane == MOE_TOPK + k, rank, route)
        gates = jnp.where(lane == k, exps[k] / den, gates)
    route_ref[...] = route
    gates_ref[...] = gates
    new_carry = carry + jnp.sum(onehot, axis=0, keepdims=True)
    carry_scr[...] = jnp.broadcast_to(new_carry, carry_scr.shape)
    cnt_ref[...] = jnp.broadcast_to(new_carry, cnt_ref.shape)


def _oproj(a, m, x2, wo, g1, b1, wr, br, alpha):
    N, D = x2.shape
    nt = N // ROW_TILE
    row = lambda i: (i, 0)
    full = lambda i: (0, 0)
    return pl.pallas_call(
        functools.partial(_oproj_kernel, alpha),
        out_shape=(jax.ShapeDtypeStruct((N, D), F32),
                   jax.ShapeDtypeStruct((N, LANES), jnp.int32),
                   jax.ShapeDtypeStruct((N, LANES), F32),
                   jax.ShapeDtypeStruct((SUBLANES, LANES), F32)),
        grid=(nt,),
        in_specs=[pl.BlockSpec((ROW_TILE, a.shape[1]), row), pl.BlockSpec((ROW_TILE, m.shape[1]), row),
                  pl.BlockSpec((ROW_TILE, D), row), pl.BlockSpec(wo.shape, full),
                  pl.BlockSpec(g1.shape, full), pl.BlockSpec(b1.shape, full),
                  pl.BlockSpec(wr.shape, full), pl.BlockSpec(br.shape, full)],
        out_specs=(pl.BlockSpec((ROW_TILE, D), row), pl.BlockSpec((ROW_TILE, LANES), row),
                   pl.BlockSpec((ROW_TILE, LANES), row), pl.BlockSpec((SUBLANES, LANES), full)),
        scratch_shapes=[pltpu.VMEM((SUBLANES, LANES), F32)],
        compiler_params=pltpu.CompilerParams(dimension_semantics=("arbitrary",),
                                             vmem_limit_bytes=VMEM_LIMIT_BYTES),
        name="oproj_router",
    )(a, m, x2, wo, g1, b1, wr, br)


def _dispatch_kernel(dest_ref, tail_ref, x1_ref, xs_ref, zero_scr, sem):
    i = pl.program_id(0)

    @pl.when(i == 0)
    def _():
        zero_scr[...] = jnp.zeros_like(zero_scr)

        def tail_copy(e):
            return pltpu.make_async_copy(zero_scr, xs_ref.at[pl.ds(pl.multiple_of(tail_ref[e], MOE_GROUP), MOE_GROUP)], sem)

        def start(e, _):
            @pl.when(tail_ref[e] >= 0)
            def _():
                tail_copy(e).start()
            return 0

        def wait(e, _):
            @pl.when(tail_ref[e] >= 0)
            def _():
                tail_copy(e).wait()
            return 0

        lax.fori_loop(0, N_EXPERTS, start, 0)
        lax.fori_loop(0, N_EXPERTS, wait, 0)

        def spare_copy(blk):
            return pltpu.make_async_copy(zero_scr, xs_ref.at[pl.ds(pl.multiple_of(blk * MOE_GROUP, MOE_GROUP), MOE_GROUP)], sem)

        def start_spare(blk, _):
            spare_copy(blk).start()
            return 0

        def wait_spare(blk, _):
            spare_copy(blk).wait()
            return 0

        nblk = xs_ref.shape[0] // MOE_GROUP
        lax.fori_loop(tail_ref[N_EXPERTS], nblk, start_spare, 0)
        lax.fori_loop(tail_ref[N_EXPERTS], nblk, wait_spare, 0)

    base = i * DISPATCH_TILE

    def row_copy(t, k):
        d = dest_ref[(base + t) * MOE_TOPK + k]
        return pltpu.make_async_copy(x1_ref.at[pl.ds(t, 1)], xs_ref.at[pl.ds(d, 1)], sem)

    def start_rows(t, _):
        for k in range(MOE_TOPK):
            row_copy(t, k).start()
        return 0

    def wait_rows(t, _):
        for k in range(MOE_TOPK):
            row_copy(t, k).wait()
        return 0

    lax.fori_loop(0, DISPATCH_TILE, start_rows, 0)
    lax.fori_loop(0, DISPATCH_TILE, wait_rows, 0)


def _dispatch(dest, tail, x1, n_rows):
    N, D = x1.shape
    return pl.pallas_call(
        _dispatch_kernel,
        out_shape=jax.ShapeDtypeStruct((n_rows, D), F32),
        grid_spec=pltpu.PrefetchScalarGridSpec(
            num_scalar_prefetch=2,
            grid=(N // DISPATCH_TILE,),
            in_specs=[pl.BlockSpec((DISPATCH_TILE, D), lambda i, d, t: (i, 0))],
            out_specs=pl.BlockSpec(memory_space=pl.ANY),
            scratch_shapes=[pltpu.VMEM((MOE_GROUP, D), F32), pltpu.SemaphoreType.DMA(())],
        ),
        compiler_params=pltpu.CompilerParams(dimension_semantics=("arbitrary",),
                                             vmem_limit_bytes=VMEM_LIMIT_BYTES),
        name="dispatch",
    )(dest, tail, x1)


def _experts_kernel(be_ref, nused_ref, x_ref, wg_ref, bg_ref, wu_ref, bu_ref, wd_ref, bd_ref,
                    y_ref, wg_bf, wu_bf, wd_bf):
    i = pl.program_id(0)
    prev = be_ref[jnp.maximum(i - 1, 0)]
    changed = (i == 0) | (be_ref[i] != prev)
    active = i < nused_ref[0]

    @pl.when(active & changed)
    def _():
        wg_bf[...] = wg_ref[0].astype(BF16)
        wu_bf[...] = wu_ref[0].astype(BF16)
        wd_bf[...] = wd_ref[0].astype(BF16)

    @pl.when(active)
    def _():
        xb = x_ref[...].astype(BF16)
        g = jnp.minimum(_dot(xb, wg_bf[...]) + bg_ref[0], SWIGLU_LIMIT)
        u = jnp.clip(_dot(xb, wu_bf[...]) + bu_ref[0], -SWIGLU_LIMIT, SWIGLU_LIMIT)
        h = g * (1.0 / (1.0 + jnp.exp(-SWIGLU_ALPHA * g))) * (u + 1.0)
        y_ref[...] = _dot(h.astype(BF16), wd_bf[...]) + bd_ref[0]

    @pl.when(jnp.logical_not(active))
    def _():
        y_ref[...] = jnp.zeros_like(y_ref)


def _experts(blk_expert, n_used, xs, wg, bg, wu, bu, wd, bd):
    n_rows, D = xs.shape
    E, _, F = wg.shape
    nblk = n_rows // MOE_GROUP

    def rowmap(i, be, nu):
        return (jnp.minimum(i, nu[0] - 1), 0)

    def wmap(i, be, nu):
        return (be[i], 0, 0)

    return pl.pallas_call(
        _experts_kernel,
        out_shape=jax.ShapeDtypeStruct((n_rows, D), F32),
        grid_spec=pltpu.PrefetchScalarGridSpec(
            num_scalar_prefetch=2,
            grid=(nblk,),
            in_specs=[pl.BlockSpec((MOE_GROUP, D), rowmap),
                      pl.BlockSpec((1, D, F), wmap), pl.BlockSpec((1, 1, F), wmap),
                      pl.BlockSpec((1, D, F), wmap), pl.BlockSpec((1, 1, F), wmap),
                      pl.BlockSpec((1, F, D), wmap), pl.BlockSpec((1, 1, D), wmap)],
            out_specs=pl.BlockSpec((MOE_GROUP, D), lambda i, be, nu: (i, 0)),
            scratch_shapes=[pltpu.VMEM((D, F), BF16), pltpu.VMEM((D, F), BF16), pltpu.VMEM((F, D), BF16)],
        ),
        compiler_params=pltpu.CompilerParams(dimension_semantics=("arbitrary",),
                                             vmem_limit_bytes=VMEM_LIMIT_BYTES),
        name="experts",
    )(blk_expert, n_used, xs, wg, bg, wu, bu, wd, bd)


def _combine_kernel(alpha, dest_ref, gates_ref, x1_ref, g_ref, b_ref, y_ref, o_ref, ybuf, sem):
    i = pl.program_id(0)
    base = i * COMBINE_TILE

    def row_copy(t, k):
        d = dest_ref[(base + t) * MOE_TOPK + k]
        return pltpu.make_async_copy(y_ref.at[pl.ds(d, 1)], ybuf.at[k, pl.ds(t, 1)], sem)

    def start_rows(t, _):
        for k in range(MOE_TOPK):
            row_copy(t, k).start()
        return 0

    def wait_rows(t, _):
        for k in range(MOE_TOPK):
            row_copy(t, k).wait()
        return 0

    lax.fori_loop(0, COMBINE_TILE, start_rows, 0)
    lax.fori_loop(0, COMBINE_TILE, wait_rows, 0)

    gates = gates_ref[...]
    ffn = gates[:, 0:1] * ybuf[0]
    for k in range(1, MOE_TOPK):
        ffn = ffn + gates[:, k:k + 1] * ybuf[k]
    o_ref[...] = _layer_norm(alpha * x1_ref[...] + ffn, g_ref[...], b_ref[...])


def _combine(dest, gates, x1, g2, b2, y_rows, alpha):
    N, D = x1.shape
    row = lambda i, d: (i, 0)
    full = lambda i, d: (0, 0)
    return pl.pallas_call(
        functools.partial(_combine_kernel, alpha),
        out_shape=jax.ShapeDtypeStruct((N, D), F32),
        grid_spec=pltpu.PrefetchScalarGridSpec(
            num_scalar_prefetch=1,
            grid=(N // COMBINE_TILE,),
            in_specs=[pl.BlockSpec((COMBINE_TILE, LANES), row), pl.BlockSpec((COMBINE_TILE, D), row),
                      pl.BlockSpec(g2.shape, full), pl.BlockSpec(b2.shape, full),
                      pl.BlockSpec(memory_space=pl.ANY)],
            out_specs=pl.BlockSpec((COMBINE_TILE, D), row),
            scratch_shapes=[pltpu.VMEM((MOE_TOPK, COMBINE_TILE, D), F32), pltpu.SemaphoreType.DMA(())],
        ),
        compiler_params=pltpu.CompilerParams(dimension_semantics=("arbitrary",),
                                             vmem_limit_bytes=VMEM_LIMIT_BYTES),
        name="combine",
    )(dest, gates, x1, g2, b2, y_rows)


def _rot_partner(w, half):
    return jnp.concatenate([-w[..., half:2 * half], w[..., :half]], axis=-1)


def _layer_weights(w_in, w_q_b, w_kv_b):
    D = w_in.shape[0]
    o1 = Q_LORA
    o2 = o1 + KV_LORA
    o3 = o2 + MLA_ROPE
    w_ql, w_kvl, w_kr = w_in[:, :o1], w_in[:, o1:o2], w_in[:, o2:o3]
    w_mq, w_mk, w_mv = w_in[:, o3:o3 + _MH], w_in[:, o3 + _MH:o3 + 2 * _MH], w_in[:, o3 + 2 * _MH:]
    zpad = lambda n: jnp.zeros((D, n), w_in.dtype)
    tail = SLOT - MLA_NOPE - MLA_ROPE
    kr_slot = jnp.concatenate([zpad(MLA_NOPE), w_kr, zpad(tail)], axis=1)
    krs_slot = jnp.concatenate([zpad(MLA_NOPE), _rot_partner(w_kr, MLA_ROPE // 2), zpad(tail)], axis=1)

    def moba_partner(w):
        w3 = w.reshape(D, MOBA_HEADS, MOBA_HD)
        part = jnp.concatenate([_rot_partner(w3[..., :MOBA_ROT], MOBA_ROT // 2),
                                jnp.zeros((D, MOBA_HEADS, MOBA_HD - MOBA_ROT), w.dtype)], axis=-1)
        return part.reshape(D, _MH)

    win = jnp.concatenate([w_ql, w_kvl, kr_slot, krs_slot, w_mq, moba_partner(w_mq), w_mk, moba_partner(w_mk), w_mv],
                          axis=1).astype(BF16)

    wq3 = w_q_b.reshape(Q_LORA, MLA_HEADS, MLA_NOPE + MLA_ROPE)
    zq = jnp.zeros((Q_LORA, MLA_HEADS, tail), w_q_b.dtype)
    wq = jnp.concatenate([wq3, zq], axis=-1).reshape(Q_LORA, MLA_HEADS * SLOT).astype(BF16)
    wqs = jnp.concatenate([jnp.zeros((Q_LORA, MLA_HEADS, MLA_NOPE), w_q_b.dtype),
                           _rot_partner(wq3[..., MLA_NOPE:], MLA_ROPE // 2), zq],
                          axis=-1).reshape(Q_LORA, MLA_HEADS * SLOT).astype(BF16)
    wkv3 = w_kv_b.reshape(KV_LORA, MLA_HEADS, MLA_NOPE + MLA_V)
    wk = jnp.concatenate([wkv3[..., :MLA_NOPE], jnp.zeros((KV_LORA, MLA_HEADS, SLOT - MLA_NOPE), w_kv_b.dtype)],
                         axis=-1).reshape(KV_LORA, MLA_HEADS * SLOT).astype(BF16)
    wv = jnp.concatenate([wkv3[..., MLA_NOPE:], jnp.zeros((KV_LORA, MLA_HEADS, SLOT - MLA_V), w_kv_b.dtype)],
                         axis=-1).reshape(KV_LORA, MLA_HEADS * SLOT).astype(BF16)
    return win, wq, wqs, wk, wv


def _rope_tables(positions):
    pos = positions.astype(F32).reshape(-1, 1)
    n = pos.shape[0]

    def cs(d_rot):
        inv_freq = ROPE_THETA ** (-jnp.arange(0, d_rot, 2, dtype=F32) / d_rot)
        ang = pos * inv_freq
        return jnp.cos(ang), jnp.sin(ang)

    ca, sa = cs(MLA_ROPE)
    tail = SLOT - MLA_NOPE - MLA_ROPE
    ca_t = jnp.concatenate([jnp.ones((n, MLA_NOPE), F32), ca, ca, jnp.zeros((n, tail), F32)], axis=1)
    sa_t = jnp.concatenate([jnp.zeros((n, MLA_NOPE), F32), sa, sa, jnp.zeros((n, tail), F32)], axis=1)
    cb, sb = cs(MOBA_ROT)
    rest = MOBA_HD - MOBA_ROT
    cb_h = jnp.concatenate([cb, cb, jnp.ones((n, rest), F32)], axis=1)
    sb_h = jnp.concatenate([sb, sb, jnp.zeros((n, rest), F32)], axis=1)
    return ca_t, sa_t, jnp.concatenate([cb_h, cb_h], axis=1), jnp.concatenate([sb_h, sb_h], axis=1)


def kernel(x, positions, w_in, q_a_norm, w_q_b, kv_a_norm, w_kv_b, w_o, ln1_g, ln1_b, w_router, b_router,
           w_gate, b_gate, w_up, b_up, w_down, b_down, ln2_g, ln2_b):
    B, T, D = x.shape
    depth = w_in.shape[0]
    alpha = (2.0 * depth) ** 0.25
    N = B * T
    assert T % MOBA_BLOCK == 0 and T // MOBA_BLOCK <= SUBLANES and N % ROW_TILE == 0
    n_asg = N * MOE_TOPK
    n_rows = n_asg + N_EXPERTS * MOE_GROUP
    ca, sa, cb, sb = _rope_tables(positions)
    h = x.reshape(N, D)
    for l in range(depth):
        win, wq, wqs, wk, wv = _layer_weights(w_in[l], w_q_b[l], w_kv_b[l])
        qa, ka, va, mq, mk, mv = _prep(h, win, wq, wqs, wk, wv, q_a_norm[l].reshape(1, -1),
                                       kv_a_norm[l].reshape(1, -1), ca, sa, cb, sb, B, T)
        a = _attention(qa, ka, va, B, T, "mla_attention")
        m = _attention(mq, mk, mv, B, T, "moba_attention")
        wr = jnp.concatenate([w_router[l], jnp.zeros((D, LANES - N_EXPERTS), F32)], axis=1)
        br = jnp.concatenate([b_router[l], jnp.full((LANES - N_EXPERTS,), -jnp.inf, F32)]).reshape(1, LANES)
        x1, route, gates, cnt = _oproj(a, m, h, w_o[l].astype(BF16), ln1_g[l].reshape(1, D), ln1_b[l].reshape(1, D),
                                       wr, br, alpha)
        counts = cnt[0, :N_EXPERTS].astype(jnp.int32)
        padded = (counts + MOE_GROUP - 1) // MOE_GROUP * MOE_GROUP
        pad_end = jnp.cumsum(padded).astype(jnp.int32)
        pad_start = pad_end - padded
        e_idx = route[:, :MOE_TOPK]
        dest = (pad_start[e_idx] + route[:, MOE_TOPK:2 * MOE_TOPK]).reshape(n_asg)
        n_used = (pad_end[-1:] // MOE_GROUP).astype(jnp.int32)
        tail = jnp.concatenate([jnp.where(padded > 0, pad_end - MOE_GROUP, -1).astype(jnp.int32), n_used])
        blk_start = jnp.arange(n_rows // MOE_GROUP, dtype=jnp.int32) * MOE_GROUP
        blk_expert = jnp.minimum(jnp.sum((blk_start[:, None] >= pad_end[None, :]).astype(jnp.int32), axis=1),
                                 N_EXPERTS - 1)
        xs = _dispatch(dest, tail, x1, n_rows)
        y_rows = _experts(blk_expert, n_used, xs, w_gate[l], b_gate[l].reshape(N_EXPERTS, 1, -1),
                          w_up[l], b_up[l].reshape(N_EXPERTS, 1, -1), w_down[l], b_down[l].reshape(N_EXPERTS, 1, -1))
        h = _combine(dest, gates, x1, ln2_g[l].reshape(1, D), ln2_b[l].reshape(1, D), y_rows, alpha)
    return h.reshape(B, T, D)
```

```python
import functools
import math

import jax
import jax.numpy as jnp
from jax import lax
from jax.experimental import pallas as pl
from jax.experimental.pallas import tpu as pltpu

ROPE_THETA = 500000.0
MLA_HEADS = 8
MLA_NOPE = 64
MLA_ROPE = 32
MLA_V = 64
Q_LORA = 256
KV_LORA = 128
MOBA_HEADS = 8
MOBA_HD = 64
MOBA_ROT = MOBA_HD // 4
MOBA_BLOCK = 256
MOBA_TOPK = 3
N_EXPERTS = 32
MOE_TOPK = 4
SWIGLU_LIMIT = 7.0
SWIGLU_ALPHA = 1.702
RMS_EPS = 1e-6
LN_EPS = 1e-5

LANES = 128
SUBLANES = 8
VMEM_LIMIT_BYTES = 56 * 1024 * 1024

SLOT = LANES
TQ = MOBA_BLOCK
ROW_TILE = 256
MOE_GROUP = 256
DISPATCH_TILE = 256
COMBINE_TILE = 128
HEAD_LANES = 64
ONES_LANE = 64
BIAS_LANE = 64
ATTN_HEADS_PER_STEP = 8
LOG2E = math.log2(math.e)
NEG_BIG = -(2.0 ** 100)

F32 = jnp.float32
BF16 = jnp.bfloat16
NT_DIMS = (((1,), (1,)), ((), ()))


def _dot(a, b, precision=None):
    return jnp.dot(a, b, preferred_element_type=F32, precision=precision)


def _dot_nt(a, b, precision=None):
    return lax.dot_general(a, b, NT_DIMS, preferred_element_type=F32, precision=precision)


def _rms(x, g):
    return x * lax.rsqrt(jnp.mean(x * x, axis=-1, keepdims=True) + RMS_EPS) * g


def _layer_norm(x, g, b):
    mu = jnp.mean(x, axis=-1, keepdims=True)
    xc = x - mu
    var = jnp.mean(xc * xc, axis=-1, keepdims=True)
    return xc * lax.rsqrt(var + LN_EPS) * g + b


_C_QL = 0
_C_KVL = _C_QL + Q_LORA
_C_KR = _C_KVL + KV_LORA
_C_KRS = _C_KR + SLOT
_C_MK = _C_KRS + SLOT
_MH = MOBA_HEADS * MOBA_HD
_C_MKS = _C_MK + _MH
_C_END = _C_MKS + _MH


def _prep_kernel(x_ref, win_ref, wmt_ref, wqt_ref, wqst_ref, wk_ref, wvt_ref, qg_ref, kvg_ref,
                 ca_ref, sa_ref, cat_ref, sat_ref, cb_ref, sb_ref, cbt_ref, sbt_ref,
                 qat_ref, ka_ref, vat_ref, mqt_ref, mk_ref, mvt_ref, kmean_scr):
    c = pl.program_id(1)
    xb = x_ref[...].astype(BF16)
    ca = ca_ref[...]
    sa = sa_ref[...]
    lane = lax.broadcasted_iota(jnp.int32, (ROW_TILE, SLOT), 1)
    head_lanes = lane < HEAD_LANES
    ones_rows = (lax.broadcasted_iota(jnp.int32, (SLOT - HEAD_LANES, ROW_TILE), 0) == ONES_LANE - HEAD_LANES).astype(F32)

    ql = _dot(xb, win_ref[:, _C_QL:_C_KVL])
    kvl = _dot(xb, win_ref[:, _C_KVL:_C_KR])
    kr = _dot(xb, win_ref[:, _C_KR:_C_KRS])
    krs = _dot(xb, win_ref[:, _C_KRS:_C_MK])
    qn = _rms(ql, qg_ref[...]).astype(BF16)
    kvn = _rms(kvl, kvg_ref[...]).astype(BF16)
    q_t = _dot_nt(wqt_ref[...], qn)
    qs_t = _dot_nt(wqst_ref[...], qn)
    kn = _dot(kvn, wk_ref[...])
    v_t = _dot_nt(wvt_ref[...], kvn)
    cat = cat_ref[...]
    sat = sat_ref[...]
    scale_a = LOG2E / math.sqrt(MLA_NOPE + MLA_ROPE)
    kro = kr * ca + krs * sa
    for h in range(MLA_HEADS):
        sl = slice(h * SLOT, (h + 1) * SLOT)
        qat_ref[0, sl, :] = ((q_t[sl] * cat + qs_t[sl] * sat) * scale_a).astype(BF16)
        ka_ref[:, sl] = (kn[:, sl] + kro).astype(BF16)
        vat_ref[0, sl, :] = jnp.concatenate([v_t[h * SLOT:h * SLOT + HEAD_LANES], ones_rows], axis=0).astype(BF16)

    mk = _dot(xb, win_ref[:, _C_MK:_C_MKS])
    mks = _dot(xb, win_ref[:, _C_MKS:_C_END])
    cb = cb_ref[...]
    sb = sb_ref[...]
    npair = _MH // LANES
    pair = lambda a, j: a[:, j * LANES:(j + 1) * LANES]
    mk_rot = [pair(mk, j) * cb + pair(mks, j) * sb for j in range(npair)]
    mk_all = jnp.concatenate(mk_rot, axis=1)

    mq_t = _dot_nt(wmt_ref[0:_MH, :], xb)
    mqs_t = _dot_nt(wmt_ref[_MH:2 * _MH, :], xb)
    mv_t = _dot_nt(wmt_ref[2 * _MH:3 * _MH, :], xb)
    cbt = cbt_ref[...]
    sbt = sbt_ref[...]
    head = lambda a, h: a[h * MOBA_HD:(h + 1) * MOBA_HD]
    mq_rot_t = [head(mq_t, h) * cbt + head(mqs_t, h) * sbt for h in range(MOBA_HEADS)]

    nrow = MOBA_HEADS * SUBLANES
    @pl.when(c == 0)
    def _():
        kmean_scr[...] = jnp.zeros_like(kmean_scr)

    row_i = lax.broadcasted_iota(jnp.int32, (nrow, _MH), 0)
    lane_i = lax.broadcasted_iota(jnp.int32, (nrow, _MH), 1)
    kmean_c = jnp.mean(mk_all, axis=0, keepdims=True)
    put = ((row_i % SUBLANES) == c) & ((lane_i // MOBA_HD) == (row_i // SUBLANES))
    table = kmean_scr[...]
    gate_t = _dot(table, jnp.concatenate(mq_rot_t, axis=0), precision=lax.Precision.HIGHEST)
    kmean_scr[...] = jnp.where(put, jnp.broadcast_to(kmean_c, (nrow, _MH)), table)

    n_idx = lax.broadcasted_iota(jnp.int32, (SUBLANES, ROW_TILE), 0)
    valid = n_idx < c
    scale_b = LOG2E / math.sqrt(MOBA_HD)
    zero_rows = jnp.zeros((SLOT - HEAD_LANES - SUBLANES, ROW_TILE), F32)
    block_onehot = (lane == BIAS_LANE + c).astype(F32)
    for h in range(MOBA_HEADS):
        g = jnp.where(valid, gate_t[h * SUBLANES:(h + 1) * SUBLANES, :], -jnp.inf)
        rank = jnp.zeros((SUBLANES, ROW_TILE), jnp.int32)
        for k in range(1, SUBLANES):
            other = pltpu.roll(g, k, axis=0)
            other_n = pltpu.roll(n_idx, k, axis=0)
            beats = (other > g) | ((other == g) & (other_n < n_idx))
            rank = rank + beats.astype(jnp.int32)
        keep = (valid & (rank < MOBA_TOPK)) | (n_idx == c)
        bias = jnp.where(keep, 0.0, NEG_BIG)
        sl = slice(h * SLOT, (h + 1) * SLOT)
        mqt_ref[0, sl, :] = jnp.concatenate([mq_rot_t[h] * scale_b, bias, zero_rows], axis=0).astype(BF16)
        mvt_ref[0, sl, :] = jnp.concatenate([head(mv_t, h), ones_rows], axis=0).astype(BF16)
        j, hh = divmod(h, 2)
        k_h = mk_rot[j] if hh == 0 else pltpu.roll(mk_rot[j], HEAD_LANES, axis=1)
        mk_ref[:, sl] = jnp.where(head_lanes, k_h, block_onehot).astype(BF16)


def _prep(x2, win, wmt, wqt, wqst, wk, wvt, qg, kvg, tables, B, T):
    N, D = x2.shape
    nt = T // ROW_TILE
    row = lambda b, c: (b * nt + c, 0)
    col = lambda b, c: (0, b * nt + c)
    full = lambda b, c: (0, 0)
    ca, sa, cat, sat, cb, sb, cbt, sbt = tables
    width = MLA_HEADS * SLOT

    def rows(w):
        return pl.BlockSpec((ROW_TILE, w), row)

    def cols(a):
        return pl.BlockSpec((a.shape[0], ROW_TILE), col)

    def whole(a):
        return pl.BlockSpec(a.shape, full)

    rowmajor = jax.ShapeDtypeStruct((N, width), BF16)
    transposed = jax.ShapeDtypeStruct((B * nt, width, ROW_TILE), BF16)
    t_spec = pl.BlockSpec((1, width, ROW_TILE), lambda b, c: (b * nt + c, 0, 0))
    return pl.pallas_call(
        _prep_kernel,
        out_shape=(transposed, rowmajor, transposed, transposed, rowmajor, transposed),
        grid=(B, nt),
        in_specs=[rows(D), whole(win), whole(wmt), whole(wqt), whole(wqst), whole(wk), whole(wvt), whole(qg), whole(kvg),
                  rows(LANES), rows(LANES), cols(cat), cols(sat), rows(LANES), rows(LANES), cols(cbt), cols(sbt)],
        out_specs=(t_spec, rows(width), t_spec, t_spec, rows(width), t_spec),
        scratch_shapes=[pltpu.VMEM((MOBA_HEADS * SUBLANES, _MH), F32)],
        compiler_params=pltpu.CompilerParams(dimension_semantics=("arbitrary", "arbitrary"),
                                             vmem_limit_bytes=VMEM_LIMIT_BYTES),
        name="prep",
    )(x2, win, wmt, wqt, wqst, wk, wvt, qg, kvg, ca, sa, cat, sat, cb, sb, cbt, sbt)


def _attn_kernel(qt_ref, k_ref, vt_ref, o_ref, s_scr):
    i = pl.program_id(2)
    nh = ATTN_HEADS_PER_STEP
    slot = lambda h: slice(h * SLOT, (h + 1) * SLOT)

    def scores(h, j0):
        return _dot(k_ref[pl.ds(j0, TQ), slot(h)], qt_ref[0, slot(h), :])

    def fold(s):
        out = s[0:SUBLANES]
        for t in range(1, TQ // SUBLANES):
            out = jnp.maximum(out, s[t * SUBLANES:(t + 1) * SUBLANES])
        return out

    def pass1(j, mrun):
        j0 = pl.multiple_of(j * TQ, TQ)
        out = []
        for h in range(nh):
            s = scores(h, j0)
            s_scr[h, j] = s
            out.append(jnp.maximum(mrun[h], fold(s)))
        return tuple(out)

    mrun = lax.fori_loop(0, i, pass1, tuple(jnp.full((SUBLANES, TQ), -jnp.inf, F32) for _ in range(nh)))
    d0 = pl.multiple_of(i * TQ, TQ)
    key = lax.broadcasted_iota(jnp.int32, (TQ, TQ), 0)
    query = lax.broadcasted_iota(jnp.int32, (TQ, TQ), 1)
    ms = []
    for h in range(nh):
        s = jnp.where(key <= query, scores(h, d0), -jnp.inf)
        s_scr[h, i] = s
        ms.append(jnp.max(jnp.maximum(mrun[h], fold(s)), axis=0, keepdims=True))

    def pass2(j, acc):
        out = []
        for h in range(nh):
            p = jnp.exp2(s_scr[h, j] - ms[h]).astype(BF16)
            out.append(acc[h] + _dot(vt_ref[j, slot(h), :], p))
        return tuple(out)

    acc = lax.fori_loop(0, i + 1, pass2, tuple(jnp.zeros((SLOT, TQ), F32) for _ in range(nh)))
    outs = [a[0:HEAD_LANES] / a[ONES_LANE:ONES_LANE + 1] for a in acc]
    for jj in range(nh // 2):
        both = jnp.concatenate([outs[2 * jj], outs[2 * jj + 1]], axis=0)
        o_ref[:, jj * LANES:(jj + 1) * LANES] = both.T.astype(o_ref.dtype)


def _attention(qt, k, vt, B, T, name):
    N = k.shape[0]
    nq = T // TQ
    nh = ATTN_HEADS_PER_STEP
    heads = k.shape[1] // SLOT
    return pl.pallas_call(
        _attn_kernel,
        out_shape=jax.ShapeDtypeStruct((N, heads * HEAD_LANES), BF16),
        grid=(B, heads // nh, nq),
        in_specs=[pl.BlockSpec((1, nh * SLOT, TQ), lambda b, g, i: (b * nq + i, g, 0)),
                  pl.BlockSpec((T, nh * SLOT), lambda b, g, i: (b, g)),
                  pl.BlockSpec((nq, nh * SLOT, TQ), lambda b, g, i: (b, g, 0))],
        out_specs=pl.BlockSpec((TQ, nh * HEAD_LANES), lambda b, g, i: (b * nq + i, g)),
        scratch_shapes=[pltpu.VMEM((nh, nq, TQ, TQ), F32)],
        compiler_params=pltpu.CompilerParams(dimension_semantics=("arbitrary", "arbitrary", "arbitrary"),
                                             vmem_limit_bytes=VMEM_LIMIT_BYTES),
        name=name,
    )(qt, k, vt)


def _oproj_kernel(alpha, a_ref, m_ref, x_ref, wo_ref, g_ref, b_ref, wr_ref, br_ref,
                  x1_ref, route_ref, gates_ref, cnt_ref, carry_scr):
    i = pl.program_id(0)

    @pl.when(i == 0)
    def _():
        carry_scr[...] = jnp.zeros_like(carry_scr)

    wa = a_ref.shape[1]
    mix = _dot(a_ref[...], wo_ref[:wa, :]) + _dot(m_ref[...], wo_ref[wa:, :])
    x1 = _layer_norm(alpha * x_ref[...] + mix, g_ref[...], b_ref[...])
    x1_ref[...] = x1

    logits = _dot(x1, wr_ref[...], precision=lax.Precision.HIGHEST) + br_ref[...]
    lane = lax.broadcasted_iota(jnp.int32, logits.shape, 1)
    lane_f = lane.astype(F32)
    vals, idxs = [], []
    work = logits
    for _ in range(MOE_TOPK):
        mx = jnp.max(work, axis=-1, keepdims=True)
        ix = jnp.min(jnp.where(work == mx, lane_f, float(LANES)), axis=-1, keepdims=True).astype(jnp.int32)
        vals.append(mx)
        idxs.append(ix)
        work = jnp.where(lane == ix, -jnp.inf, work)
    exps = [jnp.exp(v - vals[0]) for v in vals]
    den = exps[0]
    for e in exps[1:]:
        den = den + e

    onehot = jnp.zeros(logits.shape, F32)
    for ix in idxs:
        onehot = onehot + (lane == ix).astype(F32)
    rows = logits.shape[0]
    r = lax.broadcasted_iota(jnp.int32, (rows, rows), 0)
    cidx = lax.broadcasted_iota(jnp.int32, (rows, rows), 1)
    lower = (cidx < r).astype(BF16)
    carry = carry_scr[0:1, :]
    before = _dot(lower, onehot.astype(BF16)) + carry
    route = jnp.zeros(logits.shape, jnp.int32)
    gates = jnp.zeros(logits.shape, F32)
    for k in range(MOE_TOPK):
        rank = jnp.sum(jnp.where(lane == idxs[k], before, 0.0), axis=-1, keepdims=True).astype(jnp.int32)
        route = jnp.where(lane == k, idxs[k], route)
        route = jnp.where(lane == MOE_TOPK + k, rank, route)
        gates = jnp.where(lane == k, exps[k] / den, gates)
    route_ref[...] = route
    gates_ref[...] = gates
    new_carry = carry + jnp.sum(onehot, axis=0, keepdims=True)
    carry_scr[...] = jnp.broadcast_to(new_carry, carry_scr.shape)
    cnt_ref[...] = jnp.broadcast_to(new_carry, cnt_ref.shape)


def _oproj(a, m, x2, wo, g1, b1, wr, br, alpha):
    N, D = x2.shape
    nt = N // ROW_TILE
    row = lambda i: (i, 0)
    full = lambda i: (0, 0)
    return pl.pallas_call(
        functools.partial(_oproj_kernel, alpha),
        out_shape=(jax.ShapeDtypeStruct((N, D), F32),
                   jax.ShapeDtypeStruct((N, LANES), jnp.int32),
                   jax.ShapeDtypeStruct((N, LANES), F32),
                   jax.ShapeDtypeStruct((SUBLANES, LANES), F32)),
        grid=(nt,),
        in_specs=[pl.BlockSpec((ROW_TILE, a.shape[1]), row), pl.BlockSpec((ROW_TILE, m.shape[1]), row),
                  pl.BlockSpec((ROW_TILE, D), row), pl.BlockSpec(wo.shape, full),
                  pl.BlockSpec(g1.shape, full), pl.BlockSpec(b1.shape, full),
                  pl.BlockSpec(wr.shape, full), pl.BlockSpec(br.shape, full)],
        out_specs=(pl.BlockSpec((ROW_TILE, D), row), pl.BlockSpec((ROW_TILE, LANES), row),
                   pl.BlockSpec((ROW_TILE, LANES), row), pl.BlockSpec((SUBLANES, LANES), full)),
        scratch_shapes=[pltpu.VMEM((SUBLANES, LANES), F32)],
        compiler_params=pltpu.CompilerParams(dimension_semantics=("arbitrary",),
                                             vmem_limit_bytes=VMEM_LIMIT_BYTES),
        name="oproj_router",
    )(a, m, x2, wo, g1, b1, wr, br)


def _dispatch_kernel(dest_ref, tail_ref, x1_ref, xs_ref, zero_scr, sem):
    i = pl.program_id(0)

    @pl.when(i == 0)
    def _():
        zero_scr[...] = jnp.zeros_like(zero_scr)

        def tail_copy(e):
            return pltpu.make_async_copy(zero_scr, xs_ref.at[pl.ds(pl.multiple_of(tail_ref[e], MOE_GROUP), MOE_GROUP)], sem)

        def start(e, _):
            @pl.when(tail_ref[e] >= 0)
            def _():
                tail_copy(e).start()
            return 0

        def wait(e, _):
            @pl.when(tail_ref[e] >= 0)
            def _():
                tail_copy(e).wait()
            return 0

        lax.fori_loop(0, N_EXPERTS, start, 0)
        lax.fori_loop(0, N_EXPERTS, wait, 0)

        def spare_copy(blk):
            return pltpu.make_async_copy(zero_scr, xs_ref.at[pl.ds(pl.multiple_of(blk * MOE_GROUP, MOE_GROUP), MOE_GROUP)], sem)

        def start_spare(blk, _):
            spare_copy(blk).start()
            return 0

        def wait_spare(blk, _):
            spare_copy(blk).wait()
            return 0

        nblk = xs_ref.shape[0] // MOE_GROUP
        lax.fori_loop(tail_ref[N_EXPERTS], nblk, start_spare, 0)
        lax.fori_loop(tail_ref[N_EXPERTS], nblk, wait_spare, 0)

    base = i * DISPATCH_TILE

    def row_copy(t, k):
        d = dest_ref[(base + t) * MOE_TOPK + k]
        return pltpu.make_async_copy(x1_ref.at[pl.ds(t, 1)], xs_ref.at[pl.ds(d, 1)], sem)

    def start_rows(t, _):
        for k in range(MOE_TOPK):
            row_copy(t, k).start()
        return 0

    lax.fori_loop(0, DISPATCH_TILE, start_rows, 0)
    for k in range(MOE_TOPK):
        pltpu.make_async_copy(x1_ref, xs_ref.at[pl.ds(0, DISPATCH_TILE)], sem).wait()


def _dispatch(dest, tail, x1, n_rows):
    N, D = x1.shape
    return pl.pallas_call(
        _dispatch_kernel,
        out_shape=jax.ShapeDtypeStruct((n_rows, D), F32),
        grid_spec=pltpu.PrefetchScalarGridSpec(
            num_scalar_prefetch=2,
            grid=(N // DISPATCH_TILE,),
            in_specs=[pl.BlockSpec((DISPATCH_TILE, D), lambda i, d, t: (i, 0))],
            out_specs=pl.BlockSpec(memory_space=pl.ANY),
            scratch_shapes=[pltpu.VMEM((MOE_GROUP, D), F32), pltpu.SemaphoreType.DMA(())],
        ),
        compiler_params=pltpu.CompilerParams(dimension_semantics=("arbitrary",),
                                             vmem_limit_bytes=VMEM_LIMIT_BYTES),
        name="dispatch",
    )(dest, tail, x1)


def _experts_kernel(be_ref, nused_ref, x_ref, wg_ref, bg_ref, wu_ref, bu_ref, wd_ref, bd_ref,
                    y_ref, wg_bf, wu_bf, wd_bf):
    i = pl.program_id(0)
    prev = be_ref[jnp.maximum(i - 1, 0)]
    changed = (i == 0) | (be_ref[i] != prev)
    active = i < nused_ref[0]

    @pl.when(active & changed)
    def _():
        wg_bf[...] = wg_ref[0].astype(BF16)
        wu_bf[...] = wu_ref[0].astype(BF16)
        wd_bf[...] = wd_ref[0].astype(BF16)

    @pl.when(active)
    def _():
        xb = x_ref[...].astype(BF16)
        g = jnp.minimum(_dot(xb, wg_bf[...]) + bg_ref[0], SWIGLU_LIMIT)
        u = jnp.clip(_dot(xb, wu_bf[...]) + bu_ref[0], -SWIGLU_LIMIT, SWIGLU_LIMIT)
        h = g * (1.0 / (1.0 + jnp.exp(-SWIGLU_ALPHA * g))) * (u + 1.0)
        y_ref[...] = _dot(h.astype(BF16), wd_bf[...]) + bd_ref[0]

    @pl.when(jnp.logical_not(active))
    def _():
        y_ref[...] = jnp.zeros_like(y_ref)


def _experts(blk_expert, n_used, xs, wg, bg, wu, bu, wd, bd):
    n_rows, D = xs.shape
    E, _, F = wg.shape
    nblk = n_rows // MOE_GROUP

    def rowmap(i, be, nu):
        return (jnp.minimum(i, nu[0] - 1), 0)

    def wmap(i, be, nu):
        return (be[i], 0, 0)

    return pl.pallas_call(
        _experts_kernel,
        out_shape=jax.ShapeDtypeStruct((n_rows, D), F32),
        grid_spec=pltpu.PrefetchScalarGridSpec(
            num_scalar_prefetch=2,
            grid=(nblk,),
            in_specs=[pl.BlockSpec((MOE_GROUP, D), rowmap),
                      pl.BlockSpec((1, D, F), wmap), pl.BlockSpec((1, 1, F), wmap),
                      pl.BlockSpec((1, D, F), wmap), pl.BlockSpec((1, 1, F), wmap),
                      pl.BlockSpec((1, F, D), wmap), pl.BlockSpec((1, 1, D), wmap)],
            out_specs=pl.BlockSpec((MOE_GROUP, D), lambda i, be, nu: (i, 0)),
            scratch_shapes=[pltpu.VMEM((D, F), BF16), pltpu.VMEM((D, F), BF16), pltpu.VMEM((F, D), BF16)],
        ),
        compiler_params=pltpu.CompilerParams(dimension_semantics=("arbitrary",),
                                             vmem_limit_bytes=VMEM_LIMIT_BYTES),
        name="experts",
    )(blk_expert, n_used, xs, wg, bg, wu, bu, wd, bd)


def _combine_kernel(alpha, dest_ref, gates_ref, x1_ref, g_ref, b_ref, y_ref, o_ref, ybuf, sems):
    i = pl.program_id(0)
    buf = i % 2

    def gather_tile(tile, into):
        base = tile * COMBINE_TILE

        def start_rows(t, _):
            for k in range(MOE_TOPK):
                d = dest_ref[(base + t) * MOE_TOPK + k]
                pltpu.make_async_copy(y_ref.at[pl.ds(d, 1)], ybuf.at[into, k, pl.ds(t, 1)], sems.at[into]).start()
            return 0

        lax.fori_loop(0, COMBINE_TILE, start_rows, 0)

    @pl.when(i == 0)
    def _():
        gather_tile(0, 0)

    @pl.when(i + 1 < pl.num_programs(0))
    def _():
        gather_tile(i + 1, 1 - buf)

    for k in range(MOE_TOPK):
        pltpu.make_async_copy(y_ref.at[pl.ds(0, COMBINE_TILE)], ybuf.at[buf, k], sems.at[buf]).wait()

    gates = gates_ref[...]
    ffn = gates[:, 0:1] * ybuf[buf, 0]
    for k in range(1, MOE_TOPK):
        ffn = ffn + gates[:, k:k + 1] * ybuf[buf, k]
    o_ref[...] = _layer_norm(alpha * x1_ref[...] + ffn, g_ref[...], b_ref[...])


def _combine(dest, gates, x1, g2, b2, y_rows, alpha):
    N, D = x1.shape
    row = lambda i, d: (i, 0)
    full = lambda i, d: (0, 0)
    return pl.pallas_call(
        functools.partial(_combine_kernel, alpha),
        out_shape=jax.ShapeDtypeStruct((N, D), F32),
        grid_spec=pltpu.PrefetchScalarGridSpec(
            num_scalar_prefetch=1,
            grid=(N // COMBINE_TILE,),
            in_specs=[pl.BlockSpec((COMBINE_TILE, LANES), row), pl.BlockSpec((COMBINE_TILE, D), row),
                      pl.BlockSpec(g2.shape, full), pl.BlockSpec(b2.shape, full),
                      pl.BlockSpec(memory_space=pl.ANY)],
            out_specs=pl.BlockSpec((COMBINE_TILE, D), row),
            scratch_shapes=[pltpu.VMEM((2, MOE_TOPK, COMBINE_TILE, D), F32), pltpu.SemaphoreType.DMA((2,))],
        ),
        compiler_params=pltpu.CompilerParams(dimension_semantics=("arbitrary",),
                                             vmem_limit_bytes=VMEM_LIMIT_BYTES),
        name="combine",
    )(dest, gates, x1, g2, b2, y_rows)


def _rot_partner(w, half):
    return jnp.concatenate([-w[..., half:2 * half], w[..., :half]], axis=-1)


def _layer_weights(w_in, w_q_b, w_kv_b):
    D = w_in.shape[0]
    o1 = Q_LORA
    o2 = o1 + KV_LORA
    o3 = o2 + MLA_ROPE
    w_ql, w_kvl, w_kr = w_in[:, :o1], w_in[:, o1:o2], w_in[:, o2:o3]
    w_mq, w_mk, w_mv = w_in[:, o3:o3 + _MH], w_in[:, o3 + _MH:o3 + 2 * _MH], w_in[:, o3 + 2 * _MH:]
    zpad = lambda n: jnp.zeros((D, n), w_in.dtype)
    tail = SLOT - MLA_NOPE - MLA_ROPE
    kr_slot = jnp.concatenate([zpad(MLA_NOPE), w_kr, zpad(tail)], axis=1)
    krs_slot = jnp.concatenate([zpad(MLA_NOPE), _rot_partner(w_kr, MLA_ROPE // 2), zpad(tail)], axis=1)

    def moba_partner(w):
        w3 = w.reshape(D, MOBA_HEADS, MOBA_HD)
        part = jnp.concatenate([_rot_partner(w3[..., :MOBA_ROT], MOBA_ROT // 2),
                                jnp.zeros((D, MOBA_HEADS, MOBA_HD - MOBA_ROT), w.dtype)], axis=-1)
        return part.reshape(D, _MH)

    win = jnp.concatenate([w_ql, w_kvl, kr_slot, krs_slot, w_mk, moba_partner(w_mk)], axis=1).astype(BF16)
    wmt = jnp.concatenate([w_mq, moba_partner(w_mq), w_mv], axis=1).T.astype(BF16)

    wq3 = w_q_b.reshape(Q_LORA, MLA_HEADS, MLA_NOPE + MLA_ROPE)
    zq = jnp.zeros((Q_LORA, MLA_HEADS, tail), w_q_b.dtype)
    wq = jnp.concatenate([wq3, zq], axis=-1).reshape(Q_LORA, MLA_HEADS * SLOT)
    wqs = jnp.concatenate([jnp.zeros((Q_LORA, MLA_HEADS, MLA_NOPE), w_q_b.dtype),
                           _rot_partner(wq3[..., MLA_NOPE:], MLA_ROPE // 2), zq],
                          axis=-1).reshape(Q_LORA, MLA_HEADS * SLOT)
    wkv3 = w_kv_b.reshape(KV_LORA, MLA_HEADS, MLA_NOPE + MLA_V)
    wk = jnp.concatenate([wkv3[..., :MLA_NOPE], jnp.zeros((KV_LORA, MLA_HEADS, SLOT - MLA_NOPE), w_kv_b.dtype)],
                         axis=-1).reshape(KV_LORA, MLA_HEADS * SLOT).astype(BF16)
    wv = jnp.concatenate([wkv3[..., MLA_NOPE:], jnp.zeros((KV_LORA, MLA_HEADS, SLOT - MLA_V), w_kv_b.dtype)],
                         axis=-1).reshape(KV_LORA, MLA_HEADS * SLOT)
    return win, wmt, wq.T.astype(BF16), wqs.T.astype(BF16), wk, wv.T.astype(BF16)


def _rope_tables(positions):
    pos = positions.astype(F32).reshape(-1, 1)
    n = pos.shape[0]

    def cs(d_rot):
        inv_freq = ROPE_THETA ** (-jnp.arange(0, d_rot, 2, dtype=F32) / d_rot)
        ang = pos * inv_freq
        return jnp.cos(ang), jnp.sin(ang)

    ca, sa = cs(MLA_ROPE)
    tail = SLOT - MLA_NOPE - MLA_ROPE
    ca_t = jnp.concatenate([jnp.ones((n, MLA_NOPE), F32), ca, ca, jnp.zeros((n, tail), F32)], axis=1)
    sa_t = jnp.concatenate([jnp.zeros((n, MLA_NOPE), F32), sa, sa, jnp.zeros((n, tail), F32)], axis=1)
    cb, sb = cs(MOBA_ROT)
    rest = MOBA_HD - MOBA_ROT
    cb_h = jnp.concatenate([cb, cb, jnp.ones((n, rest), F32)], axis=1)
    sb_h = jnp.concatenate([sb, sb, jnp.zeros((n, rest), F32)], axis=1)
    return (ca_t, sa_t, ca_t.T, sa_t.T, jnp.concatenate([cb_h, cb_h], axis=1), jnp.concatenate([sb_h, sb_h], axis=1),
            cb_h.T, sb_h.T)


def kernel(x, positions, w_in, q_a_norm, w_q_b, kv_a_norm, w_kv_b, w_o, ln1_g, ln1_b, w_router, b_router,
           w_gate, b_gate, w_up, b_up, w_down, b_down, ln2_g, ln2_b):
    B, T, D = x.shape
    depth = w_in.shape[0]
    alpha = (2.0 * depth) ** 0.25
    N = B * T
    assert T % MOBA_BLOCK == 0 and T // MOBA_BLOCK <= SUBLANES and N % ROW_TILE == 0
    n_asg = N * MOE_TOPK
    n_rows = n_asg + N_EXPERTS * MOE_GROUP
    tables = _rope_tables(positions)
    h = x.reshape(N, D)
    for l in range(depth):
        win, wmt, wqt, wqst, wk, wvt = _layer_weights(w_in[l], w_q_b[l], w_kv_b[l])
        qa, ka, va, mq, mk, mv = _prep(h, win, wmt, wqt, wqst, wk, wvt, q_a_norm[l].reshape(1, -1),
                                       kv_a_norm[l].reshape(1, -1), tables, B, T)
        a = _attention(qa, ka, va, B, T, "mla_attention")
        m = _attention(mq, mk, mv, B, T, "moba_attention")
        wr = jnp.concatenate([w_router[l], jnp.zeros((D, LANES - N_EXPERTS), F32)], axis=1)
        br = jnp.concatenate([b_router[l], jnp.full((LANES - N_EXPERTS,), -jnp.inf, F32)]).reshape(1, LANES)
        x1, route, gates, cnt = _oproj(a, m, h, w_o[l].astype(BF16), ln1_g[l].reshape(1, D), ln1_b[l].reshape(1, D),
                                       wr, br, alpha)
        counts = cnt[0, :N_EXPERTS].astype(jnp.int32)
        padded = (counts + MOE_GROUP - 1) // MOE_GROUP * MOE_GROUP
        pad_end = jnp.cumsum(padded).astype(jnp.int32)
        pad_start = pad_end - padded
        e_idx = route[:, :MOE_TOPK]
        dest = (pad_start[e_idx] + route[:, MOE_TOPK:2 * MOE_TOPK]).reshape(n_asg)
        n_used = (pad_end[-1:] // MOE_GROUP).astype(jnp.int32)
        tail = jnp.concatenate([jnp.where(padded > 0, pad_end - MOE_GROUP, -1).astype(jnp.int32), n_used])
        blk_start = jnp.arange(n_rows // MOE_GROUP, dtype=jnp.int32) * MOE_GROUP
        blk_expert = jnp.minimum(jnp.sum((blk_start[:, None] >= pad_end[None, :]).astype(jnp.int32), axis=1),
                                 N_EXPERTS - 1)
        xs = _dispatch(dest, tail, x1, n_rows)
        y_rows = _experts(blk_expert, n_used, xs, w_gate[l], b_gate[l].reshape(N_EXPERTS, 1, -1),
                          w_up[l], b_up[l].reshape(N_EXPERTS, 1, -1), w_down[l], b_down[l].reshape(N_EXPERTS, 1, -1))
        h = _combine(dest, gates, x1, ln2_g[l].reshape(1, D), ln2_b[l].reshape(1, D), y_rows, alpha)
    return h.reshape(B, T, D)
```

```python
import functools
import math

import jax
import jax.numpy as jnp
from jax import lax
from jax.experimental import pallas as pl
from jax.experimental.pallas import tpu as pltpu

ROPE_THETA = 500000.0
MLA_HEADS = 8
MLA_NOPE = 64
MLA_ROPE = 32
MLA_V = 64
Q_LORA = 256
KV_LORA = 128
MOBA_HEADS = 8
MOBA_HD = 64
MOBA_ROT = MOBA_HD // 4
MOBA_BLOCK = 256
MOBA_TOPK = 3
N_EXPERTS = 32
MOE_TOPK = 4
SWIGLU_LIMIT = 7.0
SWIGLU_ALPHA = 1.702
RMS_EPS = 1e-6
LN_EPS = 1e-5

LANES = 128
SUBLANES = 8
VMEM_LIMIT_BYTES = 56 * 1024 * 1024

SLOT = LANES
TQ = MOBA_BLOCK
ROW_TILE = 256
MOE_GROUP = 256
GROUP_SUB = MOE_GROUP * SUBLANES
DISPATCH_TILE = 256
COMBINE_TILE = 128
ISSUE_UNROLL = 4
HEAD_LANES = 64
ONES_LANE = 64
BIAS_LANE = 64
ATTN_HEADS_PER_STEP = 8
LOG2E = math.log2(math.e)
NEG_BIG = -(2.0 ** 100)

F32 = jnp.float32
BF16 = jnp.bfloat16
NT_DIMS = (((1,), (1,)), ((), ()))


def _dot(a, b, precision=None):
    return jnp.dot(a, b, preferred_element_type=F32, precision=precision)


def _dot_nt(a, b, precision=None):
    return lax.dot_general(a, b, NT_DIMS, preferred_element_type=F32, precision=precision)


def _rows_to_tiles(ref, x):
    rows = x.shape[0]
    for c in range(SUBLANES):
        ref[pl.ds(c, rows, stride=SUBLANES), :] = x[:, c * LANES:(c + 1) * LANES]


def _tiles_to_rows(ref, rows):
    return jnp.concatenate([ref[pl.ds(c, rows, stride=SUBLANES), :] for c in range(SUBLANES)], axis=1)


def _rms(x, g):
    return x * lax.rsqrt(jnp.mean(x * x, axis=-1, keepdims=True) + RMS_EPS) * g


def _layer_norm(x, g, b):
    mu = jnp.mean(x, axis=-1, keepdims=True)
    xc = x - mu
    var = jnp.mean(xc * xc, axis=-1, keepdims=True)
    return xc * lax.rsqrt(var + LN_EPS) * g + b


_C_QL = 0
_C_KVL = _C_QL + Q_LORA
_C_KR = _C_KVL + KV_LORA
_C_KRS = _C_KR + SLOT
_C_MK = _C_KRS + SLOT
_MH = MOBA_HEADS * MOBA_HD
_C_MKS = _C_MK + _MH
_C_END = _C_MKS + _MH


def _prep_kernel(x_ref, win_ref, wmt_ref, wqt_ref, wqst_ref, wk_ref, wvt_ref, qg_ref, kvg_ref,
                 ca_ref, sa_ref, cat_ref, sat_ref, cb_ref, sb_ref, cbt_ref, sbt_ref,
                 qat_ref, ka_ref, vat_ref, mqt_ref, mk_ref, mvt_ref, kmean_scr):
    c = pl.program_id(1)
    xb = x_ref[...].astype(BF16)
    ca = ca_ref[...]
    sa = sa_ref[...]
    lane = lax.broadcasted_iota(jnp.int32, (ROW_TILE, SLOT), 1)
    head_lanes = lane < HEAD_LANES
    ones_rows = (lax.broadcasted_iota(jnp.int32, (SLOT - HEAD_LANES, ROW_TILE), 0) == ONES_LANE - HEAD_LANES).astype(F32)

    ql = _dot(xb, win_ref[:, _C_QL:_C_KVL])
    kvl = _dot(xb, win_ref[:, _C_KVL:_C_KR])
    kr = _dot(xb, win_ref[:, _C_KR:_C_KRS])
    krs = _dot(xb, win_ref[:, _C_KRS:_C_MK])
    qn = _rms(ql, qg_ref[...]).astype(BF16)
    kvn = _rms(kvl, kvg_ref[...]).astype(BF16)
    q_t = _dot_nt(wqt_ref[...], qn)
    qs_t = _dot_nt(wqst_ref[...], qn)
    kn = _dot(kvn, wk_ref[...])
    v_t = _dot_nt(wvt_ref[...], kvn)
    cat = cat_ref[...]
    sat = sat_ref[...]
    scale_a = LOG2E / math.sqrt(MLA_NOPE + MLA_ROPE)
    kro = kr * ca + krs * sa
    for h in range(MLA_HEADS):
        sl = slice(h * SLOT, (h + 1) * SLOT)
        qat_ref[0, sl, :] = ((q_t[sl] * cat + qs_t[sl] * sat) * scale_a).astype(BF16)
        ka_ref[:, sl] = (kn[:, sl] + kro).astype(BF16)
        vat_ref[0, sl, :] = jnp.concatenate([v_t[h * SLOT:h * SLOT + HEAD_LANES], ones_rows], axis=0).astype(BF16)

    mk = _dot(xb, win_ref[:, _C_MK:_C_MKS])
    mks = _dot(xb, win_ref[:, _C_MKS:_C_END])
    cb = cb_ref[...]
    sb = sb_ref[...]
    npair = _MH // LANES
    pair = lambda a, j: a[:, j * LANES:(j + 1) * LANES]
    mk_rot = [pair(mk, j) * cb + pair(mks, j) * sb for j in range(npair)]
    mk_all = jnp.concatenate(mk_rot, axis=1)

    mq_t = _dot_nt(wmt_ref[0:_MH, :], xb)
    mqs_t = _dot_nt(wmt_ref[_MH:2 * _MH, :], xb)
    mv_t = _dot_nt(wmt_ref[2 * _MH:3 * _MH, :], xb)
    cbt = cbt_ref[...]
    sbt = sbt_ref[...]
    head = lambda a, h: a[h * MOBA_HD:(h + 1) * MOBA_HD]
    mq_rot_t = [head(mq_t, h) * cbt + head(mqs_t, h) * sbt for h in range(MOBA_HEADS)]

    nrow = MOBA_HEADS * SUBLANES
    @pl.when(c == 0)
    def _():
        kmean_scr[...] = jnp.zeros_like(kmean_scr)

    row_i = lax.broadcasted_iota(jnp.int32, (nrow, _MH), 0)
    lane_i = lax.broadcasted_iota(jnp.int32, (nrow, _MH), 1)
    kmean_c = jnp.mean(mk_all, axis=0, keepdims=True)
    put = ((row_i % SUBLANES) == c) & ((lane_i // MOBA_HD) == (row_i // SUBLANES))
    table = kmean_scr[...]
    gate_t = _dot(table, jnp.concatenate(mq_rot_t, axis=0), precision=lax.Precision.HIGHEST)
    kmean_scr[...] = jnp.where(put, jnp.broadcast_to(kmean_c, (nrow, _MH)), table)

    n_idx = lax.broadcasted_iota(jnp.int32, (SUBLANES, ROW_TILE), 0)
    valid = n_idx < c
    scale_b = LOG2E / math.sqrt(MOBA_HD)
    zero_rows = jnp.zeros((SLOT - HEAD_LANES - SUBLANES, ROW_TILE), F32)
    block_onehot = (lane == BIAS_LANE + c).astype(F32)
    for h in range(MOBA_HEADS):
        g = jnp.where(valid, gate_t[h * SUBLANES:(h + 1) * SUBLANES, :], -jnp.inf)
        rank = jnp.zeros((SUBLANES, ROW_TILE), jnp.int32)
        for k in range(1, SUBLANES):
            other = pltpu.roll(g, k, axis=0)
            other_n = pltpu.roll(n_idx, k, axis=0)
            beats = (other > g) | ((other == g) & (other_n < n_idx))
            rank = rank + beats.astype(jnp.int32)
        keep = (valid & (rank < MOBA_TOPK)) | (n_idx == c)
        bias = jnp.where(keep, 0.0, NEG_BIG)
        sl = slice(h * SLOT, (h + 1) * SLOT)
        mqt_ref[0, sl, :] = jnp.concatenate([mq_rot_t[h] * scale_b, bias, zero_rows], axis=0).astype(BF16)
        mvt_ref[0, sl, :] = jnp.concatenate([head(mv_t, h), ones_rows], axis=0).astype(BF16)
        j, hh = divmod(h, 2)
        k_h = mk_rot[j] if hh == 0 else pltpu.roll(mk_rot[j], HEAD_LANES, axis=1)
        mk_ref[:, sl] = jnp.where(head_lanes, k_h, block_onehot).astype(BF16)


def _prep(x2, win, wmt, wqt, wqst, wk, wvt, qg, kvg, tables, B, T):
    N, D = x2.shape
    nt = T // ROW_TILE
    row = lambda b, c: (b * nt + c, 0)
    col = lambda b, c: (0, b * nt + c)
    full = lambda b, c: (0, 0)
    ca, sa, cat, sat, cb, sb, cbt, sbt = tables
    width = MLA_HEADS * SLOT

    def rows(w):
        return pl.BlockSpec((ROW_TILE, w), row)

    def cols(a):
        return pl.BlockSpec((a.shape[0], ROW_TILE), col)

    def whole(a):
        return pl.BlockSpec(a.shape, full)

    rowmajor = jax.ShapeDtypeStruct((N, width), BF16)
    transposed = jax.ShapeDtypeStruct((B * nt, width, ROW_TILE), BF16)
    t_spec = pl.BlockSpec((1, width, ROW_TILE), lambda b, c: (b * nt + c, 0, 0))
    return pl.pallas_call(
        _prep_kernel,
        out_shape=(transposed, rowmajor, transposed, transposed, rowmajor, transposed),
        grid=(B, nt),
        in_specs=[rows(D), whole(win), whole(wmt), whole(wqt), whole(wqst), whole(wk), whole(wvt), whole(qg), whole(kvg),
                  rows(LANES), rows(LANES), cols(cat), cols(sat), rows(LANES), rows(LANES), cols(cbt), cols(sbt)],
        out_specs=(t_spec, rows(width), t_spec, t_spec, rows(width), t_spec),
        scratch_shapes=[pltpu.VMEM((MOBA_HEADS * SUBLANES, _MH), F32)],
        compiler_params=pltpu.CompilerParams(dimension_semantics=("arbitrary", "arbitrary"),
                                             vmem_limit_bytes=VMEM_LIMIT_BYTES),
        name="prep",
    )(x2, win, wmt, wqt, wqst, wk, wvt, qg, kvg, ca, sa, cat, sat, cb, sb, cbt, sbt)


def _attn_kernel(qt_ref, k_ref, vt_ref, o_ref, s_scr):
    i = pl.program_id(2)
    nh = ATTN_HEADS_PER_STEP
    slot = lambda h: slice(h * SLOT, (h + 1) * SLOT)

    def scores(h, j0):
        return _dot(k_ref[pl.ds(j0, TQ), slot(h)], qt_ref[0, slot(h), :])

    def fold(s):
        out = s[0:SUBLANES]
        for t in range(1, TQ // SUBLANES):
            out = jnp.maximum(out, s[t * SUBLANES:(t + 1) * SUBLANES])
        return out

    def pass1(j, mrun):
        j0 = pl.multiple_of(j * TQ, TQ)
        out = []
        for h in range(nh):
            s = scores(h, j0)
            s_scr[h, j] = s
            out.append(jnp.maximum(mrun[h], fold(s)))
        return tuple(out)

    mrun = lax.fori_loop(0, i, pass1, tuple(jnp.full((SUBLANES, TQ), -jnp.inf, F32) for _ in range(nh)))
    d0 = pl.multiple_of(i * TQ, TQ)
    key = lax.broadcasted_iota(jnp.int32, (TQ, TQ), 0)
    query = lax.broadcasted_iota(jnp.int32, (TQ, TQ), 1)
    ms = []
    for h in range(nh):
        s = jnp.where(key <= query, scores(h, d0), -jnp.inf)
        s_scr[h, i] = s
        ms.append(jnp.max(jnp.maximum(mrun[h], fold(s)), axis=0, keepdims=True))

    def pass2(j, acc):
        out = []
        for h in range(nh):
            p = jnp.exp2(s_scr[h, j] - ms[h]).astype(BF16)
            out.append(acc[h] + _dot(vt_ref[j, slot(h), :], p))
        return tuple(out)

    acc = lax.fori_loop(0, i + 1, pass2, tuple(jnp.zeros((SLOT, TQ), F32) for _ in range(nh)))
    outs = [a[0:HEAD_LANES] / a[ONES_LANE:ONES_LANE + 1] for a in acc]
    for jj in range(nh // 2):
        both = jnp.concatenate([outs[2 * jj], outs[2 * jj + 1]], axis=0)
        o_ref[:, jj * LANES:(jj + 1) * LANES] = both.T.astype(o_ref.dtype)


def _attention(qt, k, vt, B, T, name):
    N = k.shape[0]
    nq = T // TQ
    nh = ATTN_HEADS_PER_STEP
    heads = k.shape[1] // SLOT
    return pl.pallas_call(
        _attn_kernel,
        out_shape=jax.ShapeDtypeStruct((N, heads * HEAD_LANES), BF16),
        grid=(B, heads // nh, nq),
        in_specs=[pl.BlockSpec((1, nh * SLOT, TQ), lambda b, g, i: (b * nq + i, g, 0)),
                  pl.BlockSpec((T, nh * SLOT), lambda b, g, i: (b, g)),
                  pl.BlockSpec((nq, nh * SLOT, TQ), lambda b, g, i: (b, g, 0))],
        out_specs=pl.BlockSpec((TQ, nh * HEAD_LANES), lambda b, g, i: (b * nq + i, g)),
        scratch_shapes=[pltpu.VMEM((nh, nq, TQ, TQ), F32)],
        compiler_params=pltpu.CompilerParams(dimension_semantics=("arbitrary", "arbitrary", "arbitrary"),
                                             vmem_limit_bytes=VMEM_LIMIT_BYTES),
        name=name,
    )(qt, k, vt)


def _oproj_kernel(alpha, a_ref, m_ref, x_ref, wo_ref, g_ref, b_ref, wr_ref, br_ref,
                  x1_ref, x1t_ref, route_ref, gates_ref, cnt_ref, carry_scr):
    i = pl.program_id(0)

    @pl.when(i == 0)
    def _():
        carry_scr[...] = jnp.zeros_like(carry_scr)

    wa = a_ref.shape[1]
    mix = _dot(a_ref[...], wo_ref[:wa, :]) + _dot(m_ref[...], wo_ref[wa:, :])
    x1 = _layer_norm(alpha * x_ref[...] + mix, g_ref[...], b_ref[...])
    x1_ref[...] = x1
    _rows_to_tiles(x1t_ref, x1)

    logits = _dot(x1, wr_ref[...], precision=lax.Precision.HIGHEST) + br_ref[...]
    lane = lax.broadcasted_iota(jnp.int32, logits.shape, 1)
    lane_f = lane.astype(F32)
    vals, idxs = [], []
    work = logits
    for _ in range(MOE_TOPK):
        mx = jnp.max(work, axis=-1, keepdims=True)
        ix = jnp.min(jnp.where(work == mx, lane_f, float(LANES)), axis=-1, keepdims=True).astype(jnp.int32)
        vals.append(mx)
        idxs.append(ix)
        work = jnp.where(lane == ix, -jnp.inf, work)
    exps = [jnp.exp(v - vals[0]) for v in vals]
    den = exps[0]
    for e in exps[1:]:
        den = den + e

    onehot = jnp.zeros(logits.shape, F32)
    for ix in idxs:
        onehot = onehot + (lane == ix).astype(F32)
    rows = logits.shape[0]
    r = lax.broadcasted_iota(jnp.int32, (rows, rows), 0)
    cidx = lax.broadcasted_iota(jnp.int32, (rows, rows), 1)
    lower = (cidx < r).astype(BF16)
    carry = carry_scr[0:1, :]
    before = _dot(lower, onehot.astype(BF16)) + carry
    route = jnp.zeros(logits.shape, jnp.int32)
    gates = jnp.zeros(logits.shape, F32)
    for k in range(MOE_TOPK):
        rank = jnp.sum(jnp.where(lane == idxs[k], before, 0.0), axis=-1, keepdims=True).astype(jnp.int32)
        route = jnp.where(lane == k, idxs[k], route)
        route = jnp.where(lane == MOE_TOPK + k, rank, route)
        gates = jnp.where(lane == k, exps[k] / den, gates)
    route_ref[...] = route
    gates_ref[...] = gates
    new_carry = carry + jnp.sum(onehot, axis=0, keepdims=True)
    carry_scr[...] = jnp.broadcast_to(new_carry, carry_scr.shape)
    cnt_ref[...] = jnp.broadcast_to(new_carry, cnt_ref.shape)


def _oproj(a, m, x2, wo, g1, b1, wr, br, alpha):
    N, D = x2.shape
    nt = N // ROW_TILE
    row = lambda i: (i, 0)
    full = lambda i: (0, 0)
    return pl.pallas_call(
        functools.partial(_oproj_kernel, alpha),
        out_shape=(jax.ShapeDtypeStruct((N, D), F32),
                   jax.ShapeDtypeStruct((N * SUBLANES, LANES), F32),
                   jax.ShapeDtypeStruct((N, LANES), jnp.int32),
                   jax.ShapeDtypeStruct((N, LANES), F32),
                   jax.ShapeDtypeStruct((SUBLANES, LANES), F32)),
        grid=(nt,),
        in_specs=[pl.BlockSpec((ROW_TILE, a.shape[1]), row), pl.BlockSpec((ROW_TILE, m.shape[1]), row),
                  pl.BlockSpec((ROW_TILE, D), row), pl.BlockSpec(wo.shape, full),
                  pl.BlockSpec(g1.shape, full), pl.BlockSpec(b1.shape, full),
                  pl.BlockSpec(wr.shape, full), pl.BlockSpec(br.shape, full)],
        out_specs=(pl.BlockSpec((ROW_TILE, D), row), pl.BlockSpec((ROW_TILE * SUBLANES, LANES), row),
                   pl.BlockSpec((ROW_TILE, LANES), row),
                   pl.BlockSpec((ROW_TILE, LANES), row), pl.BlockSpec((SUBLANES, LANES), full)),
        scratch_shapes=[pltpu.VMEM((SUBLANES, LANES), F32)],
        compiler_params=pltpu.CompilerParams(dimension_semantics=("arbitrary",),
                                             vmem_limit_bytes=VMEM_LIMIT_BYTES),
        name="oproj_router",
    )(a, m, x2, wo, g1, b1, wr, br)


def _dispatch_kernel(dest_ref, tail_ref, x1_ref, xs_ref, zero_scr, sem):
    i = pl.program_id(0)

    @pl.when(i == 0)
    def _():
        zero_scr[...] = jnp.zeros_like(zero_scr)

        def tail_copy(e):
            return pltpu.make_async_copy(zero_scr, xs_ref.at[pl.ds(pl.multiple_of(tail_ref[e] * SUBLANES, GROUP_SUB), GROUP_SUB)], sem)

        def start(e, _):
            @pl.when(tail_ref[e] >= 0)
            def _():
                tail_copy(e).start()
            return 0

        def wait(e, _):
            @pl.when(tail_ref[e] >= 0)
            def _():
                tail_copy(e).wait()
            return 0

        lax.fori_loop(0, N_EXPERTS, start, 0)
        lax.fori_loop(0, N_EXPERTS, wait, 0)

        def spare_copy(blk):
            return pltpu.make_async_copy(zero_scr, xs_ref.at[pl.ds(pl.multiple_of(blk * GROUP_SUB, GROUP_SUB), GROUP_SUB)], sem)

        def start_spare(blk, _):
            spare_copy(blk).start()
            return 0

        def wait_spare(blk, _):
            spare_copy(blk).wait()
            return 0

        nblk = xs_ref.shape[0] // GROUP_SUB
        lax.fori_loop(tail_ref[N_EXPERTS], nblk, start_spare, 0)
        lax.fori_loop(tail_ref[N_EXPERTS], nblk, wait_spare, 0)

    base = i * DISPATCH_TILE

    def row_copy(t, k):
        d = dest_ref[(base + t) * MOE_TOPK + k]
        return pltpu.make_async_copy(x1_ref.at[pl.ds(pl.multiple_of(t * SUBLANES, SUBLANES), SUBLANES)],
                                     xs_ref.at[pl.ds(pl.multiple_of(d * SUBLANES, SUBLANES), SUBLANES)], sem)

    def start_rows(t, _):
        for k in range(MOE_TOPK):
            row_copy(t, k).start()
        return 0

    lax.fori_loop(0, DISPATCH_TILE, start_rows, 0, unroll=ISSUE_UNROLL)
    for k in range(MOE_TOPK):
        pltpu.make_async_copy(x1_ref, xs_ref.at[pl.ds(0, DISPATCH_TILE * SUBLANES)], sem).wait()


def _dispatch(dest, tail, x1t, n_rows):
    return pl.pallas_call(
        _dispatch_kernel,
        out_shape=jax.ShapeDtypeStruct((n_rows * SUBLANES, LANES), F32),
        grid_spec=pltpu.PrefetchScalarGridSpec(
            num_scalar_prefetch=2,
            grid=(x1t.shape[0] // (DISPATCH_TILE * SUBLANES),),
            in_specs=[pl.BlockSpec((DISPATCH_TILE * SUBLANES, LANES), lambda i, d, t: (i, 0))],
            out_specs=pl.BlockSpec(memory_space=pl.ANY),
            scratch_shapes=[pltpu.VMEM((GROUP_SUB, LANES), F32), pltpu.SemaphoreType.DMA(())],
        ),
        compiler_params=pltpu.CompilerParams(dimension_semantics=("arbitrary",),
                                             vmem_limit_bytes=VMEM_LIMIT_BYTES),
        name="dispatch",
    )(dest, tail, x1t)


def _experts_kernel(be_ref, nused_ref, x_ref, wg_ref, bg_ref, wu_ref, bu_ref, wd_ref, bd_ref,
                    y_ref, wg_bf, wu_bf, wd_bf):
    i = pl.program_id(0)
    prev = be_ref[jnp.maximum(i - 1, 0)]
    changed = (i == 0) | (be_ref[i] != prev)
    active = i < nused_ref[0]

    @pl.when(active & changed)
    def _():
        wg_bf[...] = wg_ref[0].astype(BF16)
        wu_bf[...] = wu_ref[0].astype(BF16)
        wd_bf[...] = wd_ref[0].astype(BF16)

    @pl.when(active)
    def _():
        xb = _tiles_to_rows(x_ref, MOE_GROUP).astype(BF16)
        g = jnp.minimum(_dot(xb, wg_bf[...]) + bg_ref[0], SWIGLU_LIMIT)
        u = jnp.clip(_dot(xb, wu_bf[...]) + bu_ref[0], -SWIGLU_LIMIT, SWIGLU_LIMIT)
        h = g * (1.0 / (1.0 + jnp.exp(-SWIGLU_ALPHA * g))) * (u + 1.0)
        _rows_to_tiles(y_ref, _dot(h.astype(BF16), wd_bf[...]) + bd_ref[0])

    @pl.when(jnp.logical_not(active))
    def _():
        y_ref[...] = jnp.zeros_like(y_ref)


def _experts(blk_expert, n_used, xs, wg, bg, wu, bu, wd, bd):
    E, D, F = wg.shape
    nblk = xs.shape[0] // GROUP_SUB

    def rowmap(i, be, nu):
        return (jnp.minimum(i, nu[0] - 1), 0)

    def wmap(i, be, nu):
        return (be[i], 0, 0)

    return pl.pallas_call(
        _experts_kernel,
        out_shape=jax.ShapeDtypeStruct(xs.shape, F32),
        grid_spec=pltpu.PrefetchScalarGridSpec(
            num_scalar_prefetch=2,
            grid=(nblk,),
            in_specs=[pl.BlockSpec((GROUP_SUB, LANES), rowmap),
                      pl.BlockSpec((1, D, F), wmap), pl.BlockSpec((1, 1, F), wmap),
                      pl.BlockSpec((1, D, F), wmap), pl.BlockSpec((1, 1, F), wmap),
                      pl.BlockSpec((1, F, D), wmap), pl.BlockSpec((1, 1, D), wmap)],
            out_specs=pl.BlockSpec((GROUP_SUB, LANES), lambda i, be, nu: (i, 0)),
            scratch_shapes=[pltpu.VMEM((D, F), BF16), pltpu.VMEM((D, F), BF16), pltpu.VMEM((F, D), BF16)],
        ),
        compiler_params=pltpu.CompilerParams(dimension_semantics=("arbitrary",),
                                             vmem_limit_bytes=VMEM_LIMIT_BYTES),
        name="experts",
    )(blk_expert, n_used, xs, wg, bg, wu, bu, wd, bd)


def _combine_kernel(alpha, dest_ref, gates_ref, x1_ref, g_ref, b_ref, y_ref, o_ref, ybuf, sems):
    i = pl.program_id(0)
    buf = i % 2

    def gather_tile(tile, into):
        base = tile * COMBINE_TILE

        def start_rows(t, _):
            for k in range(MOE_TOPK):
                d = dest_ref[(base + t) * MOE_TOPK + k]
                pltpu.make_async_copy(y_ref.at[pl.ds(pl.multiple_of(d * SUBLANES, SUBLANES), SUBLANES)],
                                      ybuf.at[into, k, pl.ds(pl.multiple_of(t * SUBLANES, SUBLANES), SUBLANES)],
                                      sems.at[into]).start()
            return 0

        lax.fori_loop(0, COMBINE_TILE, start_rows, 0, unroll=ISSUE_UNROLL)

    @pl.when(i == 0)
    def _():
        gather_tile(0, 0)

    @pl.when(i + 1 < pl.num_programs(0))
    def _():
        gather_tile(i + 1, 1 - buf)

    for k in range(MOE_TOPK):
        pltpu.make_async_copy(y_ref.at[pl.ds(0, COMBINE_TILE * SUBLANES)], ybuf.at[buf, k], sems.at[buf]).wait()

    gates = gates_ref[...]
    ffn = gates[:, 0:1] * _tiles_to_rows(ybuf.at[buf, 0], COMBINE_TILE)
    for k in range(1, MOE_TOPK):
        ffn = ffn + gates[:, k:k + 1] * _tiles_to_rows(ybuf.at[buf, k], COMBINE_TILE)
    o_ref[...] = _layer_norm(alpha * x1_ref[...] + ffn, g_ref[...], b_ref[...])


def _combine(dest, gates, x1, g2, b2, y_rows, alpha):
    N, D = x1.shape
    row = lambda i, d: (i, 0)
    full = lambda i, d: (0, 0)
    return pl.pallas_call(
        functools.partial(_combine_kernel, alpha),
        out_shape=jax.ShapeDtypeStruct((N, D), F32),
        grid_spec=pltpu.PrefetchScalarGridSpec(
            num_scalar_prefetch=1,
            grid=(N // COMBINE_TILE,),
            in_specs=[pl.BlockSpec((COMBINE_TILE, LANES), row), pl.BlockSpec((COMBINE_TILE, D), row),
                      pl.BlockSpec(g2.shape, full), pl.BlockSpec(b2.shape, full),
                      pl.BlockSpec(memory_space=pl.ANY)],
            out_specs=pl.BlockSpec((COMBINE_TILE, D), row),
            scratch_shapes=[pltpu.VMEM((2, MOE_TOPK, COMBINE_TILE * SUBLANES, LANES), F32),
                            pltpu.SemaphoreType.DMA((2,))],
        ),
        compiler_params=pltpu.CompilerParams(dimension_semantics=("arbitrary",),
                                             vmem_limit_bytes=VMEM_LIMIT_BYTES),
        name="combine",
    )(dest, gates, x1, g2, b2, y_rows)


def _rot_partner(w, half):
    return jnp.concatenate([-w[..., half:2 * half], w[..., :half]], axis=-1)


def _layer_weights(w_in, w_q_b, w_kv_b):
    D = w_in.shape[0]
    o1 = Q_LORA
    o2 = o1 + KV_LORA
    o3 = o2 + MLA_ROPE
    w_ql, w_kvl, w_kr = w_in[:, :o1], w_in[:, o1:o2], w_in[:, o2:o3]
    w_mq, w_mk, w_mv = w_in[:, o3:o3 + _MH], w_in[:, o3 + _MH:o3 + 2 * _MH], w_in[:, o3 + 2 * _MH:]
    zpad = lambda n: jnp.zeros((D, n), w_in.dtype)
    tail = SLOT - MLA_NOPE - MLA_ROPE
    kr_slot = jnp.concatenate([zpad(MLA_NOPE), w_kr, zpad(tail)], axis=1)
    krs_slot = jnp.concatenate([zpad(MLA_NOPE), _rot_partner(w_kr, MLA_ROPE // 2), zpad(tail)], axis=1)

    def moba_partner(w):
        w3 = w.reshape(D, MOBA_HEADS, MOBA_HD)
        part = jnp.concatenate([_rot_partner(w3[..., :MOBA_ROT], MOBA_ROT // 2),
                                jnp.zeros((D, MOBA_HEADS, MOBA_HD - MOBA_ROT), w.dtype)], axis=-1)
        return part.reshape(D, _MH)

    win = jnp.concatenate([w_ql, w_kvl, kr_slot, krs_slot, w_mk, moba_partner(w_mk)], axis=1).astype(BF16)
    wmt = jnp.concatenate([w_mq, moba_partner(w_mq), w_mv], axis=1).T.astype(BF16)

    wq3 = w_q_b.reshape(Q_LORA, MLA_HEADS, MLA_NOPE + MLA_ROPE)
    zq = jnp.zeros((Q_LORA, MLA_HEADS, tail), w_q_b.dtype)
    wq = jnp.concatenate([wq3, zq], axis=-1).reshape(Q_LORA, MLA_HEADS * SLOT)
    wqs = jnp.concatenate([jnp.zeros((Q_LORA, MLA_HEADS, MLA_NOPE), w_q_b.dtype),
                           _rot_partner(wq3[..., MLA_NOPE:], MLA_ROPE // 2), zq],
                          axis=-1).reshape(Q_LORA, MLA_HEADS * SLOT)
    wkv3 = w_kv_b.reshape(KV_LORA, MLA_HEADS, MLA_NOPE + MLA_V)
    wk = jnp.concatenate([wkv3[..., :MLA_NOPE], jnp.zeros((KV_LORA, MLA_HEADS, SLOT - MLA_NOPE), w_kv_b.dtype)],
                         axis=-1).reshape(KV_LORA, MLA_HEADS * SLOT).astype(BF16)
    wv = jnp.concatenate([wkv3[..., MLA_NOPE:], jnp.zeros((KV_LORA, MLA_HEADS, SLOT - MLA_V), w_kv_b.dtype)],
                         axis=-1).reshape(KV_LORA, MLA_HEADS * SLOT)
    return win, wmt, wq.T.astype(BF16), wqs.T.astype(BF16), wk, wv.T.astype(BF16)


def _rope_tables(positions):
    pos = positions.astype(F32).reshape(-1, 1)
    n = pos.shape[0]

    def cs(d_rot):
        inv_freq = ROPE_THETA ** (-jnp.arange(0, d_rot, 2, dtype=F32) / d_rot)
        ang = pos * inv_freq
        return jnp.cos(ang), jnp.sin(ang)

    ca, sa = cs(MLA_ROPE)
    tail = SLOT - MLA_NOPE - MLA_ROPE
    ca_t = jnp.concatenate([jnp.ones((n, MLA_NOPE), F32), ca, ca, jnp.zeros((n, tail), F32)], axis=1)
    sa_t = jnp.concatenate([jnp.zeros((n, MLA_NOPE), F32), sa, sa, jnp.zeros((n, tail), F32)], axis=1)
    cb, sb = cs(MOBA_ROT)
    rest = MOBA_HD - MOBA_ROT
    cb_h = jnp.concatenate([cb, cb, jnp.ones((n, rest), F32)], axis=1)
    sb_h = jnp.concatenate([sb, sb, jnp.zeros((n, rest), F32)], axis=1)
    return (ca_t, sa_t, ca_t.T, sa_t.T, jnp.concatenate([cb_h, cb_h], axis=1), jnp.concatenate([sb_h, sb_h], axis=1),
            cb_h.T, sb_h.T)


def kernel(x, positions, w_in, q_a_norm, w_q_b, kv_a_norm, w_kv_b, w_o, ln1_g, ln1_b, w_router, b_router,
           w_gate, b_gate, w_up, b_up, w_down, b_down, ln2_g, ln2_b):
    B, T, D = x.shape
    depth = w_in.shape[0]
    alpha = (2.0 * depth) ** 0.25
    N = B * T
    assert T % MOBA_BLOCK == 0 and T // MOBA_BLOCK <= SUBLANES and N % ROW_TILE == 0
    assert D == SUBLANES * LANES
    n_asg = N * MOE_TOPK
    n_rows = n_asg + N_EXPERTS * MOE_GROUP
    tables = _rope_tables(positions)
    h = x.reshape(N, D)
    for l in range(depth):
        win, wmt, wqt, wqst, wk, wvt = _layer_weights(w_in[l], w_q_b[l], w_kv_b[l])
        qa, ka, va, mq, mk, mv = _prep(h, win, wmt, wqt, wqst, wk, wvt, q_a_norm[l].reshape(1, -1),
                                       kv_a_norm[l].reshape(1, -1), tables, B, T)
        a = _attention(qa, ka, va, B, T, "mla_attention")
        m = _attention(mq, mk, mv, B, T, "moba_attention")
        wr = jnp.concatenate([w_router[l], jnp.zeros((D, LANES - N_EXPERTS), F32)], axis=1)
        br = jnp.concatenate([b_router[l], jnp.full((LANES - N_EXPERTS,), -jnp.inf, F32)]).reshape(1, LANES)
        x1, x1t, route, gates, cnt = _oproj(a, m, h, w_o[l].astype(BF16), ln1_g[l].reshape(1, D), ln1_b[l].reshape(1, D),
                                       wr, br, alpha)
        counts = cnt[0, :N_EXPERTS].astype(jnp.int32)
        padded = (counts + MOE_GROUP - 1) // MOE_GROUP * MOE_GROUP
        pad_end = jnp.cumsum(padded).astype(jnp.int32)
        pad_start = pad_end - padded
        e_idx = route[:, :MOE_TOPK]
        dest = (pad_start[e_idx] + route[:, MOE_TOPK:2 * MOE_TOPK]).reshape(n_asg)
        n_used = (pad_end[-1:] // MOE_GROUP).astype(jnp.int32)
        tail = jnp.concatenate([jnp.where(padded > 0, pad_end - MOE_GROUP, -1).astype(jnp.int32), n_used])
        blk_start = jnp.arange(n_rows // MOE_GROUP, dtype=jnp.int32) * MOE_GROUP
        blk_expert = jnp.minimum(jnp.sum((blk_start[:, None] >= pad_end[None, :]).astype(jnp.int32), axis=1),
                                 N_EXPERTS - 1)
        xs = _dispatch(dest, tail, x1t, n_rows)
        y_rows = _experts(blk_expert, n_used, xs, w_gate[l], b_gate[l].reshape(N_EXPERTS, 1, -1),
                          w_up[l], b_up[l].reshape(N_EXPERTS, 1, -1), w_down[l], b_down[l].reshape(N_EXPERTS, 1, -1))
        h = _combine(dest, gates, x1, ln2_g[l].reshape(1, D), ln2_b[l].reshape(1, D), y_rows, alpha)
    return h.reshape(B, T, D)
```

```python
import functools
import math

import jax
import jax.numpy as jnp
from jax import lax
from jax.experimental import pallas as pl
from jax.experimental.pallas import tpu as pltpu

ROPE_THETA = 500000.0
MLA_HEADS = 8
MLA_NOPE = 64
MLA_ROPE = 32
MLA_V = 64
Q_LORA = 256
KV_LORA = 128
MOBA_HEADS = 8
MOBA_HD = 64
MOBA_ROT = MOBA_HD // 4
MOBA_BLOCK = 256
MOBA_TOPK = 3
N_EXPERTS = 32
MOE_TOPK = 4
SWIGLU_LIMIT = 7.0
SWIGLU_ALPHA = 1.702
RMS_EPS = 1e-6
LN_EPS = 1e-5

LANES = 128
SUBLANES = 8
VMEM_LIMIT_BYTES = 56 * 1024 * 1024

SLOT = LANES
TQ = MOBA_BLOCK
ROW_TILE = 256
MOE_GROUP = 256
GROUP_SUB = MOE_GROUP * SUBLANES
DISPATCH_TILE = 256
COMBINE_TILE = 128
ISSUE_UNROLL = 4
DMA_PRIORITIES = 2
HEAD_LANES = 64
ONES_LANE = 64
BIAS_LANE = 64
ATTN_HEADS_PER_STEP = 8
LOG2E = math.log2(math.e)
NEG_BIG = -(2.0 ** 100)

F32 = jnp.float32
BF16 = jnp.bfloat16
NT_DIMS = (((1,), (1,)), ((), ()))


def _dot(a, b, precision=None):
    return jnp.dot(a, b, preferred_element_type=F32, precision=precision)


def _dot_nt(a, b, precision=None):
    return lax.dot_general(a, b, NT_DIMS, preferred_element_type=F32, precision=precision)


def _rows_to_tiles(ref, x):
    rows = x.shape[0]
    for c in range(SUBLANES):
        ref[pl.ds(c, rows, stride=SUBLANES), :] = x[:, c * LANES:(c + 1) * LANES]


def _tiles_to_rows(ref, rows):
    return jnp.concatenate([ref[pl.ds(c, rows, stride=SUBLANES), :] for c in range(SUBLANES)], axis=1)


def _rms(x, g):
    return x * lax.rsqrt(jnp.mean(x * x, axis=-1, keepdims=True) + RMS_EPS) * g


def _layer_norm(x, g, b):
    mu = jnp.mean(x, axis=-1, keepdims=True)
    xc = x - mu
    var = jnp.mean(xc * xc, axis=-1, keepdims=True)
    return xc * lax.rsqrt(var + LN_EPS) * g + b


_C_QL = 0
_C_KVL = _C_QL + Q_LORA
_C_KR = _C_KVL + KV_LORA
_C_KRS = _C_KR + SLOT
_C_MK = _C_KRS + SLOT
_MH = MOBA_HEADS * MOBA_HD
_C_MKS = _C_MK + _MH
_C_END = _C_MKS + _MH


def _prep_kernel(x_ref, win_ref, wmt_ref, wqt_ref, wqst_ref, wk_ref, wvt_ref, qg_ref, kvg_ref,
                 ca_ref, sa_ref, cat_ref, sat_ref, cb_ref, sb_ref, cbt_ref, sbt_ref,
                 qat_ref, ka_ref, vat_ref, mqt_ref, mk_ref, mvt_ref, kmean_scr):
    c = pl.program_id(1)
    xb = x_ref[...].astype(BF16)
    ca = ca_ref[...]
    sa = sa_ref[...]
    lane = lax.broadcasted_iota(jnp.int32, (ROW_TILE, SLOT), 1)
    head_lanes = lane < HEAD_LANES
    ones_rows = (lax.broadcasted_iota(jnp.int32, (SLOT - HEAD_LANES, ROW_TILE), 0) == ONES_LANE - HEAD_LANES).astype(F32)

    ql = _dot(xb, win_ref[:, _C_QL:_C_KVL])
    kvl = _dot(xb, win_ref[:, _C_KVL:_C_KR])
    kr = _dot(xb, win_ref[:, _C_KR:_C_KRS])
    krs = _dot(xb, win_ref[:, _C_KRS:_C_MK])
    qn = _rms(ql, qg_ref[...]).astype(BF16)
    kvn = _rms(kvl, kvg_ref[...]).astype(BF16)
    q_t = _dot_nt(wqt_ref[...], qn)
    qs_t = _dot_nt(wqst_ref[...], qn)
    kn = _dot(kvn, wk_ref[...])
    v_t = _dot_nt(wvt_ref[...], kvn)
    cat = cat_ref[...]
    sat = sat_ref[...]
    scale_a = LOG2E / math.sqrt(MLA_NOPE + MLA_ROPE)
    kro = kr * ca + krs * sa
    for h in range(MLA_HEADS):
        sl = slice(h * SLOT, (h + 1) * SLOT)
        qat_ref[0, sl, :] = ((q_t[sl] * cat + qs_t[sl] * sat) * scale_a).astype(BF16)
        ka_ref[:, sl] = (kn[:, sl] + kro).astype(BF16)
        vat_ref[0, sl, :] = jnp.concatenate([v_t[h * SLOT:h * SLOT + HEAD_LANES], ones_rows], axis=0).astype(BF16)

    mk = _dot(xb, win_ref[:, _C_MK:_C_MKS])
    mks = _dot(xb, win_ref[:, _C_MKS:_C_END])
    cb = cb_ref[...]
    sb = sb_ref[...]
    npair = _MH // LANES
    pair = lambda a, j: a[:, j * LANES:(j + 1) * LANES]
    mk_rot = [pair(mk, j) * cb + pair(mks, j) * sb for j in range(npair)]
    mk_all = jnp.concatenate(mk_rot, axis=1)

    mq_t = _dot_nt(wmt_ref[0:_MH, :], xb)
    mqs_t = _dot_nt(wmt_ref[_MH:2 * _MH, :], xb)
    mv_t = _dot_nt(wmt_ref[2 * _MH:3 * _MH, :], xb)
    cbt = cbt_ref[...]
    sbt = sbt_ref[...]
    head = lambda a, h: a[h * MOBA_HD:(h + 1) * MOBA_HD]
    mq_rot_t = [head(mq_t, h) * cbt + head(mqs_t, h) * sbt for h in range(MOBA_HEADS)]

    nrow = MOBA_HEADS * SUBLANES
    @pl.when(c == 0)
    def _():
        kmean_scr[...] = jnp.zeros_like(kmean_scr)

    row_i = lax.broadcasted_iota(jnp.int32, (nrow, _MH), 0)
    lane_i = lax.broadcasted_iota(jnp.int32, (nrow, _MH), 1)
    kmean_c = jnp.mean(mk_all, axis=0, keepdims=True)
    put = ((row_i % SUBLANES) == c) & ((lane_i // MOBA_HD) == (row_i // SUBLANES))
    table = kmean_scr[...]
    gate_t = _dot(table, jnp.concatenate(mq_rot_t, axis=0), precision=lax.Precision.HIGHEST)
    kmean_scr[...] = jnp.where(put, jnp.broadcast_to(kmean_c, (nrow, _MH)), table)

    n_idx = lax.broadcasted_iota(jnp.int32, (SUBLANES, ROW_TILE), 0)
    valid = n_idx < c
    scale_b = LOG2E / math.sqrt(MOBA_HD)
    zero_rows = jnp.zeros((SLOT - HEAD_LANES - SUBLANES, ROW_TILE), F32)
    block_onehot = (lane == BIAS_LANE + c).astype(F32)
    for h in range(MOBA_HEADS):
        g = jnp.where(valid, gate_t[h * SUBLANES:(h + 1) * SUBLANES, :], -jnp.inf)
        rank = jnp.zeros((SUBLANES, ROW_TILE), jnp.int32)
        for k in range(1, SUBLANES):
            other = pltpu.roll(g, k, axis=0)
            other_n = pltpu.roll(n_idx, k, axis=0)
            beats = (other > g) | ((other == g) & (other_n < n_idx))
            rank = rank + beats.astype(jnp.int32)
        keep = (valid & (rank < MOBA_TOPK)) | (n_idx == c)
        bias = jnp.where(keep, 0.0, NEG_BIG)
        sl = slice(h * SLOT, (h + 1) * SLOT)
        mqt_ref[0, sl, :] = jnp.concatenate([mq_rot_t[h] * scale_b, bias, zero_rows], axis=0).astype(BF16)
        mvt_ref[0, sl, :] = jnp.concatenate([head(mv_t, h), ones_rows], axis=0).astype(BF16)
        j, hh = divmod(h, 2)
        k_h = mk_rot[j] if hh == 0 else pltpu.roll(mk_rot[j], HEAD_LANES, axis=1)
        mk_ref[:, sl] = jnp.where(head_lanes, k_h, block_onehot).astype(BF16)


def _prep(x2, win, wmt, wqt, wqst, wk, wvt, qg, kvg, tables, B, T):
    N, D = x2.shape
    nt = T // ROW_TILE
    row = lambda b, c: (b * nt + c, 0)
    col = lambda b, c: (0, b * nt + c)
    full = lambda b, c: (0, 0)
    ca, sa, cat, sat, cb, sb, cbt, sbt = tables
    width = MLA_HEADS * SLOT

    def rows(w):
        return pl.BlockSpec((ROW_TILE, w), row)

    def cols(a):
        return pl.BlockSpec((a.shape[0], ROW_TILE), col)

    def whole(a):
        return pl.BlockSpec(a.shape, full)

    rowmajor = jax.ShapeDtypeStruct((N, width), BF16)
    transposed = jax.ShapeDtypeStruct((B * nt, width, ROW_TILE), BF16)
    t_spec = pl.BlockSpec((1, width, ROW_TILE), lambda b, c: (b * nt + c, 0, 0))
    return pl.pallas_call(
        _prep_kernel,
        out_shape=(transposed, rowmajor, transposed, transposed, rowmajor, transposed),
        grid=(B, nt),
        in_specs=[rows(D), whole(win), whole(wmt), whole(wqt), whole(wqst), whole(wk), whole(wvt), whole(qg), whole(kvg),
                  rows(LANES), rows(LANES), cols(cat), cols(sat), rows(LANES), rows(LANES), cols(cbt), cols(sbt)],
        out_specs=(t_spec, rows(width), t_spec, t_spec, rows(width), t_spec),
        scratch_shapes=[pltpu.VMEM((MOBA_HEADS * SUBLANES, _MH), F32)],
        compiler_params=pltpu.CompilerParams(dimension_semantics=("arbitrary", "arbitrary"),
                                             vmem_limit_bytes=VMEM_LIMIT_BYTES),
        name="prep",
    )(x2, win, wmt, wqt, wqst, wk, wvt, qg, kvg, ca, sa, cat, sat, cb, sb, cbt, sbt)


def _attn_kernel(qt_ref, k_ref, vt_ref, o_ref, s_scr):
    i = pl.program_id(2)
    nh = ATTN_HEADS_PER_STEP
    slot = lambda h: slice(h * SLOT, (h + 1) * SLOT)

    def scores(h, j0):
        return _dot(k_ref[pl.ds(j0, TQ), slot(h)], qt_ref[0, slot(h), :])

    def fold(s):
        out = s[0:SUBLANES]
        for t in range(1, TQ // SUBLANES):
            out = jnp.maximum(out, s[t * SUBLANES:(t + 1) * SUBLANES])
        return out

    def pass1(j, mrun):
        j0 = pl.multiple_of(j * TQ, TQ)
        out = []
        for h in range(nh):
            s = scores(h, j0)
            s_scr[h, j] = s
            out.append(jnp.maximum(mrun[h], fold(s)))
        return tuple(out)

    mrun = lax.fori_loop(0, i, pass1, tuple(jnp.full((SUBLANES, TQ), -jnp.inf, F32) for _ in range(nh)))
    d0 = pl.multiple_of(i * TQ, TQ)
    key = lax.broadcasted_iota(jnp.int32, (TQ, TQ), 0)
    query = lax.broadcasted_iota(jnp.int32, (TQ, TQ), 1)
    ms = []
    for h in range(nh):
        s = jnp.where(key <= query, scores(h, d0), -jnp.inf)
        s_scr[h, i] = s
        ms.append(jnp.max(jnp.maximum(mrun[h], fold(s)), axis=0, keepdims=True))

    def pass2(j, acc):
        out = []
        for h in range(nh):
            p = jnp.exp2(s_scr[h, j] - ms[h]).astype(BF16)
            out.append(acc[h] + _dot(vt_ref[j, slot(h), :], p))
        return tuple(out)

    acc = lax.fori_loop(0, i + 1, pass2, tuple(jnp.zeros((SLOT, TQ), F32) for _ in range(nh)))
    outs = [a[0:HEAD_LANES] / a[ONES_LANE:ONES_LANE + 1] for a in acc]
    for jj in range(nh // 2):
        both = jnp.concatenate([outs[2 * jj], outs[2 * jj + 1]], axis=0)
        o_ref[:, jj * LANES:(jj + 1) * LANES] = both.T.astype(o_ref.dtype)


def _attention(qt, k, vt, B, T, name):
    N = k.shape[0]
    nq = T // TQ
    nh = ATTN_HEADS_PER_STEP
    heads = k.shape[1] // SLOT
    return pl.pallas_call(
        _attn_kernel,
        out_shape=jax.ShapeDtypeStruct((N, heads * HEAD_LANES), BF16),
        grid=(B, heads // nh, nq),
        in_specs=[pl.BlockSpec((1, nh * SLOT, TQ), lambda b, g, i: (b * nq + i, g, 0)),
                  pl.BlockSpec((T, nh * SLOT), lambda b, g, i: (b, g)),
                  pl.BlockSpec((nq, nh * SLOT, TQ), lambda b, g, i: (b, g, 0))],
        out_specs=pl.BlockSpec((TQ, nh * HEAD_LANES), lambda b, g, i: (b * nq + i, g)),
        scratch_shapes=[pltpu.VMEM((nh, nq, TQ, TQ), F32)],
        compiler_params=pltpu.CompilerParams(dimension_semantics=("arbitrary", "arbitrary", "arbitrary"),
                                             vmem_limit_bytes=VMEM_LIMIT_BYTES),
        name=name,
    )(qt, k, vt)


def _oproj_kernel(alpha, a_ref, m_ref, x_ref, wo_ref, g_ref, b_ref, wr_ref, br_ref,
                  x1_ref, x1t_ref, route_ref, gates_ref, cnt_ref, carry_scr):
    i = pl.program_id(0)

    @pl.when(i == 0)
    def _():
        carry_scr[...] = jnp.zeros_like(carry_scr)

    wa = a_ref.shape[1]
    mix = _dot(a_ref[...], wo_ref[:wa, :]) + _dot(m_ref[...], wo_ref[wa:, :])
    x1 = _layer_norm(alpha * x_ref[...] + mix, g_ref[...], b_ref[...])
    x1_ref[...] = x1
    _rows_to_tiles(x1t_ref, x1)

    logits = _dot(x1, wr_ref[...], precision=lax.Precision.HIGHEST) + br_ref[...]
    lane = lax.broadcasted_iota(jnp.int32, logits.shape, 1)
    lane_f = lane.astype(F32)
    vals, idxs = [], []
    work = logits
    for _ in range(MOE_TOPK):
        mx = jnp.max(work, axis=-1, keepdims=True)
        ix = jnp.min(jnp.where(work == mx, lane_f, float(LANES)), axis=-1, keepdims=True).astype(jnp.int32)
        vals.append(mx)
        idxs.append(ix)
        work = jnp.where(lane == ix, -jnp.inf, work)
    exps = [jnp.exp(v - vals[0]) for v in vals]
    den = exps[0]
    for e in exps[1:]:
        den = den + e

    onehot = jnp.zeros(logits.shape, F32)
    for ix in idxs:
        onehot = onehot + (lane == ix).astype(F32)
    rows = logits.shape[0]
    r = lax.broadcasted_iota(jnp.int32, (rows, rows), 0)
    cidx = lax.broadcasted_iota(jnp.int32, (rows, rows), 1)
    lower = (cidx < r).astype(BF16)
    carry = carry_scr[0:1, :]
    before = _dot(lower, onehot.astype(BF16)) + carry
    route = jnp.zeros(logits.shape, jnp.int32)
    gates = jnp.zeros(logits.shape, F32)
    for k in range(MOE_TOPK):
        rank = jnp.sum(jnp.where(lane == idxs[k], before, 0.0), axis=-1, keepdims=True).astype(jnp.int32)
        route = jnp.where(lane == k, idxs[k], route)
        route = jnp.where(lane == MOE_TOPK + k, rank, route)
        gates = jnp.where(lane == k, exps[k] / den, gates)
    route_ref[...] = route
    gates_ref[...] = gates
    new_carry = carry + jnp.sum(onehot, axis=0, keepdims=True)
    carry_scr[...] = jnp.broadcast_to(new_carry, carry_scr.shape)
    cnt_ref[...] = jnp.broadcast_to(new_carry, cnt_ref.shape)


def _oproj(a, m, x2, wo, g1, b1, wr, br, alpha):
    N, D = x2.shape
    nt = N // ROW_TILE
    row = lambda i: (i, 0)
    full = lambda i: (0, 0)
    return pl.pallas_call(
        functools.partial(_oproj_kernel, alpha),
        out_shape=(jax.ShapeDtypeStruct((N, D), F32),
                   jax.ShapeDtypeStruct((N * SUBLANES, LANES), F32),
                   jax.ShapeDtypeStruct((N, LANES), jnp.int32),
                   jax.ShapeDtypeStruct((N, LANES), F32),
                   jax.ShapeDtypeStruct((SUBLANES, LANES), F32)),
        grid=(nt,),
        in_specs=[pl.BlockSpec((ROW_TILE, a.shape[1]), row), pl.BlockSpec((ROW_TILE, m.shape[1]), row),
                  pl.BlockSpec((ROW_TILE, D), row), pl.BlockSpec(wo.shape, full),
                  pl.BlockSpec(g1.shape, full), pl.BlockSpec(b1.shape, full),
                  pl.BlockSpec(wr.shape, full), pl.BlockSpec(br.shape, full)],
        out_specs=(pl.BlockSpec((ROW_TILE, D), row), pl.BlockSpec((ROW_TILE * SUBLANES, LANES), row),
                   pl.BlockSpec((ROW_TILE, LANES), row),
                   pl.BlockSpec((ROW_TILE, LANES), row), pl.BlockSpec((SUBLANES, LANES), full)),
        scratch_shapes=[pltpu.VMEM((SUBLANES, LANES), F32)],
        compiler_params=pltpu.CompilerParams(dimension_semantics=("arbitrary",),
                                             vmem_limit_bytes=VMEM_LIMIT_BYTES),
        name="oproj_router",
    )(a, m, x2, wo, g1, b1, wr, br)


def _dispatch_kernel(dest_ref, tail_ref, x1_ref, xs_ref, zero_scr, sem):
    i = pl.program_id(0)

    @pl.when(i == 0)
    def _():
        zero_scr[...] = jnp.zeros_like(zero_scr)

        def tail_copy(e):
            return pltpu.make_async_copy(zero_scr, xs_ref.at[pl.ds(pl.multiple_of(tail_ref[e] * SUBLANES, GROUP_SUB), GROUP_SUB)], sem)

        def start(e, _):
            @pl.when(tail_ref[e] >= 0)
            def _():
                tail_copy(e).start()
            return 0

        def wait(e, _):
            @pl.when(tail_ref[e] >= 0)
            def _():
                tail_copy(e).wait()
            return 0

        lax.fori_loop(0, N_EXPERTS, start, 0)
        lax.fori_loop(0, N_EXPERTS, wait, 0)

        def spare_copy(blk):
            return pltpu.make_async_copy(zero_scr, xs_ref.at[pl.ds(pl.multiple_of(blk * GROUP_SUB, GROUP_SUB), GROUP_SUB)], sem)

        def start_spare(blk, _):
            spare_copy(blk).start()
            return 0

        def wait_spare(blk, _):
            spare_copy(blk).wait()
            return 0

        nblk = xs_ref.shape[0] // GROUP_SUB
        lax.fori_loop(tail_ref[N_EXPERTS], nblk, start_spare, 0)
        lax.fori_loop(tail_ref[N_EXPERTS], nblk, wait_spare, 0)

    base = i * DISPATCH_TILE

    def row_copy(t, k):
        d = dest_ref[(base + t) * MOE_TOPK + k]
        return pltpu.make_async_copy(x1_ref.at[pl.ds(pl.multiple_of(t * SUBLANES, SUBLANES), SUBLANES)],
                                     xs_ref.at[pl.ds(pl.multiple_of(d * SUBLANES, SUBLANES), SUBLANES)], sem)

    def start_rows(t, _):
        for k in range(MOE_TOPK):
            row_copy(t, k).start(priority=k % DMA_PRIORITIES)
        return 0

    lax.fori_loop(0, DISPATCH_TILE, start_rows, 0, unroll=ISSUE_UNROLL)
    for k in range(MOE_TOPK):
        pltpu.make_async_copy(x1_ref, xs_ref.at[pl.ds(0, DISPATCH_TILE * SUBLANES)], sem).wait()


def _dispatch(dest, tail, x1t, n_rows):
    return pl.pallas_call(
        _dispatch_kernel,
        out_shape=jax.ShapeDtypeStruct((n_rows * SUBLANES, LANES), F32),
        grid_spec=pltpu.PrefetchScalarGridSpec(
            num_scalar_prefetch=2,
            grid=(x1t.shape[0] // (DISPATCH_TILE * SUBLANES),),
            in_specs=[pl.BlockSpec((DISPATCH_TILE * SUBLANES, LANES), lambda i, d, t: (i, 0))],
            out_specs=pl.BlockSpec(memory_space=pl.ANY),
            scratch_shapes=[pltpu.VMEM((GROUP_SUB, LANES), F32), pltpu.SemaphoreType.DMA(())],
        ),
        compiler_params=pltpu.CompilerParams(dimension_semantics=("arbitrary",),
                                             vmem_limit_bytes=VMEM_LIMIT_BYTES),
        name="dispatch",
    )(dest, tail, x1t)


def _experts_kernel(be_ref, slot_ref, nxt_ref, nused_ref, x_ref, wg_hbm, bg_ref, wu_hbm, bu_ref, wd_hbm, bd_ref,
                    y_ref, wg_st, wu_st, wd_st, wg_bf, wu_bf, wd_bf, sems):
    i = pl.program_id(0)
    prev = be_ref[jnp.maximum(i - 1, 0)]
    changed = (i == 0) | (be_ref[i] != prev)
    active = i < nused_ref[0]
    slot = slot_ref[i]

    def weight_copies(expert, s):
        return (pltpu.make_async_copy(wg_hbm.at[expert], wg_st.at[s], sems.at[s, 0]),
                pltpu.make_async_copy(wu_hbm.at[expert], wu_st.at[s], sems.at[s, 1]),
                pltpu.make_async_copy(wd_hbm.at[expert], wd_st.at[s], sems.at[s, 2]))

    @pl.when(i == 0)
    def _():
        for cp in weight_copies(be_ref[0], slot):
            cp.start()

    @pl.when(active & changed)
    def _():
        for cp in weight_copies(be_ref[i], slot):
            cp.wait()

        @pl.when(nxt_ref[i] >= 0)
        def _():
            for cp in weight_copies(nxt_ref[i], 1 - slot):
                cp.start()

        wg_bf[...] = wg_st[slot].astype(BF16)
        wu_bf[...] = wu_st[slot].astype(BF16)
        wd_bf[...] = wd_st[slot].astype(BF16)

    @pl.when(active)
    def _():
        xb = _tiles_to_rows(x_ref, MOE_GROUP).astype(BF16)
        g = jnp.minimum(_dot(xb, wg_bf[...]) + bg_ref[0], SWIGLU_LIMIT)
        u = jnp.clip(_dot(xb, wu_bf[...]) + bu_ref[0], -SWIGLU_LIMIT, SWIGLU_LIMIT)
        h = g * (1.0 / (1.0 + jnp.exp(-SWIGLU_ALPHA * g))) * (u + 1.0)
        _rows_to_tiles(y_ref, _dot(h.astype(BF16), wd_bf[...]) + bd_ref[0])

    @pl.when(jnp.logical_not(active))
    def _():
        y_ref[...] = jnp.zeros_like(y_ref)


def _experts(blk_expert, blk_slot, blk_next, n_used, xs, wg, bg, wu, bu, wd, bd):
    E, D, F = wg.shape
    nblk = xs.shape[0] // GROUP_SUB

    def rowmap(i, be, sl, nx, nu):
        return (jnp.minimum(i, nu[0] - 1), 0)

    def bmap(i, be, sl, nx, nu):
        return (be[i], 0, 0)

    hbm = pl.BlockSpec(memory_space=pl.ANY)
    return pl.pallas_call(
        _experts_kernel,
        out_shape=jax.ShapeDtypeStruct(xs.shape, F32),
        grid_spec=pltpu.PrefetchScalarGridSpec(
            num_scalar_prefetch=4,
            grid=(nblk,),
            in_specs=[pl.BlockSpec((GROUP_SUB, LANES), rowmap),
                      hbm, pl.BlockSpec((1, 1, F), bmap),
                      hbm, pl.BlockSpec((1, 1, F), bmap),
                      hbm, pl.BlockSpec((1, 1, D), bmap)],
            out_specs=pl.BlockSpec((GROUP_SUB, LANES), lambda i, be, sl, nx, nu: (i, 0)),
            scratch_shapes=[pltpu.VMEM((2, D, F), F32), pltpu.VMEM((2, D, F), F32), pltpu.VMEM((2, F, D), F32),
                            pltpu.VMEM((D, F), BF16), pltpu.VMEM((D, F), BF16), pltpu.VMEM((F, D), BF16),
                            pltpu.SemaphoreType.DMA((2, 3))],
        ),
        compiler_params=pltpu.CompilerParams(dimension_semantics=("arbitrary",),
                                             vmem_limit_bytes=VMEM_LIMIT_BYTES),
        name="experts",
    )(blk_expert, blk_slot, blk_next, n_used, xs, wg, bg, wu, bu, wd, bd)


def _combine_kernel(alpha, dest_ref, gates_ref, x1_ref, g_ref, b_ref, y_ref, o_ref, ybuf, sems):
    i = pl.program_id(0)
    buf = i % 2

    def gather_tile(tile, into):
        base = tile * COMBINE_TILE

        def start_rows(t, _):
            for k in range(MOE_TOPK):
                d = dest_ref[(base + t) * MOE_TOPK + k]
                pltpu.make_async_copy(y_ref.at[pl.ds(pl.multiple_of(d * SUBLANES, SUBLANES), SUBLANES)],
                                      ybuf.at[into, k, pl.ds(pl.multiple_of(t * SUBLANES, SUBLANES), SUBLANES)],
                                      sems.at[into]).start(priority=k % DMA_PRIORITIES)
            return 0

        lax.fori_loop(0, COMBINE_TILE, start_rows, 0, unroll=ISSUE_UNROLL)

    @pl.when(i == 0)
    def _():
        gather_tile(0, 0)

    @pl.when(i + 1 < pl.num_programs(0))
    def _():
        gather_tile(i + 1, 1 - buf)

    for k in range(MOE_TOPK):
        pltpu.make_async_copy(y_ref.at[pl.ds(0, COMBINE_TILE * SUBLANES)], ybuf.at[buf, k], sems.at[buf]).wait()

    gates = gates_ref[...]
    ffn = gates[:, 0:1] * _tiles_to_rows(ybuf.at[buf, 0], COMBINE_TILE)
    for k in range(1, MOE_TOPK):
        ffn = ffn + gates[:, k:k + 1] * _tiles_to_rows(ybuf.at[buf, k], COMBINE_TILE)
    o_ref[...] = _layer_norm(alpha * x1_ref[...] + ffn, g_ref[...], b_ref[...])


def _combine(dest, gates, x1, g2, b2, y_rows, alpha):
    N, D = x1.shape
    row = lambda i, d: (i, 0)
    full = lambda i, d: (0, 0)
    return pl.pallas_call(
        functools.partial(_combine_kernel, alpha),
        out_shape=jax.ShapeDtypeStruct((N, D), F32),
        grid_spec=pltpu.PrefetchScalarGridSpec(
            num_scalar_prefetch=1,
            grid=(N // COMBINE_TILE,),
            in_specs=[pl.BlockSpec((COMBINE_TILE, LANES), row), pl.BlockSpec((COMBINE_TILE, D), row),
                      pl.BlockSpec(g2.shape, full), pl.BlockSpec(b2.shape, full),
                      pl.BlockSpec(memory_space=pl.ANY)],
            out_specs=pl.BlockSpec((COMBINE_TILE, D), row),
            scratch_shapes=[pltpu.VMEM((2, MOE_TOPK, COMBINE_TILE * SUBLANES, LANES), F32),
                            pltpu.SemaphoreType.DMA((2,))],
        ),
        compiler_params=pltpu.CompilerParams(dimension_semantics=("arbitrary",),
                                             vmem_limit_bytes=VMEM_LIMIT_BYTES),
        name="combine",
    )(dest, gates, x1, g2, b2, y_rows)


def _rot_partner(w, half):
    return jnp.concatenate([-w[..., half:2 * half], w[..., :half]], axis=-1)


def _layer_weights(w_in, w_q_b, w_kv_b):
    D = w_in.shape[0]
    o1 = Q_LORA
    o2 = o1 + KV_LORA
    o3 = o2 + MLA_ROPE
    w_ql, w_kvl, w_kr = w_in[:, :o1], w_in[:, o1:o2], w_in[:, o2:o3]
    w_mq, w_mk, w_mv = w_in[:, o3:o3 + _MH], w_in[:, o3 + _MH:o3 + 2 * _MH], w_in[:, o3 + 2 * _MH:]
    zpad = lambda n: jnp.zeros((D, n), w_in.dtype)
    tail = SLOT - MLA_NOPE - MLA_ROPE
    kr_slot = jnp.concatenate([zpad(MLA_NOPE), w_kr, zpad(tail)], axis=1)
    krs_slot = jnp.concatenate([zpad(MLA_NOPE), _rot_partner(w_kr, MLA_ROPE // 2), zpad(tail)], axis=1)

    def moba_partner(w):
        w3 = w.reshape(D, MOBA_HEADS, MOBA_HD)
        part = jnp.concatenate([_rot_partner(w3[..., :MOBA_ROT], MOBA_ROT // 2),
                                jnp.zeros((D, MOBA_HEADS, MOBA_HD - MOBA_ROT), w.dtype)], axis=-1)
        return part.reshape(D, _MH)

    win = jnp.concatenate([w_ql, w_kvl, kr_slot, krs_slot, w_mk, moba_partner(w_mk)], axis=1).astype(BF16)
    wmt = jnp.concatenate([w_mq, moba_partner(w_mq), w_mv], axis=1).T.astype(BF16)

    wq3 = w_q_b.reshape(Q_LORA, MLA_HEADS, MLA_NOPE + MLA_ROPE)
    zq = jnp.zeros((Q_LORA, MLA_HEADS, tail), w_q_b.dtype)
    wq = jnp.concatenate([wq3, zq], axis=-1).reshape(Q_LORA, MLA_HEADS * SLOT)
    wqs = jnp.concatenate([jnp.zeros((Q_LORA, MLA_HEADS, MLA_NOPE), w_q_b.dtype),
                           _rot_partner(wq3[..., MLA_NOPE:], MLA_ROPE // 2), zq],
                          axis=-1).reshape(Q_LORA, MLA_HEADS * SLOT)
    wkv3 = w_kv_b.reshape(KV_LORA, MLA_HEADS, MLA_NOPE + MLA_V)
    wk = jnp.concatenate([wkv3[..., :MLA_NOPE], jnp.zeros((KV_LORA, MLA_HEADS, SLOT - MLA_NOPE), w_kv_b.dtype)],
                         axis=-1).reshape(KV_LORA, MLA_HEADS * SLOT).astype(BF16)
    wv = jnp.concatenate([wkv3[..., MLA_NOPE:], jnp.zeros((KV_LORA, MLA_HEADS, SLOT - MLA_V), w_kv_b.dtype)],
                         axis=-1).reshape(KV_LORA, MLA_HEADS * SLOT)
    return win, wmt, wq.T.astype(BF16), wqs.T.astype(BF16), wk, wv.T.astype(BF16)


def _rope_tables(positions):
    pos = positions.astype(F32).reshape(-1, 1)
    n = pos.shape[0]

    def cs(d_rot):
        inv_freq = ROPE_THETA ** (-jnp.arange(0, d_rot, 2, dtype=F32) / d_rot)
        ang = pos * inv_freq
        return jnp.cos(ang), jnp.sin(ang)

    ca, sa = cs(MLA_ROPE)
    tail = SLOT - MLA_NOPE - MLA_ROPE
    ca_t = jnp.concatenate([jnp.ones((n, MLA_NOPE), F32), ca, ca, jnp.zeros((n, tail), F32)], axis=1)
    sa_t = jnp.concatenate([jnp.zeros((n, MLA_NOPE), F32), sa, sa, jnp.zeros((n, tail), F32)], axis=1)
    cb, sb = cs(MOBA_ROT)
    rest = MOBA_HD - MOBA_ROT
    cb_h = jnp.concatenate([cb, cb, jnp.ones((n, rest), F32)], axis=1)
    sb_h = jnp.concatenate([sb, sb, jnp.zeros((n, rest), F32)], axis=1)
    return (ca_t, sa_t, ca_t.T, sa_t.T, jnp.concatenate([cb_h, cb_h], axis=1), jnp.concatenate([sb_h, sb_h], axis=1),
            cb_h.T, sb_h.T)


def kernel(x, positions, w_in, q_a_norm, w_q_b, kv_a_norm, w_kv_b, w_o, ln1_g, ln1_b, w_router, b_router,
           w_gate, b_gate, w_up, b_up, w_down, b_down, ln2_g, ln2_b):
    B, T, D = x.shape
    depth = w_in.shape[0]
    alpha = (2.0 * depth) ** 0.25
    N = B * T
    assert T % MOBA_BLOCK == 0 and T // MOBA_BLOCK <= SUBLANES and N % ROW_TILE == 0
    assert D == SUBLANES * LANES
    n_asg = N * MOE_TOPK
    n_rows = n_asg + N_EXPERTS * MOE_GROUP
    tables = _rope_tables(positions)
    h = x.reshape(N, D)
    for l in range(depth):
        win, wmt, wqt, wqst, wk, wvt = _layer_weights(w_in[l], w_q_b[l], w_kv_b[l])
        qa, ka, va, mq, mk, mv = _prep(h, win, wmt, wqt, wqst, wk, wvt, q_a_norm[l].reshape(1, -1),
                                       kv_a_norm[l].reshape(1, -1), tables, B, T)
        a = _attention(qa, ka, va, B, T, "mla_attention")
        m = _attention(mq, mk, mv, B, T, "moba_attention")
        wr = jnp.concatenate([w_router[l], jnp.zeros((D, LANES - N_EXPERTS), F32)], axis=1)
        br = jnp.concatenate([b_router[l], jnp.full((LANES - N_EXPERTS,), -jnp.inf, F32)]).reshape(1, LANES)
        x1, x1t, route, gates, cnt = _oproj(a, m, h, w_o[l].astype(BF16), ln1_g[l].reshape(1, D), ln1_b[l].reshape(1, D),
                                       wr, br, alpha)
        counts = cnt[0, :N_EXPERTS].astype(jnp.int32)
        padded = (counts + MOE_GROUP - 1) // MOE_GROUP * MOE_GROUP
        pad_end = jnp.cumsum(padded).astype(jnp.int32)
        pad_start = pad_end - padded
        e_idx = route[:, :MOE_TOPK]
        dest = (pad_start[e_idx] + route[:, MOE_TOPK:2 * MOE_TOPK]).reshape(n_asg)
        n_used = (pad_end[-1:] // MOE_GROUP).astype(jnp.int32)
        tail = jnp.concatenate([jnp.where(padded > 0, pad_end - MOE_GROUP, -1).astype(jnp.int32), n_used])
        blk_start = jnp.arange(n_rows // MOE_GROUP, dtype=jnp.int32) * MOE_GROUP
        blk_expert = jnp.minimum(jnp.sum((blk_start[:, None] >= pad_end[None, :]).astype(jnp.int32), axis=1),
                                 N_EXPERTS - 1)
        nblk = n_rows // MOE_GROUP
        starts_group = jnp.concatenate([jnp.ones((1,), jnp.int32), (blk_expert[1:] != blk_expert[:-1]).astype(jnp.int32)])
        blk_slot = (jnp.cumsum(starts_group) - 1) % 2
        next_start = pad_end[blk_expert] // MOE_GROUP
        blk_next = jnp.where(next_start < n_used[0], blk_expert[jnp.minimum(next_start, nblk - 1)], -1).astype(jnp.int32)
        xs = _dispatch(dest, tail, x1t, n_rows)
        y_rows = _experts(blk_expert, blk_slot.astype(jnp.int32), blk_next, n_used, xs, w_gate[l], b_gate[l].reshape(N_EXPERTS, 1, -1),
                          w_up[l], b_up[l].reshape(N_EXPERTS, 1, -1), w_down[l], b_down[l].reshape(N_EXPERTS, 1, -1))
        h = _combine(dest, gates, x1, ln2_g[l].reshape(1, D), ln2_b[l].reshape(1, D), y_rows, alpha)
    return h.reshape(B, T, D)
```

```python
import functools
import math

import jax
import jax.numpy as jnp
from jax import lax
from jax.experimental import pallas as pl
from jax.experimental.pallas import tpu as pltpu

ROPE_THETA = 500000.0
MLA_HEADS = 8
MLA_NOPE = 64
MLA_ROPE = 32
MLA_V = 64
Q_LORA = 256
KV_LORA = 128
MOBA_HEADS = 8
MOBA_HD = 64
MOBA_ROT = MOBA_HD // 4
MOBA_BLOCK = 256
MOBA_TOPK = 3
N_EXPERTS = 32
MOE_TOPK = 4
SWIGLU_LIMIT = 7.0
SWIGLU_ALPHA = 1.702
RMS_EPS = 1e-6
LN_EPS = 1e-5

LANES = 128
SUBLANES = 8
VMEM_LIMIT_BYTES = 56 * 1024 * 1024

SLOT = LANES
TQ = MOBA_BLOCK
ROW_TILE = 256
MOE_GROUP = 256
GROUP_SUB = MOE_GROUP * SUBLANES
DISPATCH_TILE = 256
COMBINE_TILE = 128
ISSUE_UNROLL = 4
DMA_PRIORITIES = 2
HEAD_LANES = 64
ONES_LANE = 64
BIAS_LANE = 64
ATTN_HEADS_PER_STEP = 8
LOG2E = math.log2(math.e)
NEG_BIG = -(2.0 ** 100)

F32 = jnp.float32
BF16 = jnp.bfloat16
NT_DIMS = (((1,), (1,)), ((), ()))


def _dot(a, b, precision=None):
    return jnp.dot(a, b, preferred_element_type=F32, precision=precision)


def _dot_nt(a, b, precision=None):
    return lax.dot_general(a, b, NT_DIMS, preferred_element_type=F32, precision=precision)


def _rows_to_tiles(ref, x):
    rows = x.shape[0]
    for c in range(SUBLANES):
        ref[pl.ds(c, rows, stride=SUBLANES), :] = x[:, c * LANES:(c + 1) * LANES]


def _tiles_to_rows(ref, rows):
    return jnp.concatenate([ref[pl.ds(c, rows, stride=SUBLANES), :] for c in range(SUBLANES)], axis=1)


def _rms(x, g):
    return x * lax.rsqrt(jnp.mean(x * x, axis=-1, keepdims=True) + RMS_EPS) * g


def _layer_norm(x, g, b):
    mu = jnp.mean(x, axis=-1, keepdims=True)
    xc = x - mu
    var = jnp.mean(xc * xc, axis=-1, keepdims=True)
    return xc * lax.rsqrt(var + LN_EPS) * g + b


_C_QL = 0
_C_KVL = _C_QL + Q_LORA
_C_KR = _C_KVL + KV_LORA
_C_KRS = _C_KR + SLOT
_C_MK = _C_KRS + SLOT
_MH = MOBA_HEADS * MOBA_HD
_C_MKS = _C_MK + _MH
_C_END = _C_MKS + _MH


def _prep_kernel(x_ref, win_ref, wmt_ref, wqt_ref, wqst_ref, wk_ref, wvt_ref, qg_ref, kvg_ref,
                 ca_ref, sa_ref, cat_ref, sat_ref, cb_ref, sb_ref, cbt_ref, sbt_ref,
                 qat_ref, ka_ref, vat_ref, mqt_ref, mk_ref, mvt_ref, kmean_scr):
    c = pl.program_id(1)
    xb = x_ref[...].astype(BF16)
    ca = ca_ref[...]
    sa = sa_ref[...]
    lane = lax.broadcasted_iota(jnp.int32, (ROW_TILE, SLOT), 1)
    head_lanes = lane < HEAD_LANES
    ones_rows = (lax.broadcasted_iota(jnp.int32, (SLOT - HEAD_LANES, ROW_TILE), 0) == ONES_LANE - HEAD_LANES).astype(F32)

    ql = _dot(xb, win_ref[:, _C_QL:_C_KVL])
    kvl = _dot(xb, win_ref[:, _C_KVL:_C_KR])
    kr = _dot(xb, win_ref[:, _C_KR:_C_KRS])
    krs = _dot(xb, win_ref[:, _C_KRS:_C_MK])
    qn = _rms(ql, qg_ref[...]).astype(BF16)
    kvn = _rms(kvl, kvg_ref[...]).astype(BF16)
    q_t = _dot_nt(wqt_ref[...], qn)
    qs_t = _dot_nt(wqst_ref[...], qn)
    kn = _dot(kvn, wk_ref[...])
    v_t = _dot_nt(wvt_ref[...], kvn)
    cat = cat_ref[...]
    sat = sat_ref[...]
    scale_a = LOG2E / math.sqrt(MLA_NOPE + MLA_ROPE)
    kro = kr * ca + krs * sa
    for h in range(MLA_HEADS):
        sl = slice(h * SLOT, (h + 1) * SLOT)
        qat_ref[0, sl, :] = ((q_t[sl] * cat + qs_t[sl] * sat) * scale_a).astype(BF16)
        ka_ref[:, sl] = (kn[:, sl] + kro).astype(BF16)
        vat_ref[0, sl, :] = jnp.concatenate([v_t[h * SLOT:h * SLOT + HEAD_LANES], ones_rows], axis=0).astype(BF16)

    mk = _dot(xb, win_ref[:, _C_MK:_C_MKS])
    mks = _dot(xb, win_ref[:, _C_MKS:_C_END])
    cb = cb_ref[...]
    sb = sb_ref[...]
    npair = _MH // LANES
    pair = lambda a, j: a[:, j * LANES:(j + 1) * LANES]
    mk_rot = [pair(mk, j) * cb + pair(mks, j) * sb for j in range(npair)]
    mk_all = jnp.concatenate(mk_rot, axis=1)

    mq_t = _dot_nt(wmt_ref[0:_MH, :], xb)
    mqs_t = _dot_nt(wmt_ref[_MH:2 * _MH, :], xb)
    mv_t = _dot_nt(wmt_ref[2 * _MH:3 * _MH, :], xb)
    cbt = cbt_ref[...]
    sbt = sbt_ref[...]
    head = lambda a, h: a[h * MOBA_HD:(h + 1) * MOBA_HD]
    mq_rot_t = [head(mq_t, h) * cbt + head(mqs_t, h) * sbt for h in range(MOBA_HEADS)]

    nrow = MOBA_HEADS * SUBLANES
    @pl.when(c == 0)
    def _():
        kmean_scr[...] = jnp.zeros_like(kmean_scr)

    row_i = lax.broadcasted_iota(jnp.int32, (nrow, _MH), 0)
    lane_i = lax.broadcasted_iota(jnp.int32, (nrow, _MH), 1)
    kmean_c = jnp.mean(mk_all, axis=0, keepdims=True)
    put = ((row_i % SUBLANES) == c) & ((lane_i // MOBA_HD) == (row_i // SUBLANES))
    table = kmean_scr[...]
    gate_t = _dot(table, jnp.concatenate(mq_rot_t, axis=0), precision=lax.Precision.HIGHEST)
    kmean_scr[...] = jnp.where(put, jnp.broadcast_to(kmean_c, (nrow, _MH)), table)

    n_idx = lax.broadcasted_iota(jnp.int32, (SUBLANES, ROW_TILE), 0)
    valid = n_idx < c
    scale_b = LOG2E / math.sqrt(MOBA_HD)
    zero_rows = jnp.zeros((SLOT - HEAD_LANES - SUBLANES, ROW_TILE), F32)
    block_onehot = (lane == BIAS_LANE + c).astype(F32)
    for h in range(MOBA_HEADS):
        g = jnp.where(valid, gate_t[h * SUBLANES:(h + 1) * SUBLANES, :], -jnp.inf)
        rank = jnp.zeros((SUBLANES, ROW_TILE), jnp.int32)
        for k in range(1, SUBLANES):
            other = pltpu.roll(g, k, axis=0)
            other_n = pltpu.roll(n_idx, k, axis=0)
            beats = (other > g) | ((other == g) & (other_n < n_idx))
            rank = rank + beats.astype(jnp.int32)
        keep = (valid & (rank < MOBA_TOPK)) | (n_idx == c)
        bias = jnp.where(keep, 0.0, NEG_BIG)
        sl = slice(h * SLOT, (h + 1) * SLOT)
        mqt_ref[0, sl, :] = jnp.concatenate([mq_rot_t[h] * scale_b, bias, zero_rows], axis=0).astype(BF16)
        mvt_ref[0, sl, :] = jnp.concatenate([head(mv_t, h), ones_rows], axis=0).astype(BF16)
        j, hh = divmod(h, 2)
        k_h = mk_rot[j] if hh == 0 else pltpu.roll(mk_rot[j], HEAD_LANES, axis=1)
        mk_ref[:, sl] = jnp.where(head_lanes, k_h, block_onehot).astype(BF16)


def _prep(x2, win, wmt, wqt, wqst, wk, wvt, qg, kvg, tables, B, T):
    N, D = x2.shape
    nt = T // ROW_TILE
    row = lambda b, c: (b * nt + c, 0)
    col = lambda b, c: (0, b * nt + c)
    full = lambda b, c: (0, 0)
    ca, sa, cat, sat, cb, sb, cbt, sbt = tables
    width = MLA_HEADS * SLOT

    def rows(w):
        return pl.BlockSpec((ROW_TILE, w), row)

    def cols(a):
        return pl.BlockSpec((a.shape[0], ROW_TILE), col)

    def whole(a):
        return pl.BlockSpec(a.shape, full)

    rowmajor = jax.ShapeDtypeStruct((N, width), BF16)
    transposed = jax.ShapeDtypeStruct((B * nt, width, ROW_TILE), BF16)
    t_spec = pl.BlockSpec((1, width, ROW_TILE), lambda b, c: (b * nt + c, 0, 0))
    return pl.pallas_call(
        _prep_kernel,
        out_shape=(transposed, rowmajor, transposed, transposed, rowmajor, transposed),
        grid=(B, nt),
        in_specs=[rows(D), whole(win), whole(wmt), whole(wqt), whole(wqst), whole(wk), whole(wvt), whole(qg), whole(kvg),
                  rows(LANES), rows(LANES), cols(cat), cols(sat), rows(LANES), rows(LANES), cols(cbt), cols(sbt)],
        out_specs=(t_spec, rows(width), t_spec, t_spec, rows(width), t_spec),
        scratch_shapes=[pltpu.VMEM((MOBA_HEADS * SUBLANES, _MH), F32)],
        compiler_params=pltpu.CompilerParams(dimension_semantics=("arbitrary", "arbitrary"),
                                             vmem_limit_bytes=VMEM_LIMIT_BYTES),
        name="prep",
    )(x2, win, wmt, wqt, wqst, wk, wvt, qg, kvg, ca, sa, cat, sat, cb, sb, cbt, sbt)


def _attn_kernel(qt_ref, k_ref, vt_ref, o_ref, s_scr):
    i = pl.program_id(2)
    nh = ATTN_HEADS_PER_STEP
    slot = lambda h: slice(h * SLOT, (h + 1) * SLOT)

    def scores(h, j0):
        return _dot(k_ref[pl.ds(j0, TQ), slot(h)], qt_ref[0, slot(h), :])

    def fold(s):
        out = s[0:SUBLANES]
        for t in range(1, TQ // SUBLANES):
            out = jnp.maximum(out, s[t * SUBLANES:(t + 1) * SUBLANES])
        return out

    def pass1(j, mrun):
        j0 = pl.multiple_of(j * TQ, TQ)
        out = []
        for h in range(nh):
            s = scores(h, j0)
            s_scr[h, j] = s
            out.append(jnp.maximum(mrun[h], fold(s)))
        return tuple(out)

    mrun = lax.fori_loop(0, i, pass1, tuple(jnp.full((SUBLANES, TQ), -jnp.inf, F32) for _ in range(nh)))
    d0 = pl.multiple_of(i * TQ, TQ)
    key = lax.broadcasted_iota(jnp.int32, (TQ, TQ), 0)
    query = lax.broadcasted_iota(jnp.int32, (TQ, TQ), 1)
    ms = []
    for h in range(nh):
        s = jnp.where(key <= query, scores(h, d0), -jnp.inf)
        s_scr[h, i] = s
        ms.append(jnp.max(jnp.maximum(mrun[h], fold(s)), axis=0, keepdims=True))

    def pass2(j, acc):
        out = []
        for h in range(nh):
            p = jnp.exp2(s_scr[h, j] - ms[h]).astype(BF16)
            out.append(acc[h] + _dot(vt_ref[j, slot(h), :], p))
        return tuple(out)

    acc = lax.fori_loop(0, i + 1, pass2, tuple(jnp.zeros((SLOT, TQ), F32) for _ in range(nh)))
    outs = [a[0:HEAD_LANES] / a[ONES_LANE:ONES_LANE + 1] for a in acc]
    for jj in range(nh // 2):
        both = jnp.concatenate([outs[2 * jj], outs[2 * jj + 1]], axis=0)
        o_ref[:, jj * LANES:(jj + 1) * LANES] = both.T.astype(o_ref.dtype)


def _attention(qt, k, vt, B, T, name):
    N = k.shape[0]
    nq = T // TQ
    nh = ATTN_HEADS_PER_STEP
    heads = k.shape[1] // SLOT
    return pl.pallas_call(
        _attn_kernel,
        out_shape=jax.ShapeDtypeStruct((N, heads * HEAD_LANES), BF16),
        grid=(B, heads // nh, nq),
        in_specs=[pl.BlockSpec((1, nh * SLOT, TQ), lambda b, g, i: (b * nq + i, g, 0)),
                  pl.BlockSpec((T, nh * SLOT), lambda b, g, i: (b, g)),
                  pl.BlockSpec((nq, nh * SLOT, TQ), lambda b, g, i: (b, g, 0))],
        out_specs=pl.BlockSpec((TQ, nh * HEAD_LANES), lambda b, g, i: (b * nq + i, g)),
        scratch_shapes=[pltpu.VMEM((nh, nq, TQ, TQ), F32)],
        compiler_params=pltpu.CompilerParams(dimension_semantics=("arbitrary", "arbitrary", "arbitrary"),
                                             vmem_limit_bytes=VMEM_LIMIT_BYTES),
        name=name,
    )(qt, k, vt)


def _oproj_kernel(alpha, a_ref, m_ref, x_ref, wo_ref, g_ref, b_ref, wr_ref, br_ref,
                  x1_ref, x1t_ref, route_ref, gates_ref, cnt_ref, carry_scr):
    i = pl.program_id(0)

    @pl.when(i == 0)
    def _():
        carry_scr[...] = jnp.zeros_like(carry_scr)

    mix = _dot(jnp.concatenate([a_ref[...], m_ref[...]], axis=1), wo_ref[...])
    x1 = _layer_norm(alpha * x_ref[...] + mix, g_ref[...], b_ref[...])
    x1_ref[...] = x1
    _rows_to_tiles(x1t_ref, x1)

    ne = br_ref.shape[0]
    rows = x1.shape[0]
    x_hi = x1.astype(BF16)
    x_lo = (x1 - x_hi.astype(F32)).astype(BF16)
    both = _dot(x_hi, wr_ref[...])
    logits_rm = both[:, :LANES] + both[:, LANES:] + _dot(x_lo, wr_ref[:, :LANES])
    logits = logits_rm.T[0:ne] + br_ref[...]
    expert = lax.broadcasted_iota(jnp.int32, logits.shape, 0)
    expert_f = expert.astype(F32)
    vals, idxs = [], []
    work = logits
    for _ in range(MOE_TOPK):
        mx = jnp.max(work, axis=0, keepdims=True)
        ix = jnp.min(jnp.where(work == mx, expert_f, float(ne)), axis=0, keepdims=True).astype(jnp.int32)
        vals.append(mx)
        idxs.append(ix)
        work = jnp.where(expert == ix, -jnp.inf, work)
    exps = [jnp.exp(v - vals[0]) for v in vals]
    den = exps[0]
    for e in exps[1:]:
        den = den + e

    onehot = jnp.zeros(logits.shape, F32)
    for ix in idxs:
        onehot = onehot + (expert == ix).astype(F32)
    r = lax.broadcasted_iota(jnp.int32, (rows, rows), 0)
    cidx = lax.broadcasted_iota(jnp.int32, (rows, rows), 1)
    earlier = (r < cidx).astype(BF16)
    carry = carry_scr[:, 0:1]
    before = _dot(onehot.astype(BF16), earlier) + carry
    row8 = lax.broadcasted_iota(jnp.int32, (SUBLANES, rows), 0)
    route = jnp.zeros((SUBLANES, rows), jnp.int32)
    gates = jnp.zeros((SUBLANES, rows), F32)
    for k in range(MOE_TOPK):
        rank = jnp.sum(jnp.where(expert == idxs[k], before, 0.0), axis=0, keepdims=True).astype(jnp.int32)
        route = jnp.where(row8 == k, idxs[k], route)
        route = jnp.where(row8 == MOE_TOPK + k, rank, route)
        gates = jnp.where(row8 == k, exps[k] / den, gates)
    route_ref[...] = route
    gates_ref[...] = jnp.concatenate([gates, jnp.zeros((LANES - SUBLANES, rows), F32)], axis=0).T
    new_carry = carry + jnp.sum(onehot, axis=1, keepdims=True)
    carry_scr[...] = jnp.broadcast_to(new_carry, carry_scr.shape)
    cnt_ref[...] = jnp.broadcast_to(new_carry, cnt_ref.shape)


def _oproj(a, m, x2, wo, g1, b1, wr, br, alpha):
    N, D = x2.shape
    nt = N // ROW_TILE
    row = lambda i: (i, 0)
    full = lambda i: (0, 0)
    return pl.pallas_call(
        functools.partial(_oproj_kernel, alpha),
        out_shape=(jax.ShapeDtypeStruct((N, D), F32),
                   jax.ShapeDtypeStruct((N * SUBLANES, LANES), F32),
                   jax.ShapeDtypeStruct((SUBLANES, N), jnp.int32),
                   jax.ShapeDtypeStruct((N, LANES), F32),
                   jax.ShapeDtypeStruct((br.shape[0], LANES), F32)),
        grid=(nt,),
        in_specs=[pl.BlockSpec((ROW_TILE, a.shape[1]), row), pl.BlockSpec((ROW_TILE, m.shape[1]), row),
                  pl.BlockSpec((ROW_TILE, D), row), pl.BlockSpec(wo.shape, full),
                  pl.BlockSpec(g1.shape, full), pl.BlockSpec(b1.shape, full),
                  pl.BlockSpec(wr.shape, full), pl.BlockSpec(br.shape, full)],
        out_specs=(pl.BlockSpec((ROW_TILE, D), row), pl.BlockSpec((ROW_TILE * SUBLANES, LANES), row),
                   pl.BlockSpec((SUBLANES, ROW_TILE), lambda i: (0, i)),
                   pl.BlockSpec((ROW_TILE, LANES), row), pl.BlockSpec((br.shape[0], LANES), full)),
        scratch_shapes=[pltpu.VMEM((br.shape[0], LANES), F32)],
        compiler_params=pltpu.CompilerParams(dimension_semantics=("arbitrary",),
                                             vmem_limit_bytes=VMEM_LIMIT_BYTES),
        name="oproj_router",
    )(a, m, x2, wo, g1, b1, wr, br)


def _dispatch_kernel(dest_ref, tail_ref, x1_ref, xs_ref, zero_scr, sem):
    i = pl.program_id(0)

    @pl.when(i == 0)
    def _():
        zero_scr[...] = jnp.zeros_like(zero_scr)

        def tail_copy(e):
            return pltpu.make_async_copy(zero_scr, xs_ref.at[pl.ds(pl.multiple_of(tail_ref[e] * SUBLANES, GROUP_SUB), GROUP_SUB)], sem)

        def start(e, _):
            @pl.when(tail_ref[e] >= 0)
            def _():
                tail_copy(e).start()
            return 0

        def wait(e, _):
            @pl.when(tail_ref[e] >= 0)
            def _():
                tail_copy(e).wait()
            return 0

        lax.fori_loop(0, N_EXPERTS, start, 0)
        lax.fori_loop(0, N_EXPERTS, wait, 0)

        def spare_copy(blk):
            return pltpu.make_async_copy(zero_scr, xs_ref.at[pl.ds(pl.multiple_of(blk * GROUP_SUB, GROUP_SUB), GROUP_SUB)], sem)

        def start_spare(blk, _):
            spare_copy(blk).start()
            return 0

        def wait_spare(blk, _):
            spare_copy(blk).wait()
            return 0

        nblk = xs_ref.shape[0] // GROUP_SUB
        lax.fori_loop(tail_ref[N_EXPERTS], nblk, start_spare, 0)
        lax.fori_loop(tail_ref[N_EXPERTS], nblk, wait_spare, 0)

    base = i * DISPATCH_TILE
    n_tok = pl.num_programs(0) * DISPATCH_TILE

    def row_copy(t, k):
        d = dest_ref[k * n_tok + base + t]
        return pltpu.make_async_copy(x1_ref.at[pl.ds(pl.multiple_of(t * SUBLANES, SUBLANES), SUBLANES)],
                                     xs_ref.at[pl.ds(pl.multiple_of(d * SUBLANES, SUBLANES), SUBLANES)], sem)

    def start_rows(t, _):
        for k in range(MOE_TOPK):
            row_copy(t, k).start(priority=k % DMA_PRIORITIES)
        return 0

    lax.fori_loop(0, DISPATCH_TILE, start_rows, 0, unroll=ISSUE_UNROLL)
    for k in range(MOE_TOPK):
        pltpu.make_async_copy(x1_ref, xs_ref.at[pl.ds(0, DISPATCH_TILE * SUBLANES)], sem).wait()


def _dispatch(dest, tail, x1t, n_rows):
    return pl.pallas_call(
        _dispatch_kernel,
        out_shape=jax.ShapeDtypeStruct((n_rows * SUBLANES, LANES), F32),
        grid_spec=pltpu.PrefetchScalarGridSpec(
            num_scalar_prefetch=2,
            grid=(x1t.shape[0] // (DISPATCH_TILE * SUBLANES),),
            in_specs=[pl.BlockSpec((DISPATCH_TILE * SUBLANES, LANES), lambda i, d, t: (i, 0))],
            out_specs=pl.BlockSpec(memory_space=pl.ANY),
            scratch_shapes=[pltpu.VMEM((GROUP_SUB, LANES), F32), pltpu.SemaphoreType.DMA(())],
        ),
        compiler_params=pltpu.CompilerParams(dimension_semantics=("arbitrary",),
                                             vmem_limit_bytes=VMEM_LIMIT_BYTES),
        name="dispatch",
    )(dest, tail, x1t)


def _experts_kernel(be_ref, slot_ref, nxt_ref, nused_ref, x_ref, wg_hbm, bg_ref, wu_hbm, bu_ref, wd_hbm, bd_ref,
                    y_ref, wg_st, wu_st, wd_st, wg_bf, wu_bf, wd_bf, sems):
    i = pl.program_id(0)
    prev = be_ref[jnp.maximum(i - 1, 0)]
    changed = (i == 0) | (be_ref[i] != prev)
    active = i < nused_ref[0]
    slot = slot_ref[i]

    def weight_copies(expert, s):
        return (pltpu.make_async_copy(wg_hbm.at[expert], wg_st.at[s], sems.at[s, 0]),
                pltpu.make_async_copy(wu_hbm.at[expert], wu_st.at[s], sems.at[s, 1]),
                pltpu.make_async_copy(wd_hbm.at[expert], wd_st.at[s], sems.at[s, 2]))

    @pl.when(i == 0)
    def _():
        for cp in weight_copies(be_ref[0], slot):
            cp.start()

    @pl.when(active & changed)
    def _():
        for cp in weight_copies(be_ref[i], slot):
            cp.wait()

        @pl.when(nxt_ref[i] >= 0)
        def _():
            for cp in weight_copies(nxt_ref[i], 1 - slot):
                cp.start()

        wg_bf[...] = wg_st[slot].astype(BF16)
        wu_bf[...] = wu_st[slot].astype(BF16)
        wd_bf[...] = wd_st[slot].astype(BF16)

    @pl.when(active)
    def _():
        xb = _tiles_to_rows(x_ref, MOE_GROUP).astype(BF16)
        g = jnp.minimum(_dot(xb, wg_bf[...]) + bg_ref[0], SWIGLU_LIMIT)
        u = jnp.clip(_dot(xb, wu_bf[...]) + bu_ref[0], -SWIGLU_LIMIT, SWIGLU_LIMIT)
        h = g * (1.0 / (1.0 + jnp.exp(-SWIGLU_ALPHA * g))) * (u + 1.0)
        _rows_to_tiles(y_ref, _dot(h.astype(BF16), wd_bf[...]) + bd_ref[0])

    @pl.when(jnp.logical_not(active))
    def _():
        y_ref[...] = jnp.zeros_like(y_ref)


def _experts(blk_expert, blk_slot, blk_next, n_used, xs, wg, bg, wu, bu, wd, bd):
    E, D, F = wg.shape
    nblk = xs.shape[0] // GROUP_SUB

    def rowmap(i, be, sl, nx, nu):
        return (jnp.minimum(i, nu[0] - 1), 0)

    def bmap(i, be, sl, nx, nu):
        return (be[i], 0, 0)

    hbm = pl.BlockSpec(memory_space=pl.ANY)
    return pl.pallas_call(
        _experts_kernel,
        out_shape=jax.ShapeDtypeStruct(xs.shape, F32),
        grid_spec=pltpu.PrefetchScalarGridSpec(
            num_scalar_prefetch=4,
            grid=(nblk,),
            in_specs=[pl.BlockSpec((GROUP_SUB, LANES), rowmap),
                      hbm, pl.BlockSpec((1, 1, F), bmap),
                      hbm, pl.BlockSpec((1, 1, F), bmap),
                      hbm, pl.BlockSpec((1, 1, D), bmap)],
            out_specs=pl.BlockSpec((GROUP_SUB, LANES), lambda i, be, sl, nx, nu: (i, 0)),
            scratch_shapes=[pltpu.VMEM((2, D, F), F32), pltpu.VMEM((2, D, F), F32), pltpu.VMEM((2, F, D), F32),
                            pltpu.VMEM((D, F), BF16), pltpu.VMEM((D, F), BF16), pltpu.VMEM((F, D), BF16),
                            pltpu.SemaphoreType.DMA((2, 3))],
        ),
        compiler_params=pltpu.CompilerParams(dimension_semantics=("arbitrary",),
                                             vmem_limit_bytes=VMEM_LIMIT_BYTES),
        name="experts",
    )(blk_expert, blk_slot, blk_next, n_used, xs, wg, bg, wu, bu, wd, bd)


def _combine_kernel(alpha, dest_ref, gates_ref, x1_ref, g_ref, b_ref, y_ref, o_ref, ybuf, sems):
    i = pl.program_id(0)
    buf = i % 2

    n_tok = pl.num_programs(0) * COMBINE_TILE

    def gather_tile(tile, into):
        base = tile * COMBINE_TILE

        def start_rows(t, _):
            for k in range(MOE_TOPK):
                d = dest_ref[k * n_tok + base + t]
                pltpu.make_async_copy(y_ref.at[pl.ds(pl.multiple_of(d * SUBLANES, SUBLANES), SUBLANES)],
                                      ybuf.at[into, k, pl.ds(pl.multiple_of(t * SUBLANES, SUBLANES), SUBLANES)],
                                      sems.at[into]).start(priority=k % DMA_PRIORITIES)
            return 0

        lax.fori_loop(0, COMBINE_TILE, start_rows, 0, unroll=ISSUE_UNROLL)

    @pl.when(i == 0)
    def _():
        gather_tile(0, 0)

    @pl.when(i + 1 < pl.num_programs(0))
    def _():
        gather_tile(i + 1, 1 - buf)

    for k in range(MOE_TOPK):
        pltpu.make_async_copy(y_ref.at[pl.ds(0, COMBINE_TILE * SUBLANES)], ybuf.at[buf, k], sems.at[buf]).wait()

    gates = gates_ref[...]
    ffn = gates[:, 0:1] * _tiles_to_rows(ybuf.at[buf, 0], COMBINE_TILE)
    for k in range(1, MOE_TOPK):
        ffn = ffn + gates[:, k:k + 1] * _tiles_to_rows(ybuf.at[buf, k], COMBINE_TILE)
    o_ref[...] = _layer_norm(alpha * x1_ref[...] + ffn, g_ref[...], b_ref[...])


def _combine(dest, gates, x1, g2, b2, y_rows, alpha):
    N, D = x1.shape
    row = lambda i, d: (i, 0)
    full = lambda i, d: (0, 0)
    return pl.pallas_call(
        functools.partial(_combine_kernel, alpha),
        out_shape=jax.ShapeDtypeStruct((N, D), F32),
        grid_spec=pltpu.PrefetchScalarGridSpec(
            num_scalar_prefetch=1,
            grid=(N // COMBINE_TILE,),
            in_specs=[pl.BlockSpec((COMBINE_TILE, LANES), row), pl.BlockSpec((COMBINE_TILE, D), row),
                      pl.BlockSpec(g2.shape, full), pl.BlockSpec(b2.shape, full),
                      pl.BlockSpec(memory_space=pl.ANY)],
            out_specs=pl.BlockSpec((COMBINE_TILE, D), row),
            scratch_shapes=[pltpu.VMEM((2, MOE_TOPK, COMBINE_TILE * SUBLANES, LANES), F32),
                            pltpu.SemaphoreType.DMA((2,))],
        ),
        compiler_params=pltpu.CompilerParams(dimension_semantics=("arbitrary",),
                                             vmem_limit_bytes=VMEM_LIMIT_BYTES),
        name="combine",
    )(dest, gates, x1, g2, b2, y_rows)


def _rot_partner(w, half):
    return jnp.concatenate([-w[..., half:2 * half], w[..., :half]], axis=-1)


def _layer_weights(w_in, w_q_b, w_kv_b):
    D = w_in.shape[0]
    o1 = Q_LORA
    o2 = o1 + KV_LORA
    o3 = o2 + MLA_ROPE
    w_ql, w_kvl, w_kr = w_in[:, :o1], w_in[:, o1:o2], w_in[:, o2:o3]
    w_mq, w_mk, w_mv = w_in[:, o3:o3 + _MH], w_in[:, o3 + _MH:o3 + 2 * _MH], w_in[:, o3 + 2 * _MH:]
    zpad = lambda n: jnp.zeros((D, n), w_in.dtype)
    tail = SLOT - MLA_NOPE - MLA_ROPE
    kr_slot = jnp.concatenate([zpad(MLA_NOPE), w_kr, zpad(tail)], axis=1)
    krs_slot = jnp.concatenate([zpad(MLA_NOPE), _rot_partner(w_kr, MLA_ROPE // 2), zpad(tail)], axis=1)

    def moba_partner(w):
        w3 = w.reshape(D, MOBA_HEADS, MOBA_HD)
        part = jnp.concatenate([_rot_partner(w3[..., :MOBA_ROT], MOBA_ROT // 2),
                                jnp.zeros((D, MOBA_HEADS, MOBA_HD - MOBA_ROT), w.dtype)], axis=-1)
        return part.reshape(D, _MH)

    win = jnp.concatenate([w_ql, w_kvl, kr_slot, krs_slot, w_mk, moba_partner(w_mk)], axis=1).astype(BF16)
    wmt = jnp.concatenate([w_mq, moba_partner(w_mq), w_mv], axis=1).T.astype(BF16)

    wq3 = w_q_b.reshape(Q_LORA, MLA_HEADS, MLA_NOPE + MLA_ROPE)
    zq = jnp.zeros((Q_LORA, MLA_HEADS, tail), w_q_b.dtype)
    wq = jnp.concatenate([wq3, zq], axis=-1).reshape(Q_LORA, MLA_HEADS * SLOT)
    wqs = jnp.concatenate([jnp.zeros((Q_LORA, MLA_HEADS, MLA_NOPE), w_q_b.dtype),
                           _rot_partner(wq3[..., MLA_NOPE:], MLA_ROPE // 2), zq],
                          axis=-1).reshape(Q_LORA, MLA_HEADS * SLOT)
    wkv3 = w_kv_b.reshape(KV_LORA, MLA_HEADS, MLA_NOPE + MLA_V)
    wk = jnp.concatenate([wkv3[..., :MLA_NOPE], jnp.zeros((KV_LORA, MLA_HEADS, SLOT - MLA_NOPE), w_kv_b.dtype)],
                         axis=-1).reshape(KV_LORA, MLA_HEADS * SLOT).astype(BF16)
    wv = jnp.concatenate([wkv3[..., MLA_NOPE:], jnp.zeros((KV_LORA, MLA_HEADS, SLOT - MLA_V), w_kv_b.dtype)],
                         axis=-1).reshape(KV_LORA, MLA_HEADS * SLOT)
    return win, wmt, wq.T.astype(BF16), wqs.T.astype(BF16), wk, wv.T.astype(BF16)


def _rope_tables(positions):
    pos = positions.astype(F32).reshape(-1, 1)
    n = pos.shape[0]

    def cs(d_rot):
        inv_freq = ROPE_THETA ** (-jnp.arange(0, d_rot, 2, dtype=F32) / d_rot)
        ang = pos * inv_freq
        return jnp.cos(ang), jnp.sin(ang)

    ca, sa = cs(MLA_ROPE)
    tail = SLOT - MLA_NOPE - MLA_ROPE
    ca_t = jnp.concatenate([jnp.ones((n, MLA_NOPE), F32), ca, ca, jnp.zeros((n, tail), F32)], axis=1)
    sa_t = jnp.concatenate([jnp.zeros((n, MLA_NOPE), F32), sa, sa, jnp.zeros((n, tail), F32)], axis=1)
    cb, sb = cs(MOBA_ROT)
    rest = MOBA_HD - MOBA_ROT
    cb_h = jnp.concatenate([cb, cb, jnp.ones((n, rest), F32)], axis=1)
    sb_h = jnp.concatenate([sb, sb, jnp.zeros((n, rest), F32)], axis=1)
    return (ca_t, sa_t, ca_t.T, sa_t.T, jnp.concatenate([cb_h, cb_h], axis=1), jnp.concatenate([sb_h, sb_h], axis=1),
            cb_h.T, sb_h.T)


def kernel(x, positions, w_in, q_a_norm, w_q_b, kv_a_norm, w_kv_b, w_o, ln1_g, ln1_b, w_router, b_router,
           w_gate, b_gate, w_up, b_up, w_down, b_down, ln2_g, ln2_b):
    B, T, D = x.shape
    depth = w_in.shape[0]
    alpha = (2.0 * depth) ** 0.25
    N = B * T
    assert T % MOBA_BLOCK == 0 and T // MOBA_BLOCK <= SUBLANES and N % ROW_TILE == 0
    assert D == SUBLANES * LANES
    n_asg = N * MOE_TOPK
    n_rows = n_asg + N_EXPERTS * MOE_GROUP
    tables = _rope_tables(positions)
    h = x.reshape(N, D)
    for l in range(depth):
        win, wmt, wqt, wqst, wk, wvt = _layer_weights(w_in[l], w_q_b[l], w_kv_b[l])
        qa, ka, va, mq, mk, mv = _prep(h, win, wmt, wqt, wqst, wk, wvt, q_a_norm[l].reshape(1, -1),
                                       kv_a_norm[l].reshape(1, -1), tables, B, T)
        a = _attention(qa, ka, va, B, T, "mla_attention")
        m = _attention(mq, mk, mv, B, T, "moba_attention")
        wr_pad = jnp.concatenate([w_router[l], jnp.zeros((D, LANES - N_EXPERTS), F32)], axis=1)
        wr_hi = wr_pad.astype(BF16)
        wr = jnp.concatenate([wr_hi, (wr_pad - wr_hi.astype(F32)).astype(BF16)], axis=1)
        br = b_router[l].reshape(N_EXPERTS, 1)
        x1, x1t, route, gates, cnt = _oproj(a, m, h, w_o[l].astype(BF16), ln1_g[l].reshape(1, D), ln1_b[l].reshape(1, D),
                                       wr, br, alpha)
        counts = cnt[:, 0].astype(jnp.int32)
        padded = (counts + MOE_GROUP - 1) // MOE_GROUP * MOE_GROUP
        pad_end = jnp.cumsum(padded).astype(jnp.int32)
        pad_start = pad_end - padded
        e_idx = route[:MOE_TOPK]
        group_start = jnp.sum(jnp.where(e_idx[..., None] == jnp.arange(N_EXPERTS, dtype=jnp.int32), pad_start, 0), axis=-1)
        dest = (group_start + route[MOE_TOPK:2 * MOE_TOPK]).reshape(n_asg)
        n_used = (pad_end[-1:] // MOE_GROUP).astype(jnp.int32)
        tail = jnp.concatenate([jnp.where(padded > 0, pad_end - MOE_GROUP, -1).astype(jnp.int32), n_used])
        blk_start = jnp.arange(n_rows // MOE_GROUP, dtype=jnp.int32) * MOE_GROUP
        blk_expert = jnp.minimum(jnp.sum((blk_start[:, None] >= pad_end[None, :]).astype(jnp.int32), axis=1),
                                 N_EXPERTS - 1)
        nblk = n_rows // MOE_GROUP
        starts_group = jnp.concatenate([jnp.ones((1,), jnp.int32), (blk_expert[1:] != blk_expert[:-1]).astype(jnp.int32)])
        blk_slot = (jnp.cumsum(starts_group) - 1) % 2
        next_start = pad_end[blk_expert] // MOE_GROUP
        blk_next = jnp.where(next_start < n_used[0], blk_expert[jnp.minimum(next_start, nblk - 1)], -1).astype(jnp.int32)
        xs = _dispatch(dest, tail, x1t, n_rows)
        y_rows = _experts(blk_expert, blk_slot.astype(jnp.int32), blk_next, n_used, xs, w_gate[l], b_gate[l].reshape(N_EXPERTS, 1, -1),
                          w_up[l], b_up[l].reshape(N_EXPERTS, 1, -1), w_down[l], b_down[l].reshape(N_EXPERTS, 1, -1))
        h = _combine(dest, gates, x1, ln2_g[l].reshape(1, D), ln2_b[l].reshape(1, D), y_rows, alpha)
    return h.reshape(B, T, D)
```

```python
import functools
import math

import jax
import jax.numpy as jnp
from jax import lax
from jax.experimental import pallas as pl
from jax.experimental.pallas import tpu as pltpu

ROPE_THETA = 500000.0
MLA_HEADS = 8
MLA_NOPE = 64
MLA_ROPE = 32
MLA_V = 64
Q_LORA = 256
KV_LORA = 128
MOBA_HEADS = 8
MOBA_HD = 64
MOBA_ROT = MOBA_HD // 4
MOBA_BLOCK = 256
MOBA_TOPK = 3
N_EXPERTS = 32
MOE_TOPK = 4
SWIGLU_LIMIT = 7.0
SWIGLU_ALPHA = 1.702
RMS_EPS = 1e-6
LN_EPS = 1e-5

LANES = 128
SUBLANES = 8
VMEM_LIMIT_BYTES = 56 * 1024 * 1024

SLOT = LANES
TQ = MOBA_BLOCK
ROW_TILE = 256
MOE_GROUP = 256
GROUP_SUB = MOE_GROUP * SUBLANES
DISPATCH_TILE = 256
COMBINE_TILE = 128
ISSUE_UNROLL = 4
DMA_PRIORITIES = 2
HEAD_LANES = 64
ONES_LANE = 64
BIAS_LANE = 64
ATTN_HEADS_PER_STEP = 8
LOG2E = math.log2(math.e)
NEG_BIG = -(2.0 ** 100)

F32 = jnp.float32
BF16 = jnp.bfloat16
NT_DIMS = (((1,), (1,)), ((), ()))


def _dot(a, b, precision=None):
    return jnp.dot(a, b, preferred_element_type=F32, precision=precision)


def _dot_nt(a, b, precision=None):
    return lax.dot_general(a, b, NT_DIMS, preferred_element_type=F32, precision=precision)


def _rows_to_tiles(ref, x):
    rows = x.shape[0]
    for c in range(SUBLANES):
        ref[pl.ds(c, rows, stride=SUBLANES), :] = x[:, c * LANES:(c + 1) * LANES]


def _tiles_to_rows(ref, rows):
    return jnp.concatenate([ref[pl.ds(c, rows, stride=SUBLANES), :] for c in range(SUBLANES)], axis=1)


def _rms(x, g):
    return x * lax.rsqrt(jnp.mean(x * x, axis=-1, keepdims=True) + RMS_EPS) * g


def _layer_norm(x, g, b):
    mu = jnp.mean(x, axis=-1, keepdims=True)
    xc = x - mu
    var = jnp.mean(xc * xc, axis=-1, keepdims=True)
    return xc * lax.rsqrt(var + LN_EPS) * g + b


_C_QL = 0
_C_KVL = _C_QL + Q_LORA
_C_KR = _C_KVL + KV_LORA
_C_KRS = _C_KR + SLOT
_C_MK = _C_KRS + SLOT
_MH = MOBA_HEADS * MOBA_HD
_C_MKS = _C_MK + _MH
_C_END = _C_MKS + _MH


def _prep_kernel(x_ref, win_ref, wmt_ref, wqt_ref, wqst_ref, wk_ref, wvt_ref, qg_ref, kvg_ref,
                 ca_ref, sa_ref, cat_ref, sat_ref, cb_ref, sb_ref, cbt_ref, sbt_ref,
                 qat_ref, ka_ref, vat_ref, mqt_ref, mk_ref, mvt_ref, kmean_scr):
    c = pl.program_id(1)
    xb = x_ref[...].astype(BF16)
    ca = ca_ref[...]
    sa = sa_ref[...]
    lane = lax.broadcasted_iota(jnp.int32, (ROW_TILE, SLOT), 1)
    head_lanes = lane < HEAD_LANES
    ones_rows = (lax.broadcasted_iota(jnp.int32, (SLOT - HEAD_LANES, ROW_TILE), 0) == ONES_LANE - HEAD_LANES).astype(F32)

    ql = _dot(xb, win_ref[:, _C_QL:_C_KVL])
    kvl = _dot(xb, win_ref[:, _C_KVL:_C_KR])
    kr = _dot(xb, win_ref[:, _C_KR:_C_KRS])
    krs = _dot(xb, win_ref[:, _C_KRS:_C_MK])
    qn = _rms(ql, qg_ref[...]).astype(BF16)
    kvn = _rms(kvl, kvg_ref[...]).astype(BF16)
    q_t = _dot_nt(wqt_ref[...], qn)
    qs_t = _dot_nt(wqst_ref[...], qn)
    kn = _dot(kvn, wk_ref[...])
    v_t = _dot_nt(wvt_ref[...], kvn)
    cat = cat_ref[...]
    sat = sat_ref[...]
    scale_a = LOG2E / math.sqrt(MLA_NOPE + MLA_ROPE)
    kro = kr * ca + krs * sa
    qd = MLA_NOPE + MLA_ROPE
    q_pad = jnp.zeros((SLOT - qd, ROW_TILE), F32)
    for h in range(MLA_HEADS):
        sl = slice(h * SLOT, (h + 1) * SLOT)
        q_rope = q_t[h * qd + MLA_NOPE:(h + 1) * qd] * cat + qs_t[h * MLA_ROPE:(h + 1) * MLA_ROPE] * sat
        q_slot = jnp.concatenate([q_t[h * qd:h * qd + MLA_NOPE], q_rope, q_pad], axis=0)
        qat_ref[0, sl, :] = (q_slot * scale_a).astype(BF16)
        ka_ref[:, sl] = (kn[:, sl] + kro).astype(BF16)
        vat_ref[0, sl, :] = jnp.concatenate([v_t[h * MLA_V:(h + 1) * MLA_V], ones_rows], axis=0).astype(BF16)

    mk = _dot(xb, win_ref[:, _C_MK:_C_MKS])
    mks = _dot(xb, win_ref[:, _C_MKS:_C_END])
    cb = cb_ref[...]
    sb = sb_ref[...]
    npair = _MH // LANES
    pair = lambda a, j: a[:, j * LANES:(j + 1) * LANES]
    mk_rot = [pair(mk, j) * cb + pair(mks, j) * sb for j in range(npair)]
    mk_all = jnp.concatenate(mk_rot, axis=1)

    nrot = MOBA_HEADS * MOBA_ROT
    mq_t = _dot_nt(wmt_ref[0:_MH, :], xb)
    mqs_t = _dot_nt(wmt_ref[_MH:_MH + nrot, :], xb)
    mv_t = _dot_nt(wmt_ref[_MH + nrot:2 * _MH + nrot, :], xb)
    cbt = cbt_ref[...]
    sbt = sbt_ref[...]
    head = lambda a, h: a[h * MOBA_HD:(h + 1) * MOBA_HD]
    mq_rot_t = [jnp.concatenate([head(mq_t, h)[0:MOBA_ROT] * cbt + mqs_t[h * MOBA_ROT:(h + 1) * MOBA_ROT] * sbt,
                                 head(mq_t, h)[MOBA_ROT:]], axis=0) for h in range(MOBA_HEADS)]

    nrow = MOBA_HEADS * SUBLANES
    @pl.when(c == 0)
    def _():
        kmean_scr[...] = jnp.zeros_like(kmean_scr)

    row_i = lax.broadcasted_iota(jnp.int32, (nrow, _MH), 0)
    lane_i = lax.broadcasted_iota(jnp.int32, (nrow, _MH), 1)
    kmean_c = jnp.mean(mk_all, axis=0, keepdims=True)
    put = ((row_i % SUBLANES) == c) & ((lane_i // MOBA_HD) == (row_i // SUBLANES))
    table = kmean_scr[...]
    gate_t = _dot(table, jnp.concatenate(mq_rot_t, axis=0), precision=lax.Precision.HIGHEST)
    kmean_scr[...] = jnp.where(put, jnp.broadcast_to(kmean_c, (nrow, _MH)), table)

    n_idx = lax.broadcasted_iota(jnp.int32, (SUBLANES, ROW_TILE), 0)
    valid = n_idx < c
    scale_b = LOG2E / math.sqrt(MOBA_HD)
    zero_rows = jnp.zeros((SLOT - HEAD_LANES - SUBLANES, ROW_TILE), F32)
    block_onehot = (lane == BIAS_LANE + c).astype(F32)
    for h in range(MOBA_HEADS):
        g = jnp.where(valid, gate_t[h * SUBLANES:(h + 1) * SUBLANES, :], -jnp.inf)
        rank = jnp.zeros((SUBLANES, ROW_TILE), jnp.int32)
        for k in range(1, SUBLANES):
            other = pltpu.roll(g, k, axis=0)
            other_n = pltpu.roll(n_idx, k, axis=0)
            beats = (other > g) | ((other == g) & (other_n < n_idx))
            rank = rank + beats.astype(jnp.int32)
        keep = (valid & (rank < MOBA_TOPK)) | (n_idx == c)
        bias = jnp.where(keep, 0.0, NEG_BIG)
        sl = slice(h * SLOT, (h + 1) * SLOT)
        mqt_ref[0, sl, :] = jnp.concatenate([mq_rot_t[h] * scale_b, bias, zero_rows], axis=0).astype(BF16)
        mvt_ref[0, sl, :] = jnp.concatenate([head(mv_t, h), ones_rows], axis=0).astype(BF16)
        j, hh = divmod(h, 2)
        k_h = mk_rot[j] if hh == 0 else pltpu.roll(mk_rot[j], HEAD_LANES, axis=1)
        mk_ref[:, sl] = jnp.where(head_lanes, k_h, block_onehot).astype(BF16)


def _prep(x2, win, wmt, wqt, wqst, wk, wvt, qg, kvg, tables, B, T):
    N, D = x2.shape
    nt = T // ROW_TILE
    row = lambda b, c: (b * nt + c, 0)
    col = lambda b, c: (0, b * nt + c)
    full = lambda b, c: (0, 0)
    ca, sa, cat, sat, cb, sb, cbt, sbt = tables
    width = MLA_HEADS * SLOT

    def rows(w):
        return pl.BlockSpec((ROW_TILE, w), row)

    def cols(a):
        return pl.BlockSpec((a.shape[0], ROW_TILE), col)

    def whole(a):
        return pl.BlockSpec(a.shape, full)

    rowmajor = jax.ShapeDtypeStruct((N, width), BF16)
    transposed = jax.ShapeDtypeStruct((B * nt, width, ROW_TILE), BF16)
    t_spec = pl.BlockSpec((1, width, ROW_TILE), lambda b, c: (b * nt + c, 0, 0))
    return pl.pallas_call(
        _prep_kernel,
        out_shape=(transposed, rowmajor, transposed, transposed, rowmajor, transposed),
        grid=(B, nt),
        in_specs=[rows(D), whole(win), whole(wmt), whole(wqt), whole(wqst), whole(wk), whole(wvt), whole(qg), whole(kvg),
                  rows(LANES), rows(LANES), cols(cat), cols(sat), rows(LANES), rows(LANES), cols(cbt), cols(sbt)],
        out_specs=(t_spec, rows(width), t_spec, t_spec, rows(width), t_spec),
        scratch_shapes=[pltpu.VMEM((MOBA_HEADS * SUBLANES, _MH), F32)],
        compiler_params=pltpu.CompilerParams(dimension_semantics=("arbitrary", "arbitrary"),
                                             vmem_limit_bytes=VMEM_LIMIT_BYTES),
        name="prep",
    )(x2, win, wmt, wqt, wqst, wk, wvt, qg, kvg, ca, sa, cat, sat, cb, sb, cbt, sbt)


def _attn_kernel(qt_ref, k_ref, vt_ref, o_ref, s_scr):
    i = pl.program_id(2)
    nh = ATTN_HEADS_PER_STEP
    slot = lambda h: slice(h * SLOT, (h + 1) * SLOT)
    nblk = i + 1
    npair = nblk // 2
    q_pos = i * TQ + lax.broadcasted_iota(jnp.int32, (1, TQ), 1)

    def fold(s):
        out = s[0:SUBLANES]
        for t in range(1, s.shape[0] // SUBLANES):
            out = jnp.maximum(out, s[t * SUBLANES:(t + 1) * SUBLANES])
        return out

    def score_blocks(j, n, mrun):
        j0 = pl.multiple_of(j * TQ, TQ)
        visible = (j0 + lax.broadcasted_iota(jnp.int32, (n * TQ, TQ), 0)) <= q_pos
        out = []
        for h in range(nh):
            s = _dot(k_ref[pl.ds(j0, n * TQ), slot(h)], qt_ref[0, slot(h), :])
            s = jnp.where(visible, s, -jnp.inf)
            s_scr[h, pl.ds(j, n)] = s.reshape(n, TQ, TQ)
            out.append(jnp.maximum(mrun[h], fold(s)))
        return tuple(out)

    mrun = tuple(jnp.full((SUBLANES, TQ), -jnp.inf, F32) for _ in range(nh))
    mrun = lax.fori_loop(0, npair, lambda jj, m: score_blocks(2 * jj, 2, m), mrun)
    mrun = lax.fori_loop(2 * npair, nblk, lambda j, m: score_blocks(j, 1, m), mrun)
    ms = [jnp.max(m, axis=0, keepdims=True) for m in mrun]

    def accumulate(j, n, acc):
        out = []
        for h in range(nh):
            p = jnp.exp2(s_scr[h, pl.ds(j, n)].reshape(n * TQ, TQ) - ms[h]).astype(BF16)
            vt = jnp.concatenate([vt_ref[j + t, slot(h), :] for t in range(n)], axis=1)
            out.append(acc[h] + _dot(vt, p))
        return tuple(out)

    acc = tuple(jnp.zeros((SLOT, TQ), F32) for _ in range(nh))
    acc = lax.fori_loop(0, npair, lambda jj, a: accumulate(2 * jj, 2, a), acc)
    acc = lax.fori_loop(2 * npair, nblk, lambda j, a: accumulate(j, 1, a), acc)
    outs = [a[0:HEAD_LANES] / a[ONES_LANE:ONES_LANE + 1] for a in acc]
    for jj in range(nh // 2):
        both = jnp.concatenate([outs[2 * jj], outs[2 * jj + 1]], axis=0)
        o_ref[:, jj * LANES:(jj + 1) * LANES] = both.T.astype(o_ref.dtype)


def _attention(qt, k, vt, B, T, name):
    N = k.shape[0]
    nq = T // TQ
    nh = ATTN_HEADS_PER_STEP
    heads = k.shape[1] // SLOT
    return pl.pallas_call(
        _attn_kernel,
        out_shape=jax.ShapeDtypeStruct((N, heads * HEAD_LANES), BF16),
        grid=(B, heads // nh, nq),
        in_specs=[pl.BlockSpec((1, nh * SLOT, TQ), lambda b, g, i: (b * nq + i, g, 0)),
                  pl.BlockSpec((T, nh * SLOT), lambda b, g, i: (b, g)),
                  pl.BlockSpec((nq, nh * SLOT, TQ), lambda b, g, i: (b, g, 0))],
        out_specs=pl.BlockSpec((TQ, nh * HEAD_LANES), lambda b, g, i: (b * nq + i, g)),
        scratch_shapes=[pltpu.VMEM((nh, nq, TQ, TQ), F32)],
        compiler_params=pltpu.CompilerParams(dimension_semantics=("arbitrary", "arbitrary", "arbitrary"),
                                             vmem_limit_bytes=VMEM_LIMIT_BYTES),
        name=name,
    )(qt, k, vt)


def _oproj_kernel(alpha, a_ref, m_ref, x_ref, wo_ref, g_ref, b_ref, wr_ref, br_ref,
                  x1_ref, x1t_ref, route_ref, gates_ref, cnt_ref, carry_scr):
    i = pl.program_id(0)

    @pl.when(i == 0)
    def _():
        carry_scr[...] = jnp.zeros_like(carry_scr)

    mix = _dot(jnp.concatenate([a_ref[...], m_ref[...]], axis=1), wo_ref[...])
    x1 = _layer_norm(alpha * x_ref[...] + mix, g_ref[...], b_ref[...])
    x1_ref[...] = x1
    _rows_to_tiles(x1t_ref, x1)

    ne = br_ref.shape[0]
    rows = x1.shape[0]
    x_hi = x1.astype(BF16)
    x_lo = (x1 - x_hi.astype(F32)).astype(BF16)
    both = _dot(x_hi, wr_ref[...])
    logits_rm = both[:, :LANES] + both[:, LANES:] + _dot(x_lo, wr_ref[:, :LANES])
    logits = logits_rm.T[0:ne] + br_ref[...]
    expert = lax.broadcasted_iota(jnp.int32, logits.shape, 0)
    expert_f = expert.astype(F32)
    vals, idxs = [], []
    work = logits
    for _ in range(MOE_TOPK):
        mx = jnp.max(work, axis=0, keepdims=True)
        ix = jnp.min(jnp.where(work == mx, expert_f, float(ne)), axis=0, keepdims=True).astype(jnp.int32)
        vals.append(mx)
        idxs.append(ix)
        work = jnp.where(expert == ix, -jnp.inf, work)
    exps = [jnp.exp(v - vals[0]) for v in vals]
    den = exps[0]
    for e in exps[1:]:
        den = den + e

    onehot = jnp.zeros(logits.shape, F32)
    for ix in idxs:
        onehot = onehot + (expert == ix).astype(F32)
    r = lax.broadcasted_iota(jnp.int32, (rows, rows), 0)
    cidx = lax.broadcasted_iota(jnp.int32, (rows, rows), 1)
    earlier = (r < cidx).astype(BF16)
    carry = carry_scr[:, 0:1]
    before = _dot(onehot.astype(BF16), earlier) + carry
    row8 = lax.broadcasted_iota(jnp.int32, (SUBLANES, rows), 0)
    route = jnp.zeros((SUBLANES, rows), jnp.int32)
    gates = jnp.zeros((SUBLANES, rows), F32)
    for k in range(MOE_TOPK):
        rank = jnp.sum(jnp.where(expert == idxs[k], before, 0.0), axis=0, keepdims=True).astype(jnp.int32)
        route = jnp.where(row8 == k, idxs[k], route)
        route = jnp.where(row8 == MOE_TOPK + k, rank, route)
        gates = jnp.where(row8 == k, exps[k] / den, gates)
    route_ref[...] = route
    gates_ref[...] = jnp.concatenate([gates, jnp.zeros((LANES - SUBLANES, rows), F32)], axis=0).T
    new_carry = carry + jnp.sum(onehot, axis=1, keepdims=True)
    carry_scr[...] = jnp.broadcast_to(new_carry, carry_scr.shape)
    cnt_ref[...] = jnp.broadcast_to(new_carry, cnt_ref.shape)


def _oproj(a, m, x2, wo, g1, b1, wr, br, alpha):
    N, D = x2.shape
    nt = N // ROW_TILE
    row = lambda i: (i, 0)
    full = lambda i: (0, 0)
    return pl.pallas_call(
        functools.partial(_oproj_kernel, alpha),
        out_shape=(jax.ShapeDtypeStruct((N, D), F32),
                   jax.ShapeDtypeStruct((N * SUBLANES, LANES), F32),
                   jax.ShapeDtypeStruct((SUBLANES, N), jnp.int32),
                   jax.ShapeDtypeStruct((N, LANES), F32),
                   jax.ShapeDtypeStruct((br.shape[0], LANES), F32)),
        grid=(nt,),
        in_specs=[pl.BlockSpec((ROW_TILE, a.shape[1]), row), pl.BlockSpec((ROW_TILE, m.shape[1]), row),
                  pl.BlockSpec((ROW_TILE, D), row), pl.BlockSpec(wo.shape, full),
                  pl.BlockSpec(g1.shape, full), pl.BlockSpec(b1.shape, full),
                  pl.BlockSpec(wr.shape, full), pl.BlockSpec(br.shape, full)],
        out_specs=(pl.BlockSpec((ROW_TILE, D), row), pl.BlockSpec((ROW_TILE * SUBLANES, LANES), row),
                   pl.BlockSpec((SUBLANES, ROW_TILE), lambda i: (0, i)),
                   pl.BlockSpec((ROW_TILE, LANES), row), pl.BlockSpec((br.shape[0], LANES), full)),
        scratch_shapes=[pltpu.VMEM((br.shape[0], LANES), F32)],
        compiler_params=pltpu.CompilerParams(dimension_semantics=("arbitrary",),
                                             vmem_limit_bytes=VMEM_LIMIT_BYTES),
        name="oproj_router",
    )(a, m, x2, wo, g1, b1, wr, br)


def _dispatch_kernel(dest_ref, tail_ref, x1_ref, xs_ref, zero_scr, sem):
    i = pl.program_id(0)

    @pl.when(i == 0)
    def _():
        zero_scr[...] = jnp.zeros_like(zero_scr)

        def tail_copy(e):
            return pltpu.make_async_copy(zero_scr, xs_ref.at[pl.ds(pl.multiple_of(tail_ref[e] * SUBLANES, GROUP_SUB), GROUP_SUB)], sem)

        def start(e, _):
            @pl.when(tail_ref[e] >= 0)
            def _():
                tail_copy(e).start()
            return 0

        def wait(e, _):
            @pl.when(tail_ref[e] >= 0)
            def _():
                tail_copy(e).wait()
            return 0

        lax.fori_loop(0, N_EXPERTS, start, 0)
        lax.fori_loop(0, N_EXPERTS, wait, 0)

        def spare_copy(blk):
            return pltpu.make_async_copy(zero_scr, xs_ref.at[pl.ds(pl.multiple_of(blk * GROUP_SUB, GROUP_SUB), GROUP_SUB)], sem)

        def start_spare(blk, _):
            spare_copy(blk).start()
            return 0

        def wait_spare(blk, _):
            spare_copy(blk).wait()
            return 0

        nblk = xs_ref.shape[0] // GROUP_SUB
        lax.fori_loop(tail_ref[N_EXPERTS], nblk, start_spare, 0)
        lax.fori_loop(tail_ref[N_EXPERTS], nblk, wait_spare, 0)

    base = i * DISPATCH_TILE
    n_tok = pl.num_programs(0) * DISPATCH_TILE

    def row_copy(t, k):
        d = dest_ref[k * n_tok + base + t]
        return pltpu.make_async_copy(x1_ref.at[pl.ds(pl.multiple_of(t * SUBLANES, SUBLANES), SUBLANES)],
                                     xs_ref.at[pl.ds(pl.multiple_of(d * SUBLANES, SUBLANES), SUBLANES)], sem)

    def start_rows(t, _):
        for k in range(MOE_TOPK):
            row_copy(t, k).start(priority=k % DMA_PRIORITIES)
        return 0

    lax.fori_loop(0, DISPATCH_TILE, start_rows, 0, unroll=ISSUE_UNROLL)
    for k in range(MOE_TOPK):
        pltpu.make_async_copy(x1_ref, xs_ref.at[pl.ds(0, DISPATCH_TILE * SUBLANES)], sem).wait()


def _dispatch(dest, tail, x1t, n_rows):
    return pl.pallas_call(
        _dispatch_kernel,
        out_shape=jax.ShapeDtypeStruct((n_rows * SUBLANES, LANES), F32),
        grid_spec=pltpu.PrefetchScalarGridSpec(
            num_scalar_prefetch=2,
            grid=(x1t.shape[0] // (DISPATCH_TILE * SUBLANES),),
            in_specs=[pl.BlockSpec((DISPATCH_TILE * SUBLANES, LANES), lambda i, d, t: (i, 0))],
            out_specs=pl.BlockSpec(memory_space=pl.ANY),
            scratch_shapes=[pltpu.VMEM((GROUP_SUB, LANES), F32), pltpu.SemaphoreType.DMA(())],
        ),
        compiler_params=pltpu.CompilerParams(dimension_semantics=("arbitrary",),
                                             vmem_limit_bytes=VMEM_LIMIT_BYTES),
        name="dispatch",
    )(dest, tail, x1t)


def _experts_kernel(be_ref, slot_ref, nxt_ref, nused_ref, x_ref, wg_hbm, bg_ref, wu_hbm, bu_ref, wd_hbm, bd_ref,
                    y_ref, wg_st, wu_st, wd_st, wg_bf, wu_bf, wd_bf, sems):
    i = pl.program_id(0)
    prev = be_ref[jnp.maximum(i - 1, 0)]
    changed = (i == 0) | (be_ref[i] != prev)
    active = i < nused_ref[0]
    slot = slot_ref[i]

    def weight_copies(expert, s):
        return (pltpu.make_async_copy(wg_hbm.at[expert], wg_st.at[s], sems.at[s, 0]),
                pltpu.make_async_copy(wu_hbm.at[expert], wu_st.at[s], sems.at[s, 1]),
                pltpu.make_async_copy(wd_hbm.at[expert], wd_st.at[s], sems.at[s, 2]))

    @pl.when(i == 0)
    def _():
        for cp in weight_copies(be_ref[0], slot):
            cp.start()

    @pl.when(active & changed)
    def _():
        for cp in weight_copies(be_ref[i], slot):
            cp.wait()

        @pl.when(nxt_ref[i] >= 0)
        def _():
            for cp in weight_copies(nxt_ref[i], 1 - slot):
                cp.start()

        wg_bf[...] = wg_st[slot].astype(BF16)
        wu_bf[...] = wu_st[slot].astype(BF16)
        wd_bf[...] = wd_st[slot].astype(BF16)

    @pl.when(active)
    def _():
        xb = _tiles_to_rows(x_ref, MOE_GROUP).astype(BF16)
        g = jnp.minimum(_dot(xb, wg_bf[...]) + bg_ref[0], SWIGLU_LIMIT)
        u = jnp.clip(_dot(xb, wu_bf[...]) + bu_ref[0], -SWIGLU_LIMIT, SWIGLU_LIMIT)
        h = g * (1.0 / (1.0 + jnp.exp(-SWIGLU_ALPHA * g))) * (u + 1.0)
        _rows_to_tiles(y_ref, _dot(h.astype(BF16), wd_bf[...]) + bd_ref[0])

    @pl.when(jnp.logical_not(active))
    def _():
        y_ref[...] = jnp.zeros_like(y_ref)


def _experts(blk_expert, blk_slot, blk_next, n_used, xs, wg, bg, wu, bu, wd, bd):
    E, D, F = wg.shape
    nblk = xs.shape[0] // GROUP_SUB

    def rowmap(i, be, sl, nx, nu):
        return (jnp.minimum(i, nu[0] - 1), 0)

    def bmap(i, be, sl, nx, nu):
        return (be[i], 0, 0)

    hbm = pl.BlockSpec(memory_space=pl.ANY)
    return pl.pallas_call(
        _experts_kernel,
        out_shape=jax.ShapeDtypeStruct(xs.shape, F32),
        grid_spec=pltpu.PrefetchScalarGridSpec(
            num_scalar_prefetch=4,
            grid=(nblk,),
            in_specs=[pl.BlockSpec((GROUP_SUB, LANES), rowmap),
                      hbm, pl.BlockSpec((1, 1, F), bmap),
                      hbm, pl.BlockSpec((1, 1, F), bmap),
                      hbm, pl.BlockSpec((1, 1, D), bmap)],
            out_specs=pl.BlockSpec((GROUP_SUB, LANES), lambda i, be, sl, nx, nu: (i, 0)),
            scratch_shapes=[pltpu.VMEM((2, D, F), F32), pltpu.VMEM((2, D, F), F32), pltpu.VMEM((2, F, D), F32),
                            pltpu.VMEM((D, F), BF16), pltpu.VMEM((D, F), BF16), pltpu.VMEM((F, D), BF16),
                            pltpu.SemaphoreType.DMA((2, 3))],
        ),
        compiler_params=pltpu.CompilerParams(dimension_semantics=("arbitrary",),
                                             vmem_limit_bytes=VMEM_LIMIT_BYTES),
        name="experts",
    )(blk_expert, blk_slot, blk_next, n_used, xs, wg, bg, wu, bu, wd, bd)


def _combine_kernel(alpha, dest_ref, gates_ref, x1_ref, g_ref, b_ref, y_ref, o_ref, ybuf, sems):
    i = pl.program_id(0)
    buf = i % 2

    n_tok = pl.num_programs(0) * COMBINE_TILE

    def gather_tile(tile, into):
        base = tile * COMBINE_TILE

        def start_rows(t, _):
            for k in range(MOE_TOPK):
                d = dest_ref[k * n_tok + base + t]
                pltpu.make_async_copy(y_ref.at[pl.ds(pl.multiple_of(d * SUBLANES, SUBLANES), SUBLANES)],
                                      ybuf.at[into, k, pl.ds(pl.multiple_of(t * SUBLANES, SUBLANES), SUBLANES)],
                                      sems.at[into]).start(priority=k % DMA_PRIORITIES)
            return 0

        lax.fori_loop(0, COMBINE_TILE, start_rows, 0, unroll=ISSUE_UNROLL)

    @pl.when(i == 0)
    def _():
        gather_tile(0, 0)

    @pl.when(i + 1 < pl.num_programs(0))
    def _():
        gather_tile(i + 1, 1 - buf)

    for k in range(MOE_TOPK):
        pltpu.make_async_copy(y_ref.at[pl.ds(0, COMBINE_TILE * SUBLANES)], ybuf.at[buf, k], sems.at[buf]).wait()

    gates = gates_ref[...]
    ffn = gates[:, 0:1] * _tiles_to_rows(ybuf.at[buf, 0], COMBINE_TILE)
    for k in range(1, MOE_TOPK):
        ffn = ffn + gates[:, k:k + 1] * _tiles_to_rows(ybuf.at[buf, k], COMBINE_TILE)
    o_ref[...] = _layer_norm(alpha * x1_ref[...] + ffn, g_ref[...], b_ref[...])


def _combine(dest, gates, x1, g2, b2, y_rows, alpha):
    N, D = x1.shape
    row = lambda i, d: (i, 0)
    full = lambda i, d: (0, 0)
    return pl.pallas_call(
        functools.partial(_combine_kernel, alpha),
        out_shape=jax.ShapeDtypeStruct((N, D), F32),
        grid_spec=pltpu.PrefetchScalarGridSpec(
            num_scalar_prefetch=1,
            grid=(N // COMBINE_TILE,),
            in_specs=[pl.BlockSpec((COMBINE_TILE, LANES), row), pl.BlockSpec((COMBINE_TILE, D), row),
                      pl.BlockSpec(g2.shape, full), pl.BlockSpec(b2.shape, full),
                      pl.BlockSpec(memory_space=pl.ANY)],
            out_specs=pl.BlockSpec((COMBINE_TILE, D), row),
            scratch_shapes=[pltpu.VMEM((2, MOE_TOPK, COMBINE_TILE * SUBLANES, LANES), F32),
                            pltpu.SemaphoreType.DMA((2,))],
        ),
        compiler_params=pltpu.CompilerParams(dimension_semantics=("arbitrary",),
                                             vmem_limit_bytes=VMEM_LIMIT_BYTES),
        name="combine",
    )(dest, gates, x1, g2, b2, y_rows)


def _rot_partner(w, half):
    return jnp.concatenate([-w[..., half:2 * half], w[..., :half]], axis=-1)


def _layer_weights(w_in, w_q_b, w_kv_b):
    D = w_in.shape[0]
    o1 = Q_LORA
    o2 = o1 + KV_LORA
    o3 = o2 + MLA_ROPE
    w_ql, w_kvl, w_kr = w_in[:, :o1], w_in[:, o1:o2], w_in[:, o2:o3]
    w_mq, w_mk, w_mv = w_in[:, o3:o3 + _MH], w_in[:, o3 + _MH:o3 + 2 * _MH], w_in[:, o3 + 2 * _MH:]
    zpad = lambda n: jnp.zeros((D, n), w_in.dtype)
    tail = SLOT - MLA_NOPE - MLA_ROPE
    kr_slot = jnp.concatenate([zpad(MLA_NOPE), w_kr, zpad(tail)], axis=1)
    krs_slot = jnp.concatenate([zpad(MLA_NOPE), _rot_partner(w_kr, MLA_ROPE // 2), zpad(tail)], axis=1)

    def moba_partner(w):
        w3 = w.reshape(D, MOBA_HEADS, MOBA_HD)
        part = jnp.concatenate([_rot_partner(w3[..., :MOBA_ROT], MOBA_ROT // 2),
                                jnp.zeros((D, MOBA_HEADS, MOBA_HD - MOBA_ROT), w.dtype)], axis=-1)
        return part.reshape(D, _MH)

    win = jnp.concatenate([w_ql, w_kvl, kr_slot, krs_slot, w_mk, moba_partner(w_mk)], axis=1).astype(BF16)
    mq_partner = _rot_partner(w_mq.reshape(D, MOBA_HEADS, MOBA_HD)[..., :MOBA_ROT], MOBA_ROT // 2)
    wmt = jnp.concatenate([w_mq, mq_partner.reshape(D, MOBA_HEADS * MOBA_ROT), w_mv], axis=1).T.astype(BF16)

    wq3 = w_q_b.reshape(Q_LORA, MLA_HEADS, MLA_NOPE + MLA_ROPE)
    wqs = _rot_partner(wq3[..., MLA_NOPE:], MLA_ROPE // 2).reshape(Q_LORA, MLA_HEADS * MLA_ROPE)
    wkv3 = w_kv_b.reshape(KV_LORA, MLA_HEADS, MLA_NOPE + MLA_V)
    wk = jnp.concatenate([wkv3[..., :MLA_NOPE], jnp.zeros((KV_LORA, MLA_HEADS, SLOT - MLA_NOPE), w_kv_b.dtype)],
                         axis=-1).reshape(KV_LORA, MLA_HEADS * SLOT).astype(BF16)
    wv = wkv3[..., MLA_NOPE:].reshape(KV_LORA, MLA_HEADS * MLA_V)
    return win, wmt, w_q_b.T.astype(BF16), wqs.T.astype(BF16), wk, wv.T.astype(BF16)


def _rope_tables(positions):
    pos = positions.astype(F32).reshape(-1, 1)
    n = pos.shape[0]

    def cs(d_rot):
        inv_freq = ROPE_THETA ** (-jnp.arange(0, d_rot, 2, dtype=F32) / d_rot)
        ang = pos * inv_freq
        return jnp.cos(ang), jnp.sin(ang)

    ca, sa = cs(MLA_ROPE)
    tail = SLOT - MLA_NOPE - MLA_ROPE
    ca_t = jnp.concatenate([jnp.ones((n, MLA_NOPE), F32), ca, ca, jnp.zeros((n, tail), F32)], axis=1)
    sa_t = jnp.concatenate([jnp.zeros((n, MLA_NOPE), F32), sa, sa, jnp.zeros((n, tail), F32)], axis=1)
    cb, sb = cs(MOBA_ROT)
    rest = MOBA_HD - MOBA_ROT
    cb_h = jnp.concatenate([cb, cb, jnp.ones((n, rest), F32)], axis=1)
    sb_h = jnp.concatenate([sb, sb, jnp.zeros((n, rest), F32)], axis=1)
    two = lambda a: jnp.concatenate([a, a], axis=1)
    return (ca_t, sa_t, two(ca).T, two(sa).T, two(cb_h), two(sb_h), two(cb).T, two(sb).T)


def kernel(x, positions, w_in, q_a_norm, w_q_b, kv_a_norm, w_kv_b, w_o, ln1_g, ln1_b, w_router, b_router,
           w_gate, b_gate, w_up, b_up, w_down, b_down, ln2_g, ln2_b):
    B, T, D = x.shape
    depth = w_in.shape[0]
    alpha = (2.0 * depth) ** 0.25
    N = B * T
    assert T % MOBA_BLOCK == 0 and T // MOBA_BLOCK <= SUBLANES and N % ROW_TILE == 0
    assert D == SUBLANES * LANES
    n_asg = N * MOE_TOPK
    n_rows = n_asg + N_EXPERTS * MOE_GROUP
    tables = _rope_tables(positions)
    h = x.reshape(N, D)
    for l in range(depth):
        win, wmt, wqt, wqst, wk, wvt = _layer_weights(w_in[l], w_q_b[l], w_kv_b[l])
        qa, ka, va, mq, mk, mv = _prep(h, win, wmt, wqt, wqst, wk, wvt, q_a_norm[l].reshape(1, -1),
                                       kv_a_norm[l].reshape(1, -1), tables, B, T)
        a = _attention(qa, ka, va, B, T, "mla_attention")
        m = _attention(mq, mk, mv, B, T, "moba_attention")
        wr_pad = jnp.concatenate([w_router[l], jnp.zeros((D, LANES - N_EXPERTS), F32)], axis=1)
        wr_hi = wr_pad.astype(BF16)
        wr = jnp.concatenate([wr_hi, (wr_pad - wr_hi.astype(F32)).astype(BF16)], axis=1)
        br = b_router[l].reshape(N_EXPERTS, 1)
        x1, x1t, route, gates, cnt = _oproj(a, m, h, w_o[l].astype(BF16), ln1_g[l].reshape(1, D), ln1_b[l].reshape(1, D),
                                       wr, br, alpha)
        counts = cnt[:, 0].astype(jnp.int32)
        padded = (counts + MOE_GROUP - 1) // MOE_GROUP * MOE_GROUP
        pad_end = jnp.cumsum(padded).astype(jnp.int32)
        pad_start = pad_end - padded
        e_idx = route[:MOE_TOPK]
        group_start = jnp.sum(jnp.where(e_idx[..., None] == jnp.arange(N_EXPERTS, dtype=jnp.int32), pad_start, 0), axis=-1)
        dest = (group_start + route[MOE_TOPK:2 * MOE_TOPK]).reshape(n_asg)
        n_used = (pad_end[-1:] // MOE_GROUP).astype(jnp.int32)
        tail = jnp.concatenate([jnp.where(padded > 0, pad_end - MOE_GROUP, -1).astype(jnp.int32), n_used])
        blk_start = jnp.arange(n_rows // MOE_GROUP, dtype=jnp.int32) * MOE_GROUP
        blk_expert = jnp.minimum(jnp.sum((blk_start[:, None] >= pad_end[None, :]).astype(jnp.int32), axis=1),
                                 N_EXPERTS - 1)
        nblk = n_rows // MOE_GROUP
        starts_group = jnp.concatenate([jnp.ones((1,), jnp.int32), (blk_expert[1:] != blk_expert[:-1]).astype(jnp.int32)])
        blk_slot = (jnp.cumsum(starts_group) - 1) % 2
        next_start = pad_end[blk_expert] // MOE_GROUP
        blk_next = jnp.where(next_start < n_used[0], blk_expert[jnp.minimum(next_start, nblk - 1)], -1).astype(jnp.int32)
        xs = _dispatch(dest, tail, x1t, n_rows)
        y_rows = _experts(blk_expert, blk_slot.astype(jnp.int32), blk_next, n_used, xs, w_gate[l], b_gate[l].reshape(N_EXPERTS, 1, -1),
                          w_up[l], b_up[l].reshape(N_EXPERTS, 1, -1), w_down[l], b_down[l].reshape(N_EXPERTS, 1, -1))
        h = _combine(dest, gates, x1, ln2_g[l].reshape(1, D), ln2_b[l].reshape(1, D), y_rows, alpha)
    return h.reshape(B, T, D)
```

```python
import functools
import math

import jax
import jax.numpy as jnp
from jax import lax
from jax.experimental import pallas as pl
from jax.experimental.pallas import tpu as pltpu

ROPE_THETA = 500000.0
MLA_HEADS = 8
MLA_NOPE = 64
MLA_ROPE = 32
MLA_V = 64
Q_LORA = 256
KV_LORA = 128
MOBA_HEADS = 8
MOBA_HD = 64
MOBA_ROT = MOBA_HD // 4
MOBA_BLOCK = 256
MOBA_TOPK = 3
N_EXPERTS = 32
MOE_TOPK = 4
SWIGLU_LIMIT = 7.0
SWIGLU_ALPHA = 1.702
RMS_EPS = 1e-6
LN_EPS = 1e-5

LANES = 128
SUBLANES = 8
VMEM_LIMIT_BYTES = 56 * 1024 * 1024

SLOT = LANES
TQ = MOBA_BLOCK
ROW_TILE = 256
MOE_GROUP = 256
GROUP_SUB = MOE_GROUP * SUBLANES
DISPATCH_TILE = 256
COMBINE_TILE = 256
ISSUE_UNROLL = 4
DMA_PRIORITIES = 2
HEAD_LANES = 64
ONES_LANE = 64
BIAS_LANE = 64
ATTN_HEADS_PER_STEP = 8
LOG2E = math.log2(math.e)
NEG_BIG = -(2.0 ** 100)

F32 = jnp.float32
BF16 = jnp.bfloat16
NT_DIMS = (((1,), (1,)), ((), ()))


def _dot(a, b, precision=None):
    return jnp.dot(a, b, preferred_element_type=F32, precision=precision)


def _dot_nt(a, b, precision=None):
    return lax.dot_general(a, b, NT_DIMS, preferred_element_type=F32, precision=precision)


def _rows_to_tiles(ref, x):
    rows = x.shape[0]
    for c in range(SUBLANES):
        ref[pl.ds(c, rows, stride=SUBLANES), :] = x[:, c * LANES:(c + 1) * LANES]


def _tiles_to_rows(ref, rows):
    return jnp.concatenate([ref[pl.ds(c, rows, stride=SUBLANES), :] for c in range(SUBLANES)], axis=1)


def _rms(x, g):
    return x * lax.rsqrt(jnp.mean(x * x, axis=-1, keepdims=True) + RMS_EPS) * g


def _layer_norm(x, g, b):
    mu = jnp.mean(x, axis=-1, keepdims=True)
    xc = x - mu
    var = jnp.mean(xc * xc, axis=-1, keepdims=True)
    return xc * lax.rsqrt(var + LN_EPS) * g + b


_C_QL = 0
_C_KVL = _C_QL + Q_LORA
_C_KR = _C_KVL + KV_LORA
_C_KRS = _C_KR + SLOT
_C_MK = _C_KRS + SLOT
_MH = MOBA_HEADS * MOBA_HD
_C_MKS = _C_MK + _MH
_C_END = _C_MKS + _MH


def _prep_kernel(x_ref, win_ref, wmt_ref, wqt_ref, wqst_ref, wk_ref, wvt_ref, qg_ref, kvg_ref,
                 cat_ref, sat_ref, cbt_ref, sbt_ref,
                 qat_ref, ka_ref, vat_ref, mqt_ref, mk_ref, mvt_ref, kmean_scr):
    c = pl.program_id(1)
    xb = x_ref[...].astype(BF16)
    cat = cat_ref[...]
    sat = sat_ref[...]
    cbt = cbt_ref[...]
    sbt = sbt_ref[...]
    const = lambda v, n: jnp.full((n, ROW_TILE), v, F32)
    tail = SLOT - MLA_NOPE - MLA_ROPE
    rest = MOBA_HD - MOBA_ROT
    ca = jnp.concatenate([const(1.0, MLA_NOPE), cat, const(0.0, tail)], axis=0).T
    sa = jnp.concatenate([const(0.0, MLA_NOPE), sat, const(0.0, tail)], axis=0).T
    cb = jnp.concatenate([cbt, const(1.0, rest), cbt, const(1.0, rest)], axis=0).T
    sb = jnp.concatenate([sbt, const(0.0, rest), sbt, const(0.0, rest)], axis=0).T
    lane =lax.broadcasted_iota(jnp.int32, (ROW_TILE, SLOT), 1)
    head_lanes = lane < HEAD_LANES
    ones_rows = (lax.broadcasted_iota(jnp.int32, (SLOT - HEAD_LANES, ROW_TILE), 0) == ONES_LANE - HEAD_LANES).astype(F32)

    ql = _dot(xb, win_ref[:, _C_QL:_C_KVL])
    kvl = _dot(xb, win_ref[:, _C_KVL:_C_KR])
    kr = _dot(xb, win_ref[:, _C_KR:_C_KRS])
    krs = _dot(xb, win_ref[:, _C_KRS:_C_MK])
    qn = _rms(ql, qg_ref[...]).astype(BF16)
    kvn = _rms(kvl, kvg_ref[...]).astype(BF16)
    q_t = _dot_nt(wqt_ref[...], qn)
    qs_t = _dot_nt(wqst_ref[...], qn)
    kn = _dot(kvn, wk_ref[...])
    v_t = _dot_nt(wvt_ref[...], kvn)
    scale_a = LOG2E / math.sqrt(MLA_NOPE + MLA_ROPE)
    kro = kr * ca + krs * sa
    qd = MLA_NOPE + MLA_ROPE
    q_pad = jnp.zeros((SLOT - qd, ROW_TILE), F32)
    for h in range(MLA_HEADS):
        sl = slice(h * SLOT, (h + 1) * SLOT)
        q_rope = q_t[h * qd + MLA_NOPE:(h + 1) * qd] * cat + qs_t[h * MLA_ROPE:(h + 1) * MLA_ROPE] * sat
        q_slot = jnp.concatenate([q_t[h * qd:h * qd + MLA_NOPE], q_rope, q_pad], axis=0)
        qat_ref[0, sl, :] = (q_slot * scale_a).astype(BF16)
        ka_ref[:, sl] = (kn[:, sl] + kro).astype(BF16)
        vat_ref[0, sl, :] = jnp.concatenate([v_t[h * MLA_V:(h + 1) * MLA_V], ones_rows], axis=0).astype(BF16)

    mk = _dot(xb, win_ref[:, _C_MK:_C_MKS])
    mks = _dot(xb, win_ref[:, _C_MKS:_C_END])
    npair = _MH // LANES
    pair = lambda a, j: a[:, j * LANES:(j + 1) * LANES]
    mk_rot = [pair(mk, j) * cb + pair(mks, j) * sb for j in range(npair)]
    mk_all = jnp.concatenate(mk_rot, axis=1)

    nrot = MOBA_HEADS * MOBA_ROT
    mq_t = _dot_nt(wmt_ref[0:_MH, :], xb)
    mqs_t = _dot_nt(wmt_ref[_MH:_MH + nrot, :], xb)
    mv_t = _dot_nt(wmt_ref[_MH + nrot:2 * _MH + nrot, :], xb)
    head = lambda a, h: a[h * MOBA_HD:(h + 1) * MOBA_HD]
    mq_rot_t = [jnp.concatenate([head(mq_t, h)[0:MOBA_ROT] * cbt + mqs_t[h * MOBA_ROT:(h + 1) * MOBA_ROT] * sbt,
                                 head(mq_t, h)[MOBA_ROT:]], axis=0) for h in range(MOBA_HEADS)]

    nrow = MOBA_HEADS * SUBLANES
    @pl.when(c == 0)
    def _():
        kmean_scr[...] = jnp.zeros_like(kmean_scr)

    row_i = lax.broadcasted_iota(jnp.int32, (nrow, _MH), 0)
    lane_i = lax.broadcasted_iota(jnp.int32, (nrow, _MH), 1)
    kmean_c = jnp.mean(mk_all, axis=0, keepdims=True)
    put = ((row_i % SUBLANES) == c) & ((lane_i // MOBA_HD) == (row_i // SUBLANES))
    table = kmean_scr[...]
    gate_t = _dot(table, jnp.concatenate(mq_rot_t, axis=0), precision=lax.Precision.HIGHEST)
    kmean_scr[...] = jnp.where(put, jnp.broadcast_to(kmean_c, (nrow, _MH)), table)

    n_idx = lax.broadcasted_iota(jnp.int32, (SUBLANES, ROW_TILE), 0)
    valid = n_idx < c
    scale_b = LOG2E / math.sqrt(MOBA_HD)
    zero_rows = jnp.zeros((SLOT - HEAD_LANES - SUBLANES, ROW_TILE), F32)
    block_onehot = (lane == BIAS_LANE + c).astype(F32)
    for h in range(MOBA_HEADS):
        g = jnp.where(valid, gate_t[h * SUBLANES:(h + 1) * SUBLANES, :], -jnp.inf)
        rank = jnp.zeros((SUBLANES, ROW_TILE), jnp.int32)
        for k in range(1, SUBLANES):
            other = pltpu.roll(g, k, axis=0)
            other_n = pltpu.roll(n_idx, k, axis=0)
            beats = (other > g) | ((other == g) & (other_n < n_idx))
            rank = rank + beats.astype(jnp.int32)
        keep = (valid & (rank < MOBA_TOPK)) | (n_idx == c)
        bias = jnp.where(keep, 0.0, NEG_BIG)
        sl = slice(h * SLOT, (h + 1) * SLOT)
        mqt_ref[0, sl, :] = jnp.concatenate([mq_rot_t[h] * scale_b, bias, zero_rows], axis=0).astype(BF16)
        mvt_ref[0, sl, :] = jnp.concatenate([head(mv_t, h), ones_rows], axis=0).astype(BF16)
        j, hh = divmod(h, 2)
        k_h = mk_rot[j] if hh == 0 else pltpu.roll(mk_rot[j], HEAD_LANES, axis=1)
        mk_ref[:, sl] = jnp.where(head_lanes, k_h, block_onehot).astype(BF16)


def _prep(x2, win, wmt, wqt, wqst, wk, wvt, qg, kvg, tables, B, T):
    N, D = x2.shape
    nt = T // ROW_TILE
    row = lambda b, c: (b * nt + c, 0)
    col = lambda b, c: (0, b * nt + c)
    full = lambda b, c: (0, 0)
    cat, sat, cbt, sbt = tables
    width = MLA_HEADS * SLOT

    def rows(w):
        return pl.BlockSpec((ROW_TILE, w), row)

    def cols(a):
        return pl.BlockSpec((a.shape[0], ROW_TILE), col)

    def whole(a):
        return pl.BlockSpec(a.shape, full)

    rowmajor = jax.ShapeDtypeStruct((N, width), BF16)
    transposed = jax.ShapeDtypeStruct((B * nt, width, ROW_TILE), BF16)
    t_spec = pl.BlockSpec((1, width, ROW_TILE), lambda b, c: (b * nt + c, 0, 0))
    return pl.pallas_call(
        _prep_kernel,
        out_shape=(transposed, rowmajor, transposed, transposed, rowmajor, transposed),
        grid=(B, nt),
        in_specs=[rows(D), whole(win), whole(wmt), whole(wqt), whole(wqst), whole(wk), whole(wvt), whole(qg), whole(kvg),
                  cols(cat), cols(sat), cols(cbt), cols(sbt)],
        out_specs=(t_spec, rows(width), t_spec, t_spec, rows(width), t_spec),
        scratch_shapes=[pltpu.VMEM((MOBA_HEADS * SUBLANES, _MH), F32)],
        compiler_params=pltpu.CompilerParams(dimension_semantics=("arbitrary", "arbitrary"),
                                             vmem_limit_bytes=VMEM_LIMIT_BYTES),
        name="prep",
    )(x2, win, wmt, wqt, wqst, wk, wvt, qg, kvg, cat, sat, cbt, sbt)


def _attn_kernel(qt_ref, k_ref, vt_ref, o_ref, s_scr):
    i = pl.program_id(2)
    nh = ATTN_HEADS_PER_STEP
    slot = lambda h: slice(h * SLOT, (h + 1) * SLOT)
    nblk = i + 1
    npair = nblk // 2
    q_pos = i * TQ + lax.broadcasted_iota(jnp.int32, (1, TQ), 1)

    def fold(s):
        out = s[0:SUBLANES]
        for t in range(1, s.shape[0] // SUBLANES):
            out = jnp.maximum(out, s[t * SUBLANES:(t + 1) * SUBLANES])
        return out

    def score_blocks(j, n, mrun):
        j0 = pl.multiple_of(j * TQ, TQ)
        visible = (j0 + lax.broadcasted_iota(jnp.int32, (n * TQ, TQ), 0)) <= q_pos
        out = []
        for h in range(nh):
            s = _dot(k_ref[pl.ds(j0, n * TQ), slot(h)], qt_ref[0, slot(h), :])
            s = jnp.where(visible, s, -jnp.inf)
            s_scr[h, pl.ds(j, n)] = s.reshape(n, TQ, TQ)
            out.append(jnp.maximum(mrun[h], fold(s)))
        return tuple(out)

    mrun = tuple(jnp.full((SUBLANES, TQ), -jnp.inf, F32) for _ in range(nh))
    mrun = lax.fori_loop(0, npair, lambda jj, m: score_blocks(2 * jj, 2, m), mrun)
    mrun = lax.fori_loop(2 * npair, nblk, lambda j, m: score_blocks(j, 1, m), mrun)
    ms = [jnp.max(m, axis=0, keepdims=True) for m in mrun]

    def accumulate(j, n, acc):
        out = []
        for h in range(nh):
            p = jnp.exp2(s_scr[h, pl.ds(j, n)].reshape(n * TQ, TQ) - ms[h]).astype(BF16)
            vt = jnp.concatenate([vt_ref[j + t, slot(h), :] for t in range(n)], axis=1)
            out.append(acc[h] + _dot(vt, p))
        return tuple(out)

    acc = tuple(jnp.zeros((SLOT, TQ), F32) for _ in range(nh))
    acc = lax.fori_loop(0, npair, lambda jj, a: accumulate(2 * jj, 2, a), acc)
    acc = lax.fori_loop(2 * npair, nblk, lambda j, a: accumulate(j, 1, a), acc)
    outs = [a[0:HEAD_LANES] / a[ONES_LANE:ONES_LANE + 1] for a in acc]
    for jj in range(nh // 2):
        both = jnp.concatenate([outs[2 * jj], outs[2 * jj + 1]], axis=0)
        o_ref[:, jj * LANES:(jj + 1) * LANES] = both.T.astype(o_ref.dtype)


def _attention(qt, k, vt, B, T, name):
    N = k.shape[0]
    nq = T // TQ
    nh = ATTN_HEADS_PER_STEP
    heads = k.shape[1] // SLOT
    return pl.pallas_call(
        _attn_kernel,
        out_shape=jax.ShapeDtypeStruct((N, heads * HEAD_LANES), BF16),
        grid=(B, heads // nh, nq),
        in_specs=[pl.BlockSpec((1, nh * SLOT, TQ), lambda b, g, i: (b * nq + i, g, 0)),
                  pl.BlockSpec((T, nh * SLOT), lambda b, g, i: (b, g)),
                  pl.BlockSpec((nq, nh * SLOT, TQ), lambda b, g, i: (b, g, 0))],
        out_specs=pl.BlockSpec((TQ, nh * HEAD_LANES), lambda b, g, i: (b * nq + i, g)),
        scratch_shapes=[pltpu.VMEM((nh, nq, TQ, TQ), F32)],
        compiler_params=pltpu.CompilerParams(dimension_semantics=("arbitrary", "arbitrary", "arbitrary"),
                                             vmem_limit_bytes=VMEM_LIMIT_BYTES),
        name=name,
    )(qt, k, vt)


def _oproj_kernel(alpha, a_ref, m_ref, x_ref, wo_ref, g_ref, b_ref, wr_ref, br_ref,
                  x1_ref, x1t_ref, route_ref, gates_ref, cnt_ref, carry_scr):
    i = pl.program_id(0)

    @pl.when(i == 0)
    def _():
        carry_scr[...] = jnp.zeros_like(carry_scr)

    mix = _dot(jnp.concatenate([a_ref[...], m_ref[...]], axis=1), wo_ref[...])
    x1 = _layer_norm(alpha * x_ref[...] + mix, g_ref[...], b_ref[...])
    x1_ref[...] = x1
    _rows_to_tiles(x1t_ref, x1)

    ne = br_ref.shape[0]
    rows = x1.shape[0]
    x_hi = x1.astype(BF16)
    x_lo = (x1 - x_hi.astype(F32)).astype(BF16)
    both = _dot(x_hi, wr_ref[...])
    logits_rm = both[:, :LANES] + both[:, LANES:] + _dot(x_lo, wr_ref[:, :LANES])
    logits = logits_rm.T[0:ne] + br_ref[...]
    expert = lax.broadcasted_iota(jnp.int32, logits.shape, 0)
    expert_f = expert.astype(F32)
    vals, idxs = [], []
    work = logits
    for _ in range(MOE_TOPK):
        mx = jnp.max(work, axis=0, keepdims=True)
        ix = jnp.min(jnp.where(work == mx, expert_f, float(ne)), axis=0, keepdims=True).astype(jnp.int32)
        vals.append(mx)
        idxs.append(ix)
        work = jnp.where(expert == ix, -jnp.inf, work)
    exps = [jnp.exp(v - vals[0]) for v in vals]
    den = exps[0]
    for e in exps[1:]:
        den = den + e

    onehot = jnp.zeros(logits.shape, F32)
    for ix in idxs:
        onehot = onehot + (expert == ix).astype(F32)
    r = lax.broadcasted_iota(jnp.int32, (rows, rows), 0)
    cidx = lax.broadcasted_iota(jnp.int32, (rows, rows), 1)
    earlier = (r < cidx).astype(BF16)
    carry = carry_scr[:, 0:1]
    before = _dot(onehot.astype(BF16), earlier) + carry
    row8 = lax.broadcasted_iota(jnp.int32, (SUBLANES, rows), 0)
    route = jnp.zeros((SUBLANES, rows), jnp.int32)
    gates = jnp.zeros((SUBLANES, rows), F32)
    for k in range(MOE_TOPK):
        rank = jnp.sum(jnp.where(expert == idxs[k], before, 0.0), axis=0, keepdims=True).astype(jnp.int32)
        route = jnp.where(row8 == k, idxs[k], route)
        route = jnp.where(row8 == MOE_TOPK + k, rank, route)
        gates = jnp.where(row8 == k, exps[k] / den, gates)
    route_ref[...] = route
    gates_ref[...] = jnp.concatenate([gates, jnp.zeros((LANES - SUBLANES, rows), F32)], axis=0).T
    new_carry = carry + jnp.sum(onehot, axis=1, keepdims=True)
    carry_scr[...] = jnp.broadcast_to(new_carry, carry_scr.shape)
    cnt_ref[...] = jnp.broadcast_to(new_carry, cnt_ref.shape)


def _oproj(a, m, x2, wo, g1, b1, wr, br, alpha):
    N, D = x2.shape
    nt = N // ROW_TILE
    row = lambda i: (i, 0)
    full = lambda i: (0, 0)
    return pl.pallas_call(
        functools.partial(_oproj_kernel, alpha),
        out_shape=(jax.ShapeDtypeStruct((N, D), F32),
                   jax.ShapeDtypeStruct((N * SUBLANES, LANES), F32),
                   jax.ShapeDtypeStruct((SUBLANES, N), jnp.int32),
                   jax.ShapeDtypeStruct((N, LANES), F32),
                   jax.ShapeDtypeStruct((br.shape[0], LANES), F32)),
        grid=(nt,),
        in_specs=[pl.BlockSpec((ROW_TILE, a.shape[1]), row), pl.BlockSpec((ROW_TILE, m.shape[1]), row),
                  pl.BlockSpec((ROW_TILE, D), row), pl.BlockSpec(wo.shape, full),
                  pl.BlockSpec(g1.shape, full), pl.BlockSpec(b1.shape, full),
                  pl.BlockSpec(wr.shape, full), pl.BlockSpec(br.shape, full)],
        out_specs=(pl.BlockSpec((ROW_TILE, D), row), pl.BlockSpec((ROW_TILE * SUBLANES, LANES), row),
                   pl.BlockSpec((SUBLANES, ROW_TILE), lambda i: (0, i)),
                   pl.BlockSpec((ROW_TILE, LANES), row), pl.BlockSpec((br.shape[0], LANES), full)),
        scratch_shapes=[pltpu.VMEM((br.shape[0], LANES), F32)],
        compiler_params=pltpu.CompilerParams(dimension_semantics=("arbitrary",),
                                             vmem_limit_bytes=VMEM_LIMIT_BYTES),
        name="oproj_router",
    )(a, m, x2, wo, g1, b1, wr, br)


def _dispatch_kernel(dest_ref, tail_ref, x1_ref, xs_ref, zero_scr, sem):
    i = pl.program_id(0)

    @pl.when(i == 0)
    def _():
        zero_scr[...] = jnp.zeros_like(zero_scr)

        def tail_copy(e):
            return pltpu.make_async_copy(zero_scr, xs_ref.at[pl.ds(pl.multiple_of(tail_ref[e] * SUBLANES, GROUP_SUB), GROUP_SUB)], sem)

        def start(e, _):
            @pl.when(tail_ref[e] >= 0)
            def _():
                tail_copy(e).start()
            return 0

        def wait(e, _):
            @pl.when(tail_ref[e] >= 0)
            def _():
                tail_copy(e).wait()
            return 0

        lax.fori_loop(0, N_EXPERTS, start, 0)
        lax.fori_loop(0, N_EXPERTS, wait, 0)

        def spare_copy(blk):
            return pltpu.make_async_copy(zero_scr, xs_ref.at[pl.ds(pl.multiple_of(blk * GROUP_SUB, GROUP_SUB), GROUP_SUB)], sem)

        def start_spare(blk, _):
            spare_copy(blk).start()
            return 0

        def wait_spare(blk, _):
            spare_copy(blk).wait()
            return 0

        nblk = xs_ref.shape[0] // GROUP_SUB
        lax.fori_loop(tail_ref[N_EXPERTS], nblk, start_spare, 0)
        lax.fori_loop(tail_ref[N_EXPERTS], nblk, wait_spare, 0)

    base = i * DISPATCH_TILE
    n_tok = pl.num_programs(0) * DISPATCH_TILE

    def row_copy(t, k):
        d = dest_ref[k * n_tok + base + t]
        return pltpu.make_async_copy(x1_ref.at[pl.ds(pl.multiple_of(t * SUBLANES, SUBLANES), SUBLANES)],
                                     xs_ref.at[pl.ds(pl.multiple_of(d * SUBLANES, SUBLANES), SUBLANES)], sem)

    def start_rows(t, _):
        for k in range(MOE_TOPK):
            row_copy(t, k).start(priority=k % DMA_PRIORITIES)
        return 0

    lax.fori_loop(0, DISPATCH_TILE, start_rows, 0, unroll=ISSUE_UNROLL)
    for k in range(MOE_TOPK):
        pltpu.make_async_copy(x1_ref, xs_ref.at[pl.ds(0, DISPATCH_TILE * SUBLANES)], sem).wait()


def _dispatch(dest, tail, x1t, n_rows):
    return pl.pallas_call(
        _dispatch_kernel,
        out_shape=jax.ShapeDtypeStruct((n_rows * SUBLANES, LANES), F32),
        grid_spec=pltpu.PrefetchScalarGridSpec(
            num_scalar_prefetch=2,
            grid=(x1t.shape[0] // (DISPATCH_TILE * SUBLANES),),
            in_specs=[pl.BlockSpec((DISPATCH_TILE * SUBLANES, LANES), lambda i, d, t: (i, 0))],
            out_specs=pl.BlockSpec(memory_space=pl.ANY),
            scratch_shapes=[pltpu.VMEM((GROUP_SUB, LANES), F32), pltpu.SemaphoreType.DMA(())],
        ),
        compiler_params=pltpu.CompilerParams(dimension_semantics=("arbitrary",),
                                             vmem_limit_bytes=VMEM_LIMIT_BYTES),
        name="dispatch",
    )(dest, tail, x1t)


def _experts_kernel(be_ref, slot_ref, nxt_ref, nused_ref, x_ref, wg_hbm, bg_ref, wu_hbm, bu_ref, wd_hbm, bd_ref,
                    y_ref, wg_st, wu_st, wd_st, wg_bf, wu_bf, wd_bf, sems):
    i = pl.program_id(0)
    prev = be_ref[jnp.maximum(i - 1, 0)]
    changed = (i == 0) | (be_ref[i] != prev)
    active = i < nused_ref[0]
    slot = slot_ref[i]

    def weight_copies(expert, s):
        return (pltpu.make_async_copy(wg_hbm.at[expert], wg_st.at[s], sems.at[s, 0]),
                pltpu.make_async_copy(wu_hbm.at[expert], wu_st.at[s], sems.at[s, 1]),
                pltpu.make_async_copy(wd_hbm.at[expert], wd_st.at[s], sems.at[s, 2]))

    @pl.when(i == 0)
    def _():
        for cp in weight_copies(be_ref[0], slot):
            cp.start()

    @pl.when(active & changed)
    def _():
        for cp in weight_copies(be_ref[i], slot):
            cp.wait()

        @pl.when(nxt_ref[i] >= 0)
        def _():
            for cp in weight_copies(nxt_ref[i], 1 - slot):
                cp.start()

        wg_bf[...] = wg_st[slot].astype(BF16)
        wu_bf[...] = wu_st[slot].astype(BF16)
        wd_bf[...] = wd_st[slot].astype(BF16)

    @pl.when(active)
    def _():
        xb = _tiles_to_rows(x_ref, MOE_GROUP).astype(BF16)
        g = jnp.minimum(_dot(xb, wg_bf[...]) + bg_ref[0], SWIGLU_LIMIT)
        u = jnp.clip(_dot(xb, wu_bf[...]) + bu_ref[0], -SWIGLU_LIMIT, SWIGLU_LIMIT)
        h = g * (1.0 / (1.0 + jnp.exp(-SWIGLU_ALPHA * g))) * (u + 1.0)
        _rows_to_tiles(y_ref, _dot(h.astype(BF16), wd_bf[...]) + bd_ref[0])

    @pl.when(jnp.logical_not(active))
    def _():
        y_ref[...] = jnp.zeros_like(y_ref)


def _experts(blk_expert, blk_slot, blk_next, n_used, xs, wg, bg, wu, bu, wd, bd):
    E, D, F = wg.shape
    nblk = xs.shape[0] // GROUP_SUB

    def rowmap(i, be, sl, nx, nu):
        return (jnp.minimum(i, nu[0] - 1), 0)

    def bmap(i, be, sl, nx, nu):
        return (be[i], 0, 0)

    hbm = pl.BlockSpec(memory_space=pl.ANY)
    return pl.pallas_call(
        _experts_kernel,
        out_shape=jax.ShapeDtypeStruct(xs.shape, F32),
        grid_spec=pltpu.PrefetchScalarGridSpec(
            num_scalar_prefetch=4,
            grid=(nblk,),
            in_specs=[pl.BlockSpec((GROUP_SUB, LANES), rowmap),
                      hbm, pl.BlockSpec((1, 1, F), bmap),
                      hbm, pl.BlockSpec((1, 1, F), bmap),
                      hbm, pl.BlockSpec((1, 1, D), bmap)],
            out_specs=pl.BlockSpec((GROUP_SUB, LANES), lambda i, be, sl, nx, nu: (i, 0)),
            scratch_shapes=[pltpu.VMEM((2, D, F), F32), pltpu.VMEM((2, D, F), F32), pltpu.VMEM((2, F, D), F32),
                            pltpu.VMEM((D, F), BF16), pltpu.VMEM((D, F), BF16), pltpu.VMEM((F, D), BF16),
                            pltpu.SemaphoreType.DMA((2, 3))],
        ),
        compiler_params=pltpu.CompilerParams(dimension_semantics=("arbitrary",),
                                             vmem_limit_bytes=VMEM_LIMIT_BYTES),
        name="experts",
    )(blk_expert, blk_slot, blk_next, n_used, xs, wg, bg, wu, bu, wd, bd)


def _combine_kernel(alpha, dest_ref, gates_ref, x1_ref, g_ref, b_ref, y_ref, o_ref, ybuf, sems):
    i = pl.program_id(0)
    buf = i % 2

    n_tok = pl.num_programs(0) * COMBINE_TILE

    def gather_tile(tile, into):
        base = tile * COMBINE_TILE

        def start_rows(t, _):
            for k in range(MOE_TOPK):
                d = dest_ref[k * n_tok + base + t]
                pltpu.make_async_copy(y_ref.at[pl.ds(pl.multiple_of(d * SUBLANES, SUBLANES), SUBLANES)],
                                      ybuf.at[into, k, pl.ds(pl.multiple_of(t * SUBLANES, SUBLANES), SUBLANES)],
                                      sems.at[into]).start(priority=k % DMA_PRIORITIES)
            return 0

        lax.fori_loop(0, COMBINE_TILE, start_rows, 0, unroll=ISSUE_UNROLL)

    @pl.when(i == 0)
    def _():
        gather_tile(0, 0)

    @pl.when(i + 1 < pl.num_programs(0))
    def _():
        gather_tile(i + 1, 1 - buf)

    for k in range(MOE_TOPK):
        pltpu.make_async_copy(y_ref.at[pl.ds(0, COMBINE_TILE * SUBLANES)], ybuf.at[buf, k], sems.at[buf]).wait()

    gates = gates_ref[...]
    ffn = gates[:, 0:1] * _tiles_to_rows(ybuf.at[buf, 0], COMBINE_TILE)
    for k in range(1, MOE_TOPK):
        ffn = ffn + gates[:, k:k + 1] * _tiles_to_rows(ybuf.at[buf, k], COMBINE_TILE)
    o_ref[...] = _layer_norm(alpha * x1_ref[...] + ffn, g_ref[...], b_ref[...])


def _combine(dest, gates, x1, g2, b2, y_rows, alpha):
    N, D = x1.shape
    row = lambda i, d: (i, 0)
    full = lambda i, d: (0, 0)
    return pl.pallas_call(
        functools.partial(_combine_kernel, alpha),
        out_shape=jax.ShapeDtypeStruct((N, D), F32),
        grid_spec=pltpu.PrefetchScalarGridSpec(
            num_scalar_prefetch=1,
            grid=(N // COMBINE_TILE,),
            in_specs=[pl.BlockSpec((COMBINE_TILE, LANES), row), pl.BlockSpec((COMBINE_TILE, D), row),
                      pl.BlockSpec(g2.shape, full), pl.BlockSpec(b2.shape, full),
                      pl.BlockSpec(memory_space=pl.ANY)],
            out_specs=pl.BlockSpec((COMBINE_TILE, D), row),
            scratch_shapes=[pltpu.VMEM((2, MOE_TOPK, COMBINE_TILE * SUBLANES, LANES), F32),
                            pltpu.SemaphoreType.DMA((2,))],
        ),
        compiler_params=pltpu.CompilerParams(dimension_semantics=("arbitrary",),
                                             vmem_limit_bytes=VMEM_LIMIT_BYTES),
        name="combine",
    )(dest, gates, x1, g2, b2, y_rows)


def _rot_partner(w, half):
    return jnp.concatenate([-w[..., half:2 * half], w[..., :half]], axis=-1)


def _layer_weights(w_in, w_q_b, w_kv_b):
    D = w_in.shape[0]
    o1 = Q_LORA
    o2 = o1 + KV_LORA
    o3 = o2 + MLA_ROPE
    w_ql, w_kvl, w_kr = w_in[:, :o1], w_in[:, o1:o2], w_in[:, o2:o3]
    w_mq, w_mk, w_mv = w_in[:, o3:o3 + _MH], w_in[:, o3 + _MH:o3 + 2 * _MH], w_in[:, o3 + 2 * _MH:]
    zpad = lambda n: jnp.zeros((D, n), w_in.dtype)
    tail = SLOT - MLA_NOPE - MLA_ROPE
    kr_slot = jnp.concatenate([zpad(MLA_NOPE), w_kr, zpad(tail)], axis=1)
    krs_slot = jnp.concatenate([zpad(MLA_NOPE), _rot_partner(w_kr, MLA_ROPE // 2), zpad(tail)], axis=1)

    def moba_partner(w):
        w3 = w.reshape(D, MOBA_HEADS, MOBA_HD)
        part = jnp.concatenate([_rot_partner(w3[..., :MOBA_ROT], MOBA_ROT // 2),
                                jnp.zeros((D, MOBA_HEADS, MOBA_HD - MOBA_ROT), w.dtype)], axis=-1)
        return part.reshape(D, _MH)

    win = jnp.concatenate([w_ql, w_kvl, kr_slot, krs_slot, w_mk, moba_partner(w_mk)], axis=1).astype(BF16)
    mq_partner = _rot_partner(w_mq.reshape(D, MOBA_HEADS, MOBA_HD)[..., :MOBA_ROT], MOBA_ROT // 2)
    wmt = jnp.concatenate([w_mq, mq_partner.reshape(D, MOBA_HEADS * MOBA_ROT), w_mv], axis=1).T.astype(BF16)

    wq3 = w_q_b.reshape(Q_LORA, MLA_HEADS, MLA_NOPE + MLA_ROPE)
    wqs = _rot_partner(wq3[..., MLA_NOPE:], MLA_ROPE // 2).reshape(Q_LORA, MLA_HEADS * MLA_ROPE)
    wkv3 = w_kv_b.reshape(KV_LORA, MLA_HEADS, MLA_NOPE + MLA_V)
    wk = jnp.concatenate([wkv3[..., :MLA_NOPE], jnp.zeros((KV_LORA, MLA_HEADS, SLOT - MLA_NOPE), w_kv_b.dtype)],
                         axis=-1).reshape(KV_LORA, MLA_HEADS * SLOT).astype(BF16)
    wv = wkv3[..., MLA_NOPE:].reshape(KV_LORA, MLA_HEADS * MLA_V)
    return win, wmt, w_q_b.T.astype(BF16), wqs.T.astype(BF16), wk, wv.T.astype(BF16)


def _rope_tables(positions):
    pos = positions.astype(F32).reshape(1, -1)

    def cs(d_rot):
        inv_freq = ROPE_THETA ** (-jnp.arange(0, d_rot, 2, dtype=F32) / d_rot)
        ang = jnp.concatenate([inv_freq, inv_freq]).reshape(d_rot, 1) * pos
        return jnp.cos(ang), jnp.sin(ang)

    return cs(MLA_ROPE) + cs(MOBA_ROT)


def kernel(x, positions, w_in, q_a_norm, w_q_b, kv_a_norm, w_kv_b, w_o, ln1_g, ln1_b, w_router, b_router,
           w_gate, b_gate, w_up, b_up, w_down, b_down, ln2_g, ln2_b):
    B, T, D = x.shape
    depth = w_in.shape[0]
    alpha = (2.0 * depth) ** 0.25
    N = B * T
    assert T % MOBA_BLOCK == 0 and T // MOBA_BLOCK <= SUBLANES and N % ROW_TILE == 0
    assert D == SUBLANES * LANES
    n_asg = N * MOE_TOPK
    n_rows = n_asg + N_EXPERTS * MOE_GROUP
    tables = _rope_tables(positions)
    h = x.reshape(N, D)
    for l in range(depth):
        win, wmt, wqt, wqst, wk, wvt = _layer_weights(w_in[l], w_q_b[l], w_kv_b[l])
        qa, ka, va, mq, mk, mv = _prep(h, win, wmt, wqt, wqst, wk, wvt, q_a_norm[l].reshape(1, -1),
                                       kv_a_norm[l].reshape(1, -1), tables, B, T)
        a = _attention(qa, ka, va, B, T, "mla_attention")
        m = _attention(mq, mk, mv, B, T, "moba_attention")
        wr_pad = jnp.concatenate([w_router[l], jnp.zeros((D, LANES - N_EXPERTS), F32)], axis=1)
        wr_hi = wr_pad.astype(BF16)
        wr = jnp.concatenate([wr_hi, (wr_pad - wr_hi.astype(F32)).astype(BF16)], axis=1)
        br = b_router[l].reshape(N_EXPERTS, 1)
        x1, x1t, route, gates, cnt = _oproj(a, m, h, w_o[l].astype(BF16), ln1_g[l].reshape(1, D), ln1_b[l].reshape(1, D),
                                       wr, br, alpha)
        counts = cnt[:, 0].astype(jnp.int32)
        padded = (counts + MOE_GROUP - 1) // MOE_GROUP * MOE_GROUP
        pad_end = jnp.cumsum(padded).astype(jnp.int32)
        pad_start = pad_end - padded
        e_idx = route[:MOE_TOPK]
        group_start = jnp.sum(jnp.where(e_idx[..., None] == jnp.arange(N_EXPERTS, dtype=jnp.int32), pad_start, 0), axis=-1)
        dest = (group_start + route[MOE_TOPK:2 * MOE_TOPK]).reshape(n_asg)
        n_used = (pad_end[-1:] // MOE_GROUP).astype(jnp.int32)
        tail = jnp.concatenate([jnp.where(padded > 0, pad_end - MOE_GROUP, -1).astype(jnp.int32), n_used])
        blk_start = jnp.arange(n_rows // MOE_GROUP, dtype=jnp.int32) * MOE_GROUP
        blk_expert = jnp.minimum(jnp.sum((blk_start[:, None] >= pad_end[None, :]).astype(jnp.int32), axis=1),
                                 N_EXPERTS - 1)
        nblk = n_rows // MOE_GROUP
        starts_group = jnp.concatenate([jnp.ones((1,), jnp.int32), (blk_expert[1:] != blk_expert[:-1]).astype(jnp.int32)])
        blk_slot = (jnp.cumsum(starts_group) - 1) % 2
        next_start = pad_end[blk_expert] // MOE_GROUP
        blk_next = jnp.where(next_start < n_used[0], blk_expert[jnp.minimum(next_start, nblk - 1)], -1).astype(jnp.int32)
        xs = _dispatch(dest, tail, x1t, n_rows)
        y_rows = _experts(blk_expert, blk_slot.astype(jnp.int32), blk_next, n_used, xs, w_gate[l], b_gate[l].reshape(N_EXPERTS, 1, -1),
                          w_up[l], b_up[l].reshape(N_EXPERTS, 1, -1), w_down[l], b_down[l].reshape(N_EXPERTS, 1, -1))
        h = _combine(dest, gates, x1, ln2_g[l].reshape(1, D), ln2_b[l].reshape(1, D), y_rows, alpha)
    return h.reshape(B, T, D)
```

```python
import functools
import math

import jax
import jax.numpy as jnp
from jax import lax
from jax.experimental import pallas as pl
from jax.experimental.pallas import tpu as pltpu

ROPE_THETA = 500000.0
MLA_HEADS = 8
MLA_NOPE = 64
MLA_ROPE = 32
MLA_V = 64
Q_LORA = 256
KV_LORA = 128
MOBA_HEADS = 8
MOBA_HD = 64
MOBA_ROT = MOBA_HD // 4
MOBA_BLOCK = 256
MOBA_TOPK = 3
N_EXPERTS = 32
MOE_TOPK = 4
SWIGLU_LIMIT = 7.0
SWIGLU_ALPHA = 1.702
RMS_EPS = 1e-6
LN_EPS = 1e-5

LANES = 128
SUBLANES = 8
VMEM_LIMIT_BYTES = 56 * 1024 * 1024

SLOT = LANES
TQ = MOBA_BLOCK
ROW_TILE = 256
MOE_GROUP = 256
GROUP_SUB = MOE_GROUP * SUBLANES
COMBINE_TILE = 256
ISSUE_UNROLL = 4
GATHER_UNROLL = 16
DMA_PRIORITIES = 2
HEAD_LANES = 64
ONES_LANE = 64
BIAS_LANE = 64
ATTN_HEADS_PER_STEP = 8
LOG2E = math.log2(math.e)
NEG_BIG = -(2.0 ** 100)

F32 = jnp.float32
BF16 = jnp.bfloat16
NT_DIMS = (((1,), (1,)), ((), ()))


def _dot(a, b, precision=None):
    return jnp.dot(a, b, preferred_element_type=F32, precision=precision)


def _dot_nt(a, b, precision=None):
    return lax.dot_general(a, b, NT_DIMS, preferred_element_type=F32, precision=precision)


def _rows_to_tiles(ref, x):
    rows = x.shape[0]
    for c in range(SUBLANES):
        ref[pl.ds(c, rows, stride=SUBLANES), :] = x[:, c * LANES:(c + 1) * LANES]


def _tiles_to_rows(ref, rows):
    return jnp.concatenate([ref[pl.ds(c, rows, stride=SUBLANES), :] for c in range(SUBLANES)], axis=1)


def _rms(x, g):
    return x * lax.rsqrt(jnp.mean(x * x, axis=-1, keepdims=True) + RMS_EPS) * g


def _layer_norm(x, g, b):
    mu = jnp.mean(x, axis=-1, keepdims=True)
    xc = x - mu
    var = jnp.mean(xc * xc, axis=-1, keepdims=True)
    return xc * lax.rsqrt(var + LN_EPS) * g + b


_C_QL = 0
_C_KVL = _C_QL + Q_LORA
_C_KR = _C_KVL + KV_LORA
_C_KRS = _C_KR + SLOT
_C_MK = _C_KRS + SLOT
_MH = MOBA_HEADS * MOBA_HD
_C_MKS = _C_MK + _MH
_C_END = _C_MKS + _MH


def _prep_kernel(x_ref, win_ref, wmt_ref, wqt_ref, wqst_ref, wk_ref, wvt_ref, qg_ref, kvg_ref,
                 cat_ref, sat_ref, cbt_ref, sbt_ref,
                 qat_ref, ka_ref, vat_ref, mqt_ref, mk_ref, mvt_ref, kmean_scr):
    c = pl.program_id(1)
    xb = x_ref[...].astype(BF16)
    cat = cat_ref[...]
    sat = sat_ref[...]
    cbt = cbt_ref[...]
    sbt = sbt_ref[...]
    const = lambda v, n: jnp.full((n, ROW_TILE), v, F32)
    tail = SLOT - MLA_NOPE - MLA_ROPE
    rest = MOBA_HD - MOBA_ROT
    ca = jnp.concatenate([const(1.0, MLA_NOPE), cat, const(0.0, tail)], axis=0).T
    sa = jnp.concatenate([const(0.0, MLA_NOPE), sat, const(0.0, tail)], axis=0).T
    cb = jnp.concatenate([cbt, const(1.0, rest), cbt, const(1.0, rest)], axis=0).T
    sb = jnp.concatenate([sbt, const(0.0, rest), sbt, const(0.0, rest)], axis=0).T
    lane =lax.broadcasted_iota(jnp.int32, (ROW_TILE, SLOT), 1)
    head_lanes = lane < HEAD_LANES
    ones_rows = (lax.broadcasted_iota(jnp.int32, (SLOT - HEAD_LANES, ROW_TILE), 0) == ONES_LANE - HEAD_LANES).astype(F32)

    ql = _dot(xb, win_ref[:, _C_QL:_C_KVL])
    kvl = _dot(xb, win_ref[:, _C_KVL:_C_KR])
    kr = _dot(xb, win_ref[:, _C_KR:_C_KRS])
    krs = _dot(xb, win_ref[:, _C_KRS:_C_MK])
    qn = _rms(ql, qg_ref[...]).astype(BF16)
    kvn = _rms(kvl, kvg_ref[...]).astype(BF16)
    q_t = _dot_nt(wqt_ref[...], qn)
    qs_t = _dot_nt(wqst_ref[...], qn)
    kn = _dot(kvn, wk_ref[...])
    v_t = _dot_nt(wvt_ref[...], kvn)
    scale_a = LOG2E / math.sqrt(MLA_NOPE + MLA_ROPE)
    kro = kr * ca + krs * sa
    qd = MLA_NOPE + MLA_ROPE
    q_pad = jnp.zeros((SLOT - qd, ROW_TILE), F32)
    for h in range(MLA_HEADS):
        sl = slice(h * SLOT, (h + 1) * SLOT)
        q_rope = q_t[h * qd + MLA_NOPE:(h + 1) * qd] * cat + qs_t[h * MLA_ROPE:(h + 1) * MLA_ROPE] * sat
        q_slot = jnp.concatenate([q_t[h * qd:h * qd + MLA_NOPE], q_rope, q_pad], axis=0)
        qat_ref[0, sl, :] = (q_slot * scale_a).astype(BF16)
        ka_ref[:, sl] = (kn[:, sl] + kro).astype(BF16)
        vat_ref[0, sl, :] = jnp.concatenate([v_t[h * MLA_V:(h + 1) * MLA_V], ones_rows], axis=0).astype(BF16)

    mk = _dot(xb, win_ref[:, _C_MK:_C_MKS])
    mks = _dot(xb, win_ref[:, _C_MKS:_C_END])
    npair = _MH // LANES
    pair = lambda a, j: a[:, j * LANES:(j + 1) * LANES]
    mk_rot = [pair(mk, j) * cb + pair(mks, j) * sb for j in range(npair)]
    mk_all = jnp.concatenate(mk_rot, axis=1)

    nrot = MOBA_HEADS * MOBA_ROT
    mq_t = _dot_nt(wmt_ref[0:_MH, :], xb)
    mqs_t = _dot_nt(wmt_ref[_MH:_MH + nrot, :], xb)
    mv_t = _dot_nt(wmt_ref[_MH + nrot:2 * _MH + nrot, :], xb)
    head = lambda a, h: a[h * MOBA_HD:(h + 1) * MOBA_HD]
    mq_rot_t = [jnp.concatenate([head(mq_t, h)[0:MOBA_ROT] * cbt + mqs_t[h * MOBA_ROT:(h + 1) * MOBA_ROT] * sbt,
                                 head(mq_t, h)[MOBA_ROT:]], axis=0) for h in range(MOBA_HEADS)]

    nrow = MOBA_HEADS * SUBLANES
    @pl.when(c == 0)
    def _():
        kmean_scr[...] = jnp.zeros_like(kmean_scr)

    row_i = lax.broadcasted_iota(jnp.int32, (nrow, _MH), 0)
    lane_i = lax.broadcasted_iota(jnp.int32, (nrow, _MH), 1)
    kmean_c = jnp.mean(mk_all, axis=0, keepdims=True)
    put = ((row_i % SUBLANES) == c) & ((lane_i // MOBA_HD) == (row_i // SUBLANES))
    table = kmean_scr[...]
    gate_t = _dot(table, jnp.concatenate(mq_rot_t, axis=0), precision=lax.Precision.HIGHEST)
    kmean_scr[...] = jnp.where(put, jnp.broadcast_to(kmean_c, (nrow, _MH)), table)

    n_idx = lax.broadcasted_iota(jnp.int32, (SUBLANES, ROW_TILE), 0)
    valid = n_idx < c
    scale_b = LOG2E / math.sqrt(MOBA_HD)
    zero_rows = jnp.zeros((SLOT - HEAD_LANES - SUBLANES, ROW_TILE), F32)
    block_onehot = (lane == BIAS_LANE + c).astype(F32)
    for h in range(MOBA_HEADS):
        g = jnp.where(valid, gate_t[h * SUBLANES:(h + 1) * SUBLANES, :], -jnp.inf)
        rank = jnp.zeros((SUBLANES, ROW_TILE), jnp.int32)
        for k in range(1, SUBLANES):
            other = pltpu.roll(g, k, axis=0)
            other_n = pltpu.roll(n_idx, k, axis=0)
            beats = (other > g) | ((other == g) & (other_n < n_idx))
            rank = rank + beats.astype(jnp.int32)
        keep = (valid & (rank < MOBA_TOPK)) | (n_idx == c)
        bias = jnp.where(keep, 0.0, NEG_BIG)
        sl = slice(h * SLOT, (h + 1) * SLOT)
        mqt_ref[0, sl, :] = jnp.concatenate([mq_rot_t[h] * scale_b, bias, zero_rows], axis=0).astype(BF16)
        mvt_ref[0, sl, :] = jnp.concatenate([head(mv_t, h), ones_rows], axis=0).astype(BF16)
        j, hh = divmod(h, 2)
        k_h = mk_rot[j] if hh == 0 else pltpu.roll(mk_rot[j], HEAD_LANES, axis=1)
        mk_ref[:, sl] = jnp.where(head_lanes, k_h, block_onehot).astype(BF16)


def _prep(x2, win, wmt, wqt, wqst, wk, wvt, qg, kvg, tables, B, T):
    N, D = x2.shape
    nt = T // ROW_TILE
    row = lambda b, c: (b * nt + c, 0)
    col = lambda b, c: (0, b * nt + c)
    full = lambda b, c: (0, 0)
    cat, sat, cbt, sbt = tables
    width = MLA_HEADS * SLOT

    def rows(w):
        return pl.BlockSpec((ROW_TILE, w), row)

    def cols(a):
        return pl.BlockSpec((a.shape[0], ROW_TILE), col)

    def whole(a):
        return pl.BlockSpec(a.shape, full)

    rowmajor = jax.ShapeDtypeStruct((N, width), BF16)
    transposed = jax.ShapeDtypeStruct((B * nt, width, ROW_TILE), BF16)
    t_spec = pl.BlockSpec((1, width, ROW_TILE), lambda b, c: (b * nt + c, 0, 0))
    return pl.pallas_call(
        _prep_kernel,
        out_shape=(transposed, rowmajor, transposed, transposed, rowmajor, transposed),
        grid=(B, nt),
        in_specs=[rows(D), whole(win), whole(wmt), whole(wqt), whole(wqst), whole(wk), whole(wvt), whole(qg), whole(kvg),
                  cols(cat), cols(sat), cols(cbt), cols(sbt)],
        out_specs=(t_spec, rows(width), t_spec, t_spec, rows(width), t_spec),
        scratch_shapes=[pltpu.VMEM((MOBA_HEADS * SUBLANES, _MH), F32)],
        compiler_params=pltpu.CompilerParams(dimension_semantics=("arbitrary", "arbitrary"),
                                             vmem_limit_bytes=VMEM_LIMIT_BYTES),
        name="prep",
    )(x2, win, wmt, wqt, wqst, wk, wvt, qg, kvg, cat, sat, cbt, sbt)


def _attn_kernel(qt_ref, k_ref, vt_ref, o_ref, s_scr):
    i = pl.program_id(2)
    nh = ATTN_HEADS_PER_STEP
    slot = lambda h: slice(h * SLOT, (h + 1) * SLOT)
    nblk = i + 1
    npair = nblk // 2
    q_pos = i * TQ + lax.broadcasted_iota(jnp.int32, (1, TQ), 1)

    def fold(s):
        out = s[0:SUBLANES]
        for t in range(1, s.shape[0] // SUBLANES):
            out = jnp.maximum(out, s[t * SUBLANES:(t + 1) * SUBLANES])
        return out

    def score_blocks(j, n, mrun):
        j0 = pl.multiple_of(j * TQ, TQ)
        visible = (j0 + lax.broadcasted_iota(jnp.int32, (n * TQ, TQ), 0)) <= q_pos
        out = []
        for h in range(nh):
            s = _dot(k_ref[pl.ds(j0, n * TQ), slot(h)], qt_ref[0, slot(h), :])
            s = jnp.where(visible, s, -jnp.inf)
            s_scr[h, pl.ds(j, n)] = s.reshape(n, TQ, TQ)
            out.append(jnp.maximum(mrun[h], fold(s)))
        return tuple(out)

    mrun = tuple(jnp.full((SUBLANES, TQ), -jnp.inf, F32) for _ in range(nh))
    mrun = lax.fori_loop(0, npair, lambda jj, m: score_blocks(2 * jj, 2, m), mrun)
    mrun = lax.fori_loop(2 * npair, nblk, lambda j, m: score_blocks(j, 1, m), mrun)
    ms = [jnp.max(m, axis=0, keepdims=True) for m in mrun]

    def accumulate(j, n, acc):
        out = []
        for h in range(nh):
            p = jnp.exp2(s_scr[h, pl.ds(j, n)].reshape(n * TQ, TQ) - ms[h]).astype(BF16)
            vt = jnp.concatenate([vt_ref[j + t, slot(h), :] for t in range(n)], axis=1)
            out.append(acc[h] + _dot(vt, p))
        return tuple(out)

    acc = tuple(jnp.zeros((SLOT, TQ), F32) for _ in range(nh))
    acc = lax.fori_loop(0, npair, lambda jj, a: accumulate(2 * jj, 2, a), acc)
    acc = lax.fori_loop(2 * npair, nblk, lambda j, a: accumulate(j, 1, a), acc)
    outs = [a[0:HEAD_LANES] / a[ONES_LANE:ONES_LANE + 1] for a in acc]
    for jj in range(nh // 2):
        both = jnp.concatenate([outs[2 * jj], outs[2 * jj + 1]], axis=0)
        o_ref[:, jj * LANES:(jj + 1) * LANES] = both.T.astype(o_ref.dtype)


def _attention(qt, k, vt, B, T, name):
    N = k.shape[0]
    nq = T // TQ
    nh = ATTN_HEADS_PER_STEP
    heads = k.shape[1] // SLOT
    return pl.pallas_call(
        _attn_kernel,
        out_shape=jax.ShapeDtypeStruct((N, heads * HEAD_LANES), BF16),
        grid=(B, heads // nh, nq),
        in_specs=[pl.BlockSpec((1, nh * SLOT, TQ), lambda b, g, i: (b * nq + i, g, 0)),
                  pl.BlockSpec((T, nh * SLOT), lambda b, g, i: (b, g)),
                  pl.BlockSpec((nq, nh * SLOT, TQ), lambda b, g, i: (b, g, 0))],
        out_specs=pl.BlockSpec((TQ, nh * HEAD_LANES), lambda b, g, i: (b * nq + i, g)),
        scratch_shapes=[pltpu.VMEM((nh, nq, TQ, TQ), F32)],
        compiler_params=pltpu.CompilerParams(dimension_semantics=("arbitrary", "arbitrary", "arbitrary"),
                                             vmem_limit_bytes=VMEM_LIMIT_BYTES),
        name=name,
    )(qt, k, vt)


def _oproj_kernel(alpha, a_ref, m_ref, x_ref, wo_ref, g_ref, b_ref, wr_ref, br_ref,
                  x1_ref, x1t_ref, route_ref, gates_ref, cnt_ref, carry_scr):
    i = pl.program_id(0)

    @pl.when(i == 0)
    def _():
        carry_scr[...] = jnp.zeros_like(carry_scr)

    mix = _dot(jnp.concatenate([a_ref[...], m_ref[...]], axis=1), wo_ref[...])
    x1 = _layer_norm(alpha * x_ref[...] + mix, g_ref[...], b_ref[...])
    x1_ref[...] = x1
    _rows_to_tiles(x1t_ref, x1)

    ne = br_ref.shape[0]
    rows = x1.shape[0]
    x_hi = x1.astype(BF16)
    x_lo = (x1 - x_hi.astype(F32)).astype(BF16)
    both = _dot(x_hi, wr_ref[...])
    logits_rm = both[:, :LANES] + both[:, LANES:] + _dot(x_lo, wr_ref[:, :LANES])
    logits = logits_rm.T[0:ne] + br_ref[...]
    expert = lax.broadcasted_iota(jnp.int32, logits.shape, 0)
    expert_f = expert.astype(F32)
    vals, idxs = [], []
    work = logits
    for _ in range(MOE_TOPK):
        mx = jnp.max(work, axis=0, keepdims=True)
        ix = jnp.min(jnp.where(work == mx, expert_f, float(ne)), axis=0, keepdims=True).astype(jnp.int32)
        vals.append(mx)
        idxs.append(ix)
        work = jnp.where(expert == ix, -jnp.inf, work)
    exps = [jnp.exp(v - vals[0]) for v in vals]
    den = exps[0]
    for e in exps[1:]:
        den = den + e

    onehot = jnp.zeros(logits.shape, F32)
    for ix in idxs:
        onehot = onehot + (expert == ix).astype(F32)
    r = lax.broadcasted_iota(jnp.int32, (rows, rows), 0)
    cidx = lax.broadcasted_iota(jnp.int32, (rows, rows), 1)
    earlier = (r < cidx).astype(BF16)
    carry = carry_scr[:, 0:1]
    before = _dot(onehot.astype(BF16), earlier) + carry
    row8 = lax.broadcasted_iota(jnp.int32, (SUBLANES, rows), 0)
    route = jnp.zeros((SUBLANES, rows), jnp.int32)
    gates = jnp.zeros((SUBLANES, rows), F32)
    for k in range(MOE_TOPK):
        rank = jnp.sum(jnp.where(expert == idxs[k], before, 0.0), axis=0, keepdims=True).astype(jnp.int32)
        route = jnp.where(row8 == k, idxs[k], route)
        route = jnp.where(row8 == MOE_TOPK + k, rank, route)
        gates = jnp.where(row8 == k, exps[k] / den, gates)
    route_ref[...] = route
    gates_ref[...] = jnp.concatenate([gates, jnp.zeros((LANES - SUBLANES, rows), F32)], axis=0).T
    new_carry = carry + jnp.sum(onehot, axis=1, keepdims=True)
    carry_scr[...] = jnp.broadcast_to(new_carry, carry_scr.shape)
    cnt_ref[...] = jnp.broadcast_to(new_carry, cnt_ref.shape)


def _oproj(a, m, x2, wo, g1, b1, wr, br, alpha):
    N, D = x2.shape
    nt = N // ROW_TILE
    row = lambda i: (i, 0)
    full = lambda i: (0, 0)
    return pl.pallas_call(
        functools.partial(_oproj_kernel, alpha),
        out_shape=(jax.ShapeDtypeStruct((N, D), F32),
                   jax.ShapeDtypeStruct((N * SUBLANES, LANES), F32),
                   jax.ShapeDtypeStruct((SUBLANES, N), jnp.int32),
                   jax.ShapeDtypeStruct((N, LANES), F32),
                   jax.ShapeDtypeStruct((br.shape[0], LANES), F32)),
        grid=(nt,),
        in_specs=[pl.BlockSpec((ROW_TILE, a.shape[1]), row), pl.BlockSpec((ROW_TILE, m.shape[1]), row),
                  pl.BlockSpec((ROW_TILE, D), row), pl.BlockSpec(wo.shape, full),
                  pl.BlockSpec(g1.shape, full), pl.BlockSpec(b1.shape, full),
                  pl.BlockSpec(wr.shape, full), pl.BlockSpec(br.shape, full)],
        out_specs=(pl.BlockSpec((ROW_TILE, D), row), pl.BlockSpec((ROW_TILE * SUBLANES, LANES), row),
                   pl.BlockSpec((SUBLANES, ROW_TILE), lambda i: (0, i)),
                   pl.BlockSpec((ROW_TILE, LANES), row), pl.BlockSpec((br.shape[0], LANES), full)),
        scratch_shapes=[pltpu.VMEM((br.shape[0], LANES), F32)],
        compiler_params=pltpu.CompilerParams(dimension_semantics=("arbitrary",),
                                             vmem_limit_bytes=VMEM_LIMIT_BYTES),
        name="oproj_router",
    )(a, m, x2, wo, g1, b1, wr, br)


def _experts_kernel(be_ref, slot_ref, nxt_ref, nused_ref, rowtok_ref, x1t_hbm, wg_hbm, bg_ref, wu_hbm, bu_ref,
                    wd_hbm, bd_ref, y_ref, xbuf, wg_st, wu_st, wd_st, wg_bf, wu_bf, wd_bf, sems, xsems):
    i = pl.program_id(0)
    prev = be_ref[jnp.maximum(i - 1, 0)]
    changed = (i == 0) | (be_ref[i] != prev)
    active = i < nused_ref[0]
    slot = slot_ref[i]
    xslot = i % 2

    def gather_block(blk, into):
        base = blk * MOE_GROUP

        def start_rows(rg, _):
            for k in range(GATHER_UNROLL):
                r = rg * GATHER_UNROLL + k
                tok = rowtok_ref[base + r]
                pltpu.make_async_copy(x1t_hbm.at[pl.ds(pl.multiple_of(tok * SUBLANES, SUBLANES), SUBLANES)],
                                      xbuf.at[into, pl.ds(pl.multiple_of(r * SUBLANES, SUBLANES), SUBLANES)],
                                      xsems.at[into]).start(priority=k % DMA_PRIORITIES)
            return 0

        lax.fori_loop(0, MOE_GROUP // GATHER_UNROLL, start_rows, 0)

    @pl.when(i == 0)
    def _():
        gather_block(0, 0)

    @pl.when(i + 1 < nused_ref[0])
    def _():
        gather_block(i + 1, 1 - xslot)

    def weight_copies(expert, s):
        return (pltpu.make_async_copy(wg_hbm.at[expert], wg_st.at[s], sems.at[s, 0]),
                pltpu.make_async_copy(wu_hbm.at[expert], wu_st.at[s], sems.at[s, 1]),
                pltpu.make_async_copy(wd_hbm.at[expert], wd_st.at[s], sems.at[s, 2]))

    @pl.when(i == 0)
    def _():
        for cp in weight_copies(be_ref[0], slot):
            cp.start()

    @pl.when(active & changed)
    def _():
        for cp in weight_copies(be_ref[i], slot):
            cp.wait()

        @pl.when(nxt_ref[i] >= 0)
        def _():
            for cp in weight_copies(nxt_ref[i], 1 - slot):
                cp.start()

        wg_bf[...] = wg_st[slot].astype(BF16)
        wu_bf[...] = wu_st[slot].astype(BF16)
        wd_bf[...] = wd_st[slot].astype(BF16)

    @pl.when(active)
    def _():
        pltpu.make_async_copy(x1t_hbm.at[pl.ds(0, GROUP_SUB)], xbuf.at[xslot], xsems.at[xslot]).wait()
        xb = _tiles_to_rows(xbuf.at[xslot], MOE_GROUP).astype(BF16)
        g = jnp.minimum(_dot(xb, wg_bf[...]) + bg_ref[0], SWIGLU_LIMIT)
        u = jnp.clip(_dot(xb, wu_bf[...]) + bu_ref[0], -SWIGLU_LIMIT, SWIGLU_LIMIT)
        h = g * (1.0 / (1.0 + jnp.exp(-SWIGLU_ALPHA * g))) * (u + 1.0)
        _rows_to_tiles(y_ref, _dot(h.astype(BF16), wd_bf[...]) + bd_ref[0])

    @pl.when(jnp.logical_not(active))
    def _():
        y_ref[...] = jnp.zeros_like(y_ref)


def _experts(blk_expert, blk_slot, blk_next, n_used, row_tok, x1t, wg, bg, wu, bu, wd, bd):
    E, D, F = wg.shape
    nblk = row_tok.shape[0] // MOE_GROUP

    def bmap(i, be, sl, nx, nu, rt):
        return (be[i], 0, 0)

    hbm = pl.BlockSpec(memory_space=pl.ANY)
    return pl.pallas_call(
        _experts_kernel,
        out_shape=jax.ShapeDtypeStruct((nblk * GROUP_SUB, LANES), F32),
        grid_spec=pltpu.PrefetchScalarGridSpec(
            num_scalar_prefetch=5,
            grid=(nblk,),
            in_specs=[hbm,
                      hbm, pl.BlockSpec((1, 1, F), bmap),
                      hbm, pl.BlockSpec((1, 1, F), bmap),
                      hbm, pl.BlockSpec((1, 1, D), bmap)],
            out_specs=pl.BlockSpec((GROUP_SUB, LANES), lambda i, be, sl, nx, nu, rt: (i, 0)),
            scratch_shapes=[pltpu.VMEM((2, GROUP_SUB, LANES), F32),
                            pltpu.VMEM((2, D, F), F32), pltpu.VMEM((2, D, F), F32), pltpu.VMEM((2, F, D), F32),
                            pltpu.VMEM((D, F), BF16), pltpu.VMEM((D, F), BF16), pltpu.VMEM((F, D), BF16),
                            pltpu.SemaphoreType.DMA((2, 3)), pltpu.SemaphoreType.DMA((2,))],
        ),
        compiler_params=pltpu.CompilerParams(dimension_semantics=("arbitrary",),
                                             vmem_limit_bytes=VMEM_LIMIT_BYTES),
        name="experts",
    )(blk_expert, blk_slot, blk_next, n_used, row_tok, x1t, wg, bg, wu, bu, wd, bd)


def _combine_kernel(alpha, dest_ref, gates_ref, x1_ref, g_ref, b_ref, y_ref, o_ref, ybuf, sems):
    i = pl.program_id(0)
    buf = i % 2

    n_tok = pl.num_programs(0) * COMBINE_TILE

    def gather_tile(tile, into):
        base = tile * COMBINE_TILE

        def start_rows(t, _):
            for k in range(MOE_TOPK):
                d = dest_ref[k * n_tok + base + t]
                pltpu.make_async_copy(y_ref.at[pl.ds(pl.multiple_of(d * SUBLANES, SUBLANES), SUBLANES)],
                                      ybuf.at[into, k, pl.ds(pl.multiple_of(t * SUBLANES, SUBLANES), SUBLANES)],
                                      sems.at[into]).start(priority=k % DMA_PRIORITIES)
            return 0

        lax.fori_loop(0, COMBINE_TILE, start_rows, 0, unroll=ISSUE_UNROLL)

    @pl.when(i == 0)
    def _():
        gather_tile(0, 0)

    @pl.when(i + 1 < pl.num_programs(0))
    def _():
        gather_tile(i + 1, 1 - buf)

    for k in range(MOE_TOPK):
        pltpu.make_async_copy(y_ref.at[pl.ds(0, COMBINE_TILE * SUBLANES)], ybuf.at[buf, k], sems.at[buf]).wait()

    gates = gates_ref[...]
    ffn = gates[:, 0:1] * _tiles_to_rows(ybuf.at[buf, 0], COMBINE_TILE)
    for k in range(1, MOE_TOPK):
        ffn = ffn + gates[:, k:k + 1] * _tiles_to_rows(ybuf.at[buf, k], COMBINE_TILE)
    o_ref[...] = _layer_norm(alpha * x1_ref[...] + ffn, g_ref[...], b_ref[...])


def _combine(dest, gates, x1, g2, b2, y_rows, alpha):
    N, D = x1.shape
    row = lambda i, d: (i, 0)
    full = lambda i, d: (0, 0)
    return pl.pallas_call(
        functools.partial(_combine_kernel, alpha),
        out_shape=jax.ShapeDtypeStruct((N, D), F32),
        grid_spec=pltpu.PrefetchScalarGridSpec(
            num_scalar_prefetch=1,
            grid=(N // COMBINE_TILE,),
            in_specs=[pl.BlockSpec((COMBINE_TILE, LANES), row), pl.BlockSpec((COMBINE_TILE, D), row),
                      pl.BlockSpec(g2.shape, full), pl.BlockSpec(b2.shape, full),
                      pl.BlockSpec(memory_space=pl.ANY)],
            out_specs=pl.BlockSpec((COMBINE_TILE, D), row),
            scratch_shapes=[pltpu.VMEM((2, MOE_TOPK, COMBINE_TILE * SUBLANES, LANES), F32),
                            pltpu.SemaphoreType.DMA((2,))],
        ),
        compiler_params=pltpu.CompilerParams(dimension_semantics=("arbitrary",),
                                             vmem_limit_bytes=VMEM_LIMIT_BYTES),
        name="combine",
    )(dest, gates, x1, g2, b2, y_rows)


def _rot_partner(w, half):
    return jnp.concatenate([-w[..., half:2 * half], w[..., :half]], axis=-1)


def _layer_weights(w_in, w_q_b, w_kv_b):
    D = w_in.shape[0]
    o1 = Q_LORA
    o2 = o1 + KV_LORA
    o3 = o2 + MLA_ROPE
    w_ql, w_kvl, w_kr = w_in[:, :o1], w_in[:, o1:o2], w_in[:, o2:o3]
    w_mq, w_mk, w_mv = w_in[:, o3:o3 + _MH], w_in[:, o3 + _MH:o3 + 2 * _MH], w_in[:, o3 + 2 * _MH:]
    zpad = lambda n: jnp.zeros((D, n), w_in.dtype)
    tail = SLOT - MLA_NOPE - MLA_ROPE
    kr_slot = jnp.concatenate([zpad(MLA_NOPE), w_kr, zpad(tail)], axis=1)
    krs_slot = jnp.concatenate([zpad(MLA_NOPE), _rot_partner(w_kr, MLA_ROPE // 2), zpad(tail)], axis=1)

    def moba_partner(w):
        w3 = w.reshape(D, MOBA_HEADS, MOBA_HD)
        part = jnp.concatenate([_rot_partner(w3[..., :MOBA_ROT], MOBA_ROT // 2),
                                jnp.zeros((D, MOBA_HEADS, MOBA_HD - MOBA_ROT), w.dtype)], axis=-1)
        return part.reshape(D, _MH)

    win = jnp.concatenate([w_ql, w_kvl, kr_slot, krs_slot, w_mk, moba_partner(w_mk)], axis=1).astype(BF16)
    mq_partner = _rot_partner(w_mq.reshape(D, MOBA_HEADS, MOBA_HD)[..., :MOBA_ROT], MOBA_ROT // 2)
    wmt = jnp.concatenate([w_mq, mq_partner.reshape(D, MOBA_HEADS * MOBA_ROT), w_mv], axis=1).T.astype(BF16)

    wq3 = w_q_b.reshape(Q_LORA, MLA_HEADS, MLA_NOPE + MLA_ROPE)
    wqs = _rot_partner(wq3[..., MLA_NOPE:], MLA_ROPE // 2).reshape(Q_LORA, MLA_HEADS * MLA_ROPE)
    wkv3 = w_kv_b.reshape(KV_LORA, MLA_HEADS, MLA_NOPE + MLA_V)
    wk = jnp.concatenate([wkv3[..., :MLA_NOPE], jnp.zeros((KV_LORA, MLA_HEADS, SLOT - MLA_NOPE), w_kv_b.dtype)],
                         axis=-1).reshape(KV_LORA, MLA_HEADS * SLOT).astype(BF16)
    wv = wkv3[..., MLA_NOPE:].reshape(KV_LORA, MLA_HEADS * MLA_V)
    return win, wmt, w_q_b.T.astype(BF16), wqs.T.astype(BF16), wk, wv.T.astype(BF16)


def _rope_tables(positions):
    pos = positions.astype(F32).reshape(1, -1)

    def cs(d_rot):
        inv_freq = ROPE_THETA ** (-jnp.arange(0, d_rot, 2, dtype=F32) / d_rot)
        ang = jnp.concatenate([inv_freq, inv_freq]).reshape(d_rot, 1) * pos
        return jnp.cos(ang), jnp.sin(ang)

    return cs(MLA_ROPE) + cs(MOBA_ROT)


def kernel(x, positions, w_in, q_a_norm, w_q_b, kv_a_norm, w_kv_b, w_o, ln1_g, ln1_b, w_router, b_router,
           w_gate, b_gate, w_up, b_up, w_down, b_down, ln2_g, ln2_b):
    B, T, D = x.shape
    depth = w_in.shape[0]
    alpha = (2.0 * depth) ** 0.25
    N = B * T
    assert T % MOBA_BLOCK == 0 and T // MOBA_BLOCK <= SUBLANES and N % ROW_TILE == 0
    assert D == SUBLANES * LANES
    n_asg = N * MOE_TOPK
    n_rows = n_asg + N_EXPERTS * MOE_GROUP
    tables = _rope_tables(positions)
    h = x.reshape(N, D)
    for l in range(depth):
        win, wmt, wqt, wqst, wk, wvt = _layer_weights(w_in[l], w_q_b[l], w_kv_b[l])
        qa, ka, va, mq, mk, mv = _prep(h, win, wmt, wqt, wqst, wk, wvt, q_a_norm[l].reshape(1, -1),
                                       kv_a_norm[l].reshape(1, -1), tables, B, T)
        a = _attention(qa, ka, va, B, T, "mla_attention")
        m = _attention(mq, mk, mv, B, T, "moba_attention")
        wr_pad = jnp.concatenate([w_router[l], jnp.zeros((D, LANES - N_EXPERTS), F32)], axis=1)
        wr_hi = wr_pad.astype(BF16)
        wr = jnp.concatenate([wr_hi, (wr_pad - wr_hi.astype(F32)).astype(BF16)], axis=1)
        br = b_router[l].reshape(N_EXPERTS, 1)
        x1, x1t, route, gates, cnt = _oproj(a, m, h, w_o[l].astype(BF16), ln1_g[l].reshape(1, D), ln1_b[l].reshape(1, D),
                                       wr, br, alpha)
        counts = cnt[:, 0].astype(jnp.int32)
        padded = (counts + MOE_GROUP - 1) // MOE_GROUP * MOE_GROUP
        pad_end = jnp.cumsum(padded).astype(jnp.int32)
        pad_start = pad_end - padded
        e_idx = route[:MOE_TOPK]
        group_start = jnp.sum(jnp.where(e_idx[..., None] == jnp.arange(N_EXPERTS, dtype=jnp.int32), pad_start, 0), axis=-1)
        dest = (group_start + route[MOE_TOPK:2 * MOE_TOPK]).reshape(n_asg)
        n_used = (pad_end[-1:] // MOE_GROUP).astype(jnp.int32)
        row_tok = jnp.zeros((n_rows,), jnp.int32).at[dest].set(jnp.tile(jnp.arange(N, dtype=jnp.int32), MOE_TOPK))
        blk_start = jnp.arange(n_rows // MOE_GROUP, dtype=jnp.int32) * MOE_GROUP
        blk_expert = jnp.minimum(jnp.sum((blk_start[:, None] >= pad_end[None, :]).astype(jnp.int32), axis=1),
                                 N_EXPERTS - 1)
        nblk = n_rows // MOE_GROUP
        starts_group = jnp.concatenate([jnp.ones((1,), jnp.int32), (blk_expert[1:] != blk_expert[:-1]).astype(jnp.int32)])
        blk_slot = (jnp.cumsum(starts_group) - 1) % 2
        next_start = pad_end[blk_expert] // MOE_GROUP
        blk_next = jnp.where(next_start < n_used[0], blk_expert[jnp.minimum(next_start, nblk - 1)], -1).astype(jnp.int32)
        y_rows = _experts(blk_expert, blk_slot.astype(jnp.int32), blk_next, n_used, row_tok, x1t, w_gate[l], b_gate[l].reshape(N_EXPERTS, 1, -1),
                          w_up[l], b_up[l].reshape(N_EXPERTS, 1, -1), w_down[l], b_down[l].reshape(N_EXPERTS, 1, -1))
        h = _combine(dest, gates, x1, ln2_g[l].reshape(1, D), ln2_b[l].reshape(1, D), y_rows, alpha)
    return h.reshape(B, T, D)
```

```python
import functools
import math

import jax
import jax.numpy as jnp
from jax import lax
from jax.experimental import pallas as pl
from jax.experimental.pallas import tpu as pltpu

ROPE_THETA = 500000.0
MLA_HEADS = 8
MLA_NOPE = 64
MLA_ROPE = 32
MLA_V = 64
Q_LORA = 256
KV_LORA = 128
MOBA_HEADS = 8
MOBA_HD = 64
MOBA_ROT = MOBA_HD // 4
MOBA_BLOCK = 256
MOBA_TOPK = 3
N_EXPERTS = 32
MOE_TOPK = 4
SWIGLU_LIMIT = 7.0
SWIGLU_ALPHA = 1.702
RMS_EPS = 1e-6
LN_EPS = 1e-5

LANES = 128
SUBLANES = 8
VMEM_LIMIT_BYTES = 56 * 1024 * 1024

SLOT = LANES
TQ = MOBA_BLOCK
ROW_TILE = 256
OPROJ_TILE = 512
MOE_GROUP = 512
MOE_HALF = MOE_GROUP // 2
GROUP_SUB = MOE_GROUP * SUBLANES
HALF_SUB = MOE_HALF * SUBLANES
DISPATCH_TILE = 512
COMBINE_TILE = 512
ISSUE_UNROLL = 4
DMA_PRIORITIES = 2
HEAD_LANES = 64
ONES_LANE = 64
BIAS_LANE = 64
ATTN_HEADS_PER_STEP = 8
LOG2E = math.log2(math.e)
NEG_BIG = -(2.0 ** 100)

F32 = jnp.float32
BF16 = jnp.bfloat16
NT_DIMS = (((1,), (1,)), ((), ()))


def _dot(a, b, precision=None):
    return jnp.dot(a, b, preferred_element_type=F32, precision=precision)


def _dot_nt(a, b, precision=None):
    return lax.dot_general(a, b, NT_DIMS, preferred_element_type=F32, precision=precision)


def _rows_to_tiles(ref, x):
    rows = x.shape[0]
    for c in range(SUBLANES):
        ref[pl.ds(c, rows, stride=SUBLANES), :] = x[:, c * LANES:(c + 1) * LANES]


def _tiles_to_rows(ref, rows):
    return jnp.concatenate([ref[pl.ds(c, rows, stride=SUBLANES), :] for c in range(SUBLANES)], axis=1)


def _rms(x, g):
    return x * lax.rsqrt(jnp.mean(x * x, axis=-1, keepdims=True) + RMS_EPS) * g


def _layer_norm(x, g, b):
    mu = jnp.mean(x, axis=-1, keepdims=True)
    xc = x - mu
    var = jnp.mean(xc * xc, axis=-1, keepdims=True)
    return xc * lax.rsqrt(var + LN_EPS) * g + b


_C_QL = 0
_C_KVL = _C_QL + Q_LORA
_C_KR = _C_KVL + KV_LORA
_C_KRS = _C_KR + SLOT
_C_MK = _C_KRS + SLOT
_MH = MOBA_HEADS * MOBA_HD
_C_MKS = _C_MK + _MH
_C_END = _C_MKS + _MH


def _prep_kernel(x_ref, win_ref, wmt_ref, wqt_ref, wqst_ref, wk_ref, wvt_ref, qg_ref, kvg_ref,
                 cat_ref, sat_ref, cbt_ref, sbt_ref,
                 qat_ref, ka_ref, vat_ref, mqt_ref, mk_ref, mvt_ref, kmean_scr):
    c = pl.program_id(1)
    xb = x_ref[...].astype(BF16)
    cat = cat_ref[...]
    sat = sat_ref[...]
    cbt = cbt_ref[...]
    sbt = sbt_ref[...]
    const = lambda v, n: jnp.full((n, ROW_TILE), v, F32)
    tail = SLOT - MLA_NOPE - MLA_ROPE
    rest = MOBA_HD - MOBA_ROT
    ca = jnp.concatenate([const(1.0, MLA_NOPE), cat, const(0.0, tail)], axis=0).T
    sa = jnp.concatenate([const(0.0, MLA_NOPE), sat, const(0.0, tail)], axis=0).T
    cb = jnp.concatenate([cbt, const(1.0, rest), cbt, const(1.0, rest)], axis=0).T
    sb = jnp.concatenate([sbt, const(0.0, rest), sbt, const(0.0, rest)], axis=0).T
    lane =lax.broadcasted_iota(jnp.int32, (ROW_TILE, SLOT), 1)
    head_lanes = lane < HEAD_LANES
    ones_rows = (lax.broadcasted_iota(jnp.int32, (SLOT - HEAD_LANES, ROW_TILE), 0) == ONES_LANE - HEAD_LANES).astype(F32)

    ql = _dot(xb, win_ref[:, _C_QL:_C_KVL])
    kvl = _dot(xb, win_ref[:, _C_KVL:_C_KR])
    kr = _dot(xb, win_ref[:, _C_KR:_C_KRS])
    krs = _dot(xb, win_ref[:, _C_KRS:_C_MK])
    qn = _rms(ql, qg_ref[...]).astype(BF16)
    kvn = _rms(kvl, kvg_ref[...]).astype(BF16)
    q_t = _dot_nt(wqt_ref[...], qn)
    qs_t = _dot_nt(wqst_ref[...], qn)
    kn = _dot(kvn, wk_ref[...])
    v_t = _dot_nt(wvt_ref[...], kvn)
    scale_a = LOG2E / math.sqrt(MLA_NOPE + MLA_ROPE)
    kro = kr * ca + krs * sa
    qd = MLA_NOPE + MLA_ROPE
    q_pad = jnp.zeros((SLOT - qd, ROW_TILE), F32)
    for h in range(MLA_HEADS):
        sl = slice(h * SLOT, (h + 1) * SLOT)
        q_rope = q_t[h * qd + MLA_NOPE:(h + 1) * qd] * cat + qs_t[h * MLA_ROPE:(h + 1) * MLA_ROPE] * sat
        q_slot = jnp.concatenate([q_t[h * qd:h * qd + MLA_NOPE], q_rope, q_pad], axis=0)
        qat_ref[0, sl, :] = (q_slot * scale_a).astype(BF16)
        ka_ref[:, sl] = (kn[:, sl] + kro).astype(BF16)
        vat_ref[0, sl, :] = jnp.concatenate([v_t[h * MLA_V:(h + 1) * MLA_V], ones_rows], axis=0).astype(BF16)

    mk = _dot(xb, win_ref[:, _C_MK:_C_MKS])
    mks = _dot(xb, win_ref[:, _C_MKS:_C_END])
    npair = _MH // LANES
    pair = lambda a, j: a[:, j * LANES:(j + 1) * LANES]
    mk_rot = [pair(mk, j) * cb + pair(mks, j) * sb for j in range(npair)]
    mk_all = jnp.concatenate(mk_rot, axis=1)

    nrot = MOBA_HEADS * MOBA_ROT
    mq_t = _dot_nt(wmt_ref[0:_MH, :], xb)
    mqs_t = _dot_nt(wmt_ref[_MH:_MH + nrot, :], xb)
    mv_t = _dot_nt(wmt_ref[_MH + nrot:2 * _MH + nrot, :], xb)
    head = lambda a, h: a[h * MOBA_HD:(h + 1) * MOBA_HD]
    mq_rot_t = [jnp.concatenate([head(mq_t, h)[0:MOBA_ROT] * cbt + mqs_t[h * MOBA_ROT:(h + 1) * MOBA_ROT] * sbt,
                                 head(mq_t, h)[MOBA_ROT:]], axis=0) for h in range(MOBA_HEADS)]

    nrow = MOBA_HEADS * SUBLANES
    @pl.when(c == 0)
    def _():
        kmean_scr[...] = jnp.zeros_like(kmean_scr)

    row_i = lax.broadcasted_iota(jnp.int32, (nrow, _MH), 0)
    lane_i = lax.broadcasted_iota(jnp.int32, (nrow, _MH), 1)
    kmean_c = jnp.mean(mk_all, axis=0, keepdims=True)
    put = ((row_i % SUBLANES) == c) & ((lane_i // MOBA_HD) == (row_i // SUBLANES))
    table = kmean_scr[...]
    gate_t = _dot(table, jnp.concatenate(mq_rot_t, axis=0), precision=lax.Precision.HIGHEST)
    kmean_scr[...] = jnp.where(put, jnp.broadcast_to(kmean_c, (nrow, _MH)), table)

    n_idx = lax.broadcasted_iota(jnp.int32, (SUBLANES, ROW_TILE), 0)
    valid = n_idx < c
    scale_b = LOG2E / math.sqrt(MOBA_HD)
    zero_rows = jnp.zeros((SLOT - HEAD_LANES - SUBLANES, ROW_TILE), F32)
    block_onehot = (lane == BIAS_LANE + c).astype(F32)
    for h in range(MOBA_HEADS):
        g = jnp.where(valid, gate_t[h * SUBLANES:(h + 1) * SUBLANES, :], -jnp.inf)
        rank = jnp.zeros((SUBLANES, ROW_TILE), jnp.int32)
        for k in range(1, SUBLANES):
            other = pltpu.roll(g, k, axis=0)
            other_n = pltpu.roll(n_idx, k, axis=0)
            beats = (other > g) | ((other == g) & (other_n < n_idx))
            rank = rank + beats.astype(jnp.int32)
        keep = (valid & (rank < MOBA_TOPK)) | (n_idx == c)
        bias = jnp.where(keep, 0.0, NEG_BIG)
        sl = slice(h * SLOT, (h + 1) * SLOT)
        mqt_ref[0, sl, :] = jnp.concatenate([mq_rot_t[h] * scale_b, bias, zero_rows], axis=0).astype(BF16)
        mvt_ref[0, sl, :] = jnp.concatenate([head(mv_t, h), ones_rows], axis=0).astype(BF16)
        j, hh = divmod(h, 2)
        k_h = mk_rot[j] if hh == 0 else pltpu.roll(mk_rot[j], HEAD_LANES, axis=1)
        mk_ref[:, sl] = jnp.where(head_lanes, k_h, block_onehot).astype(BF16)


def _prep(x2, win, wmt, wqt, wqst, wk, wvt, qg, kvg, tables, B, T):
    N, D = x2.shape
    nt = T // ROW_TILE
    row = lambda b, c: (b * nt + c, 0)
    col = lambda b, c: (0, b * nt + c)
    full = lambda b, c: (0, 0)
    cat, sat, cbt, sbt = tables
    width = MLA_HEADS * SLOT

    def rows(w):
        return pl.BlockSpec((ROW_TILE, w), row)

    def cols(a):
        return pl.BlockSpec((a.shape[0], ROW_TILE), col)

    def whole(a):
        return pl.BlockSpec(a.shape, full)

    rowmajor = jax.ShapeDtypeStruct((N, width), BF16)
    transposed = jax.ShapeDtypeStruct((B * nt, width, ROW_TILE), BF16)
    t_spec = pl.BlockSpec((1, width, ROW_TILE), lambda b, c: (b * nt + c, 0, 0))
    return pl.pallas_call(
        _prep_kernel,
        out_shape=(transposed, rowmajor, transposed, transposed, rowmajor, transposed),
        grid=(B, nt),
        in_specs=[rows(D), whole(win), whole(wmt), whole(wqt), whole(wqst), whole(wk), whole(wvt), whole(qg), whole(kvg),
                  cols(cat), cols(sat), cols(cbt), cols(sbt)],
        out_specs=(t_spec, rows(width), t_spec, t_spec, rows(width), t_spec),
        scratch_shapes=[pltpu.VMEM((MOBA_HEADS * SUBLANES, _MH), F32)],
        compiler_params=pltpu.CompilerParams(dimension_semantics=("arbitrary", "arbitrary"),
                                             vmem_limit_bytes=VMEM_LIMIT_BYTES),
        name="prep",
    )(x2, win, wmt, wqt, wqst, wk, wvt, qg, kvg, cat, sat, cbt, sbt)


def _attn_kernel(qt_ref, k_ref, vt_ref, o_ref, s_scr):
    i = pl.program_id(2)
    nh = ATTN_HEADS_PER_STEP
    slot = lambda h: slice(h * SLOT, (h + 1) * SLOT)
    nblk = i + 1
    npair = nblk // 2
    q_pos = i * TQ + lax.broadcasted_iota(jnp.int32, (1, TQ), 1)

    def fold(s):
        out = s[0:SUBLANES]
        for t in range(1, s.shape[0] // SUBLANES):
            out = jnp.maximum(out, s[t * SUBLANES:(t + 1) * SUBLANES])
        return out

    def score_blocks(j, n, mrun):
        j0 = pl.multiple_of(j * TQ, TQ)
        visible = (j0 + lax.broadcasted_iota(jnp.int32, (n * TQ, TQ), 0)) <= q_pos
        out = []
        for h in range(nh):
            s = _dot(k_ref[pl.ds(j0, n * TQ), slot(h)], qt_ref[0, slot(h), :])
            s = jnp.where(visible, s, -jnp.inf)
            s_scr[h, pl.ds(j, n)] = s.reshape(n, TQ, TQ)
            out.append(jnp.maximum(mrun[h], fold(s)))
        return tuple(out)

    mrun = tuple(jnp.full((SUBLANES, TQ), -jnp.inf, F32) for _ in range(nh))
    mrun = lax.fori_loop(0, npair, lambda jj, m: score_blocks(2 * jj, 2, m), mrun)
    mrun = lax.fori_loop(2 * npair, nblk, lambda j, m: score_blocks(j, 1, m), mrun)
    ms = [jnp.max(m, axis=0, keepdims=True) for m in mrun]

    def accumulate(j, n, acc):
        out = []
        for h in range(nh):
            p = jnp.exp2(s_scr[h, pl.ds(j, n)].reshape(n * TQ, TQ) - ms[h]).astype(BF16)
            vt = jnp.concatenate([vt_ref[j + t, slot(h), :] for t in range(n)], axis=1)
            out.append(acc[h] + _dot(vt, p))
        return tuple(out)

    acc = tuple(jnp.zeros((SLOT, TQ), F32) for _ in range(nh))
    acc = lax.fori_loop(0, npair, lambda jj, a: accumulate(2 * jj, 2, a), acc)
    acc = lax.fori_loop(2 * npair, nblk, lambda j, a: accumulate(j, 1, a), acc)
    outs = [a[0:HEAD_LANES] / a[ONES_LANE:ONES_LANE + 1] for a in acc]
    for jj in range(nh // 2):
        both = jnp.concatenate([outs[2 * jj], outs[2 * jj + 1]], axis=0)
        o_ref[:, jj * LANES:(jj + 1) * LANES] = both.T.astype(o_ref.dtype)


def _attention(qt, k, vt, B, T, name):
    N = k.shape[0]
    nq = T // TQ
    nh = ATTN_HEADS_PER_STEP
    heads = k.shape[1] // SLOT
    return pl.pallas_call(
        _attn_kernel,
        out_shape=jax.ShapeDtypeStruct((N, heads * HEAD_LANES), BF16),
        grid=(B, heads // nh, nq),
        in_specs=[pl.BlockSpec((1, nh * SLOT, TQ), lambda b, g, i: (b * nq + i, g, 0)),
                  pl.BlockSpec((T, nh * SLOT), lambda b, g, i: (b, g)),
                  pl.BlockSpec((nq, nh * SLOT, TQ), lambda b, g, i: (b, g, 0))],
        out_specs=pl.BlockSpec((TQ, nh * HEAD_LANES), lambda b, g, i: (b * nq + i, g)),
        scratch_shapes=[pltpu.VMEM((nh, nq, TQ, TQ), F32)],
        compiler_params=pltpu.CompilerParams(dimension_semantics=("arbitrary", "arbitrary", "arbitrary"),
                                             vmem_limit_bytes=VMEM_LIMIT_BYTES),
        name=name,
    )(qt, k, vt)


def _oproj_kernel(alpha, a_ref, m_ref, x_ref, wo_ref, g_ref, b_ref, wr_ref, br_ref,
                  x1_ref, x1t_ref, route_ref, gates_ref, cnt_ref, carry_scr):
    i = pl.program_id(0)

    @pl.when(i == 0)
    def _():
        carry_scr[...] = jnp.zeros_like(carry_scr)

    mix = _dot(jnp.concatenate([a_ref[...], m_ref[...]], axis=1), wo_ref[...])
    x1 = _layer_norm(alpha * x_ref[...] + mix, g_ref[...], b_ref[...])
    x1_ref[...] = x1
    _rows_to_tiles(x1t_ref, x1)

    ne = br_ref.shape[0]
    rows = x1.shape[0]
    x_hi = x1.astype(BF16)
    x_lo = (x1 - x_hi.astype(F32)).astype(BF16)
    both = _dot(x_hi, wr_ref[...])
    logits_rm = both[:, :LANES] + both[:, LANES:] + _dot(x_lo, wr_ref[:, :LANES])
    logits = logits_rm.T[0:ne] + br_ref[...]
    expert = lax.broadcasted_iota(jnp.int32, logits.shape, 0)
    expert_f = expert.astype(F32)
    vals, idxs = [], []
    work = logits
    for _ in range(MOE_TOPK):
        mx = jnp.max(work, axis=0, keepdims=True)
        ix = jnp.min(jnp.where(work == mx, expert_f, float(ne)), axis=0, keepdims=True).astype(jnp.int32)
        vals.append(mx)
        idxs.append(ix)
        work = jnp.where(expert == ix, -jnp.inf, work)
    exps = [jnp.exp(v - vals[0]) for v in vals]
    den = exps[0]
    for e in exps[1:]:
        den = den + e

    onehot = jnp.zeros(logits.shape, F32)
    for ix in idxs:
        onehot = onehot + (expert == ix).astype(F32)
    r = lax.broadcasted_iota(jnp.int32, (rows, rows), 0)
    cidx = lax.broadcasted_iota(jnp.int32, (rows, rows), 1)
    earlier = (r < cidx).astype(BF16)
    carry = carry_scr[:, 0:1]
    before = _dot(onehot.astype(BF16), earlier) + carry
    row8 = lax.broadcasted_iota(jnp.int32, (SUBLANES, rows), 0)
    route = jnp.zeros((SUBLANES, rows), jnp.int32)
    gates = jnp.zeros((SUBLANES, rows), F32)
    for k in range(MOE_TOPK):
        rank = jnp.sum(jnp.where(expert == idxs[k], before, 0.0), axis=0, keepdims=True).astype(jnp.int32)
        route = jnp.where(row8 == k, idxs[k], route)
        route = jnp.where(row8 == MOE_TOPK + k, rank, route)
        gates = jnp.where(row8 == k, exps[k] / den, gates)
    route_ref[...] = route
    gates_ref[...] = jnp.concatenate([gates, jnp.zeros((LANES - SUBLANES, rows), F32)], axis=0).T
    new_carry = carry + jnp.sum(onehot, axis=1, keepdims=True)
    carry_scr[...] = jnp.broadcast_to(new_carry, carry_scr.shape)
    cnt_ref[...] = jnp.broadcast_to(new_carry, cnt_ref.shape)


def _oproj(a, m, x2, wo, g1, b1, wr, br, alpha):
    N, D = x2.shape
    nt = N // OPROJ_TILE
    row = lambda i: (i, 0)
    full = lambda i: (0, 0)
    return pl.pallas_call(
        functools.partial(_oproj_kernel, alpha),
        out_shape=(jax.ShapeDtypeStruct((N, D), F32),
                   jax.ShapeDtypeStruct((N * SUBLANES, LANES), F32),
                   jax.ShapeDtypeStruct((SUBLANES, N), jnp.int32),
                   jax.ShapeDtypeStruct((N, LANES), F32),
                   jax.ShapeDtypeStruct((br.shape[0], LANES), F32)),
        grid=(nt,),
        in_specs=[pl.BlockSpec((OPROJ_TILE, a.shape[1]), row), pl.BlockSpec((OPROJ_TILE, m.shape[1]), row),
                  pl.BlockSpec((OPROJ_TILE, D), row), pl.BlockSpec(wo.shape, full),
                  pl.BlockSpec(g1.shape, full), pl.BlockSpec(b1.shape, full),
                  pl.BlockSpec(wr.shape, full), pl.BlockSpec(br.shape, full)],
        out_specs=(pl.BlockSpec((OPROJ_TILE, D), row), pl.BlockSpec((OPROJ_TILE * SUBLANES, LANES), row),
                   pl.BlockSpec((SUBLANES, OPROJ_TILE), lambda i: (0, i)),
                   pl.BlockSpec((OPROJ_TILE, LANES), row), pl.BlockSpec((br.shape[0], LANES), full)),
        scratch_shapes=[pltpu.VMEM((br.shape[0], LANES), F32)],
        compiler_params=pltpu.CompilerParams(dimension_semantics=("arbitrary",),
                                             vmem_limit_bytes=VMEM_LIMIT_BYTES),
        name="oproj_router",
    )(a, m, x2, wo, g1, b1, wr, br)


def _dispatch_kernel(dest_ref, tail_ref, x1_ref, xs_ref, zero_scr, sem):
    i = pl.program_id(0)

    @pl.when(i == 0)
    def _():
        zero_scr[...] = jnp.zeros_like(zero_scr)

        def tail_copy(e):
            return pltpu.make_async_copy(zero_scr, xs_ref.at[pl.ds(pl.multiple_of(tail_ref[e] * SUBLANES, GROUP_SUB), GROUP_SUB)], sem)

        def start(e, _):
            @pl.when(tail_ref[e] >= 0)
            def _():
                tail_copy(e).start()
            return 0

        def wait(e, _):
            @pl.when(tail_ref[e] >= 0)
            def _():
                tail_copy(e).wait()
            return 0

        lax.fori_loop(0, N_EXPERTS, start, 0)
        lax.fori_loop(0, N_EXPERTS, wait, 0)

        def spare_copy(blk):
            return pltpu.make_async_copy(zero_scr, xs_ref.at[pl.ds(pl.multiple_of(blk * GROUP_SUB, GROUP_SUB), GROUP_SUB)], sem)

        def start_spare(blk, _):
            spare_copy(blk).start()
            return 0

        def wait_spare(blk, _):
            spare_copy(blk).wait()
            return 0

        nblk = xs_ref.shape[0] // GROUP_SUB
        lax.fori_loop(tail_ref[N_EXPERTS], nblk, start_spare, 0)
        lax.fori_loop(tail_ref[N_EXPERTS], nblk, wait_spare, 0)

    base = i * DISPATCH_TILE
    n_tok = pl.num_programs(0) * DISPATCH_TILE

    def row_copy(t, k):
        d = dest_ref[k * n_tok + base + t]
        return pltpu.make_async_copy(x1_ref.at[pl.ds(pl.multiple_of(t * SUBLANES, SUBLANES), SUBLANES)],
                                     xs_ref.at[pl.ds(pl.multiple_of(d * SUBLANES, SUBLANES), SUBLANES)], sem)

    def start_rows(t, _):
        for k in range(MOE_TOPK):
            row_copy(t, k).start(priority=k % DMA_PRIORITIES)
        return 0

    lax.fori_loop(0, DISPATCH_TILE, start_rows, 0, unroll=ISSUE_UNROLL)
    for k in range(MOE_TOPK):
        pltpu.make_async_copy(x1_ref, xs_ref.at[pl.ds(0, DISPATCH_TILE * SUBLANES)], sem).wait()


def _dispatch(dest, tail, x1t, n_rows):
    return pl.pallas_call(
        _dispatch_kernel,
        out_shape=jax.ShapeDtypeStruct((n_rows * SUBLANES, LANES), F32),
        grid_spec=pltpu.PrefetchScalarGridSpec(
            num_scalar_prefetch=2,
            grid=(x1t.shape[0] // (DISPATCH_TILE * SUBLANES),),
            in_specs=[pl.BlockSpec((DISPATCH_TILE * SUBLANES, LANES), lambda i, d, t: (i, 0))],
            out_specs=pl.BlockSpec(memory_space=pl.ANY),
            scratch_shapes=[pltpu.VMEM((GROUP_SUB, LANES), F32), pltpu.SemaphoreType.DMA(())],
        ),
        compiler_params=pltpu.CompilerParams(dimension_semantics=("arbitrary",),
                                             vmem_limit_bytes=VMEM_LIMIT_BYTES),
        name="dispatch",
    )(dest, tail, x1t)


def _experts_kernel(be_ref, slot_ref, nxt_ref, full_ref, nused_ref, x_ref, wg_hbm, bg_ref, wu_hbm, bu_ref, wd_hbm, bd_ref,
                    y_ref, wg_st, wu_st, wd_st, wg_bf, wu_bf, wd_bf, sems):
    i = pl.program_id(0)
    prev = be_ref[jnp.maximum(i - 1, 0)]
    changed = (i == 0) | (be_ref[i] != prev)
    active = i < nused_ref[0]
    slot = slot_ref[i]

    def weight_copies(expert, s):
        return (pltpu.make_async_copy(wg_hbm.at[expert], wg_st.at[s], sems.at[s, 0]),
                pltpu.make_async_copy(wu_hbm.at[expert], wu_st.at[s], sems.at[s, 1]),
                pltpu.make_async_copy(wd_hbm.at[expert], wd_st.at[s], sems.at[s, 2]))

    @pl.when(i == 0)
    def _():
        for cp in weight_copies(be_ref[0], slot):
            cp.start()

    @pl.when(active & changed)
    def _():
        for cp in weight_copies(be_ref[i], slot):
            cp.wait()

        @pl.when(nxt_ref[i] >= 0)
        def _():
            for cp in weight_copies(nxt_ref[i], 1 - slot):
                cp.start()

        wg_bf[...] = wg_st[slot].astype(BF16)
        wu_bf[...] = wu_st[slot].astype(BF16)
        wd_bf[...] = wd_st[slot].astype(BF16)

    def mlp_half(part):
        rows = pl.ds(part * HALF_SUB, HALF_SUB)
        xb = _tiles_to_rows(x_ref.at[rows], MOE_HALF).astype(BF16)
        g = jnp.minimum(_dot(xb, wg_bf[...]) + bg_ref[0], SWIGLU_LIMIT)
        u = jnp.clip(_dot(xb, wu_bf[...]) + bu_ref[0], -SWIGLU_LIMIT, SWIGLU_LIMIT)
        h = g * (1.0 / (1.0 + jnp.exp(-SWIGLU_ALPHA * g))) * (u + 1.0)
        _rows_to_tiles(y_ref.at[rows], _dot(h.astype(BF16), wd_bf[...]) + bd_ref[0])

    second = active & (full_ref[i] > 0)

    @pl.when(active)
    def _():
        mlp_half(0)

    @pl.when(second)
    def _():
        mlp_half(1)

    @pl.when(jnp.logical_not(active))
    def _():
        y_ref[pl.ds(0, HALF_SUB), :] = jnp.zeros((HALF_SUB, LANES), F32)

    @pl.when(jnp.logical_not(second))
    def _():
        y_ref[pl.ds(HALF_SUB, HALF_SUB), :] = jnp.zeros((HALF_SUB, LANES), F32)


def _experts(blk_expert, blk_slot, blk_next, blk_full, n_used, xs, wg, bg, wu, bu, wd, bd):
    E, D, F = wg.shape
    nblk = xs.shape[0] // GROUP_SUB

    def rowmap(i, be, sl, nx, fl, nu):
        return (jnp.minimum(i, nu[0] - 1), 0)

    def bmap(i, be, sl, nx, fl, nu):
        return (be[i], 0, 0)

    hbm = pl.BlockSpec(memory_space=pl.ANY)
    return pl.pallas_call(
        _experts_kernel,
        out_shape=jax.ShapeDtypeStruct(xs.shape, F32),
        grid_spec=pltpu.PrefetchScalarGridSpec(
            num_scalar_prefetch=5,
            grid=(nblk,),
            in_specs=[pl.BlockSpec((GROUP_SUB, LANES), rowmap),
                      hbm, pl.BlockSpec((1, 1, F), bmap),
                      hbm, pl.BlockSpec((1, 1, F), bmap),
                      hbm, pl.BlockSpec((1, 1, D), bmap)],
            out_specs=pl.BlockSpec((GROUP_SUB, LANES), lambda i, be, sl, nx, fl, nu: (i, 0)),
            scratch_shapes=[pltpu.VMEM((2, D, F), F32), pltpu.VMEM((2, D, F), F32), pltpu.VMEM((2, F, D), F32),
                            pltpu.VMEM((D, F), BF16), pltpu.VMEM((D, F), BF16), pltpu.VMEM((F, D), BF16),
                            pltpu.SemaphoreType.DMA((2, 3))],
        ),
        compiler_params=pltpu.CompilerParams(dimension_semantics=("arbitrary",),
                                             vmem_limit_bytes=VMEM_LIMIT_BYTES),
        name="experts",
    )(blk_expert, blk_slot, blk_next, blk_full, n_used, xs, wg, bg, wu, bu, wd, bd)


def _combine_kernel(alpha, dest_ref, gates_ref, x1_ref, g_ref, b_ref, y_ref, o_ref, ybuf, sems):
    i = pl.program_id(0)
    buf = i % 2

    n_tok = pl.num_programs(0) * COMBINE_TILE

    def gather_tile(tile, into):
        base = tile * COMBINE_TILE

        def start_rows(t, _):
            for k in range(MOE_TOPK):
                d = dest_ref[k * n_tok + base + t]
                pltpu.make_async_copy(y_ref.at[pl.ds(pl.multiple_of(d * SUBLANES, SUBLANES), SUBLANES)],
                                      ybuf.at[into, k, pl.ds(pl.multiple_of(t * SUBLANES, SUBLANES), SUBLANES)],
                                      sems.at[into]).start(priority=k % DMA_PRIORITIES)
            return 0

        lax.fori_loop(0, COMBINE_TILE, start_rows, 0, unroll=ISSUE_UNROLL)

    @pl.when(i == 0)
    def _():
        gather_tile(0, 0)

    @pl.when(i + 1 < pl.num_programs(0))
    def _():
        gather_tile(i + 1, 1 - buf)

    for k in range(MOE_TOPK):
        pltpu.make_async_copy(y_ref.at[pl.ds(0, COMBINE_TILE * SUBLANES)], ybuf.at[buf, k], sems.at[buf]).wait()

    gates = gates_ref[...]
    ffn = gates[:, 0:1] * _tiles_to_rows(ybuf.at[buf, 0], COMBINE_TILE)
    for k in range(1, MOE_TOPK):
        ffn = ffn + gates[:, k:k + 1] * _tiles_to_rows(ybuf.at[buf, k], COMBINE_TILE)
    o_ref[...] = _layer_norm(alpha * x1_ref[...] + ffn, g_ref[...], b_ref[...])


def _combine(dest, gates, x1, g2, b2, y_rows, alpha):
    N, D = x1.shape
    row = lambda i, d: (i, 0)
    full = lambda i, d: (0, 0)
    return pl.pallas_call(
        functools.partial(_combine_kernel, alpha),
        out_shape=jax.ShapeDtypeStruct((N, D), F32),
        grid_spec=pltpu.PrefetchScalarGridSpec(
            num_scalar_prefetch=1,
            grid=(N // COMBINE_TILE,),
            in_specs=[pl.BlockSpec((COMBINE_TILE, LANES), row), pl.BlockSpec((COMBINE_TILE, D), row),
                      pl.BlockSpec(g2.shape, full), pl.BlockSpec(b2.shape, full),
                      pl.BlockSpec(memory_space=pl.ANY)],
            out_specs=pl.BlockSpec((COMBINE_TILE, D), row),
            scratch_shapes=[pltpu.VMEM((2, MOE_TOPK, COMBINE_TILE * SUBLANES, LANES), F32),
                            pltpu.SemaphoreType.DMA((2,))],
        ),
        compiler_params=pltpu.CompilerParams(dimension_semantics=("arbitrary",),
                                             vmem_limit_bytes=VMEM_LIMIT_BYTES),
        name="combine",
    )(dest, gates, x1, g2, b2, y_rows)


def _rot_partner(w, half):
    return jnp.concatenate([-w[..., half:2 * half], w[..., :half]], axis=-1)


def _layer_weights(w_in, w_q_b, w_kv_b):
    D = w_in.shape[0]
    o1 = Q_LORA
    o2 = o1 + KV_LORA
    o3 = o2 + MLA_ROPE
    w_ql, w_kvl, w_kr = w_in[:, :o1], w_in[:, o1:o2], w_in[:, o2:o3]
    w_mq, w_mk, w_mv = w_in[:, o3:o3 + _MH], w_in[:, o3 + _MH:o3 + 2 * _MH], w_in[:, o3 + 2 * _MH:]
    zpad = lambda n: jnp.zeros((D, n), w_in.dtype)
    tail = SLOT - MLA_NOPE - MLA_ROPE
    kr_slot = jnp.concatenate([zpad(MLA_NOPE), w_kr, zpad(tail)], axis=1)
    krs_slot = jnp.concatenate([zpad(MLA_NOPE), _rot_partner(w_kr, MLA_ROPE // 2), zpad(tail)], axis=1)

    def moba_partner(w):
        w3 = w.reshape(D, MOBA_HEADS, MOBA_HD)
        part = jnp.concatenate([_rot_partner(w3[..., :MOBA_ROT], MOBA_ROT // 2),
                                jnp.zeros((D, MOBA_HEADS, MOBA_HD - MOBA_ROT), w.dtype)], axis=-1)
        return part.reshape(D, _MH)

    win = jnp.concatenate([w_ql, w_kvl, kr_slot, krs_slot, w_mk, moba_partner(w_mk)], axis=1).astype(BF16)
    mq_partner = _rot_partner(w_mq.reshape(D, MOBA_HEADS, MOBA_HD)[..., :MOBA_ROT], MOBA_ROT // 2)
    wmt = jnp.concatenate([w_mq, mq_partner.reshape(D, MOBA_HEADS * MOBA_ROT), w_mv], axis=1).T.astype(BF16)

    wq3 = w_q_b.reshape(Q_LORA, MLA_HEADS, MLA_NOPE + MLA_ROPE)
    wqs = _rot_partner(wq3[..., MLA_NOPE:], MLA_ROPE // 2).reshape(Q_LORA, MLA_HEADS * MLA_ROPE)
    wkv3 = w_kv_b.reshape(KV_LORA, MLA_HEADS, MLA_NOPE + MLA_V)
    wk = jnp.concatenate([wkv3[..., :MLA_NOPE], jnp.zeros((KV_LORA, MLA_HEADS, SLOT - MLA_NOPE), w_kv_b.dtype)],
                         axis=-1).reshape(KV_LORA, MLA_HEADS * SLOT).astype(BF16)
    wv = wkv3[..., MLA_NOPE:].reshape(KV_LORA, MLA_HEADS * MLA_V)
    return win, wmt, w_q_b.T.astype(BF16), wqs.T.astype(BF16), wk, wv.T.astype(BF16)


def _rope_tables(positions):
    pos = positions.astype(F32).reshape(1, -1)

    def cs(d_rot):
        inv_freq = ROPE_THETA ** (-jnp.arange(0, d_rot, 2, dtype=F32) / d_rot)
        ang = jnp.concatenate([inv_freq, inv_freq]).reshape(d_rot, 1) * pos
        return jnp.cos(ang), jnp.sin(ang)

    return cs(MLA_ROPE) + cs(MOBA_ROT)


def kernel(x, positions, w_in, q_a_norm, w_q_b, kv_a_norm, w_kv_b, w_o, ln1_g, ln1_b, w_router, b_router,
           w_gate, b_gate, w_up, b_up, w_down, b_down, ln2_g, ln2_b):
    B, T, D = x.shape
    depth = w_in.shape[0]
    alpha = (2.0 * depth) ** 0.25
    N = B * T
    assert T % MOBA_BLOCK == 0 and T // MOBA_BLOCK <= SUBLANES and N % OPROJ_TILE == 0
    assert D == SUBLANES * LANES
    n_asg = N * MOE_TOPK
    n_rows = n_asg + N_EXPERTS * MOE_GROUP
    tables = _rope_tables(positions)
    h = x.reshape(N, D)
    for l in range(depth):
        win, wmt, wqt, wqst, wk, wvt = _layer_weights(w_in[l], w_q_b[l], w_kv_b[l])
        qa, ka, va, mq, mk, mv = _prep(h, win, wmt, wqt, wqst, wk, wvt, q_a_norm[l].reshape(1, -1),
                                       kv_a_norm[l].reshape(1, -1), tables, B, T)
        a = _attention(qa, ka, va, B, T, "mla_attention")
        m = _attention(mq, mk, mv, B, T, "moba_attention")
        wr_pad = jnp.concatenate([w_router[l], jnp.zeros((D, LANES - N_EXPERTS), F32)], axis=1)
        wr_hi = wr_pad.astype(BF16)
        wr = jnp.concatenate([wr_hi, (wr_pad - wr_hi.astype(F32)).astype(BF16)], axis=1)
        br = b_router[l].reshape(N_EXPERTS, 1)
        x1, x1t, route, gates, cnt = _oproj(a, m, h, w_o[l].astype(BF16), ln1_g[l].reshape(1, D), ln1_b[l].reshape(1, D),
                                       wr, br, alpha)
        counts = cnt[:, 0].astype(jnp.int32)
        padded = (counts + MOE_GROUP - 1) // MOE_GROUP * MOE_GROUP
        pad_end = jnp.cumsum(padded).astype(jnp.int32)
        pad_start = pad_end - padded
        e_idx = route[:MOE_TOPK]
        group_start = jnp.sum(jnp.where(e_idx[..., None] == jnp.arange(N_EXPERTS, dtype=jnp.int32), pad_start, 0), axis=-1)
        dest = (group_start + route[MOE_TOPK:2 * MOE_TOPK]).reshape(n_asg)
        n_used = (pad_end[-1:] // MOE_GROUP).astype(jnp.int32)
        tail = jnp.concatenate([jnp.where(padded > 0, pad_end - MOE_GROUP, -1).astype(jnp.int32), n_used])
        blk_start = jnp.arange(n_rows // MOE_GROUP, dtype=jnp.int32) * MOE_GROUP
        blk_expert = jnp.minimum(jnp.sum((blk_start[:, None] >= pad_end[None, :]).astype(jnp.int32), axis=1),
                                 N_EXPERTS - 1)
        nblk = n_rows // MOE_GROUP
        starts_group = jnp.concatenate([jnp.ones((1,), jnp.int32), (blk_expert[1:] != blk_expert[:-1]).astype(jnp.int32)])
        blk_slot = (jnp.cumsum(starts_group) - 1) % 2
        next_start = pad_end[blk_expert] // MOE_GROUP
        blk_next = jnp.where(next_start < n_used[0], blk_expert[jnp.minimum(next_start, nblk - 1)], -1).astype(jnp.int32)
        real_end = (pad_start + counts)[blk_expert]
        blk_full = (real_end > blk_start + MOE_HALF).astype(jnp.int32)
        xs = _dispatch(dest, tail, x1t, n_rows)
        y_rows = _experts(blk_expert, blk_slot.astype(jnp.int32), blk_next, blk_full, n_used, xs, w_gate[l], b_gate[l].reshape(N_EXPERTS, 1, -1),
                          w_up[l], b_up[l].reshape(N_EXPERTS, 1, -1), w_down[l], b_down[l].reshape(N_EXPERTS, 1, -1))
        h = _combine(dest, gates, x1, ln2_g[l].reshape(1, D), ln2_b[l].reshape(1, D), y_rows, alpha)
    return h.reshape(B, T, D)
```

```python
import functools
import math

import jax
import jax.numpy as jnp
from jax import lax
from jax.experimental import pallas as pl
from jax.experimental.pallas import tpu as pltpu

ROPE_THETA = 500000.0
MLA_HEADS = 8
MLA_NOPE = 64
MLA_ROPE = 32
MLA_V = 64
Q_LORA = 256
KV_LORA = 128
MOBA_HEADS = 8
MOBA_HD = 64
MOBA_ROT = MOBA_HD // 4
MOBA_BLOCK = 256
MOBA_TOPK = 3
N_EXPERTS = 32
MOE_TOPK = 4
SWIGLU_LIMIT = 7.0
SWIGLU_ALPHA = 1.702
RMS_EPS = 1e-6
LN_EPS = 1e-5

LANES = 128
SUBLANES = 8
VMEM_LIMIT_BYTES = 56 * 1024 * 1024

SLOT = LANES
TQ = MOBA_BLOCK
ROW_TILE = 256
OPROJ_TILE = 512
MOE_GROUP = 512
MOE_HALF = MOE_GROUP // 2
GROUP_SUB = MOE_GROUP * SUBLANES
HALF_SUB = MOE_HALF * SUBLANES
DISPATCH_TILE = 512
COMBINE_TILE = 512
ISSUE_UNROLL = 4
DMA_PRIORITIES = 2
HEAD_LANES = 64
ONES_LANE = 64
BIAS_LANE = 64
ATTN_HEADS_PER_STEP = 8
LOG2E = math.log2(math.e)
NEG_BIG = -(2.0 ** 100)

F32 = jnp.float32
BF16 = jnp.bfloat16
NT_DIMS = (((1,), (1,)), ((), ()))


def _dot(a, b, precision=None):
    return jnp.dot(a, b, preferred_element_type=F32, precision=precision)


def _dot_nt(a, b, precision=None):
    return lax.dot_general(a, b, NT_DIMS, preferred_element_type=F32, precision=precision)


def _rows_to_tiles(ref, x):
    rows = x.shape[0]
    for c in range(SUBLANES):
        ref[pl.ds(c, rows, stride=SUBLANES), :] = x[:, c * LANES:(c + 1) * LANES]


def _tiles_to_rows(ref, rows):
    return jnp.concatenate([ref[pl.ds(c, rows, stride=SUBLANES), :] for c in range(SUBLANES)], axis=1)


def _rms(x, g):
    return x * lax.rsqrt(jnp.mean(x * x, axis=-1, keepdims=True) + RMS_EPS) * g


def _layer_norm(x, g, b):
    mu = jnp.mean(x, axis=-1, keepdims=True)
    xc = x - mu
    var = jnp.mean(xc * xc, axis=-1, keepdims=True)
    return xc * lax.rsqrt(var + LN_EPS) * g + b


_C_QL = 0
_C_KVL = _C_QL + Q_LORA
_C_KR = _C_KVL + KV_LORA
_C_KRS = _C_KR + SLOT
_C_MK = _C_KRS + SLOT
_MH = MOBA_HEADS * MOBA_HD
_C_MKS = _C_MK + _MH
_C_END = _C_MKS + _MH


def _prep_kernel(x_ref, win_ref, wmt_ref, wqt_ref, wqst_ref, wk_ref, wvt_ref, qg_ref, kvg_ref,
                 cat_ref, sat_ref, cbt_ref, sbt_ref,
                 qat_ref, ka_ref, vat_ref, mqt_ref, mk_ref, mvt_ref, kmean_scr):
    c = pl.program_id(1)
    xb = x_ref[...].astype(BF16)
    cat = cat_ref[...]
    sat = sat_ref[...]
    cbt = cbt_ref[...]
    sbt = sbt_ref[...]
    const = lambda v, n: jnp.full((n, ROW_TILE), v, F32)
    tail = SLOT - MLA_NOPE - MLA_ROPE
    rest = MOBA_HD - MOBA_ROT
    ca = jnp.concatenate([const(1.0, MLA_NOPE), cat, const(0.0, tail)], axis=0).T
    sa = jnp.concatenate([const(0.0, MLA_NOPE), sat, const(0.0, tail)], axis=0).T
    cb = jnp.concatenate([cbt, const(1.0, rest), cbt, const(1.0, rest)], axis=0).T
    sb = jnp.concatenate([sbt, const(0.0, rest), sbt, const(0.0, rest)], axis=0).T
    lane =lax.broadcasted_iota(jnp.int32, (ROW_TILE, SLOT), 1)
    head_lanes = lane < HEAD_LANES
    ones_rows = (lax.broadcasted_iota(jnp.int32, (SLOT - HEAD_LANES, ROW_TILE), 0) == ONES_LANE - HEAD_LANES).astype(F32)

    ql = _dot(xb, win_ref[:, _C_QL:_C_KVL])
    kvl = _dot(xb, win_ref[:, _C_KVL:_C_KR])
    kr = _dot(xb, win_ref[:, _C_KR:_C_KRS])
    krs = _dot(xb, win_ref[:, _C_KRS:_C_MK])
    qn = _rms(ql, qg_ref[...]).astype(BF16)
    kvn = _rms(kvl, kvg_ref[...]).astype(BF16)
    q_t = _dot_nt(wqt_ref[...], qn)
    qs_t = _dot_nt(wqst_ref[...], qn)
    kn = _dot(kvn, wk_ref[...])
    v_t = _dot_nt(wvt_ref[...], kvn)
    scale_a = LOG2E / math.sqrt(MLA_NOPE + MLA_ROPE)
    kro = kr * ca + krs * sa
    qd = MLA_NOPE + MLA_ROPE
    q_pad = jnp.zeros((SLOT - qd, ROW_TILE), F32)
    for h in range(MLA_HEADS):
        sl = slice(h * SLOT, (h + 1) * SLOT)
        q_rope = q_t[h * qd + MLA_NOPE:(h + 1) * qd] * cat + qs_t[h * MLA_ROPE:(h + 1) * MLA_ROPE] * sat
        q_slot = jnp.concatenate([q_t[h * qd:h * qd + MLA_NOPE], q_rope, q_pad], axis=0)
        qat_ref[0, sl, :] = (q_slot * scale_a).astype(BF16)
        ka_ref[:, sl] = (kn[:, sl] + kro).astype(BF16)
        vat_ref[0, sl, :] = jnp.concatenate([v_t[h * MLA_V:(h + 1) * MLA_V], ones_rows], axis=0).astype(BF16)

    mk = _dot(xb, win_ref[:, _C_MK:_C_MKS])
    mks = _dot(xb, win_ref[:, _C_MKS:_C_END])
    npair = _MH // LANES
    pair = lambda a, j: a[:, j * LANES:(j + 1) * LANES]
    mk_rot = [pair(mk, j) * cb + pair(mks, j) * sb for j in range(npair)]
    mk_all = jnp.concatenate(mk_rot, axis=1)

    nrot = MOBA_HEADS * MOBA_ROT
    mq_t = _dot_nt(wmt_ref[0:_MH, :], xb)
    mqs_t = _dot_nt(wmt_ref[_MH:_MH + nrot, :], xb)
    mv_t = _dot_nt(wmt_ref[_MH + nrot:2 * _MH + nrot, :], xb)
    head = lambda a, h: a[h * MOBA_HD:(h + 1) * MOBA_HD]
    mq_rot_t = [jnp.concatenate([head(mq_t, h)[0:MOBA_ROT] * cbt + mqs_t[h * MOBA_ROT:(h + 1) * MOBA_ROT] * sbt,
                                 head(mq_t, h)[MOBA_ROT:]], axis=0) for h in range(MOBA_HEADS)]

    nrow = MOBA_HEADS * SUBLANES
    @pl.when(c == 0)
    def _():
        kmean_scr[...] = jnp.zeros_like(kmean_scr)

    row_i = lax.broadcasted_iota(jnp.int32, (nrow, _MH), 0)
    lane_i = lax.broadcasted_iota(jnp.int32, (nrow, _MH), 1)
    kmean_c = jnp.mean(mk_all, axis=0, keepdims=True)
    put = ((row_i % SUBLANES) == c) & ((lane_i // MOBA_HD) == (row_i // SUBLANES))
    table = kmean_scr[...]
    gate_t = _dot(table, jnp.concatenate(mq_rot_t, axis=0), precision=lax.Precision.HIGHEST)
    kmean_scr[...] = jnp.where(put, jnp.broadcast_to(kmean_c, (nrow, _MH)), table)

    n_idx = lax.broadcasted_iota(jnp.int32, (SUBLANES, ROW_TILE), 0)
    valid = n_idx < c
    scale_b = LOG2E / math.sqrt(MOBA_HD)
    zero_rows = jnp.zeros((SLOT - HEAD_LANES - SUBLANES, ROW_TILE), F32)
    block_onehot = (lane == BIAS_LANE + c).astype(F32)
    for h in range(MOBA_HEADS):
        g = jnp.where(valid, gate_t[h * SUBLANES:(h + 1) * SUBLANES, :], -jnp.inf)
        rank = jnp.zeros((SUBLANES, ROW_TILE), jnp.int32)
        for k in range(1, SUBLANES):
            other = pltpu.roll(g, k, axis=0)
            other_n = pltpu.roll(n_idx, k, axis=0)
            beats = (other > g) | ((other == g) & (other_n < n_idx))
            rank = rank + beats.astype(jnp.int32)
        keep = (valid & (rank < MOBA_TOPK)) | (n_idx == c)
        bias = jnp.where(keep, 0.0, NEG_BIG)
        sl = slice(h * SLOT, (h + 1) * SLOT)
        mqt_ref[0, sl, :] = jnp.concatenate([mq_rot_t[h] * scale_b, bias, zero_rows], axis=0).astype(BF16)
        mvt_ref[0, sl, :] = jnp.concatenate([head(mv_t, h), ones_rows], axis=0).astype(BF16)
        j, hh = divmod(h, 2)
        k_h = mk_rot[j] if hh == 0 else pltpu.roll(mk_rot[j], HEAD_LANES, axis=1)
        mk_ref[:, sl] = jnp.where(head_lanes, k_h, block_onehot).astype(BF16)


def _prep(x2, win, wmt, wqt, wqst, wk, wvt, qg, kvg, tables, B, T):
    N, D = x2.shape
    nt = T // ROW_TILE
    row = lambda b, c: (b * nt + c, 0)
    col = lambda b, c: (0, b * nt + c)
    full = lambda b, c: (0, 0)
    cat, sat, cbt, sbt = tables
    width = MLA_HEADS * SLOT

    def rows(w):
        return pl.BlockSpec((ROW_TILE, w), row)

    def cols(a):
        return pl.BlockSpec((a.shape[0], ROW_TILE), col)

    def whole(a):
        return pl.BlockSpec(a.shape, full)

    rowmajor = jax.ShapeDtypeStruct((N, width), BF16)
    transposed = jax.ShapeDtypeStruct((B * nt, width, ROW_TILE), BF16)
    t_spec = pl.BlockSpec((1, width, ROW_TILE), lambda b, c: (b * nt + c, 0, 0))
    return pl.pallas_call(
        _prep_kernel,
        out_shape=(transposed, rowmajor, transposed, transposed, rowmajor, transposed),
        grid=(B, nt),
        in_specs=[rows(D), whole(win), whole(wmt), whole(wqt), whole(wqst), whole(wk), whole(wvt), whole(qg), whole(kvg),
                  cols(cat), cols(sat), cols(cbt), cols(sbt)],
        out_specs=(t_spec, rows(width), t_spec, t_spec, rows(width), t_spec),
        scratch_shapes=[pltpu.VMEM((MOBA_HEADS * SUBLANES, _MH), F32)],
        compiler_params=pltpu.CompilerParams(dimension_semantics=("arbitrary", "arbitrary"),
                                             vmem_limit_bytes=VMEM_LIMIT_BYTES),
        name="prep",
    )(x2, win, wmt, wqt, wqst, wk, wvt, qg, kvg, cat, sat, cbt, sbt)


def _attn_kernel(qt_ref, k_ref, vt_ref, o_ref, s_scr):
    i = pl.program_id(2)
    nh = ATTN_HEADS_PER_STEP
    slot = lambda h: slice(h * SLOT, (h + 1) * SLOT)
    nblk = i + 1
    npair = nblk // 2
    q_pos = i * TQ + lax.broadcasted_iota(jnp.int32, (1, TQ), 1)

    def fold(s):
        out = s[0:SUBLANES]
        for t in range(1, s.shape[0] // SUBLANES):
            out = jnp.maximum(out, s[t * SUBLANES:(t + 1) * SUBLANES])
        return out

    def score_blocks(j, n, mrun):
        j0 = pl.multiple_of(j * TQ, TQ)
        visible = (j0 + lax.broadcasted_iota(jnp.int32, (n * TQ, TQ), 0)) <= q_pos
        out = []
        for h in range(nh):
            s = _dot(k_ref[pl.ds(j0, n * TQ), slot(h)], qt_ref[0, slot(h), :])
            s = jnp.where(visible, s, -jnp.inf)
            s_scr[h, pl.ds(j, n)] = s.reshape(n, TQ, TQ)
            out.append(jnp.maximum(mrun[h], fold(s)))
        return tuple(out)

    mrun = tuple(jnp.full((SUBLANES, TQ), -jnp.inf, F32) for _ in range(nh))
    mrun = lax.fori_loop(0, npair, lambda jj, m: score_blocks(2 * jj, 2, m), mrun)
    mrun = lax.fori_loop(2 * npair, nblk, lambda j, m: score_blocks(j, 1, m), mrun)
    ms = [jnp.max(m, axis=0, keepdims=True) for m in mrun]

    def accumulate(j, n, acc):
        out = []
        for h in range(nh):
            p = jnp.exp2(s_scr[h, pl.ds(j, n)].reshape(n * TQ, TQ) - ms[h]).astype(BF16)
            vt = jnp.concatenate([vt_ref[j + t, slot(h), :] for t in range(n)], axis=1)
            out.append(acc[h] + _dot(vt, p))
        return tuple(out)

    acc = tuple(jnp.zeros((SLOT, TQ), F32) for _ in range(nh))
    acc = lax.fori_loop(0, npair, lambda jj, a: accumulate(2 * jj, 2, a), acc)
    acc = lax.fori_loop(2 * npair, nblk, lambda j, a: accumulate(j, 1, a), acc)
    outs = [a[0:HEAD_LANES] / a[ONES_LANE:ONES_LANE + 1] for a in acc]
    for jj in range(nh // 2):
        both = jnp.concatenate([outs[2 * jj], outs[2 * jj + 1]], axis=0)
        o_ref[:, jj * LANES:(jj + 1) * LANES] = both.T.astype(o_ref.dtype)


def _attention(qt, k, vt, B, T, name):
    N = k.shape[0]
    nq = T // TQ
    nh = ATTN_HEADS_PER_STEP
    heads = k.shape[1] // SLOT
    return pl.pallas_call(
        _attn_kernel,
        out_shape=jax.ShapeDtypeStruct((N, heads * HEAD_LANES), BF16),
        grid=(B, heads // nh, nq),
        in_specs=[pl.BlockSpec((1, nh * SLOT, TQ), lambda b, g, i: (b * nq + i, g, 0)),
                  pl.BlockSpec((T, nh * SLOT), lambda b, g, i: (b, g)),
                  pl.BlockSpec((nq, nh * SLOT, TQ), lambda b, g, i: (b, g, 0))],
        out_specs=pl.BlockSpec((TQ, nh * HEAD_LANES), lambda b, g, i: (b * nq + i, g)),
        scratch_shapes=[pltpu.VMEM((nh, nq, TQ, TQ), F32)],
        compiler_params=pltpu.CompilerParams(dimension_semantics=("arbitrary", "arbitrary", "arbitrary"),
                                             vmem_limit_bytes=VMEM_LIMIT_BYTES),
        name=name,
    )(qt, k, vt)


def _oproj_kernel(alpha, a_ref, m_ref, x_ref, wo_ref, g_ref, b_ref, wr_ref, br_ref,
                  x1_ref, x1t_ref, route_ref, gates_ref, cnt_ref, carry_scr):
    i = pl.program_id(0)

    @pl.when(i == 0)
    def _():
        carry_scr[...] = jnp.zeros_like(carry_scr)

    mix = _dot(jnp.concatenate([a_ref[...], m_ref[...]], axis=1), wo_ref[...])
    x1 = _layer_norm(alpha * x_ref[...] + mix, g_ref[...], b_ref[...])
    x1_ref[...] = x1
    _rows_to_tiles(x1t_ref, x1)

    ne = br_ref.shape[0]
    rows = x1.shape[0]
    x_hi = x1.astype(BF16)
    x_lo = (x1 - x_hi.astype(F32)).astype(BF16)
    both = _dot(x_hi, wr_ref[...])
    logits_rm = both[:, :LANES] + both[:, LANES:] + _dot(x_lo, wr_ref[:, :LANES])
    logits = logits_rm.T[0:ne] + br_ref[...]
    expert = lax.broadcasted_iota(jnp.int32, logits.shape, 0)
    expert_f = expert.astype(F32)
    vals, idxs = [], []
    work = logits
    for _ in range(MOE_TOPK):
        mx = jnp.max(work, axis=0, keepdims=True)
        ix = jnp.min(jnp.where(work == mx, expert_f, float(ne)), axis=0, keepdims=True).astype(jnp.int32)
        vals.append(mx)
        idxs.append(ix)
        work = jnp.where(expert == ix, -jnp.inf, work)
    exps = [jnp.exp(v - vals[0]) for v in vals]
    den = exps[0]
    for e in exps[1:]:
        den = den + e

    onehot = jnp.zeros(logits.shape, F32)
    for ix in idxs:
        onehot = onehot + (expert == ix).astype(F32)
    r = lax.broadcasted_iota(jnp.int32, (rows, rows), 0)
    cidx = lax.broadcasted_iota(jnp.int32, (rows, rows), 1)
    earlier = (r < cidx).astype(BF16)
    carry = carry_scr[:, 0:1]
    before = _dot(onehot.astype(BF16), earlier) + carry
    row8 = lax.broadcasted_iota(jnp.int32, (SUBLANES, rows), 0)
    route = jnp.zeros((SUBLANES, rows), jnp.int32)
    gates = jnp.zeros((SUBLANES, rows), F32)
    for k in range(MOE_TOPK):
        rank = jnp.sum(jnp.where(expert == idxs[k], before, 0.0), axis=0, keepdims=True).astype(jnp.int32)
        route = jnp.where(row8 == k, idxs[k], route)
        route = jnp.where(row8 == MOE_TOPK + k, rank, route)
        gates = jnp.where(row8 == k, exps[k] / den, gates)
    route_ref[...] = route
    gates_ref[...] = jnp.concatenate([gates, jnp.zeros((LANES - SUBLANES, rows), F32)], axis=0).T
    new_carry = carry + jnp.sum(onehot, axis=1, keepdims=True)
    carry_scr[...] = jnp.broadcast_to(new_carry, carry_scr.shape)
    cnt_ref[...] = jnp.broadcast_to(new_carry, cnt_ref.shape)


def _oproj(a, m, x2, wo, g1, b1, wr, br, alpha):
    N, D = x2.shape
    nt = N // OPROJ_TILE
    row = lambda i: (i, 0)
    full = lambda i: (0, 0)
    return pl.pallas_call(
        functools.partial(_oproj_kernel, alpha),
        out_shape=(jax.ShapeDtypeStruct((N, D), F32),
                   jax.ShapeDtypeStruct((N * SUBLANES, LANES), F32),
                   jax.ShapeDtypeStruct((SUBLANES, N), jnp.int32),
                   jax.ShapeDtypeStruct((N, LANES), F32),
                   jax.ShapeDtypeStruct((br.shape[0], LANES), F32)),
        grid=(nt,),
        in_specs=[pl.BlockSpec((OPROJ_TILE, a.shape[1]), row), pl.BlockSpec((OPROJ_TILE, m.shape[1]), row),
                  pl.BlockSpec((OPROJ_TILE, D), row), pl.BlockSpec(wo.shape, full),
                  pl.BlockSpec(g1.shape, full), pl.BlockSpec(b1.shape, full),
                  pl.BlockSpec(wr.shape, full), pl.BlockSpec(br.shape, full)],
        out_specs=(pl.BlockSpec((OPROJ_TILE, D), row), pl.BlockSpec((OPROJ_TILE * SUBLANES, LANES), row),
                   pl.BlockSpec((SUBLANES, OPROJ_TILE), lambda i: (0, i)),
                   pl.BlockSpec((OPROJ_TILE, LANES), row), pl.BlockSpec((br.shape[0], LANES), full)),
        scratch_shapes=[pltpu.VMEM((br.shape[0], LANES), F32)],
        compiler_params=pltpu.CompilerParams(dimension_semantics=("arbitrary",),
                                             vmem_limit_bytes=VMEM_LIMIT_BYTES),
        name="oproj_router",
    )(a, m, x2, wo, g1, b1, wr, br)


def _dispatch_kernel(dest_ref, tail_ref, x1_ref, xs_ref, zero_scr, sem):
    i = pl.program_id(0)

    @pl.when(i == 0)
    def _():
        zero_scr[...] = jnp.zeros_like(zero_scr)

        def tail_copy(e):
            return pltpu.make_async_copy(zero_scr, xs_ref.at[pl.ds(pl.multiple_of(tail_ref[e] * SUBLANES, GROUP_SUB), GROUP_SUB)], sem)

        def start(e, _):
            @pl.when(tail_ref[e] >= 0)
            def _():
                tail_copy(e).start()
            return 0

        def wait(e, _):
            @pl.when(tail_ref[e] >= 0)
            def _():
                tail_copy(e).wait()
            return 0

        lax.fori_loop(0, N_EXPERTS, start, 0)
        lax.fori_loop(0, N_EXPERTS, wait, 0)

        def spare_copy(blk):
            return pltpu.make_async_copy(zero_scr, xs_ref.at[pl.ds(pl.multiple_of(blk * GROUP_SUB, GROUP_SUB), GROUP_SUB)], sem)

        def start_spare(blk, _):
            spare_copy(blk).start()
            return 0

        def wait_spare(blk, _):
            spare_copy(blk).wait()
            return 0

        nblk = xs_ref.shape[0] // GROUP_SUB
        lax.fori_loop(tail_ref[N_EXPERTS], nblk, start_spare, 0)
        lax.fori_loop(tail_ref[N_EXPERTS], nblk, wait_spare, 0)

    base = i * DISPATCH_TILE
    n_tok = pl.num_programs(0) * DISPATCH_TILE

    def row_copy(t, k):
        d = dest_ref[k * n_tok + base + t]
        return pltpu.make_async_copy(x1_ref.at[pl.ds(pl.multiple_of(t * SUBLANES, SUBLANES), SUBLANES)],
                                     xs_ref.at[pl.ds(pl.multiple_of(d * SUBLANES, SUBLANES), SUBLANES)], sem)

    def start_rows(t, _):
        for k in range(MOE_TOPK):
            row_copy(t, k).start(priority=k % DMA_PRIORITIES)
        return 0

    lax.fori_loop(0, DISPATCH_TILE, start_rows, 0, unroll=ISSUE_UNROLL)
    for k in range(MOE_TOPK):
        pltpu.make_async_copy(x1_ref, xs_ref.at[pl.ds(0, DISPATCH_TILE * SUBLANES)], sem).wait()


def _dispatch(dest, tail, x1t, n_rows):
    return pl.pallas_call(
        _dispatch_kernel,
        out_shape=jax.ShapeDtypeStruct((n_rows * SUBLANES, LANES), F32),
        grid_spec=pltpu.PrefetchScalarGridSpec(
            num_scalar_prefetch=2,
            grid=(x1t.shape[0] // (DISPATCH_TILE * SUBLANES),),
            in_specs=[pl.BlockSpec((DISPATCH_TILE * SUBLANES, LANES), lambda i, d, t: (i, 0))],
            out_specs=pl.BlockSpec(memory_space=pl.ANY),
            scratch_shapes=[pltpu.VMEM((GROUP_SUB, LANES), F32), pltpu.SemaphoreType.DMA(())],
        ),
        compiler_params=pltpu.CompilerParams(dimension_semantics=("arbitrary",),
                                             vmem_limit_bytes=VMEM_LIMIT_BYTES),
        name="dispatch",
    )(dest, tail, x1t)


def _experts_kernel(be_ref, slot_ref, nxt_ref, full_ref, nused_ref, x_ref, wg_hbm, bg_ref, wu_hbm, bu_ref, wd_hbm, bd_ref,
                    y_ref, wg_st, wu_st, wd_st, wg_bf, wu_bf, wd_bf, sems):
    i = pl.program_id(0)
    prev = be_ref[jnp.maximum(i - 1, 0)]
    changed = (i == 0) | (be_ref[i] != prev)
    active = i < nused_ref[0]
    slot = slot_ref[i]

    def weight_copies(expert, s):
        return (pltpu.make_async_copy(wg_hbm.at[expert], wg_st.at[s], sems.at[s, 0]),
                pltpu.make_async_copy(wu_hbm.at[expert], wu_st.at[s], sems.at[s, 1]),
                pltpu.make_async_copy(wd_hbm.at[expert], wd_st.at[s], sems.at[s, 2]))

    @pl.when(i == 0)
    def _():
        for cp in weight_copies(be_ref[0], slot):
            cp.start()

    @pl.when(active & changed)
    def _():
        for cp in weight_copies(be_ref[i], slot):
            cp.wait()

        @pl.when(nxt_ref[i] >= 0)
        def _():
            for cp in weight_copies(nxt_ref[i], 1 - slot):
                cp.start()

        wg_bf[...] = wg_st[slot].astype(BF16)
        wu_bf[...] = wu_st[slot].astype(BF16)
        wd_bf[...] = wd_st[slot].astype(BF16)

    def mlp(nrows):
        rows = pl.ds(0, nrows * SUBLANES)
        xb = _tiles_to_rows(x_ref.at[rows], nrows).astype(BF16)
        g = jnp.minimum(_dot(xb, wg_bf[...]) + bg_ref[0], SWIGLU_LIMIT)
        u = jnp.clip(_dot(xb, wu_bf[...]) + bu_ref[0], -SWIGLU_LIMIT, SWIGLU_LIMIT)
        h = g * (1.0 / (1.0 + jnp.exp(-SWIGLU_ALPHA * g))) * (u + 1.0)
        _rows_to_tiles(y_ref.at[rows], _dot(h.astype(BF16), wd_bf[...]) + bd_ref[0])

    whole = active & (full_ref[i] > 0)

    @pl.when(whole)
    def _():
        mlp(MOE_GROUP)

    @pl.when(active & jnp.logical_not(whole))
    def _():
        mlp(MOE_HALF)

    @pl.when(jnp.logical_not(active))
    def _():
        y_ref[pl.ds(0, HALF_SUB), :] = jnp.zeros((HALF_SUB, LANES), F32)

    @pl.when(jnp.logical_not(whole))
    def _():
        y_ref[pl.ds(HALF_SUB, HALF_SUB), :] = jnp.zeros((HALF_SUB, LANES), F32)


def _experts(blk_expert, blk_slot, blk_next, blk_full, n_used, xs, wg, bg, wu, bu, wd, bd):
    E, D, F = wg.shape
    nblk = xs.shape[0] // GROUP_SUB

    def rowmap(i, be, sl, nx, fl, nu):
        return (jnp.minimum(i, nu[0] - 1), 0)

    def bmap(i, be, sl, nx, fl, nu):
        return (be[i], 0, 0)

    hbm = pl.BlockSpec(memory_space=pl.ANY)
    return pl.pallas_call(
        _experts_kernel,
        out_shape=jax.ShapeDtypeStruct(xs.shape, F32),
        grid_spec=pltpu.PrefetchScalarGridSpec(
            num_scalar_prefetch=5,
            grid=(nblk,),
            in_specs=[pl.BlockSpec((GROUP_SUB, LANES), rowmap),
                      hbm, pl.BlockSpec((1, 1, F), bmap),
                      hbm, pl.BlockSpec((1, 1, F), bmap),
                      hbm, pl.BlockSpec((1, 1, D), bmap)],
            out_specs=pl.BlockSpec((GROUP_SUB, LANES), lambda i, be, sl, nx, fl, nu: (i, 0)),
            scratch_shapes=[pltpu.VMEM((2, D, F), F32), pltpu.VMEM((2, D, F), F32), pltpu.VMEM((2, F, D), F32),
                            pltpu.VMEM((D, F), BF16), pltpu.VMEM((D, F), BF16), pltpu.VMEM((F, D), BF16),
                            pltpu.SemaphoreType.DMA((2, 3))],
        ),
        compiler_params=pltpu.CompilerParams(dimension_semantics=("arbitrary",),
                                             vmem_limit_bytes=VMEM_LIMIT_BYTES),
        name="experts",
    )(blk_expert, blk_slot, blk_next, blk_full, n_used, xs, wg, bg, wu, bu, wd, bd)


def _combine_kernel(alpha, dest_ref, gates_ref, x1_ref, g_ref, b_ref, y_ref, o_ref, ybuf, sems):
    i = pl.program_id(0)
    buf = i % 2

    n_tok = pl.num_programs(0) * COMBINE_TILE

    def gather_tile(tile, into):
        base = tile * COMBINE_TILE

        def start_rows(t, _):
            for k in range(MOE_TOPK):
                d = dest_ref[k * n_tok + base + t]
                pltpu.make_async_copy(y_ref.at[pl.ds(pl.multiple_of(d * SUBLANES, SUBLANES), SUBLANES)],
                                      ybuf.at[into, k, pl.ds(pl.multiple_of(t * SUBLANES, SUBLANES), SUBLANES)],
                                      sems.at[into]).start(priority=k % DMA_PRIORITIES)
            return 0

        lax.fori_loop(0, COMBINE_TILE, start_rows, 0, unroll=ISSUE_UNROLL)

    @pl.when(i == 0)
    def _():
        gather_tile(0, 0)

    @pl.when(i + 1 < pl.num_programs(0))
    def _():
        gather_tile(i + 1, 1 - buf)

    for k in range(MOE_TOPK):
        pltpu.make_async_copy(y_ref.at[pl.ds(0, COMBINE_TILE * SUBLANES)], ybuf.at[buf, k], sems.at[buf]).wait()

    gates = gates_ref[...]
    ffn = gates[:, 0:1] * _tiles_to_rows(ybuf.at[buf, 0], COMBINE_TILE)
    for k in range(1, MOE_TOPK):
        ffn = ffn + gates[:, k:k + 1] * _tiles_to_rows(ybuf.at[buf, k], COMBINE_TILE)
    o_ref[...] = _layer_norm(alpha * x1_ref[...] + ffn, g_ref[...], b_ref[...])


def _combine(dest, gates, x1, g2, b2, y_rows, alpha):
    N, D = x1.shape
    row = lambda i, d: (i, 0)
    full = lambda i, d: (0, 0)
    return pl.pallas_call(
        functools.partial(_combine_kernel, alpha),
        out_shape=jax.ShapeDtypeStruct((N, D), F32),
        grid_spec=pltpu.PrefetchScalarGridSpec(
            num_scalar_prefetch=1,
            grid=(N // COMBINE_TILE,),
            in_specs=[pl.BlockSpec((COMBINE_TILE, LANES), row), pl.BlockSpec((COMBINE_TILE, D), row),
                      pl.BlockSpec(g2.shape, full), pl.BlockSpec(b2.shape, full),
                      pl.BlockSpec(memory_space=pl.ANY)],
            out_specs=pl.BlockSpec((COMBINE_TILE, D), row),
            scratch_shapes=[pltpu.VMEM((2, MOE_TOPK, COMBINE_TILE * SUBLANES, LANES), F32),
                            pltpu.SemaphoreType.DMA((2,))],
        ),
        compiler_params=pltpu.CompilerParams(dimension_semantics=("arbitrary",),
                                             vmem_limit_bytes=VMEM_LIMIT_BYTES),
        name="combine",
    )(dest, gates, x1, g2, b2, y_rows)


def _rot_partner(w, half):
    return jnp.concatenate([-w[..., half:2 * half], w[..., :half]], axis=-1)


def _layer_weights(w_in, w_q_b, w_kv_b):
    D = w_in.shape[0]
    o1 = Q_LORA
    o2 = o1 + KV_LORA
    o3 = o2 + MLA_ROPE
    w_ql, w_kvl, w_kr = w_in[:, :o1], w_in[:, o1:o2], w_in[:, o2:o3]
    w_mq, w_mk, w_mv = w_in[:, o3:o3 + _MH], w_in[:, o3 + _MH:o3 + 2 * _MH], w_in[:, o3 + 2 * _MH:]
    zpad = lambda n: jnp.zeros((D, n), w_in.dtype)
    tail = SLOT - MLA_NOPE - MLA_ROPE
    kr_slot = jnp.concatenate([zpad(MLA_NOPE), w_kr, zpad(tail)], axis=1)
    krs_slot = jnp.concatenate([zpad(MLA_NOPE), _rot_partner(w_kr, MLA_ROPE // 2), zpad(tail)], axis=1)

    def moba_partner(w):
        w3 = w.reshape(D, MOBA_HEADS, MOBA_HD)
        part = jnp.concatenate([_rot_partner(w3[..., :MOBA_ROT], MOBA_ROT // 2),
                                jnp.zeros((D, MOBA_HEADS, MOBA_HD - MOBA_ROT), w.dtype)], axis=-1)
        return part.reshape(D, _MH)

    win = jnp.concatenate([w_ql, w_kvl, kr_slot, krs_slot, w_mk, moba_partner(w_mk)], axis=1).astype(BF16)
    mq_partner = _rot_partner(w_mq.reshape(D, MOBA_HEADS, MOBA_HD)[..., :MOBA_ROT], MOBA_ROT // 2)
    wmt = jnp.concatenate([w_mq, mq_partner.reshape(D, MOBA_HEADS * MOBA_ROT), w_mv], axis=1).T.astype(BF16)

    wq3 = w_q_b.reshape(Q_LORA, MLA_HEADS, MLA_NOPE + MLA_ROPE)
    wqs = _rot_partner(wq3[..., MLA_NOPE:], MLA_ROPE // 2).reshape(Q_LORA, MLA_HEADS * MLA_ROPE)
    wkv3 = w_kv_b.reshape(KV_LORA, MLA_HEADS, MLA_NOPE + MLA_V)
    wk = jnp.concatenate([wkv3[..., :MLA_NOPE], jnp.zeros((KV_LORA, MLA_HEADS, SLOT - MLA_NOPE), w_kv_b.dtype)],
                         axis=-1).reshape(KV_LORA, MLA_HEADS * SLOT).astype(BF16)
    wv = wkv3[..., MLA_NOPE:].reshape(KV_LORA, MLA_HEADS * MLA_V)
    return win, wmt, w_q_b.T.astype(BF16), wqs.T.astype(BF16), wk, wv.T.astype(BF16)


def _rope_tables(positions):
    pos = positions.astype(F32).reshape(1, -1)

    def cs(d_rot):
        inv_freq = ROPE_THETA ** (-jnp.arange(0, d_rot, 2, dtype=F32) / d_rot)
        ang = jnp.concatenate([inv_freq, inv_freq]).reshape(d_rot, 1) * pos
        return jnp.cos(ang), jnp.sin(ang)

    return cs(MLA_ROPE) + cs(MOBA_ROT)


def kernel(x, positions, w_in, q_a_norm, w_q_b, kv_a_norm, w_kv_b, w_o, ln1_g, ln1_b, w_router, b_router,
           w_gate, b_gate, w_up, b_up, w_down, b_down, ln2_g, ln2_b):
    B, T, D = x.shape
    depth = w_in.shape[0]
    alpha = (2.0 * depth) ** 0.25
    N = B * T
    assert T % MOBA_BLOCK == 0 and T // MOBA_BLOCK <= SUBLANES and N % OPROJ_TILE == 0
    assert D == SUBLANES * LANES
    n_asg = N * MOE_TOPK
    n_rows = n_asg + N_EXPERTS * MOE_GROUP
    tables = _rope_tables(positions)
    h = x.reshape(N, D)
    for l in range(depth):
        win, wmt, wqt, wqst, wk, wvt = _layer_weights(w_in[l], w_q_b[l], w_kv_b[l])
        qa, ka, va, mq, mk, mv = _prep(h, win, wmt, wqt, wqst, wk, wvt, q_a_norm[l].reshape(1, -1),
                                       kv_a_norm[l].reshape(1, -1), tables, B, T)
        a = _attention(qa, ka, va, B, T, "mla_attention")
        m = _attention(mq, mk, mv, B, T, "moba_attention")
        wr_pad = jnp.concatenate([w_router[l], jnp.zeros((D, LANES - N_EXPERTS), F32)], axis=1)
        wr_hi = wr_pad.astype(BF16)
        wr = jnp.concatenate([wr_hi, (wr_pad - wr_hi.astype(F32)).astype(BF16)], axis=1)
        br = b_router[l].reshape(N_EXPERTS, 1)
        x1, x1t, route, gates, cnt = _oproj(a, m, h, w_o[l].astype(BF16), ln1_g[l].reshape(1, D), ln1_b[l].reshape(1, D),
                                       wr, br, alpha)
        counts = cnt[:, 0].astype(jnp.int32)
        padded = (counts + MOE_GROUP - 1) // MOE_GROUP * MOE_GROUP
        pad_end = jnp.cumsum(padded).astype(jnp.int32)
        pad_start = pad_end - padded
        e_idx = route[:MOE_TOPK]
        group_start = jnp.sum(jnp.where(e_idx[..., None] == jnp.arange(N_EXPERTS, dtype=jnp.int32), pad_start, 0), axis=-1)
        dest = (group_start + route[MOE_TOPK:2 * MOE_TOPK]).reshape(n_asg)
        n_used = (pad_end[-1:] // MOE_GROUP).astype(jnp.int32)
        tail = jnp.concatenate([jnp.where(padded > 0, pad_end - MOE_GROUP, -1).astype(jnp.int32), n_used])
        blk_start = jnp.arange(n_rows // MOE_GROUP, dtype=jnp.int32) * MOE_GROUP
        blk_expert = jnp.minimum(jnp.sum((blk_start[:, None] >= pad_end[None, :]).astype(jnp.int32), axis=1),
                                 N_EXPERTS - 1)
        nblk = n_rows // MOE_GROUP
        starts_group = jnp.concatenate([jnp.ones((1,), jnp.int32), (blk_expert[1:] != blk_expert[:-1]).astype(jnp.int32)])
        blk_slot = (jnp.cumsum(starts_group) - 1) % 2
        next_start = pad_end[blk_expert] // MOE_GROUP
        blk_next = jnp.where(next_start < n_used[0], blk_expert[jnp.minimum(next_start, nblk - 1)], -1).astype(jnp.int32)
        real_end = (pad_start + counts)[blk_expert]
        blk_full = (real_end > blk_start + MOE_HALF).astype(jnp.int32)
        xs = _dispatch(dest, tail, x1t, n_rows)
        y_rows = _experts(blk_expert, blk_slot.astype(jnp.int32), blk_next, blk_full, n_used, xs, w_gate[l], b_gate[l].reshape(N_EXPERTS, 1, -1),
                          w_up[l], b_up[l].reshape(N_EXPERTS, 1, -1), w_down[l], b_down[l].reshape(N_EXPERTS, 1, -1))
        h = _combine(dest, gates, x1, ln2_g[l].reshape(1, D), ln2_b[l].reshape(1, D), y_rows, alpha)
    return h.reshape(B, T, D)
```

```python
import functools
import math

import jax
import jax.numpy as jnp
from jax import lax
from jax.experimental import pallas as pl
from jax.experimental.pallas import tpu as pltpu

ROPE_THETA = 500000.0
MLA_HEADS = 8
MLA_NOPE = 64
MLA_ROPE = 32
MLA_V = 64
Q_LORA = 256
KV_LORA = 128
MOBA_HEADS = 8
MOBA_HD = 64
MOBA_ROT = MOBA_HD // 4
MOBA_BLOCK = 256
MOBA_TOPK = 3
N_EXPERTS = 32
MOE_TOPK = 4
SWIGLU_LIMIT = 7.0
SWIGLU_ALPHA = 1.702
RMS_EPS = 1e-6
LN_EPS = 1e-5

LANES = 128
SUBLANES = 8
VMEM_LIMIT_BYTES = 56 * 1024 * 1024

SLOT = LANES
TQ = MOBA_BLOCK
ROW_TILE = 256
OPROJ_TILE = 512
MOE_GROUP = 512
MOE_HALF = MOE_GROUP // 2
GROUP_SUB = MOE_GROUP * SUBLANES
HALF_SUB = MOE_HALF * SUBLANES
DISPATCH_TILE = 512
COMBINE_TILE = 512
ISSUE_UNROLL = 4
DMA_PRIORITIES = 2
HEAD_LANES = 64
ONES_LANE = 64
BIAS_LANE = 64
ATTN_HEADS_PER_STEP = 8
LOG2E = math.log2(math.e)
NEG_BIG = -(2.0 ** 100)

F32 = jnp.float32
BF16 = jnp.bfloat16
NT_DIMS = (((1,), (1,)), ((), ()))


def _dot(a, b, precision=None):
    return jnp.dot(a, b, preferred_element_type=F32, precision=precision)


def _dot_nt(a, b, precision=None):
    return lax.dot_general(a, b, NT_DIMS, preferred_element_type=F32, precision=precision)


def _rows_to_tiles(ref, x):
    rows = x.shape[0]
    for c in range(SUBLANES):
        ref[pl.ds(c, rows, stride=SUBLANES), :] = x[:, c * LANES:(c + 1) * LANES]


def _tiles_to_rows(ref, rows):
    return jnp.concatenate([ref[pl.ds(c, rows, stride=SUBLANES), :] for c in range(SUBLANES)], axis=1)


def _rms(x, g):
    return x * lax.rsqrt(jnp.mean(x * x, axis=-1, keepdims=True) + RMS_EPS) * g


def _layer_norm(x, g, b):
    mu = jnp.mean(x, axis=-1, keepdims=True)
    xc = x - mu
    var = jnp.mean(xc * xc, axis=-1, keepdims=True)
    return xc * lax.rsqrt(var + LN_EPS) * g + b


_C_QL = 0
_C_KVL = _C_QL + Q_LORA
_C_KR = _C_KVL + KV_LORA
_C_KRS = _C_KR + SLOT
_C_MK = _C_KRS + SLOT
_MH = MOBA_HEADS * MOBA_HD
_C_MKS = _C_MK + _MH
_C_END = _C_MKS + _MH


def _prep_kernel(x_ref, win_ref, wmt_ref, wqt_ref, wqst_ref, wk_ref, wvt_ref, qg_ref, kvg_ref,
                 cat_ref, sat_ref, cbt_ref, sbt_ref,
                 qat_ref, ka_ref, vat_ref, mqt_ref, mk_ref, mvt_ref, kmean_scr):
    c = pl.program_id(1)

    @pl.when(c == 0)
    def _():
        kmean_scr[...] = jnp.zeros_like(kmean_scr)

    xb = x_ref[...].astype(BF16)
    cat = cat_ref[...]
    sat = sat_ref[...]
    cbt = cbt_ref[...]
    sbt = sbt_ref[...]
    const = lambda v, n: jnp.full((n, ROW_TILE), v, F32)
    tail = SLOT - MLA_NOPE - MLA_ROPE
    rest = MOBA_HD - MOBA_ROT
    ca = jnp.concatenate([const(1.0, MLA_NOPE), cat, const(0.0, tail)], axis=0).T
    sa = jnp.concatenate([const(0.0, MLA_NOPE), sat, const(0.0, tail)], axis=0).T
    cb = jnp.concatenate([cbt, const(1.0, rest), cbt, const(1.0, rest)], axis=0).T
    sb = jnp.concatenate([sbt, const(0.0, rest), sbt, const(0.0, rest)], axis=0).T
    lane =lax.broadcasted_iota(jnp.int32, (ROW_TILE, SLOT), 1)
    head_lanes = lane < HEAD_LANES
    ones_rows = (lax.broadcasted_iota(jnp.int32, (SLOT - HEAD_LANES, ROW_TILE), 0) == ONES_LANE - HEAD_LANES).astype(F32)

    ql = _dot(xb, win_ref[:, _C_QL:_C_KVL])
    kvl = _dot(xb, win_ref[:, _C_KVL:_C_KR])
    kr = _dot(xb, win_ref[:, _C_KR:_C_KRS])
    krs = _dot(xb, win_ref[:, _C_KRS:_C_MK])
    qn = _rms(ql, qg_ref[...]).astype(BF16)
    kvn = _rms(kvl, kvg_ref[...]).astype(BF16)
    q_t = _dot_nt(wqt_ref[...], qn)
    qs_t = _dot_nt(wqst_ref[...], qn)
    kn = _dot(kvn, wk_ref[...])
    v_t = _dot_nt(wvt_ref[...], kvn)
    scale_a = LOG2E / math.sqrt(MLA_NOPE + MLA_ROPE)
    kro = kr * ca + krs * sa
    qd = MLA_NOPE + MLA_ROPE
    q_pad = jnp.zeros((SLOT - qd, ROW_TILE), F32)
    for h in range(MLA_HEADS):
        sl = slice(h * SLOT, (h + 1) * SLOT)
        q_rope = q_t[h * qd + MLA_NOPE:(h + 1) * qd] * cat + qs_t[h * MLA_ROPE:(h + 1) * MLA_ROPE] * sat
        q_slot = jnp.concatenate([q_t[h * qd:h * qd + MLA_NOPE], q_rope, q_pad], axis=0)
        qat_ref[0, sl, :] = (q_slot * scale_a).astype(BF16)
        ka_ref[:, sl] = (kn[:, sl] + kro).astype(BF16)
        vat_ref[0, sl, :] = jnp.concatenate([v_t[h * MLA_V:(h + 1) * MLA_V], ones_rows], axis=0).astype(BF16)

    mk = _dot(xb, win_ref[:, _C_MK:_C_MKS])
    mks = _dot(xb, win_ref[:, _C_MKS:_C_END])
    npair = _MH // LANES
    pair = lambda a, j: a[:, j * LANES:(j + 1) * LANES]
    mk_rot = [pair(mk, j) * cb + pair(mks, j) * sb for j in range(npair)]
    mk_all = jnp.concatenate(mk_rot, axis=1)

    nrot = MOBA_HEADS * MOBA_ROT
    mq_t = _dot_nt(wmt_ref[0:_MH, :], xb)
    mqs_t = _dot_nt(wmt_ref[_MH:_MH + nrot, :], xb)
    mv_t = _dot_nt(wmt_ref[_MH + nrot:2 * _MH + nrot, :], xb)
    head = lambda a, h: a[h * MOBA_HD:(h + 1) * MOBA_HD]
    mq_rot_t = [jnp.concatenate([head(mq_t, h)[0:MOBA_ROT] * cbt + mqs_t[h * MOBA_ROT:(h + 1) * MOBA_ROT] * sbt,
                                 head(mq_t, h)[MOBA_ROT:]], axis=0) for h in range(MOBA_HEADS)]

    nrow = MOBA_HEADS * SUBLANES
    row_i = lax.broadcasted_iota(jnp.int32, (nrow, _MH), 0)
    lane_i = lax.broadcasted_iota(jnp.int32, (nrow, _MH), 1)
    kmean_c = jnp.mean(mk_all, axis=0, keepdims=True)
    put = ((row_i % SUBLANES) == c) & ((lane_i // MOBA_HD) == (row_i // SUBLANES))
    table = kmean_scr[...]
    gate_t = _dot(table, jnp.concatenate(mq_rot_t, axis=0), precision=lax.Precision.HIGHEST)
    kmean_scr[...] = jnp.where(put, jnp.broadcast_to(kmean_c, (nrow, _MH)), table)

    n_idx = lax.broadcasted_iota(jnp.int32, (SUBLANES, ROW_TILE), 0)
    valid = n_idx < c
    scale_b = LOG2E / math.sqrt(MOBA_HD)
    zero_rows = jnp.zeros((SLOT - HEAD_LANES - SUBLANES, ROW_TILE), F32)
    block_onehot = (lane == BIAS_LANE + c).astype(F32)
    for h in range(MOBA_HEADS):
        g = jnp.where(valid, gate_t[h * SUBLANES:(h + 1) * SUBLANES, :], -jnp.inf)
        rank = jnp.zeros((SUBLANES, ROW_TILE), jnp.int32)
        for k in range(1, SUBLANES):
            other = pltpu.roll(g, k, axis=0)
            other_n = pltpu.roll(n_idx, k, axis=0)
            beats = (other > g) | ((other == g) & (other_n < n_idx))
            rank = rank + beats.astype(jnp.int32)
        keep = (valid & (rank < MOBA_TOPK)) | (n_idx == c)
        bias = jnp.where(keep, 0.0, NEG_BIG)
        sl = slice(h * SLOT, (h + 1) * SLOT)
        mqt_ref[0, sl, :] = jnp.concatenate([mq_rot_t[h] * scale_b, bias, zero_rows], axis=0).astype(BF16)
        mvt_ref[0, sl, :] = jnp.concatenate([head(mv_t, h), ones_rows], axis=0).astype(BF16)
        j, hh = divmod(h, 2)
        k_h = mk_rot[j] if hh == 0 else pltpu.roll(mk_rot[j], HEAD_LANES, axis=1)
        mk_ref[:, sl] = jnp.where(head_lanes, k_h, block_onehot).astype(BF16)


def _prep(x2, win, wmt, wqt, wqst, wk, wvt, qg, kvg, tables, B, T):
    N, D = x2.shape
    nt = T // ROW_TILE
    row = lambda b, c: (b * nt + c, 0)
    col = lambda b, c: (0, b * nt + c)
    full = lambda b, c: (0, 0)
    cat, sat, cbt, sbt = tables
    width = MLA_HEADS * SLOT

    def rows(w):
        return pl.BlockSpec((ROW_TILE, w), row)

    def cols(a):
        return pl.BlockSpec((a.shape[0], ROW_TILE), col)

    def whole(a):
        return pl.BlockSpec(a.shape, full)

    rowmajor = jax.ShapeDtypeStruct((N, width), BF16)
    transposed = jax.ShapeDtypeStruct((B * nt, width, ROW_TILE), BF16)
    t_spec = pl.BlockSpec((1, width, ROW_TILE), lambda b, c: (b * nt + c, 0, 0))
    return pl.pallas_call(
        _prep_kernel,
        out_shape=(transposed, rowmajor, transposed, transposed, rowmajor, transposed),
        grid=(B, nt),
        in_specs=[rows(D), whole(win), whole(wmt), whole(wqt), whole(wqst), whole(wk), whole(wvt), whole(qg), whole(kvg),
                  cols(cat), cols(sat), cols(cbt), cols(sbt)],
        out_specs=(t_spec, rows(width), t_spec, t_spec, rows(width), t_spec),
        scratch_shapes=[pltpu.VMEM((MOBA_HEADS * SUBLANES, _MH), F32)],
        compiler_params=pltpu.CompilerParams(dimension_semantics=("arbitrary", "arbitrary"),
                                             vmem_limit_bytes=VMEM_LIMIT_BYTES),
        name="prep",
    )(x2, win, wmt, wqt, wqst, wk, wvt, qg, kvg, cat, sat, cbt, sbt)


def _attn_kernel(qt_ref, k_ref, vt_ref, o_ref, s_scr):
    i = pl.program_id(2)
    nh = ATTN_HEADS_PER_STEP
    slot = lambda h: slice(h * SLOT, (h + 1) * SLOT)
    nblk = i + 1
    npair = nblk // 2
    q_pos = i * TQ + lax.broadcasted_iota(jnp.int32, (1, TQ), 1)

    def fold(s):
        out = s[0:SUBLANES]
        for t in range(1, s.shape[0] // SUBLANES):
            out = jnp.maximum(out, s[t * SUBLANES:(t + 1) * SUBLANES])
        return out

    def score_blocks(j, n, mrun):
        j0 = pl.multiple_of(j * TQ, TQ)
        visible = (j0 + lax.broadcasted_iota(jnp.int32, (n * TQ, TQ), 0)) <= q_pos
        out = []
        for h in range(nh):
            s = _dot(k_ref[pl.ds(j0, n * TQ), slot(h)], qt_ref[0, slot(h), :])
            s = jnp.where(visible, s, -jnp.inf)
            s_scr[h, pl.ds(j, n)] = s.reshape(n, TQ, TQ)
            out.append(jnp.maximum(mrun[h], fold(s)))
        return tuple(out)

    mrun = tuple(jnp.full((SUBLANES, TQ), -jnp.inf, F32) for _ in range(nh))
    mrun = lax.fori_loop(0, npair, lambda jj, m: score_blocks(2 * jj, 2, m), mrun)
    mrun = lax.fori_loop(2 * npair, nblk, lambda j, m: score_blocks(j, 1, m), mrun)
    ms = [jnp.max(m, axis=0, keepdims=True) for m in mrun]

    def accumulate(j, n, acc):
        out = []
        for h in range(nh):
            p = jnp.exp2(s_scr[h, pl.ds(j, n)].reshape(n * TQ, TQ) - ms[h]).astype(BF16)
            vt = jnp.concatenate([vt_ref[j + t, slot(h), :] for t in range(n)], axis=1)
            out.append(acc[h] + _dot(vt, p))
        return tuple(out)

    acc = tuple(jnp.zeros((SLOT, TQ), F32) for _ in range(nh))
    acc = lax.fori_loop(0, npair, lambda jj, a: accumulate(2 * jj, 2, a), acc)
    acc = lax.fori_loop(2 * npair, nblk, lambda j, a: accumulate(j, 1, a), acc)
    outs = [a[0:HEAD_LANES] / a[ONES_LANE:ONES_LANE + 1] for a in acc]
    for jj in range(nh // 2):
        both = jnp.concatenate([outs[2 * jj], outs[2 * jj + 1]], axis=0)
        o_ref[:, jj * LANES:(jj + 1) * LANES] = both.T.astype(o_ref.dtype)


def _attention(qt, k, vt, B, T, name):
    N = k.shape[0]
    nq = T // TQ
    nh = ATTN_HEADS_PER_STEP
    heads = k.shape[1] // SLOT
    return pl.pallas_call(
        _attn_kernel,
        out_shape=jax.ShapeDtypeStruct((N, heads * HEAD_LANES), BF16),
        grid=(B, heads // nh, nq),
        in_specs=[pl.BlockSpec((1, nh * SLOT, TQ), lambda b, g, i: (b * nq + i, g, 0)),
                  pl.BlockSpec((T, nh * SLOT), lambda b, g, i: (b, g)),
                  pl.BlockSpec((nq, nh * SLOT, TQ), lambda b, g, i: (b, g, 0))],
        out_specs=pl.BlockSpec((TQ, nh * HEAD_LANES), lambda b, g, i: (b * nq + i, g)),
        scratch_shapes=[pltpu.VMEM((nh, nq, TQ, TQ), F32)],
        compiler_params=pltpu.CompilerParams(dimension_semantics=("arbitrary", "arbitrary", "arbitrary"),
                                             vmem_limit_bytes=VMEM_LIMIT_BYTES),
        name=name,
    )(qt, k, vt)


def _oproj_kernel(alpha, a_ref, m_ref, x_ref, wo_ref, g_ref, b_ref, wr_ref, br_ref,
                  x1_ref, x1t_ref, route_ref, gates_ref, cnt_ref, carry_scr):
    i = pl.program_id(0)

    @pl.when(i == 0)
    def _():
        carry_scr[...] = jnp.zeros_like(carry_scr)

    mix = _dot(jnp.concatenate([a_ref[...], m_ref[...]], axis=1), wo_ref[...])
    x1 = _layer_norm(alpha * x_ref[...] + mix, g_ref[...], b_ref[...])
    x1_ref[...] = x1
    _rows_to_tiles(x1t_ref, x1)

    ne = br_ref.shape[0]
    rows = x1.shape[0]
    x_hi = x1.astype(BF16)
    x_lo = (x1 - x_hi.astype(F32)).astype(BF16)
    both = _dot(x_hi, wr_ref[...])
    logits_rm = both[:, :LANES] + both[:, LANES:] + _dot(x_lo, wr_ref[:, :LANES])
    logits = logits_rm.T[0:ne] + br_ref[...]
    expert = lax.broadcasted_iota(jnp.int32, logits.shape, 0)
    expert_f = expert.astype(F32)
    vals, idxs = [], []
    work = logits
    for _ in range(MOE_TOPK):
        mx = jnp.max(work, axis=0, keepdims=True)
        ix = jnp.min(jnp.where(work == mx, expert_f, float(ne)), axis=0, keepdims=True).astype(jnp.int32)
        vals.append(mx)
        idxs.append(ix)
        work = jnp.where(expert == ix, -jnp.inf, work)
    exps = [jnp.exp(v - vals[0]) for v in vals]
    den = exps[0]
    for e in exps[1:]:
        den = den + e

    onehot = jnp.zeros(logits.shape, F32)
    for ix in idxs:
        onehot = onehot + (expert == ix).astype(F32)
    r = lax.broadcasted_iota(jnp.int32, (rows, rows), 0)
    cidx = lax.broadcasted_iota(jnp.int32, (rows, rows), 1)
    earlier = (r < cidx).astype(BF16)
    carry = carry_scr[:, 0:1]
    before = _dot(onehot.astype(BF16), earlier) + carry
    row8 = lax.broadcasted_iota(jnp.int32, (SUBLANES, rows), 0)
    route = jnp.zeros((SUBLANES, rows), jnp.int32)
    gates = jnp.zeros((SUBLANES, rows), F32)
    for k in range(MOE_TOPK):
        rank = jnp.sum(jnp.where(expert == idxs[k], before, 0.0), axis=0, keepdims=True).astype(jnp.int32)
        route = jnp.where(row8 == k, idxs[k], route)
        route = jnp.where(row8 == MOE_TOPK + k, rank, route)
        gates = jnp.where(row8 == k, exps[k] / den, gates)
    route_ref[...] = route
    gates_ref[...] = jnp.concatenate([gates, jnp.zeros((LANES - SUBLANES, rows), F32)], axis=0).T
    new_carry = carry + jnp.sum(onehot, axis=1, keepdims=True)
    carry_scr[...] = jnp.broadcast_to(new_carry, carry_scr.shape)
    cnt_ref[...] = jnp.broadcast_to(new_carry, cnt_ref.shape)


def _oproj(a, m, x2, wo, g1, b1, wr, br, alpha):
    N, D = x2.shape
    nt = N // OPROJ_TILE
    row = lambda i: (i, 0)
    full = lambda i: (0, 0)
    return pl.pallas_call(
        functools.partial(_oproj_kernel, alpha),
        out_shape=(jax.ShapeDtypeStruct((N, D), F32),
                   jax.ShapeDtypeStruct((N * SUBLANES, LANES), F32),
                   jax.ShapeDtypeStruct((SUBLANES, N), jnp.int32),
                   jax.ShapeDtypeStruct((N, LANES), F32),
                   jax.ShapeDtypeStruct((br.shape[0], LANES), F32)),
        grid=(nt,),
        in_specs=[pl.BlockSpec((OPROJ_TILE, a.shape[1]), row), pl.BlockSpec((OPROJ_TILE, m.shape[1]), row),
                  pl.BlockSpec((OPROJ_TILE, D), row), pl.BlockSpec(wo.shape, full),
                  pl.BlockSpec(g1.shape, full), pl.BlockSpec(b1.shape, full),
                  pl.BlockSpec(wr.shape, full), pl.BlockSpec(br.shape, full)],
        out_specs=(pl.BlockSpec((OPROJ_TILE, D), row), pl.BlockSpec((OPROJ_TILE * SUBLANES, LANES), row),
                   pl.BlockSpec((SUBLANES, OPROJ_TILE), lambda i: (0, i)),
                   pl.BlockSpec((OPROJ_TILE, LANES), row), pl.BlockSpec((br.shape[0], LANES), full)),
        scratch_shapes=[pltpu.VMEM((br.shape[0], LANES), F32)],
        compiler_params=pltpu.CompilerParams(dimension_semantics=("arbitrary",),
                                             vmem_limit_bytes=VMEM_LIMIT_BYTES),
        name="oproj_router",
    )(a, m, x2, wo, g1, b1, wr, br)


def _dispatch_kernel(dest_ref, tail_ref, x1_ref, xs_ref, zero_scr, sem):
    i = pl.program_id(0)

    @pl.when(i == 0)
    def _():
        zero_scr[...] = jnp.zeros_like(zero_scr)

        def tail_copy(e):
            return pltpu.make_async_copy(zero_scr, xs_ref.at[pl.ds(pl.multiple_of(tail_ref[e] * SUBLANES, GROUP_SUB), GROUP_SUB)], sem)

        def start(e, _):
            @pl.when(tail_ref[e] >= 0)
            def _():
                tail_copy(e).start()
            return 0

        def wait(e, _):
            @pl.when(tail_ref[e] >= 0)
            def _():
                tail_copy(e).wait()
            return 0

        lax.fori_loop(0, N_EXPERTS, start, 0)
        lax.fori_loop(0, N_EXPERTS, wait, 0)

        def spare_copy(blk):
            return pltpu.make_async_copy(zero_scr, xs_ref.at[pl.ds(pl.multiple_of(blk * GROUP_SUB, GROUP_SUB), GROUP_SUB)], sem)

        def start_spare(blk, _):
            spare_copy(blk).start()
            return 0

        def wait_spare(blk, _):
            spare_copy(blk).wait()
            return 0

        nblk = xs_ref.shape[0] // GROUP_SUB
        lax.fori_loop(tail_ref[N_EXPERTS], nblk, start_spare, 0)
        lax.fori_loop(tail_ref[N_EXPERTS], nblk, wait_spare, 0)

    base = i * DISPATCH_TILE
    n_tok = pl.num_programs(0) * DISPATCH_TILE

    def row_copy(t, k):
        d = dest_ref[k * n_tok + base + t]
        return pltpu.make_async_copy(x1_ref.at[pl.ds(pl.multiple_of(t * SUBLANES, SUBLANES), SUBLANES)],
                                     xs_ref.at[pl.ds(pl.multiple_of(d * SUBLANES, SUBLANES), SUBLANES)], sem)

    def start_rows(t, _):
        for k in range(MOE_TOPK):
            row_copy(t, k).start(priority=k % DMA_PRIORITIES)
        return 0

    lax.fori_loop(0, DISPATCH_TILE, start_rows, 0, unroll=ISSUE_UNROLL)
    for k in range(MOE_TOPK):
        pltpu.make_async_copy(x1_ref, xs_ref.at[pl.ds(0, DISPATCH_TILE * SUBLANES)], sem).wait()


def _dispatch(dest, tail, x1t, n_rows):
    return pl.pallas_call(
        _dispatch_kernel,
        out_shape=jax.ShapeDtypeStruct((n_rows * SUBLANES, LANES), F32),
        grid_spec=pltpu.PrefetchScalarGridSpec(
            num_scalar_prefetch=2,
            grid=(x1t.shape[0] // (DISPATCH_TILE * SUBLANES),),
            in_specs=[pl.BlockSpec((DISPATCH_TILE * SUBLANES, LANES), lambda i, d, t: (i, 0))],
            out_specs=pl.BlockSpec(memory_space=pl.ANY),
            scratch_shapes=[pltpu.VMEM((GROUP_SUB, LANES), F32), pltpu.SemaphoreType.DMA(())],
        ),
        compiler_params=pltpu.CompilerParams(dimension_semantics=("arbitrary",),
                                             vmem_limit_bytes=VMEM_LIMIT_BYTES),
        name="dispatch",
    )(dest, tail, x1t)


def _experts_kernel(be_ref, slot_ref, nxt_ref, full_ref, nused_ref, x_ref, wg_hbm, bg_ref, wu_hbm, bu_ref, wd_hbm, bd_ref,
                    y_ref, wg_st, wu_st, wd_st, wg_bf, wu_bf, wd_bf, sems):
    i = pl.program_id(0)
    prev = be_ref[jnp.maximum(i - 1, 0)]
    changed = (i == 0) | (be_ref[i] != prev)
    active = i < nused_ref[0]
    slot = slot_ref[i]

    def weight_copies(expert, s):
        return (pltpu.make_async_copy(wg_hbm.at[expert], wg_st.at[s], sems.at[s, 0]),
                pltpu.make_async_copy(wu_hbm.at[expert], wu_st.at[s], sems.at[s, 1]),
                pltpu.make_async_copy(wd_hbm.at[expert], wd_st.at[s], sems.at[s, 2]))

    @pl.when(i == 0)
    def _():
        for cp in weight_copies(be_ref[0], slot):
            cp.start()

    @pl.when(active & changed)
    def _():
        for cp in weight_copies(be_ref[i], slot):
            cp.wait()

        @pl.when(nxt_ref[i] >= 0)
        def _():
            for cp in weight_copies(nxt_ref[i], 1 - slot):
                cp.start()

        wg_bf[...] = wg_st[slot].astype(BF16)
        wu_bf[...] = wu_st[slot].astype(BF16)
        wd_bf[...] = wd_st[slot].astype(BF16)

    def mlp(nrows):
        rows = pl.ds(0, nrows * SUBLANES)
        xb = _tiles_to_rows(x_ref.at[rows], nrows).astype(BF16)
        g = jnp.minimum(_dot(xb, wg_bf[...]) + bg_ref[0], SWIGLU_LIMIT)
        u = jnp.clip(_dot(xb, wu_bf[...]) + bu_ref[0], -SWIGLU_LIMIT, SWIGLU_LIMIT)
        h = g * (1.0 / (1.0 + jnp.exp(-SWIGLU_ALPHA * g))) * (u + 1.0)
        _rows_to_tiles(y_ref.at[rows], _dot(h.astype(BF16), wd_bf[...]) + bd_ref[0])

    whole = active & (full_ref[i] > 0)

    @pl.when(whole)
    def _():
        mlp(MOE_GROUP)

    @pl.when(active & jnp.logical_not(whole))
    def _():
        mlp(MOE_HALF)

    @pl.when(jnp.logical_not(active))
    def _():
        y_ref[pl.ds(0, HALF_SUB), :] = jnp.zeros((HALF_SUB, LANES), F32)

    @pl.when(jnp.logical_not(whole))
    def _():
        y_ref[pl.ds(HALF_SUB, HALF_SUB), :] = jnp.zeros((HALF_SUB, LANES), F32)


def _experts(blk_expert, blk_slot, blk_next, blk_full, n_used, xs, wg, bg, wu, bu, wd, bd):
    E, D, F = wg.shape
    nblk = xs.shape[0] // GROUP_SUB

    def rowmap(i, be, sl, nx, fl, nu):
        return (jnp.minimum(i, nu[0] - 1), 0)

    def bmap(i, be, sl, nx, fl, nu):
        return (be[i], 0, 0)

    hbm = pl.BlockSpec(memory_space=pl.ANY)
    return pl.pallas_call(
        _experts_kernel,
        out_shape=jax.ShapeDtypeStruct(xs.shape, F32),
        grid_spec=pltpu.PrefetchScalarGridSpec(
            num_scalar_prefetch=5,
            grid=(nblk,),
            in_specs=[pl.BlockSpec((GROUP_SUB, LANES), rowmap),
                      hbm, pl.BlockSpec((1, 1, F), bmap),
                      hbm, pl.BlockSpec((1, 1, F), bmap),
                      hbm, pl.BlockSpec((1, 1, D), bmap)],
            out_specs=pl.BlockSpec((GROUP_SUB, LANES), lambda i, be, sl, nx, fl, nu: (i, 0)),
            scratch_shapes=[pltpu.VMEM((2, D, F), F32), pltpu.VMEM((2, D, F), F32), pltpu.VMEM((2, F, D), F32),
                            pltpu.VMEM((D, F), BF16), pltpu.VMEM((D, F), BF16), pltpu.VMEM((F, D), BF16),
                            pltpu.SemaphoreType.DMA((2, 3))],
        ),
        compiler_params=pltpu.CompilerParams(dimension_semantics=("arbitrary",),
                                             vmem_limit_bytes=VMEM_LIMIT_BYTES),
        name="experts",
    )(blk_expert, blk_slot, blk_next, blk_full, n_used, xs, wg, bg, wu, bu, wd, bd)


def _combine_kernel(alpha, dest_ref, gates_ref, x1_ref, g_ref, b_ref, y_ref, o_ref, ybuf, sems):
    i = pl.program_id(0)
    buf = i % 2

    n_tok = pl.num_programs(0) * COMBINE_TILE

    def gather_tile(tile, into):
        base = tile * COMBINE_TILE

        def start_rows(t, _):
            for k in range(MOE_TOPK):
                d = dest_ref[k * n_tok + base + t]
                pltpu.make_async_copy(y_ref.at[pl.ds(pl.multiple_of(d * SUBLANES, SUBLANES), SUBLANES)],
                                      ybuf.at[into, k, pl.ds(pl.multiple_of(t * SUBLANES, SUBLANES), SUBLANES)],
                                      sems.at[into]).start(priority=k % DMA_PRIORITIES)
            return 0

        lax.fori_loop(0, COMBINE_TILE, start_rows, 0, unroll=ISSUE_UNROLL)

    @pl.when(i == 0)
    def _():
        gather_tile(0, 0)

    @pl.when(i + 1 < pl.num_programs(0))
    def _():
        gather_tile(i + 1, 1 - buf)

    for k in range(MOE_TOPK):
        pltpu.make_async_copy(y_ref.at[pl.ds(0, COMBINE_TILE * SUBLANES)], ybuf.at[buf, k], sems.at[buf]).wait()

    gates = gates_ref[...]
    ffn = gates[:, 0:1] * _tiles_to_rows(ybuf.at[buf, 0], COMBINE_TILE)
    for k in range(1, MOE_TOPK):
        ffn = ffn + gates[:, k:k + 1] * _tiles_to_rows(ybuf.at[buf, k], COMBINE_TILE)
    o_ref[...] = _layer_norm(alpha * x1_ref[...] + ffn, g_ref[...], b_ref[...])


def _combine(dest, gates, x1, g2, b2, y_rows, alpha):
    N, D = x1.shape
    row = lambda i, d: (i, 0)
    full = lambda i, d: (0, 0)
    return pl.pallas_call(
        functools.partial(_combine_kernel, alpha),
        out_shape=jax.ShapeDtypeStruct((N, D), F32),
        grid_spec=pltpu.PrefetchScalarGridSpec(
            num_scalar_prefetch=1,
            grid=(N // COMBINE_TILE,),
            in_specs=[pl.BlockSpec((COMBINE_TILE, LANES), row), pl.BlockSpec((COMBINE_TILE, D), row),
                      pl.BlockSpec(g2.shape, full), pl.BlockSpec(b2.shape, full),
                      pl.BlockSpec(memory_space=pl.ANY)],
            out_specs=pl.BlockSpec((COMBINE_TILE, D), row),
            scratch_shapes=[pltpu.VMEM((2, MOE_TOPK, COMBINE_TILE * SUBLANES, LANES), F32),
                            pltpu.SemaphoreType.DMA((2,))],
        ),
        compiler_params=pltpu.CompilerParams(dimension_semantics=("arbitrary",),
                                             vmem_limit_bytes=VMEM_LIMIT_BYTES),
        name="combine",
    )(dest, gates, x1, g2, b2, y_rows)


def _rot_partner(w, half):
    return jnp.concatenate([-w[..., half:2 * half], w[..., :half]], axis=-1)


def _layer_weights(w_in, w_q_b, w_kv_b):
    D = w_in.shape[0]
    o1 = Q_LORA
    o2 = o1 + KV_LORA
    o3 = o2 + MLA_ROPE
    w_ql, w_kvl, w_kr = w_in[:, :o1], w_in[:, o1:o2], w_in[:, o2:o3]
    w_mq, w_mk, w_mv = w_in[:, o3:o3 + _MH], w_in[:, o3 + _MH:o3 + 2 * _MH], w_in[:, o3 + 2 * _MH:]
    zpad = lambda n: jnp.zeros((D, n), w_in.dtype)
    tail = SLOT - MLA_NOPE - MLA_ROPE
    kr_slot = jnp.concatenate([zpad(MLA_NOPE), w_kr, zpad(tail)], axis=1)
    krs_slot = jnp.concatenate([zpad(MLA_NOPE), _rot_partner(w_kr, MLA_ROPE // 2), zpad(tail)], axis=1)

    def moba_partner(w):
        w3 = w.reshape(D, MOBA_HEADS, MOBA_HD)
        part = jnp.concatenate([_rot_partner(w3[..., :MOBA_ROT], MOBA_ROT // 2),
                                jnp.zeros((D, MOBA_HEADS, MOBA_HD - MOBA_ROT), w.dtype)], axis=-1)
        return part.reshape(D, _MH)

    win = jnp.concatenate([w_ql, w_kvl, kr_slot, krs_slot, w_mk, moba_partner(w_mk)], axis=1).astype(BF16)
    mq_partner = _rot_partner(w_mq.reshape(D, MOBA_HEADS, MOBA_HD)[..., :MOBA_ROT], MOBA_ROT // 2)
    wmt = jnp.concatenate([w_mq, mq_partner.reshape(D, MOBA_HEADS * MOBA_ROT), w_mv], axis=1).T.astype(BF16)

    wq3 = w_q_b.reshape(Q_LORA, MLA_HEADS, MLA_NOPE + MLA_ROPE)
    wqs = _rot_partner(wq3[..., MLA_NOPE:], MLA_ROPE // 2).reshape(Q_LORA, MLA_HEADS * MLA_ROPE)
    wkv3 = w_kv_b.reshape(KV_LORA, MLA_HEADS, MLA_NOPE + MLA_V)
    wk = jnp.concatenate([wkv3[..., :MLA_NOPE], jnp.zeros((KV_LORA, MLA_HEADS, SLOT - MLA_NOPE), w_kv_b.dtype)],
                         axis=-1).reshape(KV_LORA, MLA_HEADS * SLOT).astype(BF16)
    wv = wkv3[..., MLA_NOPE:].reshape(KV_LORA, MLA_HEADS * MLA_V)
    return win, wmt, w_q_b.T.astype(BF16), wqs.T.astype(BF16), wk, wv.T.astype(BF16)


def _rope_tables(positions):
    pos = positions.astype(F32).reshape(1, -1)

    def cs(d_rot):
        inv_freq = ROPE_THETA ** (-jnp.arange(0, d_rot, 2, dtype=F32) / d_rot)
        ang = jnp.concatenate([inv_freq, inv_freq]).reshape(d_rot, 1) * pos
        return jnp.cos(ang), jnp.sin(ang)

    return cs(MLA_ROPE) + cs(MOBA_ROT)


def kernel(x, positions, w_in, q_a_norm, w_q_b, kv_a_norm, w_kv_b, w_o, ln1_g, ln1_b, w_router, b_router,
           w_gate, b_gate, w_up, b_up, w_down, b_down, ln2_g, ln2_b):
    B, T, D = x.shape
    depth = w_in.shape[0]
    alpha = (2.0 * depth) ** 0.25
    N = B * T
    assert T % MOBA_BLOCK == 0 and T // MOBA_BLOCK <= SUBLANES and N % OPROJ_TILE == 0
    assert D == SUBLANES * LANES
    n_asg = N * MOE_TOPK
    n_rows = n_asg + N_EXPERTS * MOE_GROUP
    tables = _rope_tables(positions)
    h = x.reshape(N, D)
    for l in range(depth):
        win, wmt, wqt, wqst, wk, wvt = _layer_weights(w_in[l], w_q_b[l], w_kv_b[l])
        qa, ka, va, mq, mk, mv = _prep(h, win, wmt, wqt, wqst, wk, wvt, q_a_norm[l].reshape(1, -1),
                                       kv_a_norm[l].reshape(1, -1), tables, B, T)
        a = _attention(qa, ka, va, B, T, "mla_attention")
        m = _attention(mq, mk, mv, B, T, "moba_attention")
        wr_pad = jnp.concatenate([w_router[l], jnp.zeros((D, LANES - N_EXPERTS), F32)], axis=1)
        wr_hi = wr_pad.astype(BF16)
        wr = jnp.concatenate([wr_hi, (wr_pad - wr_hi.astype(F32)).astype(BF16)], axis=1)
        br = b_router[l].reshape(N_EXPERTS, 1)
        x1, x1t, route, gates, cnt = _oproj(a, m, h, w_o[l].astype(BF16), ln1_g[l].reshape(1, D), ln1_b[l].reshape(1, D),
                                       wr, br, alpha)
        counts = cnt[:, 0].astype(jnp.int32)
        padded = (counts + MOE_GROUP - 1) // MOE_GROUP * MOE_GROUP
        pad_end = jnp.cumsum(padded).astype(jnp.int32)
        pad_start = pad_end - padded
        e_idx = route[:MOE_TOPK]
        group_start = jnp.sum(jnp.where(e_idx[..., None] == jnp.arange(N_EXPERTS, dtype=jnp.int32), pad_start, 0), axis=-1)
        dest = (group_start + route[MOE_TOPK:2 * MOE_TOPK]).reshape(n_asg)
        n_used = (pad_end[-1:] // MOE_GROUP).astype(jnp.int32)
        tail = jnp.concatenate([jnp.where(padded > 0, pad_end - MOE_GROUP, -1).astype(jnp.int32), n_used])
        blk_start = jnp.arange(n_rows // MOE_GROUP, dtype=jnp.int32) * MOE_GROUP
        blk_expert = jnp.minimum(jnp.sum((blk_start[:, None] >= pad_end[None, :]).astype(jnp.int32), axis=1),
                                 N_EXPERTS - 1)
        nblk = n_rows // MOE_GROUP
        starts_group = jnp.concatenate([jnp.ones((1,), jnp.int32), (blk_expert[1:] != blk_expert[:-1]).astype(jnp.int32)])
        blk_slot = (jnp.cumsum(starts_group) - 1) % 2
        next_start = pad_end[blk_expert] // MOE_GROUP
        blk_next = jnp.where(next_start < n_used[0], blk_expert[jnp.minimum(next_start, nblk - 1)], -1).astype(jnp.int32)
        real_end = (pad_start + counts)[blk_expert]
        blk_full = (real_end > blk_start + MOE_HALF).astype(jnp.int32)
        xs = _dispatch(dest, tail, x1t, n_rows)
        y_rows = _experts(blk_expert, blk_slot.astype(jnp.int32), blk_next, blk_full, n_used, xs, w_gate[l], b_gate[l].reshape(N_EXPERTS, 1, -1),
                          w_up[l], b_up[l].reshape(N_EXPERTS, 1, -1), w_down[l], b_down[l].reshape(N_EXPERTS, 1, -1))
        h = _combine(dest, gates, x1, ln2_g[l].reshape(1, D), ln2_b[l].reshape(1, D), y_rows, alpha)
    return h.reshape(B, T, D)
```

```python
import functools
import math

import jax
import jax.numpy as jnp
from jax import lax
from jax.experimental import pallas as pl
from jax.experimental.pallas import tpu as pltpu

ROPE_THETA = 500000.0
MLA_HEADS = 8
MLA_NOPE = 64
MLA_ROPE = 32
MLA_V = 64
Q_LORA = 256
KV_LORA = 128
MOBA_HEADS = 8
MOBA_HD = 64
MOBA_ROT = MOBA_HD // 4
MOBA_BLOCK = 256
MOBA_TOPK = 3
N_EXPERTS = 32
MOE_TOPK = 4
SWIGLU_LIMIT = 7.0
SWIGLU_ALPHA = 1.702
RMS_EPS = 1e-6
LN_EPS = 1e-5

LANES = 128
SUBLANES = 8
VMEM_LIMIT_BYTES = 56 * 1024 * 1024

SLOT = LANES
TQ = MOBA_BLOCK
ROW_TILE = 256
OPROJ_TILE = 512
MOE_GROUP = 512
MOE_HALF = MOE_GROUP // 2
GROUP_SUB = MOE_GROUP * SUBLANES
HALF_SUB = MOE_HALF * SUBLANES
DISPATCH_TILE = 512
COMBINE_TILE = 512
ISSUE_UNROLL = 4
DMA_PRIORITIES = 2
HEAD_LANES = 64
ONES_LANE = 64
BIAS_LANE = 64
ATTN_HEADS_PER_STEP = 8
LOG2E = math.log2(math.e)
NEG_BIG = -(2.0 ** 100)

F32 = jnp.float32
BF16 = jnp.bfloat16
NT_DIMS = (((1,), (1,)), ((), ()))


def _dot(a, b, precision=None):
    return jnp.dot(a, b, preferred_element_type=F32, precision=precision)


def _dot_nt(a, b, precision=None):
    return lax.dot_general(a, b, NT_DIMS, preferred_element_type=F32, precision=precision)


def _rows_to_tiles(ref, x):
    rows = x.shape[0]
    for c in range(SUBLANES):
        ref[pl.ds(c, rows, stride=SUBLANES), :] = x[:, c * LANES:(c + 1) * LANES]


def _tiles_to_rows(ref, rows):
    return jnp.concatenate([ref[pl.ds(c, rows, stride=SUBLANES), :] for c in range(SUBLANES)], axis=1)


def _rms(x, g):
    return x * lax.rsqrt(jnp.mean(x * x, axis=-1, keepdims=True) + RMS_EPS) * g


def _layer_norm(x, g, b):
    mu = jnp.mean(x, axis=-1, keepdims=True)
    xc = x - mu
    var = jnp.mean(xc * xc, axis=-1, keepdims=True)
    return xc * lax.rsqrt(var + LN_EPS) * g + b


_C_QL = 0
_C_KVL = _C_QL + Q_LORA
_C_KR = _C_KVL + KV_LORA
_C_KRS = _C_KR + SLOT
_C_MK = _C_KRS + SLOT
_MH = MOBA_HEADS * MOBA_HD
_C_MKS = _C_MK + _MH
_C_END = _C_MKS + _MH


def _prep_kernel(x_ref, win_ref, wmt_ref, wqt_ref, wqst_ref, wk_ref, wvt_ref, qg_ref, kvg_ref,
                 cat_ref, sat_ref, cbt_ref, sbt_ref,
                 qat_ref, ka_ref, vat_ref, mqt_ref, mk_ref, mvt_ref, kmean_scr):
    c = pl.program_id(1)

    @pl.when(c == 0)
    def _():
        kmean_scr[...] = jnp.zeros_like(kmean_scr)

    xb = x_ref[...].astype(BF16)
    cat = cat_ref[...]
    sat = sat_ref[...]
    cbt = cbt_ref[...]
    sbt = sbt_ref[...]
    const = lambda v, n: jnp.full((n, ROW_TILE), v, F32)
    tail = SLOT - MLA_NOPE - MLA_ROPE
    rest = MOBA_HD - MOBA_ROT
    ca = jnp.concatenate([const(1.0, MLA_NOPE), cat, const(0.0, tail)], axis=0).T
    sa = jnp.concatenate([const(0.0, MLA_NOPE), sat, const(0.0, tail)], axis=0).T
    cb = jnp.concatenate([cbt, const(1.0, rest), cbt, const(1.0, rest)], axis=0).T
    sb = jnp.concatenate([sbt, const(0.0, rest), sbt, const(0.0, rest)], axis=0).T
    lane =lax.broadcasted_iota(jnp.int32, (ROW_TILE, SLOT), 1)
    head_lanes = lane < HEAD_LANES
    ones_rows = (lax.broadcasted_iota(jnp.int32, (SLOT - HEAD_LANES, ROW_TILE), 0) == ONES_LANE - HEAD_LANES).astype(F32)

    ql = _dot(xb, win_ref[:, _C_QL:_C_KVL])
    kvl = _dot(xb, win_ref[:, _C_KVL:_C_KR])
    kr = _dot(xb, win_ref[:, _C_KR:_C_KRS])
    krs = _dot(xb, win_ref[:, _C_KRS:_C_MK])
    qn = _rms(ql, qg_ref[...]).astype(BF16)
    kvn = _rms(kvl, kvg_ref[...]).astype(BF16)
    q_t = _dot_nt(wqt_ref[...], qn)
    qs_t = _dot_nt(wqst_ref[...], qn)
    kn = _dot(kvn, wk_ref[...])
    v_t = _dot_nt(wvt_ref[...], kvn)
    scale_a = LOG2E / math.sqrt(MLA_NOPE + MLA_ROPE)
    kro = kr * ca + krs * sa
    qd = MLA_NOPE + MLA_ROPE
    q_pad = jnp.zeros((SLOT - qd, ROW_TILE), F32)
    for h in range(MLA_HEADS):
        sl = slice(h * SLOT, (h + 1) * SLOT)
        q_rope = q_t[h * qd + MLA_NOPE:(h + 1) * qd] * cat + qs_t[h * MLA_ROPE:(h + 1) * MLA_ROPE] * sat
        q_slot = jnp.concatenate([q_t[h * qd:h * qd + MLA_NOPE], q_rope, q_pad], axis=0)
        qat_ref[0, sl, :] = (q_slot * scale_a).astype(BF16)
        ka_ref[:, sl] = (kn[:, sl] + kro).astype(BF16)
        vat_ref[0, sl, :] = jnp.concatenate([v_t[h * MLA_V:(h + 1) * MLA_V], ones_rows], axis=0).astype(BF16)

    mk = _dot(xb, win_ref[:, _C_MK:_C_MKS])
    mks = _dot(xb, win_ref[:, _C_MKS:_C_END])
    npair = _MH // LANES
    pair = lambda a, j: a[:, j * LANES:(j + 1) * LANES]
    mk_rot = [pair(mk, j) * cb + pair(mks, j) * sb for j in range(npair)]
    mk_all = jnp.concatenate(mk_rot, axis=1)

    nrot = MOBA_HEADS * MOBA_ROT
    mq_t = _dot_nt(wmt_ref[0:_MH, :], xb)
    mqs_t = _dot_nt(wmt_ref[_MH:_MH + nrot, :], xb)
    mv_t = _dot_nt(wmt_ref[_MH + nrot:2 * _MH + nrot, :], xb)
    head = lambda a, h: a[h * MOBA_HD:(h + 1) * MOBA_HD]
    mq_rot_t = [jnp.concatenate([head(mq_t, h)[0:MOBA_ROT] * cbt + mqs_t[h * MOBA_ROT:(h + 1) * MOBA_ROT] * sbt,
                                 head(mq_t, h)[MOBA_ROT:]], axis=0) for h in range(MOBA_HEADS)]

    nrow = MOBA_HEADS * SUBLANES
    row_i = lax.broadcasted_iota(jnp.int32, (nrow, _MH), 0)
    lane_i = lax.broadcasted_iota(jnp.int32, (nrow, _MH), 1)
    kmean_c = jnp.mean(mk_all, axis=0, keepdims=True)
    put = ((row_i % SUBLANES) == c) & ((lane_i // MOBA_HD) == (row_i // SUBLANES))
    table = kmean_scr[...]
    gate_t = _dot(table, jnp.concatenate(mq_rot_t, axis=0), precision=lax.Precision.HIGHEST)
    kmean_scr[...] = jnp.where(put, jnp.broadcast_to(kmean_c, (nrow, _MH)), table)

    n_idx = lax.broadcasted_iota(jnp.int32, (SUBLANES, ROW_TILE), 0)
    valid = n_idx < c
    scale_b = LOG2E / math.sqrt(MOBA_HD)
    zero_rows = jnp.zeros((SLOT - HEAD_LANES - SUBLANES, ROW_TILE), F32)
    block_onehot = (lane == BIAS_LANE + c).astype(F32)
    for h in range(MOBA_HEADS):
        g = jnp.where(valid, gate_t[h * SUBLANES:(h + 1) * SUBLANES, :], -jnp.inf)
        rank = jnp.zeros((SUBLANES, ROW_TILE), jnp.int32)
        for k in range(1, SUBLANES):
            other = pltpu.roll(g, k, axis=0)
            other_n = pltpu.roll(n_idx, k, axis=0)
            beats = (other > g) | ((other == g) & (other_n < n_idx))
            rank = rank + beats.astype(jnp.int32)
        keep = (valid & (rank < MOBA_TOPK)) | (n_idx == c)
        bias = jnp.where(keep, 0.0, NEG_BIG)
        sl = slice(h * SLOT, (h + 1) * SLOT)
        mqt_ref[0, sl, :] = jnp.concatenate([mq_rot_t[h] * scale_b, bias, zero_rows], axis=0).astype(BF16)
        mvt_ref[0, sl, :] = jnp.concatenate([head(mv_t, h), ones_rows], axis=0).astype(BF16)
        j, hh = divmod(h, 2)
        k_h = mk_rot[j] if hh == 0 else pltpu.roll(mk_rot[j], HEAD_LANES, axis=1)
        mk_ref[:, sl] = jnp.where(head_lanes, k_h, block_onehot).astype(BF16)


def _prep(x2, win, wmt, wqt, wqst, wk, wvt, qg, kvg, tables, B, T):
    N, D = x2.shape
    nt = T // ROW_TILE
    row = lambda b, c: (b * nt + c, 0)
    col = lambda b, c: (0, b * nt + c)
    full = lambda b, c: (0, 0)
    cat, sat, cbt, sbt = tables
    width = MLA_HEADS * SLOT

    def rows(w):
        return pl.BlockSpec((ROW_TILE, w), row)

    def cols(a):
        return pl.BlockSpec((a.shape[0], ROW_TILE), col)

    def whole(a):
        return pl.BlockSpec(a.shape, full)

    rowmajor = jax.ShapeDtypeStruct((N, width), BF16)
    transposed = jax.ShapeDtypeStruct((B * nt, width, ROW_TILE), BF16)
    t_spec = pl.BlockSpec((1, width, ROW_TILE), lambda b, c: (b * nt + c, 0, 0))
    return pl.pallas_call(
        _prep_kernel,
        out_shape=(transposed, rowmajor, transposed, transposed, rowmajor, transposed),
        grid=(B, nt),
        in_specs=[rows(D), whole(win), whole(wmt), whole(wqt), whole(wqst), whole(wk), whole(wvt), whole(qg), whole(kvg),
                  cols(cat), cols(sat), cols(cbt), cols(sbt)],
        out_specs=(t_spec, rows(width), t_spec, t_spec, rows(width), t_spec),
        scratch_shapes=[pltpu.VMEM((MOBA_HEADS * SUBLANES, _MH), F32)],
        compiler_params=pltpu.CompilerParams(dimension_semantics=("arbitrary", "arbitrary"),
                                             vmem_limit_bytes=VMEM_LIMIT_BYTES),
        name="prep",
    )(x2, win, wmt, wqt, wqst, wk, wvt, qg, kvg, cat, sat, cbt, sbt)


def _attn_kernel(qt_ref, k_ref, vt_ref, o_ref, s_scr):
    i = pl.program_id(2)
    nh = ATTN_HEADS_PER_STEP
    slot = lambda h: slice(h * SLOT, (h + 1) * SLOT)
    nblk = i + 1
    npair = nblk // 2
    q_pos = i * TQ + lax.broadcasted_iota(jnp.int32, (1, TQ), 1)

    def fold(s):
        out = s[0:SUBLANES]
        for t in range(1, s.shape[0] // SUBLANES):
            out = jnp.maximum(out, s[t * SUBLANES:(t + 1) * SUBLANES])
        return out

    def score_blocks(j, n, mrun):
        j0 = pl.multiple_of(j * TQ, TQ)
        visible = (j0 + lax.broadcasted_iota(jnp.int32, (n * TQ, TQ), 0)) <= q_pos
        out = []
        for h in range(nh):
            s = _dot(k_ref[pl.ds(j0, n * TQ), slot(h)], qt_ref[0, slot(h), :])
            s = jnp.where(visible, s, -jnp.inf)
            s_scr[h, pl.ds(j, n)] = s.reshape(n, TQ, TQ)
            out.append(jnp.maximum(mrun[h], fold(s)))
        return tuple(out)

    mrun = tuple(jnp.full((SUBLANES, TQ), -jnp.inf, F32) for _ in range(nh))
    mrun = lax.fori_loop(0, npair, lambda jj, m: score_blocks(2 * jj, 2, m), mrun)
    mrun = lax.fori_loop(2 * npair, nblk, lambda j, m: score_blocks(j, 1, m), mrun)
    ms = [jnp.max(m, axis=0, keepdims=True) for m in mrun]

    def accumulate(j, n, acc):
        out = []
        for h in range(nh):
            p = jnp.exp2(s_scr[h, pl.ds(j, n)].reshape(n * TQ, TQ) - ms[h]).astype(BF16)
            vt = jnp.concatenate([vt_ref[j + t, slot(h), :] for t in range(n)], axis=1)
            out.append(acc[h] + _dot(vt, p))
        return tuple(out)

    acc = tuple(jnp.zeros((SLOT, TQ), F32) for _ in range(nh))
    acc = lax.fori_loop(0, npair, lambda jj, a: accumulate(2 * jj, 2, a), acc)
    acc = lax.fori_loop(2 * npair, nblk, lambda j, a: accumulate(j, 1, a), acc)
    outs = [a[0:HEAD_LANES] / a[ONES_LANE:ONES_LANE + 1] for a in acc]
    for jj in range(nh // 2):
        both = jnp.concatenate([outs[2 * jj], outs[2 * jj + 1]], axis=0)
        o_ref[:, jj * LANES:(jj + 1) * LANES] = both.T.astype(o_ref.dtype)


def _attention(qt, k, vt, B, T, name):
    N = k.shape[0]
    nq = T // TQ
    nh = ATTN_HEADS_PER_STEP
    heads = k.shape[1] // SLOT
    return pl.pallas_call(
        _attn_kernel,
        out_shape=jax.ShapeDtypeStruct((N, heads * HEAD_LANES), BF16),
        grid=(B, heads // nh, nq),
        in_specs=[pl.BlockSpec((1, nh * SLOT, TQ), lambda b, g, i: (b * nq + i, g, 0)),
                  pl.BlockSpec((T, nh * SLOT), lambda b, g, i: (b, g)),
                  pl.BlockSpec((nq, nh * SLOT, TQ), lambda b, g, i: (b, g, 0))],
        out_specs=pl.BlockSpec((TQ, nh * HEAD_LANES), lambda b, g, i: (b * nq + i, g)),
        scratch_shapes=[pltpu.VMEM((nh, nq, TQ, TQ), F32)],
        compiler_params=pltpu.CompilerParams(dimension_semantics=("arbitrary", "arbitrary", "arbitrary"),
                                             vmem_limit_bytes=VMEM_LIMIT_BYTES),
        name=name,
    )(qt, k, vt)


def _oproj_kernel(alpha, a_ref, m_ref, x_ref, wo_ref, g_ref, b_ref, wr_ref, br_ref,
                  x1_ref, x1t_ref, route_ref, gates_ref, cnt_ref, carry_scr):
    i = pl.program_id(0)

    @pl.when(i == 0)
    def _():
        carry_scr[...] = jnp.zeros_like(carry_scr)

    mix = _dot(jnp.concatenate([a_ref[...], m_ref[...]], axis=1), wo_ref[...])
    x1 = _layer_norm(alpha * x_ref[...] + mix, g_ref[...], b_ref[...])
    x1_ref[...] = x1
    _rows_to_tiles(x1t_ref, x1)

    ne = br_ref.shape[0]
    rows = x1.shape[0]
    x_hi = x1.astype(BF16)
    x_lo = (x1 - x_hi.astype(F32)).astype(BF16)
    both = _dot(x_hi, wr_ref[...])
    logits_rm = both[:, :LANES] + both[:, LANES:] + _dot(x_lo, wr_ref[:, :LANES])
    logits = logits_rm.T[0:ne] + br_ref[...]
    expert = lax.broadcasted_iota(jnp.int32, logits.shape, 0)
    expert_f = expert.astype(F32)
    vals, idxs = [], []
    work = logits
    for _ in range(MOE_TOPK):
        mx = jnp.max(work, axis=0, keepdims=True)
        ix = jnp.min(jnp.where(work == mx, expert_f, float(ne)), axis=0, keepdims=True).astype(jnp.int32)
        vals.append(mx)
        idxs.append(ix)
        work = jnp.where(expert == ix, -jnp.inf, work)
    exps = [jnp.exp(v - vals[0]) for v in vals]
    den = exps[0]
    for e in exps[1:]:
        den = den + e

    onehot = jnp.zeros(logits.shape, F32)
    for ix in idxs:
        onehot = onehot + (expert == ix).astype(F32)
    r = lax.broadcasted_iota(jnp.int32, (rows, rows), 0)
    cidx = lax.broadcasted_iota(jnp.int32, (rows, rows), 1)
    earlier = (r < cidx).astype(BF16)
    carry = carry_scr[:, 0:1]
    before = _dot(onehot.astype(BF16), earlier) + carry
    row8 = lax.broadcasted_iota(jnp.int32, (SUBLANES, rows), 0)
    route = jnp.zeros((SUBLANES, rows), jnp.int32)
    gates = jnp.zeros((SUBLANES, rows), F32)
    for k in range(MOE_TOPK):
        rank = jnp.sum(jnp.where(expert == idxs[k], before, 0.0), axis=0, keepdims=True).astype(jnp.int32)
        route = jnp.where(row8 == k, idxs[k], route)
        route = jnp.where(row8 == MOE_TOPK + k, rank, route)
        gates = jnp.where(row8 == k, exps[k] / den, gates)
    route_ref[...] = route
    gates_ref[...] = jnp.concatenate([gates, jnp.zeros((LANES - SUBLANES, rows), F32)], axis=0).T
    new_carry = carry + jnp.sum(onehot, axis=1, keepdims=True)
    carry_scr[...] = jnp.broadcast_to(new_carry, carry_scr.shape)
    cnt_ref[...] = jnp.broadcast_to(new_carry, cnt_ref.shape)


def _oproj(a, m, x2, wo, g1, b1, wr, br, alpha):
    N, D = x2.shape
    nt = N // OPROJ_TILE
    row = lambda i: (i, 0)
    full = lambda i: (0, 0)
    return pl.pallas_call(
        functools.partial(_oproj_kernel, alpha),
        out_shape=(jax.ShapeDtypeStruct((N, D), F32),
                   jax.ShapeDtypeStruct((N * SUBLANES, LANES), F32),
                   jax.ShapeDtypeStruct((SUBLANES, N), jnp.int32),
                   jax.ShapeDtypeStruct((N, LANES), F32),
                   jax.ShapeDtypeStruct((br.shape[0], LANES), F32)),
        grid=(nt,),
        in_specs=[pl.BlockSpec((OPROJ_TILE, a.shape[1]), row), pl.BlockSpec((OPROJ_TILE, m.shape[1]), row),
                  pl.BlockSpec((OPROJ_TILE, D), row), pl.BlockSpec(wo.shape, full),
                  pl.BlockSpec(g1.shape, full), pl.BlockSpec(b1.shape, full),
                  pl.BlockSpec(wr.shape, full), pl.BlockSpec(br.shape, full)],
        out_specs=(pl.BlockSpec((OPROJ_TILE, D), row), pl.BlockSpec((OPROJ_TILE * SUBLANES, LANES), row),
                   pl.BlockSpec((SUBLANES, OPROJ_TILE), lambda i: (0, i)),
                   pl.BlockSpec((OPROJ_TILE, LANES), row), pl.BlockSpec((br.shape[0], LANES), full)),
        scratch_shapes=[pltpu.VMEM((br.shape[0], LANES), F32)],
        compiler_params=pltpu.CompilerParams(dimension_semantics=("arbitrary",),
                                             vmem_limit_bytes=VMEM_LIMIT_BYTES),
        name="oproj_router",
    )(a, m, x2, wo, g1, b1, wr, br)


def _dispatch_kernel(dest_ref, tail_ref, x1_ref, xs_ref, zero_scr, sem):
    i = pl.program_id(0)

    @pl.when(i == 0)
    def _():
        zero_scr[...] = jnp.zeros_like(zero_scr)

        def tail_copy(e):
            return pltpu.make_async_copy(zero_scr, xs_ref.at[pl.ds(pl.multiple_of(tail_ref[e] * SUBLANES, GROUP_SUB), GROUP_SUB)], sem)

        def start(e, _):
            @pl.when(tail_ref[e] >= 0)
            def _():
                tail_copy(e).start()
            return 0

        def wait(e, _):
            @pl.when(tail_ref[e] >= 0)
            def _():
                tail_copy(e).wait()
            return 0

        lax.fori_loop(0, N_EXPERTS, start, 0)
        lax.fori_loop(0, N_EXPERTS, wait, 0)

        def spare_copy(blk):
            return pltpu.make_async_copy(zero_scr, xs_ref.at[pl.ds(pl.multiple_of(blk * GROUP_SUB, GROUP_SUB), GROUP_SUB)], sem)

        def start_spare(blk, _):
            spare_copy(blk).start()
            return 0

        def wait_spare(blk, _):
            spare_copy(blk).wait()
            return 0

        nblk = xs_ref.shape[0] // GROUP_SUB
        lax.fori_loop(tail_ref[N_EXPERTS], nblk, start_spare, 0)
        lax.fori_loop(tail_ref[N_EXPERTS], nblk, wait_spare, 0)

    base = i * DISPATCH_TILE
    n_tok = pl.num_programs(0) * DISPATCH_TILE

    def row_copy(t, k):
        d = dest_ref[k * n_tok + base + t]
        return pltpu.make_async_copy(x1_ref.at[pl.ds(pl.multiple_of(t * SUBLANES, SUBLANES), SUBLANES)],
                                     xs_ref.at[pl.ds(pl.multiple_of(d * SUBLANES, SUBLANES), SUBLANES)], sem)

    def start_rows(t, _):
        for k in range(MOE_TOPK):
            row_copy(t, k).start(priority=k % DMA_PRIORITIES)
        return 0

    lax.fori_loop(0, DISPATCH_TILE, start_rows, 0, unroll=ISSUE_UNROLL)
    for k in range(MOE_TOPK):
        pltpu.make_async_copy(x1_ref, xs_ref.at[pl.ds(0, DISPATCH_TILE * SUBLANES)], sem).wait()


def _dispatch(dest, tail, x1t, n_rows):
    return pl.pallas_call(
        _dispatch_kernel,
        out_shape=jax.ShapeDtypeStruct((n_rows * SUBLANES, LANES), F32),
        grid_spec=pltpu.PrefetchScalarGridSpec(
            num_scalar_prefetch=2,
            grid=(x1t.shape[0] // (DISPATCH_TILE * SUBLANES),),
            in_specs=[pl.BlockSpec((DISPATCH_TILE * SUBLANES, LANES), lambda i, d, t: (i, 0))],
            out_specs=pl.BlockSpec(memory_space=pl.ANY),
            scratch_shapes=[pltpu.VMEM((GROUP_SUB, LANES), F32), pltpu.SemaphoreType.DMA(())],
        ),
        compiler_params=pltpu.CompilerParams(dimension_semantics=("arbitrary",),
                                             vmem_limit_bytes=VMEM_LIMIT_BYTES),
        name="dispatch",
    )(dest, tail, x1t)


def _experts_kernel(be_ref, slot_ref, nxt_ref, full_ref, nused_ref, x_ref, wg_hbm, bg_ref, wu_hbm, bu_ref, wd_hbm, bd_ref,
                    y_ref, wg_st, wu_st, wd_st, wg_bf, wu_bf, wd_bf, sems):
    i = pl.program_id(0)
    prev = be_ref[jnp.maximum(i - 1, 0)]
    changed = (i == 0) | (be_ref[i] != prev)
    active = i < nused_ref[0]
    slot = slot_ref[i]

    def weight_copies(expert, s):
        return (pltpu.make_async_copy(wg_hbm.at[expert], wg_st.at[s], sems.at[s, 0]),
                pltpu.make_async_copy(wu_hbm.at[expert], wu_st.at[s], sems.at[s, 1]),
                pltpu.make_async_copy(wd_hbm.at[expert], wd_st.at[s], sems.at[s, 2]))

    @pl.when(i == 0)
    def _():
        for cp in weight_copies(be_ref[0], slot):
            cp.start()

    @pl.when(active & changed)
    def _():
        for cp in weight_copies(be_ref[i], slot):
            cp.wait()

        @pl.when(nxt_ref[i] >= 0)
        def _():
            for cp in weight_copies(nxt_ref[i], 1 - slot):
                cp.start()

        wg_bf[...] = wg_st[slot].astype(BF16)
        wu_bf[...] = wu_st[slot].astype(BF16)
        wd_bf[...] = wd_st[slot].astype(BF16)

    def mlp(nrows):
        rows = pl.ds(0, nrows * SUBLANES)
        xb = _tiles_to_rows(x_ref.at[rows], nrows).astype(BF16)
        g = jnp.minimum(_dot(xb, wg_bf[...]) + bg_ref[0], SWIGLU_LIMIT)
        u = jnp.clip(_dot(xb, wu_bf[...]) + bu_ref[0], -SWIGLU_LIMIT, SWIGLU_LIMIT)
        h = g * (1.0 / (1.0 + jnp.exp(-SWIGLU_ALPHA * g))) * (u + 1.0)
        _rows_to_tiles(y_ref.at[rows], _dot(h.astype(BF16), wd_bf[...]) + bd_ref[0])

    whole = active & (full_ref[i] > 0)

    @pl.when(whole)
    def _():
        mlp(MOE_GROUP)

    @pl.when(active & jnp.logical_not(whole))
    def _():
        mlp(MOE_HALF)

    @pl.when(jnp.logical_not(active))
    def _():
        y_ref[pl.ds(0, HALF_SUB), :] = jnp.zeros((HALF_SUB, LANES), F32)

    @pl.when(jnp.logical_not(whole))
    def _():
        y_ref[pl.ds(HALF_SUB, HALF_SUB), :] = jnp.zeros((HALF_SUB, LANES), F32)


def _experts(blk_expert, blk_slot, blk_next, blk_full, n_used, xs, wg, bg, wu, bu, wd, bd):
    E, D, F = wg.shape
    nblk = xs.shape[0] // GROUP_SUB

    def rowmap(i, be, sl, nx, fl, nu):
        return (jnp.minimum(i, nu[0] - 1), 0)

    def bmap(i, be, sl, nx, fl, nu):
        return (be[i], 0, 0)

    hbm = pl.BlockSpec(memory_space=pl.ANY)
    return pl.pallas_call(
        _experts_kernel,
        out_shape=jax.ShapeDtypeStruct(xs.shape, F32),
        grid_spec=pltpu.PrefetchScalarGridSpec(
            num_scalar_prefetch=5,
            grid=(nblk,),
            in_specs=[pl.BlockSpec((GROUP_SUB, LANES), rowmap),
                      hbm, pl.BlockSpec((1, 1, F), bmap),
                      hbm, pl.BlockSpec((1, 1, F), bmap),
                      hbm, pl.BlockSpec((1, 1, D), bmap)],
            out_specs=pl.BlockSpec((GROUP_SUB, LANES), lambda i, be, sl, nx, fl, nu: (i, 0)),
            scratch_shapes=[pltpu.VMEM((2, D, F), F32), pltpu.VMEM((2, D, F), F32), pltpu.VMEM((2, F, D), F32),
                            pltpu.VMEM((D, F), BF16), pltpu.VMEM((D, F), BF16), pltpu.VMEM((F, D), BF16),
                            pltpu.SemaphoreType.DMA((2, 3))],
        ),
        compiler_params=pltpu.CompilerParams(dimension_semantics=("arbitrary",),
                                             vmem_limit_bytes=VMEM_LIMIT_BYTES),
        name="experts",
    )(blk_expert, blk_slot, blk_next, blk_full, n_used, xs, wg, bg, wu, bu, wd, bd)


def _combine_kernel(alpha, dest_ref, gates_ref, x1_ref, g_ref, b_ref, y_ref, o_ref, ybuf, sems):
    i = pl.program_id(0)
    buf = i % 2

    n_tok = pl.num_programs(0) * COMBINE_TILE

    def gather_tile(tile, into):
        base = tile * COMBINE_TILE

        def start_rows(t, _):
            for k in range(MOE_TOPK):
                d = dest_ref[k * n_tok + base + t]
                pltpu.make_async_copy(y_ref.at[pl.ds(pl.multiple_of(d * SUBLANES, SUBLANES), SUBLANES)],
                                      ybuf.at[into, k, pl.ds(pl.multiple_of(t * SUBLANES, SUBLANES), SUBLANES)],
                                      sems.at[into]).start(priority=k % DMA_PRIORITIES)
            return 0

        lax.fori_loop(0, COMBINE_TILE, start_rows, 0, unroll=ISSUE_UNROLL)

    @pl.when(i == 0)
    def _():
        gather_tile(0, 0)

    @pl.when(i + 1 < pl.num_programs(0))
    def _():
        gather_tile(i + 1, 1 - buf)

    for k in range(MOE_TOPK):
        pltpu.make_async_copy(y_ref.at[pl.ds(0, COMBINE_TILE * SUBLANES)], ybuf.at[buf, k], sems.at[buf]).wait()

    gates = gates_ref[...]
    ffn = gates[:, 0:1] * _tiles_to_rows(ybuf.at[buf, 0], COMBINE_TILE)
    for k in range(1, MOE_TOPK):
        ffn = ffn + gates[:, k:k + 1] * _tiles_to_rows(ybuf.at[buf, k], COMBINE_TILE)
    o_ref[...] = _layer_norm(alpha * x1_ref[...] + ffn, g_ref[...], b_ref[...])


def _combine(dest, gates, x1, g2, b2, y_rows, alpha):
    N, D = x1.shape
    row = lambda i, d: (i, 0)
    full = lambda i, d: (0, 0)
    return pl.pallas_call(
        functools.partial(_combine_kernel, alpha),
        out_shape=jax.ShapeDtypeStruct((N, D), F32),
        grid_spec=pltpu.PrefetchScalarGridSpec(
            num_scalar_prefetch=1,
            grid=(N // COMBINE_TILE,),
            in_specs=[pl.BlockSpec((COMBINE_TILE, LANES), row), pl.BlockSpec((COMBINE_TILE, D), row),
                      pl.BlockSpec(g2.shape, full), pl.BlockSpec(b2.shape, full),
                      pl.BlockSpec(memory_space=pl.ANY)],
            out_specs=pl.BlockSpec((COMBINE_TILE, D), row),
            scratch_shapes=[pltpu.VMEM((2, MOE_TOPK, COMBINE_TILE * SUBLANES, LANES), F32),
                            pltpu.SemaphoreType.DMA((2,))],
        ),
        compiler_params=pltpu.CompilerParams(dimension_semantics=("arbitrary",),
                                             vmem_limit_bytes=VMEM_LIMIT_BYTES),
        name="combine",
    )(dest, gates, x1, g2, b2, y_rows)


def _rot_partner(w, half):
    return jnp.concatenate([-w[..., half:2 * half], w[..., :half]], axis=-1)


def _layer_weights(w_in, w_q_b, w_kv_b):
    D = w_in.shape[0]
    o1 = Q_LORA
    o2 = o1 + KV_LORA
    o3 = o2 + MLA_ROPE
    w_ql, w_kvl, w_kr = w_in[:, :o1], w_in[:, o1:o2], w_in[:, o2:o3]
    w_mq, w_mk, w_mv = w_in[:, o3:o3 + _MH], w_in[:, o3 + _MH:o3 + 2 * _MH], w_in[:, o3 + 2 * _MH:]
    zpad = lambda n: jnp.zeros((D, n), w_in.dtype)
    tail = SLOT - MLA_NOPE - MLA_ROPE
    kr_slot = jnp.concatenate([zpad(MLA_NOPE), w_kr, zpad(tail)], axis=1)
    krs_slot = jnp.concatenate([zpad(MLA_NOPE), _rot_partner(w_kr, MLA_ROPE // 2), zpad(tail)], axis=1)

    def moba_partner(w):
        w3 = w.reshape(D, MOBA_HEADS, MOBA_HD)
        part = jnp.concatenate([_rot_partner(w3[..., :MOBA_ROT], MOBA_ROT // 2),
                                jnp.zeros((D, MOBA_HEADS, MOBA_HD - MOBA_ROT), w.dtype)], axis=-1)
        return part.reshape(D, _MH)

    win = jnp.concatenate([w_ql, w_kvl, kr_slot, krs_slot, w_mk, moba_partner(w_mk)], axis=1).astype(BF16)
    mq_partner = _rot_partner(w_mq.reshape(D, MOBA_HEADS, MOBA_HD)[..., :MOBA_ROT], MOBA_ROT // 2)
    wmt = jnp.concatenate([w_mq, mq_partner.reshape(D, MOBA_HEADS * MOBA_ROT), w_mv], axis=1).T.astype(BF16)

    wq3 = w_q_b.reshape(Q_LORA, MLA_HEADS, MLA_NOPE + MLA_ROPE)
    wqs = _rot_partner(wq3[..., MLA_NOPE:], MLA_ROPE // 2).reshape(Q_LORA, MLA_HEADS * MLA_ROPE)
    wkv3 = w_kv_b.reshape(KV_LORA, MLA_HEADS, MLA_NOPE + MLA_V)
    wk = jnp.concatenate([wkv3[..., :MLA_NOPE], jnp.zeros((KV_LORA, MLA_HEADS, SLOT - MLA_NOPE), w_kv_b.dtype)],
                         axis=-1).reshape(KV_LORA, MLA_HEADS * SLOT).astype(BF16)
    wv = wkv3[..., MLA_NOPE:].reshape(KV_LORA, MLA_HEADS * MLA_V)
    return win, wmt, w_q_b.T.astype(BF16), wqs.T.astype(BF16), wk, wv.T.astype(BF16)


def _rope_tables(positions):
    pos = positions.astype(F32).reshape(1, -1)

    def cs(d_rot):
        inv_freq = ROPE_THETA ** (-jnp.arange(0, d_rot, 2, dtype=F32) / d_rot)
        ang = jnp.concatenate([inv_freq, inv_freq]).reshape(d_rot, 1) * pos
        return jnp.cos(ang), jnp.sin(ang)

    return cs(MLA_ROPE) + cs(MOBA_ROT)


def kernel(x, positions, w_in, q_a_norm, w_q_b, kv_a_norm, w_kv_b, w_o, ln1_g, ln1_b, w_router, b_router,
           w_gate, b_gate, w_up, b_up, w_down, b_down, ln2_g, ln2_b):
    B, T, D = x.shape
    depth = w_in.shape[0]
    alpha = (2.0 * depth) ** 0.25
    N = B * T
    assert T % MOBA_BLOCK == 0 and T // MOBA_BLOCK <= SUBLANES and N % OPROJ_TILE == 0
    assert D == SUBLANES * LANES
    n_asg = N * MOE_TOPK
    n_rows = n_asg + N_EXPERTS * MOE_GROUP
    tables = _rope_tables(positions)
    h = x.reshape(N, D)
    for l in range(depth):
        win, wmt, wqt, wqst, wk, wvt = _layer_weights(w_in[l], w_q_b[l], w_kv_b[l])
        qa, ka, va, mq, mk, mv = _prep(h, win, wmt, wqt, wqst, wk, wvt, q_a_norm[l].reshape(1, -1),
                                       kv_a_norm[l].reshape(1, -1), tables, B, T)
        a = _attention(qa, ka, va, B, T, "mla_attention")
        m = _attention(mq, mk, mv, B, T, "moba_attention")
        wr_pad = jnp.concatenate([w_router[l], jnp.zeros((D, LANES - N_EXPERTS), F32)], axis=1)
        wr_hi = wr_pad.astype(BF16)
        wr = jnp.concatenate([wr_hi, (wr_pad - wr_hi.astype(F32)).astype(BF16)], axis=1)
        br = b_router[l].reshape(N_EXPERTS, 1)
        x1, x1t, route, gates, cnt = _oproj(a, m, h, w_o[l].astype(BF16), ln1_g[l].reshape(1, D), ln1_b[l].reshape(1, D),
                                       wr, br, alpha)
        er = jnp.arange(N_EXPERTS, dtype=jnp.int32)
        counts = cnt[:, 0].astype(jnp.int32)
        padded = (counts + MOE_GROUP - 1) // MOE_GROUP * MOE_GROUP
        upto = er[None, :] <= er[:, None]
        pad_end = jnp.sum(jnp.where(upto, padded[None, :], 0), axis=1)
        pad_start = pad_end - padded
        e_idx = route[:MOE_TOPK]
        group_start = jnp.sum(jnp.where(e_idx[..., None] == er, pad_start, 0), axis=-1)
        dest = (group_start + route[MOE_TOPK:2 * MOE_TOPK]).reshape(n_asg)
        n_used = (pad_end[-1:] // MOE_GROUP).astype(jnp.int32)
        tail = jnp.concatenate([jnp.where(padded > 0, pad_end - MOE_GROUP, -1).astype(jnp.int32), n_used])
        nblk = n_rows // MOE_GROUP
        blk_start = jnp.arange(nblk, dtype=jnp.int32) * MOE_GROUP
        blk_expert = jnp.minimum(jnp.sum((blk_start[:, None] >= pad_end[None, :]).astype(jnp.int32), axis=1),
                                 N_EXPERTS - 1)
        of_expert = blk_expert[:, None] == er[None, :]
        pick = lambda per_expert: jnp.sum(jnp.where(of_expert, per_expert[None, :], 0), axis=1)
        nonempty = padded > 0
        group_index = jnp.sum(jnp.where(upto & nonempty[None, :], 1, 0), axis=1) - 1
        later = (er[None, :] > er[:, None]) & nonempty[None, :]
        next_expert = jnp.min(jnp.where(later, er[None, :], N_EXPERTS), axis=1)
        next_expert = jnp.where(next_expert == N_EXPERTS, -1, next_expert)
        blk_slot = pick(group_index) % 2
        blk_next = pick(next_expert)
        blk_full = (pick(pad_start + counts) > blk_start + MOE_HALF).astype(jnp.int32)
        xs = _dispatch(dest, tail, x1t, n_rows)
        y_rows = _experts(blk_expert, blk_slot.astype(jnp.int32), blk_next, blk_full, n_used, xs, w_gate[l], b_gate[l].reshape(N_EXPERTS, 1, -1),
                          w_up[l], b_up[l].reshape(N_EXPERTS, 1, -1), w_down[l], b_down[l].reshape(N_EXPERTS, 1, -1))
        h = _combine(dest, gates, x1, ln2_g[l].reshape(1, D), ln2_b[l].reshape(1, D), y_rows, alpha)
    return h.reshape(B, T, D)
```

```python
import functools
import math

import jax
import jax.numpy as jnp
from jax import lax
from jax.experimental import pallas as pl
from jax.experimental.pallas import tpu as pltpu

ROPE_THETA = 500000.0
MLA_HEADS = 8
MLA_NOPE = 64
MLA_ROPE = 32
MLA_V = 64
Q_LORA = 256
KV_LORA = 128
MOBA_HEADS = 8
MOBA_HD = 64
MOBA_ROT = MOBA_HD // 4
MOBA_BLOCK = 256
MOBA_TOPK = 3
N_EXPERTS = 32
MOE_TOPK = 4
SWIGLU_LIMIT = 7.0
SWIGLU_ALPHA = 1.702
RMS_EPS = 1e-6
LN_EPS = 1e-5

LANES = 128
SUBLANES = 8
VMEM_LIMIT_BYTES = 56 * 1024 * 1024

SLOT = LANES
TQ = MOBA_BLOCK
ROW_TILE = 256
OPROJ_TILE = 512
MOE_GROUP = 512
MOE_HALF = MOE_GROUP // 2
GROUP_SUB = MOE_GROUP * SUBLANES
HALF_SUB = MOE_HALF * SUBLANES
DISPATCH_TILE = 512
COMBINE_TILE = 512
ISSUE_UNROLL = 4
DMA_PRIORITIES = 2
HEAD_LANES = 64
ONES_LANE = 64
BIAS_LANE = 64
ATTN_HEADS_PER_STEP = 8
LOG2E = math.log2(math.e)
NEG_BIG = -(2.0 ** 100)

F32 = jnp.float32
BF16 = jnp.bfloat16
NT_DIMS = (((1,), (1,)), ((), ()))


def _dot(a, b, precision=None):
    return jnp.dot(a, b, preferred_element_type=F32, precision=precision)


def _dot_nt(a, b, precision=None):
    return lax.dot_general(a, b, NT_DIMS, preferred_element_type=F32, precision=precision)


def _rows_to_tiles(ref, x):
    rows = x.shape[0]
    for c in range(SUBLANES):
        ref[pl.ds(c, rows, stride=SUBLANES), :] = x[:, c * LANES:(c + 1) * LANES]


def _tiles_to_rows(ref, rows):
    return jnp.concatenate([ref[pl.ds(c, rows, stride=SUBLANES), :] for c in range(SUBLANES)], axis=1)


def _rms(x, g):
    return x * lax.rsqrt(jnp.mean(x * x, axis=-1, keepdims=True) + RMS_EPS) * g


def _layer_norm(x, g, b):
    mu = jnp.mean(x, axis=-1, keepdims=True)
    xc = x - mu
    var = jnp.mean(xc * xc, axis=-1, keepdims=True)
    return xc * lax.rsqrt(var + LN_EPS) * g + b


_C_QL = 0
_C_KVL = _C_QL + Q_LORA
_C_KR = _C_KVL + KV_LORA
_C_KRS = _C_KR + SLOT
_C_MK = _C_KRS + SLOT
_MH = MOBA_HEADS * MOBA_HD
_C_MKS = _C_MK + _MH
_C_END = _C_MKS + _MH


def _prep_kernel(x_ref, win_ref, wmt_ref, wqt_ref, wqst_ref, wk_ref, wvt_ref, qg_ref, kvg_ref,
                 cat_ref, sat_ref, cbt_ref, sbt_ref,
                 qat_ref, ka_ref, vat_ref, mqt_ref, mk_ref, mvt_ref, kmean_scr):
    c = pl.program_id(1)

    @pl.when(c == 0)
    def _():
        kmean_scr[...] = jnp.zeros_like(kmean_scr)

    xb = x_ref[...].astype(BF16)
    cat = cat_ref[...]
    sat = sat_ref[...]
    cbt = cbt_ref[...]
    sbt = sbt_ref[...]
    const = lambda v, n: jnp.full((n, ROW_TILE), v, F32)
    tail = SLOT - MLA_NOPE - MLA_ROPE
    rest = MOBA_HD - MOBA_ROT
    ca = jnp.concatenate([const(1.0, MLA_NOPE), cat, const(0.0, tail)], axis=0).T
    sa = jnp.concatenate([const(0.0, MLA_NOPE), sat, const(0.0, tail)], axis=0).T
    cb = jnp.concatenate([cbt, const(1.0, rest), cbt, const(1.0, rest)], axis=0).T
    sb = jnp.concatenate([sbt, const(0.0, rest), sbt, const(0.0, rest)], axis=0).T
    lane =lax.broadcasted_iota(jnp.int32, (ROW_TILE, SLOT), 1)
    head_lanes = lane < HEAD_LANES
    ones_rows = (lax.broadcasted_iota(jnp.int32, (SLOT - HEAD_LANES, ROW_TILE), 0) == ONES_LANE - HEAD_LANES).astype(F32)

    ql = _dot(xb, win_ref[:, _C_QL:_C_KVL])
    kvl = _dot(xb, win_ref[:, _C_KVL:_C_KR])
    kr = _dot(xb, win_ref[:, _C_KR:_C_KRS])
    krs = _dot(xb, win_ref[:, _C_KRS:_C_MK])
    qn = _rms(ql, qg_ref[...]).astype(BF16)
    kvn = _rms(kvl, kvg_ref[...]).astype(BF16)
    q_t = _dot_nt(wqt_ref[...], qn)
    qs_t = _dot_nt(wqst_ref[...], qn)
    kn = _dot(kvn, wk_ref[...])
    v_t = _dot_nt(wvt_ref[...], kvn)
    scale_a = LOG2E / math.sqrt(MLA_NOPE + MLA_ROPE)
    kro = kr * ca + krs * sa
    qd = MLA_NOPE + MLA_ROPE
    q_pad = jnp.zeros((SLOT - qd, ROW_TILE), F32)
    for h in range(MLA_HEADS):
        sl = slice(h * SLOT, (h + 1) * SLOT)
        q_rope = q_t[h * qd + MLA_NOPE:(h + 1) * qd] * cat + qs_t[h * MLA_ROPE:(h + 1) * MLA_ROPE] * sat
        q_slot = jnp.concatenate([q_t[h * qd:h * qd + MLA_NOPE], q_rope, q_pad], axis=0)
        qat_ref[0, sl, :] = (q_slot * scale_a).astype(BF16)
        ka_ref[:, sl] = (kn[:, sl] + kro).astype(BF16)
        vat_ref[0, sl, :] = jnp.concatenate([v_t[h * MLA_V:(h + 1) * MLA_V], ones_rows], axis=0).astype(BF16)

    mk = _dot(xb, win_ref[:, _C_MK:_C_MKS])
    mks = _dot(xb, win_ref[:, _C_MKS:_C_END])
    npair = _MH // LANES
    pair = lambda a, j: a[:, j * LANES:(j + 1) * LANES]
    mk_rot = [pair(mk, j) * cb + pair(mks, j) * sb for j in range(npair)]
    mk_all = jnp.concatenate(mk_rot, axis=1)

    nrot = MOBA_HEADS * MOBA_ROT
    mq_t = _dot_nt(wmt_ref[0:_MH, :], xb)
    mqs_t = _dot_nt(wmt_ref[_MH:_MH + nrot, :], xb)
    mv_t = _dot_nt(wmt_ref[_MH + nrot:2 * _MH + nrot, :], xb)
    head = lambda a, h: a[h * MOBA_HD:(h + 1) * MOBA_HD]
    mq_rot_t = [jnp.concatenate([head(mq_t, h)[0:MOBA_ROT] * cbt + mqs_t[h * MOBA_ROT:(h + 1) * MOBA_ROT] * sbt,
                                 head(mq_t, h)[MOBA_ROT:]], axis=0) for h in range(MOBA_HEADS)]

    nrow = MOBA_HEADS * SUBLANES
    row_i = lax.broadcasted_iota(jnp.int32, (nrow, _MH), 0)
    lane_i = lax.broadcasted_iota(jnp.int32, (nrow, _MH), 1)
    kmean_c = jnp.mean(mk_all, axis=0, keepdims=True)
    put = ((row_i % SUBLANES) == c) & ((lane_i // MOBA_HD) == (row_i // SUBLANES))
    table = kmean_scr[...]
    gate_t = _dot(table, jnp.concatenate(mq_rot_t, axis=0), precision=lax.Precision.HIGHEST)
    kmean_scr[...] = jnp.where(put, jnp.broadcast_to(kmean_c, (nrow, _MH)), table)

    n_idx = lax.broadcasted_iota(jnp.int32, (SUBLANES, ROW_TILE), 0)
    valid = n_idx < c
    scale_b = LOG2E / math.sqrt(MOBA_HD)
    zero_rows = jnp.zeros((SLOT - HEAD_LANES - SUBLANES, ROW_TILE), F32)
    block_onehot = (lane == BIAS_LANE + c).astype(F32)
    for h in range(MOBA_HEADS):
        g = jnp.where(valid, gate_t[h * SUBLANES:(h + 1) * SUBLANES, :], -jnp.inf)
        rank = jnp.zeros((SUBLANES, ROW_TILE), jnp.int32)
        for k in range(1, SUBLANES):
            other = pltpu.roll(g, k, axis=0)
            other_n = pltpu.roll(n_idx, k, axis=0)
            beats = (other > g) | ((other == g) & (other_n < n_idx))
            rank = rank + beats.astype(jnp.int32)
        keep = (valid & (rank < MOBA_TOPK)) | (n_idx == c)
        bias = jnp.where(keep, 0.0, NEG_BIG)
        sl = slice(h * SLOT, (h + 1) * SLOT)
        mqt_ref[0, sl, :] = jnp.concatenate([mq_rot_t[h] * scale_b, bias, zero_rows], axis=0).astype(BF16)
        mvt_ref[0, sl, :] = jnp.concatenate([head(mv_t, h), ones_rows], axis=0).astype(BF16)
        j, hh = divmod(h, 2)
        k_h = mk_rot[j] if hh == 0 else pltpu.roll(mk_rot[j], HEAD_LANES, axis=1)
        mk_ref[:, sl] = jnp.where(head_lanes, k_h, block_onehot).astype(BF16)


def _prep(x2, win, wmt, wqt, wqst, wk, wvt, qg, kvg, tables, B, T):
    N, D = x2.shape
    nt = T // ROW_TILE
    row = lambda b, c: (b * nt + c, 0)
    col = lambda b, c: (0, b * nt + c)
    full = lambda b, c: (0, 0)
    cat, sat, cbt, sbt = tables
    width = MLA_HEADS * SLOT

    def rows(w):
        return pl.BlockSpec((ROW_TILE, w), row)

    def cols(a):
        return pl.BlockSpec((a.shape[0], ROW_TILE), col)

    def whole(a):
        return pl.BlockSpec(a.shape, full)

    rowmajor = jax.ShapeDtypeStruct((N, width), BF16)
    transposed = jax.ShapeDtypeStruct((B * nt, width, ROW_TILE), BF16)
    t_spec = pl.BlockSpec((1, width, ROW_TILE), lambda b, c: (b * nt + c, 0, 0))
    return pl.pallas_call(
        _prep_kernel,
        out_shape=(transposed, rowmajor, transposed, transposed, rowmajor, transposed),
        grid=(B, nt),
        in_specs=[rows(D), whole(win), whole(wmt), whole(wqt), whole(wqst), whole(wk), whole(wvt), whole(qg), whole(kvg),
                  cols(cat), cols(sat), cols(cbt), cols(sbt)],
        out_specs=(t_spec, rows(width), t_spec, t_spec, rows(width), t_spec),
        scratch_shapes=[pltpu.VMEM((MOBA_HEADS * SUBLANES, _MH), F32)],
        compiler_params=pltpu.CompilerParams(dimension_semantics=("arbitrary", "arbitrary"),
                                             vmem_limit_bytes=VMEM_LIMIT_BYTES),
        name="prep",
    )(x2, win, wmt, wqt, wqst, wk, wvt, qg, kvg, cat, sat, cbt, sbt)


def _attn_kernel(qt_ref, k_ref, vt_ref, o_ref, s_scr):
    i = pl.program_id(2)
    nh = ATTN_HEADS_PER_STEP
    slot = lambda h: slice(h * SLOT, (h + 1) * SLOT)
    nblk = i + 1
    nquad = nblk // 4
    pair_end = 4 * nquad + 2 * ((nblk - 4 * nquad) // 2)
    q_pos = i * TQ + lax.broadcasted_iota(jnp.int32, (1, TQ), 1)

    def fold(s):
        out = s[0:SUBLANES]
        for t in range(1, s.shape[0] // SUBLANES):
            out = jnp.maximum(out, s[t * SUBLANES:(t + 1) * SUBLANES])
        return out

    def score_blocks(j, n, mrun):
        j0 = pl.multiple_of(j * TQ, TQ)
        visible = (j0 + lax.broadcasted_iota(jnp.int32, (n * TQ, TQ), 0)) <= q_pos
        out = []
        for h in range(nh):
            s = _dot(k_ref[pl.ds(j0, n * TQ), slot(h)], qt_ref[0, slot(h), :])
            s = jnp.where(visible, s, -jnp.inf)
            s_scr[h, pl.ds(j, n)] = s.reshape(n, TQ, TQ)
            out.append(jnp.maximum(mrun[h], fold(s)))
        return tuple(out)

    mrun = tuple(jnp.full((SUBLANES, TQ), -jnp.inf, F32) for _ in range(nh))
    mrun = lax.fori_loop(0, nquad, lambda jq, m: score_blocks(4 * jq, 4, m), mrun)
    mrun = lax.fori_loop(2 * nquad, pair_end // 2, lambda jj, m: score_blocks(2 * jj, 2, m), mrun)
    mrun = lax.fori_loop(pair_end, nblk, lambda j, m: score_blocks(j, 1, m), mrun)
    ms = [jnp.max(m, axis=0, keepdims=True) for m in mrun]

    def accumulate(j, n, acc):
        out = []
        for h in range(nh):
            p = jnp.exp2(s_scr[h, pl.ds(j, n)].reshape(n * TQ, TQ) - ms[h]).astype(BF16)
            vt = jnp.concatenate([vt_ref[j + t, slot(h), :] for t in range(n)], axis=1)
            out.append(acc[h] + _dot(vt, p))
        return tuple(out)

    acc = tuple(jnp.zeros((SLOT, TQ), F32) for _ in range(nh))
    acc = lax.fori_loop(0, nquad, lambda jq, a: accumulate(4 * jq, 4, a), acc)
    acc = lax.fori_loop(2 * nquad, pair_end // 2, lambda jj, a: accumulate(2 * jj, 2, a), acc)
    acc = lax.fori_loop(pair_end, nblk, lambda j, a: accumulate(j, 1, a), acc)
    outs = [a[0:HEAD_LANES] / a[ONES_LANE:ONES_LANE + 1] for a in acc]
    for jj in range(nh // 2):
        both = jnp.concatenate([outs[2 * jj], outs[2 * jj + 1]], axis=0)
        o_ref[:, jj * LANES:(jj + 1) * LANES] = both.T.astype(o_ref.dtype)


def _attention(qt, k, vt, B, T, name):
    N = k.shape[0]
    nq = T // TQ
    nh = ATTN_HEADS_PER_STEP
    heads = k.shape[1] // SLOT
    return pl.pallas_call(
        _attn_kernel,
        out_shape=jax.ShapeDtypeStruct((N, heads * HEAD_LANES), BF16),
        grid=(B, heads // nh, nq),
        in_specs=[pl.BlockSpec((1, nh * SLOT, TQ), lambda b, g, i: (b * nq + i, g, 0)),
                  pl.BlockSpec((T, nh * SLOT), lambda b, g, i: (b, g)),
                  pl.BlockSpec((nq, nh * SLOT, TQ), lambda b, g, i: (b, g, 0))],
        out_specs=pl.BlockSpec((TQ, nh * HEAD_LANES), lambda b, g, i: (b * nq + i, g)),
        scratch_shapes=[pltpu.VMEM((nh, nq, TQ, TQ), F32)],
        compiler_params=pltpu.CompilerParams(dimension_semantics=("arbitrary", "arbitrary", "arbitrary"),
                                             vmem_limit_bytes=VMEM_LIMIT_BYTES),
        name=name,
    )(qt, k, vt)


def _oproj_kernel(alpha, a_ref, m_ref, x_ref, wo_ref, g_ref, b_ref, wr_ref, br_ref,
                  x1_ref, x1t_ref, route_ref, gates_ref, cnt_ref, carry_scr):
    i = pl.program_id(0)

    @pl.when(i == 0)
    def _():
        carry_scr[...] = jnp.zeros_like(carry_scr)

    mix = _dot(jnp.concatenate([a_ref[...], m_ref[...]], axis=1), wo_ref[...])
    x1 = _layer_norm(alpha * x_ref[...] + mix, g_ref[...], b_ref[...])
    x1_ref[...] = x1
    _rows_to_tiles(x1t_ref, x1)

    ne = br_ref.shape[0]
    rows = x1.shape[0]
    x_hi = x1.astype(BF16)
    x_lo = (x1 - x_hi.astype(F32)).astype(BF16)
    both = _dot(x_hi, wr_ref[...])
    logits_rm = both[:, :LANES] + both[:, LANES:] + _dot(x_lo, wr_ref[:, :LANES])
    logits = logits_rm.T[0:ne] + br_ref[...]
    expert = lax.broadcasted_iota(jnp.int32, logits.shape, 0)
    expert_f = expert.astype(F32)
    vals, idxs = [], []
    work = logits
    for _ in range(MOE_TOPK):
        mx = jnp.max(work, axis=0, keepdims=True)
        ix = jnp.min(jnp.where(work == mx, expert_f, float(ne)), axis=0, keepdims=True).astype(jnp.int32)
        vals.append(mx)
        idxs.append(ix)
        work = jnp.where(expert == ix, -jnp.inf, work)
    exps = [jnp.exp(v - vals[0]) for v in vals]
    den = exps[0]
    for e in exps[1:]:
        den = den + e

    onehot = jnp.zeros(logits.shape, F32)
    for ix in idxs:
        onehot = onehot + (expert == ix).astype(F32)
    r = lax.broadcasted_iota(jnp.int32, (rows, rows), 0)
    cidx = lax.broadcasted_iota(jnp.int32, (rows, rows), 1)
    earlier = (r < cidx).astype(BF16)
    carry = carry_scr[:, 0:1]
    before = _dot(onehot.astype(BF16), earlier) + carry
    row8 = lax.broadcasted_iota(jnp.int32, (SUBLANES, rows), 0)
    route = jnp.zeros((SUBLANES, rows), jnp.int32)
    gates = jnp.zeros((SUBLANES, rows), F32)
    for k in range(MOE_TOPK):
        rank = jnp.sum(jnp.where(expert == idxs[k], before, 0.0), axis=0, keepdims=True).astype(jnp.int32)
        route = jnp.where(row8 == k, idxs[k], route)
        route = jnp.where(row8 == MOE_TOPK + k, rank, route)
        gates = jnp.where(row8 == k, exps[k] / den, gates)
    route_ref[...] = route
    gates_ref[...] = jnp.concatenate([gates, jnp.zeros((LANES - SUBLANES, rows), F32)], axis=0).T
    new_carry = carry + jnp.sum(onehot, axis=1, keepdims=True)
    carry_scr[...] = jnp.broadcast_to(new_carry, carry_scr.shape)
    cnt_ref[...] = jnp.broadcast_to(new_carry, cnt_ref.shape)


def _oproj(a, m, x2, wo, g1, b1, wr, br, alpha):
    N, D = x2.shape
    nt = N // OPROJ_TILE
    row = lambda i: (i, 0)
    full = lambda i: (0, 0)
    return pl.pallas_call(
        functools.partial(_oproj_kernel, alpha),
        out_shape=(jax.ShapeDtypeStruct((N, D), F32),
                   jax.ShapeDtypeStruct((N * SUBLANES, LANES), F32),
                   jax.ShapeDtypeStruct((SUBLANES, N), jnp.int32),
                   jax.ShapeDtypeStruct((N, LANES), F32),
                   jax.ShapeDtypeStruct((br.shape[0], LANES), F32)),
        grid=(nt,),
        in_specs=[pl.BlockSpec((OPROJ_TILE, a.shape[1]), row), pl.BlockSpec((OPROJ_TILE, m.shape[1]), row),
                  pl.BlockSpec((OPROJ_TILE, D), row), pl.BlockSpec(wo.shape, full),
                  pl.BlockSpec(g1.shape, full), pl.BlockSpec(b1.shape, full),
                  pl.BlockSpec(wr.shape, full), pl.BlockSpec(br.shape, full)],
        out_specs=(pl.BlockSpec((OPROJ_TILE, D), row), pl.BlockSpec((OPROJ_TILE * SUBLANES, LANES), row),
                   pl.BlockSpec((SUBLANES, OPROJ_TILE), lambda i: (0, i)),
                   pl.BlockSpec((OPROJ_TILE, LANES), row), pl.BlockSpec((br.shape[0], LANES), full)),
        scratch_shapes=[pltpu.VMEM((br.shape[0], LANES), F32)],
        compiler_params=pltpu.CompilerParams(dimension_semantics=("arbitrary",),
                                             vmem_limit_bytes=VMEM_LIMIT_BYTES),
        name="oproj_router",
    )(a, m, x2, wo, g1, b1, wr, br)


def _dispatch_kernel(dest_ref, tail_ref, x1_ref, xs_ref, zero_scr, sem):
    i = pl.program_id(0)

    @pl.when(i == 0)
    def _():
        zero_scr[...] = jnp.zeros_like(zero_scr)

        def tail_copy(e):
            return pltpu.make_async_copy(zero_scr, xs_ref.at[pl.ds(pl.multiple_of(tail_ref[e] * SUBLANES, GROUP_SUB), GROUP_SUB)], sem)

        def start(e, _):
            @pl.when(tail_ref[e] >= 0)
            def _():
                tail_copy(e).start()
            return 0

        def wait(e, _):
            @pl.when(tail_ref[e] >= 0)
            def _():
                tail_copy(e).wait()
            return 0

        lax.fori_loop(0, N_EXPERTS, start, 0)
        lax.fori_loop(0, N_EXPERTS, wait, 0)

        def spare_copy(blk):
            return pltpu.make_async_copy(zero_scr, xs_ref.at[pl.ds(pl.multiple_of(blk * GROUP_SUB, GROUP_SUB), GROUP_SUB)], sem)

        def start_spare(blk, _):
            spare_copy(blk).start()
            return 0

        def wait_spare(blk, _):
            spare_copy(blk).wait()
            return 0

        nblk = xs_ref.shape[0] // GROUP_SUB
        lax.fori_loop(tail_ref[N_EXPERTS], nblk, start_spare, 0)
        lax.fori_loop(tail_ref[N_EXPERTS], nblk, wait_spare, 0)

    base = i * DISPATCH_TILE
    n_tok = pl.num_programs(0) * DISPATCH_TILE

    def row_copy(t, k):
        d = dest_ref[k * n_tok + base + t]
        return pltpu.make_async_copy(x1_ref.at[pl.ds(pl.multiple_of(t * SUBLANES, SUBLANES), SUBLANES)],
                                     xs_ref.at[pl.ds(pl.multiple_of(d * SUBLANES, SUBLANES), SUBLANES)], sem)

    def start_rows(t, _):
        for k in range(MOE_TOPK):
            row_copy(t, k).start(priority=k % DMA_PRIORITIES)
        return 0

    lax.fori_loop(0, DISPATCH_TILE, start_rows, 0, unroll=ISSUE_UNROLL)
    for k in range(MOE_TOPK):
        pltpu.make_async_copy(x1_ref, xs_ref.at[pl.ds(0, DISPATCH_TILE * SUBLANES)], sem).wait()


def _dispatch(dest, tail, x1t, n_rows):
    return pl.pallas_call(
        _dispatch_kernel,
        out_shape=jax.ShapeDtypeStruct((n_rows * SUBLANES, LANES), F32),
        grid_spec=pltpu.PrefetchScalarGridSpec(
            num_scalar_prefetch=2,
            grid=(x1t.shape[0] // (DISPATCH_TILE * SUBLANES),),
            in_specs=[pl.BlockSpec((DISPATCH_TILE * SUBLANES, LANES), lambda i, d, t: (i, 0))],
            out_specs=pl.BlockSpec(memory_space=pl.ANY),
            scratch_shapes=[pltpu.VMEM((GROUP_SUB, LANES), F32), pltpu.SemaphoreType.DMA(())],
        ),
        compiler_params=pltpu.CompilerParams(dimension_semantics=("arbitrary",),
                                             vmem_limit_bytes=VMEM_LIMIT_BYTES),
        name="dispatch",
    )(dest, tail, x1t)


def _experts_kernel(be_ref, slot_ref, nxt_ref, full_ref, nused_ref, x_ref, wg_hbm, bg_ref, wu_hbm, bu_ref, wd_hbm, bd_ref,
                    y_ref, wg_st, wu_st, wd_st, wg_bf, wu_bf, wd_bf, sems):
    i = pl.program_id(0)
    prev = be_ref[jnp.maximum(i - 1, 0)]
    changed = (i == 0) | (be_ref[i] != prev)
    active = i < nused_ref[0]
    slot = slot_ref[i]

    def weight_copies(expert, s):
        return (pltpu.make_async_copy(wg_hbm.at[expert], wg_st.at[s], sems.at[s, 0]),
                pltpu.make_async_copy(wu_hbm.at[expert], wu_st.at[s], sems.at[s, 1]),
                pltpu.make_async_copy(wd_hbm.at[expert], wd_st.at[s], sems.at[s, 2]))

    @pl.when(i == 0)
    def _():
        for cp in weight_copies(be_ref[0], slot):
            cp.start()

    @pl.when(active & changed)
    def _():
        for cp in weight_copies(be_ref[i], slot):
            cp.wait()

        @pl.when(nxt_ref[i] >= 0)
        def _():
            for cp in weight_copies(nxt_ref[i], 1 - slot):
                cp.start()

        wg_bf[...] = wg_st[slot].astype(BF16)
        wu_bf[...] = wu_st[slot].astype(BF16)
        wd_bf[...] = wd_st[slot].astype(BF16)

    def mlp(nrows):
        rows = pl.ds(0, nrows * SUBLANES)
        xb = _tiles_to_rows(x_ref.at[rows], nrows).astype(BF16)
        g = jnp.minimum(_dot(xb, wg_bf[...]) + bg_ref[0], SWIGLU_LIMIT)
        u = jnp.clip(_dot(xb, wu_bf[...]) + bu_ref[0], -SWIGLU_LIMIT, SWIGLU_LIMIT)
        h = g * (1.0 / (1.0 + jnp.exp(-SWIGLU_ALPHA * g))) * (u + 1.0)
        _rows_to_tiles(y_ref.at[rows], _dot(h.astype(BF16), wd_bf[...]) + bd_ref[0])

    whole = active & (full_ref[i] > 0)

    @pl.when(whole)
    def _():
        mlp(MOE_GROUP)

    @pl.when(active & jnp.logical_not(whole))
    def _():
        mlp(MOE_HALF)

    @pl.when(jnp.logical_not(active))
    def _():
        y_ref[pl.ds(0, HALF_SUB), :] = jnp.zeros((HALF_SUB, LANES), F32)

    @pl.when(jnp.logical_not(whole))
    def _():
        y_ref[pl.ds(HALF_SUB, HALF_SUB), :] = jnp.zeros((HALF_SUB, LANES), F32)


def _experts(blk_expert, blk_slot, blk_next, blk_full, n_used, xs, wg, bg, wu, bu, wd, bd):
    E, D, F = wg.shape
    nblk = xs.shape[0] // GROUP_SUB

    def rowmap(i, be, sl, nx, fl, nu):
        return (jnp.minimum(i, nu[0] - 1), 0)

    def bmap(i, be, sl, nx, fl, nu):
        return (be[i], 0, 0)

    hbm = pl.BlockSpec(memory_space=pl.ANY)
    return pl.pallas_call(
        _experts_kernel,
        out_shape=jax.ShapeDtypeStruct(xs.shape, F32),
        grid_spec=pltpu.PrefetchScalarGridSpec(
            num_scalar_prefetch=5,
            grid=(nblk,),
            in_specs=[pl.BlockSpec((GROUP_SUB, LANES), rowmap),
                      hbm, pl.BlockSpec((1, 1, F), bmap),
                      hbm, pl.BlockSpec((1, 1, F), bmap),
                      hbm, pl.BlockSpec((1, 1, D), bmap)],
            out_specs=pl.BlockSpec((GROUP_SUB, LANES), lambda i, be, sl, nx, fl, nu: (i, 0)),
            scratch_shapes=[pltpu.VMEM((2, D, F), F32), pltpu.VMEM((2, D, F), F32), pltpu.VMEM((2, F, D), F32),
                            pltpu.VMEM((D, F), BF16), pltpu.VMEM((D, F), BF16), pltpu.VMEM((F, D), BF16),
                            pltpu.SemaphoreType.DMA((2, 3))],
        ),
        compiler_params=pltpu.CompilerParams(dimension_semantics=("arbitrary",),
                                             vmem_limit_bytes=VMEM_LIMIT_BYTES),
        name="experts",
    )(blk_expert, blk_slot, blk_next, blk_full, n_used, xs, wg, bg, wu, bu, wd, bd)


def _combine_kernel(alpha, dest_ref, gates_ref, x1_ref, g_ref, b_ref, y_ref, o_ref, ybuf, sems):
    i = pl.program_id(0)
    buf = i % 2

    n_tok = pl.num_programs(0) * COMBINE_TILE

    def gather_tile(tile, into):
        base = tile * COMBINE_TILE

        def start_rows(t, _):
            for k in range(MOE_TOPK):
                d = dest_ref[k * n_tok + base + t]
                pltpu.make_async_copy(y_ref.at[pl.ds(pl.multiple_of(d * SUBLANES, SUBLANES), SUBLANES)],
                                      ybuf.at[into, k, pl.ds(pl.multiple_of(t * SUBLANES, SUBLANES), SUBLANES)],
                                      sems.at[into]).start(priority=k % DMA_PRIORITIES)
            return 0

        lax.fori_loop(0, COMBINE_TILE, start_rows, 0, unroll=ISSUE_UNROLL)

    @pl.when(i == 0)
    def _():
        gather_tile(0, 0)

    @pl.when(i + 1 < pl.num_programs(0))
    def _():
        gather_tile(i + 1, 1 - buf)

    for k in range(MOE_TOPK):
        pltpu.make_async_copy(y_ref.at[pl.ds(0, COMBINE_TILE * SUBLANES)], ybuf.at[buf, k], sems.at[buf]).wait()

    gates = gates_ref[...]
    ffn = gates[:, 0:1] * _tiles_to_rows(ybuf.at[buf, 0], COMBINE_TILE)
    for k in range(1, MOE_TOPK):
        ffn = ffn + gates[:, k:k + 1] * _tiles_to_rows(ybuf.at[buf, k], COMBINE_TILE)
    o_ref[...] = _layer_norm(alpha * x1_ref[...] + ffn, g_ref[...], b_ref[...])


def _combine(dest, gates, x1, g2, b2, y_rows, alpha):
    N, D = x1.shape
    row = lambda i, d: (i, 0)
    full = lambda i, d: (0, 0)
    return pl.pallas_call(
        functools.partial(_combine_kernel, alpha),
        out_shape=jax.ShapeDtypeStruct((N, D), F32),
        grid_spec=pltpu.PrefetchScalarGridSpec(
            num_scalar_prefetch=1,
            grid=(N // COMBINE_TILE,),
            in_specs=[pl.BlockSpec((COMBINE_TILE, LANES), row), pl.BlockSpec((COMBINE_TILE, D), row),
                      pl.BlockSpec(g2.shape, full), pl.BlockSpec(b2.shape, full),
                      pl.BlockSpec(memory_space=pl.ANY)],
            out_specs=pl.BlockSpec((COMBINE_TILE, D), row),
            scratch_shapes=[pltpu.VMEM((2, MOE_TOPK, COMBINE_TILE * SUBLANES, LANES), F32),
                            pltpu.SemaphoreType.DMA((2,))],
        ),
        compiler_params=pltpu.CompilerParams(dimension_semantics=("arbitrary",),
                                             vmem_limit_bytes=VMEM_LIMIT_BYTES),
        name="combine",
    )(dest, gates, x1, g2, b2, y_rows)


def _rot_partner(w, half):
    return jnp.concatenate([-w[..., half:2 * half], w[..., :half]], axis=-1)


def _layer_weights(w_in, w_q_b, w_kv_b):
    D = w_in.shape[0]
    o1 = Q_LORA
    o2 = o1 + KV_LORA
    o3 = o2 + MLA_ROPE
    w_ql, w_kvl, w_kr = w_in[:, :o1], w_in[:, o1:o2], w_in[:, o2:o3]
    w_mq, w_mk, w_mv = w_in[:, o3:o3 + _MH], w_in[:, o3 + _MH:o3 + 2 * _MH], w_in[:, o3 + 2 * _MH:]
    zpad = lambda n: jnp.zeros((D, n), w_in.dtype)
    tail = SLOT - MLA_NOPE - MLA_ROPE
    kr_slot = jnp.concatenate([zpad(MLA_NOPE), w_kr, zpad(tail)], axis=1)
    krs_slot = jnp.concatenate([zpad(MLA_NOPE), _rot_partner(w_kr, MLA_ROPE // 2), zpad(tail)], axis=1)

    def moba_partner(w):
        w3 = w.reshape(D, MOBA_HEADS, MOBA_HD)
        part = jnp.concatenate([_rot_partner(w3[..., :MOBA_ROT], MOBA_ROT // 2),
                                jnp.zeros((D, MOBA_HEADS, MOBA_HD - MOBA_ROT), w.dtype)], axis=-1)
        return part.reshape(D, _MH)

    win = jnp.concatenate([w_ql, w_kvl, kr_slot, krs_slot, w_mk, moba_partner(w_mk)], axis=1).astype(BF16)
    mq_partner = _rot_partner(w_mq.reshape(D, MOBA_HEADS, MOBA_HD)[..., :MOBA_ROT], MOBA_ROT // 2)
    wmt = jnp.concatenate([w_mq, mq_partner.reshape(D, MOBA_HEADS * MOBA_ROT), w_mv], axis=1).T.astype(BF16)

    wq3 = w_q_b.reshape(Q_LORA, MLA_HEADS, MLA_NOPE + MLA_ROPE)
    wqs = _rot_partner(wq3[..., MLA_NOPE:], MLA_ROPE // 2).reshape(Q_LORA, MLA_HEADS * MLA_ROPE)
    wkv3 = w_kv_b.reshape(KV_LORA, MLA_HEADS, MLA_NOPE + MLA_V)
    wk = jnp.concatenate([wkv3[..., :MLA_NOPE], jnp.zeros((KV_LORA, MLA_HEADS, SLOT - MLA_NOPE), w_kv_b.dtype)],
                         axis=-1).reshape(KV_LORA, MLA_HEADS * SLOT).astype(BF16)
    wv = wkv3[..., MLA_NOPE:].reshape(KV_LORA, MLA_HEADS * MLA_V)
    return win, wmt, w_q_b.T.astype(BF16), wqs.T.astype(BF16), wk, wv.T.astype(BF16)


def _rope_tables(positions):
    pos = positions.astype(F32).reshape(1, -1)

    def cs(d_rot):
        inv_freq = ROPE_THETA ** (-jnp.arange(0, d_rot, 2, dtype=F32) / d_rot)
        ang = jnp.concatenate([inv_freq, inv_freq]).reshape(d_rot, 1) * pos
        return jnp.cos(ang), jnp.sin(ang)

    return cs(MLA_ROPE) + cs(MOBA_ROT)


def kernel(x, positions, w_in, q_a_norm, w_q_b, kv_a_norm, w_kv_b, w_o, ln1_g, ln1_b, w_router, b_router,
           w_gate, b_gate, w_up, b_up, w_down, b_down, ln2_g, ln2_b):
    B, T, D = x.shape
    depth = w_in.shape[0]
    alpha = (2.0 * depth) ** 0.25
    N = B * T
    assert T % MOBA_BLOCK == 0 and T // MOBA_BLOCK <= SUBLANES and N % OPROJ_TILE == 0
    assert D == SUBLANES * LANES
    n_asg = N * MOE_TOPK
    n_rows = n_asg + N_EXPERTS * MOE_GROUP
    tables = _rope_tables(positions)
    h = x.reshape(N, D)
    for l in range(depth):
        win, wmt, wqt, wqst, wk, wvt = _layer_weights(w_in[l], w_q_b[l], w_kv_b[l])
        qa, ka, va, mq, mk, mv = _prep(h, win, wmt, wqt, wqst, wk, wvt, q_a_norm[l].reshape(1, -1),
                                       kv_a_norm[l].reshape(1, -1), tables, B, T)
        a = _attention(qa, ka, va, B, T, "mla_attention")
        m = _attention(mq, mk, mv, B, T, "moba_attention")
        wr_pad = jnp.concatenate([w_router[l], jnp.zeros((D, LANES - N_EXPERTS), F32)], axis=1)
        wr_hi = wr_pad.astype(BF16)
        wr = jnp.concatenate([wr_hi, (wr_pad - wr_hi.astype(F32)).astype(BF16)], axis=1)
        br = b_router[l].reshape(N_EXPERTS, 1)
        x1, x1t, route, gates, cnt = _oproj(a, m, h, w_o[l].astype(BF16), ln1_g[l].reshape(1, D), ln1_b[l].reshape(1, D),
                                       wr, br, alpha)
        er = jnp.arange(N_EXPERTS, dtype=jnp.int32)
        counts = cnt[:, 0].astype(jnp.int32)
        padded = (counts + MOE_GROUP - 1) // MOE_GROUP * MOE_GROUP
        upto = er[None, :] <= er[:, None]
        pad_end = jnp.sum(jnp.where(upto, padded[None, :], 0), axis=1)
        pad_start = pad_end - padded
        e_idx = route[:MOE_TOPK]
        group_start = jnp.sum(jnp.where(e_idx[..., None] == er, pad_start, 0), axis=-1)
        dest = (group_start + route[MOE_TOPK:2 * MOE_TOPK]).reshape(n_asg)
        n_used = (pad_end[-1:] // MOE_GROUP).astype(jnp.int32)
        tail = jnp.concatenate([jnp.where(padded > 0, pad_end - MOE_GROUP, -1).astype(jnp.int32), n_used])
        nblk = n_rows // MOE_GROUP
        blk_start = jnp.arange(nblk, dtype=jnp.int32) * MOE_GROUP
        blk_expert = jnp.minimum(jnp.sum((blk_start[:, None] >= pad_end[None, :]).astype(jnp.int32), axis=1),
                                 N_EXPERTS - 1)
        of_expert = blk_expert[:, None] == er[None, :]
        pick = lambda per_expert: jnp.sum(jnp.where(of_expert, per_expert[None, :], 0), axis=1)
        nonempty = padded > 0
        group_index = jnp.sum(jnp.where(upto & nonempty[None, :], 1, 0), axis=1) - 1
        later = (er[None, :] > er[:, None]) & nonempty[None, :]
        next_expert = jnp.min(jnp.where(later, er[None, :], N_EXPERTS), axis=1)
        next_expert = jnp.where(next_expert == N_EXPERTS, -1, next_expert)
        blk_slot = pick(group_index) % 2
        blk_next = pick(next_expert)
        blk_full = (pick(pad_start + counts) > blk_start + MOE_HALF).astype(jnp.int32)
        xs = _dispatch(dest, tail, x1t, n_rows)
        y_rows = _experts(blk_expert, blk_slot.astype(jnp.int32), blk_next, blk_full, n_used, xs, w_gate[l], b_gate[l].reshape(N_EXPERTS, 1, -1),
                          w_up[l], b_up[l].reshape(N_EXPERTS, 1, -1), w_down[l], b_down[l].reshape(N_EXPERTS, 1, -1))
        h = _combine(dest, gates, x1, ln2_g[l].reshape(1, D), ln2_b[l].reshape(1, D), y_rows, alpha)
    return h.reshape(B, T, D)
```

```python
import functools
import math

import jax
import jax.numpy as jnp
from jax import lax
from jax.experimental import pallas as pl
from jax.experimental.pallas import tpu as pltpu

ROPE_THETA = 500000.0
MLA_HEADS = 8
MLA_NOPE = 64
MLA_ROPE = 32
MLA_V = 64
Q_LORA = 256
KV_LORA = 128
MOBA_HEADS = 8
MOBA_HD = 64
MOBA_ROT = MOBA_HD // 4
MOBA_BLOCK = 256
MOBA_TOPK = 3
N_EXPERTS = 32
MOE_TOPK = 4
SWIGLU_LIMIT = 7.0
SWIGLU_ALPHA = 1.702
RMS_EPS = 1e-6
LN_EPS = 1e-5

LANES = 128
SUBLANES = 8
VMEM_LIMIT_BYTES = 56 * 1024 * 1024

SLOT = LANES
TQ = MOBA_BLOCK
ROW_TILE = 256
OPROJ_TILE = 512
MOE_GROUP = 1024
MOE_PARTS = 4
MOE_PART = MOE_GROUP // MOE_PARTS
GROUP_SUB = MOE_GROUP * SUBLANES
PART_SUB = MOE_PART * SUBLANES
DISPATCH_TILE = 512
COMBINE_TILE = 512
ISSUE_UNROLL = 4
DMA_PRIORITIES = 2
HEAD_LANES = 64
ONES_LANE = 64
BIAS_LANE = 64
ATTN_HEADS_PER_STEP = 8
LOG2E = math.log2(math.e)
NEG_BIG = -(2.0 ** 100)

F32 = jnp.float32
BF16 = jnp.bfloat16
NT_DIMS = (((1,), (1,)), ((), ()))


def _dot(a, b, precision=None):
    return jnp.dot(a, b, preferred_element_type=F32, precision=precision)


def _dot_nt(a, b, precision=None):
    return lax.dot_general(a, b, NT_DIMS, preferred_element_type=F32, precision=precision)


def _rows_to_tiles(ref, x):
    rows = x.shape[0]
    for c in range(SUBLANES):
        ref[pl.ds(c, rows, stride=SUBLANES), :] = x[:, c * LANES:(c + 1) * LANES]


def _tiles_to_rows(ref, rows):
    return jnp.concatenate([ref[pl.ds(c, rows, stride=SUBLANES), :] for c in range(SUBLANES)], axis=1)


def _rms(x, g):
    return x * lax.rsqrt(jnp.mean(x * x, axis=-1, keepdims=True) + RMS_EPS) * g


def _layer_norm(x, g, b):
    mu = jnp.mean(x, axis=-1, keepdims=True)
    xc = x - mu
    var = jnp.mean(xc * xc, axis=-1, keepdims=True)
    return xc * lax.rsqrt(var + LN_EPS) * g + b


_C_QL = 0
_C_KVL = _C_QL + Q_LORA
_C_KR = _C_KVL + KV_LORA
_C_KRS = _C_KR + SLOT
_C_MK = _C_KRS + SLOT
_MH = MOBA_HEADS * MOBA_HD
_C_MKS = _C_MK + _MH
_C_END = _C_MKS + _MH


def _prep_kernel(x_ref, win_ref, wmt_ref, wqt_ref, wqst_ref, wk_ref, wvt_ref, qg_ref, kvg_ref,
                 cat_ref, sat_ref, cbt_ref, sbt_ref,
                 qat_ref, ka_ref, vat_ref, mqt_ref, mk_ref, mvt_ref, kmean_scr):
    c = pl.program_id(1)

    @pl.when(c == 0)
    def _():
        kmean_scr[...] = jnp.zeros_like(kmean_scr)

    xb = x_ref[...].astype(BF16)
    cat = cat_ref[...]
    sat = sat_ref[...]
    cbt = cbt_ref[...]
    sbt = sbt_ref[...]
    const = lambda v, n: jnp.full((n, ROW_TILE), v, F32)
    tail = SLOT - MLA_NOPE - MLA_ROPE
    rest = MOBA_HD - MOBA_ROT
    ca = jnp.concatenate([const(1.0, MLA_NOPE), cat, const(0.0, tail)], axis=0).T
    sa = jnp.concatenate([const(0.0, MLA_NOPE), sat, const(0.0, tail)], axis=0).T
    cb = jnp.concatenate([cbt, const(1.0, rest), cbt, const(1.0, rest)], axis=0).T
    sb = jnp.concatenate([sbt, const(0.0, rest), sbt, const(0.0, rest)], axis=0).T
    lane =lax.broadcasted_iota(jnp.int32, (ROW_TILE, SLOT), 1)
    head_lanes = lane < HEAD_LANES
    ones_rows = (lax.broadcasted_iota(jnp.int32, (SLOT - HEAD_LANES, ROW_TILE), 0) == ONES_LANE - HEAD_LANES).astype(F32)

    ql = _dot(xb, win_ref[:, _C_QL:_C_KVL])
    kvl = _dot(xb, win_ref[:, _C_KVL:_C_KR])
    kr = _dot(xb, win_ref[:, _C_KR:_C_KRS])
    krs = _dot(xb, win_ref[:, _C_KRS:_C_MK])
    qn = _rms(ql, qg_ref[...]).astype(BF16)
    kvn = _rms(kvl, kvg_ref[...]).astype(BF16)
    q_t = _dot_nt(wqt_ref[...], qn)
    qs_t = _dot_nt(wqst_ref[...], qn)
    kn = _dot(kvn, wk_ref[...])
    v_t = _dot_nt(wvt_ref[...], kvn)
    scale_a = LOG2E / math.sqrt(MLA_NOPE + MLA_ROPE)
    kro = kr * ca + krs * sa
    qd = MLA_NOPE + MLA_ROPE
    q_pad = jnp.zeros((SLOT - qd, ROW_TILE), F32)
    for h in range(MLA_HEADS):
        sl = slice(h * SLOT, (h + 1) * SLOT)
        q_rope = q_t[h * qd + MLA_NOPE:(h + 1) * qd] * cat + qs_t[h * MLA_ROPE:(h + 1) * MLA_ROPE] * sat
        q_slot = jnp.concatenate([q_t[h * qd:h * qd + MLA_NOPE], q_rope, q_pad], axis=0)
        qat_ref[0, sl, :] = (q_slot * scale_a).astype(BF16)
        ka_ref[:, sl] = (kn[:, sl] + kro).astype(BF16)
        vat_ref[0, sl, :] = jnp.concatenate([v_t[h * MLA_V:(h + 1) * MLA_V], ones_rows], axis=0).astype(BF16)

    mk = _dot(xb, win_ref[:, _C_MK:_C_MKS])
    mks = _dot(xb, win_ref[:, _C_MKS:_C_END])
    npair = _MH // LANES
    pair = lambda a, j: a[:, j * LANES:(j + 1) * LANES]
    mk_rot = [pair(mk, j) * cb + pair(mks, j) * sb for j in range(npair)]
    mk_all = jnp.concatenate(mk_rot, axis=1)

    nrot = MOBA_HEADS * MOBA_ROT
    mq_t = _dot_nt(wmt_ref[0:_MH, :], xb)
    mqs_t = _dot_nt(wmt_ref[_MH:_MH + nrot, :], xb)
    mv_t = _dot_nt(wmt_ref[_MH + nrot:2 * _MH + nrot, :], xb)
    head = lambda a, h: a[h * MOBA_HD:(h + 1) * MOBA_HD]
    mq_rot_t = [jnp.concatenate([head(mq_t, h)[0:MOBA_ROT] * cbt + mqs_t[h * MOBA_ROT:(h + 1) * MOBA_ROT] * sbt,
                                 head(mq_t, h)[MOBA_ROT:]], axis=0) for h in range(MOBA_HEADS)]

    nrow = MOBA_HEADS * SUBLANES
    row_i = lax.broadcasted_iota(jnp.int32, (nrow, _MH), 0)
    lane_i = lax.broadcasted_iota(jnp.int32, (nrow, _MH), 1)
    kmean_c = jnp.mean(mk_all, axis=0, keepdims=True)
    put = ((row_i % SUBLANES) == c) & ((lane_i // MOBA_HD) == (row_i // SUBLANES))
    table = kmean_scr[...]
    gate_t = _dot(table, jnp.concatenate(mq_rot_t, axis=0), precision=lax.Precision.HIGHEST)
    kmean_scr[...] = jnp.where(put, jnp.broadcast_to(kmean_c, (nrow, _MH)), table)

    n_idx = lax.broadcasted_iota(jnp.int32, (SUBLANES, ROW_TILE), 0)
    valid = n_idx < c
    scale_b = LOG2E / math.sqrt(MOBA_HD)
    zero_rows = jnp.zeros((SLOT - HEAD_LANES - SUBLANES, ROW_TILE), F32)
    block_onehot = (lane == BIAS_LANE + c).astype(F32)
    for h in range(MOBA_HEADS):
        g = jnp.where(valid, gate_t[h * SUBLANES:(h + 1) * SUBLANES, :], -jnp.inf)
        rank = jnp.zeros((SUBLANES, ROW_TILE), jnp.int32)
        for k in range(1, SUBLANES):
            other = pltpu.roll(g, k, axis=0)
            other_n = pltpu.roll(n_idx, k, axis=0)
            beats = (other > g) | ((other == g) & (other_n < n_idx))
            rank = rank + beats.astype(jnp.int32)
        keep = (valid & (rank < MOBA_TOPK)) | (n_idx == c)
        bias = jnp.where(keep, 0.0, NEG_BIG)
        sl = slice(h * SLOT, (h + 1) * SLOT)
        mqt_ref[0, sl, :] = jnp.concatenate([mq_rot_t[h] * scale_b, bias, zero_rows], axis=0).astype(BF16)
        mvt_ref[0, sl, :] = jnp.concatenate([head(mv_t, h), ones_rows], axis=0).astype(BF16)
        j, hh = divmod(h, 2)
        k_h = mk_rot[j] if hh == 0 else pltpu.roll(mk_rot[j], HEAD_LANES, axis=1)
        mk_ref[:, sl] = jnp.where(head_lanes, k_h, block_onehot).astype(BF16)


def _prep(x2, win, wmt, wqt, wqst, wk, wvt, qg, kvg, tables, B, T):
    N, D = x2.shape
    nt = T // ROW_TILE
    row = lambda b, c: (b * nt + c, 0)
    col = lambda b, c: (0, b * nt + c)
    full = lambda b, c: (0, 0)
    cat, sat, cbt, sbt = tables
    width = MLA_HEADS * SLOT

    def rows(w):
        return pl.BlockSpec((ROW_TILE, w), row)

    def cols(a):
        return pl.BlockSpec((a.shape[0], ROW_TILE), col)

    def whole(a):
        return pl.BlockSpec(a.shape, full)

    rowmajor = jax.ShapeDtypeStruct((N, width), BF16)
    transposed = jax.ShapeDtypeStruct((B * nt, width, ROW_TILE), BF16)
    t_spec = pl.BlockSpec((1, width, ROW_TILE), lambda b, c: (b * nt + c, 0, 0))
    return pl.pallas_call(
        _prep_kernel,
        out_shape=(transposed, rowmajor, transposed, transposed, rowmajor, transposed),
        grid=(B, nt),
        in_specs=[rows(D), whole(win), whole(wmt), whole(wqt), whole(wqst), whole(wk), whole(wvt), whole(qg), whole(kvg),
                  cols(cat), cols(sat), cols(cbt), cols(sbt)],
        out_specs=(t_spec, rows(width), t_spec, t_spec, rows(width), t_spec),
        scratch_shapes=[pltpu.VMEM((MOBA_HEADS * SUBLANES, _MH), F32)],
        compiler_params=pltpu.CompilerParams(dimension_semantics=("arbitrary", "arbitrary"),
                                             vmem_limit_bytes=VMEM_LIMIT_BYTES),
        name="prep",
    )(x2, win, wmt, wqt, wqst, wk, wvt, qg, kvg, cat, sat, cbt, sbt)


def _attn_kernel(qt_ref, k_ref, vt_ref, o_ref, s_scr):
    i = pl.program_id(2)
    nh = ATTN_HEADS_PER_STEP
    slot = lambda h: slice(h * SLOT, (h + 1) * SLOT)
    nblk = i + 1
    nquad = nblk // 4
    pair_end = 4 * nquad + 2 * ((nblk - 4 * nquad) // 2)
    q_pos = i * TQ + lax.broadcasted_iota(jnp.int32, (1, TQ), 1)

    def fold(s):
        out = s[0:SUBLANES]
        for t in range(1, s.shape[0] // SUBLANES):
            out = jnp.maximum(out, s[t * SUBLANES:(t + 1) * SUBLANES])
        return out

    def score_blocks(j, n, mrun):
        j0 = pl.multiple_of(j * TQ, TQ)
        visible = (j0 + lax.broadcasted_iota(jnp.int32, (n * TQ, TQ), 0)) <= q_pos
        out = []
        for h in range(nh):
            s = _dot(k_ref[pl.ds(j0, n * TQ), slot(h)], qt_ref[0, slot(h), :])
            s = jnp.where(visible, s, -jnp.inf)
            s_scr[h, pl.ds(j, n)] = s.reshape(n, TQ, TQ)
            out.append(jnp.maximum(mrun[h], fold(s)))
        return tuple(out)

    mrun = tuple(jnp.full((SUBLANES, TQ), -jnp.inf, F32) for _ in range(nh))
    mrun = lax.fori_loop(0, nquad, lambda jq, m: score_blocks(4 * jq, 4, m), mrun)
    mrun = lax.fori_loop(2 * nquad, pair_end // 2, lambda jj, m: score_blocks(2 * jj, 2, m), mrun)
    mrun = lax.fori_loop(pair_end, nblk, lambda j, m: score_blocks(j, 1, m), mrun)
    ms = [jnp.max(m, axis=0, keepdims=True) for m in mrun]

    def accumulate(j, n, acc):
        out = []
        for h in range(nh):
            p = jnp.exp2(s_scr[h, pl.ds(j, n)].reshape(n * TQ, TQ) - ms[h]).astype(BF16)
            vt = jnp.concatenate([vt_ref[j + t, slot(h), :] for t in range(n)], axis=1)
            out.append(acc[h] + _dot(vt, p))
        return tuple(out)

    acc = tuple(jnp.zeros((SLOT, TQ), F32) for _ in range(nh))
    acc = lax.fori_loop(0, nquad, lambda jq, a: accumulate(4 * jq, 4, a), acc)
    acc = lax.fori_loop(2 * nquad, pair_end // 2, lambda jj, a: accumulate(2 * jj, 2, a), acc)
    acc = lax.fori_loop(pair_end, nblk, lambda j, a: accumulate(j, 1, a), acc)
    outs = [a[0:HEAD_LANES] / a[ONES_LANE:ONES_LANE + 1] for a in acc]
    for jj in range(nh // 2):
        both = jnp.concatenate([outs[2 * jj], outs[2 * jj + 1]], axis=0)
        o_ref[:, jj * LANES:(jj + 1) * LANES] = both.T.astype(o_ref.dtype)


def _attention(qt, k, vt, B, T, name):
    N = k.shape[0]
    nq = T // TQ
    nh = ATTN_HEADS_PER_STEP
    heads = k.shape[1] // SLOT
    return pl.pallas_call(
        _attn_kernel,
        out_shape=jax.ShapeDtypeStruct((N, heads * HEAD_LANES), BF16),
        grid=(B, heads // nh, nq),
        in_specs=[pl.BlockSpec((1, nh * SLOT, TQ), lambda b, g, i: (b * nq + i, g, 0)),
                  pl.BlockSpec((T, nh * SLOT), lambda b, g, i: (b, g)),
                  pl.BlockSpec((nq, nh * SLOT, TQ), lambda b, g, i: (b, g, 0))],
        out_specs=pl.BlockSpec((TQ, nh * HEAD_LANES), lambda b, g, i: (b * nq + i, g)),
        scratch_shapes=[pltpu.VMEM((nh, nq, TQ, TQ), F32)],
        compiler_params=pltpu.CompilerParams(dimension_semantics=("arbitrary", "arbitrary", "arbitrary"),
                                             vmem_limit_bytes=VMEM_LIMIT_BYTES),
        name=name,
    )(qt, k, vt)


def _oproj_kernel(alpha, a_ref, m_ref, x_ref, wo_ref, g_ref, b_ref, wr_ref, br_ref,
                  x1_ref, x1t_ref, route_ref, gates_ref, cnt_ref, carry_scr):
    i = pl.program_id(0)

    @pl.when(i == 0)
    def _():
        carry_scr[...] = jnp.zeros_like(carry_scr)

    mix = _dot(jnp.concatenate([a_ref[...], m_ref[...]], axis=1), wo_ref[...])
    x1 = _layer_norm(alpha * x_ref[...] + mix, g_ref[...], b_ref[...])
    x1_ref[...] = x1
    _rows_to_tiles(x1t_ref, x1)

    ne = br_ref.shape[0]
    rows = x1.shape[0]
    x_hi = x1.astype(BF16)
    x_lo = (x1 - x_hi.astype(F32)).astype(BF16)
    both = _dot(x_hi, wr_ref[...])
    logits_rm = both[:, :LANES] + both[:, LANES:] + _dot(x_lo, wr_ref[:, :LANES])
    logits = logits_rm.T[0:ne] + br_ref[...]
    expert = lax.broadcasted_iota(jnp.int32, logits.shape, 0)
    expert_f = expert.astype(F32)
    vals, idxs = [], []
    work = logits
    for _ in range(MOE_TOPK):
        mx = jnp.max(work, axis=0, keepdims=True)
        ix = jnp.min(jnp.where(work == mx, expert_f, float(ne)), axis=0, keepdims=True).astype(jnp.int32)
        vals.append(mx)
        idxs.append(ix)
        work = jnp.where(expert == ix, -jnp.inf, work)
    exps = [jnp.exp(v - vals[0]) for v in vals]
    den = exps[0]
    for e in exps[1:]:
        den = den + e

    onehot = jnp.zeros(logits.shape, F32)
    for ix in idxs:
        onehot = onehot + (expert == ix).astype(F32)
    r = lax.broadcasted_iota(jnp.int32, (rows, rows), 0)
    cidx = lax.broadcasted_iota(jnp.int32, (rows, rows), 1)
    earlier = (r < cidx).astype(BF16)
    carry = carry_scr[:, 0:1]
    before = _dot(onehot.astype(BF16), earlier) + carry
    row8 = lax.broadcasted_iota(jnp.int32, (SUBLANES, rows), 0)
    route = jnp.zeros((SUBLANES, rows), jnp.int32)
    gates = jnp.zeros((SUBLANES, rows), F32)
    for k in range(MOE_TOPK):
        rank = jnp.sum(jnp.where(expert == idxs[k], before, 0.0), axis=0, keepdims=True).astype(jnp.int32)
        route = jnp.where(row8 == k, idxs[k], route)
        route = jnp.where(row8 == MOE_TOPK + k, rank, route)
        gates = jnp.where(row8 == k, exps[k] / den, gates)
    route_ref[...] = route
    gates_ref[...] = jnp.concatenate([gates, jnp.zeros((LANES - SUBLANES, rows), F32)], axis=0).T
    new_carry = carry + jnp.sum(onehot, axis=1, keepdims=True)
    carry_scr[...] = jnp.broadcast_to(new_carry, carry_scr.shape)
    cnt_ref[...] = jnp.broadcast_to(new_carry, cnt_ref.shape)


def _oproj(a, m, x2, wo, g1, b1, wr, br, alpha):
    N, D = x2.shape
    nt = N // OPROJ_TILE
    row = lambda i: (i, 0)
    full = lambda i: (0, 0)
    return pl.pallas_call(
        functools.partial(_oproj_kernel, alpha),
        out_shape=(jax.ShapeDtypeStruct((N, D), F32),
                   jax.ShapeDtypeStruct((N * SUBLANES, LANES), F32),
                   jax.ShapeDtypeStruct((SUBLANES, N), jnp.int32),
                   jax.ShapeDtypeStruct((N, LANES), F32),
                   jax.ShapeDtypeStruct((br.shape[0], LANES), F32)),
        grid=(nt,),
        in_specs=[pl.BlockSpec((OPROJ_TILE, a.shape[1]), row), pl.BlockSpec((OPROJ_TILE, m.shape[1]), row),
                  pl.BlockSpec((OPROJ_TILE, D), row), pl.BlockSpec(wo.shape, full),
                  pl.BlockSpec(g1.shape, full), pl.BlockSpec(b1.shape, full),
                  pl.BlockSpec(wr.shape, full), pl.BlockSpec(br.shape, full)],
        out_specs=(pl.BlockSpec((OPROJ_TILE, D), row), pl.BlockSpec((OPROJ_TILE * SUBLANES, LANES), row),
                   pl.BlockSpec((SUBLANES, OPROJ_TILE), lambda i: (0, i)),
                   pl.BlockSpec((OPROJ_TILE, LANES), row), pl.BlockSpec((br.shape[0], LANES), full)),
        scratch_shapes=[pltpu.VMEM((br.shape[0], LANES), F32)],
        compiler_params=pltpu.CompilerParams(dimension_semantics=("arbitrary",),
                                             vmem_limit_bytes=VMEM_LIMIT_BYTES),
        name="oproj_router",
    )(a, m, x2, wo, g1, b1, wr, br)


def _dispatch_kernel(dest_ref, tail_ref, x1_ref, xs_ref, zero_scr, sem):
    i = pl.program_id(0)

    @pl.when(i == 0)
    def _():
        zero_scr[...] = jnp.zeros_like(zero_scr)

        def tail_copy(e):
            return pltpu.make_async_copy(zero_scr, xs_ref.at[pl.ds(pl.multiple_of(tail_ref[e] * SUBLANES, GROUP_SUB), GROUP_SUB)], sem)

        def start(e, _):
            @pl.when(tail_ref[e] >= 0)
            def _():
                tail_copy(e).start()
            return 0

        def wait(e, _):
            @pl.when(tail_ref[e] >= 0)
            def _():
                tail_copy(e).wait()
            return 0

        lax.fori_loop(0, N_EXPERTS, start, 0)
        lax.fori_loop(0, N_EXPERTS, wait, 0)

        def spare_copy(blk):
            return pltpu.make_async_copy(zero_scr, xs_ref.at[pl.ds(pl.multiple_of(blk * GROUP_SUB, GROUP_SUB), GROUP_SUB)], sem)

        def start_spare(blk, _):
            spare_copy(blk).start()
            return 0

        def wait_spare(blk, _):
            spare_copy(blk).wait()
            return 0

        nblk = xs_ref.shape[0] // GROUP_SUB
        lax.fori_loop(tail_ref[N_EXPERTS], nblk, start_spare, 0)
        lax.fori_loop(tail_ref[N_EXPERTS], nblk, wait_spare, 0)

    base = i * DISPATCH_TILE
    n_tok = pl.num_programs(0) * DISPATCH_TILE

    def row_copy(t, k):
        d = dest_ref[k * n_tok + base + t]
        return pltpu.make_async_copy(x1_ref.at[pl.ds(pl.multiple_of(t * SUBLANES, SUBLANES), SUBLANES)],
                                     xs_ref.at[pl.ds(pl.multiple_of(d * SUBLANES, SUBLANES), SUBLANES)], sem)

    def start_rows(t, _):
        for k in range(MOE_TOPK):
            row_copy(t, k).start(priority=k % DMA_PRIORITIES)
        return 0

    lax.fori_loop(0, DISPATCH_TILE, start_rows, 0, unroll=ISSUE_UNROLL)
    for k in range(MOE_TOPK):
        pltpu.make_async_copy(x1_ref, xs_ref.at[pl.ds(0, DISPATCH_TILE * SUBLANES)], sem).wait()


def _dispatch(dest, tail, x1t, n_rows):
    return pl.pallas_call(
        _dispatch_kernel,
        out_shape=jax.ShapeDtypeStruct((n_rows * SUBLANES, LANES), F32),
        grid_spec=pltpu.PrefetchScalarGridSpec(
            num_scalar_prefetch=2,
            grid=(x1t.shape[0] // (DISPATCH_TILE * SUBLANES),),
            in_specs=[pl.BlockSpec((DISPATCH_TILE * SUBLANES, LANES), lambda i, d, t: (i, 0))],
            out_specs=pl.BlockSpec(memory_space=pl.ANY),
            scratch_shapes=[pltpu.VMEM((GROUP_SUB, LANES), F32), pltpu.SemaphoreType.DMA(())],
        ),
        compiler_params=pltpu.CompilerParams(dimension_semantics=("arbitrary",),
                                             vmem_limit_bytes=VMEM_LIMIT_BYTES),
        name="dispatch",
    )(dest, tail, x1t)


def _experts_kernel(be_ref, nxt_ref, parts_ref, nused_ref, x_ref, wg_hbm, bg_ref, wu_hbm, bu_ref, wd_hbm, bd_ref,
                    y_ref, wg_st, wu_st, wd_st, wg_bf, wu_bf, wd_bf, sems):
    i = pl.program_id(0)
    prev = be_ref[jnp.maximum(i - 1, 0)]
    changed = (i == 0) | (be_ref[i] != prev)
    active = i < nused_ref[0]
    parts = jnp.where(active, parts_ref[i], 0)

    def weight_copies(expert):
        return (pltpu.make_async_copy(wg_hbm.at[expert], wg_st, sems.at[0]),
                pltpu.make_async_copy(wu_hbm.at[expert], wu_st, sems.at[1]),
                pltpu.make_async_copy(wd_hbm.at[expert], wd_st, sems.at[2]))

    @pl.when(i == 0)
    def _():
        for cp in weight_copies(be_ref[0]):
            cp.start()

    @pl.when(active & changed)
    def _():
        for cp in weight_copies(be_ref[i]):
            cp.wait()

    @pl.when(active & changed)
    def _():
        wg_bf[...] = wg_st[...].astype(BF16)
        wu_bf[...] = wu_st[...].astype(BF16)
        wd_bf[...] = wd_st[...].astype(BF16)

    @pl.when(active & changed & (nxt_ref[i] >= 0))
    def _():
        for cp in weight_copies(nxt_ref[i]):
            cp.start()

    def mlp(first_part, nparts):
        rows = pl.ds(first_part * PART_SUB, nparts * PART_SUB)
        xb = _tiles_to_rows(x_ref.at[rows], nparts * MOE_PART).astype(BF16)
        g = jnp.minimum(_dot(xb, wg_bf[...]) + bg_ref[0], SWIGLU_LIMIT)
        u = jnp.clip(_dot(xb, wu_bf[...]) + bu_ref[0], -SWIGLU_LIMIT, SWIGLU_LIMIT)
        h = g * (1.0 / (1.0 + jnp.exp(-SWIGLU_ALPHA * g))) * (u + 1.0)
        _rows_to_tiles(y_ref.at[rows], _dot(h.astype(BF16), wd_bf[...]) + bd_ref[0])

    for half in range(MOE_PARTS // 2):
        lo = 2 * half

        @pl.when(parts >= lo + 2)
        def _():
            mlp(lo, 2)

        @pl.when(parts == lo + 1)
        def _():
            mlp(lo, 1)

    for part in range(MOE_PARTS):
        @pl.when(parts <= part)
        def _():
            y_ref[pl.ds(part * PART_SUB, PART_SUB), :] = jnp.zeros((PART_SUB, LANES), F32)


def _experts(blk_expert, blk_next, blk_parts, n_used, xs, wg, bg, wu, bu, wd, bd):
    E, D, F = wg.shape
    nblk = xs.shape[0] // GROUP_SUB

    def rowmap(i, be, nx, pt, nu):
        return (jnp.minimum(i, nu[0] - 1), 0)

    def bmap(i, be, nx, pt, nu):
        return (be[i], 0, 0)

    hbm = pl.BlockSpec(memory_space=pl.ANY)
    return pl.pallas_call(
        _experts_kernel,
        out_shape=jax.ShapeDtypeStruct(xs.shape, F32),
        grid_spec=pltpu.PrefetchScalarGridSpec(
            num_scalar_prefetch=4,
            grid=(nblk,),
            in_specs=[pl.BlockSpec((GROUP_SUB, LANES), rowmap),
                      hbm, pl.BlockSpec((1, 1, F), bmap),
                      hbm, pl.BlockSpec((1, 1, F), bmap),
                      hbm, pl.BlockSpec((1, 1, D), bmap)],
            out_specs=pl.BlockSpec((GROUP_SUB, LANES), lambda i, be, nx, pt, nu: (i, 0)),
            scratch_shapes=[pltpu.VMEM((D, F), F32), pltpu.VMEM((D, F), F32), pltpu.VMEM((F, D), F32),
                            pltpu.VMEM((D, F), BF16), pltpu.VMEM((D, F), BF16), pltpu.VMEM((F, D), BF16),
                            pltpu.SemaphoreType.DMA((3,))],
        ),
        compiler_params=pltpu.CompilerParams(dimension_semantics=("arbitrary",),
                                             vmem_limit_bytes=VMEM_LIMIT_BYTES),
        name="experts",
    )(blk_expert, blk_next, blk_parts, n_used, xs, wg, bg, wu, bu, wd, bd)


def _combine_kernel(alpha, dest_ref, gates_ref, x1_ref, g_ref, b_ref, y_ref, o_ref, ybuf, sems):
    i = pl.program_id(0)
    buf = i % 2

    n_tok = pl.num_programs(0) * COMBINE_TILE

    def gather_tile(tile, into):
        base = tile * COMBINE_TILE

        def start_rows(t, _):
            for k in range(MOE_TOPK):
                d = dest_ref[k * n_tok + base + t]
                pltpu.make_async_copy(y_ref.at[pl.ds(pl.multiple_of(d * SUBLANES, SUBLANES), SUBLANES)],
                                      ybuf.at[into, k, pl.ds(pl.multiple_of(t * SUBLANES, SUBLANES), SUBLANES)],
                                      sems.at[into]).start(priority=k % DMA_PRIORITIES)
            return 0

        lax.fori_loop(0, COMBINE_TILE, start_rows, 0, unroll=ISSUE_UNROLL)

    @pl.when(i == 0)
    def _():
        gather_tile(0, 0)

    @pl.when(i + 1 < pl.num_programs(0))
    def _():
        gather_tile(i + 1, 1 - buf)

    for k in range(MOE_TOPK):
        pltpu.make_async_copy(y_ref.at[pl.ds(0, COMBINE_TILE * SUBLANES)], ybuf.at[buf, k], sems.at[buf]).wait()

    gates = gates_ref[...]
    ffn = gates[:, 0:1] * _tiles_to_rows(ybuf.at[buf, 0], COMBINE_TILE)
    for k in range(1, MOE_TOPK):
        ffn = ffn + gates[:, k:k + 1] * _tiles_to_rows(ybuf.at[buf, k], COMBINE_TILE)
    o_ref[...] = _layer_norm(alpha * x1_ref[...] + ffn, g_ref[...], b_ref[...])


def _combine(dest, gates, x1, g2, b2, y_rows, alpha):
    N, D = x1.shape
    row = lambda i, d: (i, 0)
    full = lambda i, d: (0, 0)
    return pl.pallas_call(
        functools.partial(_combine_kernel, alpha),
        out_shape=jax.ShapeDtypeStruct((N, D), F32),
        grid_spec=pltpu.PrefetchScalarGridSpec(
            num_scalar_prefetch=1,
            grid=(N // COMBINE_TILE,),
            in_specs=[pl.BlockSpec((COMBINE_TILE, LANES), row), pl.BlockSpec((COMBINE_TILE, D), row),
                      pl.BlockSpec(g2.shape, full), pl.BlockSpec(b2.shape, full),
                      pl.BlockSpec(memory_space=pl.ANY)],
            out_specs=pl.BlockSpec((COMBINE_TILE, D), row),
            scratch_shapes=[pltpu.VMEM((2, MOE_TOPK, COMBINE_TILE * SUBLANES, LANES), F32),
                            pltpu.SemaphoreType.DMA((2,))],
        ),
        compiler_params=pltpu.CompilerParams(dimension_semantics=("arbitrary",),
                                             vmem_limit_bytes=VMEM_LIMIT_BYTES),
        name="combine",
    )(dest, gates, x1, g2, b2, y_rows)


def _rot_partner(w, half):
    return jnp.concatenate([-w[..., half:2 * half], w[..., :half]], axis=-1)


def _layer_weights(w_in, w_q_b, w_kv_b):
    D = w_in.shape[0]
    o1 = Q_LORA
    o2 = o1 + KV_LORA
    o3 = o2 + MLA_ROPE
    w_ql, w_kvl, w_kr = w_in[:, :o1], w_in[:, o1:o2], w_in[:, o2:o3]
    w_mq, w_mk, w_mv = w_in[:, o3:o3 + _MH], w_in[:, o3 + _MH:o3 + 2 * _MH], w_in[:, o3 + 2 * _MH:]
    zpad = lambda n: jnp.zeros((D, n), w_in.dtype)
    tail = SLOT - MLA_NOPE - MLA_ROPE
    kr_slot = jnp.concatenate([zpad(MLA_NOPE), w_kr, zpad(tail)], axis=1)
    krs_slot = jnp.concatenate([zpad(MLA_NOPE), _rot_partner(w_kr, MLA_ROPE // 2), zpad(tail)], axis=1)

    def moba_partner(w):
        w3 = w.reshape(D, MOBA_HEADS, MOBA_HD)
        part = jnp.concatenate([_rot_partner(w3[..., :MOBA_ROT], MOBA_ROT // 2),
                                jnp.zeros((D, MOBA_HEADS, MOBA_HD - MOBA_ROT), w.dtype)], axis=-1)
        return part.reshape(D, _MH)

    win = jnp.concatenate([w_ql, w_kvl, kr_slot, krs_slot, w_mk, moba_partner(w_mk)], axis=1).astype(BF16)
    mq_partner = _rot_partner(w_mq.reshape(D, MOBA_HEADS, MOBA_HD)[..., :MOBA_ROT], MOBA_ROT // 2)
    wmt = jnp.concatenate([w_mq, mq_partner.reshape(D, MOBA_HEADS * MOBA_ROT), w_mv], axis=1).T.astype(BF16)

    wq3 = w_q_b.reshape(Q_LORA, MLA_HEADS, MLA_NOPE + MLA_ROPE)
    wqs = _rot_partner(wq3[..., MLA_NOPE:], MLA_ROPE // 2).reshape(Q_LORA, MLA_HEADS * MLA_ROPE)
    wkv3 = w_kv_b.reshape(KV_LORA, MLA_HEADS, MLA_NOPE + MLA_V)
    wk = jnp.concatenate([wkv3[..., :MLA_NOPE], jnp.zeros((KV_LORA, MLA_HEADS, SLOT - MLA_NOPE), w_kv_b.dtype)],
                         axis=-1).reshape(KV_LORA, MLA_HEADS * SLOT).astype(BF16)
    wv = wkv3[..., MLA_NOPE:].reshape(KV_LORA, MLA_HEADS * MLA_V)
    return win, wmt, w_q_b.T.astype(BF16), wqs.T.astype(BF16), wk, wv.T.astype(BF16)


def _rope_tables(positions):
    pos = positions.astype(F32).reshape(1, -1)

    def cs(d_rot):
        inv_freq = ROPE_THETA ** (-jnp.arange(0, d_rot, 2, dtype=F32) / d_rot)
        ang = jnp.concatenate([inv_freq, inv_freq]).reshape(d_rot, 1) * pos
        return jnp.cos(ang), jnp.sin(ang)

    return cs(MLA_ROPE) + cs(MOBA_ROT)


def kernel(x, positions, w_in, q_a_norm, w_q_b, kv_a_norm, w_kv_b, w_o, ln1_g, ln1_b, w_router, b_router,
           w_gate, b_gate, w_up, b_up, w_down, b_down, ln2_g, ln2_b):
    B, T, D = x.shape
    depth = w_in.shape[0]
    alpha = (2.0 * depth) ** 0.25
    N = B * T
    assert T % MOBA_BLOCK == 0 and T // MOBA_BLOCK <= SUBLANES and N % OPROJ_TILE == 0
    assert D == SUBLANES * LANES
    n_asg = N * MOE_TOPK
    n_rows = n_asg + N_EXPERTS * MOE_GROUP
    tables = _rope_tables(positions)
    h = x.reshape(N, D)
    for l in range(depth):
        win, wmt, wqt, wqst, wk, wvt = _layer_weights(w_in[l], w_q_b[l], w_kv_b[l])
        qa, ka, va, mq, mk, mv = _prep(h, win, wmt, wqt, wqst, wk, wvt, q_a_norm[l].reshape(1, -1),
                                       kv_a_norm[l].reshape(1, -1), tables, B, T)
        a = _attention(qa, ka, va, B, T, "mla_attention")
        m = _attention(mq, mk, mv, B, T, "moba_attention")
        wr_pad = jnp.concatenate([w_router[l], jnp.zeros((D, LANES - N_EXPERTS), F32)], axis=1)
        wr_hi = wr_pad.astype(BF16)
        wr = jnp.concatenate([wr_hi, (wr_pad - wr_hi.astype(F32)).astype(BF16)], axis=1)
        br = b_router[l].reshape(N_EXPERTS, 1)
        x1, x1t, route, gates, cnt = _oproj(a, m, h, w_o[l].astype(BF16), ln1_g[l].reshape(1, D), ln1_b[l].reshape(1, D),
                                       wr, br, alpha)
        er = jnp.arange(N_EXPERTS, dtype=jnp.int32)
        counts = cnt[:, 0].astype(jnp.int32)
        padded = (counts + MOE_GROUP - 1) // MOE_GROUP * MOE_GROUP
        upto = er[None, :] <= er[:, None]
        pad_end = jnp.sum(jnp.where(upto, padded[None, :], 0), axis=1)
        pad_start = pad_end - padded
        e_idx = route[:MOE_TOPK]
        group_start = jnp.sum(jnp.where(e_idx[..., None] == er, pad_start, 0), axis=-1)
        dest = (group_start + route[MOE_TOPK:2 * MOE_TOPK]).reshape(n_asg)
        n_used = (pad_end[-1:] // MOE_GROUP).astype(jnp.int32)
        tail = jnp.concatenate([jnp.where(padded > 0, pad_end - MOE_GROUP, -1).astype(jnp.int32), n_used])
        nblk = n_rows // MOE_GROUP
        blk_start = jnp.arange(nblk, dtype=jnp.int32) * MOE_GROUP
        blk_expert = jnp.minimum(jnp.sum((blk_start[:, None] >= pad_end[None, :]).astype(jnp.int32), axis=1),
                                 N_EXPERTS - 1)
        of_expert = blk_expert[:, None] == er[None, :]
        pick = lambda per_expert: jnp.sum(jnp.where(of_expert, per_expert[None, :], 0), axis=1)
        nonempty = padded > 0
        later = (er[None, :] > er[:, None]) & nonempty[None, :]
        next_expert = jnp.min(jnp.where(later, er[None, :], N_EXPERTS), axis=1)
        next_expert = jnp.where(next_expert == N_EXPERTS, -1, next_expert)
        blk_next = pick(next_expert)
        real_rows = pick(pad_start + counts) - blk_start
        blk_parts = jnp.clip((real_rows + MOE_PART - 1) // MOE_PART, 0, MOE_PARTS).astype(jnp.int32)
        xs = _dispatch(dest, tail, x1t, n_rows)
        y_rows = _experts(blk_expert, blk_next, blk_parts, n_used, xs, w_gate[l], b_gate[l].reshape(N_EXPERTS, 1, -1),
                          w_up[l], b_up[l].reshape(N_EXPERTS, 1, -1), w_down[l], b_down[l].reshape(N_EXPERTS, 1, -1))
        h = _combine(dest, gates, x1, ln2_g[l].reshape(1, D), ln2_b[l].reshape(1, D), y_rows, alpha)
    return h.reshape(B, T, D)
```

```python
import functools
import math

import jax
import jax.numpy as jnp
from jax import lax
from jax.experimental import pallas as pl
from jax.experimental.pallas import tpu as pltpu

ROPE_THETA = 500000.0
MLA_HEADS = 8
MLA_NOPE = 64
MLA_ROPE = 32
MLA_V = 64
Q_LORA = 256
KV_LORA = 128
MOBA_HEADS = 8
MOBA_HD = 64
MOBA_ROT = MOBA_HD // 4
MOBA_BLOCK = 256
MOBA_TOPK = 3
N_EXPERTS = 32
MOE_TOPK = 4
SWIGLU_LIMIT = 7.0
SWIGLU_ALPHA = 1.702
RMS_EPS = 1e-6
LN_EPS = 1e-5

LANES = 128
SUBLANES = 8
VMEM_LIMIT_BYTES = 56 * 1024 * 1024

SLOT = LANES
TQ = MOBA_BLOCK
ROW_TILE = 256
OPROJ_TILE = 512
MOE_GROUP = 512
MOE_HALF = MOE_GROUP // 2
GROUP_SUB = MOE_GROUP * SUBLANES
HALF_SUB = MOE_HALF * SUBLANES
FILL_TAIL, FILL_FREE = 1, 2
DISPATCH_TILE = 512
COMBINE_TILE = 512
ISSUE_UNROLL = 4
DMA_PRIORITIES = 2
HEAD_LANES = 64
ONES_LANE = 64
BIAS_LANE = 64
ATTN_HEADS_PER_STEP = 8
LOG2E = math.log2(math.e)
NEG_BIG = -(2.0 ** 100)

F32 = jnp.float32
BF16 = jnp.bfloat16
NT_DIMS = (((1,), (1,)), ((), ()))


def _dot(a, b, precision=None):
    return jnp.dot(a, b, preferred_element_type=F32, precision=precision)


def _dot_nt(a, b, precision=None):
    return lax.dot_general(a, b, NT_DIMS, preferred_element_type=F32, precision=precision)


def _rows_to_tiles(ref, x):
    rows = x.shape[0]
    for c in range(SUBLANES):
        ref[pl.ds(c, rows, stride=SUBLANES), :] = x[:, c * LANES:(c + 1) * LANES]


def _tiles_to_rows(ref, rows):
    return jnp.concatenate([ref[pl.ds(c, rows, stride=SUBLANES), :] for c in range(SUBLANES)], axis=1)


def _rms(x, g):
    return x * lax.rsqrt(jnp.mean(x * x, axis=-1, keepdims=True) + RMS_EPS) * g


def _layer_norm(x, g, b):
    mu = jnp.mean(x, axis=-1, keepdims=True)
    xc = x - mu
    var = jnp.mean(xc * xc, axis=-1, keepdims=True)
    return xc * lax.rsqrt(var + LN_EPS) * g + b


_C_QL = 0
_C_KVL = _C_QL + Q_LORA
_C_KR = _C_KVL + KV_LORA
_C_KRS = _C_KR + SLOT
_C_MK = _C_KRS + SLOT
_MH = MOBA_HEADS * MOBA_HD
_C_MKS = _C_MK + _MH
_C_END = _C_MKS + _MH


def _prep_kernel(x_ref, win_ref, wmt_ref, wqt_ref, wqst_ref, wk_ref, wvt_ref, qg_ref, kvg_ref,
                 cat_ref, sat_ref, cbt_ref, sbt_ref,
                 qat_ref, ka_ref, vat_ref, mqt_ref, mk_ref, mvt_ref, kmean_scr):
    c = pl.program_id(1)

    @pl.when(c == 0)
    def _():
        kmean_scr[...] = jnp.zeros_like(kmean_scr)

    xb = x_ref[...].astype(BF16)
    cat = cat_ref[...]
    sat = sat_ref[...]
    cbt = cbt_ref[...]
    sbt = sbt_ref[...]
    const = lambda v, n: jnp.full((n, ROW_TILE), v, F32)
    tail = SLOT - MLA_NOPE - MLA_ROPE
    rest = MOBA_HD - MOBA_ROT
    ca = jnp.concatenate([const(1.0, MLA_NOPE), cat, const(0.0, tail)], axis=0).T
    sa = jnp.concatenate([const(0.0, MLA_NOPE), sat, const(0.0, tail)], axis=0).T
    cb = jnp.concatenate([cbt, const(1.0, rest), cbt, const(1.0, rest)], axis=0).T
    sb = jnp.concatenate([sbt, const(0.0, rest), sbt, const(0.0, rest)], axis=0).T
    lane =lax.broadcasted_iota(jnp.int32, (ROW_TILE, SLOT), 1)
    head_lanes = lane < HEAD_LANES
    ones_rows = (lax.broadcasted_iota(jnp.int32, (SLOT - HEAD_LANES, ROW_TILE), 0) == ONES_LANE - HEAD_LANES).astype(F32)

    ql = _dot(xb, win_ref[:, _C_QL:_C_KVL])
    kvl = _dot(xb, win_ref[:, _C_KVL:_C_KR])
    kr = _dot(xb, win_ref[:, _C_KR:_C_KRS])
    krs = _dot(xb, win_ref[:, _C_KRS:_C_MK])
    qn = _rms(ql, qg_ref[...]).astype(BF16)
    kvn = _rms(kvl, kvg_ref[...]).astype(BF16)
    q_t = _dot_nt(wqt_ref[...], qn)
    qs_t = _dot_nt(wqst_ref[...], qn)
    kn = _dot(kvn, wk_ref[...])
    v_t = _dot_nt(wvt_ref[...], kvn)
    scale_a = LOG2E / math.sqrt(MLA_NOPE + MLA_ROPE)
    kro = kr * ca + krs * sa
    qd = MLA_NOPE + MLA_ROPE
    q_pad = jnp.zeros((SLOT - qd, ROW_TILE), F32)
    for h in range(MLA_HEADS):
        sl = slice(h * SLOT, (h + 1) * SLOT)
        q_rope = q_t[h * qd + MLA_NOPE:(h + 1) * qd] * cat + qs_t[h * MLA_ROPE:(h + 1) * MLA_ROPE] * sat
        q_slot = jnp.concatenate([q_t[h * qd:h * qd + MLA_NOPE], q_rope, q_pad], axis=0)
        qat_ref[0, sl, :] = (q_slot * scale_a).astype(BF16)
        ka_ref[:, sl] = (kn[:, sl] + kro).astype(BF16)
        vat_ref[0, sl, :] = jnp.concatenate([v_t[h * MLA_V:(h + 1) * MLA_V], ones_rows], axis=0).astype(BF16)

    mk = _dot(xb, win_ref[:, _C_MK:_C_MKS])
    mks = _dot(xb, win_ref[:, _C_MKS:_C_END])
    npair = _MH // LANES
    pair = lambda a, j: a[:, j * LANES:(j + 1) * LANES]
    mk_rot = [pair(mk, j) * cb + pair(mks, j) * sb for j in range(npair)]
    mk_all = jnp.concatenate(mk_rot, axis=1)

    nrot = MOBA_HEADS * MOBA_ROT
    mq_t = _dot_nt(wmt_ref[0:_MH, :], xb)
    mqs_t = _dot_nt(wmt_ref[_MH:_MH + nrot, :], xb)
    mv_t = _dot_nt(wmt_ref[_MH + nrot:2 * _MH + nrot, :], xb)
    head = lambda a, h: a[h * MOBA_HD:(h + 1) * MOBA_HD]
    mq_rot_t = [jnp.concatenate([head(mq_t, h)[0:MOBA_ROT] * cbt + mqs_t[h * MOBA_ROT:(h + 1) * MOBA_ROT] * sbt,
                                 head(mq_t, h)[MOBA_ROT:]], axis=0) for h in range(MOBA_HEADS)]

    nrow = MOBA_HEADS * SUBLANES
    row_i = lax.broadcasted_iota(jnp.int32, (nrow, _MH), 0)
    lane_i = lax.broadcasted_iota(jnp.int32, (nrow, _MH), 1)
    kmean_c = jnp.mean(mk_all, axis=0, keepdims=True)
    put = ((row_i % SUBLANES) == c) & ((lane_i // MOBA_HD) == (row_i // SUBLANES))
    table = kmean_scr[...]
    gate_t = _dot(table, jnp.concatenate(mq_rot_t, axis=0), precision=lax.Precision.HIGHEST)
    kmean_scr[...] = jnp.where(put, jnp.broadcast_to(kmean_c, (nrow, _MH)), table)

    n_idx = lax.broadcasted_iota(jnp.int32, (SUBLANES, ROW_TILE), 0)
    valid = n_idx < c
    scale_b = LOG2E / math.sqrt(MOBA_HD)
    zero_rows = jnp.zeros((SLOT - HEAD_LANES - SUBLANES, ROW_TILE), F32)
    block_onehot = (lane == BIAS_LANE + c).astype(F32)
    for h in range(MOBA_HEADS):
        g = jnp.where(valid, gate_t[h * SUBLANES:(h + 1) * SUBLANES, :], -jnp.inf)
        rank = jnp.zeros((SUBLANES, ROW_TILE), jnp.int32)
        for k in range(1, SUBLANES):
            other = pltpu.roll(g, k, axis=0)
            other_n = pltpu.roll(n_idx, k, axis=0)
            beats = (other > g) | ((other == g) & (other_n < n_idx))
            rank = rank + beats.astype(jnp.int32)
        keep = (valid & (rank < MOBA_TOPK)) | (n_idx == c)
        bias = jnp.where(keep, 0.0, NEG_BIG)
        sl = slice(h * SLOT, (h + 1) * SLOT)
        mqt_ref[0, sl, :] = jnp.concatenate([mq_rot_t[h] * scale_b, bias, zero_rows], axis=0).astype(BF16)
        mvt_ref[0, sl, :] = jnp.concatenate([head(mv_t, h), ones_rows], axis=0).astype(BF16)
        j, hh = divmod(h, 2)
        k_h = mk_rot[j] if hh == 0 else pltpu.roll(mk_rot[j], HEAD_LANES, axis=1)
        mk_ref[:, sl] = jnp.where(head_lanes, k_h, block_onehot).astype(BF16)


def _prep(x2, win, wmt, wqt, wqst, wk, wvt, qg, kvg, tables, B, T):
    N, D = x2.shape
    nt = T // ROW_TILE
    row = lambda b, c: (b * nt + c, 0)
    col = lambda b, c: (0, b * nt + c)
    full = lambda b, c: (0, 0)
    cat, sat, cbt, sbt = tables
    width = MLA_HEADS * SLOT

    def rows(w):
        return pl.BlockSpec((ROW_TILE, w), row)

    def cols(a):
        return pl.BlockSpec((a.shape[0], ROW_TILE), col)

    def whole(a):
        return pl.BlockSpec(a.shape, full)

    rowmajor = jax.ShapeDtypeStruct((N, width), BF16)
    transposed = jax.ShapeDtypeStruct((B * nt, width, ROW_TILE), BF16)
    t_spec = pl.BlockSpec((1, width, ROW_TILE), lambda b, c: (b * nt + c, 0, 0))
    return pl.pallas_call(
        _prep_kernel,
        out_shape=(transposed, rowmajor, transposed, transposed, rowmajor, transposed),
        grid=(B, nt),
        in_specs=[rows(D), whole(win), whole(wmt), whole(wqt), whole(wqst), whole(wk), whole(wvt), whole(qg), whole(kvg),
                  cols(cat), cols(sat), cols(cbt), cols(sbt)],
        out_specs=(t_spec, rows(width), t_spec, t_spec, rows(width), t_spec),
        scratch_shapes=[pltpu.VMEM((MOBA_HEADS * SUBLANES, _MH), F32)],
        compiler_params=pltpu.CompilerParams(dimension_semantics=("arbitrary", "arbitrary"),
                                             vmem_limit_bytes=VMEM_LIMIT_BYTES),
        name="prep",
    )(x2, win, wmt, wqt, wqst, wk, wvt, qg, kvg, cat, sat, cbt, sbt)


def _attn_kernel(qt_ref, k_ref, vt_ref, o_ref, s_scr):
    i = pl.program_id(2)
    nh = ATTN_HEADS_PER_STEP
    slot = lambda h: slice(h * SLOT, (h + 1) * SLOT)
    nblk = i + 1
    nquad = nblk // 4
    pair_end = 4 * nquad + 2 * ((nblk - 4 * nquad) // 2)
    q_pos = i * TQ + lax.broadcasted_iota(jnp.int32, (1, TQ), 1)

    def fold(s):
        out = s[0:SUBLANES]
        for t in range(1, s.shape[0] // SUBLANES):
            out = jnp.maximum(out, s[t * SUBLANES:(t + 1) * SUBLANES])
        return out

    def score_blocks(j, n, mrun):
        j0 = pl.multiple_of(j * TQ, TQ)
        visible = (j0 + lax.broadcasted_iota(jnp.int32, (n * TQ, TQ), 0)) <= q_pos
        out = []
        for h in range(nh):
            s = _dot(k_ref[pl.ds(j0, n * TQ), slot(h)], qt_ref[0, slot(h), :])
            s = jnp.where(visible, s, -jnp.inf)
            s_scr[h, pl.ds(j, n)] = s.reshape(n, TQ, TQ)
            out.append(jnp.maximum(mrun[h], fold(s)))
        return tuple(out)

    mrun = tuple(jnp.full((SUBLANES, TQ), -jnp.inf, F32) for _ in range(nh))
    mrun = lax.fori_loop(0, nquad, lambda jq, m: score_blocks(4 * jq, 4, m), mrun)
    mrun = lax.fori_loop(2 * nquad, pair_end // 2, lambda jj, m: score_blocks(2 * jj, 2, m), mrun)
    mrun = lax.fori_loop(pair_end, nblk, lambda j, m: score_blocks(j, 1, m), mrun)
    ms = [jnp.max(m, axis=0, keepdims=True) for m in mrun]

    def accumulate(j, n, acc):
        out = []
        for h in range(nh):
            p = jnp.exp2(s_scr[h, pl.ds(j, n)].reshape(n * TQ, TQ) - ms[h]).astype(BF16)
            vt = jnp.concatenate([vt_ref[j + t, slot(h), :] for t in range(n)], axis=1)
            out.append(acc[h] + _dot(vt, p))
        return tuple(out)

    acc = tuple(jnp.zeros((SLOT, TQ), F32) for _ in range(nh))
    acc = lax.fori_loop(0, nquad, lambda jq, a: accumulate(4 * jq, 4, a), acc)
    acc = lax.fori_loop(2 * nquad, pair_end // 2, lambda jj, a: accumulate(2 * jj, 2, a), acc)
    acc = lax.fori_loop(pair_end, nblk, lambda j, a: accumulate(j, 1, a), acc)
    outs = [a[0:HEAD_LANES] / a[ONES_LANE:ONES_LANE + 1] for a in acc]
    for jj in range(nh // 2):
        both = jnp.concatenate([outs[2 * jj], outs[2 * jj + 1]], axis=0)
        o_ref[:, jj * LANES:(jj + 1) * LANES] = both.T.astype(o_ref.dtype)


def _attention(qt, k, vt, B, T, name):
    N = k.shape[0]
    nq = T // TQ
    nh = ATTN_HEADS_PER_STEP
    heads = k.shape[1] // SLOT
    return pl.pallas_call(
        _attn_kernel,
        out_shape=jax.ShapeDtypeStruct((N, heads * HEAD_LANES), BF16),
        grid=(B, heads // nh, nq),
        in_specs=[pl.BlockSpec((1, nh * SLOT, TQ), lambda b, g, i: (b * nq + i, g, 0)),
                  pl.BlockSpec((T, nh * SLOT), lambda b, g, i: (b, g)),
                  pl.BlockSpec((nq, nh * SLOT, TQ), lambda b, g, i: (b, g, 0))],
        out_specs=pl.BlockSpec((TQ, nh * HEAD_LANES), lambda b, g, i: (b * nq + i, g)),
        scratch_shapes=[pltpu.VMEM((nh, nq, TQ, TQ), F32)],
        compiler_params=pltpu.CompilerParams(dimension_semantics=("arbitrary", "arbitrary", "arbitrary"),
                                             vmem_limit_bytes=VMEM_LIMIT_BYTES),
        name=name,
    )(qt, k, vt)


def _oproj_kernel(alpha, a_ref, m_ref, x_ref, wo_ref, g_ref, b_ref, wr_ref, br_ref,
                  x1_ref, x1t_ref, route_ref, gates_ref, cnt_ref, carry_scr):
    i = pl.program_id(0)

    @pl.when(i == 0)
    def _():
        carry_scr[...] = jnp.zeros_like(carry_scr)

    mix = _dot(jnp.concatenate([a_ref[...], m_ref[...]], axis=1), wo_ref[...])
    x1 = _layer_norm(alpha * x_ref[...] + mix, g_ref[...], b_ref[...])
    x1_ref[...] = x1
    _rows_to_tiles(x1t_ref, x1)

    ne = br_ref.shape[0]
    rows = x1.shape[0]
    x_hi = x1.astype(BF16)
    x_lo = (x1 - x_hi.astype(F32)).astype(BF16)
    both = _dot(x_hi, wr_ref[...])
    logits_rm = both[:, :LANES] + both[:, LANES:] + _dot(x_lo, wr_ref[:, :LANES])
    logits = logits_rm.T[0:ne] + br_ref[...]
    expert = lax.broadcasted_iota(jnp.int32, logits.shape, 0)
    expert_f = expert.astype(F32)
    vals, idxs = [], []
    work = logits
    for _ in range(MOE_TOPK):
        mx = jnp.max(work, axis=0, keepdims=True)
        ix = jnp.min(jnp.where(work == mx, expert_f, float(ne)), axis=0, keepdims=True).astype(jnp.int32)
        vals.append(mx)
        idxs.append(ix)
        work = jnp.where(expert == ix, -jnp.inf, work)
    exps = [jnp.exp(v - vals[0]) for v in vals]
    den = exps[0]
    for e in exps[1:]:
        den = den + e

    onehot = jnp.zeros(logits.shape, F32)
    for ix in idxs:
        onehot = onehot + (expert == ix).astype(F32)
    r = lax.broadcasted_iota(jnp.int32, (rows, rows), 0)
    cidx = lax.broadcasted_iota(jnp.int32, (rows, rows), 1)
    earlier = (r < cidx).astype(BF16)
    carry = carry_scr[:, 0:1]
    before = _dot(onehot.astype(BF16), earlier) + carry
    row8 = lax.broadcasted_iota(jnp.int32, (SUBLANES, rows), 0)
    route = jnp.zeros((SUBLANES, rows), jnp.int32)
    gates = jnp.zeros((SUBLANES, rows), F32)
    for k in range(MOE_TOPK):
        rank = jnp.sum(jnp.where(expert == idxs[k], before, 0.0), axis=0, keepdims=True).astype(jnp.int32)
        route = jnp.where(row8 == k, idxs[k], route)
        route = jnp.where(row8 == MOE_TOPK + k, rank, route)
        gates = jnp.where(row8 == k, exps[k] / den, gates)
    route_ref[...] = route
    gates_ref[...] = jnp.concatenate([gates, jnp.zeros((LANES - SUBLANES, rows), F32)], axis=0).T
    new_carry = carry + jnp.sum(onehot, axis=1, keepdims=True)
    carry_scr[...] = jnp.broadcast_to(new_carry, carry_scr.shape)
    cnt_ref[...] = jnp.broadcast_to(new_carry, cnt_ref.shape)


def _oproj(a, m, x2, wo, g1, b1, wr, br, alpha):
    N, D = x2.shape
    nt = N // OPROJ_TILE
    row = lambda i: (i, 0)
    full = lambda i: (0, 0)
    return pl.pallas_call(
        functools.partial(_oproj_kernel, alpha),
        out_shape=(jax.ShapeDtypeStruct((N, D), F32),
                   jax.ShapeDtypeStruct((N * SUBLANES, LANES), F32),
                   jax.ShapeDtypeStruct((SUBLANES, N), jnp.int32),
                   jax.ShapeDtypeStruct((N, LANES), F32),
                   jax.ShapeDtypeStruct((br.shape[0], LANES), F32)),
        grid=(nt,),
        in_specs=[pl.BlockSpec((OPROJ_TILE, a.shape[1]), row), pl.BlockSpec((OPROJ_TILE, m.shape[1]), row),
                  pl.BlockSpec((OPROJ_TILE, D), row), pl.BlockSpec(wo.shape, full),
                  pl.BlockSpec(g1.shape, full), pl.BlockSpec(b1.shape, full),
                  pl.BlockSpec(wr.shape, full), pl.BlockSpec(br.shape, full)],
        out_specs=(pl.BlockSpec((OPROJ_TILE, D), row), pl.BlockSpec((OPROJ_TILE * SUBLANES, LANES), row),
                   pl.BlockSpec((SUBLANES, OPROJ_TILE), lambda i: (0, i)),
                   pl.BlockSpec((OPROJ_TILE, LANES), row), pl.BlockSpec((br.shape[0], LANES), full)),
        scratch_shapes=[pltpu.VMEM((br.shape[0], LANES), F32)],
        compiler_params=pltpu.CompilerParams(dimension_semantics=("arbitrary",),
                                             vmem_limit_bytes=VMEM_LIMIT_BYTES),
        name="oproj_router",
    )(a, m, x2, wo, g1, b1, wr, br)


def _dispatch_kernel(dest_ref, fill_ref, x1_ref, xs_ref, zero_scr, sem, fill_sem):
    i = pl.program_id(0)
    nfill = xs_ref.shape[0] // HALF_SUB

    def fill_copy(blk, on):
        return pltpu.make_async_copy(zero_scr, xs_ref.at[pl.ds(pl.multiple_of(blk * HALF_SUB, HALF_SUB), HALF_SUB)], on)

    def for_blocks(kind, action):
        def body(blk, _):
            @pl.when(fill_ref[blk] == kind)
            def _():
                action(blk)
            return 0

        lax.fori_loop(0, nfill, body, 0)

    @pl.when(i == 0)
    def _():
        zero_scr[...] = jnp.zeros_like(zero_scr)
        for_blocks(FILL_TAIL, lambda blk: fill_copy(blk, sem).start())
        for_blocks(FILL_FREE, lambda blk: fill_copy(blk, fill_sem).start())
        for_blocks(FILL_TAIL, lambda blk: fill_copy(blk, sem).wait())

    base = i * DISPATCH_TILE
    n_tok = pl.num_programs(0) * DISPATCH_TILE

    def row_copy(t, k):
        d = dest_ref[k * n_tok + base + t]
        return pltpu.make_async_copy(x1_ref.at[pl.ds(pl.multiple_of(t * SUBLANES, SUBLANES), SUBLANES)],
                                     xs_ref.at[pl.ds(pl.multiple_of(d * SUBLANES, SUBLANES), SUBLANES)], sem)

    def start_rows(t, _):
        for k in range(MOE_TOPK):
            row_copy(t, k).start(priority=k % DMA_PRIORITIES)
        return 0

    lax.fori_loop(0, DISPATCH_TILE, start_rows, 0, unroll=ISSUE_UNROLL)
    for k in range(MOE_TOPK):
        pltpu.make_async_copy(x1_ref, xs_ref.at[pl.ds(0, DISPATCH_TILE * SUBLANES)], sem).wait()

    @pl.when(i == pl.num_programs(0) - 1)
    def _():
        for_blocks(FILL_FREE, lambda blk: fill_copy(blk, fill_sem).wait())


def _dispatch(dest, fill, x1t, n_rows):
    return pl.pallas_call(
        _dispatch_kernel,
        out_shape=jax.ShapeDtypeStruct((n_rows * SUBLANES, LANES), F32),
        grid_spec=pltpu.PrefetchScalarGridSpec(
            num_scalar_prefetch=2,
            grid=(x1t.shape[0] // (DISPATCH_TILE * SUBLANES),),
            in_specs=[pl.BlockSpec((DISPATCH_TILE * SUBLANES, LANES), lambda i, d, t: (i, 0))],
            out_specs=pl.BlockSpec(memory_space=pl.ANY),
            scratch_shapes=[pltpu.VMEM((HALF_SUB, LANES), F32), pltpu.SemaphoreType.DMA(()),
                            pltpu.SemaphoreType.DMA(())],
        ),
        compiler_params=pltpu.CompilerParams(dimension_semantics=("arbitrary",),
                                             vmem_limit_bytes=VMEM_LIMIT_BYTES),
        name="dispatch",
    )(dest, fill, x1t)


def _experts_kernel(be_ref, slot_ref, nxt_ref, full_ref, nused_ref, x_ref, wg_hbm, bg_ref, wu_hbm, bu_ref, wd_hbm, bd_ref,
                    y_ref, wg_st, wu_st, wd_st, wg_bf, wu_bf, wd_bf, sems):
    i = pl.program_id(0)
    prev = be_ref[jnp.maximum(i - 1, 0)]
    changed = (i == 0) | (be_ref[i] != prev)
    active = i < nused_ref[0]
    slot = slot_ref[i]

    def weight_copies(expert, s):
        return (pltpu.make_async_copy(wg_hbm.at[expert], wg_st.at[s], sems.at[s, 0]),
                pltpu.make_async_copy(wu_hbm.at[expert], wu_st.at[s], sems.at[s, 1]),
                pltpu.make_async_copy(wd_hbm.at[expert], wd_st.at[s], sems.at[s, 2]))

    @pl.when(i == 0)
    def _():
        for cp in weight_copies(be_ref[0], slot):
            cp.start()

    @pl.when(active & changed)
    def _():
        for cp in weight_copies(be_ref[i], slot):
            cp.wait()

        @pl.when(nxt_ref[i] >= 0)
        def _():
            for cp in weight_copies(nxt_ref[i], 1 - slot):
                cp.start()

        wg_bf[...] = wg_st[slot].astype(BF16)
        wu_bf[...] = wu_st[slot].astype(BF16)
        wd_bf[...] = wd_st[slot].astype(BF16)

    def mlp(nrows):
        rows = pl.ds(0, nrows * SUBLANES)
        xb = _tiles_to_rows(x_ref.at[rows], nrows).astype(BF16)
        g = jnp.minimum(_dot(xb, wg_bf[...]) + bg_ref[0], SWIGLU_LIMIT)
        u = jnp.clip(_dot(xb, wu_bf[...]) + bu_ref[0], -SWIGLU_LIMIT, SWIGLU_LIMIT)
        h = g * (1.0 / (1.0 + jnp.exp(-SWIGLU_ALPHA * g))) * (u + 1.0)
        _rows_to_tiles(y_ref.at[rows], _dot(h.astype(BF16), wd_bf[...]) + bd_ref[0])

    whole = active & (full_ref[i] > 0)

    @pl.when(whole)
    def _():
        mlp(MOE_GROUP)

    @pl.when(active & jnp.logical_not(whole))
    def _():
        mlp(MOE_HALF)

    @pl.when(jnp.logical_not(active))
    def _():
        y_ref[pl.ds(0, HALF_SUB), :] = jnp.zeros((HALF_SUB, LANES), F32)

    @pl.when(jnp.logical_not(whole))
    def _():
        y_ref[pl.ds(HALF_SUB, HALF_SUB), :] = jnp.zeros((HALF_SUB, LANES), F32)


def _experts(blk_expert, blk_slot, blk_next, blk_full, n_used, xs, wg, bg, wu, bu, wd, bd):
    E, D, F = wg.shape
    nblk = xs.shape[0] // GROUP_SUB

    def rowmap(i, be, sl, nx, fl, nu):
        return (jnp.minimum(i, nu[0] - 1), 0)

    def bmap(i, be, sl, nx, fl, nu):
        return (be[i], 0, 0)

    hbm = pl.BlockSpec(memory_space=pl.ANY)
    return pl.pallas_call(
        _experts_kernel,
        out_shape=jax.ShapeDtypeStruct(xs.shape, F32),
        grid_spec=pltpu.PrefetchScalarGridSpec(
            num_scalar_prefetch=5,
            grid=(nblk,),
            in_specs=[pl.BlockSpec((GROUP_SUB, LANES), rowmap),
                      hbm, pl.BlockSpec((1, 1, F), bmap),
                      hbm, pl.BlockSpec((1, 1, F), bmap),
                      hbm, pl.BlockSpec((1, 1, D), bmap)],
            out_specs=pl.BlockSpec((GROUP_SUB, LANES), lambda i, be, sl, nx, fl, nu: (i, 0)),
            scratch_shapes=[pltpu.VMEM((2, D, F), F32), pltpu.VMEM((2, D, F), F32), pltpu.VMEM((2, F, D), F32),
                            pltpu.VMEM((D, F), BF16), pltpu.VMEM((D, F), BF16), pltpu.VMEM((F, D), BF16),
                            pltpu.SemaphoreType.DMA((2, 3))],
        ),
        compiler_params=pltpu.CompilerParams(dimension_semantics=("arbitrary",),
                                             vmem_limit_bytes=VMEM_LIMIT_BYTES),
        name="experts",
    )(blk_expert, blk_slot, blk_next, blk_full, n_used, xs, wg, bg, wu, bu, wd, bd)


def _combine_kernel(alpha, dest_ref, gates_ref, x1_ref, g_ref, b_ref, y_ref, o_ref, ybuf, sems):
    i = pl.program_id(0)
    buf = i % 2

    n_tok = pl.num_programs(0) * COMBINE_TILE

    def gather_tile(tile, into):
        base = tile * COMBINE_TILE

        def start_rows(t, _):
            for k in range(MOE_TOPK):
                d = dest_ref[k * n_tok + base + t]
                pltpu.make_async_copy(y_ref.at[pl.ds(pl.multiple_of(d * SUBLANES, SUBLANES), SUBLANES)],
                                      ybuf.at[into, k, pl.ds(pl.multiple_of(t * SUBLANES, SUBLANES), SUBLANES)],
                                      sems.at[into]).start(priority=k % DMA_PRIORITIES)
            return 0

        lax.fori_loop(0, COMBINE_TILE, start_rows, 0, unroll=ISSUE_UNROLL)

    @pl.when(i == 0)
    def _():
        gather_tile(0, 0)

    @pl.when(i + 1 < pl.num_programs(0))
    def _():
        gather_tile(i + 1, 1 - buf)

    for k in range(MOE_TOPK):
        pltpu.make_async_copy(y_ref.at[pl.ds(0, COMBINE_TILE * SUBLANES)], ybuf.at[buf, k], sems.at[buf]).wait()

    gates = gates_ref[...]
    ffn = gates[:, 0:1] * _tiles_to_rows(ybuf.at[buf, 0], COMBINE_TILE)
    for k in range(1, MOE_TOPK):
        ffn = ffn + gates[:, k:k + 1] * _tiles_to_rows(ybuf.at[buf, k], COMBINE_TILE)
    o_ref[...] = _layer_norm(alpha * x1_ref[...] + ffn, g_ref[...], b_ref[...])


def _combine(dest, gates, x1, g2, b2, y_rows, alpha):
    N, D = x1.shape
    row = lambda i, d: (i, 0)
    full = lambda i, d: (0, 0)
    return pl.pallas_call(
        functools.partial(_combine_kernel, alpha),
        out_shape=jax.ShapeDtypeStruct((N, D), F32),
        grid_spec=pltpu.PrefetchScalarGridSpec(
            num_scalar_prefetch=1,
            grid=(N // COMBINE_TILE,),
            in_specs=[pl.BlockSpec((COMBINE_TILE, LANES), row), pl.BlockSpec((COMBINE_TILE, D), row),
                      pl.BlockSpec(g2.shape, full), pl.BlockSpec(b2.shape, full),
                      pl.BlockSpec(memory_space=pl.ANY)],
            out_specs=pl.BlockSpec((COMBINE_TILE, D), row),
            scratch_shapes=[pltpu.VMEM((2, MOE_TOPK, COMBINE_TILE * SUBLANES, LANES), F32),
                            pltpu.SemaphoreType.DMA((2,))],
        ),
        compiler_params=pltpu.CompilerParams(dimension_semantics=("arbitrary",),
                                             vmem_limit_bytes=VMEM_LIMIT_BYTES),
        name="combine",
    )(dest, gates, x1, g2, b2, y_rows)


def _rot_partner(w, half):
    return jnp.concatenate([-w[..., half:2 * half], w[..., :half]], axis=-1)


def _layer_weights(w_in, w_q_b, w_kv_b):
    D = w_in.shape[0]
    o1 = Q_LORA
    o2 = o1 + KV_LORA
    o3 = o2 + MLA_ROPE
    w_ql, w_kvl, w_kr = w_in[:, :o1], w_in[:, o1:o2], w_in[:, o2:o3]
    w_mq, w_mk, w_mv = w_in[:, o3:o3 + _MH], w_in[:, o3 + _MH:o3 + 2 * _MH], w_in[:, o3 + 2 * _MH:]
    zpad = lambda n: jnp.zeros((D, n), w_in.dtype)
    tail = SLOT - MLA_NOPE - MLA_ROPE
    kr_slot = jnp.concatenate([zpad(MLA_NOPE), w_kr, zpad(tail)], axis=1)
    krs_slot = jnp.concatenate([zpad(MLA_NOPE), _rot_partner(w_kr, MLA_ROPE // 2), zpad(tail)], axis=1)

    def moba_partner(w):
        w3 = w.reshape(D, MOBA_HEADS, MOBA_HD)
        part = jnp.concatenate([_rot_partner(w3[..., :MOBA_ROT], MOBA_ROT // 2),
                                jnp.zeros((D, MOBA_HEADS, MOBA_HD - MOBA_ROT), w.dtype)], axis=-1)
        return part.reshape(D, _MH)

    win = jnp.concatenate([w_ql, w_kvl, kr_slot, krs_slot, w_mk, moba_partner(w_mk)], axis=1).astype(BF16)
    mq_partner = _rot_partner(w_mq.reshape(D, MOBA_HEADS, MOBA_HD)[..., :MOBA_ROT], MOBA_ROT // 2)
    wmt = jnp.concatenate([w_mq, mq_partner.reshape(D, MOBA_HEADS * MOBA_ROT), w_mv], axis=1).T.astype(BF16)

    wq3 = w_q_b.reshape(Q_LORA, MLA_HEADS, MLA_NOPE + MLA_ROPE)
    wqs = _rot_partner(wq3[..., MLA_NOPE:], MLA_ROPE // 2).reshape(Q_LORA, MLA_HEADS * MLA_ROPE)
    wkv3 = w_kv_b.reshape(KV_LORA, MLA_HEADS, MLA_NOPE + MLA_V)
    wk = jnp.concatenate([wkv3[..., :MLA_NOPE], jnp.zeros((KV_LORA, MLA_HEADS, SLOT - MLA_NOPE), w_kv_b.dtype)],
                         axis=-1).reshape(KV_LORA, MLA_HEADS * SLOT).astype(BF16)
    wv = wkv3[..., MLA_NOPE:].reshape(KV_LORA, MLA_HEADS * MLA_V)
    return win, wmt, w_q_b.T.astype(BF16), wqs.T.astype(BF16), wk, wv.T.astype(BF16)


def _rope_tables(positions):
    pos = positions.astype(F32).reshape(1, -1)

    def cs(d_rot):
        inv_freq = ROPE_THETA ** (-jnp.arange(0, d_rot, 2, dtype=F32) / d_rot)
        ang = jnp.concatenate([inv_freq, inv_freq]).reshape(d_rot, 1) * pos
        return jnp.cos(ang), jnp.sin(ang)

    return cs(MLA_ROPE) + cs(MOBA_ROT)


def kernel(x, positions, w_in, q_a_norm, w_q_b, kv_a_norm, w_kv_b, w_o, ln1_g, ln1_b, w_router, b_router,
           w_gate, b_gate, w_up, b_up, w_down, b_down, ln2_g, ln2_b):
    B, T, D = x.shape
    depth = w_in.shape[0]
    alpha = (2.0 * depth) ** 0.25
    N = B * T
    assert T % MOBA_BLOCK == 0 and T // MOBA_BLOCK <= SUBLANES and N % OPROJ_TILE == 0
    assert D == SUBLANES * LANES
    n_asg = N * MOE_TOPK
    n_rows = n_asg + N_EXPERTS * MOE_GROUP
    tables = _rope_tables(positions)
    h = x.reshape(N, D)
    for l in range(depth):
        win, wmt, wqt, wqst, wk, wvt = _layer_weights(w_in[l], w_q_b[l], w_kv_b[l])
        qa, ka, va, mq, mk, mv = _prep(h, win, wmt, wqt, wqst, wk, wvt, q_a_norm[l].reshape(1, -1),
                                       kv_a_norm[l].reshape(1, -1), tables, B, T)
        a = _attention(qa, ka, va, B, T, "mla_attention")
        m = _attention(mq, mk, mv, B, T, "moba_attention")
        wr_pad = jnp.concatenate([w_router[l], jnp.zeros((D, LANES - N_EXPERTS), F32)], axis=1)
        wr_hi = wr_pad.astype(BF16)
        wr = jnp.concatenate([wr_hi, (wr_pad - wr_hi.astype(F32)).astype(BF16)], axis=1)
        br = b_router[l].reshape(N_EXPERTS, 1)
        x1, x1t, route, gates, cnt = _oproj(a, m, h, w_o[l].astype(BF16), ln1_g[l].reshape(1, D), ln1_b[l].reshape(1, D),
                                       wr, br, alpha)
        er = jnp.arange(N_EXPERTS, dtype=jnp.int32)
        counts = cnt[:, 0].astype(jnp.int32)
        padded = (counts + MOE_GROUP - 1) // MOE_GROUP * MOE_GROUP
        upto = er[None, :] <= er[:, None]
        pad_end = jnp.sum(jnp.where(upto, padded[None, :], 0), axis=1)
        pad_start = pad_end - padded
        e_idx = route[:MOE_TOPK]
        group_start = jnp.sum(jnp.where(e_idx[..., None] == er, pad_start, 0), axis=-1)
        dest = (group_start + route[MOE_TOPK:2 * MOE_TOPK]).reshape(n_asg)
        n_used = (pad_end[-1:] // MOE_GROUP).astype(jnp.int32)
        nblk = n_rows // MOE_GROUP
        blk_start = jnp.arange(nblk, dtype=jnp.int32) * MOE_GROUP
        blk_expert = jnp.minimum(jnp.sum((blk_start[:, None] >= pad_end[None, :]).astype(jnp.int32), axis=1),
                                 N_EXPERTS - 1)
        of_expert = blk_expert[:, None] == er[None, :]
        pick = lambda per_expert: jnp.sum(jnp.where(of_expert, per_expert[None, :], 0), axis=1)
        nonempty = padded > 0
        group_index = jnp.sum(jnp.where(upto & nonempty[None, :], 1, 0), axis=1) - 1
        later = (er[None, :] > er[:, None]) & nonempty[None, :]
        next_expert = jnp.min(jnp.where(later, er[None, :], N_EXPERTS), axis=1)
        next_expert = jnp.where(next_expert == N_EXPERTS, -1, next_expert)
        blk_slot = pick(group_index) % 2
        blk_next = pick(next_expert)
        blk_full = (pick(pad_start + counts) > blk_start + MOE_HALF).astype(jnp.int32)
        half_start = jnp.arange(n_rows // MOE_HALF, dtype=jnp.int32) * MOE_HALF
        half_expert = jnp.minimum(jnp.sum((half_start[:, None] >= pad_end[None, :]).astype(jnp.int32), axis=1),
                                  N_EXPERTS - 1)
        half_real_end = jnp.sum(jnp.where(half_expert[:, None] == er[None, :], (pad_start + counts)[None, :], 0), axis=1)
        no_real_rows = (half_start >= pad_end[-1]) | (half_start >= half_real_end)
        fill = jnp.where(no_real_rows, FILL_FREE,
                         jnp.where(half_start + MOE_HALF > half_real_end, FILL_TAIL, 0)).astype(jnp.int32)
        xs = _dispatch(dest, fill, x1t, n_rows)
        y_rows = _experts(blk_expert, blk_slot.astype(jnp.int32), blk_next, blk_full, n_used, xs, w_gate[l], b_gate[l].reshape(N_EXPERTS, 1, -1),
                          w_up[l], b_up[l].reshape(N_EXPERTS, 1, -1), w_down[l], b_down[l].reshape(N_EXPERTS, 1, -1))
        h = _combine(dest, gates, x1, ln2_g[l].reshape(1, D), ln2_b[l].reshape(1, D), y_rows, alpha)
    return h.reshape(B, T, D)
```

```python
import functools
import math

import jax
import jax.numpy as jnp
from jax import lax
from jax.experimental import pallas as pl
from jax.experimental.pallas import tpu as pltpu

ROPE_THETA = 500000.0
MLA_HEADS = 8
MLA_NOPE = 64
MLA_ROPE = 32
MLA_V = 64
Q_LORA = 256
KV_LORA = 128
MOBA_HEADS = 8
MOBA_HD = 64
MOBA_ROT = MOBA_HD // 4
MOBA_BLOCK = 256
MOBA_TOPK = 3
N_EXPERTS = 32
MOE_TOPK = 4
SWIGLU_LIMIT = 7.0
SWIGLU_ALPHA = 1.702
RMS_EPS = 1e-6
LN_EPS = 1e-5

LANES = 128
SUBLANES = 8
VMEM_LIMIT_BYTES = 56 * 1024 * 1024

SLOT = LANES
TQ = MOBA_BLOCK
ROW_TILE = 256
OPROJ_TILE = 512
MOE_GROUP = 512
MOE_HALF = MOE_GROUP // 2
GROUP_SUB = MOE_GROUP * SUBLANES
HALF_SUB = MOE_HALF * SUBLANES
FILL_TAIL, FILL_FREE = 1, 2
DISPATCH_TILE = 512
COMBINE_TILE = 512
ISSUE_UNROLL = 4
DMA_PRIORITIES = 2
HEAD_LANES = 64
ONES_LANE = 64
BIAS_LANE = 64
ATTN_HEADS_PER_STEP = 8
LOG2E = math.log2(math.e)
NEG_BIG = -(2.0 ** 100)

F32 = jnp.float32
BF16 = jnp.bfloat16
NT_DIMS = (((1,), (1,)), ((), ()))


def _dot(a, b, precision=None):
    return jnp.dot(a, b, preferred_element_type=F32, precision=precision)


def _dot_nt(a, b, precision=None):
    return lax.dot_general(a, b, NT_DIMS, preferred_element_type=F32, precision=precision)


def _rows_to_tiles(ref, x):
    rows = x.shape[0]
    for c in range(SUBLANES):
        ref[pl.ds(c, rows, stride=SUBLANES), :] = x[:, c * LANES:(c + 1) * LANES]


def _tiles_to_rows(ref, rows):
    return jnp.concatenate([ref[pl.ds(c, rows, stride=SUBLANES), :] for c in range(SUBLANES)], axis=1)


def _rms(x, g):
    return x * lax.rsqrt(jnp.mean(x * x, axis=-1, keepdims=True) + RMS_EPS) * g


def _layer_norm(x, g, b):
    mu = jnp.mean(x, axis=-1, keepdims=True)
    xc = x - mu
    var = jnp.mean(xc * xc, axis=-1, keepdims=True)
    return xc * lax.rsqrt(var + LN_EPS) * g + b


_C_QL = 0
_C_KVL = _C_QL + Q_LORA
_C_KR = _C_KVL + KV_LORA
_C_KRS = _C_KR + SLOT
_C_MK = _C_KRS + SLOT
_MH = MOBA_HEADS * MOBA_HD
_C_MKS = _C_MK + _MH
_C_END = _C_MKS + _MH


def _prep_kernel(x_ref, win_ref, wmt_ref, wqt_ref, wqst_ref, wk_ref, wvt_ref, qg_ref, kvg_ref,
                 cat_ref, sat_ref, cbt_ref, sbt_ref,
                 qat_ref, ka_ref, vat_ref, mqt_ref, mk_ref, mvt_ref, kmean_scr):
    c = pl.program_id(1)

    @pl.when(c == 0)
    def _():
        kmean_scr[...] = jnp.zeros_like(kmean_scr)

    xb = x_ref[...].astype(BF16)
    cat = cat_ref[...]
    sat = sat_ref[...]
    cbt = cbt_ref[...]
    sbt = sbt_ref[...]
    const = lambda v, n: jnp.full((n, ROW_TILE), v, F32)
    tail = SLOT - MLA_NOPE - MLA_ROPE
    rest = MOBA_HD - MOBA_ROT
    ca = jnp.concatenate([const(1.0, MLA_NOPE), cat, const(0.0, tail)], axis=0).T
    sa = jnp.concatenate([const(0.0, MLA_NOPE), sat, const(0.0, tail)], axis=0).T
    cb = jnp.concatenate([cbt, const(1.0, rest), cbt, const(1.0, rest)], axis=0).T
    sb = jnp.concatenate([sbt, const(0.0, rest), sbt, const(0.0, rest)], axis=0).T
    lane =lax.broadcasted_iota(jnp.int32, (ROW_TILE, SLOT), 1)
    head_lanes = lane < HEAD_LANES
    ones_rows = (lax.broadcasted_iota(jnp.int32, (SLOT - HEAD_LANES, ROW_TILE), 0) == ONES_LANE - HEAD_LANES).astype(F32)

    ql = _dot(xb, win_ref[:, _C_QL:_C_KVL])
    kvl = _dot(xb, win_ref[:, _C_KVL:_C_KR])
    kr = _dot(xb, win_ref[:, _C_KR:_C_KRS])
    krs = _dot(xb, win_ref[:, _C_KRS:_C_MK])
    qn = _rms(ql, qg_ref[...]).astype(BF16)
    kvn = _rms(kvl, kvg_ref[...]).astype(BF16)
    q_t = _dot_nt(wqt_ref[...], qn)
    qs_t = _dot_nt(wqst_ref[...], qn)
    kn = _dot(kvn, wk_ref[...])
    v_t = _dot_nt(wvt_ref[...], kvn)
    scale_a = LOG2E / math.sqrt(MLA_NOPE + MLA_ROPE)
    kro = kr * ca + krs * sa
    qd = MLA_NOPE + MLA_ROPE
    q_pad = jnp.zeros((SLOT - qd, ROW_TILE), F32)
    for h in range(MLA_HEADS):
        sl = slice(h * SLOT, (h + 1) * SLOT)
        q_rope = q_t[h * qd + MLA_NOPE:(h + 1) * qd] * cat + qs_t[h * MLA_ROPE:(h + 1) * MLA_ROPE] * sat
        q_slot = jnp.concatenate([q_t[h * qd:h * qd + MLA_NOPE], q_rope, q_pad], axis=0)
        qat_ref[0, sl, :] = (q_slot * scale_a).astype(BF16)
        ka_ref[:, sl] = (kn[:, sl] + kro).astype(BF16)
        vat_ref[0, sl, :] = jnp.concatenate([v_t[h * MLA_V:(h + 1) * MLA_V], ones_rows], axis=0).astype(BF16)

    mk = _dot(xb, win_ref[:, _C_MK:_C_MKS])
    mks = _dot(xb, win_ref[:, _C_MKS:_C_END])
    npair = _MH // LANES
    pair = lambda a, j: a[:, j * LANES:(j + 1) * LANES]
    mk_rot = [pair(mk, j) * cb + pair(mks, j) * sb for j in range(npair)]
    mk_all = jnp.concatenate(mk_rot, axis=1)

    nrot = MOBA_HEADS * MOBA_ROT
    mq_t = _dot_nt(wmt_ref[0:_MH, :], xb)
    mqs_t = _dot_nt(wmt_ref[_MH:_MH + nrot, :], xb)
    mv_t = _dot_nt(wmt_ref[_MH + nrot:2 * _MH + nrot, :], xb)
    head = lambda a, h: a[h * MOBA_HD:(h + 1) * MOBA_HD]
    mq_rot_t = [jnp.concatenate([head(mq_t, h)[0:MOBA_ROT] * cbt + mqs_t[h * MOBA_ROT:(h + 1) * MOBA_ROT] * sbt,
                                 head(mq_t, h)[MOBA_ROT:]], axis=0) for h in range(MOBA_HEADS)]

    nrow = MOBA_HEADS * SUBLANES
    row_i = lax.broadcasted_iota(jnp.int32, (nrow, _MH), 0)
    lane_i = lax.broadcasted_iota(jnp.int32, (nrow, _MH), 1)
    kmean_c = jnp.mean(mk_all, axis=0, keepdims=True)
    put = ((row_i % SUBLANES) == c) & ((lane_i // MOBA_HD) == (row_i // SUBLANES))
    table = kmean_scr[...]
    gate_t = _dot(table, jnp.concatenate(mq_rot_t, axis=0), precision=lax.Precision.HIGHEST)
    kmean_scr[...] = jnp.where(put, jnp.broadcast_to(kmean_c, (nrow, _MH)), table)

    n_idx = lax.broadcasted_iota(jnp.int32, (SUBLANES, ROW_TILE), 0)
    valid = n_idx < c
    scale_b = LOG2E / math.sqrt(MOBA_HD)
    zero_rows = jnp.zeros((SLOT - HEAD_LANES - SUBLANES, ROW_TILE), F32)
    block_onehot = (lane == BIAS_LANE + c).astype(F32)
    for h in range(MOBA_HEADS):
        g = jnp.where(valid, gate_t[h * SUBLANES:(h + 1) * SUBLANES, :], -jnp.inf)
        rank = jnp.zeros((SUBLANES, ROW_TILE), jnp.int32)
        for k in range(1, SUBLANES):
            other = pltpu.roll(g, k, axis=0)
            other_n = pltpu.roll(n_idx, k, axis=0)
            beats = (other > g) | ((other == g) & (other_n < n_idx))
            rank = rank + beats.astype(jnp.int32)
        keep = (valid & (rank < MOBA_TOPK)) | (n_idx == c)
        bias = jnp.where(keep, 0.0, NEG_BIG)
        sl = slice(h * SLOT, (h + 1) * SLOT)
        mqt_ref[0, sl, :] = jnp.concatenate([mq_rot_t[h] * scale_b, bias, zero_rows], axis=0).astype(BF16)
        mvt_ref[0, sl, :] = jnp.concatenate([head(mv_t, h), ones_rows], axis=0).astype(BF16)
        j, hh = divmod(h, 2)
        k_h = mk_rot[j] if hh == 0 else pltpu.roll(mk_rot[j], HEAD_LANES, axis=1)
        mk_ref[:, sl] = jnp.where(head_lanes, k_h, block_onehot).astype(BF16)


def _prep(x2, win, wmt, wqt, wqst, wk, wvt, qg, kvg, tables, B, T):
    N, D = x2.shape
    nt = T // ROW_TILE
    row = lambda b, c: (b * nt + c, 0)
    col = lambda b, c: (0, b * nt + c)
    full = lambda b, c: (0, 0)
    cat, sat, cbt, sbt = tables
    width = MLA_HEADS * SLOT

    def rows(w):
        return pl.BlockSpec((ROW_TILE, w), row)

    def cols(a):
        return pl.BlockSpec((a.shape[0], ROW_TILE), col)

    def whole(a):
        return pl.BlockSpec(a.shape, full)

    rowmajor = jax.ShapeDtypeStruct((N, width), BF16)
    transposed = jax.ShapeDtypeStruct((B * nt, width, ROW_TILE), BF16)
    t_spec = pl.BlockSpec((1, width, ROW_TILE), lambda b, c: (b * nt + c, 0, 0))
    return pl.pallas_call(
        _prep_kernel,
        out_shape=(transposed, rowmajor, transposed, transposed, rowmajor, transposed),
        grid=(B, nt),
        in_specs=[rows(D), whole(win), whole(wmt), whole(wqt), whole(wqst), whole(wk), whole(wvt), whole(qg), whole(kvg),
                  cols(cat), cols(sat), cols(cbt), cols(sbt)],
        out_specs=(t_spec, rows(width), t_spec, t_spec, rows(width), t_spec),
        scratch_shapes=[pltpu.VMEM((MOBA_HEADS * SUBLANES, _MH), F32)],
        compiler_params=pltpu.CompilerParams(dimension_semantics=("arbitrary", "arbitrary"),
                                             vmem_limit_bytes=VMEM_LIMIT_BYTES),
        name="prep",
    )(x2, win, wmt, wqt, wqst, wk, wvt, qg, kvg, cat, sat, cbt, sbt)


def _attn_kernel(qt_ref, k_ref, vt_ref, o_ref, s_scr, acc_scr):
    i = pl.program_id(2)
    nh = ATTN_HEADS_PER_STEP
    slot = lambda h: slice(h * SLOT, (h + 1) * SLOT)
    nblk = i + 1
    nquad = nblk // 4
    pair_end = 4 * nquad + 2 * ((nblk - 4 * nquad) // 2)
    q_pos = i * TQ + lax.broadcasted_iota(jnp.int32, (1, TQ), 1)

    def fold(s):
        out = s[0:SUBLANES]
        for t in range(1, s.shape[0] // SUBLANES):
            out = jnp.maximum(out, s[t * SUBLANES:(t + 1) * SUBLANES])
        return out

    def score_blocks(j, n, mrun):
        j0 = pl.multiple_of(j * TQ, TQ)
        visible = (j0 + lax.broadcasted_iota(jnp.int32, (n * TQ, TQ), 0)) <= q_pos
        out = []
        for h in range(nh):
            s = _dot(k_ref[pl.ds(j0, n * TQ), slot(h)], qt_ref[0, slot(h), :])
            s = jnp.where(visible, s, -jnp.inf)
            s_scr[h, pl.ds(j, n)] = s.reshape(n, TQ, TQ)
            out.append(jnp.maximum(mrun[h], fold(s)))
        return tuple(out)

    mrun = tuple(jnp.full((SUBLANES, TQ), -jnp.inf, F32) for _ in range(nh))
    mrun = lax.fori_loop(0, nquad, lambda jq, m: score_blocks(4 * jq, 4, m), mrun)
    mrun = lax.fori_loop(2 * nquad, pair_end // 2, lambda jj, m: score_blocks(2 * jj, 2, m), mrun)
    mrun = lax.fori_loop(pair_end, nblk, lambda j, m: score_blocks(j, 1, m), mrun)
    ms = [jnp.max(m, axis=0, keepdims=True) for m in mrun]

    def accumulate(j, n, carry):
        for h in range(nh):
            p = jnp.exp2(s_scr[h, pl.ds(j, n)].reshape(n * TQ, TQ) - ms[h]).astype(BF16)
            vt = jnp.concatenate([vt_ref[j + t, slot(h), :] for t in range(n)], axis=1)
            acc_scr[h] += _dot(vt, p)
        return carry

    acc_scr[...] = jnp.zeros_like(acc_scr)
    lax.fori_loop(0, nquad, lambda jq, c: accumulate(4 * jq, 4, c), 0)
    lax.fori_loop(2 * nquad, pair_end // 2, lambda jj, c: accumulate(2 * jj, 2, c), 0)
    lax.fori_loop(pair_end, nblk, lambda j, c: accumulate(j, 1, c), 0)
    outs = [acc_scr[h, 0:HEAD_LANES] / acc_scr[h, ONES_LANE:ONES_LANE + 1] for h in range(nh)]
    for jj in range(nh // 2):
        both = jnp.concatenate([outs[2 * jj], outs[2 * jj + 1]], axis=0)
        o_ref[:, jj * LANES:(jj + 1) * LANES] = both.T.astype(o_ref.dtype)


def _attention(qt, k, vt, B, T, name):
    N = k.shape[0]
    nq = T // TQ
    nh = ATTN_HEADS_PER_STEP
    heads = k.shape[1] // SLOT
    return pl.pallas_call(
        _attn_kernel,
        out_shape=jax.ShapeDtypeStruct((N, heads * HEAD_LANES), BF16),
        grid=(B, heads // nh, nq),
        in_specs=[pl.BlockSpec((1, nh * SLOT, TQ), lambda b, g, i: (b * nq + i, g, 0)),
                  pl.BlockSpec((T, nh * SLOT), lambda b, g, i: (b, g)),
                  pl.BlockSpec((nq, nh * SLOT, TQ), lambda b, g, i: (b, g, 0))],
        out_specs=pl.BlockSpec((TQ, nh * HEAD_LANES), lambda b, g, i: (b * nq + i, g)),
        scratch_shapes=[pltpu.VMEM((nh, nq, TQ, TQ), F32), pltpu.VMEM((nh, SLOT, TQ), F32)],
        compiler_params=pltpu.CompilerParams(dimension_semantics=("arbitrary", "arbitrary", "arbitrary"),
                                             vmem_limit_bytes=VMEM_LIMIT_BYTES),
        name=name,
    )(qt, k, vt)


def _oproj_kernel(alpha, a_ref, m_ref, x_ref, wo_ref, g_ref, b_ref, wr_ref, br_ref,
                  x1_ref, x1t_ref, route_ref, gates_ref, cnt_ref, carry_scr):
    i = pl.program_id(0)

    @pl.when(i == 0)
    def _():
        carry_scr[...] = jnp.zeros_like(carry_scr)

    mix = _dot(jnp.concatenate([a_ref[...], m_ref[...]], axis=1), wo_ref[...])
    x1 = _layer_norm(alpha * x_ref[...] + mix, g_ref[...], b_ref[...])
    x1_ref[...] = x1
    _rows_to_tiles(x1t_ref, x1)

    ne = br_ref.shape[0]
    rows = x1.shape[0]
    x_hi = x1.astype(BF16)
    x_lo = (x1 - x_hi.astype(F32)).astype(BF16)
    both = _dot(x_hi, wr_ref[...])
    logits_rm = both[:, :LANES] + both[:, LANES:] + _dot(x_lo, wr_ref[:, :LANES])
    logits = logits_rm.T[0:ne] + br_ref[...]
    expert = lax.broadcasted_iota(jnp.int32, logits.shape, 0)
    expert_f = expert.astype(F32)
    vals, idxs = [], []
    work = logits
    for _ in range(MOE_TOPK):
        mx = jnp.max(work, axis=0, keepdims=True)
        ix = jnp.min(jnp.where(work == mx, expert_f, float(ne)), axis=0, keepdims=True).astype(jnp.int32)
        vals.append(mx)
        idxs.append(ix)
        work = jnp.where(expert == ix, -jnp.inf, work)
    exps = [jnp.exp(v - vals[0]) for v in vals]
    den = exps[0]
    for e in exps[1:]:
        den = den + e

    onehot = jnp.zeros(logits.shape, F32)
    for ix in idxs:
        onehot = onehot + (expert == ix).astype(F32)
    r = lax.broadcasted_iota(jnp.int32, (rows, rows), 0)
    cidx = lax.broadcasted_iota(jnp.int32, (rows, rows), 1)
    earlier = (r < cidx).astype(BF16)
    carry = carry_scr[:, 0:1]
    before = _dot(onehot.astype(BF16), earlier) + carry
    row8 = lax.broadcasted_iota(jnp.int32, (SUBLANES, rows), 0)
    route = jnp.zeros((SUBLANES, rows), jnp.int32)
    gates = jnp.zeros((SUBLANES, rows), F32)
    for k in range(MOE_TOPK):
        rank = jnp.sum(jnp.where(expert == idxs[k], before, 0.0), axis=0, keepdims=True).astype(jnp.int32)
        route = jnp.where(row8 == k, idxs[k], route)
        route = jnp.where(row8 == MOE_TOPK + k, rank, route)
        gates = jnp.where(row8 == k, exps[k] / den, gates)
    route_ref[...] = route
    gates_ref[...] = jnp.concatenate([gates, jnp.zeros((LANES - SUBLANES, rows), F32)], axis=0).T
    new_carry = carry + jnp.sum(onehot, axis=1, keepdims=True)
    carry_scr[...] = jnp.broadcast_to(new_carry, carry_scr.shape)
    cnt_ref[...] = jnp.broadcast_to(new_carry, cnt_ref.shape)


def _oproj(a, m, x2, wo, g1, b1, wr, br, alpha):
    N, D = x2.shape
    nt = N // OPROJ_TILE
    row = lambda i: (i, 0)
    full = lambda i: (0, 0)
    return pl.pallas_call(
        functools.partial(_oproj_kernel, alpha),
        out_shape=(jax.ShapeDtypeStruct((N, D), F32),
                   jax.ShapeDtypeStruct((N * SUBLANES, LANES), F32),
                   jax.ShapeDtypeStruct((SUBLANES, N), jnp.int32),
                   jax.ShapeDtypeStruct((N, LANES), F32),
                   jax.ShapeDtypeStruct((br.shape[0], LANES), F32)),
        grid=(nt,),
        in_specs=[pl.BlockSpec((OPROJ_TILE, a.shape[1]), row), pl.BlockSpec((OPROJ_TILE, m.shape[1]), row),
                  pl.BlockSpec((OPROJ_TILE, D), row), pl.BlockSpec(wo.shape, full),
                  pl.BlockSpec(g1.shape, full), pl.BlockSpec(b1.shape, full),
                  pl.BlockSpec(wr.shape, full), pl.BlockSpec(br.shape, full)],
        out_specs=(pl.BlockSpec((OPROJ_TILE, D), row), pl.BlockSpec((OPROJ_TILE * SUBLANES, LANES), row),
                   pl.BlockSpec((SUBLANES, OPROJ_TILE), lambda i: (0, i)),
                   pl.BlockSpec((OPROJ_TILE, LANES), row), pl.BlockSpec((br.shape[0], LANES), full)),
        scratch_shapes=[pltpu.VMEM((br.shape[0], LANES), F32)],
        compiler_params=pltpu.CompilerParams(dimension_semantics=("arbitrary",),
                                             vmem_limit_bytes=VMEM_LIMIT_BYTES),
        name="oproj_router",
    )(a, m, x2, wo, g1, b1, wr, br)


def _dispatch_kernel(dest_ref, fill_ref, x1_ref, xs_ref, zero_scr, sem, fill_sem):
    i = pl.program_id(0)
    nfill = xs_ref.shape[0] // HALF_SUB

    def fill_copy(blk, on):
        return pltpu.make_async_copy(zero_scr, xs_ref.at[pl.ds(pl.multiple_of(blk * HALF_SUB, HALF_SUB), HALF_SUB)], on)

    def for_blocks(kind, action):
        def body(blk, _):
            @pl.when(fill_ref[blk] == kind)
            def _():
                action(blk)
            return 0

        lax.fori_loop(0, nfill, body, 0)

    @pl.when(i == 0)
    def _():
        zero_scr[...] = jnp.zeros_like(zero_scr)
        for_blocks(FILL_TAIL, lambda blk: fill_copy(blk, sem).start())
        for_blocks(FILL_FREE, lambda blk: fill_copy(blk, fill_sem).start())
        for_blocks(FILL_TAIL, lambda blk: fill_copy(blk, sem).wait())

    base = i * DISPATCH_TILE
    n_tok = pl.num_programs(0) * DISPATCH_TILE

    def row_copy(t, k):
        d = dest_ref[k * n_tok + base + t]
        return pltpu.make_async_copy(x1_ref.at[pl.ds(pl.multiple_of(t * SUBLANES, SUBLANES), SUBLANES)],
                                     xs_ref.at[pl.ds(pl.multiple_of(d * SUBLANES, SUBLANES), SUBLANES)], sem)

    def start_rows(t, _):
        for k in range(MOE_TOPK):
            row_copy(t, k).start(priority=k % DMA_PRIORITIES)
        return 0

    lax.fori_loop(0, DISPATCH_TILE, start_rows, 0, unroll=ISSUE_UNROLL)
    for k in range(MOE_TOPK):
        pltpu.make_async_copy(x1_ref, xs_ref.at[pl.ds(0, DISPATCH_TILE * SUBLANES)], sem).wait()

    @pl.when(i == pl.num_programs(0) - 1)
    def _():
        for_blocks(FILL_FREE, lambda blk: fill_copy(blk, fill_sem).wait())


def _dispatch(dest, fill, x1t, n_rows):
    return pl.pallas_call(
        _dispatch_kernel,
        out_shape=jax.ShapeDtypeStruct((n_rows * SUBLANES, LANES), F32),
        grid_spec=pltpu.PrefetchScalarGridSpec(
            num_scalar_prefetch=2,
            grid=(x1t.shape[0] // (DISPATCH_TILE * SUBLANES),),
            in_specs=[pl.BlockSpec((DISPATCH_TILE * SUBLANES, LANES), lambda i, d, t: (i, 0))],
            out_specs=pl.BlockSpec(memory_space=pl.ANY),
            scratch_shapes=[pltpu.VMEM((HALF_SUB, LANES), F32), pltpu.SemaphoreType.DMA(()),
                            pltpu.SemaphoreType.DMA(())],
        ),
        compiler_params=pltpu.CompilerParams(dimension_semantics=("arbitrary",),
                                             vmem_limit_bytes=VMEM_LIMIT_BYTES),
        name="dispatch",
    )(dest, fill, x1t)


def _experts_kernel(be_ref, slot_ref, nxt_ref, full_ref, nused_ref, x_ref, wg_hbm, bg_ref, wu_hbm, bu_ref, wd_hbm, bd_ref,
                    y_ref, wg_st, wu_st, wd_st, wg_bf, wu_bf, wd_bf, sems):
    i = pl.program_id(0)
    prev = be_ref[jnp.maximum(i - 1, 0)]
    changed = (i == 0) | (be_ref[i] != prev)
    active = i < nused_ref[0]
    slot = slot_ref[i]

    def weight_copies(expert, s):
        return (pltpu.make_async_copy(wg_hbm.at[expert], wg_st.at[s], sems.at[s, 0]),
                pltpu.make_async_copy(wu_hbm.at[expert], wu_st.at[s], sems.at[s, 1]),
                pltpu.make_async_copy(wd_hbm.at[expert], wd_st.at[s], sems.at[s, 2]))

    @pl.when(i == 0)
    def _():
        for cp in weight_copies(be_ref[0], slot):
            cp.start()

    @pl.when(active & changed)
    def _():
        for cp in weight_copies(be_ref[i], slot):
            cp.wait()

        @pl.when(nxt_ref[i] >= 0)
        def _():
            for cp in weight_copies(nxt_ref[i], 1 - slot):
                cp.start()

        wg_bf[...] = wg_st[slot].astype(BF16)
        wu_bf[...] = wu_st[slot].astype(BF16)
        wd_bf[...] = wd_st[slot].astype(BF16)

    def mlp(nrows):
        rows = pl.ds(0, nrows * SUBLANES)
        xb = _tiles_to_rows(x_ref.at[rows], nrows).astype(BF16)
        g = jnp.minimum(_dot(xb, wg_bf[...]) + bg_ref[0], SWIGLU_LIMIT)
        u = jnp.clip(_dot(xb, wu_bf[...]) + bu_ref[0], -SWIGLU_LIMIT, SWIGLU_LIMIT)
        h = g * (1.0 / (1.0 + jnp.exp(-SWIGLU_ALPHA * g))) * (u + 1.0)
        _rows_to_tiles(y_ref.at[rows], _dot(h.astype(BF16), wd_bf[...]) + bd_ref[0])

    whole = active & (full_ref[i] > 0)

    @pl.when(whole)
    def _():
        mlp(MOE_GROUP)

    @pl.when(active & jnp.logical_not(whole))
    def _():
        mlp(MOE_HALF)

    @pl.when(jnp.logical_not(active))
    def _():
        y_ref[pl.ds(0, HALF_SUB), :] = jnp.zeros((HALF_SUB, LANES), F32)

    @pl.when(jnp.logical_not(whole))
    def _():
        y_ref[pl.ds(HALF_SUB, HALF_SUB), :] = jnp.zeros((HALF_SUB, LANES), F32)


def _experts(blk_expert, blk_slot, blk_next, blk_full, n_used, xs, wg, bg, wu, bu, wd, bd):
    E, D, F = wg.shape
    nblk = xs.shape[0] // GROUP_SUB

    def rowmap(i, be, sl, nx, fl, nu):
        return (jnp.minimum(i, nu[0] - 1), 0)

    def bmap(i, be, sl, nx, fl, nu):
        return (be[i], 0, 0)

    hbm = pl.BlockSpec(memory_space=pl.ANY)
    return pl.pallas_call(
        _experts_kernel,
        out_shape=jax.ShapeDtypeStruct(xs.shape, F32),
        grid_spec=pltpu.PrefetchScalarGridSpec(
            num_scalar_prefetch=5,
            grid=(nblk,),
            in_specs=[pl.BlockSpec((GROUP_SUB, LANES), rowmap),
                      hbm, pl.BlockSpec((1, 1, F), bmap),
                      hbm, pl.BlockSpec((1, 1, F), bmap),
                      hbm, pl.BlockSpec((1, 1, D), bmap)],
            out_specs=pl.BlockSpec((GROUP_SUB, LANES), lambda i, be, sl, nx, fl, nu: (i, 0)),
            scratch_shapes=[pltpu.VMEM((2, D, F), F32), pltpu.VMEM((2, D, F), F32), pltpu.VMEM((2, F, D), F32),
                            pltpu.VMEM((D, F), BF16), pltpu.VMEM((D, F), BF16), pltpu.VMEM((F, D), BF16),
                            pltpu.SemaphoreType.DMA((2, 3))],
        ),
        compiler_params=pltpu.CompilerParams(dimension_semantics=("arbitrary",),
                                             vmem_limit_bytes=VMEM_LIMIT_BYTES),
        name="experts",
    )(blk_expert, blk_slot, blk_next, blk_full, n_used, xs, wg, bg, wu, bu, wd, bd)


def _combine_kernel(alpha, dest_ref, gates_ref, x1_ref, g_ref, b_ref, y_ref, o_ref, ybuf, sems):
    i = pl.program_id(0)
    buf = i % 2

    n_tok = pl.num_programs(0) * COMBINE_TILE

    def gather_tile(tile, into):
        base = tile * COMBINE_TILE

        def start_rows(t, _):
            for k in range(MOE_TOPK):
                d = dest_ref[k * n_tok + base + t]
                pltpu.make_async_copy(y_ref.at[pl.ds(pl.multiple_of(d * SUBLANES, SUBLANES), SUBLANES)],
                                      ybuf.at[into, k, pl.ds(pl.multiple_of(t * SUBLANES, SUBLANES), SUBLANES)],
                                      sems.at[into]).start(priority=k % DMA_PRIORITIES)
            return 0

        lax.fori_loop(0, COMBINE_TILE, start_rows, 0, unroll=ISSUE_UNROLL)

    @pl.when(i == 0)
    def _():
        gather_tile(0, 0)

    @pl.when(i + 1 < pl.num_programs(0))
    def _():
        gather_tile(i + 1, 1 - buf)

    for k in range(MOE_TOPK):
        pltpu.make_async_copy(y_ref.at[pl.ds(0, COMBINE_TILE * SUBLANES)], ybuf.at[buf, k], sems.at[buf]).wait()

    gates = gates_ref[...]
    ffn = gates[:, 0:1] * _tiles_to_rows(ybuf.at[buf, 0], COMBINE_TILE)
    for k in range(1, MOE_TOPK):
        ffn = ffn + gates[:, k:k + 1] * _tiles_to_rows(ybuf.at[buf, k], COMBINE_TILE)
    o_ref[...] = _layer_norm(alpha * x1_ref[...] + ffn, g_ref[...], b_ref[...])


def _combine(dest, gates, x1, g2, b2, y_rows, alpha):
    N, D = x1.shape
    row = lambda i, d: (i, 0)
    full = lambda i, d: (0, 0)
    return pl.pallas_call(
        functools.partial(_combine_kernel, alpha),
        out_shape=jax.ShapeDtypeStruct((N, D), F32),
        grid_spec=pltpu.PrefetchScalarGridSpec(
            num_scalar_prefetch=1,
            grid=(N // COMBINE_TILE,),
            in_specs=[pl.BlockSpec((COMBINE_TILE, LANES), row), pl.BlockSpec((COMBINE_TILE, D), row),
                      pl.BlockSpec(g2.shape, full), pl.BlockSpec(b2.shape, full),
                      pl.BlockSpec(memory_space=pl.ANY)],
            out_specs=pl.BlockSpec((COMBINE_TILE, D), row),
            scratch_shapes=[pltpu.VMEM((2, MOE_TOPK, COMBINE_TILE * SUBLANES, LANES), F32),
                            pltpu.SemaphoreType.DMA((2,))],
        ),
        compiler_params=pltpu.CompilerParams(dimension_semantics=("arbitrary",),
                                             vmem_limit_bytes=VMEM_LIMIT_BYTES),
        name="combine",
    )(dest, gates, x1, g2, b2, y_rows)


def _rot_partner(w, half):
    return jnp.concatenate([-w[..., half:2 * half], w[..., :half]], axis=-1)


def _layer_weights(w_in, w_q_b, w_kv_b):
    D = w_in.shape[0]
    o1 = Q_LORA
    o2 = o1 + KV_LORA
    o3 = o2 + MLA_ROPE
    w_ql, w_kvl, w_kr = w_in[:, :o1], w_in[:, o1:o2], w_in[:, o2:o3]
    w_mq, w_mk, w_mv = w_in[:, o3:o3 + _MH], w_in[:, o3 + _MH:o3 + 2 * _MH], w_in[:, o3 + 2 * _MH:]
    zpad = lambda n: jnp.zeros((D, n), w_in.dtype)
    tail = SLOT - MLA_NOPE - MLA_ROPE
    kr_slot = jnp.concatenate([zpad(MLA_NOPE), w_kr, zpad(tail)], axis=1)
    krs_slot = jnp.concatenate([zpad(MLA_NOPE), _rot_partner(w_kr, MLA_ROPE // 2), zpad(tail)], axis=1)

    def moba_partner(w):
        w3 = w.reshape(D, MOBA_HEADS, MOBA_HD)
        part = jnp.concatenate([_rot_partner(w3[..., :MOBA_ROT], MOBA_ROT // 2),
                                jnp.zeros((D, MOBA_HEADS, MOBA_HD - MOBA_ROT), w.dtype)], axis=-1)
        return part.reshape(D, _MH)

    win = jnp.concatenate([w_ql, w_kvl, kr_slot, krs_slot, w_mk, moba_partner(w_mk)], axis=1).astype(BF16)
    mq_partner = _rot_partner(w_mq.reshape(D, MOBA_HEADS, MOBA_HD)[..., :MOBA_ROT], MOBA_ROT // 2)
    wmt = jnp.concatenate([w_mq, mq_partner.reshape(D, MOBA_HEADS * MOBA_ROT), w_mv], axis=1).T.astype(BF16)

    wq3 = w_q_b.reshape(Q_LORA, MLA_HEADS, MLA_NOPE + MLA_ROPE)
    wqs = _rot_partner(wq3[..., MLA_NOPE:], MLA_ROPE // 2).reshape(Q_LORA, MLA_HEADS * MLA_ROPE)
    wkv3 = w_kv_b.reshape(KV_LORA, MLA_HEADS, MLA_NOPE + MLA_V)
    wk = jnp.concatenate([wkv3[..., :MLA_NOPE], jnp.zeros((KV_LORA, MLA_HEADS, SLOT - MLA_NOPE), w_kv_b.dtype)],
                         axis=-1).reshape(KV_LORA, MLA_HEADS * SLOT).astype(BF16)
    wv = wkv3[..., MLA_NOPE:].reshape(KV_LORA, MLA_HEADS * MLA_V)
    return win, wmt, w_q_b.T.astype(BF16), wqs.T.astype(BF16), wk, wv.T.astype(BF16)


def _rope_tables(positions):
    pos = positions.astype(F32).reshape(1, -1)

    def cs(d_rot):
        inv_freq = ROPE_THETA ** (-jnp.arange(0, d_rot, 2, dtype=F32) / d_rot)
        ang = jnp.concatenate([inv_freq, inv_freq]).reshape(d_rot, 1) * pos
        return jnp.cos(ang), jnp.sin(ang)

    return cs(MLA_ROPE) + cs(MOBA_ROT)


def kernel(x, positions, w_in, q_a_norm, w_q_b, kv_a_norm, w_kv_b, w_o, ln1_g, ln1_b, w_router, b_router,
           w_gate, b_gate, w_up, b_up, w_down, b_down, ln2_g, ln2_b):
    B, T, D = x.shape
    depth = w_in.shape[0]
    alpha = (2.0 * depth) ** 0.25
    N = B * T
    assert T % MOBA_BLOCK == 0 and T // MOBA_BLOCK <= SUBLANES and N % OPROJ_TILE == 0
    assert D == SUBLANES * LANES
    n_asg = N * MOE_TOPK
    n_rows = n_asg + N_EXPERTS * MOE_GROUP
    tables = _rope_tables(positions)
    h = x.reshape(N, D)
    for l in range(depth):
        win, wmt, wqt, wqst, wk, wvt = _layer_weights(w_in[l], w_q_b[l], w_kv_b[l])
        qa, ka, va, mq, mk, mv = _prep(h, win, wmt, wqt, wqst, wk, wvt, q_a_norm[l].reshape(1, -1),
                                       kv_a_norm[l].reshape(1, -1), tables, B, T)
        a = _attention(qa, ka, va, B, T, "mla_attention")
        m = _attention(mq, mk, mv, B, T, "moba_attention")
        wr_pad = jnp.concatenate([w_router[l], jnp.zeros((D, LANES - N_EXPERTS), F32)], axis=1)
        wr_hi = wr_pad.astype(BF16)
        wr = jnp.concatenate([wr_hi, (wr_pad - wr_hi.astype(F32)).astype(BF16)], axis=1)
        br = b_router[l].reshape(N_EXPERTS, 1)
        x1, x1t, route, gates, cnt = _oproj(a, m, h, w_o[l].astype(BF16), ln1_g[l].reshape(1, D), ln1_b[l].reshape(1, D),
                                       wr, br, alpha)
        er = jnp.arange(N_EXPERTS, dtype=jnp.int32)
        counts = cnt[:, 0].astype(jnp.int32)
        padded = (counts + MOE_GROUP - 1) // MOE_GROUP * MOE_GROUP
        upto = er[None, :] <= er[:, None]
        pad_end = jnp.sum(jnp.where(upto, padded[None, :], 0), axis=1)
        pad_start = pad_end - padded
        e_idx = route[:MOE_TOPK]
        group_start = jnp.sum(jnp.where(e_idx[..., None] == er, pad_start, 0), axis=-1)
        dest = (group_start + route[MOE_TOPK:2 * MOE_TOPK]).reshape(n_asg)
        n_used = (pad_end[-1:] // MOE_GROUP).astype(jnp.int32)
        nblk = n_rows // MOE_GROUP
        blk_start = jnp.arange(nblk, dtype=jnp.int32) * MOE_GROUP
        blk_expert = jnp.minimum(jnp.sum((blk_start[:, None] >= pad_end[None, :]).astype(jnp.int32), axis=1),
                                 N_EXPERTS - 1)
        of_expert = blk_expert[:, None] == er[None, :]
        pick = lambda per_expert: jnp.sum(jnp.where(of_expert, per_expert[None, :], 0), axis=1)
        nonempty = padded > 0
        group_index = jnp.sum(jnp.where(upto & nonempty[None, :], 1, 0), axis=1) - 1
        later = (er[None, :] > er[:, None]) & nonempty[None, :]
        next_expert = jnp.min(jnp.where(later, er[None, :], N_EXPERTS), axis=1)
        next_expert = jnp.where(next_expert == N_EXPERTS, -1, next_expert)
        blk_slot = pick(group_index) % 2
        blk_next = pick(next_expert)
        blk_full = (pick(pad_start + counts) > blk_start + MOE_HALF).astype(jnp.int32)
        half_start = jnp.arange(n_rows // MOE_HALF, dtype=jnp.int32) * MOE_HALF
        half_expert = jnp.minimum(jnp.sum((half_start[:, None] >= pad_end[None, :]).astype(jnp.int32), axis=1),
                                  N_EXPERTS - 1)
        half_real_end = jnp.sum(jnp.where(half_expert[:, None] == er[None, :], (pad_start + counts)[None, :], 0), axis=1)
        no_real_rows = (half_start >= pad_end[-1]) | (half_start >= half_real_end)
        fill = jnp.where(no_real_rows, FILL_FREE,
                         jnp.where(half_start + MOE_HALF > half_real_end, FILL_TAIL, 0)).astype(jnp.int32)
        xs = _dispatch(dest, fill, x1t, n_rows)
        y_rows = _experts(blk_expert, blk_slot.astype(jnp.int32), blk_next, blk_full, n_used, xs, w_gate[l], b_gate[l].reshape(N_EXPERTS, 1, -1),
                          w_up[l], b_up[l].reshape(N_EXPERTS, 1, -1), w_down[l], b_down[l].reshape(N_EXPERTS, 1, -1))
        h = _combine(dest, gates, x1, ln2_g[l].reshape(1, D), ln2_b[l].reshape(1, D), y_rows, alpha)
    return h.reshape(B, T, D)
```

```python
import functools
import math

import jax
import jax.numpy as jnp
from jax import lax
from jax.experimental import pallas as pl
from jax.experimental.pallas import tpu as pltpu

ROPE_THETA = 500000.0
MLA_HEADS = 8
MLA_NOPE = 64
MLA_ROPE = 32
MLA_V = 64
Q_LORA = 256
KV_LORA = 128
MOBA_HEADS = 8
MOBA_HD = 64
MOBA_ROT = MOBA_HD // 4
MOBA_BLOCK = 256
MOBA_TOPK = 3
N_EXPERTS = 32
MOE_TOPK = 4
SWIGLU_LIMIT = 7.0
SWIGLU_ALPHA = 1.702
RMS_EPS = 1e-6
LN_EPS = 1e-5

LANES = 128
SUBLANES = 8
VMEM_LIMIT_BYTES = 56 * 1024 * 1024

SLOT = LANES
TQ = MOBA_BLOCK
ROW_TILE = 256
OPROJ_TILE = 512
MOE_GROUP = 512
MOE_HALF = MOE_GROUP // 2
GROUP_SUB = MOE_GROUP * SUBLANES
HALF_SUB = MOE_HALF * SUBLANES
MOE_QUARTER = MOE_GROUP // 4
QUARTER_SUB = MOE_QUARTER * SUBLANES
FILL_TAIL, FILL_FREE = 1, 2
DISPATCH_TILE = 512
COMBINE_TILE = 512
ISSUE_UNROLL = 4
DMA_PRIORITIES = 2
HEAD_LANES = 64
ONES_LANE = 64
BIAS_LANE = 64
ATTN_HEADS_PER_STEP = 8
LOG2E = math.log2(math.e)
NEG_BIG = -(2.0 ** 100)

F32 = jnp.float32
BF16 = jnp.bfloat16
NT_DIMS = (((1,), (1,)), ((), ()))


def _dot(a, b, precision=None):
    return jnp.dot(a, b, preferred_element_type=F32, precision=precision)


def _dot_nt(a, b, precision=None):
    return lax.dot_general(a, b, NT_DIMS, preferred_element_type=F32, precision=precision)


def _rows_to_tiles(ref, x):
    rows = x.shape[0]
    for c in range(SUBLANES):
        ref[pl.ds(c, rows, stride=SUBLANES), :] = x[:, c * LANES:(c + 1) * LANES]


def _tiles_to_rows(ref, rows):
    return jnp.concatenate([ref[pl.ds(c, rows, stride=SUBLANES), :] for c in range(SUBLANES)], axis=1)


def _rms(x, g):
    return x * lax.rsqrt(jnp.mean(x * x, axis=-1, keepdims=True) + RMS_EPS) * g


def _layer_norm(x, g, b):
    mu = jnp.mean(x, axis=-1, keepdims=True)
    xc = x - mu
    var = jnp.mean(xc * xc, axis=-1, keepdims=True)
    return xc * lax.rsqrt(var + LN_EPS) * g + b


_C_QL = 0
_C_KVL = _C_QL + Q_LORA
_C_KR = _C_KVL + KV_LORA
_C_KRS = _C_KR + SLOT
_C_MK = _C_KRS + SLOT
_MH = MOBA_HEADS * MOBA_HD
_C_MKS = _C_MK + _MH
_C_END = _C_MKS + _MH


def _prep_kernel(x_ref, win_ref, wmt_ref, wqt_ref, wqst_ref, wk_ref, wvt_ref, qg_ref, kvg_ref,
                 cat_ref, sat_ref, cbt_ref, sbt_ref,
                 qat_ref, ka_ref, vat_ref, mqt_ref, mk_ref, mvt_ref, kmean_scr):
    c = pl.program_id(1)

    @pl.when(c == 0)
    def _():
        kmean_scr[...] = jnp.zeros_like(kmean_scr)

    xb = x_ref[...].astype(BF16)
    cat = cat_ref[...]
    sat = sat_ref[...]
    cbt = cbt_ref[...]
    sbt = sbt_ref[...]
    const = lambda v, n: jnp.full((n, ROW_TILE), v, F32)
    tail = SLOT - MLA_NOPE - MLA_ROPE
    rest = MOBA_HD - MOBA_ROT
    ca = jnp.concatenate([const(1.0, MLA_NOPE), cat, const(0.0, tail)], axis=0).T
    sa = jnp.concatenate([const(0.0, MLA_NOPE), sat, const(0.0, tail)], axis=0).T
    cb = jnp.concatenate([cbt, const(1.0, rest), cbt, const(1.0, rest)], axis=0).T
    sb = jnp.concatenate([sbt, const(0.0, rest), sbt, const(0.0, rest)], axis=0).T
    lane =lax.broadcasted_iota(jnp.int32, (ROW_TILE, SLOT), 1)
    head_lanes = lane < HEAD_LANES
    ones_rows = (lax.broadcasted_iota(jnp.int32, (SLOT - HEAD_LANES, ROW_TILE), 0) == ONES_LANE - HEAD_LANES).astype(F32)

    ql = _dot(xb, win_ref[:, _C_QL:_C_KVL])
    kvl = _dot(xb, win_ref[:, _C_KVL:_C_KR])
    kr = _dot(xb, win_ref[:, _C_KR:_C_KRS])
    krs = _dot(xb, win_ref[:, _C_KRS:_C_MK])
    qn = _rms(ql, qg_ref[...]).astype(BF16)
    kvn = _rms(kvl, kvg_ref[...]).astype(BF16)
    q_t = _dot_nt(wqt_ref[...], qn)
    qs_t = _dot_nt(wqst_ref[...], qn)
    kn = _dot(kvn, wk_ref[...])
    v_t = _dot_nt(wvt_ref[...], kvn)
    scale_a = LOG2E / math.sqrt(MLA_NOPE + MLA_ROPE)
    kro = kr * ca + krs * sa
    qd = MLA_NOPE + MLA_ROPE
    q_pad = jnp.zeros((SLOT - qd, ROW_TILE), F32)
    for h in range(MLA_HEADS):
        sl = slice(h * SLOT, (h + 1) * SLOT)
        q_rope = q_t[h * qd + MLA_NOPE:(h + 1) * qd] * cat + qs_t[h * MLA_ROPE:(h + 1) * MLA_ROPE] * sat
        q_slot = jnp.concatenate([q_t[h * qd:h * qd + MLA_NOPE], q_rope, q_pad], axis=0)
        qat_ref[0, sl, :] = (q_slot * scale_a).astype(BF16)
        ka_ref[:, sl] = (kn[:, sl] + kro).astype(BF16)
        vat_ref[0, sl, :] = jnp.concatenate([v_t[h * MLA_V:(h + 1) * MLA_V], ones_rows], axis=0).astype(BF16)

    mk = _dot(xb, win_ref[:, _C_MK:_C_MKS])
    mks = _dot(xb, win_ref[:, _C_MKS:_C_END])
    npair = _MH // LANES
    pair = lambda a, j: a[:, j * LANES:(j + 1) * LANES]
    mk_rot = [pair(mk, j) * cb + pair(mks, j) * sb for j in range(npair)]
    mk_all = jnp.concatenate(mk_rot, axis=1)

    nrot = MOBA_HEADS * MOBA_ROT
    mq_t = _dot_nt(wmt_ref[0:_MH, :], xb)
    mqs_t = _dot_nt(wmt_ref[_MH:_MH + nrot, :], xb)
    mv_t = _dot_nt(wmt_ref[_MH + nrot:2 * _MH + nrot, :], xb)
    head = lambda a, h: a[h * MOBA_HD:(h + 1) * MOBA_HD]
    mq_rot_t = [jnp.concatenate([head(mq_t, h)[0:MOBA_ROT] * cbt + mqs_t[h * MOBA_ROT:(h + 1) * MOBA_ROT] * sbt,
                                 head(mq_t, h)[MOBA_ROT:]], axis=0) for h in range(MOBA_HEADS)]

    nrow = MOBA_HEADS * SUBLANES
    row_i = lax.broadcasted_iota(jnp.int32, (nrow, _MH), 0)
    lane_i = lax.broadcasted_iota(jnp.int32, (nrow, _MH), 1)
    kmean_c = jnp.mean(mk_all, axis=0, keepdims=True)
    put = ((row_i % SUBLANES) == c) & ((lane_i // MOBA_HD) == (row_i // SUBLANES))
    table = kmean_scr[...]
    gate_t = _dot(table, jnp.concatenate(mq_rot_t, axis=0), precision=lax.Precision.HIGHEST)
    kmean_scr[...] = jnp.where(put, jnp.broadcast_to(kmean_c, (nrow, _MH)), table)

    n_idx = lax.broadcasted_iota(jnp.int32, (SUBLANES, ROW_TILE), 0)
    valid = n_idx < c
    scale_b = LOG2E / math.sqrt(MOBA_HD)
    zero_rows = jnp.zeros((SLOT - HEAD_LANES - SUBLANES, ROW_TILE), F32)
    block_onehot = (lane == BIAS_LANE + c).astype(F32)
    for h in range(MOBA_HEADS):
        g = jnp.where(valid, gate_t[h * SUBLANES:(h + 1) * SUBLANES, :], -jnp.inf)
        rank = jnp.zeros((SUBLANES, ROW_TILE), jnp.int32)
        for k in range(1, SUBLANES):
            other = pltpu.roll(g, k, axis=0)
            other_n = pltpu.roll(n_idx, k, axis=0)
            beats = (other > g) | ((other == g) & (other_n < n_idx))
            rank = rank + beats.astype(jnp.int32)
        keep = (valid & (rank < MOBA_TOPK)) | (n_idx == c)
        bias = jnp.where(keep, 0.0, NEG_BIG)
        sl = slice(h * SLOT, (h + 1) * SLOT)
        mqt_ref[0, sl, :] = jnp.concatenate([mq_rot_t[h] * scale_b, bias, zero_rows], axis=0).astype(BF16)
        mvt_ref[0, sl, :] = jnp.concatenate([head(mv_t, h), ones_rows], axis=0).astype(BF16)
        j, hh = divmod(h, 2)
        k_h = mk_rot[j] if hh == 0 else pltpu.roll(mk_rot[j], HEAD_LANES, axis=1)
        mk_ref[:, sl] = jnp.where(head_lanes, k_h, block_onehot).astype(BF16)


def _prep(x2, win, wmt, wqt, wqst, wk, wvt, qg, kvg, tables, B, T):
    N, D = x2.shape
    nt = T // ROW_TILE
    row = lambda b, c: (b * nt + c, 0)
    col = lambda b, c: (0, b * nt + c)
    full = lambda b, c: (0, 0)
    cat, sat, cbt, sbt = tables
    width = MLA_HEADS * SLOT

    def rows(w):
        return pl.BlockSpec((ROW_TILE, w), row)

    def cols(a):
        return pl.BlockSpec((a.shape[0], ROW_TILE), col)

    def whole(a):
        return pl.BlockSpec(a.shape, full)

    rowmajor = jax.ShapeDtypeStruct((N, width), BF16)
    transposed = jax.ShapeDtypeStruct((B * nt, width, ROW_TILE), BF16)
    t_spec = pl.BlockSpec((1, width, ROW_TILE), lambda b, c: (b * nt + c, 0, 0))
    return pl.pallas_call(
        _prep_kernel,
        out_shape=(transposed, rowmajor, transposed, transposed, rowmajor, transposed),
        grid=(B, nt),
        in_specs=[rows(D), whole(win), whole(wmt), whole(wqt), whole(wqst), whole(wk), whole(wvt), whole(qg), whole(kvg),
                  cols(cat), cols(sat), cols(cbt), cols(sbt)],
        out_specs=(t_spec, rows(width), t_spec, t_spec, rows(width), t_spec),
        scratch_shapes=[pltpu.VMEM((MOBA_HEADS * SUBLANES, _MH), F32)],
        compiler_params=pltpu.CompilerParams(dimension_semantics=("arbitrary", "arbitrary"),
                                             vmem_limit_bytes=VMEM_LIMIT_BYTES),
        name="prep",
    )(x2, win, wmt, wqt, wqst, wk, wvt, qg, kvg, cat, sat, cbt, sbt)


def _attn_kernel(qt_ref, k_ref, vt_ref, o_ref, s_scr, acc_scr):
    i = pl.program_id(2)
    nh = ATTN_HEADS_PER_STEP
    slot = lambda h: slice(h * SLOT, (h + 1) * SLOT)
    nblk = i + 1
    nquad = nblk // 4
    pair_end = 4 * nquad + 2 * ((nblk - 4 * nquad) // 2)
    q_pos = i * TQ + lax.broadcasted_iota(jnp.int32, (1, TQ), 1)

    def fold(s):
        out = s[0:SUBLANES]
        for t in range(1, s.shape[0] // SUBLANES):
            out = jnp.maximum(out, s[t * SUBLANES:(t + 1) * SUBLANES])
        return out

    def score_blocks(j, n, mrun):
        j0 = pl.multiple_of(j * TQ, TQ)
        visible = (j0 + lax.broadcasted_iota(jnp.int32, (n * TQ, TQ), 0)) <= q_pos
        out = []
        for h in range(nh):
            s = _dot(k_ref[pl.ds(j0, n * TQ), slot(h)], qt_ref[0, slot(h), :])
            s = jnp.where(visible, s, -jnp.inf)
            s_scr[h, pl.ds(j, n)] = s.reshape(n, TQ, TQ)
            out.append(jnp.maximum(mrun[h], fold(s)))
        return tuple(out)

    mrun = tuple(jnp.full((SUBLANES, TQ), -jnp.inf, F32) for _ in range(nh))
    mrun = lax.fori_loop(0, nquad, lambda jq, m: score_blocks(4 * jq, 4, m), mrun)
    mrun = lax.fori_loop(2 * nquad, pair_end // 2, lambda jj, m: score_blocks(2 * jj, 2, m), mrun)
    mrun = lax.fori_loop(pair_end, nblk, lambda j, m: score_blocks(j, 1, m), mrun)
    ms = [jnp.max(m, axis=0, keepdims=True) for m in mrun]

    def accumulate(j, n, carry):
        for h in range(nh):
            p = jnp.exp2(s_scr[h, pl.ds(j, n)].reshape(n * TQ, TQ) - ms[h]).astype(BF16)
            vt = jnp.concatenate([vt_ref[j + t, slot(h), :] for t in range(n)], axis=1)
            acc_scr[h] += _dot(vt, p)
        return carry

    acc_scr[...] = jnp.zeros_like(acc_scr)
    lax.fori_loop(0, nquad, lambda jq, c: accumulate(4 * jq, 4, c), 0)
    lax.fori_loop(2 * nquad, pair_end // 2, lambda jj, c: accumulate(2 * jj, 2, c), 0)
    lax.fori_loop(pair_end, nblk, lambda j, c: accumulate(j, 1, c), 0)
    outs = [acc_scr[h, 0:HEAD_LANES] / acc_scr[h, ONES_LANE:ONES_LANE + 1] for h in range(nh)]
    for jj in range(nh // 2):
        both = jnp.concatenate([outs[2 * jj], outs[2 * jj + 1]], axis=0)
        o_ref[:, jj * LANES:(jj + 1) * LANES] = both.T.astype(o_ref.dtype)


def _attention(qt, k, vt, B, T, name):
    N = k.shape[0]
    nq = T // TQ
    nh = ATTN_HEADS_PER_STEP
    heads = k.shape[1] // SLOT
    return pl.pallas_call(
        _attn_kernel,
        out_shape=jax.ShapeDtypeStruct((N, heads * HEAD_LANES), BF16),
        grid=(B, heads // nh, nq),
        in_specs=[pl.BlockSpec((1, nh * SLOT, TQ), lambda b, g, i: (b * nq + i, g, 0)),
                  pl.BlockSpec((T, nh * SLOT), lambda b, g, i: (b, g)),
                  pl.BlockSpec((nq, nh * SLOT, TQ), lambda b, g, i: (b, g, 0))],
        out_specs=pl.BlockSpec((TQ, nh * HEAD_LANES), lambda b, g, i: (b * nq + i, g)),
        scratch_shapes=[pltpu.VMEM((nh, nq, TQ, TQ), F32), pltpu.VMEM((nh, SLOT, TQ), F32)],
        compiler_params=pltpu.CompilerParams(dimension_semantics=("arbitrary", "arbitrary", "arbitrary"),
                                             vmem_limit_bytes=VMEM_LIMIT_BYTES),
        name=name,
    )(qt, k, vt)


def _oproj_kernel(alpha, a_ref, m_ref, x_ref, wo_ref, g_ref, b_ref, wr_ref, br_ref,
                  x1_ref, x1t_ref, route_ref, gates_ref, cnt_ref, carry_scr):
    i = pl.program_id(0)

    @pl.when(i == 0)
    def _():
        carry_scr[...] = jnp.zeros_like(carry_scr)

    mix = _dot(jnp.concatenate([a_ref[...], m_ref[...]], axis=1), wo_ref[...])
    x1 = _layer_norm(alpha * x_ref[...] + mix, g_ref[...], b_ref[...])
    x1_ref[...] = x1
    _rows_to_tiles(x1t_ref, x1)

    ne = br_ref.shape[0]
    rows = x1.shape[0]
    x_hi = x1.astype(BF16)
    x_lo = (x1 - x_hi.astype(F32)).astype(BF16)
    both = _dot(x_hi, wr_ref[...])
    logits_rm = both[:, :LANES] + both[:, LANES:] + _dot(x_lo, wr_ref[:, :LANES])
    logits = logits_rm.T[0:ne] + br_ref[...]
    expert = lax.broadcasted_iota(jnp.int32, logits.shape, 0)
    expert_f = expert.astype(F32)
    vals, idxs = [], []
    work = logits
    for _ in range(MOE_TOPK):
        mx = jnp.max(work, axis=0, keepdims=True)
        ix = jnp.min(jnp.where(work == mx, expert_f, float(ne)), axis=0, keepdims=True).astype(jnp.int32)
        vals.append(mx)
        idxs.append(ix)
        work = jnp.where(expert == ix, -jnp.inf, work)
    exps = [jnp.exp(v - vals[0]) for v in vals]
    den = exps[0]
    for e in exps[1:]:
        den = den + e

    onehot = jnp.zeros(logits.shape, F32)
    for ix in idxs:
        onehot = onehot + (expert == ix).astype(F32)
    r = lax.broadcasted_iota(jnp.int32, (rows, rows), 0)
    cidx = lax.broadcasted_iota(jnp.int32, (rows, rows), 1)
    earlier = (r < cidx).astype(BF16)
    carry = carry_scr[:, 0:1]
    before = _dot(onehot.astype(BF16), earlier) + carry
    row8 = lax.broadcasted_iota(jnp.int32, (SUBLANES, rows), 0)
    route = jnp.zeros((SUBLANES, rows), jnp.int32)
    gates = jnp.zeros((SUBLANES, rows), F32)
    for k in range(MOE_TOPK):
        rank = jnp.sum(jnp.where(expert == idxs[k], before, 0.0), axis=0, keepdims=True).astype(jnp.int32)
        route = jnp.where(row8 == k, idxs[k], route)
        route = jnp.where(row8 == MOE_TOPK + k, rank, route)
        gates = jnp.where(row8 == k, exps[k] / den, gates)
    route_ref[...] = route
    gates_ref[...] = jnp.concatenate([gates, jnp.zeros((LANES - SUBLANES, rows), F32)], axis=0).T
    new_carry = carry + jnp.sum(onehot, axis=1, keepdims=True)
    carry_scr[...] = jnp.broadcast_to(new_carry, carry_scr.shape)
    cnt_ref[...] = jnp.broadcast_to(new_carry, cnt_ref.shape)


def _oproj(a, m, x2, wo, g1, b1, wr, br, alpha):
    N, D = x2.shape
    nt = N // OPROJ_TILE
    row = lambda i: (i, 0)
    full = lambda i: (0, 0)
    return pl.pallas_call(
        functools.partial(_oproj_kernel, alpha),
        out_shape=(jax.ShapeDtypeStruct((N, D), F32),
                   jax.ShapeDtypeStruct((N * SUBLANES, LANES), F32),
                   jax.ShapeDtypeStruct((SUBLANES, N), jnp.int32),
                   jax.ShapeDtypeStruct((N, LANES), F32),
                   jax.ShapeDtypeStruct((br.shape[0], LANES), F32)),
        grid=(nt,),
        in_specs=[pl.BlockSpec((OPROJ_TILE, a.shape[1]), row), pl.BlockSpec((OPROJ_TILE, m.shape[1]), row),
                  pl.BlockSpec((OPROJ_TILE, D), row), pl.BlockSpec(wo.shape, full),
                  pl.BlockSpec(g1.shape, full), pl.BlockSpec(b1.shape, full),
                  pl.BlockSpec(wr.shape, full), pl.BlockSpec(br.shape, full)],
        out_specs=(pl.BlockSpec((OPROJ_TILE, D), row), pl.BlockSpec((OPROJ_TILE * SUBLANES, LANES), row),
                   pl.BlockSpec((SUBLANES, OPROJ_TILE), lambda i: (0, i)),
                   pl.BlockSpec((OPROJ_TILE, LANES), row), pl.BlockSpec((br.shape[0], LANES), full)),
        scratch_shapes=[pltpu.VMEM((br.shape[0], LANES), F32)],
        compiler_params=pltpu.CompilerParams(dimension_semantics=("arbitrary",),
                                             vmem_limit_bytes=VMEM_LIMIT_BYTES),
        name="oproj_router",
    )(a, m, x2, wo, g1, b1, wr, br)


def _dispatch_kernel(dest_ref, fill_ref, x1_ref, xs_ref, zero_scr, sem, fill_sem):
    i = pl.program_id(0)
    nfill = xs_ref.shape[0] // HALF_SUB

    def fill_copy(blk, on):
        return pltpu.make_async_copy(zero_scr, xs_ref.at[pl.ds(pl.multiple_of(blk * HALF_SUB, HALF_SUB), HALF_SUB)], on)

    def for_blocks(kind, action):
        def body(blk, _):
            @pl.when(fill_ref[blk] == kind)
            def _():
                action(blk)
            return 0

        lax.fori_loop(0, nfill, body, 0)

    @pl.when(i == 0)
    def _():
        zero_scr[...] = jnp.zeros_like(zero_scr)
        for_blocks(FILL_TAIL, lambda blk: fill_copy(blk, sem).start())
        for_blocks(FILL_FREE, lambda blk: fill_copy(blk, fill_sem).start())
        for_blocks(FILL_TAIL, lambda blk: fill_copy(blk, sem).wait())

    base = i * DISPATCH_TILE
    n_tok = pl.num_programs(0) * DISPATCH_TILE

    def row_copy(t, k):
        d = dest_ref[k * n_tok + base + t]
        return pltpu.make_async_copy(x1_ref.at[pl.ds(pl.multiple_of(t * SUBLANES, SUBLANES), SUBLANES)],
                                     xs_ref.at[pl.ds(pl.multiple_of(d * SUBLANES, SUBLANES), SUBLANES)], sem)

    def start_rows(t, _):
        for k in range(MOE_TOPK):
            row_copy(t, k).start(priority=k % DMA_PRIORITIES)
        return 0

    lax.fori_loop(0, DISPATCH_TILE, start_rows, 0, unroll=ISSUE_UNROLL)
    for k in range(MOE_TOPK):
        pltpu.make_async_copy(x1_ref, xs_ref.at[pl.ds(0, DISPATCH_TILE * SUBLANES)], sem).wait()

    @pl.when(i == pl.num_programs(0) - 1)
    def _():
        for_blocks(FILL_FREE, lambda blk: fill_copy(blk, fill_sem).wait())


def _dispatch(dest, fill, x1t, n_rows):
    return pl.pallas_call(
        _dispatch_kernel,
        out_shape=jax.ShapeDtypeStruct((n_rows * SUBLANES, LANES), F32),
        grid_spec=pltpu.PrefetchScalarGridSpec(
            num_scalar_prefetch=2,
            grid=(x1t.shape[0] // (DISPATCH_TILE * SUBLANES),),
            in_specs=[pl.BlockSpec((DISPATCH_TILE * SUBLANES, LANES), lambda i, d, t: (i, 0))],
            out_specs=pl.BlockSpec(memory_space=pl.ANY),
            scratch_shapes=[pltpu.VMEM((HALF_SUB, LANES), F32), pltpu.SemaphoreType.DMA(()),
                            pltpu.SemaphoreType.DMA(())],
        ),
        compiler_params=pltpu.CompilerParams(dimension_semantics=("arbitrary",),
                                             vmem_limit_bytes=VMEM_LIMIT_BYTES),
        name="dispatch",
    )(dest, fill, x1t)


def _experts_kernel(be_ref, slot_ref, nxt_ref, full_ref, nused_ref, x_ref, wg_hbm, bg_ref, wu_hbm, bu_ref, wd_hbm, bd_ref,
                    y_ref, wg_st, wu_st, wd_st, wg_bf, wu_bf, wd_bf, sems):
    i = pl.program_id(0)
    prev = be_ref[jnp.maximum(i - 1, 0)]
    changed = (i == 0) | (be_ref[i] != prev)
    active = i < nused_ref[0]
    slot = slot_ref[i]

    def weight_copies(expert, s):
        return (pltpu.make_async_copy(wg_hbm.at[expert], wg_st.at[s], sems.at[s, 0]),
                pltpu.make_async_copy(wu_hbm.at[expert], wu_st.at[s], sems.at[s, 1]),
                pltpu.make_async_copy(wd_hbm.at[expert], wd_st.at[s], sems.at[s, 2]))

    @pl.when(i == 0)
    def _():
        for cp in weight_copies(be_ref[0], slot):
            cp.start()

    @pl.when(active & changed)
    def _():
        for cp in weight_copies(be_ref[i], slot):
            cp.wait()

        @pl.when(nxt_ref[i] >= 0)
        def _():
            for cp in weight_copies(nxt_ref[i], 1 - slot):
                cp.start()

        wg_bf[...] = wg_st[slot].astype(BF16)
        wu_bf[...] = wu_st[slot].astype(BF16)
        wd_bf[...] = wd_st[slot].astype(BF16)

    def mlp(nrows):
        rows = pl.ds(0, nrows * SUBLANES)
        xb = _tiles_to_rows(x_ref.at[rows], nrows).astype(BF16)
        g = jnp.minimum(_dot(xb, wg_bf[...]) + bg_ref[0], SWIGLU_LIMIT)
        u = jnp.clip(_dot(xb, wu_bf[...]) + bu_ref[0], -SWIGLU_LIMIT, SWIGLU_LIMIT)
        h = g * (1.0 / (1.0 + jnp.exp(-SWIGLU_ALPHA * g))) * (u + 1.0)
        _rows_to_tiles(y_ref.at[rows], _dot(h.astype(BF16), wd_bf[...]) + bd_ref[0])

    quarters = jnp.where(active, full_ref[i], 0)
    for q in range(1, MOE_GROUP // MOE_QUARTER + 1):
        @pl.when(quarters == q)
        def _():
            mlp(q * MOE_QUARTER)

    for q in range(MOE_GROUP // MOE_QUARTER):
        @pl.when(quarters <= q)
        def _():
            y_ref[pl.ds(q * QUARTER_SUB, QUARTER_SUB), :] = jnp.zeros((QUARTER_SUB, LANES), F32)


def _experts(blk_expert, blk_slot, blk_next, blk_full, n_used, xs, wg, bg, wu, bu, wd, bd):
    E, D, F = wg.shape
    nblk = xs.shape[0] // GROUP_SUB

    def rowmap(i, be, sl, nx, fl, nu):
        return (jnp.minimum(i, nu[0] - 1), 0)

    def bmap(i, be, sl, nx, fl, nu):
        return (be[i], 0, 0)

    hbm = pl.BlockSpec(memory_space=pl.ANY)
    return pl.pallas_call(
        _experts_kernel,
        out_shape=jax.ShapeDtypeStruct(xs.shape, F32),
        grid_spec=pltpu.PrefetchScalarGridSpec(
            num_scalar_prefetch=5,
            grid=(nblk,),
            in_specs=[pl.BlockSpec((GROUP_SUB, LANES), rowmap),
                      hbm, pl.BlockSpec((1, 1, F), bmap),
                      hbm, pl.BlockSpec((1, 1, F), bmap),
                      hbm, pl.BlockSpec((1, 1, D), bmap)],
            out_specs=pl.BlockSpec((GROUP_SUB, LANES), lambda i, be, sl, nx, fl, nu: (i, 0)),
            scratch_shapes=[pltpu.VMEM((2, D, F), F32), pltpu.VMEM((2, D, F), F32), pltpu.VMEM((2, F, D), F32),
                            pltpu.VMEM((D, F), BF16), pltpu.VMEM((D, F), BF16), pltpu.VMEM((F, D), BF16),
                            pltpu.SemaphoreType.DMA((2, 3))],
        ),
        compiler_params=pltpu.CompilerParams(dimension_semantics=("arbitrary",),
                                             vmem_limit_bytes=VMEM_LIMIT_BYTES),
        name="experts",
    )(blk_expert, blk_slot, blk_next, blk_full, n_used, xs, wg, bg, wu, bu, wd, bd)


def _combine_kernel(alpha, dest_ref, gates_ref, x1_ref, g_ref, b_ref, y_ref, o_ref, ybuf, sems):
    i = pl.program_id(0)
    buf = i % 2

    n_tok = pl.num_programs(0) * COMBINE_TILE

    def gather_tile(tile, into):
        base = tile * COMBINE_TILE

        def start_rows(t, _):
            for k in range(MOE_TOPK):
                d = dest_ref[k * n_tok + base + t]
                pltpu.make_async_copy(y_ref.at[pl.ds(pl.multiple_of(d * SUBLANES, SUBLANES), SUBLANES)],
                                      ybuf.at[into, k, pl.ds(pl.multiple_of(t * SUBLANES, SUBLANES), SUBLANES)],
                                      sems.at[into]).start(priority=k % DMA_PRIORITIES)
            return 0

        lax.fori_loop(0, COMBINE_TILE, start_rows, 0, unroll=ISSUE_UNROLL)

    @pl.when(i == 0)
    def _():
        gather_tile(0, 0)

    @pl.when(i + 1 < pl.num_programs(0))
    def _():
        gather_tile(i + 1, 1 - buf)

    for k in range(MOE_TOPK):
        pltpu.make_async_copy(y_ref.at[pl.ds(0, COMBINE_TILE * SUBLANES)], ybuf.at[buf, k], sems.at[buf]).wait()

    gates = gates_ref[...]
    ffn = gates[:, 0:1] * _tiles_to_rows(ybuf.at[buf, 0], COMBINE_TILE)
    for k in range(1, MOE_TOPK):
        ffn = ffn + gates[:, k:k + 1] * _tiles_to_rows(ybuf.at[buf, k], COMBINE_TILE)
    o_ref[...] = _layer_norm(alpha * x1_ref[...] + ffn, g_ref[...], b_ref[...])


def _combine(dest, gates, x1, g2, b2, y_rows, alpha):
    N, D = x1.shape
    row = lambda i, d: (i, 0)
    full = lambda i, d: (0, 0)
    return pl.pallas_call(
        functools.partial(_combine_kernel, alpha),
        out_shape=jax.ShapeDtypeStruct((N, D), F32),
        grid_spec=pltpu.PrefetchScalarGridSpec(
            num_scalar_prefetch=1,
            grid=(N // COMBINE_TILE,),
            in_specs=[pl.BlockSpec((COMBINE_TILE, LANES), row), pl.BlockSpec((COMBINE_TILE, D), row),
                      pl.BlockSpec(g2.shape, full), pl.BlockSpec(b2.shape, full),
                      pl.BlockSpec(memory_space=pl.ANY)],
            out_specs=pl.BlockSpec((COMBINE_TILE, D), row),
            scratch_shapes=[pltpu.VMEM((2, MOE_TOPK, COMBINE_TILE * SUBLANES, LANES), F32),
                            pltpu.SemaphoreType.DMA((2,))],
        ),
        compiler_params=pltpu.CompilerParams(dimension_semantics=("arbitrary",),
                                             vmem_limit_bytes=VMEM_LIMIT_BYTES),
        name="combine",
    )(dest, gates, x1, g2, b2, y_rows)


def _rot_partner(w, half):
    return jnp.concatenate([-w[..., half:2 * half], w[..., :half]], axis=-1)


def _layer_weights(w_in, w_q_b, w_kv_b):
    D = w_in.shape[0]
    o1 = Q_LORA
    o2 = o1 + KV_LORA
    o3 = o2 + MLA_ROPE
    w_ql, w_kvl, w_kr = w_in[:, :o1], w_in[:, o1:o2], w_in[:, o2:o3]
    w_mq, w_mk, w_mv = w_in[:, o3:o3 + _MH], w_in[:, o3 + _MH:o3 + 2 * _MH], w_in[:, o3 + 2 * _MH:]
    zpad = lambda n: jnp.zeros((D, n), w_in.dtype)
    tail = SLOT - MLA_NOPE - MLA_ROPE
    kr_slot = jnp.concatenate([zpad(MLA_NOPE), w_kr, zpad(tail)], axis=1)
    krs_slot = jnp.concatenate([zpad(MLA_NOPE), _rot_partner(w_kr, MLA_ROPE // 2), zpad(tail)], axis=1)

    def moba_partner(w):
        w3 = w.reshape(D, MOBA_HEADS, MOBA_HD)
        part = jnp.concatenate([_rot_partner(w3[..., :MOBA_ROT], MOBA_ROT // 2),
                                jnp.zeros((D, MOBA_HEADS, MOBA_HD - MOBA_ROT), w.dtype)], axis=-1)
        return part.reshape(D, _MH)

    win = jnp.concatenate([w_ql, w_kvl, kr_slot, krs_slot, w_mk, moba_partner(w_mk)], axis=1).astype(BF16)
    mq_partner = _rot_partner(w_mq.reshape(D, MOBA_HEADS, MOBA_HD)[..., :MOBA_ROT], MOBA_ROT // 2)
    wmt = jnp.concatenate([w_mq, mq_partner.reshape(D, MOBA_HEADS * MOBA_ROT), w_mv], axis=1).T.astype(BF16)

    wq3 = w_q_b.reshape(Q_LORA, MLA_HEADS, MLA_NOPE + MLA_ROPE)
    wqs = _rot_partner(wq3[..., MLA_NOPE:], MLA_ROPE // 2).reshape(Q_LORA, MLA_HEADS * MLA_ROPE)
    wkv3 = w_kv_b.reshape(KV_LORA, MLA_HEADS, MLA_NOPE + MLA_V)
    wk = jnp.concatenate([wkv3[..., :MLA_NOPE], jnp.zeros((KV_LORA, MLA_HEADS, SLOT - MLA_NOPE), w_kv_b.dtype)],
                         axis=-1).reshape(KV_LORA, MLA_HEADS * SLOT).astype(BF16)
    wv = wkv3[..., MLA_NOPE:].reshape(KV_LORA, MLA_HEADS * MLA_V)
    return win, wmt, w_q_b.T.astype(BF16), wqs.T.astype(BF16), wk, wv.T.astype(BF16)


def _rope_tables(positions):
    pos = positions.astype(F32).reshape(1, -1)

    def cs(d_rot):
        inv_freq = ROPE_THETA ** (-jnp.arange(0, d_rot, 2, dtype=F32) / d_rot)
        ang = jnp.concatenate([inv_freq, inv_freq]).reshape(d_rot, 1) * pos
        return jnp.cos(ang), jnp.sin(ang)

    return cs(MLA_ROPE) + cs(MOBA_ROT)


def kernel(x, positions, w_in, q_a_norm, w_q_b, kv_a_norm, w_kv_b, w_o, ln1_g, ln1_b, w_router, b_router,
           w_gate, b_gate, w_up, b_up, w_down, b_down, ln2_g, ln2_b):
    B, T, D = x.shape
    depth = w_in.shape[0]
    alpha = (2.0 * depth) ** 0.25
    N = B * T
    assert T % MOBA_BLOCK == 0 and T // MOBA_BLOCK <= SUBLANES and N % OPROJ_TILE == 0
    assert D == SUBLANES * LANES
    n_asg = N * MOE_TOPK
    n_rows = n_asg + N_EXPERTS * MOE_GROUP
    tables = _rope_tables(positions)
    h = x.reshape(N, D)
    for l in range(depth):
        win, wmt, wqt, wqst, wk, wvt = _layer_weights(w_in[l], w_q_b[l], w_kv_b[l])
        qa, ka, va, mq, mk, mv = _prep(h, win, wmt, wqt, wqst, wk, wvt, q_a_norm[l].reshape(1, -1),
                                       kv_a_norm[l].reshape(1, -1), tables, B, T)
        a = _attention(qa, ka, va, B, T, "mla_attention")
        m = _attention(mq, mk, mv, B, T, "moba_attention")
        wr_pad = jnp.concatenate([w_router[l], jnp.zeros((D, LANES - N_EXPERTS), F32)], axis=1)
        wr_hi = wr_pad.astype(BF16)
        wr = jnp.concatenate([wr_hi, (wr_pad - wr_hi.astype(F32)).astype(BF16)], axis=1)
        br = b_router[l].reshape(N_EXPERTS, 1)
        x1, x1t, route, gates, cnt = _oproj(a, m, h, w_o[l].astype(BF16), ln1_g[l].reshape(1, D), ln1_b[l].reshape(1, D),
                                       wr, br, alpha)
        er = jnp.arange(N_EXPERTS, dtype=jnp.int32)
        counts = cnt[:, 0].astype(jnp.int32)
        padded = (counts + MOE_GROUP - 1) // MOE_GROUP * MOE_GROUP
        upto = er[None, :] <= er[:, None]
        pad_end = jnp.sum(jnp.where(upto, padded[None, :], 0), axis=1)
        pad_start = pad_end - padded
        e_idx = route[:MOE_TOPK]
        group_start = jnp.sum(jnp.where(e_idx[..., None] == er, pad_start, 0), axis=-1)
        dest = (group_start + route[MOE_TOPK:2 * MOE_TOPK]).reshape(n_asg)
        n_used = (pad_end[-1:] // MOE_GROUP).astype(jnp.int32)
        nblk = n_rows // MOE_GROUP
        blk_start = jnp.arange(nblk, dtype=jnp.int32) * MOE_GROUP
        blk_expert = jnp.minimum(jnp.sum((blk_start[:, None] >= pad_end[None, :]).astype(jnp.int32), axis=1),
                                 N_EXPERTS - 1)
        of_expert = blk_expert[:, None] == er[None, :]
        pick = lambda per_expert: jnp.sum(jnp.where(of_expert, per_expert[None, :], 0), axis=1)
        nonempty = padded > 0
        group_index = jnp.sum(jnp.where(upto & nonempty[None, :], 1, 0), axis=1) - 1
        later = (er[None, :] > er[:, None]) & nonempty[None, :]
        next_expert = jnp.min(jnp.where(later, er[None, :], N_EXPERTS), axis=1)
        next_expert = jnp.where(next_expert == N_EXPERTS, -1, next_expert)
        blk_slot = pick(group_index) % 2
        blk_next = pick(next_expert)
        real_rows = pick(pad_start + counts) - blk_start
        blk_full = jnp.clip((real_rows + MOE_QUARTER - 1) // MOE_QUARTER, 0, MOE_GROUP // MOE_QUARTER).astype(jnp.int32)
        half_start = jnp.arange(n_rows // MOE_HALF, dtype=jnp.int32) * MOE_HALF
        half_expert = jnp.minimum(jnp.sum((half_start[:, None] >= pad_end[None, :]).astype(jnp.int32), axis=1),
                                  N_EXPERTS - 1)
        half_real_end = jnp.sum(jnp.where(half_expert[:, None] == er[None, :], (pad_start + counts)[None, :], 0), axis=1)
        no_real_rows = (half_start >= pad_end[-1]) | (half_start >= half_real_end)
        fill = jnp.where(no_real_rows, FILL_FREE,
                         jnp.where(half_start + MOE_HALF > half_real_end, FILL_TAIL, 0)).astype(jnp.int32)
        xs = _dispatch(dest, fill, x1t, n_rows)
        y_rows = _experts(blk_expert, blk_slot.astype(jnp.int32), blk_next, blk_full, n_used, xs, w_gate[l], b_gate[l].reshape(N_EXPERTS, 1, -1),
                          w_up[l], b_up[l].reshape(N_EXPERTS, 1, -1), w_down[l], b_down[l].reshape(N_EXPERTS, 1, -1))
        h = _combine(dest, gates, x1, ln2_g[l].reshape(1, D), ln2_b[l].reshape(1, D), y_rows, alpha)
    return h.reshape(B, T, D)
```

```python
import functools
import math

import jax
import jax.numpy as jnp
from jax import lax
from jax.experimental import pallas as pl
from jax.experimental.pallas import tpu as pltpu

ROPE_THETA = 500000.0
MLA_HEADS = 8
MLA_NOPE = 64
MLA_ROPE = 32
MLA_V = 64
Q_LORA = 256
KV_LORA = 128
MOBA_HEADS = 8
MOBA_HD = 64
MOBA_ROT = MOBA_HD // 4
MOBA_BLOCK = 256
MOBA_TOPK = 3
N_EXPERTS = 32
MOE_TOPK = 4
SWIGLU_LIMIT = 7.0
SWIGLU_ALPHA = 1.702
RMS_EPS = 1e-6
LN_EPS = 1e-5

LANES = 128
SUBLANES = 8
VMEM_LIMIT_BYTES = 56 * 1024 * 1024

SLOT = LANES
TQ = MOBA_BLOCK
ROW_TILE = 256
OPROJ_TILE = 512
MOE_GROUP = 512
MOE_HALF = MOE_GROUP // 2
GROUP_SUB = MOE_GROUP * SUBLANES
HALF_SUB = MOE_HALF * SUBLANES
FILL_TAIL, FILL_FREE = 1, 2
DISPATCH_TILE = 512
COMBINE_TILE = 512
ISSUE_UNROLL = 4
DMA_PRIORITIES = 2
WEIGHT_DMA_PRIORITY = 1
HEAD_LANES = 64
ONES_LANE = 64
BIAS_LANE = 64
ATTN_HEADS_PER_STEP = 8
LOG2E = math.log2(math.e)
NEG_BIG = -(2.0 ** 100)

F32 = jnp.float32
BF16 = jnp.bfloat16
NT_DIMS = (((1,), (1,)), ((), ()))


def _dot(a, b, precision=None):
    return jnp.dot(a, b, preferred_element_type=F32, precision=precision)


def _dot_nt(a, b, precision=None):
    return lax.dot_general(a, b, NT_DIMS, preferred_element_type=F32, precision=precision)


def _rows_to_tiles(ref, x):
    rows = x.shape[0]
    for c in range(SUBLANES):
        ref[pl.ds(c, rows, stride=SUBLANES), :] = x[:, c * LANES:(c + 1) * LANES]


def _tiles_to_rows(ref, rows):
    return jnp.concatenate([ref[pl.ds(c, rows, stride=SUBLANES), :] for c in range(SUBLANES)], axis=1)


def _rms(x, g):
    return x * lax.rsqrt(jnp.mean(x * x, axis=-1, keepdims=True) + RMS_EPS) * g


def _layer_norm(x, g, b):
    mu = jnp.mean(x, axis=-1, keepdims=True)
    xc = x - mu
    var = jnp.mean(xc * xc, axis=-1, keepdims=True)
    return xc * lax.rsqrt(var + LN_EPS) * g + b


_C_QL = 0
_C_KVL = _C_QL + Q_LORA
_C_KR = _C_KVL + KV_LORA
_C_KRS = _C_KR + SLOT
_C_MK = _C_KRS + SLOT
_MH = MOBA_HEADS * MOBA_HD
_C_MKS = _C_MK + _MH
_C_END = _C_MKS + _MH


def _prep_kernel(x_ref, win_ref, wmt_ref, wqt_ref, wqst_ref, wk_ref, wvt_ref, qg_ref, kvg_ref,
                 cat_ref, sat_ref, cbt_ref, sbt_ref,
                 qat_ref, ka_ref, vat_ref, mqt_ref, mk_ref, mvt_ref, kmean_scr):
    c = pl.program_id(1)

    @pl.when(c == 0)
    def _():
        kmean_scr[...] = jnp.zeros_like(kmean_scr)

    xb = x_ref[...].astype(BF16)
    cat = cat_ref[...]
    sat = sat_ref[...]
    cbt = cbt_ref[...]
    sbt = sbt_ref[...]
    const = lambda v, n: jnp.full((n, ROW_TILE), v, F32)
    tail = SLOT - MLA_NOPE - MLA_ROPE
    rest = MOBA_HD - MOBA_ROT
    ca = jnp.concatenate([const(1.0, MLA_NOPE), cat, const(0.0, tail)], axis=0).T
    sa = jnp.concatenate([const(0.0, MLA_NOPE), sat, const(0.0, tail)], axis=0).T
    cb = jnp.concatenate([cbt, const(1.0, rest), cbt, const(1.0, rest)], axis=0).T
    sb = jnp.concatenate([sbt, const(0.0, rest), sbt, const(0.0, rest)], axis=0).T
    lane = lax.broadcasted_iota(jnp.int32, (ROW_TILE, SLOT), 1)
    head_lanes = lane < HEAD_LANES
    ones_rows = (lax.broadcasted_iota(jnp.int32, (SLOT - HEAD_LANES, ROW_TILE), 0) == ONES_LANE - HEAD_LANES).astype(F32)

    ql = _dot(xb, win_ref[:, _C_QL:_C_KVL])
    kvl = _dot(xb, win_ref[:, _C_KVL:_C_KR])
    kr = _dot(xb, win_ref[:, _C_KR:_C_KRS])
    krs = _dot(xb, win_ref[:, _C_KRS:_C_MK])
    qn = _rms(ql, qg_ref[...]).astype(BF16)
    kvn = _rms(kvl, kvg_ref[...]).astype(BF16)
    q_t = _dot_nt(wqt_ref[...], qn)
    qs_t = _dot_nt(wqst_ref[...], qn)
    kn = _dot(kvn, wk_ref[...])
    v_t = _dot_nt(wvt_ref[...], kvn)
    scale_a = LOG2E / math.sqrt(MLA_NOPE + MLA_ROPE)
    kro = kr * ca + krs * sa
    qd = MLA_NOPE + MLA_ROPE
    q_pad = jnp.zeros((SLOT - qd, ROW_TILE), F32)
    for h in range(MLA_HEADS):
        sl = slice(h * SLOT, (h + 1) * SLOT)
        q_rope = q_t[h * qd + MLA_NOPE:(h + 1) * qd] * cat + qs_t[h * MLA_ROPE:(h + 1) * MLA_ROPE] * sat
        q_slot = jnp.concatenate([q_t[h * qd:h * qd + MLA_NOPE], q_rope, q_pad], axis=0)
        qat_ref[0, sl, :] = (q_slot * scale_a).astype(BF16)
        ka_ref[:, sl] = (kn[:, sl] + kro).astype(BF16)
        vat_ref[0, sl, :] = jnp.concatenate([v_t[h * MLA_V:(h + 1) * MLA_V], ones_rows], axis=0).astype(BF16)

    mk = _dot(xb, win_ref[:, _C_MK:_C_MKS])
    mks = _dot(xb, win_ref[:, _C_MKS:_C_END])
    npair = _MH // LANES
    pair = lambda a, j: a[:, j * LANES:(j + 1) * LANES]
    mk_rot = [pair(mk, j) * cb + pair(mks, j) * sb for j in range(npair)]
    mk_all = jnp.concatenate(mk_rot, axis=1)

    nrot = MOBA_HEADS * MOBA_ROT
    mq_t = _dot_nt(wmt_ref[0:_MH, :], xb)
    mqs_t = _dot_nt(wmt_ref[_MH:_MH + nrot, :], xb)
    mv_t = _dot_nt(wmt_ref[_MH + nrot:2 * _MH + nrot, :], xb)
    head = lambda a, h: a[h * MOBA_HD:(h + 1) * MOBA_HD]
    mq_rot_t = [jnp.concatenate([head(mq_t, h)[0:MOBA_ROT] * cbt + mqs_t[h * MOBA_ROT:(h + 1) * MOBA_ROT] * sbt,
                                 head(mq_t, h)[MOBA_ROT:]], axis=0) for h in range(MOBA_HEADS)]

    nrow = MOBA_HEADS * SUBLANES
    row_i = lax.broadcasted_iota(jnp.int32, (nrow, _MH), 0)
    lane_i = lax.broadcasted_iota(jnp.int32, (nrow, _MH), 1)
    kmean_c = jnp.mean(mk_all, axis=0, keepdims=True)
    put = ((row_i % SUBLANES) == c) & ((lane_i // MOBA_HD) == (row_i // SUBLANES))
    table = kmean_scr[...]
    gate_t = _dot(table, jnp.concatenate(mq_rot_t, axis=0), precision=lax.Precision.HIGHEST)
    kmean_scr[...] = jnp.where(put, jnp.broadcast_to(kmean_c, (nrow, _MH)), table)

    n_idx = lax.broadcasted_iota(jnp.int32, (SUBLANES, ROW_TILE), 0)
    valid = n_idx < c
    scale_b = LOG2E / math.sqrt(MOBA_HD)
    zero_rows = jnp.zeros((SLOT - HEAD_LANES - SUBLANES, ROW_TILE), F32)
    block_onehot = (lane == BIAS_LANE + c).astype(F32)
    for h in range(MOBA_HEADS):
        g = jnp.where(valid, gate_t[h * SUBLANES:(h + 1) * SUBLANES, :], -jnp.inf)
        rank = jnp.zeros((SUBLANES, ROW_TILE), jnp.int32)
        for k in range(1, SUBLANES):
            other = pltpu.roll(g, k, axis=0)
            other_n = pltpu.roll(n_idx, k, axis=0)
            beats = (other > g) | ((other == g) & (other_n < n_idx))
            rank = rank + beats.astype(jnp.int32)
        keep = (valid & (rank < MOBA_TOPK)) | (n_idx == c)
        bias = jnp.where(keep, 0.0, NEG_BIG)
        sl = slice(h * SLOT, (h + 1) * SLOT)
        mqt_ref[0, sl, :] = jnp.concatenate([mq_rot_t[h] * scale_b, bias, zero_rows], axis=0).astype(BF16)
        mvt_ref[0, sl, :] = jnp.concatenate([head(mv_t, h), ones_rows], axis=0).astype(BF16)
        j, hh = divmod(h, 2)
        k_h = mk_rot[j] if hh == 0 else pltpu.roll(mk_rot[j], HEAD_LANES, axis=1)
        mk_ref[:, sl] = jnp.where(head_lanes, k_h, block_onehot).astype(BF16)


def _prep(x2, win, wmt, wqt, wqst, wk, wvt, qg, kvg, tables, B, T):
    N, D = x2.shape
    nt = T // ROW_TILE
    row = lambda b, c: (b * nt + c, 0)
    col = lambda b, c: (0, b * nt + c)
    full = lambda b, c: (0, 0)
    cat, sat, cbt, sbt = tables
    width = MLA_HEADS * SLOT

    def rows(w):
        return pl.BlockSpec((ROW_TILE, w), row)

    def cols(a):
        return pl.BlockSpec((a.shape[0], ROW_TILE), col)

    def whole(a):
        return pl.BlockSpec(a.shape, full)

    rowmajor = jax.ShapeDtypeStruct((N, width), BF16)
    transposed = jax.ShapeDtypeStruct((B * nt, width, ROW_TILE), BF16)
    t_spec = pl.BlockSpec((1, width, ROW_TILE), lambda b, c: (b * nt + c, 0, 0))
    return pl.pallas_call(
        _prep_kernel,
        out_shape=(transposed, rowmajor, transposed, transposed, rowmajor, transposed),
        grid=(B, nt),
        in_specs=[rows(D), whole(win), whole(wmt), whole(wqt), whole(wqst), whole(wk), whole(wvt), whole(qg), whole(kvg),
                  cols(cat), cols(sat), cols(cbt), cols(sbt)],
        out_specs=(t_spec, rows(width), t_spec, t_spec, rows(width), t_spec),
        scratch_shapes=[pltpu.VMEM((MOBA_HEADS * SUBLANES, _MH), F32)],
        compiler_params=pltpu.CompilerParams(dimension_semantics=("arbitrary", "arbitrary"),
                                             vmem_limit_bytes=VMEM_LIMIT_BYTES),
        name="prep",
    )(x2, win, wmt, wqt, wqst, wk, wvt, qg, kvg, cat, sat, cbt, sbt)


def _attn_kernel(qt_ref, k_ref, vt_ref, o_ref, s_scr, acc_scr):
    i = pl.program_id(2)
    nh = ATTN_HEADS_PER_STEP
    slot = lambda h: slice(h * SLOT, (h + 1) * SLOT)
    nblk = i + 1
    nquad = nblk // 4
    pair_end = 4 * nquad + 2 * ((nblk - 4 * nquad) // 2)
    q_pos = i * TQ + lax.broadcasted_iota(jnp.int32, (1, TQ), 1)

    def fold(s):
        out = s[0:SUBLANES]
        for t in range(1, s.shape[0] // SUBLANES):
            out = jnp.maximum(out, s[t * SUBLANES:(t + 1) * SUBLANES])
        return out

    def score_blocks(j, n, mrun):
        j0 = pl.multiple_of(j * TQ, TQ)
        visible = (j0 + lax.broadcasted_iota(jnp.int32, (n * TQ, TQ), 0)) <= q_pos
        out = []
        for h in range(nh):
            s = _dot(k_ref[pl.ds(j0, n * TQ), slot(h)], qt_ref[0, slot(h), :])
            s = jnp.where(visible, s, -jnp.inf)
            s_scr[h, pl.ds(j, n)] = s.reshape(n, TQ, TQ)
            out.append(jnp.maximum(mrun[h], fold(s)))
        return tuple(out)

    mrun = tuple(jnp.full((SUBLANES, TQ), -jnp.inf, F32) for _ in range(nh))
    mrun = lax.fori_loop(0, nquad, lambda jq, m: score_blocks(4 * jq, 4, m), mrun)
    mrun = lax.fori_loop(2 * nquad, pair_end // 2, lambda jj, m: score_blocks(2 * jj, 2, m), mrun)
    mrun = lax.fori_loop(pair_end, nblk, lambda j, m: score_blocks(j, 1, m), mrun)
    ms = [jnp.max(m, axis=0, keepdims=True) for m in mrun]

    def accumulate(j, n, carry):
        for h in range(nh):
            p = jnp.exp2(s_scr[h, pl.ds(j, n)].reshape(n * TQ, TQ) - ms[h]).astype(BF16)
            vt = jnp.concatenate([vt_ref[j + t, slot(h), :] for t in range(n)], axis=1)
            acc_scr[h] += _dot(vt, p)
        return carry

    acc_scr[...] = jnp.zeros_like(acc_scr)
    lax.fori_loop(0, nquad, lambda jq, c: accumulate(4 * jq, 4, c), 0)
    lax.fori_loop(2 * nquad, pair_end // 2, lambda jj, c: accumulate(2 * jj, 2, c), 0)
    lax.fori_loop(pair_end, nblk, lambda j, c: accumulate(j, 1, c), 0)
    outs = [acc_scr[h, 0:HEAD_LANES] / acc_scr[h, ONES_LANE:ONES_LANE + 1] for h in range(nh)]
    for jj in range(nh // 2):
        both = jnp.concatenate([outs[2 * jj], outs[2 * jj + 1]], axis=0)
        o_ref[:, jj * LANES:(jj + 1) * LANES] = both.T.astype(o_ref.dtype)


def _attention(qt, k, vt, B, T, name):
    N = k.shape[0]
    nq = T // TQ
    nh = ATTN_HEADS_PER_STEP
    heads = k.shape[1] // SLOT
    return pl.pallas_call(
        _attn_kernel,
        out_shape=jax.ShapeDtypeStruct((N, heads * HEAD_LANES), BF16),
        grid=(B, heads // nh, nq),
        in_specs=[pl.BlockSpec((1, nh * SLOT, TQ), lambda b, g, i: (b * nq + i, g, 0)),
                  pl.BlockSpec((T, nh * SLOT), lambda b, g, i: (b, g)),
                  pl.BlockSpec((nq, nh * SLOT, TQ), lambda b, g, i: (b, g, 0))],
        out_specs=pl.BlockSpec((TQ, nh * HEAD_LANES), lambda b, g, i: (b * nq + i, g)),
        scratch_shapes=[pltpu.VMEM((nh, nq, TQ, TQ), F32), pltpu.VMEM((nh, SLOT, TQ), F32)],
        compiler_params=pltpu.CompilerParams(dimension_semantics=("arbitrary", "arbitrary", "arbitrary"),
                                             vmem_limit_bytes=VMEM_LIMIT_BYTES),
        name=name,
    )(qt, k, vt)


def _oproj_kernel(alpha, a_ref, m_ref, x_ref, wo_ref, g_ref, b_ref, wr_ref, br_ref,
                  x1_ref, x1t_ref, route_ref, gates_ref, cnt_ref, carry_scr):
    i = pl.program_id(0)

    @pl.when(i == 0)
    def _():
        carry_scr[...] = jnp.zeros_like(carry_scr)

    mix = _dot(jnp.concatenate([a_ref[...], m_ref[...]], axis=1), wo_ref[...])
    x1 = _layer_norm(alpha * x_ref[...] + mix, g_ref[...], b_ref[...])
    x1_ref[...] = x1
    _rows_to_tiles(x1t_ref, x1)

    ne = br_ref.shape[0]
    rows = x1.shape[0]
    x_hi = x1.astype(BF16)
    x_lo = (x1 - x_hi.astype(F32)).astype(BF16)
    both = _dot(x_hi, wr_ref[...])
    logits_rm = both[:, :LANES] + both[:, LANES:] + _dot(x_lo, wr_ref[:, :LANES])
    logits = logits_rm.T[0:ne] + br_ref[...]
    expert = lax.broadcasted_iota(jnp.int32, logits.shape, 0)
    expert_f = expert.astype(F32)
    vals, idxs = [], []
    work = logits
    for _ in range(MOE_TOPK):
        mx = jnp.max(work, axis=0, keepdims=True)
        ix = jnp.min(jnp.where(work == mx, expert_f, float(ne)), axis=0, keepdims=True).astype(jnp.int32)
        vals.append(mx)
        idxs.append(ix)
        work = jnp.where(expert == ix, -jnp.inf, work)
    exps = [jnp.exp(v - vals[0]) for v in vals]
    den = exps[0]
    for e in exps[1:]:
        den = den + e

    onehot = jnp.zeros(logits.shape, F32)
    for ix in idxs:
        onehot = onehot + (expert == ix).astype(F32)
    r = lax.broadcasted_iota(jnp.int32, (rows, rows), 0)
    cidx = lax.broadcasted_iota(jnp.int32, (rows, rows), 1)
    earlier = (r < cidx).astype(BF16)
    carry = carry_scr[:, 0:1]
    before = _dot(onehot.astype(BF16), earlier) + carry
    row8 = lax.broadcasted_iota(jnp.int32, (SUBLANES, rows), 0)
    route = jnp.zeros((SUBLANES, rows), jnp.int32)
    gates = jnp.zeros((SUBLANES, rows), F32)
    for k in range(MOE_TOPK):
        rank = jnp.sum(jnp.where(expert == idxs[k], before, 0.0), axis=0, keepdims=True).astype(jnp.int32)
        route = jnp.where(row8 == k, idxs[k], route)
        route = jnp.where(row8 == MOE_TOPK + k, rank, route)
        gates = jnp.where(row8 == k, exps[k] / den, gates)
    route_ref[...] = route
    gates_ref[...] = jnp.concatenate([gates, jnp.zeros((LANES - SUBLANES, rows), F32)], axis=0).T
    new_carry = carry + jnp.sum(onehot, axis=1, keepdims=True)
    carry_scr[...] = jnp.broadcast_to(new_carry, carry_scr.shape)
    cnt_ref[...] = jnp.broadcast_to(new_carry, cnt_ref.shape)


def _oproj(a, m, x2, wo, g1, b1, wr, br, alpha):
    N, D = x2.shape
    nt = N // OPROJ_TILE
    row = lambda i: (i, 0)
    full = lambda i: (0, 0)
    return pl.pallas_call(
        functools.partial(_oproj_kernel, alpha),
        out_shape=(jax.ShapeDtypeStruct((N, D), F32),
                   jax.ShapeDtypeStruct((N * SUBLANES, LANES), F32),
                   jax.ShapeDtypeStruct((SUBLANES, N), jnp.int32),
                   jax.ShapeDtypeStruct((N, LANES), F32),
                   jax.ShapeDtypeStruct((br.shape[0], LANES), F32)),
        grid=(nt,),
        in_specs=[pl.BlockSpec((OPROJ_TILE, a.shape[1]), row), pl.BlockSpec((OPROJ_TILE, m.shape[1]), row),
                  pl.BlockSpec((OPROJ_TILE, D), row), pl.BlockSpec(wo.shape, full),
                  pl.BlockSpec(g1.shape, full), pl.BlockSpec(b1.shape, full),
                  pl.BlockSpec(wr.shape, full), pl.BlockSpec(br.shape, full)],
        out_specs=(pl.BlockSpec((OPROJ_TILE, D), row), pl.BlockSpec((OPROJ_TILE * SUBLANES, LANES), row),
                   pl.BlockSpec((SUBLANES, OPROJ_TILE), lambda i: (0, i)),
                   pl.BlockSpec((OPROJ_TILE, LANES), row), pl.BlockSpec((br.shape[0], LANES), full)),
        scratch_shapes=[pltpu.VMEM((br.shape[0], LANES), F32)],
        compiler_params=pltpu.CompilerParams(dimension_semantics=("arbitrary",),
                                             vmem_limit_bytes=VMEM_LIMIT_BYTES),
        name="oproj_router",
    )(a, m, x2, wo, g1, b1, wr, br)


def _dispatch_kernel(dest_ref, fill_ref, x1_ref, xs_ref, zero_scr, sem, fill_sem):
    i = pl.program_id(0)
    nfill = xs_ref.shape[0] // HALF_SUB

    def fill_copy(blk, on):
        return pltpu.make_async_copy(zero_scr, xs_ref.at[pl.ds(pl.multiple_of(blk * HALF_SUB, HALF_SUB), HALF_SUB)], on)

    def for_blocks(kind, action):
        def body(blk, _):
            @pl.when(fill_ref[blk] == kind)
            def _():
                action(blk)
            return 0

        lax.fori_loop(0, nfill, body, 0)

    @pl.when(i == 0)
    def _():
        zero_scr[...] = jnp.zeros_like(zero_scr)
        for_blocks(FILL_TAIL, lambda blk: fill_copy(blk, sem).start())
        for_blocks(FILL_FREE, lambda blk: fill_copy(blk, fill_sem).start())
        for_blocks(FILL_TAIL, lambda blk: fill_copy(blk, sem).wait())

    base = i * DISPATCH_TILE
    n_tok = pl.num_programs(0) * DISPATCH_TILE

    def row_copy(t, k):
        d = dest_ref[k * n_tok + base + t]
        return pltpu.make_async_copy(x1_ref.at[pl.ds(pl.multiple_of(t * SUBLANES, SUBLANES), SUBLANES)],
                                     xs_ref.at[pl.ds(pl.multiple_of(d * SUBLANES, SUBLANES), SUBLANES)], sem)

    def start_rows(t, _):
        for k in range(MOE_TOPK):
            row_copy(t, k).start(priority=k % DMA_PRIORITIES)
        return 0

    lax.fori_loop(0, DISPATCH_TILE, start_rows, 0, unroll=ISSUE_UNROLL)
    for k in range(MOE_TOPK):
        pltpu.make_async_copy(x1_ref, xs_ref.at[pl.ds(0, DISPATCH_TILE * SUBLANES)], sem).wait()

    @pl.when(i == pl.num_programs(0) - 1)
    def _():
        for_blocks(FILL_FREE, lambda blk: fill_copy(blk, fill_sem).wait())


def _dispatch(dest, fill, x1t, n_rows):
    return pl.pallas_call(
        _dispatch_kernel,
        out_shape=jax.ShapeDtypeStruct((n_rows * SUBLANES, LANES), F32),
        grid_spec=pltpu.PrefetchScalarGridSpec(
            num_scalar_prefetch=2,
            grid=(x1t.shape[0] // (DISPATCH_TILE * SUBLANES),),
            in_specs=[pl.BlockSpec((DISPATCH_TILE * SUBLANES, LANES), lambda i, d, t: (i, 0))],
            out_specs=pl.BlockSpec(memory_space=pl.ANY),
            scratch_shapes=[pltpu.VMEM((HALF_SUB, LANES), F32), pltpu.SemaphoreType.DMA(()),
                            pltpu.SemaphoreType.DMA(())],
        ),
        compiler_params=pltpu.CompilerParams(dimension_semantics=("arbitrary",),
                                             vmem_limit_bytes=VMEM_LIMIT_BYTES),
        name="dispatch",
    )(dest, fill, x1t)


def _experts_kernel(be_ref, slot_ref, nxt_ref, full_ref, nused_ref, x_ref, wg_hbm, bg_ref, wu_hbm, bu_ref, wd_hbm, bd_ref,
                    y_ref, wg_st, wu_st, wd_st, wg_bf, wu_bf, wd_bf, sems):
    i = pl.program_id(0)
    prev = be_ref[jnp.maximum(i - 1, 0)]
    changed = (i == 0) | (be_ref[i] != prev)
    active = i < nused_ref[0]
    slot = slot_ref[i]

    def weight_copies(expert, s):
        return (pltpu.make_async_copy(wg_hbm.at[expert], wg_st.at[s], sems.at[s, 0]),
                pltpu.make_async_copy(wu_hbm.at[expert], wu_st.at[s], sems.at[s, 1]),
                pltpu.make_async_copy(wd_hbm.at[expert], wd_st.at[s], sems.at[s, 2]))

    @pl.when(i == 0)
    def _():
        for cp in weight_copies(be_ref[0], slot):
            cp.start()

    @pl.when(active & changed)
    def _():
        for cp in weight_copies(be_ref[i], slot):
            cp.wait()

        @pl.when(nxt_ref[i] >= 0)
        def _():
            for cp in weight_copies(nxt_ref[i], 1 - slot):
                cp.start(priority=WEIGHT_DMA_PRIORITY)

        wg_bf[...] = wg_st[slot].astype(BF16)
        wu_bf[...] = wu_st[slot].astype(BF16)
        wd_bf[...] = wd_st[slot].astype(BF16)

    def mlp(nrows):
        rows = pl.ds(0, nrows * SUBLANES)
        xb = _tiles_to_rows(x_ref.at[rows], nrows).astype(BF16)
        g = jnp.minimum(_dot(xb, wg_bf[...]) + bg_ref[0], SWIGLU_LIMIT)
        u = jnp.clip(_dot(xb, wu_bf[...]) + bu_ref[0], -SWIGLU_LIMIT, SWIGLU_LIMIT)
        h = g * (1.0 / (1.0 + jnp.exp(-SWIGLU_ALPHA * g))) * (u + 1.0)
        _rows_to_tiles(y_ref.at[rows], _dot(h.astype(BF16), wd_bf[...]) + bd_ref[0])

    whole = active & (full_ref[i] > 0)

    @pl.when(whole)
    def _():
        mlp(MOE_GROUP)

    @pl.when(active & jnp.logical_not(whole))
    def _():
        mlp(MOE_HALF)

    @pl.when(jnp.logical_not(active))
    def _():
        y_ref[pl.ds(0, HALF_SUB), :] = jnp.zeros((HALF_SUB, LANES), F32)

    @pl.when(jnp.logical_not(whole))
    def _():
        y_ref[pl.ds(HALF_SUB, HALF_SUB), :] = jnp.zeros((HALF_SUB, LANES), F32)


def _experts(blk_expert, blk_slot, blk_next, blk_full, n_used, xs, wg, bg, wu, bu, wd, bd):
    E, D, F = wg.shape
    nblk = xs.shape[0] // GROUP_SUB

    def rowmap(i, be, sl, nx, fl, nu):
        return (jnp.minimum(i, nu[0] - 1), 0)

    def bmap(i, be, sl, nx, fl, nu):
        return (be[i], 0, 0)

    hbm = pl.BlockSpec(memory_space=pl.ANY)
    return pl.pallas_call(
        _experts_kernel,
        out_shape=jax.ShapeDtypeStruct(xs.shape, F32),
        grid_spec=pltpu.PrefetchScalarGridSpec(
            num_scalar_prefetch=5,
            grid=(nblk,),
            in_specs=[pl.BlockSpec((GROUP_SUB, LANES), rowmap),
                      hbm, pl.BlockSpec((1, 1, F), bmap),
                      hbm, pl.BlockSpec((1, 1, F), bmap),
                      hbm, pl.BlockSpec((1, 1, D), bmap)],
            out_specs=pl.BlockSpec((GROUP_SUB, LANES), lambda i, be, sl, nx, fl, nu: (i, 0)),
            scratch_shapes=[pltpu.VMEM((2, D, F), F32), pltpu.VMEM((2, D, F), F32), pltpu.VMEM((2, F, D), F32),
                            pltpu.VMEM((D, F), BF16), pltpu.VMEM((D, F), BF16), pltpu.VMEM((F, D), BF16),
                            pltpu.SemaphoreType.DMA((2, 3))],
        ),
        compiler_params=pltpu.CompilerParams(dimension_semantics=("arbitrary",),
                                             vmem_limit_bytes=VMEM_LIMIT_BYTES),
        name="experts",
    )(blk_expert, blk_slot, blk_next, blk_full, n_used, xs, wg, bg, wu, bu, wd, bd)


def _combine_kernel(alpha, dest_ref, gates_ref, x1_ref, g_ref, b_ref, y_ref, o_ref, ybuf, sems):
    i = pl.program_id(0)
    buf = i % 2

    n_tok = pl.num_programs(0) * COMBINE_TILE

    def gather_tile(tile, into):
        base = tile * COMBINE_TILE

        def start_rows(t, _):
            for k in range(MOE_TOPK):
                d = dest_ref[k * n_tok + base + t]
                pltpu.make_async_copy(y_ref.at[pl.ds(pl.multiple_of(d * SUBLANES, SUBLANES), SUBLANES)],
                                      ybuf.at[into, k, pl.ds(pl.multiple_of(t * SUBLANES, SUBLANES), SUBLANES)],
                                      sems.at[into]).start(priority=k % DMA_PRIORITIES)
            return 0

        lax.fori_loop(0, COMBINE_TILE, start_rows, 0, unroll=ISSUE_UNROLL)

    @pl.when(i == 0)
    def _():
        gather_tile(0, 0)

    @pl.when(i + 1 < pl.num_programs(0))
    def _():
        gather_tile(i + 1, 1 - buf)

    for k in range(MOE_TOPK):
        pltpu.make_async_copy(y_ref.at[pl.ds(0, COMBINE_TILE * SUBLANES)], ybuf.at[buf, k], sems.at[buf]).wait()

    gates = gates_ref[...]
    ffn = gates[:, 0:1] * _tiles_to_rows(ybuf.at[buf, 0], COMBINE_TILE)
    for k in range(1, MOE_TOPK):
        ffn = ffn + gates[:, k:k + 1] * _tiles_to_rows(ybuf.at[buf, k], COMBINE_TILE)
    o_ref[...] = _layer_norm(alpha * x1_ref[...] + ffn, g_ref[...], b_ref[...])


def _combine(dest, gates, x1, g2, b2, y_rows, alpha):
    N, D = x1.shape
    row = lambda i, d: (i, 0)
    full = lambda i, d: (0, 0)
    return pl.pallas_call(
        functools.partial(_combine_kernel, alpha),
        out_shape=jax.ShapeDtypeStruct((N, D), F32),
        grid_spec=pltpu.PrefetchScalarGridSpec(
            num_scalar_prefetch=1,
            grid=(N // COMBINE_TILE,),
            in_specs=[pl.BlockSpec((COMBINE_TILE, LANES), row), pl.BlockSpec((COMBINE_TILE, D), row),
                      pl.BlockSpec(g2.shape, full), pl.BlockSpec(b2.shape, full),
                      pl.BlockSpec(memory_space=pl.ANY)],
            out_specs=pl.BlockSpec((COMBINE_TILE, D), row),
            scratch_shapes=[pltpu.VMEM((2, MOE_TOPK, COMBINE_TILE * SUBLANES, LANES), F32),
                            pltpu.SemaphoreType.DMA((2,))],
        ),
        compiler_params=pltpu.CompilerParams(dimension_semantics=("arbitrary",),
                                             vmem_limit_bytes=VMEM_LIMIT_BYTES),
        name="combine",
    )(dest, gates, x1, g2, b2, y_rows)


def _rot_partner(w, half):
    return jnp.concatenate([-w[..., half:2 * half], w[..., :half]], axis=-1)


def _layer_weights(w_in, w_q_b, w_kv_b):
    D = w_in.shape[0]
    o1 = Q_LORA
    o2 = o1 + KV_LORA
    o3 = o2 + MLA_ROPE
    w_ql, w_kvl, w_kr = w_in[:, :o1], w_in[:, o1:o2], w_in[:, o2:o3]
    w_mq, w_mk, w_mv = w_in[:, o3:o3 + _MH], w_in[:, o3 + _MH:o3 + 2 * _MH], w_in[:, o3 + 2 * _MH:]
    zpad = lambda n: jnp.zeros((D, n), w_in.dtype)
    tail = SLOT - MLA_NOPE - MLA_ROPE
    kr_slot = jnp.concatenate([zpad(MLA_NOPE), w_kr, zpad(tail)], axis=1)
    krs_slot = jnp.concatenate([zpad(MLA_NOPE), _rot_partner(w_kr, MLA_ROPE // 2), zpad(tail)], axis=1)

    def moba_partner(w):
        w3 = w.reshape(D, MOBA_HEADS, MOBA_HD)
        part = jnp.concatenate([_rot_partner(w3[..., :MOBA_ROT], MOBA_ROT // 2),
                                jnp.zeros((D, MOBA_HEADS, MOBA_HD - MOBA_ROT), w.dtype)], axis=-1)
        return part.reshape(D, _MH)

    win = jnp.concatenate([w_ql, w_kvl, kr_slot, krs_slot, w_mk, moba_partner(w_mk)], axis=1).astype(BF16)
    mq_partner = _rot_partner(w_mq.reshape(D, MOBA_HEADS, MOBA_HD)[..., :MOBA_ROT], MOBA_ROT // 2)
    wmt = jnp.concatenate([w_mq, mq_partner.reshape(D, MOBA_HEADS * MOBA_ROT), w_mv], axis=1).T.astype(BF16)

    wq3 = w_q_b.reshape(Q_LORA, MLA_HEADS, MLA_NOPE + MLA_ROPE)
    wqs = _rot_partner(wq3[..., MLA_NOPE:], MLA_ROPE // 2).reshape(Q_LORA, MLA_HEADS * MLA_ROPE)
    wkv3 = w_kv_b.reshape(KV_LORA, MLA_HEADS, MLA_NOPE + MLA_V)
    wk = jnp.concatenate([wkv3[..., :MLA_NOPE], jnp.zeros((KV_LORA, MLA_HEADS, SLOT - MLA_NOPE), w_kv_b.dtype)],
                         axis=-1).reshape(KV_LORA, MLA_HEADS * SLOT).astype(BF16)
    wv = wkv3[..., MLA_NOPE:].reshape(KV_LORA, MLA_HEADS * MLA_V)
    return win, wmt, w_q_b.T.astype(BF16), wqs.T.astype(BF16), wk, wv.T.astype(BF16)


def _rope_tables(positions):
    pos = positions.astype(F32).reshape(1, -1)

    def cs(d_rot):
        inv_freq = ROPE_THETA ** (-jnp.arange(0, d_rot, 2, dtype=F32) / d_rot)
        ang = jnp.concatenate([inv_freq, inv_freq]).reshape(d_rot, 1) * pos
        return jnp.cos(ang), jnp.sin(ang)

    return cs(MLA_ROPE) + cs(MOBA_ROT)


def kernel(x, positions, w_in, q_a_norm, w_q_b, kv_a_norm, w_kv_b, w_o, ln1_g, ln1_b, w_router, b_router,
           w_gate, b_gate, w_up, b_up, w_down, b_down, ln2_g, ln2_b):
    B, T, D = x.shape
    depth = w_in.shape[0]
    alpha = (2.0 * depth) ** 0.25
    N = B * T
    assert T % MOBA_BLOCK == 0 and T // MOBA_BLOCK <= SUBLANES and N % OPROJ_TILE == 0
    assert D == SUBLANES * LANES
    n_asg = N * MOE_TOPK
    n_rows = n_asg + N_EXPERTS * MOE_GROUP
    tables = _rope_tables(positions)
    h = x.reshape(N, D)
    for l in range(depth):
        win, wmt, wqt, wqst, wk, wvt = _layer_weights(w_in[l], w_q_b[l], w_kv_b[l])
        qa, ka, va, mq, mk, mv = _prep(h, win, wmt, wqt, wqst, wk, wvt, q_a_norm[l].reshape(1, -1),
                                       kv_a_norm[l].reshape(1, -1), tables, B, T)
        a = _attention(qa, ka, va, B, T, "mla_attention")
        m = _attention(mq, mk, mv, B, T, "moba_attention")
        wr_pad = jnp.concatenate([w_router[l], jnp.zeros((D, LANES - N_EXPERTS), F32)], axis=1)
        wr_hi = wr_pad.astype(BF16)
        wr = jnp.concatenate([wr_hi, (wr_pad - wr_hi.astype(F32)).astype(BF16)], axis=1)
        br = b_router[l].reshape(N_EXPERTS, 1)
        x1, x1t, route, gates, cnt = _oproj(a, m, h, w_o[l].astype(BF16), ln1_g[l].reshape(1, D), ln1_b[l].reshape(1, D),
                                       wr, br, alpha)
        er = jnp.arange(N_EXPERTS, dtype=jnp.int32)
        counts = cnt[:, 0].astype(jnp.int32)
        padded = (counts + MOE_GROUP - 1) // MOE_GROUP * MOE_GROUP
        upto = er[None, :] <= er[:, None]
        pad_end = jnp.sum(jnp.where(upto, padded[None, :], 0), axis=1)
        pad_start = pad_end - padded
        e_idx = route[:MOE_TOPK]
        group_start = jnp.sum(jnp.where(e_idx[..., None] == er, pad_start, 0), axis=-1)
        dest = (group_start + route[MOE_TOPK:2 * MOE_TOPK]).reshape(n_asg)
        n_used = (pad_end[-1:] // MOE_GROUP).astype(jnp.int32)
        nblk = n_rows // MOE_GROUP
        blk_start = jnp.arange(nblk, dtype=jnp.int32) * MOE_GROUP
        blk_expert = jnp.minimum(jnp.sum((blk_start[:, None] >= pad_end[None, :]).astype(jnp.int32), axis=1),
                                 N_EXPERTS - 1)
        of_expert = blk_expert[:, None] == er[None, :]
        pick = lambda per_expert: jnp.sum(jnp.where(of_expert, per_expert[None, :], 0), axis=1)
        nonempty = padded > 0
        group_index = jnp.sum(jnp.where(upto & nonempty[None, :], 1, 0), axis=1) - 1
        later = (er[None, :] > er[:, None]) & nonempty[None, :]
        next_expert = jnp.min(jnp.where(later, er[None, :], N_EXPERTS), axis=1)
        next_expert = jnp.where(next_expert == N_EXPERTS, -1, next_expert)
        blk_slot = pick(group_index) % 2
        blk_next = pick(next_expert)
        blk_full = (pick(pad_start + counts) > blk_start + MOE_HALF).astype(jnp.int32)
        half_start = jnp.arange(n_rows // MOE_HALF, dtype=jnp.int32) * MOE_HALF
        half_expert = jnp.minimum(jnp.sum((half_start[:, None] >= pad_end[None, :]).astype(jnp.int32), axis=1),
                                  N_EXPERTS - 1)
        half_real_end = jnp.sum(jnp.where(half_expert[:, None] == er[None, :], (pad_start + counts)[None, :], 0), axis=1)
        no_real_rows = (half_start >= pad_end[-1]) | (half_start >= half_real_end)
        fill = jnp.where(no_real_rows, FILL_FREE,
                         jnp.where(half_start + MOE_HALF > half_real_end, FILL_TAIL, 0)).astype(jnp.int32)
        xs = _dispatch(dest, fill, x1t, n_rows)
        y_rows = _experts(blk_expert, blk_slot.astype(jnp.int32), blk_next, blk_full, n_used, xs, w_gate[l], b_gate[l].reshape(N_EXPERTS, 1, -1),
                          w_up[l], b_up[l].reshape(N_EXPERTS, 1, -1), w_down[l], b_down[l].reshape(N_EXPERTS, 1, -1))
        h = _combine(dest, gates, x1, ln2_g[l].reshape(1, D), ln2_b[l].reshape(1, D), y_rows, alpha)
    return h.reshape(B, T, D)
```

```python
import functools
import math

import jax
import jax.numpy as jnp
from jax import lax
from jax.experimental import pallas as pl
from jax.experimental.pallas import tpu as pltpu

ROPE_THETA = 500000.0
MLA_HEADS = 8
MLA_NOPE = 64
MLA_ROPE = 32
MLA_V = 64
Q_LORA = 256
KV_LORA = 128
MOBA_HEADS = 8
MOBA_HD = 64
MOBA_ROT = MOBA_HD // 4
MOBA_BLOCK = 256
MOBA_TOPK = 3
N_EXPERTS = 32
MOE_TOPK = 4
SWIGLU_LIMIT = 7.0
SWIGLU_ALPHA = 1.702
RMS_EPS = 1e-6
LN_EPS = 1e-5

LANES = 128
SUBLANES = 8
VMEM_LIMIT_BYTES = 56 * 1024 * 1024

SLOT = LANES
TQ = MOBA_BLOCK
ROW_TILE = 256
OPROJ_TILE = 512
MOE_GROUP = 1024
MOE_PARTS = 4
MOE_PART = MOE_GROUP // MOE_PARTS
GROUP_SUB = MOE_GROUP * SUBLANES
PART_SUB = MOE_PART * SUBLANES
FILL_TAIL, FILL_FREE = 1, 2
DISPATCH_TILE = 512
COMBINE_TILE = 512
ISSUE_UNROLL = 4
DMA_PRIORITIES = 2
WEIGHT_DMA_PRIORITY = 1
HEAD_LANES = 64
ONES_LANE = 64
BIAS_LANE = 64
ATTN_HEADS_PER_STEP = 8
LOG2E = math.log2(math.e)
NEG_BIG = -(2.0 ** 100)

F32 = jnp.float32
BF16 = jnp.bfloat16
NT_DIMS = (((1,), (1,)), ((), ()))


def _dot(a, b, precision=None):
    return jnp.dot(a, b, preferred_element_type=F32, precision=precision)


def _dot_nt(a, b, precision=None):
    return lax.dot_general(a, b, NT_DIMS, preferred_element_type=F32, precision=precision)


def _rows_to_tiles(ref, x):
    rows = x.shape[0]
    for c in range(SUBLANES):
        ref[pl.ds(c, rows, stride=SUBLANES), :] = x[:, c * LANES:(c + 1) * LANES]


def _tiles_to_rows(ref, rows):
    return jnp.concatenate([ref[pl.ds(c, rows, stride=SUBLANES), :] for c in range(SUBLANES)], axis=1)


def _rms(x, g):
    return x * lax.rsqrt(jnp.mean(x * x, axis=-1, keepdims=True) + RMS_EPS) * g


def _layer_norm(x, g, b):
    mu = jnp.mean(x, axis=-1, keepdims=True)
    xc = x - mu
    var = jnp.mean(xc * xc, axis=-1, keepdims=True)
    return xc * lax.rsqrt(var + LN_EPS) * g + b


_C_QL = 0
_C_KVL = _C_QL + Q_LORA
_C_KR = _C_KVL + KV_LORA
_C_KRS = _C_KR + SLOT
_C_MK = _C_KRS + SLOT
_MH = MOBA_HEADS * MOBA_HD
_C_MKS = _C_MK + _MH
_C_END = _C_MKS + _MH


def _prep_kernel(x_ref, win_ref, wmt_ref, wqt_ref, wqst_ref, wk_ref, wvt_ref, qg_ref, kvg_ref,
                 cat_ref, sat_ref, cbt_ref, sbt_ref,
                 qat_ref, ka_ref, vat_ref, mqt_ref, mk_ref, mvt_ref, kmean_scr):
    c = pl.program_id(1)

    @pl.when(c == 0)
    def _():
        kmean_scr[...] = jnp.zeros_like(kmean_scr)

    xb = x_ref[...].astype(BF16)
    cat = cat_ref[...]
    sat = sat_ref[...]
    cbt = cbt_ref[...]
    sbt = sbt_ref[...]
    const = lambda v, n: jnp.full((n, ROW_TILE), v, F32)
    tail = SLOT - MLA_NOPE - MLA_ROPE
    rest = MOBA_HD - MOBA_ROT
    ca = jnp.concatenate([const(1.0, MLA_NOPE), cat, const(0.0, tail)], axis=0).T
    sa = jnp.concatenate([const(0.0, MLA_NOPE), sat, const(0.0, tail)], axis=0).T
    cb = jnp.concatenate([cbt, const(1.0, rest), cbt, const(1.0, rest)], axis=0).T
    sb = jnp.concatenate([sbt, const(0.0, rest), sbt, const(0.0, rest)], axis=0).T
    lane = lax.broadcasted_iota(jnp.int32, (ROW_TILE, SLOT), 1)
    head_lanes = lane < HEAD_LANES
    ones_rows = (lax.broadcasted_iota(jnp.int32, (SLOT - HEAD_LANES, ROW_TILE), 0) == ONES_LANE - HEAD_LANES).astype(F32)

    ql = _dot(xb, win_ref[:, _C_QL:_C_KVL])
    kvl = _dot(xb, win_ref[:, _C_KVL:_C_KR])
    kr = _dot(xb, win_ref[:, _C_KR:_C_KRS])
    krs = _dot(xb, win_ref[:, _C_KRS:_C_MK])
    qn = _rms(ql, qg_ref[...]).astype(BF16)
    kvn = _rms(kvl, kvg_ref[...]).astype(BF16)
    q_t = _dot_nt(wqt_ref[...], qn)
    qs_t = _dot_nt(wqst_ref[...], qn)
    kn = _dot(kvn, wk_ref[...])
    v_t = _dot_nt(wvt_ref[...], kvn)
    scale_a = LOG2E / math.sqrt(MLA_NOPE + MLA_ROPE)
    kro = kr * ca + krs * sa
    qd = MLA_NOPE + MLA_ROPE
    q_pad = jnp.zeros((SLOT - qd, ROW_TILE), F32)
    for h in range(MLA_HEADS):
        sl = slice(h * SLOT, (h + 1) * SLOT)
        q_rope = q_t[h * qd + MLA_NOPE:(h + 1) * qd] * cat + qs_t[h * MLA_ROPE:(h + 1) * MLA_ROPE] * sat
        q_slot = jnp.concatenate([q_t[h * qd:h * qd + MLA_NOPE], q_rope, q_pad], axis=0)
        qat_ref[0, sl, :] = (q_slot * scale_a).astype(BF16)
        ka_ref[:, sl] = (kn[:, sl] + kro).astype(BF16)
        vat_ref[0, sl, :] = jnp.concatenate([v_t[h * MLA_V:(h + 1) * MLA_V], ones_rows], axis=0).astype(BF16)

    mk = _dot(xb, win_ref[:, _C_MK:_C_MKS])
    mks = _dot(xb, win_ref[:, _C_MKS:_C_END])
    npair = _MH // LANES
    pair = lambda a, j: a[:, j * LANES:(j + 1) * LANES]
    mk_rot = [pair(mk, j) * cb + pair(mks, j) * sb for j in range(npair)]
    mk_all = jnp.concatenate(mk_rot, axis=1)

    nrot = MOBA_HEADS * MOBA_ROT
    mq_t = _dot_nt(wmt_ref[0:_MH, :], xb)
    mqs_t = _dot_nt(wmt_ref[_MH:_MH + nrot, :], xb)
    mv_t = _dot_nt(wmt_ref[_MH + nrot:2 * _MH + nrot, :], xb)
    head = lambda a, h: a[h * MOBA_HD:(h + 1) * MOBA_HD]
    mq_rot_t = [jnp.concatenate([head(mq_t, h)[0:MOBA_ROT] * cbt + mqs_t[h * MOBA_ROT:(h + 1) * MOBA_ROT] * sbt,
                                 head(mq_t, h)[MOBA_ROT:]], axis=0) for h in range(MOBA_HEADS)]

    nrow = MOBA_HEADS * SUBLANES
    row_i = lax.broadcasted_iota(jnp.int32, (nrow, _MH), 0)
    lane_i = lax.broadcasted_iota(jnp.int32, (nrow, _MH), 1)
    kmean_c = jnp.mean(mk_all, axis=0, keepdims=True)
    put = ((row_i % SUBLANES) == c) & ((lane_i // MOBA_HD) == (row_i // SUBLANES))
    table = kmean_scr[...]
    gate_t = _dot(table, jnp.concatenate(mq_rot_t, axis=0), precision=lax.Precision.HIGHEST)
    kmean_scr[...] = jnp.where(put, jnp.broadcast_to(kmean_c, (nrow, _MH)), table)

    n_idx = lax.broadcasted_iota(jnp.int32, (SUBLANES, ROW_TILE), 0)
    valid = n_idx < c
    scale_b = LOG2E / math.sqrt(MOBA_HD)
    zero_rows = jnp.zeros((SLOT - HEAD_LANES - SUBLANES, ROW_TILE), F32)
    block_onehot = (lane == BIAS_LANE + c).astype(F32)
    for h in range(MOBA_HEADS):
        g = jnp.where(valid, gate_t[h * SUBLANES:(h + 1) * SUBLANES, :], -jnp.inf)
        rank = jnp.zeros((SUBLANES, ROW_TILE), jnp.int32)
        for k in range(1, SUBLANES):
            other = pltpu.roll(g, k, axis=0)
            other_n = pltpu.roll(n_idx, k, axis=0)
            beats = (other > g) | ((other == g) & (other_n < n_idx))
            rank = rank + beats.astype(jnp.int32)
        keep = (valid & (rank < MOBA_TOPK)) | (n_idx == c)
        bias = jnp.where(keep, 0.0, NEG_BIG)
        sl = slice(h * SLOT, (h + 1) * SLOT)
        mqt_ref[0, sl, :] = jnp.concatenate([mq_rot_t[h] * scale_b, bias, zero_rows], axis=0).astype(BF16)
        mvt_ref[0, sl, :] = jnp.concatenate([head(mv_t, h), ones_rows], axis=0).astype(BF16)
        j, hh = divmod(h, 2)
        k_h = mk_rot[j] if hh == 0 else pltpu.roll(mk_rot[j], HEAD_LANES, axis=1)
        mk_ref[:, sl] = jnp.where(head_lanes, k_h, block_onehot).astype(BF16)


def _prep(x2, win, wmt, wqt, wqst, wk, wvt, qg, kvg, tables, B, T):
    N, D = x2.shape
    nt = T // ROW_TILE
    row = lambda b, c: (b * nt + c, 0)
    col = lambda b, c: (0, b * nt + c)
    full = lambda b, c: (0, 0)
    cat, sat, cbt, sbt = tables
    width = MLA_HEADS * SLOT

    def rows(w):
        return pl.BlockSpec((ROW_TILE, w), row)

    def cols(a):
        return pl.BlockSpec((a.shape[0], ROW_TILE), col)

    def whole(a):
        return pl.BlockSpec(a.shape, full)

    rowmajor = jax.ShapeDtypeStruct((N, width), BF16)
    transposed = jax.ShapeDtypeStruct((B * nt, width, ROW_TILE), BF16)
    t_spec = pl.BlockSpec((1, width, ROW_TILE), lambda b, c: (b * nt + c, 0, 0))
    return pl.pallas_call(
        _prep_kernel,
        out_shape=(transposed, rowmajor, transposed, transposed, rowmajor, transposed),
        grid=(B, nt),
        in_specs=[rows(D), whole(win), whole(wmt), whole(wqt), whole(wqst), whole(wk), whole(wvt), whole(qg), whole(kvg),
                  cols(cat), cols(sat), cols(cbt), cols(sbt)],
        out_specs=(t_spec, rows(width), t_spec, t_spec, rows(width), t_spec),
        scratch_shapes=[pltpu.VMEM((MOBA_HEADS * SUBLANES, _MH), F32)],
        compiler_params=pltpu.CompilerParams(dimension_semantics=("arbitrary", "arbitrary"),
                                             vmem_limit_bytes=VMEM_LIMIT_BYTES),
        name="prep",
    )(x2, win, wmt, wqt, wqst, wk, wvt, qg, kvg, cat, sat, cbt, sbt)


def _attn_kernel(qt_ref, k_ref, vt_ref, o_ref, s_scr, acc_scr):
    i = pl.program_id(2)
    nh = ATTN_HEADS_PER_STEP
    slot = lambda h: slice(h * SLOT, (h + 1) * SLOT)
    nblk = i + 1
    nquad = nblk // 4
    pair_end = 4 * nquad + 2 * ((nblk - 4 * nquad) // 2)
    q_pos = i * TQ + lax.broadcasted_iota(jnp.int32, (1, TQ), 1)

    def fold(s):
        out = s[0:SUBLANES]
        for t in range(1, s.shape[0] // SUBLANES):
            out = jnp.maximum(out, s[t * SUBLANES:(t + 1) * SUBLANES])
        return out

    def score_blocks(j, n, mrun):
        j0 = pl.multiple_of(j * TQ, TQ)
        visible = (j0 + lax.broadcasted_iota(jnp.int32, (n * TQ, TQ), 0)) <= q_pos
        out = []
        for h in range(nh):
            s = _dot(k_ref[pl.ds(j0, n * TQ), slot(h)], qt_ref[0, slot(h), :])
            s = jnp.where(visible, s, -jnp.inf)
            s_scr[h, pl.ds(j, n)] = s.reshape(n, TQ, TQ)
            out.append(jnp.maximum(mrun[h], fold(s)))
        return tuple(out)

    mrun = tuple(jnp.full((SUBLANES, TQ), -jnp.inf, F32) for _ in range(nh))
    mrun = lax.fori_loop(0, nquad, lambda jq, m: score_blocks(4 * jq, 4, m), mrun)
    mrun = lax.fori_loop(2 * nquad, pair_end // 2, lambda jj, m: score_blocks(2 * jj, 2, m), mrun)
    mrun = lax.fori_loop(pair_end, nblk, lambda j, m: score_blocks(j, 1, m), mrun)
    ms = [jnp.max(m, axis=0, keepdims=True) for m in mrun]

    def accumulate(j, n, carry):
        for h in range(nh):
            p = jnp.exp2(s_scr[h, pl.ds(j, n)].reshape(n * TQ, TQ) - ms[h]).astype(BF16)
            vt = jnp.concatenate([vt_ref[j + t, slot(h), :] for t in range(n)], axis=1)
            acc_scr[h] += _dot(vt, p)
        return carry

    acc_scr[...] = jnp.zeros_like(acc_scr)
    lax.fori_loop(0, nquad, lambda jq, c: accumulate(4 * jq, 4, c), 0)
    lax.fori_loop(2 * nquad, pair_end // 2, lambda jj, c: accumulate(2 * jj, 2, c), 0)
    lax.fori_loop(pair_end, nblk, lambda j, c: accumulate(j, 1, c), 0)
    outs = [acc_scr[h, 0:HEAD_LANES] / acc_scr[h, ONES_LANE:ONES_LANE + 1] for h in range(nh)]
    for jj in range(nh // 2):
        both = jnp.concatenate([outs[2 * jj], outs[2 * jj + 1]], axis=0)
        o_ref[:, jj * LANES:(jj + 1) * LANES] = both.T.astype(o_ref.dtype)


def _attention(qt, k, vt, B, T, name):
    N = k.shape[0]
    nq = T // TQ
    nh = ATTN_HEADS_PER_STEP
    heads = k.shape[1] // SLOT
    return pl.pallas_call(
        _attn_kernel,
        out_shape=jax.ShapeDtypeStruct((N, heads * HEAD_LANES), BF16),
        grid=(B, heads // nh, nq),
        in_specs=[pl.BlockSpec((1, nh * SLOT, TQ), lambda b, g, i: (b * nq + i, g, 0)),
                  pl.BlockSpec((T, nh * SLOT), lambda b, g, i: (b, g)),
                  pl.BlockSpec((nq, nh * SLOT, TQ), lambda b, g, i: (b, g, 0))],
        out_specs=pl.BlockSpec((TQ, nh * HEAD_LANES), lambda b, g, i: (b * nq + i, g)),
        scratch_shapes=[pltpu.VMEM((nh, nq, TQ, TQ), F32), pltpu.VMEM((nh, SLOT, TQ), F32)],
        compiler_params=pltpu.CompilerParams(dimension_semantics=("arbitrary", "arbitrary", "arbitrary"),
                                             vmem_limit_bytes=VMEM_LIMIT_BYTES),
        name=name,
    )(qt, k, vt)


def _oproj_kernel(alpha, a_ref, m_ref, x_ref, wo_ref, g_ref, b_ref, wr_ref, br_ref,
                  x1_ref, x1t_ref, route_ref, gates_ref, cnt_ref, carry_scr):
    i = pl.program_id(0)

    @pl.when(i == 0)
    def _():
        carry_scr[...] = jnp.zeros_like(carry_scr)

    mix = _dot(jnp.concatenate([a_ref[...], m_ref[...]], axis=1), wo_ref[...])
    x1 = _layer_norm(alpha * x_ref[...] + mix, g_ref[...], b_ref[...])
    x1_ref[...] = x1
    _rows_to_tiles(x1t_ref, x1)

    ne = br_ref.shape[0]
    rows = x1.shape[0]
    x_hi = x1.astype(BF16)
    x_lo = (x1 - x_hi.astype(F32)).astype(BF16)
    both = _dot(x_hi, wr_ref[...])
    logits_rm = both[:, :LANES] + both[:, LANES:] + _dot(x_lo, wr_ref[:, :LANES])
    logits = logits_rm.T[0:ne] + br_ref[...]
    expert = lax.broadcasted_iota(jnp.int32, logits.shape, 0)
    expert_f = expert.astype(F32)
    vals, idxs = [], []
    work = logits
    for _ in range(MOE_TOPK):
        mx = jnp.max(work, axis=0, keepdims=True)
        ix = jnp.min(jnp.where(work == mx, expert_f, float(ne)), axis=0, keepdims=True).astype(jnp.int32)
        vals.append(mx)
        idxs.append(ix)
        work = jnp.where(expert == ix, -jnp.inf, work)
    exps = [jnp.exp(v - vals[0]) for v in vals]
    den = exps[0]
    for e in exps[1:]:
        den = den + e

    onehot = jnp.zeros(logits.shape, F32)
    for ix in idxs:
        onehot = onehot + (expert == ix).astype(F32)
    r = lax.broadcasted_iota(jnp.int32, (rows, rows), 0)
    cidx = lax.broadcasted_iota(jnp.int32, (rows, rows), 1)
    earlier = (r < cidx).astype(BF16)
    carry = carry_scr[:, 0:1]
    before = _dot(onehot.astype(BF16), earlier) + carry
    row8 = lax.broadcasted_iota(jnp.int32, (SUBLANES, rows), 0)
    route = jnp.zeros((SUBLANES, rows), jnp.int32)
    gates = jnp.zeros((SUBLANES, rows), F32)
    for k in range(MOE_TOPK):
        rank = jnp.sum(jnp.where(expert == idxs[k], before, 0.0), axis=0, keepdims=True).astype(jnp.int32)
        route = jnp.where(row8 == k, idxs[k], route)
        route = jnp.where(row8 == MOE_TOPK + k, rank, route)
        gates = jnp.where(row8 == k, exps[k] / den, gates)
    route_ref[...] = route
    gates_ref[...] = jnp.concatenate([gates, jnp.zeros((LANES - SUBLANES, rows), F32)], axis=0).T
    new_carry = carry + jnp.sum(onehot, axis=1, keepdims=True)
    carry_scr[...] = jnp.broadcast_to(new_carry, carry_scr.shape)
    cnt_ref[...] = jnp.broadcast_to(new_carry, cnt_ref.shape)


def _oproj(a, m, x2, wo, g1, b1, wr, br, alpha):
    N, D = x2.shape
    nt = N // OPROJ_TILE
    row = lambda i: (i, 0)
    full = lambda i: (0, 0)
    return pl.pallas_call(
        functools.partial(_oproj_kernel, alpha),
        out_shape=(jax.ShapeDtypeStruct((N, D), F32),
                   jax.ShapeDtypeStruct((N * SUBLANES, LANES), F32),
                   jax.ShapeDtypeStruct((SUBLANES, N), jnp.int32),
                   jax.ShapeDtypeStruct((N, LANES), F32),
                   jax.ShapeDtypeStruct((br.shape[0], LANES), F32)),
        grid=(nt,),
        in_specs=[pl.BlockSpec((OPROJ_TILE, a.shape[1]), row), pl.BlockSpec((OPROJ_TILE, m.shape[1]), row),
                  pl.BlockSpec((OPROJ_TILE, D), row), pl.BlockSpec(wo.shape, full),
                  pl.BlockSpec(g1.shape, full), pl.BlockSpec(b1.shape, full),
                  pl.BlockSpec(wr.shape, full), pl.BlockSpec(br.shape, full)],
        out_specs=(pl.BlockSpec((OPROJ_TILE, D), row), pl.BlockSpec((OPROJ_TILE * SUBLANES, LANES), row),
                   pl.BlockSpec((SUBLANES, OPROJ_TILE), lambda i: (0, i)),
                   pl.BlockSpec((OPROJ_TILE, LANES), row), pl.BlockSpec((br.shape[0], LANES), full)),
        scratch_shapes=[pltpu.VMEM((br.shape[0], LANES), F32)],
        compiler_params=pltpu.CompilerParams(dimension_semantics=("arbitrary",),
                                             vmem_limit_bytes=VMEM_LIMIT_BYTES),
        name="oproj_router",
    )(a, m, x2, wo, g1, b1, wr, br)


def _dispatch_kernel(dest_ref, fill_ref, x1_ref, xs_ref, zero_scr, sem, fill_sem):
    i = pl.program_id(0)
    nfill = xs_ref.shape[0] // PART_SUB

    def fill_copy(blk, on):
        return pltpu.make_async_copy(zero_scr, xs_ref.at[pl.ds(pl.multiple_of(blk * PART_SUB, PART_SUB), PART_SUB)], on)

    def for_blocks(kind, action):
        def body(blk, _):
            @pl.when(fill_ref[blk] == kind)
            def _():
                action(blk)
            return 0

        lax.fori_loop(0, nfill, body, 0)

    @pl.when(i == 0)
    def _():
        zero_scr[...] = jnp.zeros_like(zero_scr)
        for_blocks(FILL_TAIL, lambda blk: fill_copy(blk, sem).start())
        for_blocks(FILL_FREE, lambda blk: fill_copy(blk, fill_sem).start())
        for_blocks(FILL_TAIL, lambda blk: fill_copy(blk, sem).wait())

    base = i * DISPATCH_TILE
    n_tok = pl.num_programs(0) * DISPATCH_TILE

    def row_copy(t, k):
        d = dest_ref[k * n_tok + base + t]
        return pltpu.make_async_copy(x1_ref.at[pl.ds(pl.multiple_of(t * SUBLANES, SUBLANES), SUBLANES)],
                                     xs_ref.at[pl.ds(pl.multiple_of(d * SUBLANES, SUBLANES), SUBLANES)], sem)

    def start_rows(t, _):
        for k in range(MOE_TOPK):
            row_copy(t, k).start(priority=k % DMA_PRIORITIES)
        return 0

    lax.fori_loop(0, DISPATCH_TILE, start_rows, 0, unroll=ISSUE_UNROLL)
    for k in range(MOE_TOPK):
        pltpu.make_async_copy(x1_ref, xs_ref.at[pl.ds(0, DISPATCH_TILE * SUBLANES)], sem).wait()

    @pl.when(i == pl.num_programs(0) - 1)
    def _():
        for_blocks(FILL_FREE, lambda blk: fill_copy(blk, fill_sem).wait())


def _dispatch(dest, fill, x1t, n_rows):
    return pl.pallas_call(
        _dispatch_kernel,
        out_shape=jax.ShapeDtypeStruct((n_rows * SUBLANES, LANES), F32),
        grid_spec=pltpu.PrefetchScalarGridSpec(
            num_scalar_prefetch=2,
            grid=(x1t.shape[0] // (DISPATCH_TILE * SUBLANES),),
            in_specs=[pl.BlockSpec((DISPATCH_TILE * SUBLANES, LANES), lambda i, d, t: (i, 0))],
            out_specs=pl.BlockSpec(memory_space=pl.ANY),
            scratch_shapes=[pltpu.VMEM((PART_SUB, LANES), F32), pltpu.SemaphoreType.DMA(()),
                            pltpu.SemaphoreType.DMA(())],
        ),
        compiler_params=pltpu.CompilerParams(dimension_semantics=("arbitrary",),
                                             vmem_limit_bytes=VMEM_LIMIT_BYTES),
        name="dispatch",
    )(dest, fill, x1t)


def _experts_kernel(be_ref, slot_ref, nxt_ref, full_ref, nused_ref, x_ref, wg_hbm, bg_ref, wu_hbm, bu_ref, wd_hbm, bd_ref,
                    y_ref, wg_st, wu_st, wd_st, wg_bf, wu_bf, wd_bf, sems):
    i = pl.program_id(0)
    prev = be_ref[jnp.maximum(i - 1, 0)]
    changed = (i == 0) | (be_ref[i] != prev)
    active = i < nused_ref[0]
    slot = slot_ref[i]

    def weight_copies(expert, s):
        return (pltpu.make_async_copy(wg_hbm.at[expert], wg_st.at[s], sems.at[s, 0]),
                pltpu.make_async_copy(wu_hbm.at[expert], wu_st.at[s], sems.at[s, 1]),
                pltpu.make_async_copy(wd_hbm.at[expert], wd_st.at[s], sems.at[s, 2]))

    @pl.when(i == 0)
    def _():
        for cp in weight_copies(be_ref[0], slot):
            cp.start()

    @pl.when(active & changed)
    def _():
        for cp in weight_copies(be_ref[i], slot):
            cp.wait()

        @pl.when(nxt_ref[i] >= 0)
        def _():
            for cp in weight_copies(nxt_ref[i], 1 - slot):
                cp.start(priority=WEIGHT_DMA_PRIORITY)

        wg_bf[...] = wg_st[slot].astype(BF16)
        wu_bf[...] = wu_st[slot].astype(BF16)
        wd_bf[...] = wd_st[slot].astype(BF16)

    def mlp(first_part, nparts):
        rows = pl.ds(first_part * PART_SUB, nparts * PART_SUB)
        xb = _tiles_to_rows(x_ref.at[rows], nparts * MOE_PART).astype(BF16)
        g = jnp.minimum(_dot(xb, wg_bf[...]) + bg_ref[0], SWIGLU_LIMIT)
        u = jnp.clip(_dot(xb, wu_bf[...]) + bu_ref[0], -SWIGLU_LIMIT, SWIGLU_LIMIT)
        h = g * (1.0 / (1.0 + jnp.exp(-SWIGLU_ALPHA * g))) * (u + 1.0)
        _rows_to_tiles(y_ref.at[rows], _dot(h.astype(BF16), wd_bf[...]) + bd_ref[0])

    parts = jnp.where(active, full_ref[i], 0)
    for half in range(MOE_PARTS // 2):
        lo = 2 * half

        @pl.when(parts >= lo + 2)
        def _():
            mlp(lo, 2)

        @pl.when(parts == lo + 1)
        def _():
            mlp(lo, 1)

    for part in range(MOE_PARTS):
        @pl.when(parts <= part)
        def _():
            y_ref[pl.ds(part * PART_SUB, PART_SUB), :] = jnp.zeros((PART_SUB, LANES), F32)


def _experts(blk_expert, blk_slot, blk_next, blk_full, n_used, xs, wg, bg, wu, bu, wd, bd):
    E, D, F = wg.shape
    nblk = xs.shape[0] // GROUP_SUB

    def rowmap(i, be, sl, nx, fl, nu):
        return (jnp.minimum(i, nu[0] - 1), 0)

    def bmap(i, be, sl, nx, fl, nu):
        return (be[i], 0, 0)

    hbm = pl.BlockSpec(memory_space=pl.ANY)
    return pl.pallas_call(
        _experts_kernel,
        out_shape=jax.ShapeDtypeStruct(xs.shape, F32),
        grid_spec=pltpu.PrefetchScalarGridSpec(
            num_scalar_prefetch=5,
            grid=(nblk,),
            in_specs=[pl.BlockSpec((GROUP_SUB, LANES), rowmap),
                      hbm, pl.BlockSpec((1, 1, F), bmap),
                      hbm, pl.BlockSpec((1, 1, F), bmap),
                      hbm, pl.BlockSpec((1, 1, D), bmap)],
            out_specs=pl.BlockSpec((GROUP_SUB, LANES), lambda i, be, sl, nx, fl, nu: (i, 0)),
            scratch_shapes=[pltpu.VMEM((2, D, F), F32), pltpu.VMEM((2, D, F), F32), pltpu.VMEM((2, F, D), F32),
                            pltpu.VMEM((D, F), BF16), pltpu.VMEM((D, F), BF16), pltpu.VMEM((F, D), BF16),
                            pltpu.SemaphoreType.DMA((2, 3))],
        ),
        compiler_params=pltpu.CompilerParams(dimension_semantics=("arbitrary",),
                                             vmem_limit_bytes=VMEM_LIMIT_BYTES),
        name="experts",
    )(blk_expert, blk_slot, blk_next, blk_full, n_used, xs, wg, bg, wu, bu, wd, bd)


def _combine_kernel(alpha, dest_ref, gates_ref, x1_ref, g_ref, b_ref, y_ref, o_ref, ybuf, sems):
    i = pl.program_id(0)
    buf = i % 2

    n_tok = pl.num_programs(0) * COMBINE_TILE

    def gather_tile(tile, into):
        base = tile * COMBINE_TILE

        def start_rows(t, _):
            for k in range(MOE_TOPK):
                d = dest_ref[k * n_tok + base + t]
                pltpu.make_async_copy(y_ref.at[pl.ds(pl.multiple_of(d * SUBLANES, SUBLANES), SUBLANES)],
                                      ybuf.at[into, k, pl.ds(pl.multiple_of(t * SUBLANES, SUBLANES), SUBLANES)],
                                      sems.at[into]).start(priority=k % DMA_PRIORITIES)
            return 0

        lax.fori_loop(0, COMBINE_TILE, start_rows, 0, unroll=ISSUE_UNROLL)

    @pl.when(i == 0)
    def _():
        gather_tile(0, 0)

    @pl.when(i + 1 < pl.num_programs(0))
    def _():
        gather_tile(i + 1, 1 - buf)

    for k in range(MOE_TOPK):
        pltpu.make_async_copy(y_ref.at[pl.ds(0, COMBINE_TILE * SUBLANES)], ybuf.at[buf, k], sems.at[buf]).wait()

    gates = gates_ref[...]
    ffn = gates[:, 0:1] * _tiles_to_rows(ybuf.at[buf, 0], COMBINE_TILE)
    for k in range(1, MOE_TOPK):
        ffn = ffn + gates[:, k:k + 1] * _tiles_to_rows(ybuf.at[buf, k], COMBINE_TILE)
    o_ref[...] = _layer_norm(alpha * x1_ref[...] + ffn, g_ref[...], b_ref[...])


def _combine(dest, gates, x1, g2, b2, y_rows, alpha):
    N, D = x1.shape
    row = lambda i, d: (i, 0)
    full = lambda i, d: (0, 0)
    return pl.pallas_call(
        functools.partial(_combine_kernel, alpha),
        out_shape=jax.ShapeDtypeStruct((N, D), F32),
        grid_spec=pltpu.PrefetchScalarGridSpec(
            num_scalar_prefetch=1,
            grid=(N // COMBINE_TILE,),
            in_specs=[pl.BlockSpec((COMBINE_TILE, LANES), row), pl.BlockSpec((COMBINE_TILE, D), row),
                      pl.BlockSpec(g2.shape, full), pl.BlockSpec(b2.shape, full),
                      pl.BlockSpec(memory_space=pl.ANY)],
            out_specs=pl.BlockSpec((COMBINE_TILE, D), row),
            scratch_shapes=[pltpu.VMEM((2, MOE_TOPK, COMBINE_TILE * SUBLANES, LANES), F32),
                            pltpu.SemaphoreType.DMA((2,))],
        ),
        compiler_params=pltpu.CompilerParams(dimension_semantics=("arbitrary",),
                                             vmem_limit_bytes=VMEM_LIMIT_BYTES),
        name="combine",
    )(dest, gates, x1, g2, b2, y_rows)


def _rot_partner(w, half):
    return jnp.concatenate([-w[..., half:2 * half], w[..., :half]], axis=-1)


def _layer_weights(w_in, w_q_b, w_kv_b):
    D = w_in.shape[0]
    o1 = Q_LORA
    o2 = o1 + KV_LORA
    o3 = o2 + MLA_ROPE
    w_ql, w_kvl, w_kr = w_in[:, :o1], w_in[:, o1:o2], w_in[:, o2:o3]
    w_mq, w_mk, w_mv = w_in[:, o3:o3 + _MH], w_in[:, o3 + _MH:o3 + 2 * _MH], w_in[:, o3 + 2 * _MH:]
    zpad = lambda n: jnp.zeros((D, n), w_in.dtype)
    tail = SLOT - MLA_NOPE - MLA_ROPE
    kr_slot = jnp.concatenate([zpad(MLA_NOPE), w_kr, zpad(tail)], axis=1)
    krs_slot = jnp.concatenate([zpad(MLA_NOPE), _rot_partner(w_kr, MLA_ROPE // 2), zpad(tail)], axis=1)

    def moba_partner(w):
        w3 = w.reshape(D, MOBA_HEADS, MOBA_HD)
        part = jnp.concatenate([_rot_partner(w3[..., :MOBA_ROT], MOBA_ROT // 2),
                                jnp.zeros((D, MOBA_HEADS, MOBA_HD - MOBA_ROT), w.dtype)], axis=-1)
        return part.reshape(D, _MH)

    win = jnp.concatenate([w_ql, w_kvl, kr_slot, krs_slot, w_mk, moba_partner(w_mk)], axis=1).astype(BF16)
    mq_partner = _rot_partner(w_mq.reshape(D, MOBA_HEADS, MOBA_HD)[..., :MOBA_ROT], MOBA_ROT // 2)
    wmt = jnp.concatenate([w_mq, mq_partner.reshape(D, MOBA_HEADS * MOBA_ROT), w_mv], axis=1).T.astype(BF16)

    wq3 = w_q_b.reshape(Q_LORA, MLA_HEADS, MLA_NOPE + MLA_ROPE)
    wqs = _rot_partner(wq3[..., MLA_NOPE:], MLA_ROPE // 2).reshape(Q_LORA, MLA_HEADS * MLA_ROPE)
    wkv3 = w_kv_b.reshape(KV_LORA, MLA_HEADS, MLA_NOPE + MLA_V)
    wk = jnp.concatenate([wkv3[..., :MLA_NOPE], jnp.zeros((KV_LORA, MLA_HEADS, SLOT - MLA_NOPE), w_kv_b.dtype)],
                         axis=-1).reshape(KV_LORA, MLA_HEADS * SLOT).astype(BF16)
    wv = wkv3[..., MLA_NOPE:].reshape(KV_LORA, MLA_HEADS * MLA_V)
    return win, wmt, w_q_b.T.astype(BF16), wqs.T.astype(BF16), wk, wv.T.astype(BF16)


def _rope_tables(positions):
    pos = positions.astype(F32).reshape(1, -1)

    def cs(d_rot):
        inv_freq = ROPE_THETA ** (-jnp.arange(0, d_rot, 2, dtype=F32) / d_rot)
        ang = jnp.concatenate([inv_freq, inv_freq]).reshape(d_rot, 1) * pos
        return jnp.cos(ang), jnp.sin(ang)

    return cs(MLA_ROPE) + cs(MOBA_ROT)


def kernel(x, positions, w_in, q_a_norm, w_q_b, kv_a_norm, w_kv_b, w_o, ln1_g, ln1_b, w_router, b_router,
           w_gate, b_gate, w_up, b_up, w_down, b_down, ln2_g, ln2_b):
    B, T, D = x.shape
    depth = w_in.shape[0]
    alpha = (2.0 * depth) ** 0.25
    N = B * T
    assert T % MOBA_BLOCK == 0 and T // MOBA_BLOCK <= SUBLANES and N % OPROJ_TILE == 0
    assert D == SUBLANES * LANES
    n_asg = N * MOE_TOPK
    n_rows = n_asg + N_EXPERTS * MOE_GROUP
    tables = _rope_tables(positions)
    h = x.reshape(N, D)
    for l in range(depth):
        win, wmt, wqt, wqst, wk, wvt = _layer_weights(w_in[l], w_q_b[l], w_kv_b[l])
        qa, ka, va, mq, mk, mv = _prep(h, win, wmt, wqt, wqst, wk, wvt, q_a_norm[l].reshape(1, -1),
                                       kv_a_norm[l].reshape(1, -1), tables, B, T)
        a = _attention(qa, ka, va, B, T, "mla_attention")
        m = _attention(mq, mk, mv, B, T, "moba_attention")
        wr_pad = jnp.concatenate([w_router[l], jnp.zeros((D, LANES - N_EXPERTS), F32)], axis=1)
        wr_hi = wr_pad.astype(BF16)
        wr = jnp.concatenate([wr_hi, (wr_pad - wr_hi.astype(F32)).astype(BF16)], axis=1)
        br = b_router[l].reshape(N_EXPERTS, 1)
        x1, x1t, route, gates, cnt = _oproj(a, m, h, w_o[l].astype(BF16), ln1_g[l].reshape(1, D), ln1_b[l].reshape(1, D),
                                       wr, br, alpha)
        er = jnp.arange(N_EXPERTS, dtype=jnp.int32)
        counts = cnt[:, 0].astype(jnp.int32)
        padded = (counts + MOE_GROUP - 1) // MOE_GROUP * MOE_GROUP
        upto = er[None, :] <= er[:, None]
        pad_end = jnp.sum(jnp.where(upto, padded[None, :], 0), axis=1)
        pad_start = pad_end - padded
        e_idx = route[:MOE_TOPK]
        group_start = jnp.sum(jnp.where(e_idx[..., None] == er, pad_start, 0), axis=-1)
        dest = (group_start + route[MOE_TOPK:2 * MOE_TOPK]).reshape(n_asg)
        n_used = (pad_end[-1:] // MOE_GROUP).astype(jnp.int32)
        nblk = n_rows // MOE_GROUP
        blk_start = jnp.arange(nblk, dtype=jnp.int32) * MOE_GROUP
        blk_expert = jnp.minimum(jnp.sum((blk_start[:, None] >= pad_end[None, :]).astype(jnp.int32), axis=1),
                                 N_EXPERTS - 1)
        of_expert = blk_expert[:, None] == er[None, :]
        pick = lambda per_expert: jnp.sum(jnp.where(of_expert, per_expert[None, :], 0), axis=1)
        nonempty = padded > 0
        group_index = jnp.sum(jnp.where(upto & nonempty[None, :], 1, 0), axis=1) - 1
        later = (er[None, :] > er[:, None]) & nonempty[None, :]
        next_expert = jnp.min(jnp.where(later, er[None, :], N_EXPERTS), axis=1)
        next_expert = jnp.where(next_expert == N_EXPERTS, -1, next_expert)
        blk_slot = pick(group_index) % 2
        blk_next = pick(next_expert)
        real_rows = pick(pad_start + counts) - blk_start
        blk_full = jnp.clip((real_rows + MOE_PART - 1) // MOE_PART, 0, MOE_PARTS).astype(jnp.int32)
        half_start = jnp.arange(n_rows // MOE_PART, dtype=jnp.int32) * MOE_PART
        half_expert = jnp.minimum(jnp.sum((half_start[:, None] >= pad_end[None, :]).astype(jnp.int32), axis=1),
                                  N_EXPERTS - 1)
        half_real_end = jnp.sum(jnp.where(half_expert[:, None] == er[None, :], (pad_start + counts)[None, :], 0), axis=1)
        no_real_rows = (half_start >= pad_end[-1]) | (half_start >= half_real_end)
        fill = jnp.where(no_real_rows, FILL_FREE,
                         jnp.where(half_start + MOE_PART > half_real_end, FILL_TAIL, 0)).astype(jnp.int32)
        xs = _dispatch(dest, fill, x1t, n_rows)
        y_rows = _experts(blk_expert, blk_slot.astype(jnp.int32), blk_next, blk_full, n_used, xs, w_gate[l], b_gate[l].reshape(N_EXPERTS, 1, -1),
                          w_up[l], b_up[l].reshape(N_EXPERTS, 1, -1), w_down[l], b_down[l].reshape(N_EXPERTS, 1, -1))
        h = _combine(dest, gates, x1, ln2_g[l].reshape(1, D), ln2_b[l].reshape(1, D), y_rows, alpha)
    return h.reshape(B, T, D)
```

```python
import functools
import math

import jax
import jax.numpy as jnp
from jax import lax
from jax.experimental import pallas as pl
from jax.experimental.pallas import tpu as pltpu

ROPE_THETA = 500000.0
MLA_HEADS = 8
MLA_NOPE = 64
MLA_ROPE = 32
MLA_V = 64
Q_LORA = 256
KV_LORA = 128
MOBA_HEADS = 8
MOBA_HD = 64
MOBA_ROT = MOBA_HD // 4
MOBA_BLOCK = 256
MOBA_TOPK = 3
N_EXPERTS = 32
MOE_TOPK = 4
SWIGLU_LIMIT = 7.0
SWIGLU_ALPHA = 1.702
RMS_EPS = 1e-6
LN_EPS = 1e-5

LANES = 128
SUBLANES = 8
VMEM_LIMIT_BYTES = 56 * 1024 * 1024

SLOT = LANES
TQ = MOBA_BLOCK
ROW_TILE = 256
OPROJ_TILE = 512
MOE_GROUP = 512
MOE_HALF = MOE_GROUP // 2
GROUP_SUB = MOE_GROUP * SUBLANES
HALF_SUB = MOE_HALF * SUBLANES
PACKED_SUB = SUBLANES // 2
FILL_TAIL, FILL_FREE = 1, 2
DISPATCH_TILE = 512
COMBINE_TILE = 512
ISSUE_UNROLL = 4
DMA_PRIORITIES = 2
WEIGHT_DMA_PRIORITY = 1
HEAD_LANES = 64
ONES_LANE = 64
BIAS_LANE = 64
ATTN_HEADS_PER_STEP = 8
LOG2E = math.log2(math.e)
NEG_BIG = -(2.0 ** 100)

F32 = jnp.float32
BF16 = jnp.bfloat16
NT_DIMS = (((1,), (1,)), ((), ()))


def _dot(a, b, precision=None):
    return jnp.dot(a, b, preferred_element_type=F32, precision=precision)


def _dot_nt(a, b, precision=None):
    return lax.dot_general(a, b, NT_DIMS, preferred_element_type=F32, precision=precision)


def _rows_to_tiles(ref, x):
    rows = x.shape[0]
    for c in range(SUBLANES):
        ref[pl.ds(c, rows, stride=SUBLANES), :] = x[:, c * LANES:(c + 1) * LANES]


def _tiles_to_rows(ref, rows):
    return jnp.concatenate([ref[pl.ds(c, rows, stride=SUBLANES), :] for c in range(SUBLANES)], axis=1)


def _pack_rows_to_tiles(ref, x):
    rows, width = x.shape
    bits = lambda v: lax.bitcast_convert_type(v.astype(BF16).astype(F32), jnp.uint32)
    words = (bits(x[:, :width // 2]) >> 16) | (bits(x[:, width // 2:]) & jnp.uint32(0xFFFF0000))
    for c in range(PACKED_SUB):
        ref[pl.ds(c, rows, stride=PACKED_SUB), :] = words[:, c * LANES:(c + 1) * LANES]


def _unpack_tiles_to_rows(ref, rows):
    words = jnp.concatenate([ref[pl.ds(c, rows, stride=PACKED_SUB), :] for c in range(PACKED_SUB)], axis=1)
    low = lax.bitcast_convert_type(words << 16, F32).astype(BF16)
    high = lax.bitcast_convert_type(words & jnp.uint32(0xFFFF0000), F32).astype(BF16)
    return jnp.concatenate([low, high], axis=1)


def _rms(x, g):
    return x * lax.rsqrt(jnp.mean(x * x, axis=-1, keepdims=True) + RMS_EPS) * g


def _layer_norm(x, g, b):
    mu = jnp.mean(x, axis=-1, keepdims=True)
    xc = x - mu
    var = jnp.mean(xc * xc, axis=-1, keepdims=True)
    return xc * lax.rsqrt(var + LN_EPS) * g + b


_C_QL = 0
_C_KVL = _C_QL + Q_LORA
_C_KR = _C_KVL + KV_LORA
_C_KRS = _C_KR + SLOT
_C_MK = _C_KRS + SLOT
_MH = MOBA_HEADS * MOBA_HD
_C_MKS = _C_MK + _MH
_C_END = _C_MKS + _MH


def _prep_kernel(x_ref, win_ref, wmt_ref, wqt_ref, wqst_ref, wk_ref, wvt_ref, qg_ref, kvg_ref,
                 cat_ref, sat_ref, cbt_ref, sbt_ref,
                 qat_ref, ka_ref, vat_ref, mqt_ref, mk_ref, mvt_ref, kmean_scr):
    c = pl.program_id(1)

    @pl.when(c == 0)
    def _():
        kmean_scr[...] = jnp.zeros_like(kmean_scr)

    xb = x_ref[...].astype(BF16)
    cat = cat_ref[...]
    sat = sat_ref[...]
    cbt = cbt_ref[...]
    sbt = sbt_ref[...]
    const = lambda v, n: jnp.full((n, ROW_TILE), v, F32)
    tail = SLOT - MLA_NOPE - MLA_ROPE
    rest = MOBA_HD - MOBA_ROT
    ca = jnp.concatenate([const(1.0, MLA_NOPE), cat, const(0.0, tail)], axis=0).T
    sa = jnp.concatenate([const(0.0, MLA_NOPE), sat, const(0.0, tail)], axis=0).T
    cb = jnp.concatenate([cbt, const(1.0, rest), cbt, const(1.0, rest)], axis=0).T
    sb = jnp.concatenate([sbt, const(0.0, rest), sbt, const(0.0, rest)], axis=0).T
    lane = lax.broadcasted_iota(jnp.int32, (ROW_TILE, SLOT), 1)
    head_lanes = lane < HEAD_LANES
    ones_rows = (lax.broadcasted_iota(jnp.int32, (SLOT - HEAD_LANES, ROW_TILE), 0) == ONES_LANE - HEAD_LANES).astype(F32)

    ql = _dot(xb, win_ref[:, _C_QL:_C_KVL])
    kvl = _dot(xb, win_ref[:, _C_KVL:_C_KR])
    kr = _dot(xb, win_ref[:, _C_KR:_C_KRS])
    krs = _dot(xb, win_ref[:, _C_KRS:_C_MK])
    qn = _rms(ql, qg_ref[...]).astype(BF16)
    kvn = _rms(kvl, kvg_ref[...]).astype(BF16)
    q_t = _dot_nt(wqt_ref[...], qn)
    qs_t = _dot_nt(wqst_ref[...], qn)
    kn = _dot(kvn, wk_ref[...])
    v_t = _dot_nt(wvt_ref[...], kvn)
    scale_a = LOG2E / math.sqrt(MLA_NOPE + MLA_ROPE)
    kro = kr * ca + krs * sa
    qd = MLA_NOPE + MLA_ROPE
    q_pad = jnp.zeros((SLOT - qd, ROW_TILE), F32)
    for h in range(MLA_HEADS):
        sl = slice(h * SLOT, (h + 1) * SLOT)
        q_rope = q_t[h * qd + MLA_NOPE:(h + 1) * qd] * cat + qs_t[h * MLA_ROPE:(h + 1) * MLA_ROPE] * sat
        q_slot = jnp.concatenate([q_t[h * qd:h * qd + MLA_NOPE], q_rope, q_pad], axis=0)
        qat_ref[0, sl, :] = (q_slot * scale_a).astype(BF16)
        ka_ref[:, sl] = (kn[:, sl] + kro).astype(BF16)
        vat_ref[0, sl, :] = jnp.concatenate([v_t[h * MLA_V:(h + 1) * MLA_V], ones_rows], axis=0).astype(BF16)

    mk = _dot(xb, win_ref[:, _C_MK:_C_MKS])
    mks = _dot(xb, win_ref[:, _C_MKS:_C_END])
    npair = _MH // LANES
    pair = lambda a, j: a[:, j * LANES:(j + 1) * LANES]
    mk_rot = [pair(mk, j) * cb + pair(mks, j) * sb for j in range(npair)]
    mk_all = jnp.concatenate(mk_rot, axis=1)

    nrot = MOBA_HEADS * MOBA_ROT
    mq_t = _dot_nt(wmt_ref[0:_MH, :], xb)
    mqs_t = _dot_nt(wmt_ref[_MH:_MH + nrot, :], xb)
    mv_t = _dot_nt(wmt_ref[_MH + nrot:2 * _MH + nrot, :], xb)
    head = lambda a, h: a[h * MOBA_HD:(h + 1) * MOBA_HD]
    mq_rot_t = [jnp.concatenate([head(mq_t, h)[0:MOBA_ROT] * cbt + mqs_t[h * MOBA_ROT:(h + 1) * MOBA_ROT] * sbt,
                                 head(mq_t, h)[MOBA_ROT:]], axis=0) for h in range(MOBA_HEADS)]

    nrow = MOBA_HEADS * SUBLANES
    row_i = lax.broadcasted_iota(jnp.int32, (nrow, _MH), 0)
    lane_i = lax.broadcasted_iota(jnp.int32, (nrow, _MH), 1)
    kmean_c = jnp.mean(mk_all, axis=0, keepdims=True)
    put = ((row_i % SUBLANES) == c) & ((lane_i // MOBA_HD) == (row_i // SUBLANES))
    table = kmean_scr[...]
    gate_t = _dot(table, jnp.concatenate(mq_rot_t, axis=0), precision=lax.Precision.HIGHEST)
    kmean_scr[...] = jnp.where(put, jnp.broadcast_to(kmean_c, (nrow, _MH)), table)

    n_idx = lax.broadcasted_iota(jnp.int32, (SUBLANES, ROW_TILE), 0)
    valid = n_idx < c
    scale_b = LOG2E / math.sqrt(MOBA_HD)
    zero_rows = jnp.zeros((SLOT - HEAD_LANES - SUBLANES, ROW_TILE), F32)
    block_onehot = (lane == BIAS_LANE + c).astype(F32)
    for h in range(MOBA_HEADS):
        g = jnp.where(valid, gate_t[h * SUBLANES:(h + 1) * SUBLANES, :], -jnp.inf)
        rank = jnp.zeros((SUBLANES, ROW_TILE), jnp.int32)
        for k in range(1, SUBLANES):
            other = pltpu.roll(g, k, axis=0)
            other_n = pltpu.roll(n_idx, k, axis=0)
            beats = (other > g) | ((other == g) & (other_n < n_idx))
            rank = rank + beats.astype(jnp.int32)
        keep = (valid & (rank < MOBA_TOPK)) | (n_idx == c)
        bias = jnp.where(keep, 0.0, NEG_BIG)
        sl = slice(h * SLOT, (h + 1) * SLOT)
        mqt_ref[0, sl, :] = jnp.concatenate([mq_rot_t[h] * scale_b, bias, zero_rows], axis=0).astype(BF16)
        mvt_ref[0, sl, :] = jnp.concatenate([head(mv_t, h), ones_rows], axis=0).astype(BF16)
        j, hh = divmod(h, 2)
        k_h = mk_rot[j] if hh == 0 else pltpu.roll(mk_rot[j], HEAD_LANES, axis=1)
        mk_ref[:, sl] = jnp.where(head_lanes, k_h, block_onehot).astype(BF16)


def _prep(x2, win, wmt, wqt, wqst, wk, wvt, qg, kvg, tables, B, T):
    N, D = x2.shape
    nt = T // ROW_TILE
    row = lambda b, c: (b * nt + c, 0)
    col = lambda b, c: (0, b * nt + c)
    full = lambda b, c: (0, 0)
    cat, sat, cbt, sbt = tables
    width = MLA_HEADS * SLOT

    def rows(w):
        return pl.BlockSpec((ROW_TILE, w), row)

    def cols(a):
        return pl.BlockSpec((a.shape[0], ROW_TILE), col)

    def whole(a):
        return pl.BlockSpec(a.shape, full)

    rowmajor = jax.ShapeDtypeStruct((N, width), BF16)
    transposed = jax.ShapeDtypeStruct((B * nt, width, ROW_TILE), BF16)
    t_spec = pl.BlockSpec((1, width, ROW_TILE), lambda b, c: (b * nt + c, 0, 0))
    return pl.pallas_call(
        _prep_kernel,
        out_shape=(transposed, rowmajor, transposed, transposed, rowmajor, transposed),
        grid=(B, nt),
        in_specs=[rows(D), whole(win), whole(wmt), whole(wqt), whole(wqst), whole(wk), whole(wvt), whole(qg), whole(kvg),
                  cols(cat), cols(sat), cols(cbt), cols(sbt)],
        out_specs=(t_spec, rows(width), t_spec, t_spec, rows(width), t_spec),
        scratch_shapes=[pltpu.VMEM((MOBA_HEADS * SUBLANES, _MH), F32)],
        compiler_params=pltpu.CompilerParams(dimension_semantics=("arbitrary", "arbitrary"),
                                             vmem_limit_bytes=VMEM_LIMIT_BYTES),
        name="prep",
    )(x2, win, wmt, wqt, wqst, wk, wvt, qg, kvg, cat, sat, cbt, sbt)


def _attn_kernel(qt_ref, k_ref, vt_ref, o_ref, s_scr, acc_scr):
    i = pl.program_id(2)
    nh = ATTN_HEADS_PER_STEP
    slot = lambda h: slice(h * SLOT, (h + 1) * SLOT)
    nblk = i + 1
    nquad = nblk // 4
    pair_end = 4 * nquad + 2 * ((nblk - 4 * nquad) // 2)
    q_pos = i * TQ + lax.broadcasted_iota(jnp.int32, (1, TQ), 1)

    def fold(s):
        out = s[0:SUBLANES]
        for t in range(1, s.shape[0] // SUBLANES):
            out = jnp.maximum(out, s[t * SUBLANES:(t + 1) * SUBLANES])
        return out

    def score_blocks(j, n, mrun):
        j0 = pl.multiple_of(j * TQ, TQ)
        visible = (j0 + lax.broadcasted_iota(jnp.int32, (n * TQ, TQ), 0)) <= q_pos
        out = []
        for h in range(nh):
            s = _dot(k_ref[pl.ds(j0, n * TQ), slot(h)], qt_ref[0, slot(h), :])
            s = jnp.where(visible, s, -jnp.inf)
            s_scr[h, pl.ds(j, n)] = s.reshape(n, TQ, TQ)
            out.append(jnp.maximum(mrun[h], fold(s)))
        return tuple(out)

    mrun = tuple(jnp.full((SUBLANES, TQ), -jnp.inf, F32) for _ in range(nh))
    mrun = lax.fori_loop(0, nquad, lambda jq, m: score_blocks(4 * jq, 4, m), mrun)
    mrun = lax.fori_loop(2 * nquad, pair_end // 2, lambda jj, m: score_blocks(2 * jj, 2, m), mrun)
    mrun = lax.fori_loop(pair_end, nblk, lambda j, m: score_blocks(j, 1, m), mrun)
    ms = [jnp.max(m, axis=0, keepdims=True) for m in mrun]

    def accumulate(j, n, carry):
        for h in range(nh):
            p = jnp.exp2(s_scr[h, pl.ds(j, n)].reshape(n * TQ, TQ) - ms[h]).astype(BF16)
            vt = jnp.concatenate([vt_ref[j + t, slot(h), :] for t in range(n)], axis=1)
            acc_scr[h] += _dot(vt, p)
        return carry

    acc_scr[...] = jnp.zeros_like(acc_scr)
    lax.fori_loop(0, nquad, lambda jq, c: accumulate(4 * jq, 4, c), 0)
    lax.fori_loop(2 * nquad, pair_end // 2, lambda jj, c: accumulate(2 * jj, 2, c), 0)
    lax.fori_loop(pair_end, nblk, lambda j, c: accumulate(j, 1, c), 0)
    outs = [acc_scr[h, 0:HEAD_LANES] / acc_scr[h, ONES_LANE:ONES_LANE + 1] for h in range(nh)]
    for jj in range(nh // 2):
        both = jnp.concatenate([outs[2 * jj], outs[2 * jj + 1]], axis=0)
        o_ref[:, jj * LANES:(jj + 1) * LANES] = both.T.astype(o_ref.dtype)


def _attention(qt, k, vt, B, T, name):
    N = k.shape[0]
    nq = T // TQ
    nh = ATTN_HEADS_PER_STEP
    heads = k.shape[1] // SLOT
    return pl.pallas_call(
        _attn_kernel,
        out_shape=jax.ShapeDtypeStruct((N, heads * HEAD_LANES), BF16),
        grid=(B, heads // nh, nq),
        in_specs=[pl.BlockSpec((1, nh * SLOT, TQ), lambda b, g, i: (b * nq + i, g, 0)),
                  pl.BlockSpec((T, nh * SLOT), lambda b, g, i: (b, g)),
                  pl.BlockSpec((nq, nh * SLOT, TQ), lambda b, g, i: (b, g, 0))],
        out_specs=pl.BlockSpec((TQ, nh * HEAD_LANES), lambda b, g, i: (b * nq + i, g)),
        scratch_shapes=[pltpu.VMEM((nh, nq, TQ, TQ), F32), pltpu.VMEM((nh, SLOT, TQ), F32)],
        compiler_params=pltpu.CompilerParams(dimension_semantics=("arbitrary", "arbitrary", "arbitrary"),
                                             vmem_limit_bytes=VMEM_LIMIT_BYTES),
        name=name,
    )(qt, k, vt)


def _oproj_kernel(alpha, a_ref, m_ref, x_ref, wo_ref, g_ref, b_ref, wr_ref, br_ref,
                  x1_ref, x1t_ref, route_ref, gates_ref, cnt_ref, carry_scr):
    i = pl.program_id(0)

    @pl.when(i == 0)
    def _():
        carry_scr[...] = jnp.zeros_like(carry_scr)

    mix = _dot(jnp.concatenate([a_ref[...], m_ref[...]], axis=1), wo_ref[...])
    x1 = _layer_norm(alpha * x_ref[...] + mix, g_ref[...], b_ref[...])
    x1_ref[...] = x1
    _pack_rows_to_tiles(x1t_ref, x1)

    ne = br_ref.shape[0]
    rows = x1.shape[0]
    x_hi = x1.astype(BF16)
    x_lo = (x1 - x_hi.astype(F32)).astype(BF16)
    both = _dot(x_hi, wr_ref[...])
    logits_rm = both[:, :LANES] + both[:, LANES:] + _dot(x_lo, wr_ref[:, :LANES])
    logits = logits_rm.T[0:ne] + br_ref[...]
    expert = lax.broadcasted_iota(jnp.int32, logits.shape, 0)
    expert_f = expert.astype(F32)
    vals, idxs = [], []
    work = logits
    for _ in range(MOE_TOPK):
        mx = jnp.max(work, axis=0, keepdims=True)
        ix = jnp.min(jnp.where(work == mx, expert_f, float(ne)), axis=0, keepdims=True).astype(jnp.int32)
        vals.append(mx)
        idxs.append(ix)
        work = jnp.where(expert == ix, -jnp.inf, work)
    exps = [jnp.exp(v - vals[0]) for v in vals]
    den = exps[0]
    for e in exps[1:]:
        den = den + e

    onehot = jnp.zeros(logits.shape, F32)
    for ix in idxs:
        onehot = onehot + (expert == ix).astype(F32)
    r = lax.broadcasted_iota(jnp.int32, (rows, rows), 0)
    cidx = lax.broadcasted_iota(jnp.int32, (rows, rows), 1)
    earlier = (r < cidx).astype(BF16)
    carry = carry_scr[:, 0:1]
    before = _dot(onehot.astype(BF16), earlier) + carry
    row8 = lax.broadcasted_iota(jnp.int32, (SUBLANES, rows), 0)
    route = jnp.zeros((SUBLANES, rows), jnp.int32)
    gates = jnp.zeros((SUBLANES, rows), F32)
    for k in range(MOE_TOPK):
        rank = jnp.sum(jnp.where(expert == idxs[k], before, 0.0), axis=0, keepdims=True).astype(jnp.int32)
        route = jnp.where(row8 == k, idxs[k], route)
        route = jnp.where(row8 == MOE_TOPK + k, rank, route)
        gates = jnp.where(row8 == k, exps[k] / den, gates)
    route_ref[...] = route
    gates_ref[...] = jnp.concatenate([gates, jnp.zeros((LANES - SUBLANES, rows), F32)], axis=0).T
    new_carry = carry + jnp.sum(onehot, axis=1, keepdims=True)
    carry_scr[...] = jnp.broadcast_to(new_carry, carry_scr.shape)
    cnt_ref[...] = jnp.broadcast_to(new_carry, cnt_ref.shape)


def _oproj(a, m, x2, wo, g1, b1, wr, br, alpha):
    N, D = x2.shape
    nt = N // OPROJ_TILE
    row = lambda i: (i, 0)
    full = lambda i: (0, 0)
    return pl.pallas_call(
        functools.partial(_oproj_kernel, alpha),
        out_shape=(jax.ShapeDtypeStruct((N, D), F32),
                   jax.ShapeDtypeStruct((N * PACKED_SUB, LANES), jnp.uint32),
                   jax.ShapeDtypeStruct((SUBLANES, N), jnp.int32),
                   jax.ShapeDtypeStruct((N, LANES), F32),
                   jax.ShapeDtypeStruct((br.shape[0], LANES), F32)),
        grid=(nt,),
        in_specs=[pl.BlockSpec((OPROJ_TILE, a.shape[1]), row), pl.BlockSpec((OPROJ_TILE, m.shape[1]), row),
                  pl.BlockSpec((OPROJ_TILE, D), row), pl.BlockSpec(wo.shape, full),
                  pl.BlockSpec(g1.shape, full), pl.BlockSpec(b1.shape, full),
                  pl.BlockSpec(wr.shape, full), pl.BlockSpec(br.shape, full)],
        out_specs=(pl.BlockSpec((OPROJ_TILE, D), row), pl.BlockSpec((OPROJ_TILE * PACKED_SUB, LANES), row),
                   pl.BlockSpec((SUBLANES, OPROJ_TILE), lambda i: (0, i)),
                   pl.BlockSpec((OPROJ_TILE, LANES), row), pl.BlockSpec((br.shape[0], LANES), full)),
        scratch_shapes=[pltpu.VMEM((br.shape[0], LANES), F32)],
        compiler_params=pltpu.CompilerParams(dimension_semantics=("arbitrary",),
                                             vmem_limit_bytes=VMEM_LIMIT_BYTES),
        name="oproj_router",
    )(a, m, x2, wo, g1, b1, wr, br)


def _dispatch_kernel(dest_ref, fill_ref, x1_ref, xs_ref, zero_scr, sem, fill_sem):
    i = pl.program_id(0)
    fill_sub = MOE_HALF * PACKED_SUB
    nfill = xs_ref.shape[0] // fill_sub

    def fill_copy(blk, on):
        return pltpu.make_async_copy(zero_scr, xs_ref.at[pl.ds(pl.multiple_of(blk * fill_sub, fill_sub), fill_sub)], on)

    def for_blocks(kind, action):
        def body(blk, _):
            @pl.when(fill_ref[blk] == kind)
            def _():
                action(blk)
            return 0

        lax.fori_loop(0, nfill, body, 0)

    @pl.when(i == 0)
    def _():
        zero_scr[...] = jnp.zeros_like(zero_scr)
        for_blocks(FILL_TAIL, lambda blk: fill_copy(blk, sem).start())
        for_blocks(FILL_FREE, lambda blk: fill_copy(blk, fill_sem).start())
        for_blocks(FILL_TAIL, lambda blk: fill_copy(blk, sem).wait())

    base = i * DISPATCH_TILE
    n_tok = pl.num_programs(0) * DISPATCH_TILE

    def row_copy(t, k):
        d = dest_ref[k * n_tok + base + t]
        return pltpu.make_async_copy(x1_ref.at[pl.ds(pl.multiple_of(t * PACKED_SUB, PACKED_SUB), PACKED_SUB)],
                                     xs_ref.at[pl.ds(pl.multiple_of(d * PACKED_SUB, PACKED_SUB), PACKED_SUB)], sem)

    def start_rows(t, _):
        for k in range(MOE_TOPK):
            row_copy(t, k).start(priority=k % DMA_PRIORITIES)
        return 0

    lax.fori_loop(0, DISPATCH_TILE, start_rows, 0, unroll=ISSUE_UNROLL)
    for k in range(MOE_TOPK):
        pltpu.make_async_copy(x1_ref, xs_ref.at[pl.ds(0, DISPATCH_TILE * PACKED_SUB)], sem).wait()

    @pl.when(i == pl.num_programs(0) - 1)
    def _():
        for_blocks(FILL_FREE, lambda blk: fill_copy(blk, fill_sem).wait())


def _dispatch(dest, fill, x1t, n_rows):
    return pl.pallas_call(
        _dispatch_kernel,
        out_shape=jax.ShapeDtypeStruct((n_rows * PACKED_SUB, LANES), x1t.dtype),
        grid_spec=pltpu.PrefetchScalarGridSpec(
            num_scalar_prefetch=2,
            grid=(x1t.shape[0] // (DISPATCH_TILE * PACKED_SUB),),
            in_specs=[pl.BlockSpec((DISPATCH_TILE * PACKED_SUB, LANES), lambda i, d, t: (i, 0))],
            out_specs=pl.BlockSpec(memory_space=pl.ANY),
            scratch_shapes=[pltpu.VMEM((MOE_HALF * PACKED_SUB, LANES), x1t.dtype), pltpu.SemaphoreType.DMA(()),
                            pltpu.SemaphoreType.DMA(())],
        ),
        compiler_params=pltpu.CompilerParams(dimension_semantics=("arbitrary",),
                                             vmem_limit_bytes=VMEM_LIMIT_BYTES),
        name="dispatch",
    )(dest, fill, x1t)


def _experts_kernel(be_ref, slot_ref, nxt_ref, full_ref, nused_ref, x_ref, wg_hbm, bg_ref, wu_hbm, bu_ref, wd_hbm, bd_ref,
                    y_ref, wg_st, wu_st, wd_st, wg_bf, wu_bf, wd_bf, sems):
    i = pl.program_id(0)
    prev = be_ref[jnp.maximum(i - 1, 0)]
    changed = (i == 0) | (be_ref[i] != prev)
    active = i < nused_ref[0]
    slot = slot_ref[i]

    def weight_copies(expert, s):
        return (pltpu.make_async_copy(wg_hbm.at[expert], wg_st.at[s], sems.at[s, 0]),
                pltpu.make_async_copy(wu_hbm.at[expert], wu_st.at[s], sems.at[s, 1]),
                pltpu.make_async_copy(wd_hbm.at[expert], wd_st.at[s], sems.at[s, 2]))

    @pl.when(i == 0)
    def _():
        for cp in weight_copies(be_ref[0], slot):
            cp.start()

    @pl.when(active & changed)
    def _():
        for cp in weight_copies(be_ref[i], slot):
            cp.wait()

        @pl.when(nxt_ref[i] >= 0)
        def _():
            for cp in weight_copies(nxt_ref[i], 1 - slot):
                cp.start(priority=WEIGHT_DMA_PRIORITY)

        wg_bf[...] = wg_st[slot].astype(BF16)
        wu_bf[...] = wu_st[slot].astype(BF16)
        wd_bf[...] = wd_st[slot].astype(BF16)

    def mlp(nrows):
        rows = pl.ds(0, nrows * SUBLANES)
        xb = _unpack_tiles_to_rows(x_ref.at[pl.ds(0, nrows * PACKED_SUB)], nrows)
        g = jnp.minimum(_dot(xb, wg_bf[...]) + bg_ref[0], SWIGLU_LIMIT)
        u = jnp.clip(_dot(xb, wu_bf[...]) + bu_ref[0], -SWIGLU_LIMIT, SWIGLU_LIMIT)
        h = g * (1.0 / (1.0 + jnp.exp(-SWIGLU_ALPHA * g))) * (u + 1.0)
        _rows_to_tiles(y_ref.at[rows], _dot(h.astype(BF16), wd_bf[...]) + bd_ref[0])

    whole = active & (full_ref[i] > 0)

    @pl.when(whole)
    def _():
        mlp(MOE_GROUP)

    @pl.when(active & jnp.logical_not(whole))
    def _():
        mlp(MOE_HALF)

    @pl.when(jnp.logical_not(active))
    def _():
        y_ref[pl.ds(0, HALF_SUB), :] = jnp.zeros((HALF_SUB, LANES), F32)

    @pl.when(jnp.logical_not(whole))
    def _():
        y_ref[pl.ds(HALF_SUB, HALF_SUB), :] = jnp.zeros((HALF_SUB, LANES), F32)


def _experts(blk_expert, blk_slot, blk_next, blk_full, n_used, xs, wg, bg, wu, bu, wd, bd):
    E, D, F = wg.shape
    nblk = xs.shape[0] // (MOE_GROUP * PACKED_SUB)

    def rowmap(i, be, sl, nx, fl, nu):
        return (jnp.minimum(i, nu[0] - 1), 0)

    def bmap(i, be, sl, nx, fl, nu):
        return (be[i], 0, 0)

    hbm = pl.BlockSpec(memory_space=pl.ANY)
    return pl.pallas_call(
        _experts_kernel,
        out_shape=jax.ShapeDtypeStruct((nblk * GROUP_SUB, LANES), F32),
        grid_spec=pltpu.PrefetchScalarGridSpec(
            num_scalar_prefetch=5,
            grid=(nblk,),
            in_specs=[pl.BlockSpec((MOE_GROUP * PACKED_SUB, LANES), rowmap),
                      hbm, pl.BlockSpec((1, 1, F), bmap),
                      hbm, pl.BlockSpec((1, 1, F), bmap),
                      hbm, pl.BlockSpec((1, 1, D), bmap)],
            out_specs=pl.BlockSpec((GROUP_SUB, LANES), lambda i, be, sl, nx, fl, nu: (i, 0)),
            scratch_shapes=[pltpu.VMEM((2, D, F), F32), pltpu.VMEM((2, D, F), F32), pltpu.VMEM((2, F, D), F32),
                            pltpu.VMEM((D, F), BF16), pltpu.VMEM((D, F), BF16), pltpu.VMEM((F, D), BF16),
                            pltpu.SemaphoreType.DMA((2, 3))],
        ),
        compiler_params=pltpu.CompilerParams(dimension_semantics=("arbitrary",),
                                             vmem_limit_bytes=VMEM_LIMIT_BYTES),
        name="experts",
    )(blk_expert, blk_slot, blk_next, blk_full, n_used, xs, wg, bg, wu, bu, wd, bd)


def _combine_kernel(alpha, dest_ref, gates_ref, x1_ref, g_ref, b_ref, y_ref, o_ref, ybuf, sems):
    i = pl.program_id(0)
    buf = i % 2

    n_tok = pl.num_programs(0) * COMBINE_TILE

    def gather_tile(tile, into):
        base = tile * COMBINE_TILE

        def start_rows(t, _):
            for k in range(MOE_TOPK):
                d = dest_ref[k * n_tok + base + t]
                pltpu.make_async_copy(y_ref.at[pl.ds(pl.multiple_of(d * SUBLANES, SUBLANES), SUBLANES)],
                                      ybuf.at[into, k, pl.ds(pl.multiple_of(t * SUBLANES, SUBLANES), SUBLANES)],
                                      sems.at[into]).start(priority=k % DMA_PRIORITIES)
            return 0

        lax.fori_loop(0, COMBINE_TILE, start_rows, 0, unroll=ISSUE_UNROLL)

    @pl.when(i == 0)
    def _():
        gather_tile(0, 0)

    @pl.when(i + 1 < pl.num_programs(0))
    def _():
        gather_tile(i + 1, 1 - buf)

    for k in range(MOE_TOPK):
        pltpu.make_async_copy(y_ref.at[pl.ds(0, COMBINE_TILE * SUBLANES)], ybuf.at[buf, k], sems.at[buf]).wait()

    gates = gates_ref[...]
    ffn = gates[:, 0:1] * _tiles_to_rows(ybuf.at[buf, 0], COMBINE_TILE)
    for k in range(1, MOE_TOPK):
        ffn = ffn + gates[:, k:k + 1] * _tiles_to_rows(ybuf.at[buf, k], COMBINE_TILE)
    o_ref[...] = _layer_norm(alpha * x1_ref[...] + ffn, g_ref[...], b_ref[...])


def _combine(dest, gates, x1, g2, b2, y_rows, alpha):
    N, D = x1.shape
    row = lambda i, d: (i, 0)
    full = lambda i, d: (0, 0)
    return pl.pallas_call(
        functools.partial(_combine_kernel, alpha),
        out_shape=jax.ShapeDtypeStruct((N, D), F32),
        grid_spec=pltpu.PrefetchScalarGridSpec(
            num_scalar_prefetch=1,
            grid=(N // COMBINE_TILE,),
            in_specs=[pl.BlockSpec((COMBINE_TILE, LANES), row), pl.BlockSpec((COMBINE_TILE, D), row),
                      pl.BlockSpec(g2.shape, full), pl.BlockSpec(b2.shape, full),
                      pl.BlockSpec(memory_space=pl.ANY)],
            out_specs=pl.BlockSpec((COMBINE_TILE, D), row),
            scratch_shapes=[pltpu.VMEM((2, MOE_TOPK, COMBINE_TILE * SUBLANES, LANES), F32),
                            pltpu.SemaphoreType.DMA((2,))],
        ),
        compiler_params=pltpu.CompilerParams(dimension_semantics=("arbitrary",),
                                             vmem_limit_bytes=VMEM_LIMIT_BYTES),
        name="combine",
    )(dest, gates, x1, g2, b2, y_rows)


def _rot_partner(w, half):
    return jnp.concatenate([-w[..., half:2 * half], w[..., :half]], axis=-1)


def _layer_weights(w_in, w_q_b, w_kv_b):
    D = w_in.shape[0]
    o1 = Q_LORA
    o2 = o1 + KV_LORA
    o3 = o2 + MLA_ROPE
    w_ql, w_kvl, w_kr = w_in[:, :o1], w_in[:, o1:o2], w_in[:, o2:o3]
    w_mq, w_mk, w_mv = w_in[:, o3:o3 + _MH], w_in[:, o3 + _MH:o3 + 2 * _MH], w_in[:, o3 + 2 * _MH:]
    zpad = lambda n: jnp.zeros((D, n), w_in.dtype)
    tail = SLOT - MLA_NOPE - MLA_ROPE
    kr_slot = jnp.concatenate([zpad(MLA_NOPE), w_kr, zpad(tail)], axis=1)
    krs_slot = jnp.concatenate([zpad(MLA_NOPE), _rot_partner(w_kr, MLA_ROPE // 2), zpad(tail)], axis=1)

    def moba_partner(w):
        w3 = w.reshape(D, MOBA_HEADS, MOBA_HD)
        part = jnp.concatenate([_rot_partner(w3[..., :MOBA_ROT], MOBA_ROT // 2),
                                jnp.zeros((D, MOBA_HEADS, MOBA_HD - MOBA_ROT), w.dtype)], axis=-1)
        return part.reshape(D, _MH)

    win = jnp.concatenate([w_ql, w_kvl, kr_slot, krs_slot, w_mk, moba_partner(w_mk)], axis=1).astype(BF16)
    mq_partner = _rot_partner(w_mq.reshape(D, MOBA_HEADS, MOBA_HD)[..., :MOBA_ROT], MOBA_ROT // 2)
    wmt = jnp.concatenate([w_mq, mq_partner.reshape(D, MOBA_HEADS * MOBA_ROT), w_mv], axis=1).T.astype(BF16)

    wq3 = w_q_b.reshape(Q_LORA, MLA_HEADS, MLA_NOPE + MLA_ROPE)
    wqs = _rot_partner(wq3[..., MLA_NOPE:], MLA_ROPE // 2).reshape(Q_LORA, MLA_HEADS * MLA_ROPE)
    wkv3 = w_kv_b.reshape(KV_LORA, MLA_HEADS, MLA_NOPE + MLA_V)
    wk = jnp.concatenate([wkv3[..., :MLA_NOPE], jnp.zeros((KV_LORA, MLA_HEADS, SLOT - MLA_NOPE), w_kv_b.dtype)],
                         axis=-1).reshape(KV_LORA, MLA_HEADS * SLOT).astype(BF16)
    wv = wkv3[..., MLA_NOPE:].reshape(KV_LORA, MLA_HEADS * MLA_V)
    return win, wmt, w_q_b.T.astype(BF16), wqs.T.astype(BF16), wk, wv.T.astype(BF16)


def _rope_tables(positions):
    pos = positions.astype(F32).reshape(1, -1)

    def cs(d_rot):
        inv_freq = ROPE_THETA ** (-jnp.arange(0, d_rot, 2, dtype=F32) / d_rot)
        ang = jnp.concatenate([inv_freq, inv_freq]).reshape(d_rot, 1) * pos
        return jnp.cos(ang), jnp.sin(ang)

    return cs(MLA_ROPE) + cs(MOBA_ROT)


def kernel(x, positions, w_in, q_a_norm, w_q_b, kv_a_norm, w_kv_b, w_o, ln1_g, ln1_b, w_router, b_router,
           w_gate, b_gate, w_up, b_up, w_down, b_down, ln2_g, ln2_b):
    B, T, D = x.shape
    depth = w_in.shape[0]
    alpha = (2.0 * depth) ** 0.25
    N = B * T
    assert T % MOBA_BLOCK == 0 and T // MOBA_BLOCK <= SUBLANES and N % OPROJ_TILE == 0
    assert D == SUBLANES * LANES
    n_asg = N * MOE_TOPK
    n_rows = n_asg + N_EXPERTS * MOE_GROUP
    tables = _rope_tables(positions)
    h = x.reshape(N, D)
    for l in range(depth):
        win, wmt, wqt, wqst, wk, wvt = _layer_weights(w_in[l], w_q_b[l], w_kv_b[l])
        qa, ka, va, mq, mk, mv = _prep(h, win, wmt, wqt, wqst, wk, wvt, q_a_norm[l].reshape(1, -1),
                                       kv_a_norm[l].reshape(1, -1), tables, B, T)
        a = _attention(qa, ka, va, B, T, "mla_attention")
        m = _attention(mq, mk, mv, B, T, "moba_attention")
        wr_pad = jnp.concatenate([w_router[l], jnp.zeros((D, LANES - N_EXPERTS), F32)], axis=1)
        wr_hi = wr_pad.astype(BF16)
        wr = jnp.concatenate([wr_hi, (wr_pad - wr_hi.astype(F32)).astype(BF16)], axis=1)
        br = b_router[l].reshape(N_EXPERTS, 1)
        x1, x1t, route, gates, cnt = _oproj(a, m, h, w_o[l].astype(BF16), ln1_g[l].reshape(1, D), ln1_b[l].reshape(1, D),
                                       wr, br, alpha)
        er = jnp.arange(N_EXPERTS, dtype=jnp.int32)
        counts = cnt[:, 0].astype(jnp.int32)
        padded = (counts + MOE_GROUP - 1) // MOE_GROUP * MOE_GROUP
        upto = er[None, :] <= er[:, None]
        pad_end = jnp.sum(jnp.where(upto, padded[None, :], 0), axis=1)
        pad_start = pad_end - padded
        e_idx = route[:MOE_TOPK]
        group_start = jnp.sum(jnp.where(e_idx[..., None] == er, pad_start, 0), axis=-1)
        dest = (group_start + route[MOE_TOPK:2 * MOE_TOPK]).reshape(n_asg)
        n_used = (pad_end[-1:] // MOE_GROUP).astype(jnp.int32)
        nblk = n_rows // MOE_GROUP
        blk_start = jnp.arange(nblk, dtype=jnp.int32) * MOE_GROUP
        blk_expert = jnp.minimum(jnp.sum((blk_start[:, None] >= pad_end[None, :]).astype(jnp.int32), axis=1),
                                 N_EXPERTS - 1)
        of_expert = blk_expert[:, None] == er[None, :]
        pick = lambda per_expert: jnp.sum(jnp.where(of_expert, per_expert[None, :], 0), axis=1)
        nonempty = padded > 0
        group_index = jnp.sum(jnp.where(upto & nonempty[None, :], 1, 0), axis=1) - 1
        later = (er[None, :] > er[:, None]) & nonempty[None, :]
        next_expert = jnp.min(jnp.where(later, er[None, :], N_EXPERTS), axis=1)
        next_expert = jnp.where(next_expert == N_EXPERTS, -1, next_expert)
        blk_slot = pick(group_index) % 2
        blk_next = pick(next_expert)
        blk_full = (pick(pad_start + counts) > blk_start + MOE_HALF).astype(jnp.int32)
        half_start = jnp.arange(n_rows // MOE_HALF, dtype=jnp.int32) * MOE_HALF
        half_expert = jnp.minimum(jnp.sum((half_start[:, None] >= pad_end[None, :]).astype(jnp.int32), axis=1),
                                  N_EXPERTS - 1)
        half_real_end = jnp.sum(jnp.where(half_expert[:, None] == er[None, :], (pad_start + counts)[None, :], 0), axis=1)
        no_real_rows = (half_start >= pad_end[-1]) | (half_start >= half_real_end)
        fill = jnp.where(no_real_rows, FILL_FREE,
                         jnp.where(half_start + MOE_HALF > half_real_end, FILL_TAIL, 0)).astype(jnp.int32)
        xs = _dispatch(dest, fill, x1t, n_rows)
        y_rows = _experts(blk_expert, blk_slot.astype(jnp.int32), blk_next, blk_full, n_used, xs, w_gate[l], b_gate[l].reshape(N_EXPERTS, 1, -1),
                          w_up[l], b_up[l].reshape(N_EXPERTS, 1, -1), w_down[l], b_down[l].reshape(N_EXPERTS, 1, -1))
        h = _combine(dest, gates, x1, ln2_g[l].reshape(1, D), ln2_b[l].reshape(1, D), y_rows, alpha)
    return h.reshape(B, T, D)
```

```python
import functools
import math

import jax
import jax.numpy as jnp
from jax import lax
from jax.experimental import pallas as pl
from jax.experimental.pallas import tpu as pltpu

ROPE_THETA = 500000.0
MLA_HEADS = 8
MLA_NOPE = 64
MLA_ROPE = 32
MLA_V = 64
Q_LORA = 256
KV_LORA = 128
MOBA_HEADS = 8
MOBA_HD = 64
MOBA_ROT = MOBA_HD // 4
MOBA_BLOCK = 256
MOBA_TOPK = 3
N_EXPERTS = 32
MOE_TOPK = 4
SWIGLU_LIMIT = 7.0
SWIGLU_ALPHA = 1.702
RMS_EPS = 1e-6
LN_EPS = 1e-5

LANES = 128
SUBLANES = 8
VMEM_LIMIT_BYTES = 56 * 1024 * 1024

SLOT = LANES
TQ = MOBA_BLOCK
ROW_TILE = 256
OPROJ_TILE = 512
MOE_GROUP = 512
MOE_HALF = MOE_GROUP // 2
GROUP_SUB = MOE_GROUP * SUBLANES
HALF_SUB = MOE_HALF * SUBLANES
PACKED_SUB = SUBLANES // 2
FILL_TAIL, FILL_FREE = 1, 2
DISPATCH_TILE = 512
COMBINE_TILE = 512
ISSUE_UNROLL = 4
DMA_PRIORITIES = 2
WEIGHT_DMA_PRIORITY = 1
HEAD_LANES = 64
ONES_LANE = 64
BIAS_LANE = 64
ATTN_HEADS_PER_STEP = 8
LOG2E = math.log2(math.e)
NEG_BIG = -(2.0 ** 100)

F32 = jnp.float32
BF16 = jnp.bfloat16
NT_DIMS = (((1,), (1,)), ((), ()))


def _dot(a, b, precision=None):
    return jnp.dot(a, b, preferred_element_type=F32, precision=precision)


def _dot_nt(a, b, precision=None):
    return lax.dot_general(a, b, NT_DIMS, preferred_element_type=F32, precision=precision)


def _rows_to_tiles(ref, x):
    rows = x.shape[0]
    for c in range(SUBLANES):
        ref[pl.ds(c, rows, stride=SUBLANES), :] = x[:, c * LANES:(c + 1) * LANES]


def _tiles_to_rows(ref, rows):
    return jnp.concatenate([ref[pl.ds(c, rows, stride=SUBLANES), :] for c in range(SUBLANES)], axis=1)


def _pack_rows_to_tiles(ref, x):
    rows, width = x.shape
    bits = lambda v: lax.bitcast_convert_type(v.astype(BF16).astype(F32), jnp.uint32)
    words = (bits(x[:, :width // 2]) >> 16) | (bits(x[:, width // 2:]) & jnp.uint32(0xFFFF0000))
    for c in range(PACKED_SUB):
        ref[pl.ds(c, rows, stride=PACKED_SUB), :] = words[:, c * LANES:(c + 1) * LANES]


def _unpack_tiles_to_rows(ref, rows):
    words = jnp.concatenate([ref[pl.ds(c, rows, stride=PACKED_SUB), :] for c in range(PACKED_SUB)], axis=1)
    low = lax.bitcast_convert_type(words << 16, F32).astype(BF16)
    high = lax.bitcast_convert_type(words & jnp.uint32(0xFFFF0000), F32).astype(BF16)
    return jnp.concatenate([low, high], axis=1)


def _rms(x, g):
    return x * lax.rsqrt(jnp.mean(x * x, axis=-1, keepdims=True) + RMS_EPS) * g


def _layer_norm(x, g, b):
    mu = jnp.mean(x, axis=-1, keepdims=True)
    xc = x - mu
    var = jnp.mean(xc * xc, axis=-1, keepdims=True)
    return xc * lax.rsqrt(var + LN_EPS) * g + b


_C_QL = 0
_C_KVL = _C_QL + Q_LORA
_C_KR = _C_KVL + KV_LORA
_C_KRS = _C_KR + SLOT
_C_MK = _C_KRS + SLOT
_MH = MOBA_HEADS * MOBA_HD
_C_MKS = _C_MK + _MH
_C_END = _C_MKS + _MH


def _prep_kernel(x_ref, win_ref, wmt_ref, wqt_ref, wqst_ref, wk_ref, wvt_ref, qg_ref, kvg_ref,
                 cat_ref, sat_ref, cbt_ref, sbt_ref,
                 qat_ref, ka_ref, vat_ref, mqt_ref, mk_ref, mvt_ref, kmean_scr):
    c = pl.program_id(1)

    @pl.when(c == 0)
    def _():
        kmean_scr[...] = jnp.zeros_like(kmean_scr)

    xb = x_ref[...].astype(BF16)
    both_halves = lambda t: jnp.concatenate([t, t], axis=0)
    cat = both_halves(cat_ref[...])
    sat = both_halves(sat_ref[...])
    cbt = both_halves(cbt_ref[...])
    sbt = both_halves(sbt_ref[...])
    const = lambda v, n: jnp.full((n, ROW_TILE), v, F32)
    tail = SLOT - MLA_NOPE - MLA_ROPE
    rest = MOBA_HD - MOBA_ROT
    ca = jnp.concatenate([const(1.0, MLA_NOPE), cat, const(0.0, tail)], axis=0).T
    sa = jnp.concatenate([const(0.0, MLA_NOPE), sat, const(0.0, tail)], axis=0).T
    cb = jnp.concatenate([cbt, const(1.0, rest), cbt, const(1.0, rest)], axis=0).T
    sb = jnp.concatenate([sbt, const(0.0, rest), sbt, const(0.0, rest)], axis=0).T
    lane = lax.broadcasted_iota(jnp.int32, (ROW_TILE, SLOT), 1)
    head_lanes = lane < HEAD_LANES
    ones_rows = (lax.broadcasted_iota(jnp.int32, (SLOT - HEAD_LANES, ROW_TILE), 0) == ONES_LANE - HEAD_LANES).astype(F32)

    ql = _dot(xb, win_ref[:, _C_QL:_C_KVL])
    kvl = _dot(xb, win_ref[:, _C_KVL:_C_KR])
    kr = _dot(xb, win_ref[:, _C_KR:_C_KRS])
    krs = _dot(xb, win_ref[:, _C_KRS:_C_MK])
    qn = _rms(ql, qg_ref[...]).astype(BF16)
    kvn = _rms(kvl, kvg_ref[...]).astype(BF16)
    q_t = _dot_nt(wqt_ref[...], qn)
    qs_t = _dot_nt(wqst_ref[...], qn)
    kn = _dot(kvn, wk_ref[...])
    v_t = _dot_nt(wvt_ref[...], kvn)
    scale_a = LOG2E / math.sqrt(MLA_NOPE + MLA_ROPE)
    kro = kr * ca + krs * sa
    qd = MLA_NOPE + MLA_ROPE
    q_pad = jnp.zeros((SLOT - qd, ROW_TILE), F32)
    for h in range(MLA_HEADS):
        sl = slice(h * SLOT, (h + 1) * SLOT)
        q_rope = q_t[h * qd + MLA_NOPE:(h + 1) * qd] * cat + qs_t[h * MLA_ROPE:(h + 1) * MLA_ROPE] * sat
        q_slot = jnp.concatenate([q_t[h * qd:h * qd + MLA_NOPE], q_rope, q_pad], axis=0)
        qat_ref[0, sl, :] = (q_slot * scale_a).astype(BF16)
        ka_ref[:, sl] = (kn[:, sl] + kro).astype(BF16)
        vat_ref[0, sl, :] = jnp.concatenate([v_t[h * MLA_V:(h + 1) * MLA_V], ones_rows], axis=0).astype(BF16)

    mk = _dot(xb, win_ref[:, _C_MK:_C_MKS])
    mks = _dot(xb, win_ref[:, _C_MKS:_C_END])
    npair = _MH // LANES
    pair = lambda a, j: a[:, j * LANES:(j + 1) * LANES]
    mk_rot = [pair(mk, j) * cb + pair(mks, j) * sb for j in range(npair)]
    mk_all = jnp.concatenate(mk_rot, axis=1)

    nrot = MOBA_HEADS * MOBA_ROT
    mq_t = _dot_nt(wmt_ref[0:_MH, :], xb)
    mqs_t = _dot_nt(wmt_ref[_MH:_MH + nrot, :], xb)
    mv_t = _dot_nt(wmt_ref[_MH + nrot:2 * _MH + nrot, :], xb)
    head = lambda a, h: a[h * MOBA_HD:(h + 1) * MOBA_HD]
    mq_rot_t = [jnp.concatenate([head(mq_t, h)[0:MOBA_ROT] * cbt + mqs_t[h * MOBA_ROT:(h + 1) * MOBA_ROT] * sbt,
                                 head(mq_t, h)[MOBA_ROT:]], axis=0) for h in range(MOBA_HEADS)]

    nrow = MOBA_HEADS * SUBLANES
    row_i = lax.broadcasted_iota(jnp.int32, (nrow, _MH), 0)
    lane_i = lax.broadcasted_iota(jnp.int32, (nrow, _MH), 1)
    kmean_c = jnp.mean(mk_all, axis=0, keepdims=True)
    put = ((row_i % SUBLANES) == c) & ((lane_i // MOBA_HD) == (row_i // SUBLANES))
    table = kmean_scr[...]
    gate_t = _dot(table, jnp.concatenate(mq_rot_t, axis=0), precision=lax.Precision.HIGHEST)
    kmean_scr[...] = jnp.where(put, jnp.broadcast_to(kmean_c, (nrow, _MH)), table)

    n_idx = lax.broadcasted_iota(jnp.int32, (SUBLANES, ROW_TILE), 0)
    valid = n_idx < c
    scale_b = LOG2E / math.sqrt(MOBA_HD)
    zero_rows = jnp.zeros((SLOT - HEAD_LANES - SUBLANES, ROW_TILE), F32)
    block_onehot = (lane == BIAS_LANE + c).astype(F32)
    for h in range(MOBA_HEADS):
        g = jnp.where(valid, gate_t[h * SUBLANES:(h + 1) * SUBLANES, :], -jnp.inf)
        rank = jnp.zeros((SUBLANES, ROW_TILE), jnp.int32)
        for k in range(1, SUBLANES):
            other = pltpu.roll(g, k, axis=0)
            other_n = pltpu.roll(n_idx, k, axis=0)
            beats = (other > g) | ((other == g) & (other_n < n_idx))
            rank = rank + beats.astype(jnp.int32)
        keep = (valid & (rank < MOBA_TOPK)) | (n_idx == c)
        bias = jnp.where(keep, 0.0, NEG_BIG)
        sl = slice(h * SLOT, (h + 1) * SLOT)
        mqt_ref[0, sl, :] = jnp.concatenate([mq_rot_t[h] * scale_b, bias, zero_rows], axis=0).astype(BF16)
        mvt_ref[0, sl, :] = jnp.concatenate([head(mv_t, h), ones_rows], axis=0).astype(BF16)
        j, hh = divmod(h, 2)
        k_h = mk_rot[j] if hh == 0 else pltpu.roll(mk_rot[j], HEAD_LANES, axis=1)
        mk_ref[:, sl] = jnp.where(head_lanes, k_h, block_onehot).astype(BF16)


def _prep(x2, win, wmt, wqt, wqst, wk, wvt, qg, kvg, tables, B, T):
    N, D = x2.shape
    nt = T // ROW_TILE
    row = lambda b, c: (b * nt + c, 0)
    col = lambda b, c: (0, b * nt + c)
    full = lambda b, c: (0, 0)
    cat, sat, cbt, sbt = tables
    width = MLA_HEADS * SLOT

    def rows(w):
        return pl.BlockSpec((ROW_TILE, w), row)

    def cols(a):
        return pl.BlockSpec((a.shape[0], ROW_TILE), col)

    def whole(a):
        return pl.BlockSpec(a.shape, full)

    rowmajor = jax.ShapeDtypeStruct((N, width), BF16)
    transposed = jax.ShapeDtypeStruct((B * nt, width, ROW_TILE), BF16)
    t_spec = pl.BlockSpec((1, width, ROW_TILE), lambda b, c: (b * nt + c, 0, 0))
    return pl.pallas_call(
        _prep_kernel,
        out_shape=(transposed, rowmajor, transposed, transposed, rowmajor, transposed),
        grid=(B, nt),
        in_specs=[rows(D), whole(win), whole(wmt), whole(wqt), whole(wqst), whole(wk), whole(wvt), whole(qg), whole(kvg),
                  cols(cat), cols(sat), cols(cbt), cols(sbt)],
        out_specs=(t_spec, rows(width), t_spec, t_spec, rows(width), t_spec),
        scratch_shapes=[pltpu.VMEM((MOBA_HEADS * SUBLANES, _MH), F32)],
        compiler_params=pltpu.CompilerParams(dimension_semantics=("arbitrary", "arbitrary"),
                                             vmem_limit_bytes=VMEM_LIMIT_BYTES),
        name="prep",
    )(x2, win, wmt, wqt, wqst, wk, wvt, qg, kvg, cat, sat, cbt, sbt)


def _attn_kernel(qt_ref, k_ref, vt_ref, o_ref, s_scr, acc_scr):
    i = pl.program_id(2)
    nh = ATTN_HEADS_PER_STEP
    slot = lambda h: slice(h * SLOT, (h + 1) * SLOT)
    nblk = i + 1
    nquad = nblk // 4
    pair_end = 4 * nquad + 2 * ((nblk - 4 * nquad) // 2)
    q_pos = i * TQ + lax.broadcasted_iota(jnp.int32, (1, TQ), 1)

    def fold(s):
        out = s[0:SUBLANES]
        for t in range(1, s.shape[0] // SUBLANES):
            out = jnp.maximum(out, s[t * SUBLANES:(t + 1) * SUBLANES])
        return out

    def score_blocks(j, n, mrun):
        j0 = pl.multiple_of(j * TQ, TQ)
        visible = (j0 + lax.broadcasted_iota(jnp.int32, (n * TQ, TQ), 0)) <= q_pos
        out = []
        for h in range(nh):
            s = _dot(k_ref[pl.ds(j0, n * TQ), slot(h)], qt_ref[0, slot(h), :])
            s = jnp.where(visible, s, -jnp.inf)
            s_scr[h, pl.ds(j, n)] = s.reshape(n, TQ, TQ)
            out.append(jnp.maximum(mrun[h], fold(s)))
        return tuple(out)

    mrun = tuple(jnp.full((SUBLANES, TQ), -jnp.inf, F32) for _ in range(nh))
    mrun = lax.fori_loop(0, nquad, lambda jq, m: score_blocks(4 * jq, 4, m), mrun)
    mrun = lax.fori_loop(2 * nquad, pair_end // 2, lambda jj, m: score_blocks(2 * jj, 2, m), mrun)
    mrun = lax.fori_loop(pair_end, nblk, lambda j, m: score_blocks(j, 1, m), mrun)
    ms = [jnp.max(m, axis=0, keepdims=True) for m in mrun]

    def accumulate(j, n, carry):
        for h in range(nh):
            p = jnp.exp2(s_scr[h, pl.ds(j, n)].reshape(n * TQ, TQ) - ms[h]).astype(BF16)
            vt = jnp.concatenate([vt_ref[j + t, slot(h), :] for t in range(n)], axis=1)
            acc_scr[h] += _dot(vt, p)
        return carry

    acc_scr[...] = jnp.zeros_like(acc_scr)
    lax.fori_loop(0, nquad, lambda jq, c: accumulate(4 * jq, 4, c), 0)
    lax.fori_loop(2 * nquad, pair_end // 2, lambda jj, c: accumulate(2 * jj, 2, c), 0)
    lax.fori_loop(pair_end, nblk, lambda j, c: accumulate(j, 1, c), 0)
    outs = [acc_scr[h, 0:HEAD_LANES] / acc_scr[h, ONES_LANE:ONES_LANE + 1] for h in range(nh)]
    for jj in range(nh // 2):
        both = jnp.concatenate([outs[2 * jj], outs[2 * jj + 1]], axis=0)
        o_ref[:, jj * LANES:(jj + 1) * LANES] = both.T.astype(o_ref.dtype)


def _attention(qt, k, vt, B, T, name):
    N = k.shape[0]
    nq = T // TQ
    nh = ATTN_HEADS_PER_STEP
    heads = k.shape[1] // SLOT
    return pl.pallas_call(
        _attn_kernel,
        out_shape=jax.ShapeDtypeStruct((N, heads * HEAD_LANES), BF16),
        grid=(B, heads // nh, nq),
        in_specs=[pl.BlockSpec((1, nh * SLOT, TQ), lambda b, g, i: (b * nq + i, g, 0)),
                  pl.BlockSpec((T, nh * SLOT), lambda b, g, i: (b, g)),
                  pl.BlockSpec((nq, nh * SLOT, TQ), lambda b, g, i: (b, g, 0))],
        out_specs=pl.BlockSpec((TQ, nh * HEAD_LANES), lambda b, g, i: (b * nq + i, g)),
        scratch_shapes=[pltpu.VMEM((nh, nq, TQ, TQ), F32), pltpu.VMEM((nh, SLOT, TQ), F32)],
        compiler_params=pltpu.CompilerParams(dimension_semantics=("arbitrary", "arbitrary", "arbitrary"),
                                             vmem_limit_bytes=VMEM_LIMIT_BYTES),
        name=name,
    )(qt, k, vt)


def _oproj_kernel(alpha, a_ref, m_ref, x_ref, wo_ref, g_ref, b_ref, wr_ref, br_ref,
                  x1_ref, x1t_ref, route_ref, gates_ref, cnt_ref, carry_scr):
    i = pl.program_id(0)

    @pl.when(i == 0)
    def _():
        carry_scr[...] = jnp.zeros_like(carry_scr)

    mix = _dot(jnp.concatenate([a_ref[...], m_ref[...]], axis=1), wo_ref[...])
    x1 = _layer_norm(alpha * x_ref[...] + mix, g_ref[...], b_ref[...])
    x1_ref[...] = x1
    _pack_rows_to_tiles(x1t_ref, x1)

    ne = br_ref.shape[0]
    rows = x1.shape[0]
    x_hi = x1.astype(BF16)
    x_lo = (x1 - x_hi.astype(F32)).astype(BF16)
    both = _dot(x_hi, wr_ref[...])
    logits_rm = both[:, :LANES] + both[:, LANES:] + _dot(x_lo, wr_ref[:, :LANES])
    logits = logits_rm.T[0:ne] + br_ref[...]
    expert = lax.broadcasted_iota(jnp.int32, logits.shape, 0)
    expert_f = expert.astype(F32)
    vals, idxs = [], []
    work = logits
    for _ in range(MOE_TOPK):
        mx = jnp.max(work, axis=0, keepdims=True)
        ix = jnp.min(jnp.where(work == mx, expert_f, float(ne)), axis=0, keepdims=True).astype(jnp.int32)
        vals.append(mx)
        idxs.append(ix)
        work = jnp.where(expert == ix, -jnp.inf, work)
    exps = [jnp.exp(v - vals[0]) for v in vals]
    den = exps[0]
    for e in exps[1:]:
        den = den + e

    onehot = jnp.zeros(logits.shape, F32)
    for ix in idxs:
        onehot = onehot + (expert == ix).astype(F32)
    r = lax.broadcasted_iota(jnp.int32, (rows, rows), 0)
    cidx = lax.broadcasted_iota(jnp.int32, (rows, rows), 1)
    earlier = (r < cidx).astype(BF16)
    carry = carry_scr[:, 0:1]
    before = _dot(onehot.astype(BF16), earlier) + carry
    row8 = lax.broadcasted_iota(jnp.int32, (SUBLANES, rows), 0)
    route = jnp.zeros((SUBLANES, rows), jnp.int32)
    gates = jnp.zeros((SUBLANES, rows), F32)
    for k in range(MOE_TOPK):
        rank = jnp.sum(jnp.where(expert == idxs[k], before, 0.0), axis=0, keepdims=True).astype(jnp.int32)
        route = jnp.where(row8 == k, idxs[k], route)
        route = jnp.where(row8 == MOE_TOPK + k, rank, route)
        gates = jnp.where(row8 == k, exps[k] / den, gates)
    route_ref[...] = route
    gates_ref[...] = jnp.concatenate([gates, jnp.zeros((LANES - SUBLANES, rows), F32)], axis=0).T
    new_carry = carry + jnp.sum(onehot, axis=1, keepdims=True)
    carry_scr[...] = jnp.broadcast_to(new_carry, carry_scr.shape)
    cnt_ref[...] = jnp.broadcast_to(new_carry, cnt_ref.shape)


def _oproj(a, m, x2, wo, g1, b1, wr, br, alpha):
    N, D = x2.shape
    nt = N // OPROJ_TILE
    row = lambda i: (i, 0)
    full = lambda i: (0, 0)
    return pl.pallas_call(
        functools.partial(_oproj_kernel, alpha),
        out_shape=(jax.ShapeDtypeStruct((N, D), F32),
                   jax.ShapeDtypeStruct((N * PACKED_SUB, LANES), jnp.uint32),
                   jax.ShapeDtypeStruct((SUBLANES, N), jnp.int32),
                   jax.ShapeDtypeStruct((N, LANES), F32),
                   jax.ShapeDtypeStruct((br.shape[0], LANES), F32)),
        grid=(nt,),
        in_specs=[pl.BlockSpec((OPROJ_TILE, a.shape[1]), row), pl.BlockSpec((OPROJ_TILE, m.shape[1]), row),
                  pl.BlockSpec((OPROJ_TILE, D), row), pl.BlockSpec(wo.shape, full),
                  pl.BlockSpec(g1.shape, full), pl.BlockSpec(b1.shape, full),
                  pl.BlockSpec(wr.shape, full), pl.BlockSpec(br.shape, full)],
        out_specs=(pl.BlockSpec((OPROJ_TILE, D), row), pl.BlockSpec((OPROJ_TILE * PACKED_SUB, LANES), row),
                   pl.BlockSpec((SUBLANES, OPROJ_TILE), lambda i: (0, i)),
                   pl.BlockSpec((OPROJ_TILE, LANES), row), pl.BlockSpec((br.shape[0], LANES), full)),
        scratch_shapes=[pltpu.VMEM((br.shape[0], LANES), F32)],
        compiler_params=pltpu.CompilerParams(dimension_semantics=("arbitrary",),
                                             vmem_limit_bytes=VMEM_LIMIT_BYTES),
        name="oproj_router",
    )(a, m, x2, wo, g1, b1, wr, br)


def _dispatch_kernel(dest_ref, fill_ref, x1_ref, xs_ref, zero_scr, sem, fill_sem):
    i = pl.program_id(0)
    fill_sub = MOE_HALF * PACKED_SUB
    nfill = xs_ref.shape[0] // fill_sub

    def fill_copy(blk, on):
        return pltpu.make_async_copy(zero_scr, xs_ref.at[pl.ds(pl.multiple_of(blk * fill_sub, fill_sub), fill_sub)], on)

    def for_blocks(kind, action):
        def body(blk, _):
            @pl.when(fill_ref[blk] == kind)
            def _():
                action(blk)
            return 0

        lax.fori_loop(0, nfill, body, 0)

    @pl.when(i == 0)
    def _():
        zero_scr[...] = jnp.zeros_like(zero_scr)
        for_blocks(FILL_TAIL, lambda blk: fill_copy(blk, sem).start())
        for_blocks(FILL_FREE, lambda blk: fill_copy(blk, fill_sem).start())
        for_blocks(FILL_TAIL, lambda blk: fill_copy(blk, sem).wait())

    base = i * DISPATCH_TILE
    n_tok = pl.num_programs(0) * DISPATCH_TILE

    def row_copy(t, k):
        d = dest_ref[k * n_tok + base + t]
        return pltpu.make_async_copy(x1_ref.at[pl.ds(pl.multiple_of(t * PACKED_SUB, PACKED_SUB), PACKED_SUB)],
                                     xs_ref.at[pl.ds(pl.multiple_of(d * PACKED_SUB, PACKED_SUB), PACKED_SUB)], sem)

    def start_rows(t, _):
        for k in range(MOE_TOPK):
            row_copy(t, k).start(priority=k % DMA_PRIORITIES)
        return 0

    lax.fori_loop(0, DISPATCH_TILE, start_rows, 0, unroll=ISSUE_UNROLL)
    for k in range(MOE_TOPK):
        pltpu.make_async_copy(x1_ref, xs_ref.at[pl.ds(0, DISPATCH_TILE * PACKED_SUB)], sem).wait()

    @pl.when(i == pl.num_programs(0) - 1)
    def _():
        for_blocks(FILL_FREE, lambda blk: fill_copy(blk, fill_sem).wait())


def _dispatch(dest, fill, x1t, n_rows):
    return pl.pallas_call(
        _dispatch_kernel,
        out_shape=jax.ShapeDtypeStruct((n_rows * PACKED_SUB, LANES), x1t.dtype),
        grid_spec=pltpu.PrefetchScalarGridSpec(
            num_scalar_prefetch=2,
            grid=(x1t.shape[0] // (DISPATCH_TILE * PACKED_SUB),),
            in_specs=[pl.BlockSpec((DISPATCH_TILE * PACKED_SUB, LANES), lambda i, d, t: (i, 0))],
            out_specs=pl.BlockSpec(memory_space=pl.ANY),
            scratch_shapes=[pltpu.VMEM((MOE_HALF * PACKED_SUB, LANES), x1t.dtype), pltpu.SemaphoreType.DMA(()),
                            pltpu.SemaphoreType.DMA(())],
        ),
        compiler_params=pltpu.CompilerParams(dimension_semantics=("arbitrary",),
                                             vmem_limit_bytes=VMEM_LIMIT_BYTES),
        name="dispatch",
    )(dest, fill, x1t)


def _experts_kernel(be_ref, slot_ref, nxt_ref, full_ref, nused_ref, x_ref, wg_hbm, bg_ref, wu_hbm, bu_ref, wd_hbm, bd_ref,
                    y_ref, wg_st, wu_st, wd_st, wg_bf, wu_bf, wd_bf, sems):
    i = pl.program_id(0)
    prev = be_ref[jnp.maximum(i - 1, 0)]
    changed = (i == 0) | (be_ref[i] != prev)
    active = i < nused_ref[0]
    slot = slot_ref[i]

    def weight_copies(expert, s):
        return (pltpu.make_async_copy(wg_hbm.at[expert], wg_st.at[s], sems.at[s, 0]),
                pltpu.make_async_copy(wu_hbm.at[expert], wu_st.at[s], sems.at[s, 1]),
                pltpu.make_async_copy(wd_hbm.at[expert], wd_st.at[s], sems.at[s, 2]))

    @pl.when(i == 0)
    def _():
        for cp in weight_copies(be_ref[0], slot):
            cp.start()

    @pl.when(active & changed)
    def _():
        for cp in weight_copies(be_ref[i], slot):
            cp.wait()

        @pl.when(nxt_ref[i] >= 0)
        def _():
            for cp in weight_copies(nxt_ref[i], 1 - slot):
                cp.start(priority=WEIGHT_DMA_PRIORITY)

        wg_bf[...] = wg_st[slot].astype(BF16)
        wu_bf[...] = wu_st[slot].astype(BF16)
        wd_bf[...] = wd_st[slot].astype(BF16)

    def mlp(nrows):
        rows = pl.ds(0, nrows * SUBLANES)
        xb = _unpack_tiles_to_rows(x_ref.at[pl.ds(0, nrows * PACKED_SUB)], nrows)
        g = jnp.minimum(_dot(xb, wg_bf[...]) + bg_ref[0], SWIGLU_LIMIT)
        u = jnp.clip(_dot(xb, wu_bf[...]) + bu_ref[0], -SWIGLU_LIMIT, SWIGLU_LIMIT)
        h = g * (1.0 / (1.0 + jnp.exp(-SWIGLU_ALPHA * g))) * (u + 1.0)
        _rows_to_tiles(y_ref.at[rows], _dot(h.astype(BF16), wd_bf[...]) + bd_ref[0])

    whole = active & (full_ref[i] > 0)

    @pl.when(whole)
    def _():
        mlp(MOE_GROUP)

    @pl.when(active & jnp.logical_not(whole))
    def _():
        mlp(MOE_HALF)

    @pl.when(jnp.logical_not(active))
    def _():
        y_ref[pl.ds(0, HALF_SUB), :] = jnp.zeros((HALF_SUB, LANES), F32)

    @pl.when(jnp.logical_not(whole))
    def _():
        y_ref[pl.ds(HALF_SUB, HALF_SUB), :] = jnp.zeros((HALF_SUB, LANES), F32)


def _experts(blk_expert, blk_slot, blk_next, blk_full, n_used, xs, wg, bg, wu, bu, wd, bd):
    E, D, F = wg.shape
    nblk = xs.shape[0] // (MOE_GROUP * PACKED_SUB)

    def rowmap(i, be, sl, nx, fl, nu):
        return (jnp.minimum(i, nu[0] - 1), 0)

    def bmap(i, be, sl, nx, fl, nu):
        return (be[i], 0, 0)

    hbm = pl.BlockSpec(memory_space=pl.ANY)
    return pl.pallas_call(
        _experts_kernel,
        out_shape=jax.ShapeDtypeStruct((nblk * GROUP_SUB, LANES), F32),
        grid_spec=pltpu.PrefetchScalarGridSpec(
            num_scalar_prefetch=5,
            grid=(nblk,),
            in_specs=[pl.BlockSpec((MOE_GROUP * PACKED_SUB, LANES), rowmap),
                      hbm, pl.BlockSpec((1, 1, F), bmap),
                      hbm, pl.BlockSpec((1, 1, F), bmap),
                      hbm, pl.BlockSpec((1, 1, D), bmap)],
            out_specs=pl.BlockSpec((GROUP_SUB, LANES), lambda i, be, sl, nx, fl, nu: (i, 0)),
            scratch_shapes=[pltpu.VMEM((2, D, F), F32), pltpu.VMEM((2, D, F), F32), pltpu.VMEM((2, F, D), F32),
                            pltpu.VMEM((D, F), BF16), pltpu.VMEM((D, F), BF16), pltpu.VMEM((F, D), BF16),
                            pltpu.SemaphoreType.DMA((2, 3))],
        ),
        compiler_params=pltpu.CompilerParams(dimension_semantics=("arbitrary",),
                                             vmem_limit_bytes=VMEM_LIMIT_BYTES),
        name="experts",
    )(blk_expert, blk_slot, blk_next, blk_full, n_used, xs, wg, bg, wu, bu, wd, bd)


def _combine_kernel(alpha, dest_ref, gates_ref, x1_ref, g_ref, b_ref, y_ref, o_ref, ybuf, sems):
    i = pl.program_id(0)
    buf = i % 2

    n_tok = pl.num_programs(0) * COMBINE_TILE

    def gather_tile(tile, into):
        base = tile * COMBINE_TILE

        def start_rows(t, _):
            for k in range(MOE_TOPK):
                d = dest_ref[k * n_tok + base + t]
                pltpu.make_async_copy(y_ref.at[pl.ds(pl.multiple_of(d * SUBLANES, SUBLANES), SUBLANES)],
                                      ybuf.at[into, k, pl.ds(pl.multiple_of(t * SUBLANES, SUBLANES), SUBLANES)],
                                      sems.at[into]).start(priority=k % DMA_PRIORITIES)
            return 0

        lax.fori_loop(0, COMBINE_TILE, start_rows, 0, unroll=ISSUE_UNROLL)

    @pl.when(i == 0)
    def _():
        gather_tile(0, 0)

    @pl.when(i + 1 < pl.num_programs(0))
    def _():
        gather_tile(i + 1, 1 - buf)

    for k in range(MOE_TOPK):
        pltpu.make_async_copy(y_ref.at[pl.ds(0, COMBINE_TILE * SUBLANES)], ybuf.at[buf, k], sems.at[buf]).wait()

    gates = gates_ref[...]
    ffn = gates[:, 0:1] * _tiles_to_rows(ybuf.at[buf, 0], COMBINE_TILE)
    for k in range(1, MOE_TOPK):
        ffn = ffn + gates[:, k:k + 1] * _tiles_to_rows(ybuf.at[buf, k], COMBINE_TILE)
    o_ref[...] = _layer_norm(alpha * x1_ref[...] + ffn, g_ref[...], b_ref[...])


def _combine(dest, gates, x1, g2, b2, y_rows, alpha):
    N, D = x1.shape
    row = lambda i, d: (i, 0)
    full = lambda i, d: (0, 0)
    return pl.pallas_call(
        functools.partial(_combine_kernel, alpha),
        out_shape=jax.ShapeDtypeStruct((N, D), F32),
        grid_spec=pltpu.PrefetchScalarGridSpec(
            num_scalar_prefetch=1,
            grid=(N // COMBINE_TILE,),
            in_specs=[pl.BlockSpec((COMBINE_TILE, LANES), row), pl.BlockSpec((COMBINE_TILE, D), row),
                      pl.BlockSpec(g2.shape, full), pl.BlockSpec(b2.shape, full),
                      pl.BlockSpec(memory_space=pl.ANY)],
            out_specs=pl.BlockSpec((COMBINE_TILE, D), row),
            scratch_shapes=[pltpu.VMEM((2, MOE_TOPK, COMBINE_TILE * SUBLANES, LANES), F32),
                            pltpu.SemaphoreType.DMA((2,))],
        ),
        compiler_params=pltpu.CompilerParams(dimension_semantics=("arbitrary",),
                                             vmem_limit_bytes=VMEM_LIMIT_BYTES),
        name="combine",
    )(dest, gates, x1, g2, b2, y_rows)


def _rot_partner(w, half):
    return jnp.concatenate([-w[..., half:2 * half], w[..., :half]], axis=-1)


def _layer_weights(w_in, w_q_b, w_kv_b):
    D = w_in.shape[0]
    o1 = Q_LORA
    o2 = o1 + KV_LORA
    o3 = o2 + MLA_ROPE
    w_ql, w_kvl, w_kr = w_in[:, :o1], w_in[:, o1:o2], w_in[:, o2:o3]
    w_mq, w_mk, w_mv = w_in[:, o3:o3 + _MH], w_in[:, o3 + _MH:o3 + 2 * _MH], w_in[:, o3 + 2 * _MH:]
    zpad = lambda n: jnp.zeros((D, n), w_in.dtype)
    tail = SLOT - MLA_NOPE - MLA_ROPE
    kr_slot = jnp.concatenate([zpad(MLA_NOPE), w_kr, zpad(tail)], axis=1)
    krs_slot = jnp.concatenate([zpad(MLA_NOPE), _rot_partner(w_kr, MLA_ROPE // 2), zpad(tail)], axis=1)

    def moba_partner(w):
        w3 = w.reshape(D, MOBA_HEADS, MOBA_HD)
        part = jnp.concatenate([_rot_partner(w3[..., :MOBA_ROT], MOBA_ROT // 2),
                                jnp.zeros((D, MOBA_HEADS, MOBA_HD - MOBA_ROT), w.dtype)], axis=-1)
        return part.reshape(D, _MH)

    win = jnp.concatenate([w_ql, w_kvl, kr_slot, krs_slot, w_mk, moba_partner(w_mk)], axis=1).astype(BF16)
    mq_partner = _rot_partner(w_mq.reshape(D, MOBA_HEADS, MOBA_HD)[..., :MOBA_ROT], MOBA_ROT // 2)
    wmt = jnp.concatenate([w_mq, mq_partner.reshape(D, MOBA_HEADS * MOBA_ROT), w_mv], axis=1).T.astype(BF16)

    wq3 = w_q_b.reshape(Q_LORA, MLA_HEADS, MLA_NOPE + MLA_ROPE)
    wqs = _rot_partner(wq3[..., MLA_NOPE:], MLA_ROPE // 2).reshape(Q_LORA, MLA_HEADS * MLA_ROPE)
    wkv3 = w_kv_b.reshape(KV_LORA, MLA_HEADS, MLA_NOPE + MLA_V)
    wk = jnp.concatenate([wkv3[..., :MLA_NOPE], jnp.zeros((KV_LORA, MLA_HEADS, SLOT - MLA_NOPE), w_kv_b.dtype)],
                         axis=-1).reshape(KV_LORA, MLA_HEADS * SLOT).astype(BF16)
    wv = wkv3[..., MLA_NOPE:].reshape(KV_LORA, MLA_HEADS * MLA_V)
    return win, wmt, w_q_b.T.astype(BF16), wqs.T.astype(BF16), wk, wv.T.astype(BF16)


def _rope_tables(positions):
    pos = positions.astype(F32).reshape(1, -1)

    def cs(d_rot):
        inv_freq = ROPE_THETA ** (-jnp.arange(0, d_rot, 2, dtype=F32) / d_rot)
        ang = inv_freq.reshape(d_rot // 2, 1) * pos
        return jnp.cos(ang), jnp.sin(ang)

    return cs(MLA_ROPE) + cs(MOBA_ROT)


def kernel(x, positions, w_in, q_a_norm, w_q_b, kv_a_norm, w_kv_b, w_o, ln1_g, ln1_b, w_router, b_router,
           w_gate, b_gate, w_up, b_up, w_down, b_down, ln2_g, ln2_b):
    B, T, D = x.shape
    depth = w_in.shape[0]
    alpha = (2.0 * depth) ** 0.25
    N = B * T
    assert T % MOBA_BLOCK == 0 and T // MOBA_BLOCK <= SUBLANES and N % OPROJ_TILE == 0
    assert D == SUBLANES * LANES
    n_asg = N * MOE_TOPK
    n_rows = n_asg + N_EXPERTS * MOE_GROUP
    tables = _rope_tables(positions)
    h = x.reshape(N, D)
    for l in range(depth):
        win, wmt, wqt, wqst, wk, wvt = _layer_weights(w_in[l], w_q_b[l], w_kv_b[l])
        qa, ka, va, mq, mk, mv = _prep(h, win, wmt, wqt, wqst, wk, wvt, q_a_norm[l].reshape(1, -1),
                                       kv_a_norm[l].reshape(1, -1), tables, B, T)
        a = _attention(qa, ka, va, B, T, "mla_attention")
        m = _attention(mq, mk, mv, B, T, "moba_attention")
        wr_pad = jnp.concatenate([w_router[l], jnp.zeros((D, LANES - N_EXPERTS), F32)], axis=1)
        wr_hi = wr_pad.astype(BF16)
        wr = jnp.concatenate([wr_hi, (wr_pad - wr_hi.astype(F32)).astype(BF16)], axis=1)
        br = b_router[l].reshape(N_EXPERTS, 1)
        x1, x1t, route, gates, cnt = _oproj(a, m, h, w_o[l].astype(BF16), ln1_g[l].reshape(1, D), ln1_b[l].reshape(1, D),
                                       wr, br, alpha)
        er = jnp.arange(N_EXPERTS, dtype=jnp.int32)
        counts = cnt[:, 0].astype(jnp.int32)
        padded = (counts + MOE_GROUP - 1) // MOE_GROUP * MOE_GROUP
        upto = er[None, :] <= er[:, None]
        pad_end = jnp.sum(jnp.where(upto, padded[None, :], 0), axis=1)
        pad_start = pad_end - padded
        e_idx = route[:MOE_TOPK]
        group_start = jnp.sum(jnp.where(e_idx[..., None] == er, pad_start, 0), axis=-1)
        dest = (group_start + route[MOE_TOPK:2 * MOE_TOPK]).reshape(n_asg)
        n_used = (pad_end[-1:] // MOE_GROUP).astype(jnp.int32)
        nblk = n_rows // MOE_GROUP
        blk_start = jnp.arange(nblk, dtype=jnp.int32) * MOE_GROUP
        blk_expert = jnp.minimum(jnp.sum((blk_start[:, None] >= pad_end[None, :]).astype(jnp.int32), axis=1),
                                 N_EXPERTS - 1)
        of_expert = blk_expert[:, None] == er[None, :]
        pick = lambda per_expert: jnp.sum(jnp.where(of_expert, per_expert[None, :], 0), axis=1)
        nonempty = padded > 0
        group_index = jnp.sum(jnp.where(upto & nonempty[None, :], 1, 0), axis=1) - 1
        later = (er[None, :] > er[:, None]) & nonempty[None, :]
        next_expert = jnp.min(jnp.where(later, er[None, :], N_EXPERTS), axis=1)
        next_expert = jnp.where(next_expert == N_EXPERTS, -1, next_expert)
        blk_slot = pick(group_index) % 2
        blk_next = pick(next_expert)
        blk_full = (pick(pad_start + counts) > blk_start + MOE_HALF).astype(jnp.int32)
        half_start = jnp.arange(n_rows // MOE_HALF, dtype=jnp.int32) * MOE_HALF
        half_expert = jnp.minimum(jnp.sum((half_start[:, None] >= pad_end[None, :]).astype(jnp.int32), axis=1),
                                  N_EXPERTS - 1)
        half_real_end = jnp.sum(jnp.where(half_expert[:, None] == er[None, :], (pad_start + counts)[None, :], 0), axis=1)
        no_real_rows = (half_start >= pad_end[-1]) | (half_start >= half_real_end)
        fill = jnp.where(no_real_rows, FILL_FREE,
                         jnp.where(half_start + MOE_HALF > half_real_end, FILL_TAIL, 0)).astype(jnp.int32)
        xs = _dispatch(dest, fill, x1t, n_rows)
        y_rows = _experts(blk_expert, blk_slot.astype(jnp.int32), blk_next, blk_full, n_used, xs, w_gate[l], b_gate[l].reshape(N_EXPERTS, 1, -1),
                          w_up[l], b_up[l].reshape(N_EXPERTS, 1, -1), w_down[l], b_down[l].reshape(N_EXPERTS, 1, -1))
        h = _combine(dest, gates, x1, ln2_g[l].reshape(1, D), ln2_b[l].reshape(1, D), y_rows, alpha)
    return h.reshape(B, T, D)
```

```python
import functools
import math

import jax
import jax.numpy as jnp
from jax import lax
from jax.experimental import pallas as pl
from jax.experimental.pallas import tpu as pltpu

ROPE_THETA = 500000.0
MLA_HEADS = 8
MLA_NOPE = 64
MLA_ROPE = 32
MLA_V = 64
Q_LORA = 256
KV_LORA = 128
MOBA_HEADS = 8
MOBA_HD = 64
MOBA_ROT = MOBA_HD // 4
MOBA_BLOCK = 256
MOBA_TOPK = 3
N_EXPERTS = 32
MOE_TOPK = 4
SWIGLU_LIMIT = 7.0
SWIGLU_ALPHA = 1.702
RMS_EPS = 1e-6
LN_EPS = 1e-5

LANES = 128
SUBLANES = 8
VMEM_LIMIT_BYTES = 56 * 1024 * 1024

SLOT = LANES
TQ = MOBA_BLOCK
ROW_TILE = 256
OPROJ_TILE = 512
MOE_GROUP = 512
MOE_HALF = MOE_GROUP // 2
GROUP_SUB = MOE_GROUP * SUBLANES
HALF_SUB = MOE_HALF * SUBLANES
PACKED_SUB = SUBLANES // 2
FILL_TAIL, FILL_FREE = 1, 2
DISPATCH_TILE = 512
COMBINE_TILE = 512
ISSUE_UNROLL = 4
DMA_PRIORITIES = 2
WEIGHT_DMA_PRIORITY = 1
HEAD_LANES = 64
ONES_LANE = 64
BIAS_LANE = 64
ATTN_HEADS_PER_STEP = 8
LOG2E = math.log2(math.e)
NEG_BIG = -(2.0 ** 100)

F32 = jnp.float32
BF16 = jnp.bfloat16
NT_DIMS = (((1,), (1,)), ((), ()))


def _dot(a, b, precision=None):
    return jnp.dot(a, b, preferred_element_type=F32, precision=precision)


def _dot_nt(a, b, precision=None):
    return lax.dot_general(a, b, NT_DIMS, preferred_element_type=F32, precision=precision)


def _rows_to_tiles(ref, x):
    rows = x.shape[0]
    for c in range(SUBLANES):
        ref[pl.ds(c, rows, stride=SUBLANES), :] = x[:, c * LANES:(c + 1) * LANES]


def _tiles_to_rows(ref, rows):
    return jnp.concatenate([ref[pl.ds(c, rows, stride=SUBLANES), :] for c in range(SUBLANES)], axis=1)


def _pack_rows_to_tiles(ref, x):
    rows, width = x.shape
    bits = lambda v: lax.bitcast_convert_type(v.astype(BF16).astype(F32), jnp.uint32)
    words = (bits(x[:, :width // 2]) >> 16) | (bits(x[:, width // 2:]) & jnp.uint32(0xFFFF0000))
    for c in range(PACKED_SUB):
        ref[pl.ds(c, rows, stride=PACKED_SUB), :] = words[:, c * LANES:(c + 1) * LANES]


def _unpack_tiles_to_rows(ref, rows):
    words = jnp.concatenate([ref[pl.ds(c, rows, stride=PACKED_SUB), :] for c in range(PACKED_SUB)], axis=1)
    low = lax.bitcast_convert_type(words << 16, F32).astype(BF16)
    high = lax.bitcast_convert_type(words & jnp.uint32(0xFFFF0000), F32).astype(BF16)
    return jnp.concatenate([low, high], axis=1)


def _rms(x, g):
    return x * lax.rsqrt(jnp.mean(x * x, axis=-1, keepdims=True) + RMS_EPS) * g


def _layer_norm(x, g, b):
    mu = jnp.mean(x, axis=-1, keepdims=True)
    xc = x - mu
    var = jnp.mean(xc * xc, axis=-1, keepdims=True)
    return xc * lax.rsqrt(var + LN_EPS) * g + b


_C_QL = 0
_C_KVL = _C_QL + Q_LORA
_C_KR = _C_KVL + KV_LORA
_C_MK = _C_KR + SLOT
_MH = MOBA_HEADS * MOBA_HD
_C_MKS = _C_MK + _MH
_C_END = _C_MKS + _MH


def _prep_kernel(x_ref, win_ref, wmt_ref, wqt_ref, wqst_ref, wk_ref, wvt_ref, qg_ref, kvg_ref,
                 cat_ref, sat_ref, cbt_ref, sbt_ref,
                 qat_ref, ka_ref, vat_ref, mqt_ref, mk_ref, mvt_ref, kmean_scr):
    c = pl.program_id(1)

    @pl.when(c == 0)
    def _():
        kmean_scr[...] = jnp.zeros_like(kmean_scr)

    xb = x_ref[...].astype(BF16)
    both_halves = lambda t: jnp.concatenate([t, t], axis=0)
    cat = both_halves(cat_ref[...])
    sat = both_halves(sat_ref[...])
    cbt = both_halves(cbt_ref[...])
    sbt = both_halves(sbt_ref[...])
    const = lambda v, n: jnp.full((n, ROW_TILE), v, F32)
    tail = SLOT - MLA_NOPE - MLA_ROPE
    rest = MOBA_HD - MOBA_ROT
    ca = jnp.concatenate([const(1.0, MLA_NOPE), cat, const(0.0, tail)], axis=0).T
    sa = jnp.concatenate([const(0.0, MLA_NOPE), sat, const(0.0, tail)], axis=0).T
    cb = jnp.concatenate([cbt, const(1.0, rest), cbt, const(1.0, rest)], axis=0).T
    sb = jnp.concatenate([sbt, const(0.0, rest), sbt, const(0.0, rest)], axis=0).T
    lane = lax.broadcasted_iota(jnp.int32, (ROW_TILE, SLOT), 1)
    head_lanes = lane < HEAD_LANES
    ones_rows = (lax.broadcasted_iota(jnp.int32, (SLOT - HEAD_LANES, ROW_TILE), 0) == ONES_LANE - HEAD_LANES).astype(F32)

    ql = _dot(xb, win_ref[:, _C_QL:_C_KVL])
    kvl = _dot(xb, win_ref[:, _C_KVL:_C_KR])
    kr = _dot(xb, win_ref[:, _C_KR:_C_MK])
    krs = pltpu.roll(kr, SLOT - MLA_ROPE, axis=1)
    qn = _rms(ql, qg_ref[...]).astype(BF16)
    kvn = _rms(kvl, kvg_ref[...]).astype(BF16)
    q_t = _dot_nt(wqt_ref[...], qn)
    qs_t = _dot_nt(wqst_ref[...], qn)
    kn = _dot(kvn, wk_ref[...])
    v_t = _dot_nt(wvt_ref[...], kvn)
    scale_a = LOG2E / math.sqrt(MLA_NOPE + MLA_ROPE)
    kro = kr * ca + krs * sa
    qd = MLA_NOPE + MLA_ROPE
    q_pad = jnp.zeros((SLOT - qd, ROW_TILE), F32)
    for h in range(MLA_HEADS):
        sl = slice(h * SLOT, (h + 1) * SLOT)
        q_rope = q_t[h * qd + MLA_NOPE:(h + 1) * qd] * cat + qs_t[h * MLA_ROPE:(h + 1) * MLA_ROPE] * sat
        q_slot = jnp.concatenate([q_t[h * qd:h * qd + MLA_NOPE], q_rope, q_pad], axis=0)
        qat_ref[0, sl, :] = (q_slot * scale_a).astype(BF16)
        ka_ref[:, sl] = (kn[:, sl] + kro).astype(BF16)
        vat_ref[0, sl, :] = jnp.concatenate([v_t[h * MLA_V:(h + 1) * MLA_V], ones_rows], axis=0).astype(BF16)

    mk = _dot(xb, win_ref[:, _C_MK:_C_MKS])
    mks = _dot(xb, win_ref[:, _C_MKS:_C_END])
    npair = _MH // LANES
    pair = lambda a, j: a[:, j * LANES:(j + 1) * LANES]
    mk_rot = [pair(mk, j) * cb + pair(mks, j) * sb for j in range(npair)]
    mk_all = jnp.concatenate(mk_rot, axis=1)

    nrot = MOBA_HEADS * MOBA_ROT
    mq_t = _dot_nt(wmt_ref[0:_MH, :], xb)
    mqs_t = _dot_nt(wmt_ref[_MH:_MH + nrot, :], xb)
    mv_t = _dot_nt(wmt_ref[_MH + nrot:2 * _MH + nrot, :], xb)
    head = lambda a, h: a[h * MOBA_HD:(h + 1) * MOBA_HD]
    mq_rot_t = [jnp.concatenate([head(mq_t, h)[0:MOBA_ROT] * cbt + mqs_t[h * MOBA_ROT:(h + 1) * MOBA_ROT] * sbt,
                                 head(mq_t, h)[MOBA_ROT:]], axis=0) for h in range(MOBA_HEADS)]

    nrow = MOBA_HEADS * SUBLANES
    row_i = lax.broadcasted_iota(jnp.int32, (nrow, _MH), 0)
    lane_i = lax.broadcasted_iota(jnp.int32, (nrow, _MH), 1)
    kmean_c = jnp.mean(mk_all, axis=0, keepdims=True)
    put = ((row_i % SUBLANES) == c) & ((lane_i // MOBA_HD) == (row_i // SUBLANES))
    table = kmean_scr[...]
    gate_t = _dot(table, jnp.concatenate(mq_rot_t, axis=0), precision=lax.Precision.HIGHEST)
    kmean_scr[...] = jnp.where(put, jnp.broadcast_to(kmean_c, (nrow, _MH)), table)

    n_idx = lax.broadcasted_iota(jnp.int32, (SUBLANES, ROW_TILE), 0)
    valid = n_idx < c
    scale_b = LOG2E / math.sqrt(MOBA_HD)
    zero_rows = jnp.zeros((SLOT - HEAD_LANES - SUBLANES, ROW_TILE), F32)
    block_onehot = (lane == BIAS_LANE + c).astype(F32)
    for h in range(MOBA_HEADS):
        g = jnp.where(valid, gate_t[h * SUBLANES:(h + 1) * SUBLANES, :], -jnp.inf)
        rank = jnp.zeros((SUBLANES, ROW_TILE), jnp.int32)
        for k in range(1, SUBLANES):
            other = pltpu.roll(g, k, axis=0)
            other_n = pltpu.roll(n_idx, k, axis=0)
            beats = (other > g) | ((other == g) & (other_n < n_idx))
            rank = rank + beats.astype(jnp.int32)
        keep = (valid & (rank < MOBA_TOPK)) | (n_idx == c)
        bias = jnp.where(keep, 0.0, NEG_BIG)
        sl = slice(h * SLOT, (h + 1) * SLOT)
        mqt_ref[0, sl, :] = jnp.concatenate([mq_rot_t[h] * scale_b, bias, zero_rows], axis=0).astype(BF16)
        mvt_ref[0, sl, :] = jnp.concatenate([head(mv_t, h), ones_rows], axis=0).astype(BF16)
        j, hh = divmod(h, 2)
        k_h = mk_rot[j] if hh == 0 else pltpu.roll(mk_rot[j], HEAD_LANES, axis=1)
        mk_ref[:, sl] = jnp.where(head_lanes, k_h, block_onehot).astype(BF16)


def _prep(x2, win, wmt, wqt, wqst, wk, wvt, qg, kvg, tables, B, T):
    N, D = x2.shape
    nt = T // ROW_TILE
    row = lambda b, c: (b * nt + c, 0)
    col = lambda b, c: (0, b * nt + c)
    full = lambda b, c: (0, 0)
    cat, sat, cbt, sbt = tables
    width = MLA_HEADS * SLOT

    def rows(w):
        return pl.BlockSpec((ROW_TILE, w), row)

    def cols(a):
        return pl.BlockSpec((a.shape[0], ROW_TILE), col)

    def whole(a):
        return pl.BlockSpec(a.shape, full)

    rowmajor = jax.ShapeDtypeStruct((N, width), BF16)
    transposed = jax.ShapeDtypeStruct((B * nt, width, ROW_TILE), BF16)
    t_spec = pl.BlockSpec((1, width, ROW_TILE), lambda b, c: (b * nt + c, 0, 0))
    return pl.pallas_call(
        _prep_kernel,
        out_shape=(transposed, rowmajor, transposed, transposed, rowmajor, transposed),
        grid=(B, nt),
        in_specs=[rows(D), whole(win), whole(wmt), whole(wqt), whole(wqst), whole(wk), whole(wvt), whole(qg), whole(kvg),
                  cols(cat), cols(sat), cols(cbt), cols(sbt)],
        out_specs=(t_spec, rows(width), t_spec, t_spec, rows(width), t_spec),
        scratch_shapes=[pltpu.VMEM((MOBA_HEADS * SUBLANES, _MH), F32)],
        compiler_params=pltpu.CompilerParams(dimension_semantics=("arbitrary", "arbitrary"),
                                             vmem_limit_bytes=VMEM_LIMIT_BYTES),
        name="prep",
    )(x2, win, wmt, wqt, wqst, wk, wvt, qg, kvg, cat, sat, cbt, sbt)


def _attn_kernel(qt_ref, k_ref, vt_ref, o_ref, s_scr, acc_scr):
    i = pl.program_id(2)
    nh = ATTN_HEADS_PER_STEP
    slot = lambda h: slice(h * SLOT, (h + 1) * SLOT)
    nblk = i + 1
    nquad = nblk // 4
    pair_end = 4 * nquad + 2 * ((nblk - 4 * nquad) // 2)
    q_pos = i * TQ + lax.broadcasted_iota(jnp.int32, (1, TQ), 1)

    def fold(s):
        out = s[0:SUBLANES]
        for t in range(1, s.shape[0] // SUBLANES):
            out = jnp.maximum(out, s[t * SUBLANES:(t + 1) * SUBLANES])
        return out

    def score_blocks(j, n, mrun):
        j0 = pl.multiple_of(j * TQ, TQ)
        visible = (j0 + lax.broadcasted_iota(jnp.int32, (n * TQ, TQ), 0)) <= q_pos
        out = []
        for h in range(nh):
            s = _dot(k_ref[pl.ds(j0, n * TQ), slot(h)], qt_ref[0, slot(h), :])
            s = jnp.where(visible, s, -jnp.inf)
            s_scr[h, pl.ds(j, n)] = s.reshape(n, TQ, TQ)
            out.append(jnp.maximum(mrun[h], fold(s)))
        return tuple(out)

    mrun = tuple(jnp.full((SUBLANES, TQ), -jnp.inf, F32) for _ in range(nh))
    mrun = lax.fori_loop(0, nquad, lambda jq, m: score_blocks(4 * jq, 4, m), mrun)
    mrun = lax.fori_loop(2 * nquad, pair_end // 2, lambda jj, m: score_blocks(2 * jj, 2, m), mrun)
    mrun = lax.fori_loop(pair_end, nblk, lambda j, m: score_blocks(j, 1, m), mrun)
    ms = [jnp.max(m, axis=0, keepdims=True) for m in mrun]

    def accumulate(j, n, carry):
        for h in range(nh):
            p = jnp.exp2(s_scr[h, pl.ds(j, n)].reshape(n * TQ, TQ) - ms[h]).astype(BF16)
            vt = jnp.concatenate([vt_ref[j + t, slot(h), :] for t in range(n)], axis=1)
            acc_scr[h] += _dot(vt, p)
        return carry

    acc_scr[...] = jnp.zeros_like(acc_scr)
    lax.fori_loop(0, nquad, lambda jq, c: accumulate(4 * jq, 4, c), 0)
    lax.fori_loop(2 * nquad, pair_end // 2, lambda jj, c: accumulate(2 * jj, 2, c), 0)
    lax.fori_loop(pair_end, nblk, lambda j, c: accumulate(j, 1, c), 0)
    outs = [acc_scr[h, 0:HEAD_LANES] / acc_scr[h, ONES_LANE:ONES_LANE + 1] for h in range(nh)]
    for jj in range(nh // 2):
        both = jnp.concatenate([outs[2 * jj], outs[2 * jj + 1]], axis=0)
        o_ref[:, jj * LANES:(jj + 1) * LANES] = both.T.astype(o_ref.dtype)


def _attention(qt, k, vt, B, T, name):
    N = k.shape[0]
    nq = T // TQ
    nh = ATTN_HEADS_PER_STEP
    heads = k.shape[1] // SLOT
    return pl.pallas_call(
        _attn_kernel,
        out_shape=jax.ShapeDtypeStruct((N, heads * HEAD_LANES), BF16),
        grid=(B, heads // nh, nq),
        in_specs=[pl.BlockSpec((1, nh * SLOT, TQ), lambda b, g, i: (b * nq + i, g, 0)),
                  pl.BlockSpec((T, nh * SLOT), lambda b, g, i: (b, g)),
                  pl.BlockSpec((nq, nh * SLOT, TQ), lambda b, g, i: (b, g, 0))],
        out_specs=pl.BlockSpec((TQ, nh * HEAD_LANES), lambda b, g, i: (b * nq + i, g)),
        scratch_shapes=[pltpu.VMEM((nh, nq, TQ, TQ), F32), pltpu.VMEM((nh, SLOT, TQ), F32)],
        compiler_params=pltpu.CompilerParams(dimension_semantics=("arbitrary", "arbitrary", "arbitrary"),
                                             vmem_limit_bytes=VMEM_LIMIT_BYTES),
        name=name,
    )(qt, k, vt)


def _oproj_kernel(alpha, a_ref, m_ref, x_ref, wo_ref, g_ref, b_ref, wr_ref, br_ref,
                  x1_ref, x1t_ref, route_ref, gates_ref, cnt_ref, carry_scr):
    i = pl.program_id(0)

    @pl.when(i == 0)
    def _():
        carry_scr[...] = jnp.zeros_like(carry_scr)

    mix = _dot(jnp.concatenate([a_ref[...], m_ref[...]], axis=1), wo_ref[...])
    x1 = _layer_norm(alpha * x_ref[...] + mix, g_ref[...], b_ref[...])
    x1_ref[...] = x1
    _pack_rows_to_tiles(x1t_ref, x1)

    ne = br_ref.shape[0]
    rows = x1.shape[0]
    x_hi = x1.astype(BF16)
    x_lo = (x1 - x_hi.astype(F32)).astype(BF16)
    both = _dot(x_hi, wr_ref[...])
    logits_rm = both[:, :LANES] + both[:, LANES:] + _dot(x_lo, wr_ref[:, :LANES])
    logits = logits_rm.T[0:ne] + br_ref[...]
    expert = lax.broadcasted_iota(jnp.int32, logits.shape, 0)
    expert_f = expert.astype(F32)
    vals, idxs = [], []
    work = logits
    for _ in range(MOE_TOPK):
        mx = jnp.max(work, axis=0, keepdims=True)
        ix = jnp.min(jnp.where(work == mx, expert_f, float(ne)), axis=0, keepdims=True).astype(jnp.int32)
        vals.append(mx)
        idxs.append(ix)
        work = jnp.where(expert == ix, -jnp.inf, work)
    exps = [jnp.exp(v - vals[0]) for v in vals]
    den = exps[0]
    for e in exps[1:]:
        den = den + e

    onehot = jnp.zeros(logits.shape, F32)
    for ix in idxs:
        onehot = onehot + (expert == ix).astype(F32)
    r = lax.broadcasted_iota(jnp.int32, (rows, rows), 0)
    cidx = lax.broadcasted_iota(jnp.int32, (rows, rows), 1)
    earlier = (r < cidx).astype(BF16)
    carry = carry_scr[:, 0:1]
    before = _dot(onehot.astype(BF16), earlier) + carry
    row8 = lax.broadcasted_iota(jnp.int32, (SUBLANES, rows), 0)
    route = jnp.zeros((SUBLANES, rows), jnp.int32)
    gates = jnp.zeros((SUBLANES, rows), F32)
    for k in range(MOE_TOPK):
        rank = jnp.sum(jnp.where(expert == idxs[k], before, 0.0), axis=0, keepdims=True).astype(jnp.int32)
        route = jnp.where(row8 == k, idxs[k], route)
        route = jnp.where(row8 == MOE_TOPK + k, rank, route)
        gates = jnp.where(row8 == k, exps[k] / den, gates)
    route_ref[...] = route
    gates_ref[...] = jnp.concatenate([gates, jnp.zeros((LANES - SUBLANES, rows), F32)], axis=0).T
    new_carry = carry + jnp.sum(onehot, axis=1, keepdims=True)
    carry_scr[...] = jnp.broadcast_to(new_carry, carry_scr.shape)
    cnt_ref[...] = jnp.broadcast_to(new_carry, cnt_ref.shape)


def _oproj(a, m, x2, wo, g1, b1, wr, br, alpha):
    N, D = x2.shape
    nt = N // OPROJ_TILE
    row = lambda i: (i, 0)
    full = lambda i: (0, 0)
    return pl.pallas_call(
        functools.partial(_oproj_kernel, alpha),
        out_shape=(jax.ShapeDtypeStruct((N, D), F32),
                   jax.ShapeDtypeStruct((N * PACKED_SUB, LANES), jnp.uint32),
                   jax.ShapeDtypeStruct((SUBLANES, N), jnp.int32),
                   jax.ShapeDtypeStruct((N, LANES), F32),
                   jax.ShapeDtypeStruct((br.shape[0], LANES), F32)),
        grid=(nt,),
        in_specs=[pl.BlockSpec((OPROJ_TILE, a.shape[1]), row), pl.BlockSpec((OPROJ_TILE, m.shape[1]), row),
                  pl.BlockSpec((OPROJ_TILE, D), row), pl.BlockSpec(wo.shape, full),
                  pl.BlockSpec(g1.shape, full), pl.BlockSpec(b1.shape, full),
                  pl.BlockSpec(wr.shape, full), pl.BlockSpec(br.shape, full)],
        out_specs=(pl.BlockSpec((OPROJ_TILE, D), row), pl.BlockSpec((OPROJ_TILE * PACKED_SUB, LANES), row),
                   pl.BlockSpec((SUBLANES, OPROJ_TILE), lambda i: (0, i)),
                   pl.BlockSpec((OPROJ_TILE, LANES), row), pl.BlockSpec((br.shape[0], LANES), full)),
        scratch_shapes=[pltpu.VMEM((br.shape[0], LANES), F32)],
        compiler_params=pltpu.CompilerParams(dimension_semantics=("arbitrary",),
                                             vmem_limit_bytes=VMEM_LIMIT_BYTES),
        name="oproj_router",
    )(a, m, x2, wo, g1, b1, wr, br)


def _dispatch_kernel(dest_ref, fill_ref, x1_ref, xs_ref, zero_scr, sem, fill_sem):
    i = pl.program_id(0)
    fill_sub = MOE_HALF * PACKED_SUB
    nfill = xs_ref.shape[0] // fill_sub

    def fill_copy(blk, on):
        return pltpu.make_async_copy(zero_scr, xs_ref.at[pl.ds(pl.multiple_of(blk * fill_sub, fill_sub), fill_sub)], on)

    def for_blocks(kind, action):
        def body(blk, _):
            @pl.when(fill_ref[blk] == kind)
            def _():
                action(blk)
            return 0

        lax.fori_loop(0, nfill, body, 0)

    @pl.when(i == 0)
    def _():
        zero_scr[...] = jnp.zeros_like(zero_scr)
        for_blocks(FILL_TAIL, lambda blk: fill_copy(blk, sem).start())
        for_blocks(FILL_FREE, lambda blk: fill_copy(blk, fill_sem).start())
        for_blocks(FILL_TAIL, lambda blk: fill_copy(blk, sem).wait())

    base = i * DISPATCH_TILE
    n_tok = pl.num_programs(0) * DISPATCH_TILE

    def row_copy(t, k):
        d = dest_ref[k * n_tok + base + t]
        return pltpu.make_async_copy(x1_ref.at[pl.ds(pl.multiple_of(t * PACKED_SUB, PACKED_SUB), PACKED_SUB)],
                                     xs_ref.at[pl.ds(pl.multiple_of(d * PACKED_SUB, PACKED_SUB), PACKED_SUB)], sem)

    def start_rows(t, _):
        for k in range(MOE_TOPK):
            row_copy(t, k).start(priority=k % DMA_PRIORITIES)
        return 0

    lax.fori_loop(0, DISPATCH_TILE, start_rows, 0, unroll=ISSUE_UNROLL)
    for k in range(MOE_TOPK):
        pltpu.make_async_copy(x1_ref, xs_ref.at[pl.ds(0, DISPATCH_TILE * PACKED_SUB)], sem).wait()

    @pl.when(i == pl.num_programs(0) - 1)
    def _():
        for_blocks(FILL_FREE, lambda blk: fill_copy(blk, fill_sem).wait())


def _dispatch(dest, fill, x1t, n_rows):
    return pl.pallas_call(
        _dispatch_kernel,
        out_shape=jax.ShapeDtypeStruct((n_rows * PACKED_SUB, LANES), x1t.dtype),
        grid_spec=pltpu.PrefetchScalarGridSpec(
            num_scalar_prefetch=2,
            grid=(x1t.shape[0] // (DISPATCH_TILE * PACKED_SUB),),
            in_specs=[pl.BlockSpec((DISPATCH_TILE * PACKED_SUB, LANES), lambda i, d, t: (i, 0))],
            out_specs=pl.BlockSpec(memory_space=pl.ANY),
            scratch_shapes=[pltpu.VMEM((MOE_HALF * PACKED_SUB, LANES), x1t.dtype), pltpu.SemaphoreType.DMA(()),
                            pltpu.SemaphoreType.DMA(())],
        ),
        compiler_params=pltpu.CompilerParams(dimension_semantics=("arbitrary",),
                                             vmem_limit_bytes=VMEM_LIMIT_BYTES),
        name="dispatch",
    )(dest, fill, x1t)


def _experts_kernel(be_ref, slot_ref, nxt_ref, full_ref, nused_ref, x_ref, wg_hbm, bg_ref, wu_hbm, bu_ref, wd_hbm, bd_ref,
                    y_ref, wg_st, wu_st, wd_st, wg_bf, wu_bf, wd_bf, sems):
    i = pl.program_id(0)
    prev = be_ref[jnp.maximum(i - 1, 0)]
    changed = (i == 0) | (be_ref[i] != prev)
    active = i < nused_ref[0]
    slot = slot_ref[i]

    def weight_copies(expert, s):
        return (pltpu.make_async_copy(wg_hbm.at[expert], wg_st.at[s], sems.at[s, 0]),
                pltpu.make_async_copy(wu_hbm.at[expert], wu_st.at[s], sems.at[s, 1]),
                pltpu.make_async_copy(wd_hbm.at[expert], wd_st.at[s], sems.at[s, 2]))

    @pl.when(i == 0)
    def _():
        for cp in weight_copies(be_ref[0], slot):
            cp.start()

    @pl.when(active & changed)
    def _():
        for cp in weight_copies(be_ref[i], slot):
            cp.wait()

        @pl.when(nxt_ref[i] >= 0)
        def _():
            for cp in weight_copies(nxt_ref[i], 1 - slot):
                cp.start(priority=WEIGHT_DMA_PRIORITY)

        wg_bf[...] = wg_st[slot].astype(BF16)
        wu_bf[...] = wu_st[slot].astype(BF16)
        wd_bf[...] = wd_st[slot].astype(BF16)

    def mlp(nrows):
        rows = pl.ds(0, nrows * SUBLANES)
        xb = _unpack_tiles_to_rows(x_ref.at[pl.ds(0, nrows * PACKED_SUB)], nrows)
        g = jnp.minimum(_dot(xb, wg_bf[...]) + bg_ref[0], SWIGLU_LIMIT)
        u = jnp.clip(_dot(xb, wu_bf[...]) + bu_ref[0], -SWIGLU_LIMIT, SWIGLU_LIMIT)
        h = g * (1.0 / (1.0 + jnp.exp(-SWIGLU_ALPHA * g))) * (u + 1.0)
        _rows_to_tiles(y_ref.at[rows], _dot(h.astype(BF16), wd_bf[...]) + bd_ref[0])

    whole = active & (full_ref[i] > 0)

    @pl.when(whole)
    def _():
        mlp(MOE_GROUP)

    @pl.when(active & jnp.logical_not(whole))
    def _():
        mlp(MOE_HALF)

    @pl.when(jnp.logical_not(active))
    def _():
        y_ref[pl.ds(0, HALF_SUB), :] = jnp.zeros((HALF_SUB, LANES), F32)

    @pl.when(jnp.logical_not(whole))
    def _():
        y_ref[pl.ds(HALF_SUB, HALF_SUB), :] = jnp.zeros((HALF_SUB, LANES), F32)


def _experts(blk_expert, blk_slot, blk_next, blk_full, n_used, xs, wg, bg, wu, bu, wd, bd):
    E, D, F = wg.shape
    nblk = xs.shape[0] // (MOE_GROUP * PACKED_SUB)

    def rowmap(i, be, sl, nx, fl, nu):
        return (jnp.minimum(i, nu[0] - 1), 0)

    def bmap(i, be, sl, nx, fl, nu):
        return (be[i], 0, 0)

    hbm = pl.BlockSpec(memory_space=pl.ANY)
    return pl.pallas_call(
        _experts_kernel,
        out_shape=jax.ShapeDtypeStruct((nblk * GROUP_SUB, LANES), F32),
        grid_spec=pltpu.PrefetchScalarGridSpec(
            num_scalar_prefetch=5,
            grid=(nblk,),
            in_specs=[pl.BlockSpec((MOE_GROUP * PACKED_SUB, LANES), rowmap),
                      hbm, pl.BlockSpec((1, 1, F), bmap),
                      hbm, pl.BlockSpec((1, 1, F), bmap),
                      hbm, pl.BlockSpec((1, 1, D), bmap)],
            out_specs=pl.BlockSpec((GROUP_SUB, LANES), lambda i, be, sl, nx, fl, nu: (i, 0)),
            scratch_shapes=[pltpu.VMEM((2, D, F), F32), pltpu.VMEM((2, D, F), F32), pltpu.VMEM((2, F, D), F32),
                            pltpu.VMEM((D, F), BF16), pltpu.VMEM((D, F), BF16), pltpu.VMEM((F, D), BF16),
                            pltpu.SemaphoreType.DMA((2, 3))],
        ),
        compiler_params=pltpu.CompilerParams(dimension_semantics=("arbitrary",),
                                             vmem_limit_bytes=VMEM_LIMIT_BYTES),
        name="experts",
    )(blk_expert, blk_slot, blk_next, blk_full, n_used, xs, wg, bg, wu, bu, wd, bd)


def _combine_kernel(alpha, dest_ref, gates_ref, x1_ref, g_ref, b_ref, y_ref, o_ref, ybuf, sems):
    i = pl.program_id(0)
    buf = i % 2

    n_tok = pl.num_programs(0) * COMBINE_TILE

    def gather_tile(tile, into):
        base = tile * COMBINE_TILE

        def start_rows(t, _):
            for k in range(MOE_TOPK):
                d = dest_ref[k * n_tok + base + t]
                pltpu.make_async_copy(y_ref.at[pl.ds(pl.multiple_of(d * SUBLANES, SUBLANES), SUBLANES)],
                                      ybuf.at[into, k, pl.ds(pl.multiple_of(t * SUBLANES, SUBLANES), SUBLANES)],
                                      sems.at[into]).start(priority=k % DMA_PRIORITIES)
            return 0

        lax.fori_loop(0, COMBINE_TILE, start_rows, 0, unroll=ISSUE_UNROLL)

    @pl.when(i == 0)
    def _():
        gather_tile(0, 0)

    @pl.when(i + 1 < pl.num_programs(0))
    def _():
        gather_tile(i + 1, 1 - buf)

    for k in range(MOE_TOPK):
        pltpu.make_async_copy(y_ref.at[pl.ds(0, COMBINE_TILE * SUBLANES)], ybuf.at[buf, k], sems.at[buf]).wait()

    gates = gates_ref[...]
    ffn = gates[:, 0:1] * _tiles_to_rows(ybuf.at[buf, 0], COMBINE_TILE)
    for k in range(1, MOE_TOPK):
        ffn = ffn + gates[:, k:k + 1] * _tiles_to_rows(ybuf.at[buf, k], COMBINE_TILE)
    o_ref[...] = _layer_norm(alpha * x1_ref[...] + ffn, g_ref[...], b_ref[...])


def _combine(dest, gates, x1, g2, b2, y_rows, alpha):
    N, D = x1.shape
    row = lambda i, d: (i, 0)
    full = lambda i, d: (0, 0)
    return pl.pallas_call(
        functools.partial(_combine_kernel, alpha),
        out_shape=jax.ShapeDtypeStruct((N, D), F32),
        grid_spec=pltpu.PrefetchScalarGridSpec(
            num_scalar_prefetch=1,
            grid=(N // COMBINE_TILE,),
            in_specs=[pl.BlockSpec((COMBINE_TILE, LANES), row), pl.BlockSpec((COMBINE_TILE, D), row),
                      pl.BlockSpec(g2.shape, full), pl.BlockSpec(b2.shape, full),
                      pl.BlockSpec(memory_space=pl.ANY)],
            out_specs=pl.BlockSpec((COMBINE_TILE, D), row),
            scratch_shapes=[pltpu.VMEM((2, MOE_TOPK, COMBINE_TILE * SUBLANES, LANES), F32),
                            pltpu.SemaphoreType.DMA((2,))],
        ),
        compiler_params=pltpu.CompilerParams(dimension_semantics=("arbitrary",),
                                             vmem_limit_bytes=VMEM_LIMIT_BYTES),
        name="combine",
    )(dest, gates, x1, g2, b2, y_rows)


def _rot_partner(w, half):
    return jnp.concatenate([-w[..., half:2 * half], w[..., :half]], axis=-1)


def _layer_weights(w_in, w_q_b, w_kv_b):
    D = w_in.shape[0]
    o1 = Q_LORA
    o2 = o1 + KV_LORA
    o3 = o2 + MLA_ROPE
    w_ql, w_kvl, w_kr = w_in[:, :o1], w_in[:, o1:o2], w_in[:, o2:o3]
    w_mq, w_mk, w_mv = w_in[:, o3:o3 + _MH], w_in[:, o3 + _MH:o3 + 2 * _MH], w_in[:, o3 + 2 * _MH:]
    zpad = lambda n: jnp.zeros((D, n), w_in.dtype)
    tail = SLOT - MLA_NOPE - MLA_ROPE
    assert tail == MLA_ROPE
    kr_slot = jnp.concatenate([zpad(MLA_NOPE), w_kr, _rot_partner(w_kr, MLA_ROPE // 2)], axis=1)

    def moba_partner(w):
        w3 = w.reshape(D, MOBA_HEADS, MOBA_HD)
        part = jnp.concatenate([_rot_partner(w3[..., :MOBA_ROT], MOBA_ROT // 2),
                                jnp.zeros((D, MOBA_HEADS, MOBA_HD - MOBA_ROT), w.dtype)], axis=-1)
        return part.reshape(D, _MH)

    win = jnp.concatenate([w_ql, w_kvl, kr_slot, w_mk, moba_partner(w_mk)], axis=1).astype(BF16)
    mq_partner = _rot_partner(w_mq.reshape(D, MOBA_HEADS, MOBA_HD)[..., :MOBA_ROT], MOBA_ROT // 2)
    wmt = jnp.concatenate([w_mq, mq_partner.reshape(D, MOBA_HEADS * MOBA_ROT), w_mv], axis=1).T.astype(BF16)

    wq3 = w_q_b.reshape(Q_LORA, MLA_HEADS, MLA_NOPE + MLA_ROPE)
    wqs = _rot_partner(wq3[..., MLA_NOPE:], MLA_ROPE // 2).reshape(Q_LORA, MLA_HEADS * MLA_ROPE)
    wkv3 = w_kv_b.reshape(KV_LORA, MLA_HEADS, MLA_NOPE + MLA_V)
    wk = jnp.concatenate([wkv3[..., :MLA_NOPE], jnp.zeros((KV_LORA, MLA_HEADS, SLOT - MLA_NOPE), w_kv_b.dtype)],
                         axis=-1).reshape(KV_LORA, MLA_HEADS * SLOT).astype(BF16)
    wv = wkv3[..., MLA_NOPE:].reshape(KV_LORA, MLA_HEADS * MLA_V)
    return win, wmt, w_q_b.T.astype(BF16), wqs.T.astype(BF16), wk, wv.T.astype(BF16)


def _rope_tables(positions):
    pos = positions.astype(F32).reshape(1, -1)

    def cs(d_rot):
        inv_freq = ROPE_THETA ** (-jnp.arange(0, d_rot, 2, dtype=F32) / d_rot)
        ang = inv_freq.reshape(d_rot // 2, 1) * pos
        return jnp.cos(ang), jnp.sin(ang)

    return cs(MLA_ROPE) + cs(MOBA_ROT)


def kernel(x, positions, w_in, q_a_norm, w_q_b, kv_a_norm, w_kv_b, w_o, ln1_g, ln1_b, w_router, b_router,
           w_gate, b_gate, w_up, b_up, w_down, b_down, ln2_g, ln2_b):
    B, T, D = x.shape
    depth = w_in.shape[0]
    alpha = (2.0 * depth) ** 0.25
    N = B * T
    assert T % MOBA_BLOCK == 0 and T // MOBA_BLOCK <= SUBLANES and N % OPROJ_TILE == 0
    assert D == SUBLANES * LANES
    n_asg = N * MOE_TOPK
    n_rows = n_asg + N_EXPERTS * MOE_GROUP
    tables = _rope_tables(positions)
    h = x.reshape(N, D)
    for l in range(depth):
        win, wmt, wqt, wqst, wk, wvt = _layer_weights(w_in[l], w_q_b[l], w_kv_b[l])
        qa, ka, va, mq, mk, mv = _prep(h, win, wmt, wqt, wqst, wk, wvt, q_a_norm[l].reshape(1, -1),
                                       kv_a_norm[l].reshape(1, -1), tables, B, T)
        a = _attention(qa, ka, va, B, T, "mla_attention")
        m = _attention(mq, mk, mv, B, T, "moba_attention")
        wr_pad = jnp.concatenate([w_router[l], jnp.zeros((D, LANES - N_EXPERTS), F32)], axis=1)
        wr_hi = wr_pad.astype(BF16)
        wr = jnp.concatenate([wr_hi, (wr_pad - wr_hi.astype(F32)).astype(BF16)], axis=1)
        br = b_router[l].reshape(N_EXPERTS, 1)
        x1, x1t, route, gates, cnt = _oproj(a, m, h, w_o[l].astype(BF16), ln1_g[l].reshape(1, D), ln1_b[l].reshape(1, D),
                                       wr, br, alpha)
        er = jnp.arange(N_EXPERTS, dtype=jnp.int32)
        counts = cnt[:, 0].astype(jnp.int32)
        padded = (counts + MOE_GROUP - 1) // MOE_GROUP * MOE_GROUP
        upto = er[None, :] <= er[:, None]
        pad_end = jnp.sum(jnp.where(upto, padded[None, :], 0), axis=1)
        pad_start = pad_end - padded
        e_idx = route[:MOE_TOPK]
        group_start = jnp.sum(jnp.where(e_idx[..., None] == er, pad_start, 0), axis=-1)
        dest = (group_start + route[MOE_TOPK:2 * MOE_TOPK]).reshape(n_asg)
        n_used = (pad_end[-1:] // MOE_GROUP).astype(jnp.int32)
        nblk = n_rows // MOE_GROUP
        blk_start = jnp.arange(nblk, dtype=jnp.int32) * MOE_GROUP
        blk_expert = jnp.minimum(jnp.sum((blk_start[:, None] >= pad_end[None, :]).astype(jnp.int32), axis=1),
                                 N_EXPERTS - 1)
        of_expert = blk_expert[:, None] == er[None, :]
        pick = lambda per_expert: jnp.sum(jnp.where(of_expert, per_expert[None, :], 0), axis=1)
        nonempty = padded > 0
        group_index = jnp.sum(jnp.where(upto & nonempty[None, :], 1, 0), axis=1) - 1
        later = (er[None, :] > er[:, None]) & nonempty[None, :]
        next_expert = jnp.min(jnp.where(later, er[None, :], N_EXPERTS), axis=1)
        next_expert = jnp.where(next_expert == N_EXPERTS, -1, next_expert)
        blk_slot = pick(group_index) % 2
        blk_next = pick(next_expert)
        blk_full = (pick(pad_start + counts) > blk_start + MOE_HALF).astype(jnp.int32)
        half_start = jnp.arange(n_rows // MOE_HALF, dtype=jnp.int32) * MOE_HALF
        half_expert = jnp.minimum(jnp.sum((half_start[:, None] >= pad_end[None, :]).astype(jnp.int32), axis=1),
                                  N_EXPERTS - 1)
        half_real_end = jnp.sum(jnp.where(half_expert[:, None] == er[None, :], (pad_start + counts)[None, :], 0), axis=1)
        no_real_rows = (half_start >= pad_end[-1]) | (half_start >= half_real_end)
        fill = jnp.where(no_real_rows, FILL_FREE,
                         jnp.where(half_start + MOE_HALF > half_real_end, FILL_TAIL, 0)).astype(jnp.int32)
        xs = _dispatch(dest, fill, x1t, n_rows)
        y_rows = _experts(blk_expert, blk_slot.astype(jnp.int32), blk_next, blk_full, n_used, xs, w_gate[l], b_gate[l].reshape(N_EXPERTS, 1, -1),
                          w_up[l], b_up[l].reshape(N_EXPERTS, 1, -1), w_down[l], b_down[l].reshape(N_EXPERTS, 1, -1))
        h = _combine(dest, gates, x1, ln2_g[l].reshape(1, D), ln2_b[l].reshape(1, D), y_rows, alpha)
    return h.reshape(B, T, D)
```

```python
import functools
import math

import jax
import jax.numpy as jnp
from jax import lax
from jax.experimental import pallas as pl
from jax.experimental.pallas import tpu as pltpu

ROPE_THETA = 500000.0
MLA_HEADS = 8
MLA_NOPE = 64
MLA_ROPE = 32
MLA_V = 64
Q_LORA = 256
KV_LORA = 128
MOBA_HEADS = 8
MOBA_HD = 64
MOBA_ROT = MOBA_HD // 4
MOBA_BLOCK = 256
MOBA_TOPK = 3
N_EXPERTS = 32
MOE_TOPK = 4
SWIGLU_LIMIT = 7.0
SWIGLU_ALPHA = 1.702
RMS_EPS = 1e-6
LN_EPS = 1e-5

LANES = 128
SUBLANES = 8
VMEM_LIMIT_BYTES = 56 * 1024 * 1024

SLOT = LANES
TQ = MOBA_BLOCK
ROW_TILE = 256
OPROJ_TILE = 512
MOE_GROUP = 512
MOE_HALF = MOE_GROUP // 2
PACKED_SUB = SUBLANES // 2
FILL_TAIL, FILL_FREE = 1, 2
DISPATCH_TILE = 512
COMBINE_TILE = 512
ISSUE_UNROLL = 4
DMA_PRIORITIES = 2
WEIGHT_DMA_PRIORITY = 1
HEAD_LANES = 64
ONES_LANE = 64
BIAS_LANE = 64
ATTN_HEADS_PER_STEP = 8
LOG2E = math.log2(math.e)
NEG_BIG = -(2.0 ** 100)

F32 = jnp.float32
BF16 = jnp.bfloat16
NT_DIMS = (((1,), (1,)), ((), ()))


def _dot(a, b, precision=None):
    return jnp.dot(a, b, preferred_element_type=F32, precision=precision)


def _dot_nt(a, b, precision=None):
    return lax.dot_general(a, b, NT_DIMS, preferred_element_type=F32, precision=precision)


def _pack_rows_to_tiles(ref, x):
    rows, width = x.shape
    bits = lambda v: lax.bitcast_convert_type(v.astype(BF16).astype(F32), jnp.uint32)
    words = (bits(x[:, :width // 2]) >> 16) | (bits(x[:, width // 2:]) & jnp.uint32(0xFFFF0000))
    for c in range(PACKED_SUB):
        ref[pl.ds(c, rows, stride=PACKED_SUB), :] = words[:, c * LANES:(c + 1) * LANES]


def _unpack_tiles_to_rows(ref, rows):
    words = jnp.concatenate([ref[pl.ds(c, rows, stride=PACKED_SUB), :] for c in range(PACKED_SUB)], axis=1)
    low = lax.bitcast_convert_type(words << 16, F32).astype(BF16)
    high = lax.bitcast_convert_type(words & jnp.uint32(0xFFFF0000), F32).astype(BF16)
    return jnp.concatenate([low, high], axis=1)


def _rms(x, g):
    return x * lax.rsqrt(jnp.mean(x * x, axis=-1, keepdims=True) + RMS_EPS) * g


def _layer_norm(x, g, b):
    mu = jnp.mean(x, axis=-1, keepdims=True)
    xc = x - mu
    var = jnp.mean(xc * xc, axis=-1, keepdims=True)
    return xc * lax.rsqrt(var + LN_EPS) * g + b


_C_QL = 0
_C_KVL = _C_QL + Q_LORA
_C_KR = _C_KVL + KV_LORA
_C_MK = _C_KR + SLOT
_MH = MOBA_HEADS * MOBA_HD
_C_MKS = _C_MK + _MH
_C_END = _C_MKS + _MH


def _prep_kernel(x_ref, win_ref, wmt_ref, wqt_ref, wqst_ref, wk_ref, wvt_ref, qg_ref, kvg_ref,
                 cat_ref, sat_ref, cbt_ref, sbt_ref,
                 qat_ref, ka_ref, vat_ref, mqt_ref, mk_ref, mvt_ref, kmean_scr):
    c = pl.program_id(1)

    @pl.when(c == 0)
    def _():
        kmean_scr[...] = jnp.zeros_like(kmean_scr)

    xb = x_ref[...].astype(BF16)
    both_halves = lambda t: jnp.concatenate([t, t], axis=0)
    cat = both_halves(cat_ref[...])
    sat = both_halves(sat_ref[...])
    cbt = both_halves(cbt_ref[...])
    sbt = both_halves(sbt_ref[...])
    const = lambda v, n: jnp.full((n, ROW_TILE), v, F32)
    tail = SLOT - MLA_NOPE - MLA_ROPE
    rest = MOBA_HD - MOBA_ROT
    ca = jnp.concatenate([const(1.0, MLA_NOPE), cat, const(0.0, tail)], axis=0).T
    sa = jnp.concatenate([const(0.0, MLA_NOPE), sat, const(0.0, tail)], axis=0).T
    cb = jnp.concatenate([cbt, const(1.0, rest), cbt, const(1.0, rest)], axis=0).T
    sb = jnp.concatenate([sbt, const(0.0, rest), sbt, const(0.0, rest)], axis=0).T
    lane = lax.broadcasted_iota(jnp.int32, (ROW_TILE, SLOT), 1)
    head_lanes = lane < HEAD_LANES
    ones_rows = (lax.broadcasted_iota(jnp.int32, (SLOT - HEAD_LANES, ROW_TILE), 0) == ONES_LANE - HEAD_LANES).astype(F32)

    ql = _dot(xb, win_ref[:, _C_QL:_C_KVL])
    kvl = _dot(xb, win_ref[:, _C_KVL:_C_KR])
    kr = _dot(xb, win_ref[:, _C_KR:_C_MK])
    krs = pltpu.roll(kr, SLOT - MLA_ROPE, axis=1)
    qn = _rms(ql, qg_ref[...]).astype(BF16)
    kvn = _rms(kvl, kvg_ref[...]).astype(BF16)
    q_t = _dot_nt(wqt_ref[...], qn)
    qs_t = _dot_nt(wqst_ref[...], qn)
    kn = _dot(kvn, wk_ref[...])
    v_t = _dot_nt(wvt_ref[...], kvn)
    scale_a = LOG2E / math.sqrt(MLA_NOPE + MLA_ROPE)
    kro = kr * ca + krs * sa
    qd = MLA_NOPE + MLA_ROPE
    q_pad = jnp.zeros((SLOT - qd, ROW_TILE), F32)
    for h in range(MLA_HEADS):
        sl = slice(h * SLOT, (h + 1) * SLOT)
        q_rope = q_t[h * qd + MLA_NOPE:(h + 1) * qd] * cat + qs_t[h * MLA_ROPE:(h + 1) * MLA_ROPE] * sat
        q_slot = jnp.concatenate([q_t[h * qd:h * qd + MLA_NOPE], q_rope, q_pad], axis=0)
        qat_ref[0, sl, :] = (q_slot * scale_a).astype(BF16)
        ka_ref[:, sl] = (kn[:, sl] + kro).astype(BF16)
        vat_ref[0, sl, :] = jnp.concatenate([v_t[h * MLA_V:(h + 1) * MLA_V], ones_rows], axis=0).astype(BF16)

    mk = _dot(xb, win_ref[:, _C_MK:_C_MKS])
    mks = _dot(xb, win_ref[:, _C_MKS:_C_END])
    npair = _MH // LANES
    pair = lambda a, j: a[:, j * LANES:(j + 1) * LANES]
    mk_rot = [pair(mk, j) * cb + pair(mks, j) * sb for j in range(npair)]
    mk_all = jnp.concatenate(mk_rot, axis=1)

    nrot = MOBA_HEADS * MOBA_ROT
    mq_t = _dot_nt(wmt_ref[0:_MH, :], xb)
    mqs_t = _dot_nt(wmt_ref[_MH:_MH + nrot, :], xb)
    mv_t = _dot_nt(wmt_ref[_MH + nrot:2 * _MH + nrot, :], xb)
    head = lambda a, h: a[h * MOBA_HD:(h + 1) * MOBA_HD]
    mq_rot_t = [jnp.concatenate([head(mq_t, h)[0:MOBA_ROT] * cbt + mqs_t[h * MOBA_ROT:(h + 1) * MOBA_ROT] * sbt,
                                 head(mq_t, h)[MOBA_ROT:]], axis=0) for h in range(MOBA_HEADS)]

    nrow = MOBA_HEADS * SUBLANES
    row_i = lax.broadcasted_iota(jnp.int32, (nrow, _MH), 0)
    lane_i = lax.broadcasted_iota(jnp.int32, (nrow, _MH), 1)
    kmean_c = jnp.mean(mk_all, axis=0, keepdims=True)
    put = ((row_i % SUBLANES) == c) & ((lane_i // MOBA_HD) == (row_i // SUBLANES))
    table = kmean_scr[...]
    gate_t = _dot(table, jnp.concatenate(mq_rot_t, axis=0), precision=lax.Precision.HIGHEST)
    kmean_scr[...] = jnp.where(put, jnp.broadcast_to(kmean_c, (nrow, _MH)), table)

    n_idx = lax.broadcasted_iota(jnp.int32, (SUBLANES, ROW_TILE), 0)
    valid = n_idx < c
    scale_b = LOG2E / math.sqrt(MOBA_HD)
    zero_rows = jnp.zeros((SLOT - HEAD_LANES - SUBLANES, ROW_TILE), F32)
    block_onehot = (lane == BIAS_LANE + c).astype(F32)
    for h in range(MOBA_HEADS):
        g = jnp.where(valid, gate_t[h * SUBLANES:(h + 1) * SUBLANES, :], -jnp.inf)
        rank = jnp.zeros((SUBLANES, ROW_TILE), jnp.int32)
        for k in range(1, SUBLANES):
            other = pltpu.roll(g, k, axis=0)
            other_n = pltpu.roll(n_idx, k, axis=0)
            beats = (other > g) | ((other == g) & (other_n < n_idx))
            rank = rank + beats.astype(jnp.int32)
        keep = (valid & (rank < MOBA_TOPK)) | (n_idx == c)
        bias = jnp.where(keep, 0.0, NEG_BIG)
        sl = slice(h * SLOT, (h + 1) * SLOT)
        mqt_ref[0, sl, :] = jnp.concatenate([mq_rot_t[h] * scale_b, bias, zero_rows], axis=0).astype(BF16)
        mvt_ref[0, sl, :] = jnp.concatenate([head(mv_t, h), ones_rows], axis=0).astype(BF16)
        j, hh = divmod(h, 2)
        k_h = mk_rot[j] if hh == 0 else pltpu.roll(mk_rot[j], HEAD_LANES, axis=1)
        mk_ref[:, sl] = jnp.where(head_lanes, k_h, block_onehot).astype(BF16)


def _prep(x2, win, wmt, wqt, wqst, wk, wvt, qg, kvg, tables, B, T):
    N, D = x2.shape
    nt = T // ROW_TILE
    row = lambda b, c: (b * nt + c, 0)
    col = lambda b, c: (0, b * nt + c)
    full = lambda b, c: (0, 0)
    cat, sat, cbt, sbt = tables
    width = MLA_HEADS * SLOT

    def rows(w):
        return pl.BlockSpec((ROW_TILE, w), row)

    def cols(a):
        return pl.BlockSpec((a.shape[0], ROW_TILE), col)

    def whole(a):
        return pl.BlockSpec(a.shape, full)

    rowmajor = jax.ShapeDtypeStruct((N, width), BF16)
    transposed = jax.ShapeDtypeStruct((B * nt, width, ROW_TILE), BF16)
    t_spec = pl.BlockSpec((1, width, ROW_TILE), lambda b, c: (b * nt + c, 0, 0))
    return pl.pallas_call(
        _prep_kernel,
        out_shape=(transposed, rowmajor, transposed, transposed, rowmajor, transposed),
        grid=(B, nt),
        in_specs=[rows(D), whole(win), whole(wmt), whole(wqt), whole(wqst), whole(wk), whole(wvt), whole(qg), whole(kvg),
                  cols(cat), cols(sat), cols(cbt), cols(sbt)],
        out_specs=(t_spec, rows(width), t_spec, t_spec, rows(width), t_spec),
        scratch_shapes=[pltpu.VMEM((MOBA_HEADS * SUBLANES, _MH), F32)],
        compiler_params=pltpu.CompilerParams(dimension_semantics=("arbitrary", "arbitrary"),
                                             vmem_limit_bytes=VMEM_LIMIT_BYTES),
        name="prep",
    )(x2, win, wmt, wqt, wqst, wk, wvt, qg, kvg, cat, sat, cbt, sbt)


def _attn_kernel(qt_ref, k_ref, vt_ref, o_ref, s_scr, acc_scr):
    i = pl.program_id(2)
    nh = ATTN_HEADS_PER_STEP
    slot = lambda h: slice(h * SLOT, (h + 1) * SLOT)
    nblk = i + 1
    nquad = nblk // 4
    pair_end = 4 * nquad + 2 * ((nblk - 4 * nquad) // 2)
    q_pos = i * TQ + lax.broadcasted_iota(jnp.int32, (1, TQ), 1)

    def fold(s):
        out = s[0:SUBLANES]
        for t in range(1, s.shape[0] // SUBLANES):
            out = jnp.maximum(out, s[t * SUBLANES:(t + 1) * SUBLANES])
        return out

    def score_blocks(j, n, mrun):
        j0 = pl.multiple_of(j * TQ, TQ)
        visible = (j0 + lax.broadcasted_iota(jnp.int32, (n * TQ, TQ), 0)) <= q_pos
        out = []
        for h in range(nh):
            s = _dot(k_ref[pl.ds(j0, n * TQ), slot(h)], qt_ref[0, slot(h), :])
            s = jnp.where(visible, s, -jnp.inf)
            s_scr[h, pl.ds(j, n)] = s.reshape(n, TQ, TQ)
            out.append(jnp.maximum(mrun[h], fold(s)))
        return tuple(out)

    mrun = tuple(jnp.full((SUBLANES, TQ), -jnp.inf, F32) for _ in range(nh))
    mrun = lax.fori_loop(0, nquad, lambda jq, m: score_blocks(4 * jq, 4, m), mrun)
    mrun = lax.fori_loop(2 * nquad, pair_end // 2, lambda jj, m: score_blocks(2 * jj, 2, m), mrun)
    mrun = lax.fori_loop(pair_end, nblk, lambda j, m: score_blocks(j, 1, m), mrun)
    ms = [jnp.max(m, axis=0, keepdims=True) for m in mrun]

    def accumulate(j, n, carry):
        for h in range(nh):
            p = jnp.exp2(s_scr[h, pl.ds(j, n)].reshape(n * TQ, TQ) - ms[h]).astype(BF16)
            vt = jnp.concatenate([vt_ref[j + t, slot(h), :] for t in range(n)], axis=1)
            acc_scr[h] += _dot(vt, p)
        return carry

    acc_scr[...] = jnp.zeros_like(acc_scr)
    lax.fori_loop(0, nquad, lambda jq, c: accumulate(4 * jq, 4, c), 0)
    lax.fori_loop(2 * nquad, pair_end // 2, lambda jj, c: accumulate(2 * jj, 2, c), 0)
    lax.fori_loop(pair_end, nblk, lambda j, c: accumulate(j, 1, c), 0)
    outs = [acc_scr[h, 0:HEAD_LANES] / acc_scr[h, ONES_LANE:ONES_LANE + 1] for h in range(nh)]
    for jj in range(nh // 2):
        both = jnp.concatenate([outs[2 * jj], outs[2 * jj + 1]], axis=0)
        o_ref[:, jj * LANES:(jj + 1) * LANES] = both.T.astype(o_ref.dtype)


def _attention(qt, k, vt, B, T, name):
    N = k.shape[0]
    nq = T // TQ
    nh = ATTN_HEADS_PER_STEP
    heads = k.shape[1] // SLOT
    return pl.pallas_call(
        _attn_kernel,
        out_shape=jax.ShapeDtypeStruct((N, heads * HEAD_LANES), BF16),
        grid=(B, heads // nh, nq),
        in_specs=[pl.BlockSpec((1, nh * SLOT, TQ), lambda b, g, i: (b * nq + i, g, 0)),
                  pl.BlockSpec((T, nh * SLOT), lambda b, g, i: (b, g)),
                  pl.BlockSpec((nq, nh * SLOT, TQ), lambda b, g, i: (b, g, 0))],
        out_specs=pl.BlockSpec((TQ, nh * HEAD_LANES), lambda b, g, i: (b * nq + i, g)),
        scratch_shapes=[pltpu.VMEM((nh, nq, TQ, TQ), F32), pltpu.VMEM((nh, SLOT, TQ), F32)],
        compiler_params=pltpu.CompilerParams(dimension_semantics=("arbitrary", "arbitrary", "arbitrary"),
                                             vmem_limit_bytes=VMEM_LIMIT_BYTES),
        name=name,
    )(qt, k, vt)


def _oproj_kernel(alpha, a_ref, m_ref, x_ref, wo_ref, g_ref, b_ref, wr_ref, br_ref,
                  x1_ref, x1t_ref, route_ref, gates_ref, cnt_ref, carry_scr):
    i = pl.program_id(0)

    @pl.when(i == 0)
    def _():
        carry_scr[...] = jnp.zeros_like(carry_scr)

    mix = _dot(jnp.concatenate([a_ref[...], m_ref[...]], axis=1), wo_ref[...])
    x1 = _layer_norm(alpha * x_ref[...] + mix, g_ref[...], b_ref[...])
    x1_ref[...] = x1
    _pack_rows_to_tiles(x1t_ref, x1)

    ne = br_ref.shape[0]
    rows = x1.shape[0]
    x_hi = x1.astype(BF16)
    x_lo = (x1 - x_hi.astype(F32)).astype(BF16)
    both = _dot(x_hi, wr_ref[...])
    logits_rm = both[:, :LANES] + both[:, LANES:] + _dot(x_lo, wr_ref[:, :LANES])
    logits = logits_rm.T[0:ne] + br_ref[...]
    expert = lax.broadcasted_iota(jnp.int32, logits.shape, 0)
    expert_f = expert.astype(F32)
    vals, idxs = [], []
    work = logits
    for _ in range(MOE_TOPK):
        mx = jnp.max(work, axis=0, keepdims=True)
        ix = jnp.min(jnp.where(work == mx, expert_f, float(ne)), axis=0, keepdims=True).astype(jnp.int32)
        vals.append(mx)
        idxs.append(ix)
        work = jnp.where(expert == ix, -jnp.inf, work)
    exps = [jnp.exp(v - vals[0]) for v in vals]
    den = exps[0]
    for e in exps[1:]:
        den = den + e

    onehot = jnp.zeros(logits.shape, F32)
    for ix in idxs:
        onehot = onehot + (expert == ix).astype(F32)
    r = lax.broadcasted_iota(jnp.int32, (rows, rows), 0)
    cidx = lax.broadcasted_iota(jnp.int32, (rows, rows), 1)
    earlier = (r < cidx).astype(BF16)
    carry = carry_scr[:, 0:1]
    before = _dot(onehot.astype(BF16), earlier) + carry
    row8 = lax.broadcasted_iota(jnp.int32, (SUBLANES, rows), 0)
    route = jnp.zeros((SUBLANES, rows), jnp.int32)
    gates = jnp.zeros((SUBLANES, rows), F32)
    for k in range(MOE_TOPK):
        rank = jnp.sum(jnp.where(expert == idxs[k], before, 0.0), axis=0, keepdims=True).astype(jnp.int32)
        route = jnp.where(row8 == k, idxs[k], route)
        route = jnp.where(row8 == MOE_TOPK + k, rank, route)
        gates = jnp.where(row8 == k, exps[k] / den, gates)
    route_ref[...] = route
    gates_ref[...] = jnp.concatenate([gates, jnp.zeros((LANES - SUBLANES, rows), F32)], axis=0).T
    new_carry = carry + jnp.sum(onehot, axis=1, keepdims=True)
    carry_scr[...] = jnp.broadcast_to(new_carry, carry_scr.shape)
    cnt_ref[...] = jnp.broadcast_to(new_carry, cnt_ref.shape)


def _oproj(a, m, x2, wo, g1, b1, wr, br, alpha):
    N, D = x2.shape
    nt = N // OPROJ_TILE
    row = lambda i: (i, 0)
    full = lambda i: (0, 0)
    return pl.pallas_call(
        functools.partial(_oproj_kernel, alpha),
        out_shape=(jax.ShapeDtypeStruct((N, D), F32),
                   jax.ShapeDtypeStruct((N * PACKED_SUB, LANES), jnp.uint32),
                   jax.ShapeDtypeStruct((SUBLANES, N), jnp.int32),
                   jax.ShapeDtypeStruct((N, LANES), F32),
                   jax.ShapeDtypeStruct((br.shape[0], LANES), F32)),
        grid=(nt,),
        in_specs=[pl.BlockSpec((OPROJ_TILE, a.shape[1]), row), pl.BlockSpec((OPROJ_TILE, m.shape[1]), row),
                  pl.BlockSpec((OPROJ_TILE, D), row), pl.BlockSpec(wo.shape, full),
                  pl.BlockSpec(g1.shape, full), pl.BlockSpec(b1.shape, full),
                  pl.BlockSpec(wr.shape, full), pl.BlockSpec(br.shape, full)],
        out_specs=(pl.BlockSpec((OPROJ_TILE, D), row), pl.BlockSpec((OPROJ_TILE * PACKED_SUB, LANES), row),
                   pl.BlockSpec((SUBLANES, OPROJ_TILE), lambda i: (0, i)),
                   pl.BlockSpec((OPROJ_TILE, LANES), row), pl.BlockSpec((br.shape[0], LANES), full)),
        scratch_shapes=[pltpu.VMEM((br.shape[0], LANES), F32)],
        compiler_params=pltpu.CompilerParams(dimension_semantics=("arbitrary",),
                                             vmem_limit_bytes=VMEM_LIMIT_BYTES),
        name="oproj_router",
    )(a, m, x2, wo, g1, b1, wr, br)


def _dispatch_kernel(dest_ref, fill_ref, x1_ref, xs_ref, zero_scr, sem, fill_sem):
    i = pl.program_id(0)
    fill_sub = MOE_HALF * PACKED_SUB
    nfill = xs_ref.shape[0] // fill_sub

    def fill_copy(blk, on):
        return pltpu.make_async_copy(zero_scr, xs_ref.at[pl.ds(pl.multiple_of(blk * fill_sub, fill_sub), fill_sub)], on)

    def for_blocks(kind, action):
        def body(blk, _):
            @pl.when(fill_ref[blk] == kind)
            def _():
                action(blk)
            return 0

        lax.fori_loop(0, nfill, body, 0)

    @pl.when(i == 0)
    def _():
        zero_scr[...] = jnp.zeros_like(zero_scr)
        for_blocks(FILL_TAIL, lambda blk: fill_copy(blk, sem).start())
        for_blocks(FILL_FREE, lambda blk: fill_copy(blk, fill_sem).start())
        for_blocks(FILL_TAIL, lambda blk: fill_copy(blk, sem).wait())

    base = i * DISPATCH_TILE
    n_tok = pl.num_programs(0) * DISPATCH_TILE

    def row_copy(t, k):
        d = dest_ref[k * n_tok + base + t]
        return pltpu.make_async_copy(x1_ref.at[pl.ds(pl.multiple_of(t * PACKED_SUB, PACKED_SUB), PACKED_SUB)],
                                     xs_ref.at[pl.ds(pl.multiple_of(d * PACKED_SUB, PACKED_SUB), PACKED_SUB)], sem)

    def start_rows(t, _):
        for k in range(MOE_TOPK):
            row_copy(t, k).start(priority=k % DMA_PRIORITIES)
        return 0

    lax.fori_loop(0, DISPATCH_TILE, start_rows, 0, unroll=ISSUE_UNROLL)
    for k in range(MOE_TOPK):
        pltpu.make_async_copy(x1_ref, xs_ref.at[pl.ds(0, DISPATCH_TILE * PACKED_SUB)], sem).wait()

    @pl.when(i == pl.num_programs(0) - 1)
    def _():
        for_blocks(FILL_FREE, lambda blk: fill_copy(blk, fill_sem).wait())


def _dispatch(dest, fill, x1t, n_rows):
    return pl.pallas_call(
        _dispatch_kernel,
        out_shape=jax.ShapeDtypeStruct((n_rows * PACKED_SUB, LANES), x1t.dtype),
        grid_spec=pltpu.PrefetchScalarGridSpec(
            num_scalar_prefetch=2,
            grid=(x1t.shape[0] // (DISPATCH_TILE * PACKED_SUB),),
            in_specs=[pl.BlockSpec((DISPATCH_TILE * PACKED_SUB, LANES), lambda i, d, t: (i, 0))],
            out_specs=pl.BlockSpec(memory_space=pl.ANY),
            scratch_shapes=[pltpu.VMEM((MOE_HALF * PACKED_SUB, LANES), x1t.dtype), pltpu.SemaphoreType.DMA(()),
                            pltpu.SemaphoreType.DMA(())],
        ),
        compiler_params=pltpu.CompilerParams(dimension_semantics=("arbitrary",),
                                             vmem_limit_bytes=VMEM_LIMIT_BYTES),
        name="dispatch",
    )(dest, fill, x1t)


def _experts_kernel(be_ref, slot_ref, nxt_ref, full_ref, nused_ref, x_ref, wg_hbm, bg_ref, wu_hbm, bu_ref, wd_hbm, bd_ref,
                    y_ref, wg_st, wu_st, wd_st, wg_bf, wu_bf, wd_bf, sems):
    i = pl.program_id(0)
    prev = be_ref[jnp.maximum(i - 1, 0)]
    changed = (i == 0) | (be_ref[i] != prev)
    active = i < nused_ref[0]
    slot = slot_ref[i]

    def weight_copies(expert, s):
        return (pltpu.make_async_copy(wg_hbm.at[expert], wg_st.at[s], sems.at[s, 0]),
                pltpu.make_async_copy(wu_hbm.at[expert], wu_st.at[s], sems.at[s, 1]),
                pltpu.make_async_copy(wd_hbm.at[expert], wd_st.at[s], sems.at[s, 2]))

    @pl.when(i == 0)
    def _():
        for cp in weight_copies(be_ref[0], slot):
            cp.start()

    @pl.when(active & changed)
    def _():
        for cp in weight_copies(be_ref[i], slot):
            cp.wait()

        @pl.when(nxt_ref[i] >= 0)
        def _():
            for cp in weight_copies(nxt_ref[i], 1 - slot):
                cp.start(priority=WEIGHT_DMA_PRIORITY)

        wg_bf[...] = wg_st[slot].astype(BF16)
        wu_bf[...] = wu_st[slot].astype(BF16)
        wd_bf[...] = wd_st[slot].astype(BF16)

    def mlp(nrows):
        rows = pl.ds(0, nrows * PACKED_SUB)
        xb = _unpack_tiles_to_rows(x_ref.at[rows], nrows)
        g = jnp.minimum(_dot(xb, wg_bf[...]) + bg_ref[0], SWIGLU_LIMIT)
        u = jnp.clip(_dot(xb, wu_bf[...]) + bu_ref[0], -SWIGLU_LIMIT, SWIGLU_LIMIT)
        h = g * (1.0 / (1.0 + jnp.exp(-SWIGLU_ALPHA * g))) * (u + 1.0)
        _pack_rows_to_tiles(y_ref.at[rows], _dot(h.astype(BF16), wd_bf[...]) + bd_ref[0])

    whole = active & (full_ref[i] > 0)

    @pl.when(whole)
    def _():
        mlp(MOE_GROUP)

    @pl.when(active & jnp.logical_not(whole))
    def _():
        mlp(MOE_HALF)

    @pl.when(jnp.logical_not(active))
    def _():
        y_ref[pl.ds(0, MOE_HALF * PACKED_SUB), :] = jnp.zeros((MOE_HALF * PACKED_SUB, LANES), y_ref.dtype)

    @pl.when(jnp.logical_not(whole))
    def _():
        y_ref[pl.ds(MOE_HALF * PACKED_SUB, MOE_HALF * PACKED_SUB), :] = jnp.zeros((MOE_HALF * PACKED_SUB, LANES), y_ref.dtype)


def _experts(blk_expert, blk_slot, blk_next, blk_full, n_used, xs, wg, bg, wu, bu, wd, bd):
    E, D, F = wg.shape
    nblk = xs.shape[0] // (MOE_GROUP * PACKED_SUB)

    def rowmap(i, be, sl, nx, fl, nu):
        return (jnp.minimum(i, nu[0] - 1), 0)

    def bmap(i, be, sl, nx, fl, nu):
        return (be[i], 0, 0)

    hbm = pl.BlockSpec(memory_space=pl.ANY)
    return pl.pallas_call(
        _experts_kernel,
        out_shape=jax.ShapeDtypeStruct(xs.shape, xs.dtype),
        grid_spec=pltpu.PrefetchScalarGridSpec(
            num_scalar_prefetch=5,
            grid=(nblk,),
            in_specs=[pl.BlockSpec((MOE_GROUP * PACKED_SUB, LANES), rowmap),
                      hbm, pl.BlockSpec((1, 1, F), bmap),
                      hbm, pl.BlockSpec((1, 1, F), bmap),
                      hbm, pl.BlockSpec((1, 1, D), bmap)],
            out_specs=pl.BlockSpec((MOE_GROUP * PACKED_SUB, LANES), lambda i, be, sl, nx, fl, nu: (i, 0)),
            scratch_shapes=[pltpu.VMEM((2, D, F), F32), pltpu.VMEM((2, D, F), F32), pltpu.VMEM((2, F, D), F32),
                            pltpu.VMEM((D, F), BF16), pltpu.VMEM((D, F), BF16), pltpu.VMEM((F, D), BF16),
                            pltpu.SemaphoreType.DMA((2, 3))],
        ),
        compiler_params=pltpu.CompilerParams(dimension_semantics=("arbitrary",),
                                             vmem_limit_bytes=VMEM_LIMIT_BYTES),
        name="experts",
    )(blk_expert, blk_slot, blk_next, blk_full, n_used, xs, wg, bg, wu, bu, wd, bd)


def _combine_kernel(alpha, dest_ref, gates_ref, x1_ref, g_ref, b_ref, y_ref, o_ref, ybuf, sems):
    i = pl.program_id(0)
    buf = i % 2

    n_tok = pl.num_programs(0) * COMBINE_TILE

    def gather_tile(tile, into):
        base = tile * COMBINE_TILE

        def start_rows(t, _):
            for k in range(MOE_TOPK):
                d = dest_ref[k * n_tok + base + t]
                pltpu.make_async_copy(y_ref.at[pl.ds(pl.multiple_of(d * PACKED_SUB, PACKED_SUB), PACKED_SUB)],
                                      ybuf.at[into, k, pl.ds(pl.multiple_of(t * PACKED_SUB, PACKED_SUB), PACKED_SUB)],
                                      sems.at[into]).start(priority=k % DMA_PRIORITIES)
            return 0

        lax.fori_loop(0, COMBINE_TILE, start_rows, 0, unroll=ISSUE_UNROLL)

    @pl.when(i == 0)
    def _():
        gather_tile(0, 0)

    @pl.when(i + 1 < pl.num_programs(0))
    def _():
        gather_tile(i + 1, 1 - buf)

    for k in range(MOE_TOPK):
        pltpu.make_async_copy(y_ref.at[pl.ds(0, COMBINE_TILE * PACKED_SUB)], ybuf.at[buf, k], sems.at[buf]).wait()

    gates = gates_ref[...]
    ffn = gates[:, 0:1] * _unpack_tiles_to_rows(ybuf.at[buf, 0], COMBINE_TILE).astype(F32)
    for k in range(1, MOE_TOPK):
        ffn = ffn + gates[:, k:k + 1] * _unpack_tiles_to_rows(ybuf.at[buf, k], COMBINE_TILE).astype(F32)
    o_ref[...] = _layer_norm(alpha * x1_ref[...] + ffn, g_ref[...], b_ref[...])


def _combine(dest, gates, x1, g2, b2, y_rows, alpha):
    N, D = x1.shape
    row = lambda i, d: (i, 0)
    full = lambda i, d: (0, 0)
    return pl.pallas_call(
        functools.partial(_combine_kernel, alpha),
        out_shape=jax.ShapeDtypeStruct((N, D), F32),
        grid_spec=pltpu.PrefetchScalarGridSpec(
            num_scalar_prefetch=1,
            grid=(N // COMBINE_TILE,),
            in_specs=[pl.BlockSpec((COMBINE_TILE, LANES), row), pl.BlockSpec((COMBINE_TILE, D), row),
                      pl.BlockSpec(g2.shape, full), pl.BlockSpec(b2.shape, full),
                      pl.BlockSpec(memory_space=pl.ANY)],
            out_specs=pl.BlockSpec((COMBINE_TILE, D), row),
            scratch_shapes=[pltpu.VMEM((2, MOE_TOPK, COMBINE_TILE * PACKED_SUB, LANES), y_rows.dtype),
                            pltpu.SemaphoreType.DMA((2,))],
        ),
        compiler_params=pltpu.CompilerParams(dimension_semantics=("arbitrary",),
                                             vmem_limit_bytes=VMEM_LIMIT_BYTES),
        name="combine",
    )(dest, gates, x1, g2, b2, y_rows)


def _rot_partner(w, half):
    return jnp.concatenate([-w[..., half:2 * half], w[..., :half]], axis=-1)


def _layer_weights(w_in, w_q_b, w_kv_b):
    D = w_in.shape[0]
    o1 = Q_LORA
    o2 = o1 + KV_LORA
    o3 = o2 + MLA_ROPE
    w_ql, w_kvl, w_kr = w_in[:, :o1], w_in[:, o1:o2], w_in[:, o2:o3]
    w_mq, w_mk, w_mv = w_in[:, o3:o3 + _MH], w_in[:, o3 + _MH:o3 + 2 * _MH], w_in[:, o3 + 2 * _MH:]
    zpad = lambda n: jnp.zeros((D, n), w_in.dtype)
    tail = SLOT - MLA_NOPE - MLA_ROPE
    assert tail == MLA_ROPE
    kr_slot = jnp.concatenate([zpad(MLA_NOPE), w_kr, _rot_partner(w_kr, MLA_ROPE // 2)], axis=1)

    def moba_partner(w):
        w3 = w.reshape(D, MOBA_HEADS, MOBA_HD)
        part = jnp.concatenate([_rot_partner(w3[..., :MOBA_ROT], MOBA_ROT // 2),
                                jnp.zeros((D, MOBA_HEADS, MOBA_HD - MOBA_ROT), w.dtype)], axis=-1)
        return part.reshape(D, _MH)

    win = jnp.concatenate([w_ql, w_kvl, kr_slot, w_mk, moba_partner(w_mk)], axis=1).astype(BF16)
    mq_partner = _rot_partner(w_mq.reshape(D, MOBA_HEADS, MOBA_HD)[..., :MOBA_ROT], MOBA_ROT // 2)
    wmt = jnp.concatenate([w_mq, mq_partner.reshape(D, MOBA_HEADS * MOBA_ROT), w_mv], axis=1).T.astype(BF16)

    wq3 = w_q_b.reshape(Q_LORA, MLA_HEADS, MLA_NOPE + MLA_ROPE)
    wqs = _rot_partner(wq3[..., MLA_NOPE:], MLA_ROPE // 2).reshape(Q_LORA, MLA_HEADS * MLA_ROPE)
    wkv3 = w_kv_b.reshape(KV_LORA, MLA_HEADS, MLA_NOPE + MLA_V)
    wk = jnp.concatenate([wkv3[..., :MLA_NOPE], jnp.zeros((KV_LORA, MLA_HEADS, SLOT - MLA_NOPE), w_kv_b.dtype)],
                         axis=-1).reshape(KV_LORA, MLA_HEADS * SLOT).astype(BF16)
    wv = wkv3[..., MLA_NOPE:].reshape(KV_LORA, MLA_HEADS * MLA_V)
    return win, wmt, w_q_b.T.astype(BF16), wqs.T.astype(BF16), wk, wv.T.astype(BF16)


def _rope_tables(positions):
    pos = positions.astype(F32).reshape(1, -1)

    def cs(d_rot):
        inv_freq = ROPE_THETA ** (-jnp.arange(0, d_rot, 2, dtype=F32) / d_rot)
        ang = inv_freq.reshape(d_rot // 2, 1) * pos
        return jnp.cos(ang), jnp.sin(ang)

    return cs(MLA_ROPE) + cs(MOBA_ROT)


def kernel(x, positions, w_in, q_a_norm, w_q_b, kv_a_norm, w_kv_b, w_o, ln1_g, ln1_b, w_router, b_router,
           w_gate, b_gate, w_up, b_up, w_down, b_down, ln2_g, ln2_b):
    B, T, D = x.shape
    depth = w_in.shape[0]
    alpha = (2.0 * depth) ** 0.25
    N = B * T
    assert T % MOBA_BLOCK == 0 and T // MOBA_BLOCK <= SUBLANES and N % OPROJ_TILE == 0
    assert D == SUBLANES * LANES
    n_asg = N * MOE_TOPK
    n_rows = n_asg + N_EXPERTS * MOE_GROUP
    tables = _rope_tables(positions)
    h = x.reshape(N, D)
    for l in range(depth):
        win, wmt, wqt, wqst, wk, wvt = _layer_weights(w_in[l], w_q_b[l], w_kv_b[l])
        qa, ka, va, mq, mk, mv = _prep(h, win, wmt, wqt, wqst, wk, wvt, q_a_norm[l].reshape(1, -1),
                                       kv_a_norm[l].reshape(1, -1), tables, B, T)
        a = _attention(qa, ka, va, B, T, "mla_attention")
        m = _attention(mq, mk, mv, B, T, "moba_attention")
        wr_pad = jnp.concatenate([w_router[l], jnp.zeros((D, LANES - N_EXPERTS), F32)], axis=1)
        wr_hi = wr_pad.astype(BF16)
        wr = jnp.concatenate([wr_hi, (wr_pad - wr_hi.astype(F32)).astype(BF16)], axis=1)
        br = b_router[l].reshape(N_EXPERTS, 1)
        x1, x1t, route, gates, cnt = _oproj(a, m, h, w_o[l].astype(BF16), ln1_g[l].reshape(1, D), ln1_b[l].reshape(1, D),
                                       wr, br, alpha)
        er = jnp.arange(N_EXPERTS, dtype=jnp.int32)
        counts = cnt[:, 0].astype(jnp.int32)
        padded = (counts + MOE_GROUP - 1) // MOE_GROUP * MOE_GROUP
        upto = er[None, :] <= er[:, None]
        pad_end = jnp.sum(jnp.where(upto, padded[None, :], 0), axis=1)
        pad_start = pad_end - padded
        e_idx = route[:MOE_TOPK]
        group_start = jnp.sum(jnp.where(e_idx[..., None] == er, pad_start, 0), axis=-1)
        dest = (group_start + route[MOE_TOPK:2 * MOE_TOPK]).reshape(n_asg)
        n_used = (pad_end[-1:] // MOE_GROUP).astype(jnp.int32)
        nblk = n_rows // MOE_GROUP
        blk_start = jnp.arange(nblk, dtype=jnp.int32) * MOE_GROUP
        blk_expert = jnp.minimum(jnp.sum((blk_start[:, None] >= pad_end[None, :]).astype(jnp.int32), axis=1),
                                 N_EXPERTS - 1)
        of_expert = blk_expert[:, None] == er[None, :]
        pick = lambda per_expert: jnp.sum(jnp.where(of_expert, per_expert[None, :], 0), axis=1)
        nonempty = padded > 0
        group_index = jnp.sum(jnp.where(upto & nonempty[None, :], 1, 0), axis=1) - 1
        later = (er[None, :] > er[:, None]) & nonempty[None, :]
        next_expert = jnp.min(jnp.where(later, er[None, :], N_EXPERTS), axis=1)
        next_expert = jnp.where(next_expert == N_EXPERTS, -1, next_expert)
        blk_slot = pick(group_index) % 2
        blk_next = pick(next_expert)
        blk_full = (pick(pad_start + counts) > blk_start + MOE_HALF).astype(jnp.int32)
        half_start = jnp.arange(n_rows // MOE_HALF, dtype=jnp.int32) * MOE_HALF
        half_expert = jnp.minimum(jnp.sum((half_start[:, None] >= pad_end[None, :]).astype(jnp.int32), axis=1),
                                  N_EXPERTS - 1)
        half_real_end = jnp.sum(jnp.where(half_expert[:, None] == er[None, :], (pad_start + counts)[None, :], 0), axis=1)
        no_real_rows = (half_start >= pad_end[-1]) | (half_start >= half_real_end)
        fill = jnp.where(no_real_rows, FILL_FREE,
                         jnp.where(half_start + MOE_HALF > half_real_end, FILL_TAIL, 0)).astype(jnp.int32)
        xs = _dispatch(dest, fill, x1t, n_rows)
        y_rows = _experts(blk_expert, blk_slot.astype(jnp.int32), blk_next, blk_full, n_used, xs, w_gate[l], b_gate[l].reshape(N_EXPERTS, 1, -1),
                          w_up[l], b_up[l].reshape(N_EXPERTS, 1, -1), w_down[l], b_down[l].reshape(N_EXPERTS, 1, -1))
        h = _combine(dest, gates, x1, ln2_g[l].reshape(1, D), ln2_b[l].reshape(1, D), y_rows, alpha)
    return h.reshape(B, T, D)
```

```python
import functools
import math

import jax
import jax.numpy as jnp
from jax import lax
from jax.experimental import pallas as pl
from jax.experimental.pallas import tpu as pltpu

ROPE_THETA = 500000.0
MLA_HEADS = 8
MLA_NOPE = 64
MLA_ROPE = 32
MLA_V = 64
Q_LORA = 256
KV_LORA = 128
MOBA_HEADS = 8
MOBA_HD = 64
MOBA_ROT = MOBA_HD // 4
MOBA_BLOCK = 256
MOBA_TOPK = 3
N_EXPERTS = 32
MOE_TOPK = 4
SWIGLU_LIMIT = 7.0
SWIGLU_ALPHA = 1.702
RMS_EPS = 1e-6
LN_EPS = 1e-5

LANES = 128
SUBLANES = 8
VMEM_LIMIT_BYTES = 56 * 1024 * 1024

SLOT = LANES
TQ = MOBA_BLOCK
ROW_TILE = 256
OPROJ_TILE = 512
MOE_GROUP = 512
MOE_HALF = MOE_GROUP // 2
PACKED_SUB = SUBLANES // 2
FILL_TAIL, FILL_FREE = 1, 2
DISPATCH_TILE = 512
COMBINE_TILE = 512
ISSUE_UNROLL = 4
DMA_PRIORITIES = 2
WEIGHT_DMA_PRIORITY = 1
HEAD_LANES = 64
ONES_LANE = 64
BIAS_LANE = 64
ATTN_HEADS_PER_STEP = 8
LOG2E = math.log2(math.e)
NEG_BIG = -(2.0 ** 100)

F32 = jnp.float32
BF16 = jnp.bfloat16
NT_DIMS = (((1,), (1,)), ((), ()))
TN_DIMS = (((0,), (0,)), ((), ()))


def _dot(a, b, precision=None):
    return jnp.dot(a, b, preferred_element_type=F32, precision=precision)


def _dot_nt(a, b, precision=None):
    return lax.dot_general(a, b, NT_DIMS, preferred_element_type=F32, precision=precision)


def _pack_rows_to_tiles(ref, x):
    rows, width = x.shape
    bits = lambda v: lax.bitcast_convert_type(v.astype(BF16).astype(F32), jnp.uint32)
    words = (bits(x[:, :width // 2]) >> 16) | (bits(x[:, width // 2:]) & jnp.uint32(0xFFFF0000))
    for c in range(PACKED_SUB):
        ref[pl.ds(c, rows, stride=PACKED_SUB), :] = words[:, c * LANES:(c + 1) * LANES]


def _unpack_tiles_to_rows(ref, rows):
    words = jnp.concatenate([ref[pl.ds(c, rows, stride=PACKED_SUB), :] for c in range(PACKED_SUB)], axis=1)
    low = lax.bitcast_convert_type(words << 16, F32).astype(BF16)
    high = lax.bitcast_convert_type(words & jnp.uint32(0xFFFF0000), F32).astype(BF16)
    return jnp.concatenate([low, high], axis=1)


def _rms(x, g):
    return x * lax.rsqrt(jnp.mean(x * x, axis=-1, keepdims=True) + RMS_EPS) * g


def _layer_norm(x, g, b):
    mu = jnp.mean(x, axis=-1, keepdims=True)
    xc = x - mu
    var = jnp.mean(xc * xc, axis=-1, keepdims=True)
    return xc * lax.rsqrt(var + LN_EPS) * g + b


_C_QL = 0
_C_KVL = _C_QL + Q_LORA
_C_KR = _C_KVL + KV_LORA
_C_MK = _C_KR + SLOT
_MH = MOBA_HEADS * MOBA_HD
_C_MKS = _C_MK + _MH
_C_END = _C_MKS + _MH


def _prep_kernel(x_ref, win_ref, wmt_ref, wqt_ref, wqst_ref, wk_ref, wvt_ref, qg_ref, kvg_ref,
                 cat_ref, sat_ref, cbt_ref, sbt_ref,
                 qat_ref, ka_ref, vat_ref, mqt_ref, mk_ref, mvt_ref, kmean_scr):
    c = pl.program_id(1)

    @pl.when(c == 0)
    def _():
        kmean_scr[...] = jnp.zeros_like(kmean_scr)

    xb = x_ref[...].astype(BF16)
    both_halves = lambda t: jnp.concatenate([t, t], axis=0)
    cat = both_halves(cat_ref[...])
    sat = both_halves(sat_ref[...])
    cbt = both_halves(cbt_ref[...])
    sbt = both_halves(sbt_ref[...])
    const = lambda v, n: jnp.full((n, ROW_TILE), v, F32)
    tail = SLOT - MLA_NOPE - MLA_ROPE
    rest = MOBA_HD - MOBA_ROT
    ca = jnp.concatenate([const(1.0, MLA_NOPE), cat, const(0.0, tail)], axis=0).T
    sa = jnp.concatenate([const(0.0, MLA_NOPE), sat, const(0.0, tail)], axis=0).T
    cb = jnp.concatenate([cbt, const(1.0, rest), cbt, const(1.0, rest)], axis=0).T
    sb = jnp.concatenate([sbt, const(0.0, rest), sbt, const(0.0, rest)], axis=0).T
    lane = lax.broadcasted_iota(jnp.int32, (ROW_TILE, SLOT), 1)
    head_lanes = lane < HEAD_LANES
    ones_rows = (lax.broadcasted_iota(jnp.int32, (SLOT - HEAD_LANES, ROW_TILE), 0) == ONES_LANE - HEAD_LANES).astype(F32)

    ql = _dot(xb, win_ref[:, _C_QL:_C_KVL])
    kvl = _dot(xb, win_ref[:, _C_KVL:_C_KR])
    kr = _dot(xb, win_ref[:, _C_KR:_C_MK])
    krs = pltpu.roll(kr, SLOT - MLA_ROPE, axis=1)
    qn = _rms(ql, qg_ref[...]).astype(BF16)
    kvn = _rms(kvl, kvg_ref[...]).astype(BF16)
    q_t = _dot_nt(wqt_ref[...], qn)
    qs_t = _dot_nt(wqst_ref[...], qn)
    kn = _dot(kvn, wk_ref[...])
    v_t = _dot_nt(wvt_ref[...], kvn)
    scale_a = LOG2E / math.sqrt(MLA_NOPE + MLA_ROPE)
    kro = kr * ca + krs * sa
    qd = MLA_NOPE + MLA_ROPE
    q_pad = jnp.zeros((SLOT - qd, ROW_TILE), F32)
    for h in range(MLA_HEADS):
        sl = slice(h * SLOT, (h + 1) * SLOT)
        q_rope = q_t[h * qd + MLA_NOPE:(h + 1) * qd] * cat + qs_t[h * MLA_ROPE:(h + 1) * MLA_ROPE] * sat
        q_slot = jnp.concatenate([q_t[h * qd:h * qd + MLA_NOPE], q_rope, q_pad], axis=0)
        qat_ref[0, sl, :] = (q_slot * scale_a).astype(BF16)
        ka_ref[:, sl] = (kn[:, sl] + kro).astype(BF16)
        vat_ref[0, sl, :] = jnp.concatenate([v_t[h * MLA_V:(h + 1) * MLA_V], ones_rows], axis=0).astype(BF16)

    mk = _dot(xb, win_ref[:, _C_MK:_C_MKS])
    mks = _dot(xb, win_ref[:, _C_MKS:_C_END])
    npair = _MH // LANES
    pair = lambda a, j: a[:, j * LANES:(j + 1) * LANES]
    mk_rot = [pair(mk, j) * cb + pair(mks, j) * sb for j in range(npair)]
    mk_all = jnp.concatenate(mk_rot, axis=1)

    nrot = MOBA_HEADS * MOBA_ROT
    mq_t = _dot_nt(wmt_ref[0:_MH, :], xb)
    mqs_t = _dot_nt(wmt_ref[_MH:_MH + nrot, :], xb)
    mv_t = _dot_nt(wmt_ref[_MH + nrot:2 * _MH + nrot, :], xb)
    head = lambda a, h: a[h * MOBA_HD:(h + 1) * MOBA_HD]
    mq_rot_t = [jnp.concatenate([head(mq_t, h)[0:MOBA_ROT] * cbt + mqs_t[h * MOBA_ROT:(h + 1) * MOBA_ROT] * sbt,
                                 head(mq_t, h)[MOBA_ROT:]], axis=0) for h in range(MOBA_HEADS)]

    nrow = MOBA_HEADS * SUBLANES
    row_i = lax.broadcasted_iota(jnp.int32, (nrow, _MH), 0)
    lane_i = lax.broadcasted_iota(jnp.int32, (nrow, _MH), 1)
    kmean_c = jnp.mean(mk_all, axis=0, keepdims=True)
    put = ((row_i % SUBLANES) == c) & ((lane_i // MOBA_HD) == (row_i // SUBLANES))
    table = kmean_scr[...]
    gate_t = _dot(table, jnp.concatenate(mq_rot_t, axis=0), precision=lax.Precision.HIGHEST)
    kmean_scr[...] = jnp.where(put, jnp.broadcast_to(kmean_c, (nrow, _MH)), table)

    n_idx = lax.broadcasted_iota(jnp.int32, (SUBLANES, ROW_TILE), 0)
    valid = n_idx < c
    scale_b = LOG2E / math.sqrt(MOBA_HD)
    zero_rows = jnp.zeros((SLOT - HEAD_LANES - SUBLANES, ROW_TILE), F32)
    block_onehot = (lane == BIAS_LANE + c).astype(F32)
    for h in range(MOBA_HEADS):
        g = jnp.where(valid, gate_t[h * SUBLANES:(h + 1) * SUBLANES, :], -jnp.inf)
        rank = jnp.zeros((SUBLANES, ROW_TILE), jnp.int32)
        for k in range(1, SUBLANES):
            other = pltpu.roll(g, k, axis=0)
            other_n = pltpu.roll(n_idx, k, axis=0)
            beats = (other > g) | ((other == g) & (other_n < n_idx))
            rank = rank + beats.astype(jnp.int32)
        keep = (valid & (rank < MOBA_TOPK)) | (n_idx == c)
        bias = jnp.where(keep, 0.0, NEG_BIG)
        sl = slice(h * SLOT, (h + 1) * SLOT)
        mqt_ref[0, sl, :] = jnp.concatenate([mq_rot_t[h] * scale_b, bias, zero_rows], axis=0).astype(BF16)
        mvt_ref[0, sl, :] = jnp.concatenate([head(mv_t, h), ones_rows], axis=0).astype(BF16)
        j, hh = divmod(h, 2)
        k_h = mk_rot[j] if hh == 0 else pltpu.roll(mk_rot[j], HEAD_LANES, axis=1)
        mk_ref[:, sl] = jnp.where(head_lanes, k_h, block_onehot).astype(BF16)


def _prep(x2, win, wmt, wqt, wqst, wk, wvt, qg, kvg, tables, B, T):
    N, D = x2.shape
    nt = T // ROW_TILE
    row = lambda b, c: (b * nt + c, 0)
    col = lambda b, c: (0, b * nt + c)
    full = lambda b, c: (0, 0)
    cat, sat, cbt, sbt = tables
    width = MLA_HEADS * SLOT

    def rows(w):
        return pl.BlockSpec((ROW_TILE, w), row)

    def cols(a):
        return pl.BlockSpec((a.shape[0], ROW_TILE), col)

    def whole(a):
        return pl.BlockSpec(a.shape, full)

    rowmajor = jax.ShapeDtypeStruct((N, width), BF16)
    transposed = jax.ShapeDtypeStruct((B * nt, width, ROW_TILE), BF16)
    t_spec = pl.BlockSpec((1, width, ROW_TILE), lambda b, c: (b * nt + c, 0, 0))
    return pl.pallas_call(
        _prep_kernel,
        out_shape=(transposed, rowmajor, transposed, transposed, rowmajor, transposed),
        grid=(B, nt),
        in_specs=[rows(D), whole(win), whole(wmt), whole(wqt), whole(wqst), whole(wk), whole(wvt), whole(qg), whole(kvg),
                  cols(cat), cols(sat), cols(cbt), cols(sbt)],
        out_specs=(t_spec, rows(width), t_spec, t_spec, rows(width), t_spec),
        scratch_shapes=[pltpu.VMEM((MOBA_HEADS * SUBLANES, _MH), F32)],
        compiler_params=pltpu.CompilerParams(dimension_semantics=("arbitrary", "arbitrary"),
                                             vmem_limit_bytes=VMEM_LIMIT_BYTES),
        name="prep",
    )(x2, win, wmt, wqt, wqst, wk, wvt, qg, kvg, cat, sat, cbt, sbt)


def _attn_kernel(qt_ref, k_ref, vt_ref, o_ref, s_scr, acc_scr):
    i = pl.program_id(2)
    nh = ATTN_HEADS_PER_STEP
    slot = lambda h: slice(h * SLOT, (h + 1) * SLOT)
    nblk = i + 1
    nquad = nblk // 4
    pair_end = 4 * nquad + 2 * ((nblk - 4 * nquad) // 2)
    q_pos = i * TQ + lax.broadcasted_iota(jnp.int32, (1, TQ), 1)

    def fold(s):
        out = s[0:SUBLANES]
        for t in range(1, s.shape[0] // SUBLANES):
            out = jnp.maximum(out, s[t * SUBLANES:(t + 1) * SUBLANES])
        return out

    def score_blocks(j, n, mrun):
        j0 = pl.multiple_of(j * TQ, TQ)
        visible = (j0 + lax.broadcasted_iota(jnp.int32, (n * TQ, TQ), 0)) <= q_pos
        out = []
        for h in range(nh):
            s = _dot(k_ref[pl.ds(j0, n * TQ), slot(h)], qt_ref[0, slot(h), :])
            s = jnp.where(visible, s, -jnp.inf)
            s_scr[h, pl.ds(j, n)] = s.reshape(n, TQ, TQ)
            out.append(jnp.maximum(mrun[h], fold(s)))
        return tuple(out)

    mrun = tuple(jnp.full((SUBLANES, TQ), -jnp.inf, F32) for _ in range(nh))
    mrun = lax.fori_loop(0, nquad, lambda jq, m: score_blocks(4 * jq, 4, m), mrun)
    mrun = lax.fori_loop(2 * nquad, pair_end // 2, lambda jj, m: score_blocks(2 * jj, 2, m), mrun)
    mrun = lax.fori_loop(pair_end, nblk, lambda j, m: score_blocks(j, 1, m), mrun)
    ms = [jnp.max(m, axis=0, keepdims=True) for m in mrun]

    def accumulate(j, n, carry):
        for h in range(nh):
            p = jnp.exp2(s_scr[h, pl.ds(j, n)].reshape(n * TQ, TQ) - ms[h]).astype(BF16)
            vt = jnp.concatenate([vt_ref[j + t, slot(h), :] for t in range(n)], axis=1)
            acc_scr[h] += _dot(vt, p)
        return carry

    acc_scr[...] = jnp.zeros_like(acc_scr)
    lax.fori_loop(0, nquad, lambda jq, c: accumulate(4 * jq, 4, c), 0)
    lax.fori_loop(2 * nquad, pair_end // 2, lambda jj, c: accumulate(2 * jj, 2, c), 0)
    lax.fori_loop(pair_end, nblk, lambda j, c: accumulate(j, 1, c), 0)
    outs = [acc_scr[h, 0:HEAD_LANES] / acc_scr[h, ONES_LANE:ONES_LANE + 1] for h in range(nh)]
    for h in range(nh):
        o_ref[0, h * HEAD_LANES:(h + 1) * HEAD_LANES, :] = outs[h].astype(o_ref.dtype)


def _attention(qt, k, vt, B, T, name):
    N = k.shape[0]
    nq = T // TQ
    nh = ATTN_HEADS_PER_STEP
    heads = k.shape[1] // SLOT
    return pl.pallas_call(
        _attn_kernel,
        out_shape=jax.ShapeDtypeStruct((B * nq, heads * HEAD_LANES, TQ), BF16),
        grid=(B, heads // nh, nq),
        in_specs=[pl.BlockSpec((1, nh * SLOT, TQ), lambda b, g, i: (b * nq + i, g, 0)),
                  pl.BlockSpec((T, nh * SLOT), lambda b, g, i: (b, g)),
                  pl.BlockSpec((nq, nh * SLOT, TQ), lambda b, g, i: (b, g, 0))],
        out_specs=pl.BlockSpec((1, nh * HEAD_LANES, TQ), lambda b, g, i: (b * nq + i, g, 0)),
        scratch_shapes=[pltpu.VMEM((nh, nq, TQ, TQ), F32), pltpu.VMEM((nh, SLOT, TQ), F32)],
        compiler_params=pltpu.CompilerParams(dimension_semantics=("arbitrary", "arbitrary", "arbitrary"),
                                             vmem_limit_bytes=VMEM_LIMIT_BYTES),
        name=name,
    )(qt, k, vt)


def _oproj_kernel(alpha, a_ref, m_ref, x_ref, wo_ref, g_ref, b_ref, wr_ref, br_ref,
                  x1_ref, x1t_ref, route_ref, gates_ref, cnt_ref, carry_scr):
    i = pl.program_id(0)

    @pl.when(i == 0)
    def _():
        carry_scr[...] = jnp.zeros_like(carry_scr)

    parts = []
    for t in range(a_ref.shape[0]):
        lhs_t = jnp.concatenate([a_ref[t], m_ref[t]], axis=0)
        parts.append(lax.dot_general(lhs_t, wo_ref[...], TN_DIMS, preferred_element_type=F32))
    mix = jnp.concatenate(parts, axis=0)
    x1 = _layer_norm(alpha * x_ref[...] + mix, g_ref[...], b_ref[...])
    x1_ref[...] = x1
    _pack_rows_to_tiles(x1t_ref, x1)

    ne = br_ref.shape[0]
    rows = x1.shape[0]
    x_hi = x1.astype(BF16)
    x_lo = (x1 - x_hi.astype(F32)).astype(BF16)
    both = _dot(x_hi, wr_ref[...])
    logits_rm = both[:, :LANES] + both[:, LANES:] + _dot(x_lo, wr_ref[:, :LANES])
    logits = logits_rm.T[0:ne] + br_ref[...]
    expert = lax.broadcasted_iota(jnp.int32, logits.shape, 0)
    expert_f = expert.astype(F32)
    vals, idxs = [], []
    work = logits
    for _ in range(MOE_TOPK):
        mx = jnp.max(work, axis=0, keepdims=True)
        ix = jnp.min(jnp.where(work == mx, expert_f, float(ne)), axis=0, keepdims=True).astype(jnp.int32)
        vals.append(mx)
        idxs.append(ix)
        work = jnp.where(expert == ix, -jnp.inf, work)
    exps = [jnp.exp(v - vals[0]) for v in vals]
    den = exps[0]
    for e in exps[1:]:
        den = den + e

    onehot = jnp.zeros(logits.shape, F32)
    for ix in idxs:
        onehot = onehot + (expert == ix).astype(F32)
    r = lax.broadcasted_iota(jnp.int32, (rows, rows), 0)
    cidx = lax.broadcasted_iota(jnp.int32, (rows, rows), 1)
    earlier = (r < cidx).astype(BF16)
    carry = carry_scr[:, 0:1]
    before = _dot(onehot.astype(BF16), earlier) + carry
    row8 = lax.broadcasted_iota(jnp.int32, (SUBLANES, rows), 0)
    route = jnp.zeros((SUBLANES, rows), jnp.int32)
    gates = jnp.zeros((SUBLANES, rows), F32)
    for k in range(MOE_TOPK):
        rank = jnp.sum(jnp.where(expert == idxs[k], before, 0.0), axis=0, keepdims=True).astype(jnp.int32)
        route = jnp.where(row8 == k, idxs[k], route)
        route = jnp.where(row8 == MOE_TOPK + k, rank, route)
        gates = jnp.where(row8 == k, exps[k] / den, gates)
    route_ref[...] = route
    gates_ref[...] = jnp.concatenate([gates, jnp.zeros((LANES - SUBLANES, rows), F32)], axis=0).T
    new_carry = carry + jnp.sum(onehot, axis=1, keepdims=True)
    carry_scr[...] = jnp.broadcast_to(new_carry, carry_scr.shape)
    cnt_ref[...] = jnp.broadcast_to(new_carry, cnt_ref.shape)


def _oproj(a, m, x2, wo, g1, b1, wr, br, alpha):
    N, D = x2.shape
    nt = N // OPROJ_TILE
    row = lambda i: (i, 0)
    full = lambda i: (0, 0)
    return pl.pallas_call(
        functools.partial(_oproj_kernel, alpha),
        out_shape=(jax.ShapeDtypeStruct((N, D), F32),
                   jax.ShapeDtypeStruct((N * PACKED_SUB, LANES), jnp.uint32),
                   jax.ShapeDtypeStruct((SUBLANES, N), jnp.int32),
                   jax.ShapeDtypeStruct((N, LANES), F32),
                   jax.ShapeDtypeStruct((br.shape[0], LANES), F32)),
        grid=(nt,),
        in_specs=[pl.BlockSpec((OPROJ_TILE // TQ,) + a.shape[1:], lambda i: (i, 0, 0)),
                  pl.BlockSpec((OPROJ_TILE // TQ,) + m.shape[1:], lambda i: (i, 0, 0)),
                  pl.BlockSpec((OPROJ_TILE, D), row), pl.BlockSpec(wo.shape, full),
                  pl.BlockSpec(g1.shape, full), pl.BlockSpec(b1.shape, full),
                  pl.BlockSpec(wr.shape, full), pl.BlockSpec(br.shape, full)],
        out_specs=(pl.BlockSpec((OPROJ_TILE, D), row), pl.BlockSpec((OPROJ_TILE * PACKED_SUB, LANES), row),
                   pl.BlockSpec((SUBLANES, OPROJ_TILE), lambda i: (0, i)),
                   pl.BlockSpec((OPROJ_TILE, LANES), row), pl.BlockSpec((br.shape[0], LANES), full)),
        scratch_shapes=[pltpu.VMEM((br.shape[0], LANES), F32)],
        compiler_params=pltpu.CompilerParams(dimension_semantics=("arbitrary",),
                                             vmem_limit_bytes=VMEM_LIMIT_BYTES),
        name="oproj_router",
    )(a, m, x2, wo, g1, b1, wr, br)


def _dispatch_kernel(dest_ref, fill_ref, x1_ref, xs_ref, zero_scr, sem, fill_sem):
    i = pl.program_id(0)
    fill_sub = MOE_HALF * PACKED_SUB
    nfill = xs_ref.shape[0] // fill_sub

    def fill_copy(blk, on):
        return pltpu.make_async_copy(zero_scr, xs_ref.at[pl.ds(pl.multiple_of(blk * fill_sub, fill_sub), fill_sub)], on)

    def for_blocks(kind, action):
        def body(blk, _):
            @pl.when(fill_ref[blk] == kind)
            def _():
                action(blk)
            return 0

        lax.fori_loop(0, nfill, body, 0)

    @pl.when(i == 0)
    def _():
        zero_scr[...] = jnp.zeros_like(zero_scr)
        for_blocks(FILL_TAIL, lambda blk: fill_copy(blk, sem).start())
        for_blocks(FILL_FREE, lambda blk: fill_copy(blk, fill_sem).start())
        for_blocks(FILL_TAIL, lambda blk: fill_copy(blk, sem).wait())

    base = i * DISPATCH_TILE
    n_tok = pl.num_programs(0) * DISPATCH_TILE

    def row_copy(t, k):
        d = dest_ref[k * n_tok + base + t]
        return pltpu.make_async_copy(x1_ref.at[pl.ds(pl.multiple_of(t * PACKED_SUB, PACKED_SUB), PACKED_SUB)],
                                     xs_ref.at[pl.ds(pl.multiple_of(d * PACKED_SUB, PACKED_SUB), PACKED_SUB)], sem)

    def start_rows(t, _):
        for k in range(MOE_TOPK):
            row_copy(t, k).start(priority=k % DMA_PRIORITIES)
        return 0

    lax.fori_loop(0, DISPATCH_TILE, start_rows, 0, unroll=ISSUE_UNROLL)
    for k in range(MOE_TOPK):
        pltpu.make_async_copy(x1_ref, xs_ref.at[pl.ds(0, DISPATCH_TILE * PACKED_SUB)], sem).wait()

    @pl.when(i == pl.num_programs(0) - 1)
    def _():
        for_blocks(FILL_FREE, lambda blk: fill_copy(blk, fill_sem).wait())


def _dispatch(dest, fill, x1t, n_rows):
    return pl.pallas_call(
        _dispatch_kernel,
        out_shape=jax.ShapeDtypeStruct((n_rows * PACKED_SUB, LANES), x1t.dtype),
        grid_spec=pltpu.PrefetchScalarGridSpec(
            num_scalar_prefetch=2,
            grid=(x1t.shape[0] // (DISPATCH_TILE * PACKED_SUB),),
            in_specs=[pl.BlockSpec((DISPATCH_TILE * PACKED_SUB, LANES), lambda i, d, t: (i, 0))],
            out_specs=pl.BlockSpec(memory_space=pl.ANY),
            scratch_shapes=[pltpu.VMEM((MOE_HALF * PACKED_SUB, LANES), x1t.dtype), pltpu.SemaphoreType.DMA(()),
                            pltpu.SemaphoreType.DMA(())],
        ),
        compiler_params=pltpu.CompilerParams(dimension_semantics=("arbitrary",),
                                             vmem_limit_bytes=VMEM_LIMIT_BYTES),
        name="dispatch",
    )(dest, fill, x1t)


def _experts_kernel(be_ref, slot_ref, nxt_ref, full_ref, nused_ref, x_ref, wg_hbm, bg_ref, wu_hbm, bu_ref, wd_hbm, bd_ref,
                    y_ref, wg_st, wu_st, wd_st, wg_bf, wu_bf, wd_bf, sems):
    i = pl.program_id(0)
    prev = be_ref[jnp.maximum(i - 1, 0)]
    changed = (i == 0) | (be_ref[i] != prev)
    active = i < nused_ref[0]
    slot = slot_ref[i]

    def weight_copies(expert, s):
        return (pltpu.make_async_copy(wg_hbm.at[expert], wg_st.at[s], sems.at[s, 0]),
                pltpu.make_async_copy(wu_hbm.at[expert], wu_st.at[s], sems.at[s, 1]),
                pltpu.make_async_copy(wd_hbm.at[expert], wd_st.at[s], sems.at[s, 2]))

    @pl.when(i == 0)
    def _():
        for cp in weight_copies(be_ref[0], slot):
            cp.start()

    @pl.when(active & changed)
    def _():
        for cp in weight_copies(be_ref[i], slot):
            cp.wait()

        @pl.when(nxt_ref[i] >= 0)
        def _():
            for cp in weight_copies(nxt_ref[i], 1 - slot):
                cp.start(priority=WEIGHT_DMA_PRIORITY)

        wg_bf[...] = wg_st[slot].astype(BF16)
        wu_bf[...] = wu_st[slot].astype(BF16)
        wd_bf[...] = wd_st[slot].astype(BF16)

    def mlp(nrows):
        rows = pl.ds(0, nrows * PACKED_SUB)
        xb = _unpack_tiles_to_rows(x_ref.at[rows], nrows)
        g = jnp.minimum(_dot(xb, wg_bf[...]) + bg_ref[0], SWIGLU_LIMIT)
        u = jnp.clip(_dot(xb, wu_bf[...]) + bu_ref[0], -SWIGLU_LIMIT, SWIGLU_LIMIT)
        h = g * (1.0 / (1.0 + jnp.exp(-SWIGLU_ALPHA * g))) * (u + 1.0)
        _pack_rows_to_tiles(y_ref.at[rows], _dot(h.astype(BF16), wd_bf[...]) + bd_ref[0])

    whole = active & (full_ref[i] > 0)

    @pl.when(whole)
    def _():
        mlp(MOE_GROUP)

    @pl.when(active & jnp.logical_not(whole))
    def _():
        mlp(MOE_HALF)

    @pl.when(jnp.logical_not(active))
    def _():
        y_ref[pl.ds(0, MOE_HALF * PACKED_SUB), :] = jnp.zeros((MOE_HALF * PACKED_SUB, LANES), y_ref.dtype)

    @pl.when(jnp.logical_not(whole))
    def _():
        y_ref[pl.ds(MOE_HALF * PACKED_SUB, MOE_HALF * PACKED_SUB), :] = jnp.zeros((MOE_HALF * PACKED_SUB, LANES), y_ref.dtype)


def _experts(blk_expert, blk_slot, blk_next, blk_full, n_used, xs, wg, bg, wu, bu, wd, bd):
    E, D, F = wg.shape
    nblk = xs.shape[0] // (MOE_GROUP * PACKED_SUB)

    def rowmap(i, be, sl, nx, fl, nu):
        return (jnp.minimum(i, nu[0] - 1), 0)

    def bmap(i, be, sl, nx, fl, nu):
        return (be[i], 0, 0)

    hbm = pl.BlockSpec(memory_space=pl.ANY)
    return pl.pallas_call(
        _experts_kernel,
        out_shape=jax.ShapeDtypeStruct(xs.shape, xs.dtype),
        grid_spec=pltpu.PrefetchScalarGridSpec(
            num_scalar_prefetch=5,
            grid=(nblk,),
            in_specs=[pl.BlockSpec((MOE_GROUP * PACKED_SUB, LANES), rowmap),
                      hbm, pl.BlockSpec((1, 1, F), bmap),
                      hbm, pl.BlockSpec((1, 1, F), bmap),
                      hbm, pl.BlockSpec((1, 1, D), bmap)],
            out_specs=pl.BlockSpec((MOE_GROUP * PACKED_SUB, LANES), lambda i, be, sl, nx, fl, nu: (i, 0)),
            scratch_shapes=[pltpu.VMEM((2, D, F), F32), pltpu.VMEM((2, D, F), F32), pltpu.VMEM((2, F, D), F32),
                            pltpu.VMEM((D, F), BF16), pltpu.VMEM((D, F), BF16), pltpu.VMEM((F, D), BF16),
                            pltpu.SemaphoreType.DMA((2, 3))],
        ),
        compiler_params=pltpu.CompilerParams(dimension_semantics=("arbitrary",),
                                             vmem_limit_bytes=VMEM_LIMIT_BYTES),
        name="experts",
    )(blk_expert, blk_slot, blk_next, blk_full, n_used, xs, wg, bg, wu, bu, wd, bd)


def _combine_kernel(alpha, dest_ref, gates_ref, x1_ref, g_ref, b_ref, y_ref, o_ref, ybuf, sems):
    i = pl.program_id(0)
    buf = i % 2

    n_tok = pl.num_programs(0) * COMBINE_TILE

    def gather_tile(tile, into):
        base = tile * COMBINE_TILE

        def start_rows(t, _):
            for k in range(MOE_TOPK):
                d = dest_ref[k * n_tok + base + t]
                pltpu.make_async_copy(y_ref.at[pl.ds(pl.multiple_of(d * PACKED_SUB, PACKED_SUB), PACKED_SUB)],
                                      ybuf.at[into, k, pl.ds(pl.multiple_of(t * PACKED_SUB, PACKED_SUB), PACKED_SUB)],
                                      sems.at[into]).start(priority=k % DMA_PRIORITIES)
            return 0

        lax.fori_loop(0, COMBINE_TILE, start_rows, 0, unroll=ISSUE_UNROLL)

    @pl.when(i == 0)
    def _():
        gather_tile(0, 0)

    @pl.when(i + 1 < pl.num_programs(0))
    def _():
        gather_tile(i + 1, 1 - buf)

    for k in range(MOE_TOPK):
        pltpu.make_async_copy(y_ref.at[pl.ds(0, COMBINE_TILE * PACKED_SUB)], ybuf.at[buf, k], sems.at[buf]).wait()

    gates = gates_ref[...]
    ffn = gates[:, 0:1] * _unpack_tiles_to_rows(ybuf.at[buf, 0], COMBINE_TILE).astype(F32)
    for k in range(1, MOE_TOPK):
        ffn = ffn + gates[:, k:k + 1] * _unpack_tiles_to_rows(ybuf.at[buf, k], COMBINE_TILE).astype(F32)
    o_ref[...] = _layer_norm(alpha * x1_ref[...] + ffn, g_ref[...], b_ref[...])


def _combine(dest, gates, x1, g2, b2, y_rows, alpha):
    N, D = x1.shape
    row = lambda i, d: (i, 0)
    full = lambda i, d: (0, 0)
    return pl.pallas_call(
        functools.partial(_combine_kernel, alpha),
        out_shape=jax.ShapeDtypeStruct((N, D), F32),
        grid_spec=pltpu.PrefetchScalarGridSpec(
            num_scalar_prefetch=1,
            grid=(N // COMBINE_TILE,),
            in_specs=[pl.BlockSpec((COMBINE_TILE, LANES), row), pl.BlockSpec((COMBINE_TILE, D), row),
                      pl.BlockSpec(g2.shape, full), pl.BlockSpec(b2.shape, full),
                      pl.BlockSpec(memory_space=pl.ANY)],
            out_specs=pl.BlockSpec((COMBINE_TILE, D), row),
            scratch_shapes=[pltpu.VMEM((2, MOE_TOPK, COMBINE_TILE * PACKED_SUB, LANES), y_rows.dtype),
                            pltpu.SemaphoreType.DMA((2,))],
        ),
        compiler_params=pltpu.CompilerParams(dimension_semantics=("arbitrary",),
                                             vmem_limit_bytes=VMEM_LIMIT_BYTES),
        name="combine",
    )(dest, gates, x1, g2, b2, y_rows)


def _rot_partner(w, half):
    return jnp.concatenate([-w[..., half:2 * half], w[..., :half]], axis=-1)


def _layer_weights(w_in, w_q_b, w_kv_b):
    D = w_in.shape[0]
    o1 = Q_LORA
    o2 = o1 + KV_LORA
    o3 = o2 + MLA_ROPE
    w_ql, w_kvl, w_kr = w_in[:, :o1], w_in[:, o1:o2], w_in[:, o2:o3]
    w_mq, w_mk, w_mv = w_in[:, o3:o3 + _MH], w_in[:, o3 + _MH:o3 + 2 * _MH], w_in[:, o3 + 2 * _MH:]
    zpad = lambda n: jnp.zeros((D, n), w_in.dtype)
    tail = SLOT - MLA_NOPE - MLA_ROPE
    assert tail == MLA_ROPE
    kr_slot = jnp.concatenate([zpad(MLA_NOPE), w_kr, _rot_partner(w_kr, MLA_ROPE // 2)], axis=1)

    def moba_partner(w):
        w3 = w.reshape(D, MOBA_HEADS, MOBA_HD)
        part = jnp.concatenate([_rot_partner(w3[..., :MOBA_ROT], MOBA_ROT // 2),
                                jnp.zeros((D, MOBA_HEADS, MOBA_HD - MOBA_ROT), w.dtype)], axis=-1)
        return part.reshape(D, _MH)

    win = jnp.concatenate([w_ql, w_kvl, kr_slot, w_mk, moba_partner(w_mk)], axis=1).astype(BF16)
    mq_partner = _rot_partner(w_mq.reshape(D, MOBA_HEADS, MOBA_HD)[..., :MOBA_ROT], MOBA_ROT // 2)
    wmt = jnp.concatenate([w_mq, mq_partner.reshape(D, MOBA_HEADS * MOBA_ROT), w_mv], axis=1).T.astype(BF16)

    wq3 = w_q_b.reshape(Q_LORA, MLA_HEADS, MLA_NOPE + MLA_ROPE)
    wqs = _rot_partner(wq3[..., MLA_NOPE:], MLA_ROPE // 2).reshape(Q_LORA, MLA_HEADS * MLA_ROPE)
    wkv3 = w_kv_b.reshape(KV_LORA, MLA_HEADS, MLA_NOPE + MLA_V)
    wk = jnp.concatenate([wkv3[..., :MLA_NOPE], jnp.zeros((KV_LORA, MLA_HEADS, SLOT - MLA_NOPE), w_kv_b.dtype)],
                         axis=-1).reshape(KV_LORA, MLA_HEADS * SLOT).astype(BF16)
    wv = wkv3[..., MLA_NOPE:].reshape(KV_LORA, MLA_HEADS * MLA_V)
    return win, wmt, w_q_b.T.astype(BF16), wqs.T.astype(BF16), wk, wv.T.astype(BF16)


def _rope_tables(positions):
    pos = positions.astype(F32).reshape(1, -1)

    def cs(d_rot):
        inv_freq = ROPE_THETA ** (-jnp.arange(0, d_rot, 2, dtype=F32) / d_rot)
        ang = inv_freq.reshape(d_rot // 2, 1) * pos
        return jnp.cos(ang), jnp.sin(ang)

    return cs(MLA_ROPE) + cs(MOBA_ROT)


def kernel(x, positions, w_in, q_a_norm, w_q_b, kv_a_norm, w_kv_b, w_o, ln1_g, ln1_b, w_router, b_router,
           w_gate, b_gate, w_up, b_up, w_down, b_down, ln2_g, ln2_b):
    B, T, D = x.shape
    depth = w_in.shape[0]
    alpha = (2.0 * depth) ** 0.25
    N = B * T
    assert T % MOBA_BLOCK == 0 and T // MOBA_BLOCK <= SUBLANES and N % OPROJ_TILE == 0
    assert D == SUBLANES * LANES
    n_asg = N * MOE_TOPK
    n_rows = n_asg + N_EXPERTS * MOE_GROUP
    tables = _rope_tables(positions)
    h = x.reshape(N, D)
    for l in range(depth):
        win, wmt, wqt, wqst, wk, wvt = _layer_weights(w_in[l], w_q_b[l], w_kv_b[l])
        qa, ka, va, mq, mk, mv = _prep(h, win, wmt, wqt, wqst, wk, wvt, q_a_norm[l].reshape(1, -1),
                                       kv_a_norm[l].reshape(1, -1), tables, B, T)
        a = _attention(qa, ka, va, B, T, "mla_attention")
        m = _attention(mq, mk, mv, B, T, "moba_attention")
        wr_pad = jnp.concatenate([w_router[l], jnp.zeros((D, LANES - N_EXPERTS), F32)], axis=1)
        wr_hi = wr_pad.astype(BF16)
        wr = jnp.concatenate([wr_hi, (wr_pad - wr_hi.astype(F32)).astype(BF16)], axis=1)
        br = b_router[l].reshape(N_EXPERTS, 1)
        x1, x1t, route, gates, cnt = _oproj(a, m, h, w_o[l].astype(BF16), ln1_g[l].reshape(1, D), ln1_b[l].reshape(1, D),
                                       wr, br, alpha)
        er = jnp.arange(N_EXPERTS, dtype=jnp.int32)
        counts = cnt[:, 0].astype(jnp.int32)
        padded = (counts + MOE_GROUP - 1) // MOE_GROUP * MOE_GROUP
        upto = er[None, :] <= er[:, None]
        pad_end = jnp.sum(jnp.where(upto, padded[None, :], 0), axis=1)
        pad_start = pad_end - padded
        e_idx = route[:MOE_TOPK]
        group_start = jnp.sum(jnp.where(e_idx[..., None] == er, pad_start, 0), axis=-1)
        dest = (group_start + route[MOE_TOPK:2 * MOE_TOPK]).reshape(n_asg)
        n_used = (pad_end[-1:] // MOE_GROUP).astype(jnp.int32)
        nblk = n_rows // MOE_GROUP
        blk_start = jnp.arange(nblk, dtype=jnp.int32) * MOE_GROUP
        blk_expert = jnp.minimum(jnp.sum((blk_start[:, None] >= pad_end[None, :]).astype(jnp.int32), axis=1),
                                 N_EXPERTS - 1)
        of_expert = blk_expert[:, None] == er[None, :]
        pick = lambda per_expert: jnp.sum(jnp.where(of_expert, per_expert[None, :], 0), axis=1)
        nonempty = padded > 0
        group_index = jnp.sum(jnp.where(upto & nonempty[None, :], 1, 0), axis=1) - 1
        later = (er[None, :] > er[:, None]) & nonempty[None, :]
        next_expert = jnp.min(jnp.where(later, er[None, :], N_EXPERTS), axis=1)
        next_expert = jnp.where(next_expert == N_EXPERTS, -1, next_expert)
        blk_slot = pick(group_index) % 2
        blk_next = pick(next_expert)
        blk_full = (pick(pad_start + counts) > blk_start + MOE_HALF).astype(jnp.int32)
        half_start = jnp.arange(n_rows // MOE_HALF, dtype=jnp.int32) * MOE_HALF
        half_expert = jnp.minimum(jnp.sum((half_start[:, None] >= pad_end[None, :]).astype(jnp.int32), axis=1),
                                  N_EXPERTS - 1)
        half_real_end = jnp.sum(jnp.where(half_expert[:, None] == er[None, :], (pad_start + counts)[None, :], 0), axis=1)
        no_real_rows = (half_start >= pad_end[-1]) | (half_start >= half_real_end)
        fill = jnp.where(no_real_rows, FILL_FREE,
                         jnp.where(half_start + MOE_HALF > half_real_end, FILL_TAIL, 0)).astype(jnp.int32)
        xs = _dispatch(dest, fill, x1t, n_rows)
        y_rows = _experts(blk_expert, blk_slot.astype(jnp.int32), blk_next, blk_full, n_used, xs, w_gate[l], b_gate[l].reshape(N_EXPERTS, 1, -1),
                          w_up[l], b_up[l].reshape(N_EXPERTS, 1, -1), w_down[l], b_down[l].reshape(N_EXPERTS, 1, -1))
        h = _combine(dest, gates, x1, ln2_g[l].reshape(1, D), ln2_b[l].reshape(1, D), y_rows, alpha)
    return h.reshape(B, T, D)
```

```python
import functools
import math

import jax
import jax.numpy as jnp
from jax import lax
from jax.experimental import pallas as pl
from jax.experimental.pallas import tpu as pltpu

ROPE_THETA = 500000.0
MLA_HEADS = 8
MLA_NOPE = 64
MLA_ROPE = 32
MLA_V = 64
Q_LORA = 256
KV_LORA = 128
MOBA_HEADS = 8
MOBA_HD = 64
MOBA_ROT = MOBA_HD // 4
MOBA_BLOCK = 256
MOBA_TOPK = 3
N_EXPERTS = 32
MOE_TOPK = 4
SWIGLU_LIMIT = 7.0
SWIGLU_ALPHA = 1.702
RMS_EPS = 1e-6
LN_EPS = 1e-5

LANES = 128
SUBLANES = 8
VMEM_LIMIT_BYTES = 56 * 1024 * 1024

SLOT = LANES
TQ = MOBA_BLOCK
ROW_TILE = 256
OPROJ_TILE = 1024
MOE_GROUP = 512
MOE_HALF = MOE_GROUP // 2
PACKED_SUB = SUBLANES // 2
FILL_TAIL, FILL_FREE = 1, 2
DISPATCH_TILE = 1024
COMBINE_TILE = 512
ISSUE_UNROLL = 4
DMA_PRIORITIES = 2
WEIGHT_DMA_PRIORITY = 1
HEAD_LANES = 64
ONES_LANE = 64
BIAS_LANE = 64
ATTN_HEADS_PER_STEP = 8
LOG2E = math.log2(math.e)
NEG_BIG = -(2.0 ** 100)

F32 = jnp.float32
BF16 = jnp.bfloat16
NT_DIMS = (((1,), (1,)), ((), ()))
TN_DIMS = (((0,), (0,)), ((), ()))


def _dot(a, b, precision=None):
    return jnp.dot(a, b, preferred_element_type=F32, precision=precision)


def _dot_nt(a, b, precision=None):
    return lax.dot_general(a, b, NT_DIMS, preferred_element_type=F32, precision=precision)


def _pack_rows_to_tiles(ref, x):
    rows, width = x.shape
    bits = lambda v: lax.bitcast_convert_type(v.astype(BF16).astype(F32), jnp.uint32)
    words = (bits(x[:, :width // 2]) >> 16) | (bits(x[:, width // 2:]) & jnp.uint32(0xFFFF0000))
    for c in range(PACKED_SUB):
        ref[pl.ds(c, rows, stride=PACKED_SUB), :] = words[:, c * LANES:(c + 1) * LANES]


def _unpack_tiles_to_rows(ref, rows):
    words = jnp.concatenate([ref[pl.ds(c, rows, stride=PACKED_SUB), :] for c in range(PACKED_SUB)], axis=1)
    low = lax.bitcast_convert_type(words << 16, F32).astype(BF16)
    high = lax.bitcast_convert_type(words & jnp.uint32(0xFFFF0000), F32).astype(BF16)
    return jnp.concatenate([low, high], axis=1)


def _rms(x, g):
    return x * lax.rsqrt(jnp.mean(x * x, axis=-1, keepdims=True) + RMS_EPS) * g


def _layer_norm(x, g, b):
    mu = jnp.mean(x, axis=-1, keepdims=True)
    xc = x - mu
    var = jnp.mean(xc * xc, axis=-1, keepdims=True)
    return xc * lax.rsqrt(var + LN_EPS) * g + b


_C_QL = 0
_C_KVL = _C_QL + Q_LORA
_C_KR = _C_KVL + KV_LORA
_C_MK = _C_KR + SLOT
_MH = MOBA_HEADS * MOBA_HD
_C_MKS = _C_MK + _MH
_C_END = _C_MKS + _MH


def _prep_kernel(x_ref, win_ref, wmt_ref, wqt_ref, wqst_ref, wk_ref, wvt_ref, qg_ref, kvg_ref,
                 cat_ref, sat_ref, cbt_ref, sbt_ref,
                 qat_ref, ka_ref, vat_ref, mqt_ref, mk_ref, mvt_ref, kmean_scr):
    c = pl.program_id(1)

    @pl.when(c == 0)
    def _():
        kmean_scr[...] = jnp.zeros_like(kmean_scr)

    xb = x_ref[...].astype(BF16)
    both_halves = lambda t: jnp.concatenate([t, t], axis=0)
    cat = both_halves(cat_ref[...])
    sat = both_halves(sat_ref[...])
    cbt = both_halves(cbt_ref[...])
    sbt = both_halves(sbt_ref[...])
    const = lambda v, n: jnp.full((n, ROW_TILE), v, F32)
    tail = SLOT - MLA_NOPE - MLA_ROPE
    rest = MOBA_HD - MOBA_ROT
    ca = jnp.concatenate([const(1.0, MLA_NOPE), cat, const(0.0, tail)], axis=0).T
    sa = jnp.concatenate([const(0.0, MLA_NOPE), sat, const(0.0, tail)], axis=0).T
    cb = jnp.concatenate([cbt, const(1.0, rest), cbt, const(1.0, rest)], axis=0).T
    sb = jnp.concatenate([sbt, const(0.0, rest), sbt, const(0.0, rest)], axis=0).T
    lane = lax.broadcasted_iota(jnp.int32, (ROW_TILE, SLOT), 1)
    head_lanes = lane < HEAD_LANES
    ones_rows = (lax.broadcasted_iota(jnp.int32, (SLOT - HEAD_LANES, ROW_TILE), 0) == ONES_LANE - HEAD_LANES).astype(F32)

    ql = _dot(xb, win_ref[:, _C_QL:_C_KVL])
    kvl = _dot(xb, win_ref[:, _C_KVL:_C_KR])
    kr = _dot(xb, win_ref[:, _C_KR:_C_MK])
    krs = pltpu.roll(kr, SLOT - MLA_ROPE, axis=1)
    qn = _rms(ql, qg_ref[...]).astype(BF16)
    kvn = _rms(kvl, kvg_ref[...]).astype(BF16)
    q_t = _dot_nt(wqt_ref[...], qn)
    qs_t = _dot_nt(wqst_ref[...], qn)
    kn = _dot(kvn, wk_ref[...])
    v_t = _dot_nt(wvt_ref[...], kvn)
    scale_a = LOG2E / math.sqrt(MLA_NOPE + MLA_ROPE)
    kro = kr * ca + krs * sa
    qd = MLA_NOPE + MLA_ROPE
    q_pad = jnp.zeros((SLOT - qd, ROW_TILE), F32)
    for h in range(MLA_HEADS):
        sl = slice(h * SLOT, (h + 1) * SLOT)
        q_rope = q_t[h * qd + MLA_NOPE:(h + 1) * qd] * cat + qs_t[h * MLA_ROPE:(h + 1) * MLA_ROPE] * sat
        q_slot = jnp.concatenate([q_t[h * qd:h * qd + MLA_NOPE], q_rope, q_pad], axis=0)
        qat_ref[0, sl, :] = (q_slot * scale_a).astype(BF16)
        ka_ref[:, sl] = (kn[:, sl] + kro).astype(BF16)
        vat_ref[0, sl, :] = jnp.concatenate([v_t[h * MLA_V:(h + 1) * MLA_V], ones_rows], axis=0).astype(BF16)

    mk = _dot(xb, win_ref[:, _C_MK:_C_MKS])
    mks = _dot(xb, win_ref[:, _C_MKS:_C_END])
    npair = _MH // LANES
    pair = lambda a, j: a[:, j * LANES:(j + 1) * LANES]
    mk_rot = [pair(mk, j) * cb + pair(mks, j) * sb for j in range(npair)]
    mk_all = jnp.concatenate(mk_rot, axis=1)

    nrot = MOBA_HEADS * MOBA_ROT
    mq_t = _dot_nt(wmt_ref[0:_MH, :], xb)
    mqs_t = _dot_nt(wmt_ref[_MH:_MH + nrot, :], xb)
    mv_t = _dot_nt(wmt_ref[_MH + nrot:2 * _MH + nrot, :], xb)
    head = lambda a, h: a[h * MOBA_HD:(h + 1) * MOBA_HD]
    mq_rot_t = [jnp.concatenate([head(mq_t, h)[0:MOBA_ROT] * cbt + mqs_t[h * MOBA_ROT:(h + 1) * MOBA_ROT] * sbt,
                                 head(mq_t, h)[MOBA_ROT:]], axis=0) for h in range(MOBA_HEADS)]

    nrow = MOBA_HEADS * SUBLANES
    row_i = lax.broadcasted_iota(jnp.int32, (nrow, _MH), 0)
    lane_i = lax.broadcasted_iota(jnp.int32, (nrow, _MH), 1)
    kmean_c = jnp.mean(mk_all, axis=0, keepdims=True)
    put = ((row_i % SUBLANES) == c) & ((lane_i // MOBA_HD) == (row_i // SUBLANES))
    table = kmean_scr[...]
    gate_t = _dot(table, jnp.concatenate(mq_rot_t, axis=0), precision=lax.Precision.HIGHEST)
    kmean_scr[...] = jnp.where(put, jnp.broadcast_to(kmean_c, (nrow, _MH)), table)

    n_idx = lax.broadcasted_iota(jnp.int32, (SUBLANES, ROW_TILE), 0)
    valid = n_idx < c
    scale_b = LOG2E / math.sqrt(MOBA_HD)
    zero_rows = jnp.zeros((SLOT - HEAD_LANES - SUBLANES, ROW_TILE), F32)
    block_onehot = (lane == BIAS_LANE + c).astype(F32)
    for h in range(MOBA_HEADS):
        g = jnp.where(valid, gate_t[h * SUBLANES:(h + 1) * SUBLANES, :], -jnp.inf)
        rank = jnp.zeros((SUBLANES, ROW_TILE), jnp.int32)
        for k in range(1, SUBLANES):
            other = pltpu.roll(g, k, axis=0)
            other_n = pltpu.roll(n_idx, k, axis=0)
            beats = (other > g) | ((other == g) & (other_n < n_idx))
            rank = rank + beats.astype(jnp.int32)
        keep = (valid & (rank < MOBA_TOPK)) | (n_idx == c)
        bias = jnp.where(keep, 0.0, NEG_BIG)
        sl = slice(h * SLOT, (h + 1) * SLOT)
        mqt_ref[0, sl, :] = jnp.concatenate([mq_rot_t[h] * scale_b, bias, zero_rows], axis=0).astype(BF16)
        mvt_ref[0, sl, :] = jnp.concatenate([head(mv_t, h), ones_rows], axis=0).astype(BF16)
        j, hh = divmod(h, 2)
        k_h = mk_rot[j] if hh == 0 else pltpu.roll(mk_rot[j], HEAD_LANES, axis=1)
        mk_ref[:, sl] = jnp.where(head_lanes, k_h, block_onehot).astype(BF16)


def _prep(x2, win, wmt, wqt, wqst, wk, wvt, qg, kvg, tables, B, T):
    N, D = x2.shape
    nt = T // ROW_TILE
    row = lambda b, c: (b * nt + c, 0)
    col = lambda b, c: (0, b * nt + c)
    full = lambda b, c: (0, 0)
    cat, sat, cbt, sbt = tables
    width = MLA_HEADS * SLOT

    def rows(w):
        return pl.BlockSpec((ROW_TILE, w), row)

    def cols(a):
        return pl.BlockSpec((a.shape[0], ROW_TILE), col)

    def whole(a):
        return pl.BlockSpec(a.shape, full)

    rowmajor = jax.ShapeDtypeStruct((N, width), BF16)
    transposed = jax.ShapeDtypeStruct((B * nt, width, ROW_TILE), BF16)
    t_spec = pl.BlockSpec((1, width, ROW_TILE), lambda b, c: (b * nt + c, 0, 0))
    return pl.pallas_call(
        _prep_kernel,
        out_shape=(transposed, rowmajor, transposed, transposed, rowmajor, transposed),
        grid=(B, nt),
        in_specs=[rows(D), whole(win), whole(wmt), whole(wqt), whole(wqst), whole(wk), whole(wvt), whole(qg), whole(kvg),
                  cols(cat), cols(sat), cols(cbt), cols(sbt)],
        out_specs=(t_spec, rows(width), t_spec, t_spec, rows(width), t_spec),
        scratch_shapes=[pltpu.VMEM((MOBA_HEADS * SUBLANES, _MH), F32)],
        compiler_params=pltpu.CompilerParams(dimension_semantics=("arbitrary", "arbitrary"),
                                             vmem_limit_bytes=VMEM_LIMIT_BYTES),
        name="prep",
    )(x2, win, wmt, wqt, wqst, wk, wvt, qg, kvg, cat, sat, cbt, sbt)


def _attn_kernel(qt_ref, k_ref, vt_ref, o_ref, s_scr, acc_scr):
    i = pl.program_id(2)
    nh = ATTN_HEADS_PER_STEP
    slot = lambda h: slice(h * SLOT, (h + 1) * SLOT)
    nblk = i + 1
    nquad = nblk // 4
    pair_end = 4 * nquad + 2 * ((nblk - 4 * nquad) // 2)
    q_pos = i * TQ + lax.broadcasted_iota(jnp.int32, (1, TQ), 1)

    def fold(s):
        out = s[0:SUBLANES]
        for t in range(1, s.shape[0] // SUBLANES):
            out = jnp.maximum(out, s[t * SUBLANES:(t + 1) * SUBLANES])
        return out

    def score_blocks(j, n, mrun):
        j0 = pl.multiple_of(j * TQ, TQ)
        visible = (j0 + lax.broadcasted_iota(jnp.int32, (n * TQ, TQ), 0)) <= q_pos
        out = []
        for h in range(nh):
            s = _dot(k_ref[pl.ds(j0, n * TQ), slot(h)], qt_ref[0, slot(h), :])
            s = jnp.where(visible, s, -jnp.inf)
            s_scr[h, pl.ds(j, n)] = s.reshape(n, TQ, TQ)
            out.append(jnp.maximum(mrun[h], fold(s)))
        return tuple(out)

    mrun = tuple(jnp.full((SUBLANES, TQ), -jnp.inf, F32) for _ in range(nh))
    mrun = lax.fori_loop(0, nquad, lambda jq, m: score_blocks(4 * jq, 4, m), mrun)
    mrun = lax.fori_loop(2 * nquad, pair_end // 2, lambda jj, m: score_blocks(2 * jj, 2, m), mrun)
    mrun = lax.fori_loop(pair_end, nblk, lambda j, m: score_blocks(j, 1, m), mrun)
    ms = [jnp.max(m, axis=0, keepdims=True) for m in mrun]

    def accumulate(j, n, carry):
        for h in range(nh):
            p = jnp.exp2(s_scr[h, pl.ds(j, n)].reshape(n * TQ, TQ) - ms[h]).astype(BF16)
            vt = jnp.concatenate([vt_ref[j + t, slot(h), :] for t in range(n)], axis=1)
            acc_scr[h] += _dot(vt, p)
        return carry

    acc_scr[...] = jnp.zeros_like(acc_scr)
    lax.fori_loop(0, nquad, lambda jq, c: accumulate(4 * jq, 4, c), 0)
    lax.fori_loop(2 * nquad, pair_end // 2, lambda jj, c: accumulate(2 * jj, 2, c), 0)
    lax.fori_loop(pair_end, nblk, lambda j, c: accumulate(j, 1, c), 0)
    outs = [acc_scr[h, 0:HEAD_LANES] / acc_scr[h, ONES_LANE:ONES_LANE + 1] for h in range(nh)]
    for h in range(nh):
        o_ref[0, h * HEAD_LANES:(h + 1) * HEAD_LANES, :] = outs[h].astype(o_ref.dtype)


def _attention(qt, k, vt, B, T, name):
    N = k.shape[0]
    nq = T // TQ
    nh = ATTN_HEADS_PER_STEP
    heads = k.shape[1] // SLOT
    return pl.pallas_call(
        _attn_kernel,
        out_shape=jax.ShapeDtypeStruct((B * nq, heads * HEAD_LANES, TQ), BF16),
        grid=(B, heads // nh, nq),
        in_specs=[pl.BlockSpec((1, nh * SLOT, TQ), lambda b, g, i: (b * nq + i, g, 0)),
                  pl.BlockSpec((T, nh * SLOT), lambda b, g, i: (b, g)),
                  pl.BlockSpec((nq, nh * SLOT, TQ), lambda b, g, i: (b, g, 0))],
        out_specs=pl.BlockSpec((1, nh * HEAD_LANES, TQ), lambda b, g, i: (b * nq + i, g, 0)),
        scratch_shapes=[pltpu.VMEM((nh, nq, TQ, TQ), F32), pltpu.VMEM((nh, SLOT, TQ), F32)],
        compiler_params=pltpu.CompilerParams(dimension_semantics=("arbitrary", "arbitrary", "arbitrary"),
                                             vmem_limit_bytes=VMEM_LIMIT_BYTES),
        name=name,
    )(qt, k, vt)


def _oproj_kernel(alpha, a_ref, m_ref, x_ref, wo_ref, g_ref, b_ref, wr_ref, br_ref,
                  x1_ref, x1t_ref, route_ref, gates_ref, cnt_ref, carry_scr):
    i = pl.program_id(0)

    @pl.when(i == 0)
    def _():
        carry_scr[...] = jnp.zeros_like(carry_scr)

    parts = []
    for t in range(a_ref.shape[0]):
        lhs_t = jnp.concatenate([a_ref[t], m_ref[t]], axis=0)
        parts.append(lax.dot_general(lhs_t, wo_ref[...], TN_DIMS, preferred_element_type=F32))
    mix = jnp.concatenate(parts, axis=0)
    x1 = _layer_norm(alpha * x_ref[...] + mix, g_ref[...], b_ref[...])
    x1_ref[...] = x1
    _pack_rows_to_tiles(x1t_ref, x1)

    ne = br_ref.shape[0]
    rows = x1.shape[0]
    x_hi = x1.astype(BF16)
    x_lo = (x1 - x_hi.astype(F32)).astype(BF16)
    both = _dot(x_hi, wr_ref[...])
    logits_rm = both[:, :LANES] + both[:, LANES:] + _dot(x_lo, wr_ref[:, :LANES])
    logits = logits_rm.T[0:ne] + br_ref[...]
    expert = lax.broadcasted_iota(jnp.int32, logits.shape, 0)
    expert_f = expert.astype(F32)
    vals, idxs = [], []
    work = logits
    for _ in range(MOE_TOPK):
        mx = jnp.max(work, axis=0, keepdims=True)
        ix = jnp.min(jnp.where(work == mx, expert_f, float(ne)), axis=0, keepdims=True).astype(jnp.int32)
        vals.append(mx)
        idxs.append(ix)
        work = jnp.where(expert == ix, -jnp.inf, work)
    exps = [jnp.exp(v - vals[0]) for v in vals]
    den = exps[0]
    for e in exps[1:]:
        den = den + e

    onehot = jnp.zeros(logits.shape, F32)
    for ix in idxs:
        onehot = onehot + (expert == ix).astype(F32)
    r = lax.broadcasted_iota(jnp.int32, (rows, rows), 0)
    cidx = lax.broadcasted_iota(jnp.int32, (rows, rows), 1)
    earlier = (r < cidx).astype(BF16)
    carry = carry_scr[:, 0:1]
    before = _dot(onehot.astype(BF16), earlier) + carry
    row8 = lax.broadcasted_iota(jnp.int32, (SUBLANES, rows), 0)
    route = jnp.zeros((SUBLANES, rows), jnp.int32)
    gates = jnp.zeros((SUBLANES, rows), F32)
    for k in range(MOE_TOPK):
        rank = jnp.sum(jnp.where(expert == idxs[k], before, 0.0), axis=0, keepdims=True).astype(jnp.int32)
        route = jnp.where(row8 == k, idxs[k], route)
        route = jnp.where(row8 == MOE_TOPK + k, rank, route)
        gates = jnp.where(row8 == k, exps[k] / den, gates)
    route_ref[...] = route
    gates_ref[...] = jnp.concatenate([gates, jnp.zeros((LANES - SUBLANES, rows), F32)], axis=0).T
    new_carry = carry + jnp.sum(onehot, axis=1, keepdims=True)
    carry_scr[...] = jnp.broadcast_to(new_carry, carry_scr.shape)
    cnt_ref[...] = jnp.broadcast_to(new_carry, cnt_ref.shape)


def _oproj(a, m, x2, wo, g1, b1, wr, br, alpha):
    N, D = x2.shape
    nt = N // OPROJ_TILE
    row = lambda i: (i, 0)
    full = lambda i: (0, 0)
    return pl.pallas_call(
        functools.partial(_oproj_kernel, alpha),
        out_shape=(jax.ShapeDtypeStruct((N, D), F32),
                   jax.ShapeDtypeStruct((N * PACKED_SUB, LANES), jnp.uint32),
                   jax.ShapeDtypeStruct((SUBLANES, N), jnp.int32),
                   jax.ShapeDtypeStruct((N, LANES), F32),
                   jax.ShapeDtypeStruct((br.shape[0], LANES), F32)),
        grid=(nt,),
        in_specs=[pl.BlockSpec((OPROJ_TILE // TQ,) + a.shape[1:], lambda i: (i, 0, 0)),
                  pl.BlockSpec((OPROJ_TILE // TQ,) + m.shape[1:], lambda i: (i, 0, 0)),
                  pl.BlockSpec((OPROJ_TILE, D), row), pl.BlockSpec(wo.shape, full),
                  pl.BlockSpec(g1.shape, full), pl.BlockSpec(b1.shape, full),
                  pl.BlockSpec(wr.shape, full), pl.BlockSpec(br.shape, full)],
        out_specs=(pl.BlockSpec((OPROJ_TILE, D), row), pl.BlockSpec((OPROJ_TILE * PACKED_SUB, LANES), row),
                   pl.BlockSpec((SUBLANES, OPROJ_TILE), lambda i: (0, i)),
                   pl.BlockSpec((OPROJ_TILE, LANES), row), pl.BlockSpec((br.shape[0], LANES), full)),
        scratch_shapes=[pltpu.VMEM((br.shape[0], LANES), F32)],
        compiler_params=pltpu.CompilerParams(dimension_semantics=("arbitrary",),
                                             vmem_limit_bytes=VMEM_LIMIT_BYTES),
        name="oproj_router",
    )(a, m, x2, wo, g1, b1, wr, br)


def _dispatch_kernel(dest_ref, fill_ref, x1_ref, xs_ref, zero_scr, sem, fill_sem):
    i = pl.program_id(0)
    fill_sub = MOE_HALF * PACKED_SUB
    nfill = xs_ref.shape[0] // fill_sub

    def fill_copy(blk, on):
        return pltpu.make_async_copy(zero_scr, xs_ref.at[pl.ds(pl.multiple_of(blk * fill_sub, fill_sub), fill_sub)], on)

    def for_blocks(kind, action):
        def body(blk, _):
            @pl.when(fill_ref[blk] == kind)
            def _():
                action(blk)
            return 0

        lax.fori_loop(0, nfill, body, 0)

    @pl.when(i == 0)
    def _():
        zero_scr[...] = jnp.zeros_like(zero_scr)
        for_blocks(FILL_TAIL, lambda blk: fill_copy(blk, sem).start())
        for_blocks(FILL_FREE, lambda blk: fill_copy(blk, fill_sem).start())
        for_blocks(FILL_TAIL, lambda blk: fill_copy(blk, sem).wait())

    base = i * DISPATCH_TILE
    n_tok = pl.num_programs(0) * DISPATCH_TILE

    def row_copy(t, k):
        d = dest_ref[k * n_tok + base + t]
        return pltpu.make_async_copy(x1_ref.at[pl.ds(pl.multiple_of(t * PACKED_SUB, PACKED_SUB), PACKED_SUB)],
                                     xs_ref.at[pl.ds(pl.multiple_of(d * PACKED_SUB, PACKED_SUB), PACKED_SUB)], sem)

    def start_rows(t, _):
        for k in range(MOE_TOPK):
            row_copy(t, k).start(priority=k % DMA_PRIORITIES)
        return 0

    lax.fori_loop(0, DISPATCH_TILE, start_rows, 0, unroll=ISSUE_UNROLL)
    for k in range(MOE_TOPK):
        pltpu.make_async_copy(x1_ref, xs_ref.at[pl.ds(0, DISPATCH_TILE * PACKED_SUB)], sem).wait()

    @pl.when(i == pl.num_programs(0) - 1)
    def _():
        for_blocks(FILL_FREE, lambda blk: fill_copy(blk, fill_sem).wait())


def _dispatch(dest, fill, x1t, n_rows):
    return pl.pallas_call(
        _dispatch_kernel,
        out_shape=jax.ShapeDtypeStruct((n_rows * PACKED_SUB, LANES), x1t.dtype),
        grid_spec=pltpu.PrefetchScalarGridSpec(
            num_scalar_prefetch=2,
            grid=(x1t.shape[0] // (DISPATCH_TILE * PACKED_SUB),),
            in_specs=[pl.BlockSpec((DISPATCH_TILE * PACKED_SUB, LANES), lambda i, d, t: (i, 0))],
            out_specs=pl.BlockSpec(memory_space=pl.ANY),
            scratch_shapes=[pltpu.VMEM((MOE_HALF * PACKED_SUB, LANES), x1t.dtype), pltpu.SemaphoreType.DMA(()),
                            pltpu.SemaphoreType.DMA(())],
        ),
        compiler_params=pltpu.CompilerParams(dimension_semantics=("arbitrary",),
                                             vmem_limit_bytes=VMEM_LIMIT_BYTES),
        name="dispatch",
    )(dest, fill, x1t)


def _experts_kernel(be_ref, slot_ref, nxt_ref, full_ref, nused_ref, x_ref, wg_hbm, bg_ref, wu_hbm, bu_ref, wd_hbm, bd_ref,
                    y_ref, wg_st, wu_st, wd_st, wg_bf, wu_bf, wd_bf, sems):
    i = pl.program_id(0)
    prev = be_ref[jnp.maximum(i - 1, 0)]
    changed = (i == 0) | (be_ref[i] != prev)
    active = i < nused_ref[0]
    slot = slot_ref[i]

    def weight_copies(expert, s):
        return (pltpu.make_async_copy(wg_hbm.at[expert], wg_st.at[s], sems.at[s, 0]),
                pltpu.make_async_copy(wu_hbm.at[expert], wu_st.at[s], sems.at[s, 1]),
                pltpu.make_async_copy(wd_hbm.at[expert], wd_st.at[s], sems.at[s, 2]))

    @pl.when(i == 0)
    def _():
        for cp in weight_copies(be_ref[0], slot):
            cp.start()

    @pl.when(active & changed)
    def _():
        for cp in weight_copies(be_ref[i], slot):
            cp.wait()

        @pl.when(nxt_ref[i] >= 0)
        def _():
            for cp in weight_copies(nxt_ref[i], 1 - slot):
                cp.start(priority=WEIGHT_DMA_PRIORITY)

        wg_bf[...] = wg_st[slot].astype(BF16)
        wu_bf[...] = wu_st[slot].astype(BF16)
        wd_bf[...] = wd_st[slot].astype(BF16)

    def mlp(nrows):
        rows = pl.ds(0, nrows * PACKED_SUB)
        xb = _unpack_tiles_to_rows(x_ref.at[rows], nrows)
        g = jnp.minimum(_dot(xb, wg_bf[...]) + bg_ref[0], SWIGLU_LIMIT)
        u = jnp.clip(_dot(xb, wu_bf[...]) + bu_ref[0], -SWIGLU_LIMIT, SWIGLU_LIMIT)
        h = g * (1.0 / (1.0 + jnp.exp(-SWIGLU_ALPHA * g))) * (u + 1.0)
        _pack_rows_to_tiles(y_ref.at[rows], _dot(h.astype(BF16), wd_bf[...]) + bd_ref[0])

    whole = active & (full_ref[i] > 0)

    @pl.when(whole)
    def _():
        mlp(MOE_GROUP)

    @pl.when(active & jnp.logical_not(whole))
    def _():
        mlp(MOE_HALF)

    @pl.when(jnp.logical_not(active))
    def _():
        y_ref[pl.ds(0, MOE_HALF * PACKED_SUB), :] = jnp.zeros((MOE_HALF * PACKED_SUB, LANES), y_ref.dtype)

    @pl.when(jnp.logical_not(whole))
    def _():
        y_ref[pl.ds(MOE_HALF * PACKED_SUB, MOE_HALF * PACKED_SUB), :] = jnp.zeros((MOE_HALF * PACKED_SUB, LANES), y_ref.dtype)


def _experts(blk_expert, blk_slot, blk_next, blk_full, n_used, xs, wg, bg, wu, bu, wd, bd):
    E, D, F = wg.shape
    nblk = xs.shape[0] // (MOE_GROUP * PACKED_SUB)

    def rowmap(i, be, sl, nx, fl, nu):
        return (jnp.minimum(i, nu[0] - 1), 0)

    def bmap(i, be, sl, nx, fl, nu):
        return (be[i], 0, 0)

    hbm = pl.BlockSpec(memory_space=pl.ANY)
    return pl.pallas_call(
        _experts_kernel,
        out_shape=jax.ShapeDtypeStruct(xs.shape, xs.dtype),
        grid_spec=pltpu.PrefetchScalarGridSpec(
            num_scalar_prefetch=5,
            grid=(nblk,),
            in_specs=[pl.BlockSpec((MOE_GROUP * PACKED_SUB, LANES), rowmap),
                      hbm, pl.BlockSpec((1, 1, F), bmap),
                      hbm, pl.BlockSpec((1, 1, F), bmap),
                      hbm, pl.BlockSpec((1, 1, D), bmap)],
            out_specs=pl.BlockSpec((MOE_GROUP * PACKED_SUB, LANES), lambda i, be, sl, nx, fl, nu: (i, 0)),
            scratch_shapes=[pltpu.VMEM((2, D, F), F32), pltpu.VMEM((2, D, F), F32), pltpu.VMEM((2, F, D), F32),
                            pltpu.VMEM((D, F), BF16), pltpu.VMEM((D, F), BF16), pltpu.VMEM((F, D), BF16),
                            pltpu.SemaphoreType.DMA((2, 3))],
        ),
        compiler_params=pltpu.CompilerParams(dimension_semantics=("arbitrary",),
                                             vmem_limit_bytes=VMEM_LIMIT_BYTES),
        name="experts",
    )(blk_expert, blk_slot, blk_next, blk_full, n_used, xs, wg, bg, wu, bu, wd, bd)


def _combine_kernel(alpha, dest_ref, gates_ref, x1_ref, g_ref, b_ref, y_ref, o_ref, ybuf, sems):
    i = pl.program_id(0)
    buf = i % 2

    n_tok = pl.num_programs(0) * COMBINE_TILE

    def gather_tile(tile, into):
        base = tile * COMBINE_TILE

        def start_rows(t, _):
            for k in range(MOE_TOPK):
                d = dest_ref[k * n_tok + base + t]
                pltpu.make_async_copy(y_ref.at[pl.ds(pl.multiple_of(d * PACKED_SUB, PACKED_SUB), PACKED_SUB)],
                                      ybuf.at[into, k, pl.ds(pl.multiple_of(t * PACKED_SUB, PACKED_SUB), PACKED_SUB)],
                                      sems.at[into]).start(priority=k % DMA_PRIORITIES)
            return 0

        lax.fori_loop(0, COMBINE_TILE, start_rows, 0, unroll=ISSUE_UNROLL)

    @pl.when(i == 0)
    def _():
        gather_tile(0, 0)

    @pl.when(i + 1 < pl.num_programs(0))
    def _():
        gather_tile(i + 1, 1 - buf)

    for k in range(MOE_TOPK):
        pltpu.make_async_copy(y_ref.at[pl.ds(0, COMBINE_TILE * PACKED_SUB)], ybuf.at[buf, k], sems.at[buf]).wait()

    gates = gates_ref[...]
    ffn = gates[:, 0:1] * _unpack_tiles_to_rows(ybuf.at[buf, 0], COMBINE_TILE).astype(F32)
    for k in range(1, MOE_TOPK):
        ffn = ffn + gates[:, k:k + 1] * _unpack_tiles_to_rows(ybuf.at[buf, k], COMBINE_TILE).astype(F32)
    o_ref[...] = _layer_norm(alpha * x1_ref[...] + ffn, g_ref[...], b_ref[...])


def _combine(dest, gates, x1, g2, b2, y_rows, alpha):
    N, D = x1.shape
    row = lambda i, d: (i, 0)
    full = lambda i, d: (0, 0)
    return pl.pallas_call(
        functools.partial(_combine_kernel, alpha),
        out_shape=jax.ShapeDtypeStruct((N, D), F32),
        grid_spec=pltpu.PrefetchScalarGridSpec(
            num_scalar_prefetch=1,
            grid=(N // COMBINE_TILE,),
            in_specs=[pl.BlockSpec((COMBINE_TILE, LANES), row), pl.BlockSpec((COMBINE_TILE, D), row),
                      pl.BlockSpec(g2.shape, full), pl.BlockSpec(b2.shape, full),
                      pl.BlockSpec(memory_space=pl.ANY)],
            out_specs=pl.BlockSpec((COMBINE_TILE, D), row),
            scratch_shapes=[pltpu.VMEM((2, MOE_TOPK, COMBINE_TILE * PACKED_SUB, LANES), y_rows.dtype),
                            pltpu.SemaphoreType.DMA((2,))],
        ),
        compiler_params=pltpu.CompilerParams(dimension_semantics=("arbitrary",),
                                             vmem_limit_bytes=VMEM_LIMIT_BYTES),
        name="combine",
    )(dest, gates, x1, g2, b2, y_rows)


def _rot_partner(w, half):
    return jnp.concatenate([-w[..., half:2 * half], w[..., :half]], axis=-1)


def _layer_weights(w_in, w_q_b, w_kv_b):
    D = w_in.shape[0]
    o1 = Q_LORA
    o2 = o1 + KV_LORA
    o3 = o2 + MLA_ROPE
    w_ql, w_kvl, w_kr = w_in[:, :o1], w_in[:, o1:o2], w_in[:, o2:o3]
    w_mq, w_mk, w_mv = w_in[:, o3:o3 + _MH], w_in[:, o3 + _MH:o3 + 2 * _MH], w_in[:, o3 + 2 * _MH:]
    zpad = lambda n: jnp.zeros((D, n), w_in.dtype)
    tail = SLOT - MLA_NOPE - MLA_ROPE
    assert tail == MLA_ROPE
    kr_slot = jnp.concatenate([zpad(MLA_NOPE), w_kr, _rot_partner(w_kr, MLA_ROPE // 2)], axis=1)

    def moba_partner(w):
        w3 = w.reshape(D, MOBA_HEADS, MOBA_HD)
        part = jnp.concatenate([_rot_partner(w3[..., :MOBA_ROT], MOBA_ROT // 2),
                                jnp.zeros((D, MOBA_HEADS, MOBA_HD - MOBA_ROT), w.dtype)], axis=-1)
        return part.reshape(D, _MH)

    win = jnp.concatenate([w_ql, w_kvl, kr_slot, w_mk, moba_partner(w_mk)], axis=1).astype(BF16)
    mq_partner = _rot_partner(w_mq.reshape(D, MOBA_HEADS, MOBA_HD)[..., :MOBA_ROT], MOBA_ROT // 2)
    wmt = jnp.concatenate([w_mq, mq_partner.reshape(D, MOBA_HEADS * MOBA_ROT), w_mv], axis=1).T.astype(BF16)

    wq3 = w_q_b.reshape(Q_LORA, MLA_HEADS, MLA_NOPE + MLA_ROPE)
    wqs = _rot_partner(wq3[..., MLA_NOPE:], MLA_ROPE // 2).reshape(Q_LORA, MLA_HEADS * MLA_ROPE)
    wkv3 = w_kv_b.reshape(KV_LORA, MLA_HEADS, MLA_NOPE + MLA_V)
    wk = jnp.concatenate([wkv3[..., :MLA_NOPE], jnp.zeros((KV_LORA, MLA_HEADS, SLOT - MLA_NOPE), w_kv_b.dtype)],
                         axis=-1).reshape(KV_LORA, MLA_HEADS * SLOT).astype(BF16)
    wv = wkv3[..., MLA_NOPE:].reshape(KV_LORA, MLA_HEADS * MLA_V)
    return win, wmt, w_q_b.T.astype(BF16), wqs.T.astype(BF16), wk, wv.T.astype(BF16)


def _rope_tables(positions):
    pos = positions.astype(F32).reshape(1, -1)

    def cs(d_rot):
        inv_freq = ROPE_THETA ** (-jnp.arange(0, d_rot, 2, dtype=F32) / d_rot)
        ang = inv_freq.reshape(d_rot // 2, 1) * pos
        return jnp.cos(ang), jnp.sin(ang)

    return cs(MLA_ROPE) + cs(MOBA_ROT)


def kernel(x, positions, w_in, q_a_norm, w_q_b, kv_a_norm, w_kv_b, w_o, ln1_g, ln1_b, w_router, b_router,
           w_gate, b_gate, w_up, b_up, w_down, b_down, ln2_g, ln2_b):
    B, T, D = x.shape
    depth = w_in.shape[0]
    alpha = (2.0 * depth) ** 0.25
    N = B * T
    assert T % MOBA_BLOCK == 0 and T // MOBA_BLOCK <= SUBLANES and N % OPROJ_TILE == 0
    assert D == SUBLANES * LANES
    n_asg = N * MOE_TOPK
    n_rows = n_asg + N_EXPERTS * MOE_GROUP
    tables = _rope_tables(positions)
    h = x.reshape(N, D)
    for l in range(depth):
        win, wmt, wqt, wqst, wk, wvt = _layer_weights(w_in[l], w_q_b[l], w_kv_b[l])
        qa, ka, va, mq, mk, mv = _prep(h, win, wmt, wqt, wqst, wk, wvt, q_a_norm[l].reshape(1, -1),
                                       kv_a_norm[l].reshape(1, -1), tables, B, T)
        a = _attention(qa, ka, va, B, T, "mla_attention")
        m = _attention(mq, mk, mv, B, T, "moba_attention")
        wr_pad = jnp.concatenate([w_router[l], jnp.zeros((D, LANES - N_EXPERTS), F32)], axis=1)
        wr_hi = wr_pad.astype(BF16)
        wr = jnp.concatenate([wr_hi, (wr_pad - wr_hi.astype(F32)).astype(BF16)], axis=1)
        br = b_router[l].reshape(N_EXPERTS, 1)
        x1, x1t, route, gates, cnt = _oproj(a, m, h, w_o[l].astype(BF16), ln1_g[l].reshape(1, D), ln1_b[l].reshape(1, D),
                                       wr, br, alpha)
        er = jnp.arange(N_EXPERTS, dtype=jnp.int32)
        counts = cnt[:, 0].astype(jnp.int32)
        padded = (counts + MOE_GROUP - 1) // MOE_GROUP * MOE_GROUP
        upto = er[None, :] <= er[:, None]
        pad_end = jnp.sum(jnp.where(upto, padded[None, :], 0), axis=1)
        pad_start = pad_end - padded
        e_idx = route[:MOE_TOPK]
        group_start = jnp.sum(jnp.where(e_idx[..., None] == er, pad_start, 0), axis=-1)
        dest = (group_start + route[MOE_TOPK:2 * MOE_TOPK]).reshape(n_asg)
        n_used = (pad_end[-1:] // MOE_GROUP).astype(jnp.int32)
        nblk = n_rows // MOE_GROUP
        blk_start = jnp.arange(nblk, dtype=jnp.int32) * MOE_GROUP
        blk_expert = jnp.minimum(jnp.sum((blk_start[:, None] >= pad_end[None, :]).astype(jnp.int32), axis=1),
                                 N_EXPERTS - 1)
        of_expert = blk_expert[:, None] == er[None, :]
        pick = lambda per_expert: jnp.sum(jnp.where(of_expert, per_expert[None, :], 0), axis=1)
        nonempty = padded > 0
        group_index = jnp.sum(jnp.where(upto & nonempty[None, :], 1, 0), axis=1) - 1
        later = (er[None, :] > er[:, None]) & nonempty[None, :]
        next_expert = jnp.min(jnp.where(later, er[None, :], N_EXPERTS), axis=1)
        next_expert = jnp.where(next_expert == N_EXPERTS, -1, next_expert)
        blk_slot = pick(group_index) % 2
        blk_next = pick(next_expert)
        blk_full = (pick(pad_start + counts) > blk_start + MOE_HALF).astype(jnp.int32)
        half_start = jnp.arange(n_rows // MOE_HALF, dtype=jnp.int32) * MOE_HALF
        half_expert = jnp.minimum(jnp.sum((half_start[:, None] >= pad_end[None, :]).astype(jnp.int32), axis=1),
                                  N_EXPERTS - 1)
        half_real_end = jnp.sum(jnp.where(half_expert[:, None] == er[None, :], (pad_start + counts)[None, :], 0), axis=1)
        no_real_rows = (half_start >= pad_end[-1]) | (half_start >= half_real_end)
        fill = jnp.where(no_real_rows, FILL_FREE,
                         jnp.where(half_start + MOE_HALF > half_real_end, FILL_TAIL, 0)).astype(jnp.int32)
        xs = _dispatch(dest, fill, x1t, n_rows)
        y_rows = _experts(blk_expert, blk_slot.astype(jnp.int32), blk_next, blk_full, n_used, xs, w_gate[l], b_gate[l].reshape(N_EXPERTS, 1, -1),
                          w_up[l], b_up[l].reshape(N_EXPERTS, 1, -1), w_down[l], b_down[l].reshape(N_EXPERTS, 1, -1))
        h = _combine(dest, gates, x1, ln2_g[l].reshape(1, D), ln2_b[l].reshape(1, D), y_rows, alpha)
    return h.reshape(B, T, D)
```

```python
import functools
import math

import jax
import jax.numpy as jnp
from jax import lax
from jax.experimental import pallas as pl
from jax.experimental.pallas import tpu as pltpu

ROPE_THETA = 500000.0
MLA_HEADS = 8
MLA_NOPE = 64
MLA_ROPE = 32
MLA_V = 64
Q_LORA = 256
KV_LORA = 128
MOBA_HEADS = 8
MOBA_HD = 64
MOBA_ROT = MOBA_HD // 4
MOBA_BLOCK = 256
MOBA_TOPK = 3
N_EXPERTS = 32
MOE_TOPK = 4
SWIGLU_LIMIT = 7.0
SWIGLU_ALPHA = 1.702
RMS_EPS = 1e-6
LN_EPS = 1e-5

LANES = 128
SUBLANES = 8
VMEM_LIMIT_BYTES = 56 * 1024 * 1024

SLOT = LANES
TQ = MOBA_BLOCK
ROW_TILE = 256
OPROJ_TILE = 1024
MOE_GROUP = 512
MOE_HALF = MOE_GROUP // 2
PACKED_SUB = SUBLANES // 2
FILL_TAIL, FILL_FREE = 1, 2
DISPATCH_TILE = 1024
COMBINE_TILE = 512
ISSUE_UNROLL = 4
DMA_PRIORITIES = 2
WEIGHT_DMA_PRIORITY = 1
HEAD_LANES = 64
ONES_LANE = 64
BIAS_LANE = 64
ATTN_HEADS_PER_STEP = 8
LOG2E = math.log2(math.e)
NEG_BIG = -(2.0 ** 100)

F32 = jnp.float32
BF16 = jnp.bfloat16
NT_DIMS = (((1,), (1,)), ((), ()))
TN_DIMS = (((0,), (0,)), ((), ()))


def _dot(a, b, precision=None):
    return jnp.dot(a, b, preferred_element_type=F32, precision=precision)


def _dot_nt(a, b, precision=None):
    return lax.dot_general(a, b, NT_DIMS, preferred_element_type=F32, precision=precision)


def _pack_rows_to_tiles(ref, x):
    rows, width = x.shape
    bits = lambda v: lax.bitcast_convert_type(v.astype(BF16).astype(F32), jnp.uint32)
    words = (bits(x[:, :width // 2]) >> 16) | (bits(x[:, width // 2:]) & jnp.uint32(0xFFFF0000))
    for c in range(PACKED_SUB):
        ref[pl.ds(c, rows, stride=PACKED_SUB), :] = words[:, c * LANES:(c + 1) * LANES]


def _unpack_tiles_to_rows(ref, rows):
    words = jnp.concatenate([ref[pl.ds(c, rows, stride=PACKED_SUB), :] for c in range(PACKED_SUB)], axis=1)
    low = lax.bitcast_convert_type(words << 16, F32).astype(BF16)
    high = lax.bitcast_convert_type(words & jnp.uint32(0xFFFF0000), F32).astype(BF16)
    return jnp.concatenate([low, high], axis=1)


def _rms(x, g):
    return x * lax.rsqrt(jnp.mean(x * x, axis=-1, keepdims=True) + RMS_EPS) * g


def _layer_norm(x, g, b):
    mu = jnp.mean(x, axis=-1, keepdims=True)
    xc = x - mu
    var = jnp.mean(xc * xc, axis=-1, keepdims=True)
    return xc * lax.rsqrt(var + LN_EPS) * g + b


_C_QL = 0
_C_KVL = _C_QL + Q_LORA
_C_KR = _C_KVL + KV_LORA
_C_MK = _C_KR + SLOT
_MH = MOBA_HEADS * MOBA_HD
_C_END = _C_MK + _MH


def _prep_kernel(x_ref, win_ref, wmt_ref, wqt_ref, wqst_ref, wk_ref, wvt_ref, qg_ref, kvg_ref,
                 cat_ref, sat_ref, cbt_ref, sbt_ref,
                 qat_ref, ka_ref, vat_ref, mqt_ref, mk_ref, mvt_ref, kmean_scr):
    c = pl.program_id(1)

    @pl.when(c == 0)
    def _():
        kmean_scr[...] = jnp.zeros_like(kmean_scr)

    xb = x_ref[...].astype(BF16)
    both_halves = lambda t: jnp.concatenate([t, t], axis=0)
    cat = both_halves(cat_ref[...])
    sat = both_halves(sat_ref[...])
    cbt = both_halves(cbt_ref[...])
    sbt = both_halves(sbt_ref[...])
    const = lambda v, n: jnp.full((n, ROW_TILE), v, F32)
    tail = SLOT - MLA_NOPE - MLA_ROPE
    rest = MOBA_HD - MOBA_ROT
    ca = jnp.concatenate([const(1.0, MLA_NOPE), cat, const(0.0, tail)], axis=0).T
    sa = jnp.concatenate([const(0.0, MLA_NOPE), sat, const(0.0, tail)], axis=0).T
    cb = jnp.concatenate([cbt, const(1.0, rest), cbt, const(1.0, rest)], axis=0).T
    sbk = jnp.concatenate([-sbt_ref[...], sbt_ref[...]], axis=0)
    sb = jnp.concatenate([sbk, const(0.0, rest), sbk, const(0.0, rest)], axis=0).T
    lane = lax.broadcasted_iota(jnp.int32, (ROW_TILE, SLOT), 1)
    head_lanes = lane < HEAD_LANES
    first_half = (lane % MOBA_HD) < MOBA_ROT // 2
    ones_rows = (lax.broadcasted_iota(jnp.int32, (SLOT - HEAD_LANES, ROW_TILE), 0) == ONES_LANE - HEAD_LANES).astype(F32)

    ql = _dot(xb, win_ref[:, _C_QL:_C_KVL])
    kvl = _dot(xb, win_ref[:, _C_KVL:_C_KR])
    kr = _dot(xb, win_ref[:, _C_KR:_C_MK])
    krs = pltpu.roll(kr, SLOT - MLA_ROPE, axis=1)
    qn = _rms(ql, qg_ref[...]).astype(BF16)
    kvn = _rms(kvl, kvg_ref[...]).astype(BF16)
    q_t = _dot_nt(wqt_ref[...], qn)
    qs_t = _dot_nt(wqst_ref[...], qn)
    kn = _dot(kvn, wk_ref[...])
    v_t = _dot_nt(wvt_ref[...], kvn)
    scale_a = LOG2E / math.sqrt(MLA_NOPE + MLA_ROPE)
    kro = kr * ca + krs * sa
    qd = MLA_NOPE + MLA_ROPE
    q_pad = jnp.zeros((SLOT - qd, ROW_TILE), F32)
    for h in range(MLA_HEADS):
        sl = slice(h * SLOT, (h + 1) * SLOT)
        q_rope = q_t[h * qd + MLA_NOPE:(h + 1) * qd] * cat + qs_t[h * MLA_ROPE:(h + 1) * MLA_ROPE] * sat
        q_slot = jnp.concatenate([q_t[h * qd:h * qd + MLA_NOPE], q_rope, q_pad], axis=0)
        qat_ref[0, sl, :] = (q_slot * scale_a).astype(BF16)
        ka_ref[:, sl] = (kn[:, sl] + kro).astype(BF16)
        vat_ref[0, sl, :] = jnp.concatenate([v_t[h * MLA_V:(h + 1) * MLA_V], ones_rows], axis=0).astype(BF16)

    mk = _dot(xb, win_ref[:, _C_MK:_C_END])
    npair = _MH // LANES
    pair = lambda a, j: a[:, j * LANES:(j + 1) * LANES]

    def rotated(a):
        x2 = pltpu.roll(a, LANES - MOBA_ROT // 2, axis=1)
        x1 = pltpu.roll(a, MOBA_ROT // 2, axis=1)
        return a * cb + jnp.where(first_half, x2, x1) * sb

    mk_rot = [rotated(pair(mk, j)) for j in range(npair)]
    mk_all = jnp.concatenate(mk_rot, axis=1)

    nrot = MOBA_HEADS * MOBA_ROT
    mq_t = _dot_nt(wmt_ref[0:_MH, :], xb)
    mqs_t = _dot_nt(wmt_ref[_MH:_MH + nrot, :], xb)
    mv_t = _dot_nt(wmt_ref[_MH + nrot:2 * _MH + nrot, :], xb)
    head = lambda a, h: a[h * MOBA_HD:(h + 1) * MOBA_HD]
    mq_rot_t = [jnp.concatenate([head(mq_t, h)[0:MOBA_ROT] * cbt + mqs_t[h * MOBA_ROT:(h + 1) * MOBA_ROT] * sbt,
                                 head(mq_t, h)[MOBA_ROT:]], axis=0) for h in range(MOBA_HEADS)]

    nrow = MOBA_HEADS * SUBLANES
    row_i = lax.broadcasted_iota(jnp.int32, (nrow, _MH), 0)
    lane_i = lax.broadcasted_iota(jnp.int32, (nrow, _MH), 1)
    kmean_c = jnp.mean(mk_all, axis=0, keepdims=True)
    put = ((row_i % SUBLANES) == c) & ((lane_i // MOBA_HD) == (row_i // SUBLANES))
    table = kmean_scr[...]
    gate_t = _dot(table, jnp.concatenate(mq_rot_t, axis=0), precision=lax.Precision.HIGHEST)
    kmean_scr[...] = jnp.where(put, jnp.broadcast_to(kmean_c, (nrow, _MH)), table)

    n_idx = lax.broadcasted_iota(jnp.int32, (SUBLANES, ROW_TILE), 0)
    valid = n_idx < c
    scale_b = LOG2E / math.sqrt(MOBA_HD)
    zero_rows = jnp.zeros((SLOT - HEAD_LANES - SUBLANES, ROW_TILE), F32)
    block_onehot = (lane == BIAS_LANE + c).astype(F32)
    for h in range(MOBA_HEADS):
        g = jnp.where(valid, gate_t[h * SUBLANES:(h + 1) * SUBLANES, :], -jnp.inf)
        rank = jnp.zeros((SUBLANES, ROW_TILE), jnp.int32)
        for k in range(1, SUBLANES):
            other = pltpu.roll(g, k, axis=0)
            other_n = pltpu.roll(n_idx, k, axis=0)
            beats = (other > g) | ((other == g) & (other_n < n_idx))
            rank = rank + beats.astype(jnp.int32)
        keep = (valid & (rank < MOBA_TOPK)) | (n_idx == c)
        bias = jnp.where(keep, 0.0, NEG_BIG)
        sl = slice(h * SLOT, (h + 1) * SLOT)
        mqt_ref[0, sl, :] = jnp.concatenate([mq_rot_t[h] * scale_b, bias, zero_rows], axis=0).astype(BF16)
        mvt_ref[0, sl, :] = jnp.concatenate([head(mv_t, h), ones_rows], axis=0).astype(BF16)
        j, hh = divmod(h, 2)
        k_h = mk_rot[j] if hh == 0 else pltpu.roll(mk_rot[j], HEAD_LANES, axis=1)
        mk_ref[:, sl] = jnp.where(head_lanes, k_h, block_onehot).astype(BF16)


def _prep(x2, win, wmt, wqt, wqst, wk, wvt, qg, kvg, tables, B, T):
    N, D = x2.shape
    nt = T // ROW_TILE
    row = lambda b, c: (b * nt + c, 0)
    col = lambda b, c: (0, b * nt + c)
    full = lambda b, c: (0, 0)
    cat, sat, cbt, sbt = tables
    width = MLA_HEADS * SLOT

    def rows(w):
        return pl.BlockSpec((ROW_TILE, w), row)

    def cols(a):
        return pl.BlockSpec((a.shape[0], ROW_TILE), col)

    def whole(a):
        return pl.BlockSpec(a.shape, full)

    rowmajor = jax.ShapeDtypeStruct((N, width), BF16)
    transposed = jax.ShapeDtypeStruct((B * nt, width, ROW_TILE), BF16)
    t_spec = pl.BlockSpec((1, width, ROW_TILE), lambda b, c: (b * nt + c, 0, 0))
    return pl.pallas_call(
        _prep_kernel,
        out_shape=(transposed, rowmajor, transposed, transposed, rowmajor, transposed),
        grid=(B, nt),
        in_specs=[rows(D), whole(win), whole(wmt), whole(wqt), whole(wqst), whole(wk), whole(wvt), whole(qg), whole(kvg),
                  cols(cat), cols(sat), cols(cbt), cols(sbt)],
        out_specs=(t_spec, rows(width), t_spec, t_spec, rows(width), t_spec),
        scratch_shapes=[pltpu.VMEM((MOBA_HEADS * SUBLANES, _MH), F32)],
        compiler_params=pltpu.CompilerParams(dimension_semantics=("arbitrary", "arbitrary"),
                                             vmem_limit_bytes=VMEM_LIMIT_BYTES),
        name="prep",
    )(x2, win, wmt, wqt, wqst, wk, wvt, qg, kvg, cat, sat, cbt, sbt)


def _attn_kernel(qt_ref, k_ref, vt_ref, o_ref, s_scr, acc_scr):
    i = pl.program_id(2)
    nh = ATTN_HEADS_PER_STEP
    slot = lambda h: slice(h * SLOT, (h + 1) * SLOT)
    nblk = i + 1
    nquad = nblk // 4
    pair_end = 4 * nquad + 2 * ((nblk - 4 * nquad) // 2)
    q_pos = i * TQ + lax.broadcasted_iota(jnp.int32, (1, TQ), 1)

    def fold(s):
        out = s[0:SUBLANES]
        for t in range(1, s.shape[0] // SUBLANES):
            out = jnp.maximum(out, s[t * SUBLANES:(t + 1) * SUBLANES])
        return out

    def score_blocks(j, n, mrun):
        j0 = pl.multiple_of(j * TQ, TQ)
        visible = (j0 + lax.broadcasted_iota(jnp.int32, (n * TQ, TQ), 0)) <= q_pos
        out = []
        for h in range(nh):
            s = _dot(k_ref[pl.ds(j0, n * TQ), slot(h)], qt_ref[0, slot(h), :])
            s = jnp.where(visible, s, -jnp.inf)
            s_scr[h, pl.ds(j, n)] = s.reshape(n, TQ, TQ)
            out.append(jnp.maximum(mrun[h], fold(s)))
        return tuple(out)

    mrun = tuple(jnp.full((SUBLANES, TQ), -jnp.inf, F32) for _ in range(nh))
    mrun = lax.fori_loop(0, nquad, lambda jq, m: score_blocks(4 * jq, 4, m), mrun)
    mrun = lax.fori_loop(2 * nquad, pair_end // 2, lambda jj, m: score_blocks(2 * jj, 2, m), mrun)
    mrun = lax.fori_loop(pair_end, nblk, lambda j, m: score_blocks(j, 1, m), mrun)
    ms = [jnp.max(m, axis=0, keepdims=True) for m in mrun]

    def accumulate(j, n, carry):
        for h in range(nh):
            p = jnp.exp2(s_scr[h, pl.ds(j, n)].reshape(n * TQ, TQ) - ms[h]).astype(BF16)
            vt = jnp.concatenate([vt_ref[j + t, slot(h), :] for t in range(n)], axis=1)
            acc_scr[h] += _dot(vt, p)
        return carry

    acc_scr[...] = jnp.zeros_like(acc_scr)
    lax.fori_loop(0, nquad, lambda jq, c: accumulate(4 * jq, 4, c), 0)
    lax.fori_loop(2 * nquad, pair_end // 2, lambda jj, c: accumulate(2 * jj, 2, c), 0)
    lax.fori_loop(pair_end, nblk, lambda j, c: accumulate(j, 1, c), 0)
    outs = [acc_scr[h, 0:HEAD_LANES] / acc_scr[h, ONES_LANE:ONES_LANE + 1] for h in range(nh)]
    for h in range(nh):
        o_ref[0, h * HEAD_LANES:(h + 1) * HEAD_LANES, :] = outs[h].astype(o_ref.dtype)


def _attention(qt, k, vt, B, T, name):
    N = k.shape[0]
    nq = T // TQ
    nh = ATTN_HEADS_PER_STEP
    heads = k.shape[1] // SLOT
    return pl.pallas_call(
        _attn_kernel,
        out_shape=jax.ShapeDtypeStruct((B * nq, heads * HEAD_LANES, TQ), BF16),
        grid=(B, heads // nh, nq),
        in_specs=[pl.BlockSpec((1, nh * SLOT, TQ), lambda b, g, i: (b * nq + i, g, 0)),
                  pl.BlockSpec((T, nh * SLOT), lambda b, g, i: (b, g)),
                  pl.BlockSpec((nq, nh * SLOT, TQ), lambda b, g, i: (b, g, 0))],
        out_specs=pl.BlockSpec((1, nh * HEAD_LANES, TQ), lambda b, g, i: (b * nq + i, g, 0)),
        scratch_shapes=[pltpu.VMEM((nh, nq, TQ, TQ), F32), pltpu.VMEM((nh, SLOT, TQ), F32)],
        compiler_params=pltpu.CompilerParams(dimension_semantics=("arbitrary", "arbitrary", "arbitrary"),
                                             vmem_limit_bytes=VMEM_LIMIT_BYTES),
        name=name,
    )(qt, k, vt)


def _oproj_kernel(alpha, a_ref, m_ref, x_ref, wo_ref, g_ref, b_ref, wr_ref, br_ref,
                  x1_ref, x1t_ref, route_ref, gates_ref, cnt_ref, carry_scr):
    i = pl.program_id(0)

    @pl.when(i == 0)
    def _():
        carry_scr[...] = jnp.zeros_like(carry_scr)

    parts = []
    for t in range(a_ref.shape[0]):
        lhs_t = jnp.concatenate([a_ref[t], m_ref[t]], axis=0)
        parts.append(lax.dot_general(lhs_t, wo_ref[...], TN_DIMS, preferred_element_type=F32))
    mix = jnp.concatenate(parts, axis=0)
    x1 = _layer_norm(alpha * x_ref[...] + mix, g_ref[...], b_ref[...])
    x1_ref[...] = x1
    _pack_rows_to_tiles(x1t_ref, x1)

    ne = br_ref.shape[0]
    rows = x1.shape[0]
    x_hi = x1.astype(BF16)
    x_lo = (x1 - x_hi.astype(F32)).astype(BF16)
    both = _dot(x_hi, wr_ref[...])
    logits_rm = both[:, :LANES] + both[:, LANES:] + _dot(x_lo, wr_ref[:, :LANES])
    logits = logits_rm.T[0:ne] + br_ref[...]
    expert = lax.broadcasted_iota(jnp.int32, logits.shape, 0)
    expert_f = expert.astype(F32)
    vals, idxs = [], []
    work = logits
    for _ in range(MOE_TOPK):
        mx = jnp.max(work, axis=0, keepdims=True)
        ix = jnp.min(jnp.where(work == mx, expert_f, float(ne)), axis=0, keepdims=True).astype(jnp.int32)
        vals.append(mx)
        idxs.append(ix)
        work = jnp.where(expert == ix, -jnp.inf, work)
    exps = [jnp.exp(v - vals[0]) for v in vals]
    den = exps[0]
    for e in exps[1:]:
        den = den + e

    onehot = jnp.zeros(logits.shape, F32)
    for ix in idxs:
        onehot = onehot + (expert == ix).astype(F32)
    r = lax.broadcasted_iota(jnp.int32, (rows, rows), 0)
    cidx = lax.broadcasted_iota(jnp.int32, (rows, rows), 1)
    earlier = (r < cidx).astype(BF16)
    carry = carry_scr[:, 0:1]
    before = _dot(onehot.astype(BF16), earlier) + carry
    row8 = lax.broadcasted_iota(jnp.int32, (SUBLANES, rows), 0)
    route = jnp.zeros((SUBLANES, rows), jnp.int32)
    gates = jnp.zeros((SUBLANES, rows), F32)
    for k in range(MOE_TOPK):
        rank = jnp.sum(jnp.where(expert == idxs[k], before, 0.0), axis=0, keepdims=True).astype(jnp.int32)
        route = jnp.where(row8 == k, idxs[k], route)
        route = jnp.where(row8 == MOE_TOPK + k, rank, route)
        gates = jnp.where(row8 == k, exps[k] / den, gates)
    route_ref[...] = route
    gates_ref[...] = jnp.concatenate([gates, jnp.zeros((LANES - SUBLANES, rows), F32)], axis=0).T
    new_carry = carry + jnp.sum(onehot, axis=1, keepdims=True)
    carry_scr[...] = jnp.broadcast_to(new_carry, carry_scr.shape)
    cnt_ref[...] = jnp.broadcast_to(new_carry, cnt_ref.shape)


def _oproj(a, m, x2, wo, g1, b1, wr, br, alpha):
    N, D = x2.shape
    nt = N // OPROJ_TILE
    row = lambda i: (i, 0)
    full = lambda i: (0, 0)
    return pl.pallas_call(
        functools.partial(_oproj_kernel, alpha),
        out_shape=(jax.ShapeDtypeStruct((N, D), F32),
                   jax.ShapeDtypeStruct((N * PACKED_SUB, LANES), jnp.uint32),
                   jax.ShapeDtypeStruct((SUBLANES, N), jnp.int32),
                   jax.ShapeDtypeStruct((N, LANES), F32),
                   jax.ShapeDtypeStruct((br.shape[0], LANES), F32)),
        grid=(nt,),
        in_specs=[pl.BlockSpec((OPROJ_TILE // TQ,) + a.shape[1:], lambda i: (i, 0, 0)),
                  pl.BlockSpec((OPROJ_TILE // TQ,) + m.shape[1:], lambda i: (i, 0, 0)),
                  pl.BlockSpec((OPROJ_TILE, D), row), pl.BlockSpec(wo.shape, full),
                  pl.BlockSpec(g1.shape, full), pl.BlockSpec(b1.shape, full),
                  pl.BlockSpec(wr.shape, full), pl.BlockSpec(br.shape, full)],
        out_specs=(pl.BlockSpec((OPROJ_TILE, D), row), pl.BlockSpec((OPROJ_TILE * PACKED_SUB, LANES), row),
                   pl.BlockSpec((SUBLANES, OPROJ_TILE), lambda i: (0, i)),
                   pl.BlockSpec((OPROJ_TILE, LANES), row), pl.BlockSpec((br.shape[0], LANES), full)),
        scratch_shapes=[pltpu.VMEM((br.shape[0], LANES), F32)],
        compiler_params=pltpu.CompilerParams(dimension_semantics=("arbitrary",),
                                             vmem_limit_bytes=VMEM_LIMIT_BYTES),
        name="oproj_router",
    )(a, m, x2, wo, g1, b1, wr, br)


def _dispatch_kernel(dest_ref, fill_ref, x1_ref, xs_ref, zero_scr, sem, fill_sem):
    i = pl.program_id(0)
    fill_sub = MOE_HALF * PACKED_SUB
    nfill = xs_ref.shape[0] // fill_sub

    def fill_copy(blk, on):
        return pltpu.make_async_copy(zero_scr, xs_ref.at[pl.ds(pl.multiple_of(blk * fill_sub, fill_sub), fill_sub)], on)

    def for_blocks(kind, action):
        def body(blk, _):
            @pl.when(fill_ref[blk] == kind)
            def _():
                action(blk)
            return 0

        lax.fori_loop(0, nfill, body, 0)

    @pl.when(i == 0)
    def _():
        zero_scr[...] = jnp.zeros_like(zero_scr)
        for_blocks(FILL_TAIL, lambda blk: fill_copy(blk, sem).start())
        for_blocks(FILL_FREE, lambda blk: fill_copy(blk, fill_sem).start())
        for_blocks(FILL_TAIL, lambda blk: fill_copy(blk, sem).wait())

    base = i * DISPATCH_TILE
    n_tok = pl.num_programs(0) * DISPATCH_TILE

    def row_copy(t, k):
        d = dest_ref[k * n_tok + base + t]
        return pltpu.make_async_copy(x1_ref.at[pl.ds(pl.multiple_of(t * PACKED_SUB, PACKED_SUB), PACKED_SUB)],
                                     xs_ref.at[pl.ds(pl.multiple_of(d * PACKED_SUB, PACKED_SUB), PACKED_SUB)], sem)

    def start_rows(t, _):
        for k in range(MOE_TOPK):
            row_copy(t, k).start(priority=k % DMA_PRIORITIES)
        return 0

    lax.fori_loop(0, DISPATCH_TILE, start_rows, 0, unroll=ISSUE_UNROLL)
    for k in range(MOE_TOPK):
        pltpu.make_async_copy(x1_ref, xs_ref.at[pl.ds(0, DISPATCH_TILE * PACKED_SUB)], sem).wait()

    @pl.when(i == pl.num_programs(0) - 1)
    def _():
        for_blocks(FILL_FREE, lambda blk: fill_copy(blk, fill_sem).wait())


def _dispatch(dest, fill, x1t, n_rows):
    return pl.pallas_call(
        _dispatch_kernel,
        out_shape=jax.ShapeDtypeStruct((n_rows * PACKED_SUB, LANES), x1t.dtype),
        grid_spec=pltpu.PrefetchScalarGridSpec(
            num_scalar_prefetch=2,
            grid=(x1t.shape[0] // (DISPATCH_TILE * PACKED_SUB),),
            in_specs=[pl.BlockSpec((DISPATCH_TILE * PACKED_SUB, LANES), lambda i, d, t: (i, 0))],
            out_specs=pl.BlockSpec(memory_space=pl.ANY),
            scratch_shapes=[pltpu.VMEM((MOE_HALF * PACKED_SUB, LANES), x1t.dtype), pltpu.SemaphoreType.DMA(()),
                            pltpu.SemaphoreType.DMA(())],
        ),
        compiler_params=pltpu.CompilerParams(dimension_semantics=("arbitrary",),
                                             vmem_limit_bytes=VMEM_LIMIT_BYTES),
        name="dispatch",
    )(dest, fill, x1t)


def _experts_kernel(be_ref, slot_ref, nxt_ref, full_ref, nused_ref, x_ref, wg_hbm, bg_ref, wu_hbm, bu_ref, wd_hbm, bd_ref,
                    y_ref, wg_st, wu_st, wd_st, wg_bf, wu_bf, wd_bf, sems):
    i = pl.program_id(0)
    prev = be_ref[jnp.maximum(i - 1, 0)]
    changed = (i == 0) | (be_ref[i] != prev)
    active = i < nused_ref[0]
    slot = slot_ref[i]

    def weight_copies(expert, s):
        return (pltpu.make_async_copy(wg_hbm.at[expert], wg_st.at[s], sems.at[s, 0]),
                pltpu.make_async_copy(wu_hbm.at[expert], wu_st.at[s], sems.at[s, 1]),
                pltpu.make_async_copy(wd_hbm.at[expert], wd_st.at[s], sems.at[s, 2]))

    @pl.when(i == 0)
    def _():
        for cp in weight_copies(be_ref[0], slot):
            cp.start()

    @pl.when(active & changed)
    def _():
        for cp in weight_copies(be_ref[i], slot):
            cp.wait()

        @pl.when(nxt_ref[i] >= 0)
        def _():
            for cp in weight_copies(nxt_ref[i], 1 - slot):
                cp.start(priority=WEIGHT_DMA_PRIORITY)

        wg_bf[...] = wg_st[slot].astype(BF16)
        wu_bf[...] = wu_st[slot].astype(BF16)
        wd_bf[...] = wd_st[slot].astype(BF16)

    def mlp(nrows):
        rows = pl.ds(0, nrows * PACKED_SUB)
        xb = _unpack_tiles_to_rows(x_ref.at[rows], nrows)
        g = jnp.minimum(_dot(xb, wg_bf[...]) + bg_ref[0], SWIGLU_LIMIT)
        u = jnp.clip(_dot(xb, wu_bf[...]) + bu_ref[0], -SWIGLU_LIMIT, SWIGLU_LIMIT)
        h = g * (1.0 / (1.0 + jnp.exp(-SWIGLU_ALPHA * g))) * (u + 1.0)
        _pack_rows_to_tiles(y_ref.at[rows], _dot(h.astype(BF16), wd_bf[...]) + bd_ref[0])

    whole = active & (full_ref[i] > 0)

    @pl.when(whole)
    def _():
        mlp(MOE_GROUP)

    @pl.when(active & jnp.logical_not(whole))
    def _():
        mlp(MOE_HALF)

    @pl.when(jnp.logical_not(active))
    def _():
        y_ref[pl.ds(0, MOE_HALF * PACKED_SUB), :] = jnp.zeros((MOE_HALF * PACKED_SUB, LANES), y_ref.dtype)

    @pl.when(jnp.logical_not(whole))
    def _():
        y_ref[pl.ds(MOE_HALF * PACKED_SUB, MOE_HALF * PACKED_SUB), :] = jnp.zeros((MOE_HALF * PACKED_SUB, LANES), y_ref.dtype)


def _experts(blk_expert, blk_slot, blk_next, blk_full, n_used, xs, wg, bg, wu, bu, wd, bd):
    E, D, F = wg.shape
    nblk = xs.shape[0] // (MOE_GROUP * PACKED_SUB)

    def rowmap(i, be, sl, nx, fl, nu):
        return (jnp.minimum(i, nu[0] - 1), 0)

    def bmap(i, be, sl, nx, fl, nu):
        return (be[i], 0, 0)

    hbm = pl.BlockSpec(memory_space=pl.ANY)
    return pl.pallas_call(
        _experts_kernel,
        out_shape=jax.ShapeDtypeStruct(xs.shape, xs.dtype),
        grid_spec=pltpu.PrefetchScalarGridSpec(
            num_scalar_prefetch=5,
            grid=(nblk,),
            in_specs=[pl.BlockSpec((MOE_GROUP * PACKED_SUB, LANES), rowmap),
                      hbm, pl.BlockSpec((1, 1, F), bmap),
                      hbm, pl.BlockSpec((1, 1, F), bmap),
                      hbm, pl.BlockSpec((1, 1, D), bmap)],
            out_specs=pl.BlockSpec((MOE_GROUP * PACKED_SUB, LANES), lambda i, be, sl, nx, fl, nu: (i, 0)),
            scratch_shapes=[pltpu.VMEM((2, D, F), F32), pltpu.VMEM((2, D, F), F32), pltpu.VMEM((2, F, D), F32),
                            pltpu.VMEM((D, F), BF16), pltpu.VMEM((D, F), BF16), pltpu.VMEM((F, D), BF16),
                            pltpu.SemaphoreType.DMA((2, 3))],
        ),
        compiler_params=pltpu.CompilerParams(dimension_semantics=("arbitrary",),
                                             vmem_limit_bytes=VMEM_LIMIT_BYTES),
        name="experts",
    )(blk_expert, blk_slot, blk_next, blk_full, n_used, xs, wg, bg, wu, bu, wd, bd)


def _combine_kernel(alpha, dest_ref, gates_ref, x1_ref, g_ref, b_ref, y_ref, o_ref, ybuf, sems):
    i = pl.program_id(0)
    buf = i % 2

    n_tok = pl.num_programs(0) * COMBINE_TILE

    def gather_tile(tile, into):
        base = tile * COMBINE_TILE

        def start_rows(t, _):
            for k in range(MOE_TOPK):
                d = dest_ref[k * n_tok + base + t]
                pltpu.make_async_copy(y_ref.at[pl.ds(pl.multiple_of(d * PACKED_SUB, PACKED_SUB), PACKED_SUB)],
                                      ybuf.at[into, k, pl.ds(pl.multiple_of(t * PACKED_SUB, PACKED_SUB), PACKED_SUB)],
                                      sems.at[into]).start(priority=k % DMA_PRIORITIES)
            return 0

        lax.fori_loop(0, COMBINE_TILE, start_rows, 0, unroll=ISSUE_UNROLL)

    @pl.when(i == 0)
    def _():
        gather_tile(0, 0)

    @pl.when(i + 1 < pl.num_programs(0))
    def _():
        gather_tile(i + 1, 1 - buf)

    for k in range(MOE_TOPK):
        pltpu.make_async_copy(y_ref.at[pl.ds(0, COMBINE_TILE * PACKED_SUB)], ybuf.at[buf, k], sems.at[buf]).wait()

    gates = gates_ref[...]
    ffn = gates[:, 0:1] * _unpack_tiles_to_rows(ybuf.at[buf, 0], COMBINE_TILE).astype(F32)
    for k in range(1, MOE_TOPK):
        ffn = ffn + gates[:, k:k + 1] * _unpack_tiles_to_rows(ybuf.at[buf, k], COMBINE_TILE).astype(F32)
    o_ref[...] = _layer_norm(alpha * x1_ref[...] + ffn, g_ref[...], b_ref[...])


def _combine(dest, gates, x1, g2, b2, y_rows, alpha):
    N, D = x1.shape
    row = lambda i, d: (i, 0)
    full = lambda i, d: (0, 0)
    return pl.pallas_call(
        functools.partial(_combine_kernel, alpha),
        out_shape=jax.ShapeDtypeStruct((N, D), F32),
        grid_spec=pltpu.PrefetchScalarGridSpec(
            num_scalar_prefetch=1,
            grid=(N // COMBINE_TILE,),
            in_specs=[pl.BlockSpec((COMBINE_TILE, LANES), row), pl.BlockSpec((COMBINE_TILE, D), row),
                      pl.BlockSpec(g2.shape, full), pl.BlockSpec(b2.shape, full),
                      pl.BlockSpec(memory_space=pl.ANY)],
            out_specs=pl.BlockSpec((COMBINE_TILE, D), row),
            scratch_shapes=[pltpu.VMEM((2, MOE_TOPK, COMBINE_TILE * PACKED_SUB, LANES), y_rows.dtype),
                            pltpu.SemaphoreType.DMA((2,))],
        ),
        compiler_params=pltpu.CompilerParams(dimension_semantics=("arbitrary",),
                                             vmem_limit_bytes=VMEM_LIMIT_BYTES),
        name="combine",
    )(dest, gates, x1, g2, b2, y_rows)


def _rot_partner(w, half):
    return jnp.concatenate([-w[..., half:2 * half], w[..., :half]], axis=-1)


def _layer_weights(w_in, w_q_b, w_kv_b):
    D = w_in.shape[0]
    o1 = Q_LORA
    o2 = o1 + KV_LORA
    o3 = o2 + MLA_ROPE
    w_ql, w_kvl, w_kr = w_in[:, :o1], w_in[:, o1:o2], w_in[:, o2:o3]
    w_mq, w_mk, w_mv = w_in[:, o3:o3 + _MH], w_in[:, o3 + _MH:o3 + 2 * _MH], w_in[:, o3 + 2 * _MH:]
    zpad = lambda n: jnp.zeros((D, n), w_in.dtype)
    tail = SLOT - MLA_NOPE - MLA_ROPE
    assert tail == MLA_ROPE
    kr_slot = jnp.concatenate([zpad(MLA_NOPE), w_kr, _rot_partner(w_kr, MLA_ROPE // 2)], axis=1)

    win = jnp.concatenate([w_ql, w_kvl, kr_slot, w_mk], axis=1).astype(BF16)
    mq_partner = _rot_partner(w_mq.reshape(D, MOBA_HEADS, MOBA_HD)[..., :MOBA_ROT], MOBA_ROT // 2)
    wmt = jnp.concatenate([w_mq, mq_partner.reshape(D, MOBA_HEADS * MOBA_ROT), w_mv], axis=1).T.astype(BF16)

    wq3 = w_q_b.reshape(Q_LORA, MLA_HEADS, MLA_NOPE + MLA_ROPE)
    wqs = _rot_partner(wq3[..., MLA_NOPE:], MLA_ROPE // 2).reshape(Q_LORA, MLA_HEADS * MLA_ROPE)
    wkv3 = w_kv_b.reshape(KV_LORA, MLA_HEADS, MLA_NOPE + MLA_V)
    wk = jnp.concatenate([wkv3[..., :MLA_NOPE], jnp.zeros((KV_LORA, MLA_HEADS, SLOT - MLA_NOPE), w_kv_b.dtype)],
                         axis=-1).reshape(KV_LORA, MLA_HEADS * SLOT).astype(BF16)
    wv = wkv3[..., MLA_NOPE:].reshape(KV_LORA, MLA_HEADS * MLA_V)
    return win, wmt, w_q_b.T.astype(BF16), wqs.T.astype(BF16), wk, wv.T.astype(BF16)


def _rope_tables(positions):
    pos = positions.astype(F32).reshape(1, -1)

    def cs(d_rot):
        inv_freq = ROPE_THETA ** (-jnp.arange(0, d_rot, 2, dtype=F32) / d_rot)
        ang = inv_freq.reshape(d_rot // 2, 1) * pos
        return jnp.cos(ang), jnp.sin(ang)

    return cs(MLA_ROPE) + cs(MOBA_ROT)


def kernel(x, positions, w_in, q_a_norm, w_q_b, kv_a_norm, w_kv_b, w_o, ln1_g, ln1_b, w_router, b_router,
           w_gate, b_gate, w_up, b_up, w_down, b_down, ln2_g, ln2_b):
    B, T, D = x.shape
    depth = w_in.shape[0]
    alpha = (2.0 * depth) ** 0.25
    N = B * T
    assert T % MOBA_BLOCK == 0 and T // MOBA_BLOCK <= SUBLANES and N % OPROJ_TILE == 0
    assert D == SUBLANES * LANES
    n_asg = N * MOE_TOPK
    n_rows = n_asg + N_EXPERTS * MOE_GROUP
    tables = _rope_tables(positions)
    h = x.reshape(N, D)
    for l in range(depth):
        win, wmt, wqt, wqst, wk, wvt = _layer_weights(w_in[l], w_q_b[l], w_kv_b[l])
        qa, ka, va, mq, mk, mv = _prep(h, win, wmt, wqt, wqst, wk, wvt, q_a_norm[l].reshape(1, -1),
                                       kv_a_norm[l].reshape(1, -1), tables, B, T)
        a = _attention(qa, ka, va, B, T, "mla_attention")
        m = _attention(mq, mk, mv, B, T, "moba_attention")
        wr_pad = jnp.concatenate([w_router[l], jnp.zeros((D, LANES - N_EXPERTS), F32)], axis=1)
        wr_hi = wr_pad.astype(BF16)
        wr = jnp.concatenate([wr_hi, (wr_pad - wr_hi.astype(F32)).astype(BF16)], axis=1)
        br = b_router[l].reshape(N_EXPERTS, 1)
        x1, x1t, route, gates, cnt = _oproj(a, m, h, w_o[l].astype(BF16), ln1_g[l].reshape(1, D), ln1_b[l].reshape(1, D),
                                       wr, br, alpha)
        er = jnp.arange(N_EXPERTS, dtype=jnp.int32)
        counts = cnt[:, 0].astype(jnp.int32)
        padded = (counts + MOE_GROUP - 1) // MOE_GROUP * MOE_GROUP
        upto = er[None, :] <= er[:, None]
        pad_end = jnp.sum(jnp.where(upto, padded[None, :], 0), axis=1)
        pad_start = pad_end - padded
        e_idx = route[:MOE_TOPK]
        group_start = jnp.sum(jnp.where(e_idx[..., None] == er, pad_start, 0), axis=-1)
        dest = (group_start + route[MOE_TOPK:2 * MOE_TOPK]).reshape(n_asg)
        n_used = (pad_end[-1:] // MOE_GROUP).astype(jnp.int32)
        nblk = n_rows // MOE_GROUP
        blk_start = jnp.arange(nblk, dtype=jnp.int32) * MOE_GROUP
        blk_expert = jnp.minimum(jnp.sum((blk_start[:, None] >= pad_end[None, :]).astype(jnp.int32), axis=1),
                                 N_EXPERTS - 1)
        of_expert = blk_expert[:, None] == er[None, :]
        pick = lambda per_expert: jnp.sum(jnp.where(of_expert, per_expert[None, :], 0), axis=1)
        nonempty = padded > 0
        group_index = jnp.sum(jnp.where(upto & nonempty[None, :], 1, 0), axis=1) - 1
        later = (er[None, :] > er[:, None]) & nonempty[None, :]
        next_expert = jnp.min(jnp.where(later, er[None, :], N_EXPERTS), axis=1)
        next_expert = jnp.where(next_expert == N_EXPERTS, -1, next_expert)
        blk_slot = pick(group_index) % 2
        blk_next = pick(next_expert)
        blk_full = (pick(pad_start + counts) > blk_start + MOE_HALF).astype(jnp.int32)
        half_start = jnp.arange(n_rows // MOE_HALF, dtype=jnp.int32) * MOE_HALF
        half_expert = jnp.minimum(jnp.sum((half_start[:, None] >= pad_end[None, :]).astype(jnp.int32), axis=1),
                                  N_EXPERTS - 1)
        half_real_end = jnp.sum(jnp.where(half_expert[:, None] == er[None, :], (pad_start + counts)[None, :], 0), axis=1)
        no_real_rows = (half_start >= pad_end[-1]) | (half_start >= half_real_end)
        fill = jnp.where(no_real_rows, FILL_FREE,
                         jnp.where(half_start + MOE_HALF > half_real_end, FILL_TAIL, 0)).astype(jnp.int32)
        xs = _dispatch(dest, fill, x1t, n_rows)
        y_rows = _experts(blk_expert, blk_slot.astype(jnp.int32), blk_next, blk_full, n_used, xs, w_gate[l], b_gate[l].reshape(N_EXPERTS, 1, -1),
                          w_up[l], b_up[l].reshape(N_EXPERTS, 1, -1), w_down[l], b_down[l].reshape(N_EXPERTS, 1, -1))
        h = _combine(dest, gates, x1, ln2_g[l].reshape(1, D), ln2_b[l].reshape(1, D), y_rows, alpha)
    return h.reshape(B, T, D)
```

```python
import functools
import math

import jax
import jax.numpy as jnp
from jax import lax
from jax.experimental import pallas as pl
from jax.experimental.pallas import tpu as pltpu

ROPE_THETA = 500000.0
MLA_HEADS = 8
MLA_NOPE = 64
MLA_ROPE = 32
MLA_V = 64
Q_LORA = 256
KV_LORA = 128
MOBA_HEADS = 8
MOBA_HD = 64
MOBA_ROT = MOBA_HD // 4
MOBA_BLOCK = 256
MOBA_TOPK = 3
N_EXPERTS = 32
MOE_TOPK = 4
SWIGLU_LIMIT = 7.0
SWIGLU_ALPHA = 1.702
RMS_EPS = 1e-6
LN_EPS = 1e-5

LANES = 128
SUBLANES = 8
VMEM_LIMIT_BYTES = 56 * 1024 * 1024

SLOT = LANES
TQ = MOBA_BLOCK
ROW_TILE = 256
OPROJ_TILE = 1024
MOE_GROUP = 512
MOE_HALF = MOE_GROUP // 2
PACKED_SUB = SUBLANES // 2
FILL_TAIL, FILL_FREE = 1, 2
DISPATCH_TILE = 1024
COMBINE_TILE = 512
ISSUE_UNROLL = 4
DMA_PRIORITIES = 2
WEIGHT_DMA_PRIORITY = 1
HEAD_LANES = 64
ONES_LANE = 64
BIAS_LANE = 64
ATTN_HEADS_PER_STEP = 8
LOG2E = math.log2(math.e)
NEG_BIG = -(2.0 ** 100)

F32 = jnp.float32
BF16 = jnp.bfloat16
NT_DIMS = (((1,), (1,)), ((), ()))
TN_DIMS = (((0,), (0,)), ((), ()))


def _dot(a, b, precision=None):
    return jnp.dot(a, b, preferred_element_type=F32, precision=precision)


def _dot_nt(a, b, precision=None):
    return lax.dot_general(a, b, NT_DIMS, preferred_element_type=F32, precision=precision)


def _pack_rows_to_tiles(ref, x):
    rows, width = x.shape
    bits = lambda v: lax.bitcast_convert_type(v.astype(BF16).astype(F32), jnp.uint32)
    words = (bits(x[:, :width // 2]) >> 16) | (bits(x[:, width // 2:]) & jnp.uint32(0xFFFF0000))
    for c in range(PACKED_SUB):
        ref[pl.ds(c, rows, stride=PACKED_SUB), :] = words[:, c * LANES:(c + 1) * LANES]


def _unpack_tiles_to_rows(ref, rows):
    words = jnp.concatenate([ref[pl.ds(c, rows, stride=PACKED_SUB), :] for c in range(PACKED_SUB)], axis=1)
    low = lax.bitcast_convert_type(words << 16, F32).astype(BF16)
    high = lax.bitcast_convert_type(words & jnp.uint32(0xFFFF0000), F32).astype(BF16)
    return jnp.concatenate([low, high], axis=1)


def _rms(x, g):
    return x * lax.rsqrt(jnp.mean(x * x, axis=-1, keepdims=True) + RMS_EPS) * g


def _layer_norm(x, g, b):
    mu = jnp.mean(x, axis=-1, keepdims=True)
    xc = x - mu
    var = jnp.mean(xc * xc, axis=-1, keepdims=True)
    return xc * lax.rsqrt(var + LN_EPS) * g + b


_C_QL = 0
_C_KVL = _C_QL + Q_LORA
_C_KR = _C_KVL + KV_LORA
_C_MK = _C_KR + SLOT
_MH = MOBA_HEADS * MOBA_HD
_C_END = _C_MK + _MH


def _prep_kernel(x_ref, win_ref, wmt_ref, wqt_ref, wk_ref, wvt_ref, qg_ref, kvg_ref,
                 cat_ref, sat_ref, cbt_ref, sbt_ref,
                 qat_ref, ka_ref, vat_ref, mqt_ref, mk_ref, mvt_ref, kmean_scr):
    c = pl.program_id(1)

    @pl.when(c == 0)
    def _():
        kmean_scr[...] = jnp.zeros_like(kmean_scr)

    xb = x_ref[...].astype(BF16)
    both_halves = lambda t: jnp.concatenate([t, t], axis=0)
    cat = both_halves(cat_ref[...])
    sat = both_halves(sat_ref[...])
    cbt = both_halves(cbt_ref[...])
    sat_signed = jnp.concatenate([-sat_ref[...], sat_ref[...]], axis=0)
    sbt_signed = jnp.concatenate([-sbt_ref[...], sbt_ref[...]], axis=0)

    def rotated_t(r, cos, sin_signed):
        half = r.shape[0] // 2
        return r * cos + jnp.concatenate([r[half:], r[:half]], axis=0) * sin_signed

    const = lambda v, n: jnp.full((n, ROW_TILE), v, F32)
    tail = SLOT - MLA_NOPE - MLA_ROPE
    rest = MOBA_HD - MOBA_ROT
    ca = jnp.concatenate([const(1.0, MLA_NOPE), cat, const(0.0, tail)], axis=0).T
    sa = jnp.concatenate([const(0.0, MLA_NOPE), sat, const(0.0, tail)], axis=0).T
    cb = jnp.concatenate([cbt, const(1.0, rest), cbt, const(1.0, rest)], axis=0).T
    sb = jnp.concatenate([sbt_signed, const(0.0, rest), sbt_signed, const(0.0, rest)], axis=0).T
    lane = lax.broadcasted_iota(jnp.int32, (ROW_TILE, SLOT), 1)
    head_lanes = lane < HEAD_LANES
    first_half = (lane % MOBA_HD) < MOBA_ROT // 2
    ones_rows = (lax.broadcasted_iota(jnp.int32, (SLOT - HEAD_LANES, ROW_TILE), 0) == ONES_LANE - HEAD_LANES).astype(F32)

    ql = _dot(xb, win_ref[:, _C_QL:_C_KVL])
    kvl = _dot(xb, win_ref[:, _C_KVL:_C_KR])
    kr = _dot(xb, win_ref[:, _C_KR:_C_MK])
    krs = pltpu.roll(kr, SLOT - MLA_ROPE, axis=1)
    qn = _rms(ql, qg_ref[...]).astype(BF16)
    kvn = _rms(kvl, kvg_ref[...]).astype(BF16)
    q_t = _dot_nt(wqt_ref[...], qn)
    kn = _dot(kvn, wk_ref[...])
    v_t = _dot_nt(wvt_ref[...], kvn)
    scale_a = LOG2E / math.sqrt(MLA_NOPE + MLA_ROPE)
    kro = kr * ca + krs * sa
    qd = MLA_NOPE + MLA_ROPE
    q_pad = jnp.zeros((SLOT - qd, ROW_TILE), F32)
    for h in range(MLA_HEADS):
        sl = slice(h * SLOT, (h + 1) * SLOT)
        q_rope = rotated_t(q_t[h * qd + MLA_NOPE:(h + 1) * qd], cat, sat_signed)
        q_slot = jnp.concatenate([q_t[h * qd:h * qd + MLA_NOPE], q_rope, q_pad], axis=0)
        qat_ref[0, sl, :] = (q_slot * scale_a).astype(BF16)
        ka_ref[:, sl] = (kn[:, sl] + kro).astype(BF16)
        vat_ref[0, sl, :] = jnp.concatenate([v_t[h * MLA_V:(h + 1) * MLA_V], ones_rows], axis=0).astype(BF16)

    mk = _dot(xb, win_ref[:, _C_MK:_C_END])
    npair = _MH // LANES
    pair = lambda a, j: a[:, j * LANES:(j + 1) * LANES]

    def rotated(a):
        x2 = pltpu.roll(a, LANES - MOBA_ROT // 2, axis=1)
        x1 = pltpu.roll(a, MOBA_ROT // 2, axis=1)
        return a * cb + jnp.where(first_half, x2, x1) * sb

    mk_rot = [rotated(pair(mk, j)) for j in range(npair)]
    mk_all = jnp.concatenate(mk_rot, axis=1)

    mq_t = _dot_nt(wmt_ref[0:_MH, :], xb)
    mv_t = _dot_nt(wmt_ref[_MH:2 * _MH, :], xb)
    head = lambda a, h: a[h * MOBA_HD:(h + 1) * MOBA_HD]
    mq_rot_t = [jnp.concatenate([rotated_t(head(mq_t, h)[0:MOBA_ROT], cbt, sbt_signed),
                                 head(mq_t, h)[MOBA_ROT:]], axis=0) for h in range(MOBA_HEADS)]

    nrow = MOBA_HEADS * SUBLANES
    row_i = lax.broadcasted_iota(jnp.int32, (nrow, _MH), 0)
    lane_i = lax.broadcasted_iota(jnp.int32, (nrow, _MH), 1)
    kmean_c = jnp.mean(mk_all, axis=0, keepdims=True)
    put = ((row_i % SUBLANES) == c) & ((lane_i // MOBA_HD) == (row_i // SUBLANES))
    table = kmean_scr[...]
    gate_t = _dot(table, jnp.concatenate(mq_rot_t, axis=0), precision=lax.Precision.HIGHEST)
    kmean_scr[...] = jnp.where(put, jnp.broadcast_to(kmean_c, (nrow, _MH)), table)

    n_idx = lax.broadcasted_iota(jnp.int32, (SUBLANES, ROW_TILE), 0)
    valid = n_idx < c
    scale_b = LOG2E / math.sqrt(MOBA_HD)
    zero_rows = jnp.zeros((SLOT - HEAD_LANES - SUBLANES, ROW_TILE), F32)
    block_onehot = (lane == BIAS_LANE + c).astype(F32)
    for h in range(MOBA_HEADS):
        g = jnp.where(valid, gate_t[h * SUBLANES:(h + 1) * SUBLANES, :], -jnp.inf)
        rank = jnp.zeros((SUBLANES, ROW_TILE), jnp.int32)
        for k in range(1, SUBLANES):
            other = pltpu.roll(g, k, axis=0)
            other_n = pltpu.roll(n_idx, k, axis=0)
            beats = (other > g) | ((other == g) & (other_n < n_idx))
            rank = rank + beats.astype(jnp.int32)
        keep = (valid & (rank < MOBA_TOPK)) | (n_idx == c)
        bias = jnp.where(keep, 0.0, NEG_BIG)
        sl = slice(h * SLOT, (h + 1) * SLOT)
        mqt_ref[0, sl, :] = jnp.concatenate([mq_rot_t[h] * scale_b, bias, zero_rows], axis=0).astype(BF16)
        mvt_ref[0, sl, :] = jnp.concatenate([head(mv_t, h), ones_rows], axis=0).astype(BF16)
        j, hh = divmod(h, 2)
        k_h = mk_rot[j] if hh == 0 else pltpu.roll(mk_rot[j], HEAD_LANES, axis=1)
        mk_ref[:, sl] = jnp.where(head_lanes, k_h, block_onehot).astype(BF16)


def _prep(x2, win, wmt, wqt, wk, wvt, qg, kvg, tables, B, T):
    N, D = x2.shape
    nt = T // ROW_TILE
    row = lambda b, c: (b * nt + c, 0)
    col = lambda b, c: (0, b * nt + c)
    full = lambda b, c: (0, 0)
    cat, sat, cbt, sbt = tables
    width = MLA_HEADS * SLOT

    def rows(w):
        return pl.BlockSpec((ROW_TILE, w), row)

    def cols(a):
        return pl.BlockSpec((a.shape[0], ROW_TILE), col)

    def whole(a):
        return pl.BlockSpec(a.shape, full)

    rowmajor = jax.ShapeDtypeStruct((N, width), BF16)
    transposed = jax.ShapeDtypeStruct((B * nt, width, ROW_TILE), BF16)
    t_spec = pl.BlockSpec((1, width, ROW_TILE), lambda b, c: (b * nt + c, 0, 0))
    return pl.pallas_call(
        _prep_kernel,
        out_shape=(transposed, rowmajor, transposed, transposed, rowmajor, transposed),
        grid=(B, nt),
        in_specs=[rows(D), whole(win), whole(wmt), whole(wqt), whole(wk), whole(wvt), whole(qg), whole(kvg),
                  cols(cat), cols(sat), cols(cbt), cols(sbt)],
        out_specs=(t_spec, rows(width), t_spec, t_spec, rows(width), t_spec),
        scratch_shapes=[pltpu.VMEM((MOBA_HEADS * SUBLANES, _MH), F32)],
        compiler_params=pltpu.CompilerParams(dimension_semantics=("arbitrary", "arbitrary"),
                                             vmem_limit_bytes=VMEM_LIMIT_BYTES),
        name="prep",
    )(x2, win, wmt, wqt, wk, wvt, qg, kvg, cat, sat, cbt, sbt)


def _attn_kernel(qt_ref, k_ref, vt_ref, o_ref, s_scr, acc_scr):
    i = pl.program_id(2)
    nh = ATTN_HEADS_PER_STEP
    slot = lambda h: slice(h * SLOT, (h + 1) * SLOT)
    nblk = i + 1
    nquad = nblk // 4
    pair_end = 4 * nquad + 2 * ((nblk - 4 * nquad) // 2)
    q_pos = i * TQ + lax.broadcasted_iota(jnp.int32, (1, TQ), 1)

    def fold(s):
        out = s[0:SUBLANES]
        for t in range(1, s.shape[0] // SUBLANES):
            out = jnp.maximum(out, s[t * SUBLANES:(t + 1) * SUBLANES])
        return out

    def score_blocks(j, n, mrun):
        j0 = pl.multiple_of(j * TQ, TQ)
        visible = (j0 + lax.broadcasted_iota(jnp.int32, (n * TQ, TQ), 0)) <= q_pos
        out = []
        for h in range(nh):
            s = _dot(k_ref[pl.ds(j0, n * TQ), slot(h)], qt_ref[0, slot(h), :])
            s = jnp.where(visible, s, -jnp.inf)
            s_scr[h, pl.ds(j, n)] = s.reshape(n, TQ, TQ)
            out.append(jnp.maximum(mrun[h], fold(s)))
        return tuple(out)

    mrun = tuple(jnp.full((SUBLANES, TQ), -jnp.inf, F32) for _ in range(nh))
    mrun = lax.fori_loop(0, nquad, lambda jq, m: score_blocks(4 * jq, 4, m), mrun)
    mrun = lax.fori_loop(2 * nquad, pair_end // 2, lambda jj, m: score_blocks(2 * jj, 2, m), mrun)
    mrun = lax.fori_loop(pair_end, nblk, lambda j, m: score_blocks(j, 1, m), mrun)
    ms = [jnp.max(m, axis=0, keepdims=True) for m in mrun]

    def accumulate(j, n, carry):
        for h in range(nh):
            p = jnp.exp2(s_scr[h, pl.ds(j, n)].reshape(n * TQ, TQ) - ms[h]).astype(BF16)
            vt = jnp.concatenate([vt_ref[j + t, slot(h), :] for t in range(n)], axis=1)
            acc_scr[h] += _dot(vt, p)
        return carry

    acc_scr[...] = jnp.zeros_like(acc_scr)
    lax.fori_loop(0, nquad, lambda jq, c: accumulate(4 * jq, 4, c), 0)
    lax.fori_loop(2 * nquad, pair_end // 2, lambda jj, c: accumulate(2 * jj, 2, c), 0)
    lax.fori_loop(pair_end, nblk, lambda j, c: accumulate(j, 1, c), 0)
    outs = [acc_scr[h, 0:HEAD_LANES] / acc_scr[h, ONES_LANE:ONES_LANE + 1] for h in range(nh)]
    for h in range(nh):
        o_ref[0, h * HEAD_LANES:(h + 1) * HEAD_LANES, :] = outs[h].astype(o_ref.dtype)


def _attention(qt, k, vt, B, T, name):
    N = k.shape[0]
    nq = T // TQ
    nh = ATTN_HEADS_PER_STEP
    heads = k.shape[1] // SLOT
    return pl.pallas_call(
        _attn_kernel,
        out_shape=jax.ShapeDtypeStruct((B * nq, heads * HEAD_LANES, TQ), BF16),
        grid=(B, heads // nh, nq),
        in_specs=[pl.BlockSpec((1, nh * SLOT, TQ), lambda b, g, i: (b * nq + i, g, 0)),
                  pl.BlockSpec((T, nh * SLOT), lambda b, g, i: (b, g)),
                  pl.BlockSpec((nq, nh * SLOT, TQ), lambda b, g, i: (b, g, 0))],
        out_specs=pl.BlockSpec((1, nh * HEAD_LANES, TQ), lambda b, g, i: (b * nq + i, g, 0)),
        scratch_shapes=[pltpu.VMEM((nh, nq, TQ, TQ), F32), pltpu.VMEM((nh, SLOT, TQ), F32)],
        compiler_params=pltpu.CompilerParams(dimension_semantics=("arbitrary", "arbitrary", "arbitrary"),
                                             vmem_limit_bytes=VMEM_LIMIT_BYTES),
        name=name,
    )(qt, k, vt)


def _oproj_kernel(alpha, a_ref, m_ref, x_ref, wo_ref, g_ref, b_ref, wr_ref, br_ref,
                  x1_ref, x1t_ref, route_ref, gates_ref, cnt_ref, carry_scr):
    i = pl.program_id(0)

    @pl.when(i == 0)
    def _():
        carry_scr[...] = jnp.zeros_like(carry_scr)

    parts = []
    for t in range(a_ref.shape[0]):
        lhs_t = jnp.concatenate([a_ref[t], m_ref[t]], axis=0)
        parts.append(lax.dot_general(lhs_t, wo_ref[...], TN_DIMS, preferred_element_type=F32))
    mix = jnp.concatenate(parts, axis=0)
    x1 = _layer_norm(alpha * x_ref[...] + mix, g_ref[...], b_ref[...])
    x1_ref[...] = x1
    _pack_rows_to_tiles(x1t_ref, x1)

    ne = br_ref.shape[0]
    rows = x1.shape[0]
    x_hi = x1.astype(BF16)
    x_lo = (x1 - x_hi.astype(F32)).astype(BF16)
    both = _dot(x_hi, wr_ref[...])
    logits_rm = both[:, :LANES] + both[:, LANES:] + _dot(x_lo, wr_ref[:, :LANES])
    logits = logits_rm.T[0:ne] + br_ref[...]
    expert = lax.broadcasted_iota(jnp.int32, logits.shape, 0)
    expert_f = expert.astype(F32)
    vals, idxs = [], []
    work = logits
    for _ in range(MOE_TOPK):
        mx = jnp.max(work, axis=0, keepdims=True)
        ix = jnp.min(jnp.where(work == mx, expert_f, float(ne)), axis=0, keepdims=True).astype(jnp.int32)
        vals.append(mx)
        idxs.append(ix)
        work = jnp.where(expert == ix, -jnp.inf, work)
    exps = [jnp.exp(v - vals[0]) for v in vals]
    den = exps[0]
    for e in exps[1:]:
        den = den + e

    onehot = jnp.zeros(logits.shape, F32)
    for ix in idxs:
        onehot = onehot + (expert == ix).astype(F32)
    r = lax.broadcasted_iota(jnp.int32, (rows, rows), 0)
    cidx = lax.broadcasted_iota(jnp.int32, (rows, rows), 1)
    earlier = (r < cidx).astype(BF16)
    carry = carry_scr[:, 0:1]
    before = _dot(onehot.astype(BF16), earlier) + carry
    row8 = lax.broadcasted_iota(jnp.int32, (SUBLANES, rows), 0)
    route = jnp.zeros((SUBLANES, rows), jnp.int32)
    gates = jnp.zeros((SUBLANES, rows), F32)
    for k in range(MOE_TOPK):
        rank = jnp.sum(jnp.where(expert == idxs[k], before, 0.0), axis=0, keepdims=True).astype(jnp.int32)
        route = jnp.where(row8 == k, idxs[k], route)
        route = jnp.where(row8 == MOE_TOPK + k, rank, route)
        gates = jnp.where(row8 == k, exps[k] / den, gates)
    route_ref[...] = route
    gates_ref[...] = jnp.concatenate([gates, jnp.zeros((LANES - SUBLANES, rows), F32)], axis=0).T
    new_carry = carry + jnp.sum(onehot, axis=1, keepdims=True)
    carry_scr[...] = jnp.broadcast_to(new_carry, carry_scr.shape)
    cnt_ref[...] = jnp.broadcast_to(new_carry, cnt_ref.shape)


def _oproj(a, m, x2, wo, g1, b1, wr, br, alpha):
    N, D = x2.shape
    nt = N // OPROJ_TILE
    row = lambda i: (i, 0)
    full = lambda i: (0, 0)
    return pl.pallas_call(
        functools.partial(_oproj_kernel, alpha),
        out_shape=(jax.ShapeDtypeStruct((N, D), F32),
                   jax.ShapeDtypeStruct((N * PACKED_SUB, LANES), jnp.uint32),
                   jax.ShapeDtypeStruct((SUBLANES, N), jnp.int32),
                   jax.ShapeDtypeStruct((N, LANES), F32),
                   jax.ShapeDtypeStruct((br.shape[0], LANES), F32)),
        grid=(nt,),
        in_specs=[pl.BlockSpec((OPROJ_TILE // TQ,) + a.shape[1:], lambda i: (i, 0, 0)),
                  pl.BlockSpec((OPROJ_TILE // TQ,) + m.shape[1:], lambda i: (i, 0, 0)),
                  pl.BlockSpec((OPROJ_TILE, D), row), pl.BlockSpec(wo.shape, full),
                  pl.BlockSpec(g1.shape, full), pl.BlockSpec(b1.shape, full),
                  pl.BlockSpec(wr.shape, full), pl.BlockSpec(br.shape, full)],
        out_specs=(pl.BlockSpec((OPROJ_TILE, D), row), pl.BlockSpec((OPROJ_TILE * PACKED_SUB, LANES), row),
                   pl.BlockSpec((SUBLANES, OPROJ_TILE), lambda i: (0, i)),
                   pl.BlockSpec((OPROJ_TILE, LANES), row), pl.BlockSpec((br.shape[0], LANES), full)),
        scratch_shapes=[pltpu.VMEM((br.shape[0], LANES), F32)],
        compiler_params=pltpu.CompilerParams(dimension_semantics=("arbitrary",),
                                             vmem_limit_bytes=VMEM_LIMIT_BYTES),
        name="oproj_router",
    )(a, m, x2, wo, g1, b1, wr, br)


def _dispatch_kernel(dest_ref, fill_ref, x1_ref, xs_ref, zero_scr, sem, fill_sem):
    i = pl.program_id(0)
    fill_sub = MOE_HALF * PACKED_SUB
    nfill = xs_ref.shape[0] // fill_sub

    def fill_copy(blk, on):
        return pltpu.make_async_copy(zero_scr, xs_ref.at[pl.ds(pl.multiple_of(blk * fill_sub, fill_sub), fill_sub)], on)

    def for_blocks(kind, action):
        def body(blk, _):
            @pl.when(fill_ref[blk] == kind)
            def _():
                action(blk)
            return 0

        lax.fori_loop(0, nfill, body, 0)

    @pl.when(i == 0)
    def _():
        zero_scr[...] = jnp.zeros_like(zero_scr)
        for_blocks(FILL_TAIL, lambda blk: fill_copy(blk, sem).start())
        for_blocks(FILL_FREE, lambda blk: fill_copy(blk, fill_sem).start())
        for_blocks(FILL_TAIL, lambda blk: fill_copy(blk, sem).wait())

    base = i * DISPATCH_TILE
    n_tok = pl.num_programs(0) * DISPATCH_TILE

    def row_copy(t, k):
        d = dest_ref[k * n_tok + base + t]
        return pltpu.make_async_copy(x1_ref.at[pl.ds(pl.multiple_of(t * PACKED_SUB, PACKED_SUB), PACKED_SUB)],
                                     xs_ref.at[pl.ds(pl.multiple_of(d * PACKED_SUB, PACKED_SUB), PACKED_SUB)], sem)

    def start_rows(t, _):
        for k in range(MOE_TOPK):
            row_copy(t, k).start(priority=k % DMA_PRIORITIES)
        return 0

    lax.fori_loop(0, DISPATCH_TILE, start_rows, 0, unroll=ISSUE_UNROLL)
    for k in range(MOE_TOPK):
        pltpu.make_async_copy(x1_ref, xs_ref.at[pl.ds(0, DISPATCH_TILE * PACKED_SUB)], sem).wait()

    @pl.when(i == pl.num_programs(0) - 1)
    def _():
        for_blocks(FILL_FREE, lambda blk: fill_copy(blk, fill_sem).wait())


def _dispatch(dest, fill, x1t, n_rows):
    return pl.pallas_call(
        _dispatch_kernel,
        out_shape=jax.ShapeDtypeStruct((n_rows * PACKED_SUB, LANES), x1t.dtype),
        grid_spec=pltpu.PrefetchScalarGridSpec(
            num_scalar_prefetch=2,
            grid=(x1t.shape[0] // (DISPATCH_TILE * PACKED_SUB),),
            in_specs=[pl.BlockSpec((DISPATCH_TILE * PACKED_SUB, LANES), lambda i, d, t: (i, 0))],
            out_specs=pl.BlockSpec(memory_space=pl.ANY),
            scratch_shapes=[pltpu.VMEM((MOE_HALF * PACKED_SUB, LANES), x1t.dtype), pltpu.SemaphoreType.DMA(()),
                            pltpu.SemaphoreType.DMA(())],
        ),
        compiler_params=pltpu.CompilerParams(dimension_semantics=("arbitrary",),
                                             vmem_limit_bytes=VMEM_LIMIT_BYTES),
        name="dispatch",
    )(dest, fill, x1t)


def _experts_kernel(be_ref, slot_ref, nxt_ref, full_ref, nused_ref, x_ref, wg_hbm, bg_ref, wu_hbm, bu_ref, wd_hbm, bd_ref,
                    y_ref, wg_st, wu_st, wd_st, wg_bf, wu_bf, wd_bf, sems):
    i = pl.program_id(0)
    prev = be_ref[jnp.maximum(i - 1, 0)]
    changed = (i == 0) | (be_ref[i] != prev)
    active = i < nused_ref[0]
    slot = slot_ref[i]

    def weight_copies(expert, s):
        return (pltpu.make_async_copy(wg_hbm.at[expert], wg_st.at[s], sems.at[s, 0]),
                pltpu.make_async_copy(wu_hbm.at[expert], wu_st.at[s], sems.at[s, 1]),
                pltpu.make_async_copy(wd_hbm.at[expert], wd_st.at[s], sems.at[s, 2]))

    @pl.when(i == 0)
    def _():
        for cp in weight_copies(be_ref[0], slot):
            cp.start()

    @pl.when(active & changed)
    def _():
        for cp in weight_copies(be_ref[i], slot):
            cp.wait()

        @pl.when(nxt_ref[i] >= 0)
        def _():
            for cp in weight_copies(nxt_ref[i], 1 - slot):
                cp.start(priority=WEIGHT_DMA_PRIORITY)

        wg_bf[...] = wg_st[slot].astype(BF16)
        wu_bf[...] = wu_st[slot].astype(BF16)
        wd_bf[...] = wd_st[slot].astype(BF16)

    def mlp(nrows):
        rows = pl.ds(0, nrows * PACKED_SUB)
        xb = _unpack_tiles_to_rows(x_ref.at[rows], nrows)
        g = jnp.minimum(_dot(xb, wg_bf[...]) + bg_ref[0], SWIGLU_LIMIT)
        u = jnp.clip(_dot(xb, wu_bf[...]) + bu_ref[0], -SWIGLU_LIMIT, SWIGLU_LIMIT)
        h = g * (1.0 / (1.0 + jnp.exp(-SWIGLU_ALPHA * g))) * (u + 1.0)
        _pack_rows_to_tiles(y_ref.at[rows], _dot(h.astype(BF16), wd_bf[...]) + bd_ref[0])

    whole = active & (full_ref[i] > 0)

    @pl.when(whole)
    def _():
        mlp(MOE_GROUP)

    @pl.when(active & jnp.logical_not(whole))
    def _():
        mlp(MOE_HALF)

    @pl.when(jnp.logical_not(active))
    def _():
        y_ref[pl.ds(0, MOE_HALF * PACKED_SUB), :] = jnp.zeros((MOE_HALF * PACKED_SUB, LANES), y_ref.dtype)

    @pl.when(jnp.logical_not(whole))
    def _():
        y_ref[pl.ds(MOE_HALF * PACKED_SUB, MOE_HALF * PACKED_SUB), :] = jnp.zeros((MOE_HALF * PACKED_SUB, LANES), y_ref.dtype)


def _experts(blk_expert, blk_slot, blk_next, blk_full, n_used, xs, wg, bg, wu, bu, wd, bd):
    E, D, F = wg.shape
    nblk = xs.shape[0] // (MOE_GROUP * PACKED_SUB)

    def rowmap(i, be, sl, nx, fl, nu):
        return (jnp.minimum(i, nu[0] - 1), 0)

    def bmap(i, be, sl, nx, fl, nu):
        return (be[i], 0, 0)

    hbm = pl.BlockSpec(memory_space=pl.ANY)
    return pl.pallas_call(
        _experts_kernel,
        out_shape=jax.ShapeDtypeStruct(xs.shape, xs.dtype),
        grid_spec=pltpu.PrefetchScalarGridSpec(
            num_scalar_prefetch=5,
            grid=(nblk,),
            in_specs=[pl.BlockSpec((MOE_GROUP * PACKED_SUB, LANES), rowmap),
                      hbm, pl.BlockSpec((1, 1, F), bmap),
                      hbm, pl.BlockSpec((1, 1, F), bmap),
                      hbm, pl.BlockSpec((1, 1, D), bmap)],
            out_specs=pl.BlockSpec((MOE_GROUP * PACKED_SUB, LANES), lambda i, be, sl, nx, fl, nu: (i, 0)),
            scratch_shapes=[pltpu.VMEM((2, D, F), F32), pltpu.VMEM((2, D, F), F32), pltpu.VMEM((2, F, D), F32),
                            pltpu.VMEM((D, F), BF16), pltpu.VMEM((D, F), BF16), pltpu.VMEM((F, D), BF16),
                            pltpu.SemaphoreType.DMA((2, 3))],
        ),
        compiler_params=pltpu.CompilerParams(dimension_semantics=("arbitrary",),
                                             vmem_limit_bytes=VMEM_LIMIT_BYTES),
        name="experts",
    )(blk_expert, blk_slot, blk_next, blk_full, n_used, xs, wg, bg, wu, bu, wd, bd)


def _combine_kernel(alpha, dest_ref, gates_ref, x1_ref, g_ref, b_ref, y_ref, o_ref, ybuf, sems):
    i = pl.program_id(0)
    buf = i % 2

    n_tok = pl.num_programs(0) * COMBINE_TILE

    def gather_tile(tile, into):
        base = tile * COMBINE_TILE

        def start_rows(t, _):
            for k in range(MOE_TOPK):
                d = dest_ref[k * n_tok + base + t]
                pltpu.make_async_copy(y_ref.at[pl.ds(pl.multiple_of(d * PACKED_SUB, PACKED_SUB), PACKED_SUB)],
                                      ybuf.at[into, k, pl.ds(pl.multiple_of(t * PACKED_SUB, PACKED_SUB), PACKED_SUB)],
                                      sems.at[into]).start(priority=k % DMA_PRIORITIES)
            return 0

        lax.fori_loop(0, COMBINE_TILE, start_rows, 0, unroll=ISSUE_UNROLL)

    @pl.when(i == 0)
    def _():
        gather_tile(0, 0)

    @pl.when(i + 1 < pl.num_programs(0))
    def _():
        gather_tile(i + 1, 1 - buf)

    for k in range(MOE_TOPK):
        pltpu.make_async_copy(y_ref.at[pl.ds(0, COMBINE_TILE * PACKED_SUB)], ybuf.at[buf, k], sems.at[buf]).wait()

    gates = gates_ref[...]
    ffn = gates[:, 0:1] * _unpack_tiles_to_rows(ybuf.at[buf, 0], COMBINE_TILE).astype(F32)
    for k in range(1, MOE_TOPK):
        ffn = ffn + gates[:, k:k + 1] * _unpack_tiles_to_rows(ybuf.at[buf, k], COMBINE_TILE).astype(F32)
    o_ref[...] = _layer_norm(alpha * x1_ref[...] + ffn, g_ref[...], b_ref[...])


def _combine(dest, gates, x1, g2, b2, y_rows, alpha):
    N, D = x1.shape
    row = lambda i, d: (i, 0)
    full = lambda i, d: (0, 0)
    return pl.pallas_call(
        functools.partial(_combine_kernel, alpha),
        out_shape=jax.ShapeDtypeStruct((N, D), F32),
        grid_spec=pltpu.PrefetchScalarGridSpec(
            num_scalar_prefetch=1,
            grid=(N // COMBINE_TILE,),
            in_specs=[pl.BlockSpec((COMBINE_TILE, LANES), row), pl.BlockSpec((COMBINE_TILE, D), row),
                      pl.BlockSpec(g2.shape, full), pl.BlockSpec(b2.shape, full),
                      pl.BlockSpec(memory_space=pl.ANY)],
            out_specs=pl.BlockSpec((COMBINE_TILE, D), row),
            scratch_shapes=[pltpu.VMEM((2, MOE_TOPK, COMBINE_TILE * PACKED_SUB, LANES), y_rows.dtype),
                            pltpu.SemaphoreType.DMA((2,))],
        ),
        compiler_params=pltpu.CompilerParams(dimension_semantics=("arbitrary",),
                                             vmem_limit_bytes=VMEM_LIMIT_BYTES),
        name="combine",
    )(dest, gates, x1, g2, b2, y_rows)


def _rot_partner(w, half):
    return jnp.concatenate([-w[..., half:2 * half], w[..., :half]], axis=-1)


def _layer_weights(w_in, w_q_b, w_kv_b):
    D = w_in.shape[0]
    o1 = Q_LORA
    o2 = o1 + KV_LORA
    o3 = o2 + MLA_ROPE
    w_ql, w_kvl, w_kr = w_in[:, :o1], w_in[:, o1:o2], w_in[:, o2:o3]
    w_mq, w_mk, w_mv = w_in[:, o3:o3 + _MH], w_in[:, o3 + _MH:o3 + 2 * _MH], w_in[:, o3 + 2 * _MH:]
    zpad = lambda n: jnp.zeros((D, n), w_in.dtype)
    tail = SLOT - MLA_NOPE - MLA_ROPE
    assert tail == MLA_ROPE
    kr_slot = jnp.concatenate([zpad(MLA_NOPE), w_kr, _rot_partner(w_kr, MLA_ROPE // 2)], axis=1)

    win = jnp.concatenate([w_ql, w_kvl, kr_slot, w_mk], axis=1).astype(BF16)
    wmt = jnp.concatenate([w_mq, w_mv], axis=1).T.astype(BF16)

    wkv3 = w_kv_b.reshape(KV_LORA, MLA_HEADS, MLA_NOPE + MLA_V)
    wk = jnp.concatenate([wkv3[..., :MLA_NOPE], jnp.zeros((KV_LORA, MLA_HEADS, SLOT - MLA_NOPE), w_kv_b.dtype)],
                         axis=-1).reshape(KV_LORA, MLA_HEADS * SLOT).astype(BF16)
    wv = wkv3[..., MLA_NOPE:].reshape(KV_LORA, MLA_HEADS * MLA_V)
    return win, wmt, w_q_b.T.astype(BF16), wk, wv.T.astype(BF16)


def _rope_tables(positions):
    pos = positions.astype(F32).reshape(1, -1)

    def cs(d_rot):
        inv_freq = ROPE_THETA ** (-jnp.arange(0, d_rot, 2, dtype=F32) / d_rot)
        ang = inv_freq.reshape(d_rot // 2, 1) * pos
        return jnp.cos(ang), jnp.sin(ang)

    return cs(MLA_ROPE) + cs(MOBA_ROT)


def kernel(x, positions, w_in, q_a_norm, w_q_b, kv_a_norm, w_kv_b, w_o, ln1_g, ln1_b, w_router, b_router,
           w_gate, b_gate, w_up, b_up, w_down, b_down, ln2_g, ln2_b):
    B, T, D = x.shape
    depth = w_in.shape[0]
    alpha = (2.0 * depth) ** 0.25
    N = B * T
    assert T % MOBA_BLOCK == 0 and T // MOBA_BLOCK <= SUBLANES and N % OPROJ_TILE == 0
    assert D == SUBLANES * LANES
    n_asg = N * MOE_TOPK
    n_rows = n_asg + N_EXPERTS * MOE_GROUP
    tables = _rope_tables(positions)
    h = x.reshape(N, D)
    for l in range(depth):
        win, wmt, wqt, wk, wvt = _layer_weights(w_in[l], w_q_b[l], w_kv_b[l])
        qa, ka, va, mq, mk, mv = _prep(h, win, wmt, wqt, wk, wvt, q_a_norm[l].reshape(1, -1),
                                       kv_a_norm[l].reshape(1, -1), tables, B, T)
        a = _attention(qa, ka, va, B, T, "mla_attention")
        m = _attention(mq, mk, mv, B, T, "moba_attention")
        wr_pad = jnp.concatenate([w_router[l], jnp.zeros((D, LANES - N_EXPERTS), F32)], axis=1)
        wr_hi = wr_pad.astype(BF16)
        wr = jnp.concatenate([wr_hi, (wr_pad - wr_hi.astype(F32)).astype(BF16)], axis=1)
        br = b_router[l].reshape(N_EXPERTS, 1)
        x1, x1t, route, gates, cnt = _oproj(a, m, h, w_o[l].astype(BF16), ln1_g[l].reshape(1, D), ln1_b[l].reshape(1, D),
                                       wr, br, alpha)
        er = jnp.arange(N_EXPERTS, dtype=jnp.int32)
        counts = cnt[:, 0].astype(jnp.int32)
        padded = (counts + MOE_GROUP - 1) // MOE_GROUP * MOE_GROUP
        upto = er[None, :] <= er[:, None]
        pad_end = jnp.sum(jnp.where(upto, padded[None, :], 0), axis=1)
        pad_start = pad_end - padded
        e_idx = route[:MOE_TOPK]
        group_start = jnp.sum(jnp.where(e_idx[..., None] == er, pad_start, 0), axis=-1)
        dest = (group_start + route[MOE_TOPK:2 * MOE_TOPK]).reshape(n_asg)
        n_used = (pad_end[-1:] // MOE_GROUP).astype(jnp.int32)
        nblk = n_rows // MOE_GROUP
        blk_start = jnp.arange(nblk, dtype=jnp.int32) * MOE_GROUP
        blk_expert = jnp.minimum(jnp.sum((blk_start[:, None] >= pad_end[None, :]).astype(jnp.int32), axis=1),
                                 N_EXPERTS - 1)
        of_expert = blk_expert[:, None] == er[None, :]
        pick = lambda per_expert: jnp.sum(jnp.where(of_expert, per_expert[None, :], 0), axis=1)
        nonempty = padded > 0
        group_index = jnp.sum(jnp.where(upto & nonempty[None, :], 1, 0), axis=1) - 1
        later = (er[None, :] > er[:, None]) & nonempty[None, :]
        next_expert = jnp.min(jnp.where(later, er[None, :], N_EXPERTS), axis=1)
        next_expert = jnp.where(next_expert == N_EXPERTS, -1, next_expert)
        blk_slot = pick(group_index) % 2
        blk_next = pick(next_expert)
        blk_full = (pick(pad_start + counts) > blk_start + MOE_HALF).astype(jnp.int32)
        half_start = jnp.arange(n_rows // MOE_HALF, dtype=jnp.int32) * MOE_HALF
        half_expert = jnp.minimum(jnp.sum((half_start[:, None] >= pad_end[None, :]).astype(jnp.int32), axis=1),
                                  N_EXPERTS - 1)
        half_real_end = jnp.sum(jnp.where(half_expert[:, None] == er[None, :], (pad_start + counts)[None, :], 0), axis=1)
        no_real_rows = (half_start >= pad_end[-1]) | (half_start >= half_real_end)
        fill = jnp.where(no_real_rows, FILL_FREE,
                         jnp.where(half_start + MOE_HALF > half_real_end, FILL_TAIL, 0)).astype(jnp.int32)
        xs = _dispatch(dest, fill, x1t, n_rows)
        y_rows = _experts(blk_expert, blk_slot.astype(jnp.int32), blk_next, blk_full, n_used, xs, w_gate[l], b_gate[l].reshape(N_EXPERTS, 1, -1),
                          w_up[l], b_up[l].reshape(N_EXPERTS, 1, -1), w_down[l], b_down[l].reshape(N_EXPERTS, 1, -1))
        h = _combine(dest, gates, x1, ln2_g[l].reshape(1, D), ln2_b[l].reshape(1, D), y_rows, alpha)
    return h.reshape(B, T, D)
```

```python
import functools
import math

import jax
import jax.numpy as jnp
from jax import lax
from jax.experimental import pallas as pl
from jax.experimental.pallas import tpu as pltpu

ROPE_THETA = 500000.0
MLA_HEADS = 8
MLA_NOPE = 64
MLA_ROPE = 32
MLA_V = 64
Q_LORA = 256
KV_LORA = 128
MOBA_HEADS = 8
MOBA_HD = 64
MOBA_ROT = MOBA_HD // 4
MOBA_BLOCK = 256
MOBA_TOPK = 3
N_EXPERTS = 32
MOE_TOPK = 4
SWIGLU_LIMIT = 7.0
SWIGLU_ALPHA = 1.702
RMS_EPS = 1e-6
LN_EPS = 1e-5

LANES = 128
SUBLANES = 8
VMEM_LIMIT_BYTES = 56 * 1024 * 1024

SLOT = LANES
TQ = MOBA_BLOCK
ROW_TILE = 256
OPROJ_TILE = 1024
MOE_GROUP = 512
MOE_HALF = MOE_GROUP // 2
PACKED_SUB = SUBLANES // 2
FILL_TAIL, FILL_FREE = 1, 2
DISPATCH_TILE = 1024
COMBINE_TILE = 512
ISSUE_UNROLL = 4
DMA_PRIORITIES = 2
WEIGHT_DMA_PRIORITY = 1
HEAD_LANES = 64
ONES_LANE = 64
BIAS_LANE = 64
ATTN_HEADS_PER_STEP = 8
LOG2E = math.log2(math.e)
NEG_BIG = -(2.0 ** 100)

F32 = jnp.float32
BF16 = jnp.bfloat16
NT_DIMS = (((1,), (1,)), ((), ()))
TN_DIMS = (((0,), (0,)), ((), ()))


def _dot(a, b, precision=None):
    return jnp.dot(a, b, preferred_element_type=F32, precision=precision)


def _dot_nt(a, b, precision=None):
    return lax.dot_general(a, b, NT_DIMS, preferred_element_type=F32, precision=precision)


def _pack_rows_to_tiles(ref, x):
    rows, width = x.shape
    bits = lambda v: lax.bitcast_convert_type(v.astype(BF16).astype(F32), jnp.uint32)
    words = (bits(x[:, :width // 2]) >> 16) | (bits(x[:, width // 2:]) & jnp.uint32(0xFFFF0000))
    for c in range(PACKED_SUB):
        ref[pl.ds(c, rows, stride=PACKED_SUB), :] = words[:, c * LANES:(c + 1) * LANES]


def _unpack_tiles_to_rows(ref, rows):
    words = jnp.concatenate([ref[pl.ds(c, rows, stride=PACKED_SUB), :] for c in range(PACKED_SUB)], axis=1)
    low = lax.bitcast_convert_type(words << 16, F32).astype(BF16)
    high = lax.bitcast_convert_type(words & jnp.uint32(0xFFFF0000), F32).astype(BF16)
    return jnp.concatenate([low, high], axis=1)


def _rms(x, g):
    return x * lax.rsqrt(jnp.mean(x * x, axis=-1, keepdims=True) + RMS_EPS) * g


def _layer_norm(x, g, b):
    mu = jnp.mean(x, axis=-1, keepdims=True)
    xc = x - mu
    var = jnp.mean(xc * xc, axis=-1, keepdims=True)
    return xc * lax.rsqrt(var + LN_EPS) * g + b


_C_QL = 0
_C_KVL = _C_QL + Q_LORA
_C_KR = _C_KVL + KV_LORA
_C_MK = _C_KR + SLOT
_MH = MOBA_HEADS * MOBA_HD
_C_END = _C_MK + _MH


def _prep_kernel(x_ref, win_ref, wmt_ref, wqt_ref, wk_ref, wvt_ref, qg_ref, kvg_ref,
                 cat_ref, sat_ref, cbt_ref, sbt_ref,
                 qat_ref, ka_ref, vat_ref, mqt_ref, mk_ref, mvt_ref, kmean_scr):
    c = pl.program_id(1)

    @pl.when(c == 0)
    def _():
        kmean_scr[...] = jnp.zeros_like(kmean_scr)

    xb = x_ref[...].astype(BF16)
    both_halves = lambda t: jnp.concatenate([t, t], axis=0)
    cat = both_halves(cat_ref[...])
    sat = both_halves(sat_ref[...])
    cbt = both_halves(cbt_ref[...])
    sat_signed = jnp.concatenate([-sat_ref[...], sat_ref[...]], axis=0)
    sbt_signed = jnp.concatenate([-sbt_ref[...], sbt_ref[...]], axis=0)

    def rotated_t(r, cos, sin_signed):
        half = r.shape[0] // 2
        return r * cos + jnp.concatenate([r[half:], r[:half]], axis=0) * sin_signed

    const = lambda v, n: jnp.full((n, ROW_TILE), v, F32)
    tail = SLOT - MLA_NOPE - MLA_ROPE
    rest = MOBA_HD - MOBA_ROT
    ca = jnp.concatenate([const(1.0, MLA_NOPE), cat, const(0.0, tail)], axis=0).T
    sa = jnp.concatenate([const(0.0, MLA_NOPE), sat, const(0.0, tail)], axis=0).T
    cb = jnp.concatenate([cbt, const(1.0, rest), cbt, const(1.0, rest)], axis=0).T
    sb = jnp.concatenate([sbt_signed, const(0.0, rest), sbt_signed, const(0.0, rest)], axis=0).T
    lane = lax.broadcasted_iota(jnp.int32, (ROW_TILE, SLOT), 1)
    head_lanes = lane < HEAD_LANES
    first_half = (lane % MOBA_HD) < MOBA_ROT // 2
    ones_rows = (lax.broadcasted_iota(jnp.int32, (SLOT - HEAD_LANES, ROW_TILE), 0) == ONES_LANE - HEAD_LANES).astype(F32)

    ql = _dot(xb, win_ref[:, _C_QL:_C_KVL])
    kvl = _dot(xb, win_ref[:, _C_KVL:_C_KR])
    kr = _dot(xb, win_ref[:, _C_KR:_C_MK])
    krs = pltpu.roll(kr, SLOT - MLA_ROPE, axis=1)
    qn = _rms(ql, qg_ref[...]).astype(BF16)
    kvn = _rms(kvl, kvg_ref[...]).astype(BF16)
    q_t = _dot_nt(wqt_ref[...], qn)
    kn = _dot(kvn, wk_ref[...])
    v_t = _dot_nt(wvt_ref[...], kvn)
    scale_a = LOG2E / math.sqrt(MLA_NOPE + MLA_ROPE)
    kro = kr * ca + krs * sa
    qd = MLA_NOPE + MLA_ROPE
    q_pad = jnp.zeros((SLOT - qd, ROW_TILE), F32)
    for h in range(MLA_HEADS):
        sl = slice(h * SLOT, (h + 1) * SLOT)
        q_rope = rotated_t(q_t[h * qd + MLA_NOPE:(h + 1) * qd], cat, sat_signed)
        q_slot = jnp.concatenate([q_t[h * qd:h * qd + MLA_NOPE], q_rope, q_pad], axis=0)
        qat_ref[0, sl, :] = (q_slot * scale_a).astype(BF16)
        j, hh = divmod(h, 2)
        kn_h = kn[:, j * LANES:(j + 1) * LANES]
        kn_h = kn_h if hh == 0 else pltpu.roll(kn_h, HEAD_LANES, axis=1)
        ka_ref[:, sl] = jnp.where(head_lanes, kn_h, kro).astype(BF16)
        vat_ref[0, sl, :] = jnp.concatenate([v_t[h * MLA_V:(h + 1) * MLA_V], ones_rows], axis=0).astype(BF16)

    mk = _dot(xb, win_ref[:, _C_MK:_C_END])
    npair = _MH // LANES
    pair = lambda a, j: a[:, j * LANES:(j + 1) * LANES]

    def rotated(a):
        x2 = pltpu.roll(a, LANES - MOBA_ROT // 2, axis=1)
        x1 = pltpu.roll(a, MOBA_ROT // 2, axis=1)
        return a * cb + jnp.where(first_half, x2, x1) * sb

    mk_rot = [rotated(pair(mk, j)) for j in range(npair)]
    mk_all = jnp.concatenate(mk_rot, axis=1)

    mq_t = _dot_nt(wmt_ref[0:_MH, :], xb)
    mv_t = _dot_nt(wmt_ref[_MH:2 * _MH, :], xb)
    head = lambda a, h: a[h * MOBA_HD:(h + 1) * MOBA_HD]
    mq_rot_t = [jnp.concatenate([rotated_t(head(mq_t, h)[0:MOBA_ROT], cbt, sbt_signed),
                                 head(mq_t, h)[MOBA_ROT:]], axis=0) for h in range(MOBA_HEADS)]

    nrow = MOBA_HEADS * SUBLANES
    row_i = lax.broadcasted_iota(jnp.int32, (nrow, _MH), 0)
    lane_i = lax.broadcasted_iota(jnp.int32, (nrow, _MH), 1)
    kmean_c = jnp.mean(mk_all, axis=0, keepdims=True)
    put = ((row_i % SUBLANES) == c) & ((lane_i // MOBA_HD) == (row_i // SUBLANES))
    table = kmean_scr[...]
    gate_t = _dot(table, jnp.concatenate(mq_rot_t, axis=0), precision=lax.Precision.HIGHEST)
    kmean_scr[...] = jnp.where(put, jnp.broadcast_to(kmean_c, (nrow, _MH)), table)

    n_idx = lax.broadcasted_iota(jnp.int32, (SUBLANES, ROW_TILE), 0)
    valid = n_idx < c
    scale_b = LOG2E / math.sqrt(MOBA_HD)
    zero_rows = jnp.zeros((SLOT - HEAD_LANES - SUBLANES, ROW_TILE), F32)
    block_onehot = (lane == BIAS_LANE + c).astype(F32)
    for h in range(MOBA_HEADS):
        g = jnp.where(valid, gate_t[h * SUBLANES:(h + 1) * SUBLANES, :], -jnp.inf)
        rank = jnp.zeros((SUBLANES, ROW_TILE), jnp.int32)
        for k in range(1, SUBLANES):
            other = pltpu.roll(g, k, axis=0)
            other_n = pltpu.roll(n_idx, k, axis=0)
            beats = (other > g) | ((other == g) & (other_n < n_idx))
            rank = rank + beats.astype(jnp.int32)
        keep = (valid & (rank < MOBA_TOPK)) | (n_idx == c)
        bias = jnp.where(keep, 0.0, NEG_BIG)
        sl = slice(h * SLOT, (h + 1) * SLOT)
        mqt_ref[0, sl, :] = jnp.concatenate([mq_rot_t[h] * scale_b, bias, zero_rows], axis=0).astype(BF16)
        mvt_ref[0, sl, :] = jnp.concatenate([head(mv_t, h), ones_rows], axis=0).astype(BF16)
        j, hh = divmod(h, 2)
        k_h = mk_rot[j] if hh == 0 else pltpu.roll(mk_rot[j], HEAD_LANES, axis=1)
        mk_ref[:, sl] = jnp.where(head_lanes, k_h, block_onehot).astype(BF16)


def _prep(x2, win, wmt, wqt, wk, wvt, qg, kvg, tables, B, T):
    N, D = x2.shape
    nt = T // ROW_TILE
    row = lambda b, c: (b * nt + c, 0)
    col = lambda b, c: (0, b * nt + c)
    full = lambda b, c: (0, 0)
    cat, sat, cbt, sbt = tables
    width = MLA_HEADS * SLOT

    def rows(w):
        return pl.BlockSpec((ROW_TILE, w), row)

    def cols(a):
        return pl.BlockSpec((a.shape[0], ROW_TILE), col)

    def whole(a):
        return pl.BlockSpec(a.shape, full)

    rowmajor = jax.ShapeDtypeStruct((N, width), BF16)
    transposed = jax.ShapeDtypeStruct((B * nt, width, ROW_TILE), BF16)
    t_spec = pl.BlockSpec((1, width, ROW_TILE), lambda b, c: (b * nt + c, 0, 0))
    return pl.pallas_call(
        _prep_kernel,
        out_shape=(transposed, rowmajor, transposed, transposed, rowmajor, transposed),
        grid=(B, nt),
        in_specs=[rows(D), whole(win), whole(wmt), whole(wqt), whole(wk), whole(wvt), whole(qg), whole(kvg),
                  cols(cat), cols(sat), cols(cbt), cols(sbt)],
        out_specs=(t_spec, rows(width), t_spec, t_spec, rows(width), t_spec),
        scratch_shapes=[pltpu.VMEM((MOBA_HEADS * SUBLANES, _MH), F32)],
        compiler_params=pltpu.CompilerParams(dimension_semantics=("arbitrary", "arbitrary"),
                                             vmem_limit_bytes=VMEM_LIMIT_BYTES),
        name="prep",
    )(x2, win, wmt, wqt, wk, wvt, qg, kvg, cat, sat, cbt, sbt)


def _attn_kernel(qt_ref, k_ref, vt_ref, o_ref, s_scr, acc_scr):
    i = pl.program_id(2)
    nh = ATTN_HEADS_PER_STEP
    slot = lambda h: slice(h * SLOT, (h + 1) * SLOT)
    nblk = i + 1
    nquad = nblk // 4
    pair_end = 4 * nquad + 2 * ((nblk - 4 * nquad) // 2)
    q_pos = i * TQ + lax.broadcasted_iota(jnp.int32, (1, TQ), 1)

    def fold(s):
        out = s[0:SUBLANES]
        for t in range(1, s.shape[0] // SUBLANES):
            out = jnp.maximum(out, s[t * SUBLANES:(t + 1) * SUBLANES])
        return out

    def score_blocks(j, n, mrun):
        j0 = pl.multiple_of(j * TQ, TQ)
        visible = (j0 + lax.broadcasted_iota(jnp.int32, (n * TQ, TQ), 0)) <= q_pos
        out = []
        for h in range(nh):
            s = _dot(k_ref[pl.ds(j0, n * TQ), slot(h)], qt_ref[0, slot(h), :])
            s = jnp.where(visible, s, -jnp.inf)
            s_scr[h, pl.ds(j, n)] = s.reshape(n, TQ, TQ)
            out.append(jnp.maximum(mrun[h], fold(s)))
        return tuple(out)

    mrun = tuple(jnp.full((SUBLANES, TQ), -jnp.inf, F32) for _ in range(nh))
    mrun = lax.fori_loop(0, nquad, lambda jq, m: score_blocks(4 * jq, 4, m), mrun)
    mrun = lax.fori_loop(2 * nquad, pair_end // 2, lambda jj, m: score_blocks(2 * jj, 2, m), mrun)
    mrun = lax.fori_loop(pair_end, nblk, lambda j, m: score_blocks(j, 1, m), mrun)
    ms = [jnp.max(m, axis=0, keepdims=True) for m in mrun]

    def accumulate(j, n, carry):
        for h in range(nh):
            p = jnp.exp2(s_scr[h, pl.ds(j, n)].reshape(n * TQ, TQ) - ms[h]).astype(BF16)
            vt = jnp.concatenate([vt_ref[j + t, slot(h), :] for t in range(n)], axis=1)
            acc_scr[h] += _dot(vt, p)
        return carry

    acc_scr[...] = jnp.zeros_like(acc_scr)
    lax.fori_loop(0, nquad, lambda jq, c: accumulate(4 * jq, 4, c), 0)
    lax.fori_loop(2 * nquad, pair_end // 2, lambda jj, c: accumulate(2 * jj, 2, c), 0)
    lax.fori_loop(pair_end, nblk, lambda j, c: accumulate(j, 1, c), 0)
    outs = [acc_scr[h, 0:HEAD_LANES] / acc_scr[h, ONES_LANE:ONES_LANE + 1] for h in range(nh)]
    for h in range(nh):
        o_ref[0, h * HEAD_LANES:(h + 1) * HEAD_LANES, :] = outs[h].astype(o_ref.dtype)


def _attention(qt, k, vt, B, T, name):
    N = k.shape[0]
    nq = T // TQ
    nh = ATTN_HEADS_PER_STEP
    heads = k.shape[1] // SLOT
    return pl.pallas_call(
        _attn_kernel,
        out_shape=jax.ShapeDtypeStruct((B * nq, heads * HEAD_LANES, TQ), BF16),
        grid=(B, heads // nh, nq),
        in_specs=[pl.BlockSpec((1, nh * SLOT, TQ), lambda b, g, i: (b * nq + i, g, 0)),
                  pl.BlockSpec((T, nh * SLOT), lambda b, g, i: (b, g)),
                  pl.BlockSpec((nq, nh * SLOT, TQ), lambda b, g, i: (b, g, 0))],
        out_specs=pl.BlockSpec((1, nh * HEAD_LANES, TQ), lambda b, g, i: (b * nq + i, g, 0)),
        scratch_shapes=[pltpu.VMEM((nh, nq, TQ, TQ), F32), pltpu.VMEM((nh, SLOT, TQ), F32)],
        compiler_params=pltpu.CompilerParams(dimension_semantics=("arbitrary", "arbitrary", "arbitrary"),
                                             vmem_limit_bytes=VMEM_LIMIT_BYTES),
        name=name,
    )(qt, k, vt)


def _oproj_kernel(alpha, a_ref, m_ref, x_ref, wo_ref, g_ref, b_ref, wr_ref, br_ref,
                  x1_ref, x1t_ref, route_ref, gates_ref, cnt_ref, carry_scr):
    i = pl.program_id(0)

    @pl.when(i == 0)
    def _():
        carry_scr[...] = jnp.zeros_like(carry_scr)

    parts = []
    for t in range(a_ref.shape[0]):
        lhs_t = jnp.concatenate([a_ref[t], m_ref[t]], axis=0)
        parts.append(lax.dot_general(lhs_t, wo_ref[...], TN_DIMS, preferred_element_type=F32))
    mix = jnp.concatenate(parts, axis=0)
    x1 = _layer_norm(alpha * x_ref[...] + mix, g_ref[...], b_ref[...])
    x1_ref[...] = x1
    _pack_rows_to_tiles(x1t_ref, x1)

    ne = br_ref.shape[0]
    rows = x1.shape[0]
    x_hi = x1.astype(BF16)
    x_lo = (x1 - x_hi.astype(F32)).astype(BF16)
    both = _dot(x_hi, wr_ref[...])
    logits_rm = both[:, :LANES] + both[:, LANES:] + _dot(x_lo, wr_ref[:, :LANES])
    logits = logits_rm.T[0:ne] + br_ref[...]
    expert = lax.broadcasted_iota(jnp.int32, logits.shape, 0)
    expert_f = expert.astype(F32)
    vals, idxs = [], []
    work = logits
    for _ in range(MOE_TOPK):
        mx = jnp.max(work, axis=0, keepdims=True)
        ix = jnp.min(jnp.where(work == mx, expert_f, float(ne)), axis=0, keepdims=True).astype(jnp.int32)
        vals.append(mx)
        idxs.append(ix)
        work = jnp.where(expert == ix, -jnp.inf, work)
    exps = [jnp.exp(v - vals[0]) for v in vals]
    den = exps[0]
    for e in exps[1:]:
        den = den + e

    onehot = jnp.zeros(logits.shape, F32)
    for ix in idxs:
        onehot = onehot + (expert == ix).astype(F32)
    r = lax.broadcasted_iota(jnp.int32, (rows, rows), 0)
    cidx = lax.broadcasted_iota(jnp.int32, (rows, rows), 1)
    earlier = (r < cidx).astype(BF16)
    carry = carry_scr[:, 0:1]
    before = _dot(onehot.astype(BF16), earlier) + carry
    row8 = lax.broadcasted_iota(jnp.int32, (SUBLANES, rows), 0)
    route = jnp.zeros((SUBLANES, rows), jnp.int32)
    gates = jnp.zeros((SUBLANES, rows), F32)
    for k in range(MOE_TOPK):
        rank = jnp.sum(jnp.where(expert == idxs[k], before, 0.0), axis=0, keepdims=True).astype(jnp.int32)
        route = jnp.where(row8 == k, idxs[k], route)
        route = jnp.where(row8 == MOE_TOPK + k, rank, route)
        gates = jnp.where(row8 == k, exps[k] / den, gates)
    route_ref[...] = route
    gates_ref[...] = jnp.concatenate([gates, jnp.zeros((LANES - SUBLANES, rows), F32)], axis=0).T
    new_carry = carry + jnp.sum(onehot, axis=1, keepdims=True)
    carry_scr[...] = jnp.broadcast_to(new_carry, carry_scr.shape)
    cnt_ref[...] = jnp.broadcast_to(new_carry, cnt_ref.shape)


def _oproj(a, m, x2, wo, g1, b1, wr, br, alpha):
    N, D = x2.shape
    nt = N // OPROJ_TILE
    row = lambda i: (i, 0)
    full = lambda i: (0, 0)
    return pl.pallas_call(
        functools.partial(_oproj_kernel, alpha),
        out_shape=(jax.ShapeDtypeStruct((N, D), F32),
                   jax.ShapeDtypeStruct((N * PACKED_SUB, LANES), jnp.uint32),
                   jax.ShapeDtypeStruct((SUBLANES, N), jnp.int32),
                   jax.ShapeDtypeStruct((N, LANES), F32),
                   jax.ShapeDtypeStruct((br.shape[0], LANES), F32)),
        grid=(nt,),
        in_specs=[pl.BlockSpec((OPROJ_TILE // TQ,) + a.shape[1:], lambda i: (i, 0, 0)),
                  pl.BlockSpec((OPROJ_TILE // TQ,) + m.shape[1:], lambda i: (i, 0, 0)),
                  pl.BlockSpec((OPROJ_TILE, D), row), pl.BlockSpec(wo.shape, full),
                  pl.BlockSpec(g1.shape, full), pl.BlockSpec(b1.shape, full),
                  pl.BlockSpec(wr.shape, full), pl.BlockSpec(br.shape, full)],
        out_specs=(pl.BlockSpec((OPROJ_TILE, D), row), pl.BlockSpec((OPROJ_TILE * PACKED_SUB, LANES), row),
                   pl.BlockSpec((SUBLANES, OPROJ_TILE), lambda i: (0, i)),
                   pl.BlockSpec((OPROJ_TILE, LANES), row), pl.BlockSpec((br.shape[0], LANES), full)),
        scratch_shapes=[pltpu.VMEM((br.shape[0], LANES), F32)],
        compiler_params=pltpu.CompilerParams(dimension_semantics=("arbitrary",),
                                             vmem_limit_bytes=VMEM_LIMIT_BYTES),
        name="oproj_router",
    )(a, m, x2, wo, g1, b1, wr, br)


def _dispatch_kernel(dest_ref, fill_ref, x1_ref, xs_ref, zero_scr, sem, fill_sem):
    i = pl.program_id(0)
    fill_sub = MOE_HALF * PACKED_SUB
    nfill = xs_ref.shape[0] // fill_sub

    def fill_copy(blk, on):
        return pltpu.make_async_copy(zero_scr, xs_ref.at[pl.ds(pl.multiple_of(blk * fill_sub, fill_sub), fill_sub)], on)

    def for_blocks(kind, action):
        def body(blk, _):
            @pl.when(fill_ref[blk] == kind)
            def _():
                action(blk)
            return 0

        lax.fori_loop(0, nfill, body, 0)

    @pl.when(i == 0)
    def _():
        zero_scr[...] = jnp.zeros_like(zero_scr)
        for_blocks(FILL_TAIL, lambda blk: fill_copy(blk, sem).start())
        for_blocks(FILL_FREE, lambda blk: fill_copy(blk, fill_sem).start())
        for_blocks(FILL_TAIL, lambda blk: fill_copy(blk, sem).wait())

    base = i * DISPATCH_TILE
    n_tok = pl.num_programs(0) * DISPATCH_TILE

    def row_copy(t, k):
        d = dest_ref[k * n_tok + base + t]
        return pltpu.make_async_copy(x1_ref.at[pl.ds(pl.multiple_of(t * PACKED_SUB, PACKED_SUB), PACKED_SUB)],
                                     xs_ref.at[pl.ds(pl.multiple_of(d * PACKED_SUB, PACKED_SUB), PACKED_SUB)], sem)

    def start_rows(t, _):
        for k in range(MOE_TOPK):
            row_copy(t, k).start(priority=k % DMA_PRIORITIES)
        return 0

    lax.fori_loop(0, DISPATCH_TILE, start_rows, 0, unroll=ISSUE_UNROLL)
    for k in range(MOE_TOPK):
        pltpu.make_async_copy(x1_ref, xs_ref.at[pl.ds(0, DISPATCH_TILE * PACKED_SUB)], sem).wait()

    @pl.when(i == pl.num_programs(0) - 1)
    def _():
        for_blocks(FILL_FREE, lambda blk: fill_copy(blk, fill_sem).wait())


def _dispatch(dest, fill, x1t, n_rows):
    return pl.pallas_call(
        _dispatch_kernel,
        out_shape=jax.ShapeDtypeStruct((n_rows * PACKED_SUB, LANES), x1t.dtype),
        grid_spec=pltpu.PrefetchScalarGridSpec(
            num_scalar_prefetch=2,
            grid=(x1t.shape[0] // (DISPATCH_TILE * PACKED_SUB),),
            in_specs=[pl.BlockSpec((DISPATCH_TILE * PACKED_SUB, LANES), lambda i, d, t: (i, 0))],
            out_specs=pl.BlockSpec(memory_space=pl.ANY),
            scratch_shapes=[pltpu.VMEM((MOE_HALF * PACKED_SUB, LANES), x1t.dtype), pltpu.SemaphoreType.DMA(()),
                            pltpu.SemaphoreType.DMA(())],
        ),
        compiler_params=pltpu.CompilerParams(dimension_semantics=("arbitrary",),
                                             vmem_limit_bytes=VMEM_LIMIT_BYTES),
        name="dispatch",
    )(dest, fill, x1t)


def _experts_kernel(be_ref, slot_ref, nxt_ref, full_ref, nused_ref, x_ref, wg_hbm, bg_ref, wu_hbm, bu_ref, wd_hbm, bd_ref,
                    y_ref, wg_st, wu_st, wd_st, wg_bf, wu_bf, wd_bf, sems):
    i = pl.program_id(0)
    prev = be_ref[jnp.maximum(i - 1, 0)]
    changed = (i == 0) | (be_ref[i] != prev)
    active = i < nused_ref[0]
    slot = slot_ref[i]

    def weight_copies(expert, s):
        return (pltpu.make_async_copy(wg_hbm.at[expert], wg_st.at[s], sems.at[s, 0]),
                pltpu.make_async_copy(wu_hbm.at[expert], wu_st.at[s], sems.at[s, 1]),
                pltpu.make_async_copy(wd_hbm.at[expert], wd_st.at[s], sems.at[s, 2]))

    @pl.when(i == 0)
    def _():
        for cp in weight_copies(be_ref[0], slot):
            cp.start()

    @pl.when(active & changed)
    def _():
        for cp in weight_copies(be_ref[i], slot):
            cp.wait()

        @pl.when(nxt_ref[i] >= 0)
        def _():
            for cp in weight_copies(nxt_ref[i], 1 - slot):
                cp.start(priority=WEIGHT_DMA_PRIORITY)

        wg_bf[...] = wg_st[slot].astype(BF16)
        wu_bf[...] = wu_st[slot].astype(BF16)
        wd_bf[...] = wd_st[slot].astype(BF16)

    def mlp(nrows):
        rows = pl.ds(0, nrows * PACKED_SUB)
        xb = _unpack_tiles_to_rows(x_ref.at[rows], nrows)
        g = jnp.minimum(_dot(xb, wg_bf[...]) + bg_ref[0], SWIGLU_LIMIT)
        u = jnp.clip(_dot(xb, wu_bf[...]) + bu_ref[0], -SWIGLU_LIMIT, SWIGLU_LIMIT)
        h = g * (1.0 / (1.0 + jnp.exp(-SWIGLU_ALPHA * g))) * (u + 1.0)
        _pack_rows_to_tiles(y_ref.at[rows], _dot(h.astype(BF16), wd_bf[...]) + bd_ref[0])

    whole = active & (full_ref[i] > 0)

    @pl.when(whole)
    def _():
        mlp(MOE_GROUP)

    @pl.when(active & jnp.logical_not(whole))
    def _():
        mlp(MOE_HALF)

    @pl.when(jnp.logical_not(active))
    def _():
        y_ref[pl.ds(0, MOE_HALF * PACKED_SUB), :] = jnp.zeros((MOE_HALF * PACKED_SUB, LANES), y_ref.dtype)

    @pl.when(jnp.logical_not(whole))
    def _():
        y_ref[pl.ds(MOE_HALF * PACKED_SUB, MOE_HALF * PACKED_SUB), :] = jnp.zeros((MOE_HALF * PACKED_SUB, LANES), y_ref.dtype)


def _experts(blk_expert, blk_slot, blk_next, blk_full, n_used, xs, wg, bg, wu, bu, wd, bd):
    E, D, F = wg.shape
    nblk = xs.shape[0] // (MOE_GROUP * PACKED_SUB)

    def rowmap(i, be, sl, nx, fl, nu):
        return (jnp.minimum(i, nu[0] - 1), 0)

    def bmap(i, be, sl, nx, fl, nu):
        return (be[i], 0, 0)

    hbm = pl.BlockSpec(memory_space=pl.ANY)
    return pl.pallas_call(
        _experts_kernel,
        out_shape=jax.ShapeDtypeStruct(xs.shape, xs.dtype),
        grid_spec=pltpu.PrefetchScalarGridSpec(
            num_scalar_prefetch=5,
            grid=(nblk,),
            in_specs=[pl.BlockSpec((MOE_GROUP * PACKED_SUB, LANES), rowmap),
                      hbm, pl.BlockSpec((1, 1, F), bmap),
                      hbm, pl.BlockSpec((1, 1, F), bmap),
                      hbm, pl.BlockSpec((1, 1, D), bmap)],
            out_specs=pl.BlockSpec((MOE_GROUP * PACKED_SUB, LANES), lambda i, be, sl, nx, fl, nu: (i, 0)),
            scratch_shapes=[pltpu.VMEM((2, D, F), F32), pltpu.VMEM((2, D, F), F32), pltpu.VMEM((2, F, D), F32),
                            pltpu.VMEM((D, F), BF16), pltpu.VMEM((D, F), BF16), pltpu.VMEM((F, D), BF16),
                            pltpu.SemaphoreType.DMA((2, 3))],
        ),
        compiler_params=pltpu.CompilerParams(dimension_semantics=("arbitrary",),
                                             vmem_limit_bytes=VMEM_LIMIT_BYTES),
        name="experts",
    )(blk_expert, blk_slot, blk_next, blk_full, n_used, xs, wg, bg, wu, bu, wd, bd)


def _combine_kernel(alpha, dest_ref, gates_ref, x1_ref, g_ref, b_ref, y_ref, o_ref, ybuf, sems):
    i = pl.program_id(0)
    buf = i % 2

    n_tok = pl.num_programs(0) * COMBINE_TILE

    def gather_tile(tile, into):
        base = tile * COMBINE_TILE

        def start_rows(t, _):
            for k in range(MOE_TOPK):
                d = dest_ref[k * n_tok + base + t]
                pltpu.make_async_copy(y_ref.at[pl.ds(pl.multiple_of(d * PACKED_SUB, PACKED_SUB), PACKED_SUB)],
                                      ybuf.at[into, k, pl.ds(pl.multiple_of(t * PACKED_SUB, PACKED_SUB), PACKED_SUB)],
                                      sems.at[into]).start(priority=k % DMA_PRIORITIES)
            return 0

        lax.fori_loop(0, COMBINE_TILE, start_rows, 0, unroll=ISSUE_UNROLL)

    @pl.when(i == 0)
    def _():
        gather_tile(0, 0)

    @pl.when(i + 1 < pl.num_programs(0))
    def _():
        gather_tile(i + 1, 1 - buf)

    for k in range(MOE_TOPK):
        pltpu.make_async_copy(y_ref.at[pl.ds(0, COMBINE_TILE * PACKED_SUB)], ybuf.at[buf, k], sems.at[buf]).wait()

    gates = gates_ref[...]
    ffn = gates[:, 0:1] * _unpack_tiles_to_rows(ybuf.at[buf, 0], COMBINE_TILE).astype(F32)
    for k in range(1, MOE_TOPK):
        ffn = ffn + gates[:, k:k + 1] * _unpack_tiles_to_rows(ybuf.at[buf, k], COMBINE_TILE).astype(F32)
    o_ref[...] = _layer_norm(alpha * x1_ref[...] + ffn, g_ref[...], b_ref[...])


def _combine(dest, gates, x1, g2, b2, y_rows, alpha):
    N, D = x1.shape
    row = lambda i, d: (i, 0)
    full = lambda i, d: (0, 0)
    return pl.pallas_call(
        functools.partial(_combine_kernel, alpha),
        out_shape=jax.ShapeDtypeStruct((N, D), F32),
        grid_spec=pltpu.PrefetchScalarGridSpec(
            num_scalar_prefetch=1,
            grid=(N // COMBINE_TILE,),
            in_specs=[pl.BlockSpec((COMBINE_TILE, LANES), row), pl.BlockSpec((COMBINE_TILE, D), row),
                      pl.BlockSpec(g2.shape, full), pl.BlockSpec(b2.shape, full),
                      pl.BlockSpec(memory_space=pl.ANY)],
            out_specs=pl.BlockSpec((COMBINE_TILE, D), row),
            scratch_shapes=[pltpu.VMEM((2, MOE_TOPK, COMBINE_TILE * PACKED_SUB, LANES), y_rows.dtype),
                            pltpu.SemaphoreType.DMA((2,))],
        ),
        compiler_params=pltpu.CompilerParams(dimension_semantics=("arbitrary",),
                                             vmem_limit_bytes=VMEM_LIMIT_BYTES),
        name="combine",
    )(dest, gates, x1, g2, b2, y_rows)


def _rot_partner(w, half):
    return jnp.concatenate([-w[..., half:2 * half], w[..., :half]], axis=-1)


def _layer_weights(w_in, w_q_b, w_kv_b):
    D = w_in.shape[0]
    o1 = Q_LORA
    o2 = o1 + KV_LORA
    o3 = o2 + MLA_ROPE
    w_ql, w_kvl, w_kr = w_in[:, :o1], w_in[:, o1:o2], w_in[:, o2:o3]
    w_mq, w_mk, w_mv = w_in[:, o3:o3 + _MH], w_in[:, o3 + _MH:o3 + 2 * _MH], w_in[:, o3 + 2 * _MH:]
    zpad = lambda n: jnp.zeros((D, n), w_in.dtype)
    tail = SLOT - MLA_NOPE - MLA_ROPE
    assert tail == MLA_ROPE
    kr_slot = jnp.concatenate([zpad(MLA_NOPE), w_kr, _rot_partner(w_kr, MLA_ROPE // 2)], axis=1)

    win = jnp.concatenate([w_ql, w_kvl, kr_slot, w_mk], axis=1).astype(BF16)
    wmt = jnp.concatenate([w_mq, w_mv], axis=1).T.astype(BF16)

    wkv3 = w_kv_b.reshape(KV_LORA, MLA_HEADS, MLA_NOPE + MLA_V)
    wk = wkv3[..., :MLA_NOPE].reshape(KV_LORA, MLA_HEADS * MLA_NOPE).astype(BF16)
    wv = wkv3[..., MLA_NOPE:].reshape(KV_LORA, MLA_HEADS * MLA_V)
    return win, wmt, w_q_b.T.astype(BF16), wk, wv.T.astype(BF16)


def _rope_tables(positions):
    pos = positions.astype(F32).reshape(1, -1)

    def cs(d_rot):
        inv_freq = ROPE_THETA ** (-jnp.arange(0, d_rot, 2, dtype=F32) / d_rot)
        ang = inv_freq.reshape(d_rot // 2, 1) * pos
        return jnp.cos(ang), jnp.sin(ang)

    return cs(MLA_ROPE) + cs(MOBA_ROT)


def kernel(x, positions, w_in, q_a_norm, w_q_b, kv_a_norm, w_kv_b, w_o, ln1_g, ln1_b, w_router, b_router,
           w_gate, b_gate, w_up, b_up, w_down, b_down, ln2_g, ln2_b):
    B, T, D = x.shape
    depth = w_in.shape[0]
    alpha = (2.0 * depth) ** 0.25
    N = B * T
    assert T % MOBA_BLOCK == 0 and T // MOBA_BLOCK <= SUBLANES and N % OPROJ_TILE == 0
    assert D == SUBLANES * LANES
    n_asg = N * MOE_TOPK
    n_rows = n_asg + N_EXPERTS * MOE_GROUP
    tables = _rope_tables(positions)
    h = x.reshape(N, D)
    for l in range(depth):
        win, wmt, wqt, wk, wvt = _layer_weights(w_in[l], w_q_b[l], w_kv_b[l])
        qa, ka, va, mq, mk, mv = _prep(h, win, wmt, wqt, wk, wvt, q_a_norm[l].reshape(1, -1),
                                       kv_a_norm[l].reshape(1, -1), tables, B, T)
        a = _attention(qa, ka, va, B, T, "mla_attention")
        m = _attention(mq, mk, mv, B, T, "moba_attention")
        wr_pad = jnp.concatenate([w_router[l], jnp.zeros((D, LANES - N_EXPERTS), F32)], axis=1)
        wr_hi = wr_pad.astype(BF16)
        wr = jnp.concatenate([wr_hi, (wr_pad - wr_hi.astype(F32)).astype(BF16)], axis=1)
        br = b_router[l].reshape(N_EXPERTS, 1)
        x1, x1t, route, gates, cnt = _oproj(a, m, h, w_o[l].astype(BF16), ln1_g[l].reshape(1, D), ln1_b[l].reshape(1, D),
                                       wr, br, alpha)
        er = jnp.arange(N_EXPERTS, dtype=jnp.int32)
        counts = cnt[:, 0].astype(jnp.int32)
        padded = (counts + MOE_GROUP - 1) // MOE_GROUP * MOE_GROUP
        upto = er[None, :] <= er[:, None]
        pad_end = jnp.sum(jnp.where(upto, padded[None, :], 0), axis=1)
        pad_start = pad_end - padded
        e_idx = route[:MOE_TOPK]
        group_start = jnp.sum(jnp.where(e_idx[..., None] == er, pad_start, 0), axis=-1)
        dest = (group_start + route[MOE_TOPK:2 * MOE_TOPK]).reshape(n_asg)
        n_used = (pad_end[-1:] // MOE_GROUP).astype(jnp.int32)
        nblk = n_rows // MOE_GROUP
        blk_start = jnp.arange(nblk, dtype=jnp.int32) * MOE_GROUP
        blk_expert = jnp.minimum(jnp.sum((blk_start[:, None] >= pad_end[None, :]).astype(jnp.int32), axis=1),
                                 N_EXPERTS - 1)
        of_expert = blk_expert[:, None] == er[None, :]
        pick = lambda per_expert: jnp.sum(jnp.where(of_expert, per_expert[None, :], 0), axis=1)
        nonempty = padded > 0
        group_index = jnp.sum(jnp.where(upto & nonempty[None, :], 1, 0), axis=1) - 1
        later = (er[None, :] > er[:, None]) & nonempty[None, :]
        next_expert = jnp.min(jnp.where(later, er[None, :], N_EXPERTS), axis=1)
        next_expert = jnp.where(next_expert == N_EXPERTS, -1, next_expert)
        blk_slot = pick(group_index) % 2
        blk_next = pick(next_expert)
        blk_full = (pick(pad_start + counts) > blk_start + MOE_HALF).astype(jnp.int32)
        half_start = jnp.arange(n_rows // MOE_HALF, dtype=jnp.int32) * MOE_HALF
        half_expert = jnp.minimum(jnp.sum((half_start[:, None] >= pad_end[None, :]).astype(jnp.int32), axis=1),
                                  N_EXPERTS - 1)
        half_real_end = jnp.sum(jnp.where(half_expert[:, None] == er[None, :], (pad_start + counts)[None, :], 0), axis=1)
        no_real_rows = (half_start >= pad_end[-1]) | (half_start >= half_real_end)
        fill = jnp.where(no_real_rows, FILL_FREE,
                         jnp.where(half_start + MOE_HALF > half_real_end, FILL_TAIL, 0)).astype(jnp.int32)
        xs = _dispatch(dest, fill, x1t, n_rows)
        y_rows = _experts(blk_expert, blk_slot.astype(jnp.int32), blk_next, blk_full, n_used, xs, w_gate[l], b_gate[l].reshape(N_EXPERTS, 1, -1),
                          w_up[l], b_up[l].reshape(N_EXPERTS, 1, -1), w_down[l], b_down[l].reshape(N_EXPERTS, 1, -1))
        h = _combine(dest, gates, x1, ln2_g[l].reshape(1, D), ln2_b[l].reshape(1, D), y_rows, alpha)
    return h.reshape(B, T, D)
```

```python
import functools
import math

import jax
import jax.numpy as jnp
from jax import lax
from jax.experimental import pallas as pl
from jax.experimental.pallas import tpu as pltpu

ROPE_THETA = 500000.0
MLA_HEADS = 8
MLA_NOPE = 64
MLA_ROPE = 32
MLA_V = 64
Q_LORA = 256
KV_LORA = 128
MOBA_HEADS = 8
MOBA_HD = 64
MOBA_ROT = MOBA_HD // 4
MOBA_BLOCK = 256
MOBA_TOPK = 3
N_EXPERTS = 32
MOE_TOPK = 4
SWIGLU_LIMIT = 7.0
SWIGLU_ALPHA = 1.702
RMS_EPS = 1e-6
LN_EPS = 1e-5

LANES = 128
SUBLANES = 8
VMEM_LIMIT_BYTES = 56 * 1024 * 1024

SLOT = LANES
TQ = MOBA_BLOCK
ROW_TILE = 256
OPROJ_TILE = 1024
MOE_GROUP = 512
MOE_HALF = MOE_GROUP // 2
PACKED_SUB = SUBLANES // 2
FILL_TAIL, FILL_FREE = 1, 2
DISPATCH_TILE = 1024
COMBINE_TILE = 512
ISSUE_UNROLL = 4
DMA_PRIORITIES = 2
WEIGHT_DMA_PRIORITY = 1
HEAD_LANES = 64
ONES_LANE = 64
BIAS_LANE = 64
ATTN_HEADS_PER_STEP = 8
LOG2E = math.log2(math.e)
NEG_BIG = -(2.0 ** 100)

F32 = jnp.float32
BF16 = jnp.bfloat16
NT_DIMS = (((1,), (1,)), ((), ()))
TN_DIMS = (((0,), (0,)), ((), ()))


def _dot(a, b, precision=None):
    return jnp.dot(a, b, preferred_element_type=F32, precision=precision)


def _dot_nt(a, b, precision=None):
    return lax.dot_general(a, b, NT_DIMS, preferred_element_type=F32, precision=precision)


def _pack_rows_to_tiles(ref, x):
    rows, width = x.shape
    bits = lambda v: lax.bitcast_convert_type(v.astype(BF16).astype(F32), jnp.uint32)
    words = (bits(x[:, :width // 2]) >> 16) | (bits(x[:, width // 2:]) & jnp.uint32(0xFFFF0000))
    for c in range(PACKED_SUB):
        ref[pl.ds(c, rows, stride=PACKED_SUB), :] = words[:, c * LANES:(c + 1) * LANES]


def _unpack_tiles_to_rows(ref, rows):
    words = jnp.concatenate([ref[pl.ds(c, rows, stride=PACKED_SUB), :] for c in range(PACKED_SUB)], axis=1)
    low = lax.bitcast_convert_type(words << 16, F32).astype(BF16)
    high = lax.bitcast_convert_type(words & jnp.uint32(0xFFFF0000), F32).astype(BF16)
    return jnp.concatenate([low, high], axis=1)


def _rms(x, g):
    return x * lax.rsqrt(jnp.mean(x * x, axis=-1, keepdims=True) + RMS_EPS) * g


def _layer_norm(x, g, b):
    mu = jnp.mean(x, axis=-1, keepdims=True)
    xc = x - mu
    var = jnp.mean(xc * xc, axis=-1, keepdims=True)
    return xc * lax.rsqrt(var + LN_EPS) * g + b


_C_QL = 0
_C_KVL = _C_QL + Q_LORA
_C_KR = _C_KVL + KV_LORA
_C_MK = _C_KR + SLOT
_MH = MOBA_HEADS * MOBA_HD
_C_END = _C_MK + _MH


def _prep_kernel(x_ref, win_ref, wmt_ref, wqt_ref, wk_ref, wvt_ref, qg_ref, kvg_ref,
                 cat_ref, sat_ref, cbt_ref, sbt_ref,
                 qat_ref, ka_ref, vat_ref, mqt_ref, mk_ref, mvt_ref, kmean_scr):
    c = pl.program_id(1)

    @pl.when(c == 0)
    def _():
        kmean_scr[...] = jnp.zeros_like(kmean_scr)

    xb = x_ref[...].astype(BF16)
    both_halves = lambda t: jnp.concatenate([t, t], axis=0)
    cat = both_halves(cat_ref[...])
    sat = both_halves(sat_ref[...])
    cbt = both_halves(cbt_ref[...])
    sat_signed = jnp.concatenate([-sat_ref[...], sat_ref[...]], axis=0)
    sbt_signed = jnp.concatenate([-sbt_ref[...], sbt_ref[...]], axis=0)

    def rotated_t(r, cos, sin_signed):
        half = r.shape[0] // 2
        return r * cos + jnp.concatenate([r[half:], r[:half]], axis=0) * sin_signed

    const = lambda v, n: jnp.full((n, ROW_TILE), v, F32)
    tail = SLOT - MLA_NOPE - MLA_ROPE
    rest = MOBA_HD - MOBA_ROT
    ca = jnp.concatenate([const(1.0, MLA_NOPE), cat, const(0.0, tail)], axis=0).T
    sa = jnp.concatenate([const(0.0, MLA_NOPE), sat, const(0.0, tail)], axis=0).T
    cb = jnp.concatenate([cbt, const(1.0, rest), cbt, const(1.0, rest)], axis=0).T
    sb = jnp.concatenate([sbt_signed, const(0.0, rest), sbt_signed, const(0.0, rest)], axis=0).T
    lane = lax.broadcasted_iota(jnp.int32, (ROW_TILE, SLOT), 1)
    head_lanes = lane < HEAD_LANES
    first_half = (lane % MOBA_HD) < MOBA_ROT // 2
    ones_rows = (lax.broadcasted_iota(jnp.int32, (SLOT - HEAD_LANES, ROW_TILE), 0) == ONES_LANE - HEAD_LANES).astype(F32)

    ql = _dot(xb, win_ref[:, _C_QL:_C_KVL])
    kvl = _dot(xb, win_ref[:, _C_KVL:_C_KR])
    kr = _dot(xb, win_ref[:, _C_KR:_C_MK])
    krs = pltpu.roll(kr, SLOT - MLA_ROPE, axis=1)
    qn = _rms(ql, qg_ref[...]).astype(BF16)
    kvn = _rms(kvl, kvg_ref[...]).astype(BF16)
    q_t = _dot_nt(wqt_ref[...], qn)
    kn = _dot(kvn, wk_ref[...])
    v_t = _dot_nt(wvt_ref[...], kvn)
    scale_a = LOG2E / math.sqrt(MLA_NOPE + MLA_ROPE)
    kro = kr * ca + krs * sa
    qd = MLA_NOPE + MLA_ROPE
    q_pad = jnp.zeros((SLOT - qd, ROW_TILE), F32)
    for h in range(MLA_HEADS):
        sl = slice(h * SLOT, (h + 1) * SLOT)
        q_rope = rotated_t(q_t[h * qd + MLA_NOPE:(h + 1) * qd], cat, sat_signed)
        q_slot = jnp.concatenate([q_t[h * qd:h * qd + MLA_NOPE], q_rope, q_pad], axis=0)
        qat_ref[0, sl, :] = (q_slot * scale_a).astype(BF16)
        ka_ref[:, sl] = (kn[:, sl] + kro).astype(BF16)
        vat_ref[0, sl, :] = jnp.concatenate([v_t[h * MLA_V:(h + 1) * MLA_V], ones_rows], axis=0).astype(BF16)

    mk = _dot(xb, win_ref[:, _C_MK:_C_END])
    npair = _MH // LANES
    pair = lambda a, j: a[:, j * LANES:(j + 1) * LANES]

    def rotated(a):
        x2 = pltpu.roll(a, LANES - MOBA_ROT // 2, axis=1)
        x1 = pltpu.roll(a, MOBA_ROT // 2, axis=1)
        return a * cb + jnp.where(first_half, x2, x1) * sb

    mk_rot = [rotated(pair(mk, j)) for j in range(npair)]
    mk_all = jnp.concatenate(mk_rot, axis=1)

    mq_t = _dot_nt(wmt_ref[0:_MH, :], xb)
    mv_t = _dot_nt(wmt_ref[_MH:2 * _MH, :], xb)
    head = lambda a, h: a[h * MOBA_HD:(h + 1) * MOBA_HD]
    mq_rot_t = [jnp.concatenate([rotated_t(head(mq_t, h)[0:MOBA_ROT], cbt, sbt_signed),
                                 head(mq_t, h)[MOBA_ROT:]], axis=0) for h in range(MOBA_HEADS)]

    nrow = MOBA_HEADS * SUBLANES
    row_i = lax.broadcasted_iota(jnp.int32, (nrow, _MH), 0)
    lane_i = lax.broadcasted_iota(jnp.int32, (nrow, _MH), 1)
    kmean_c = jnp.mean(mk_all, axis=0, keepdims=True)
    put = ((row_i % SUBLANES) == c) & ((lane_i // MOBA_HD) == (row_i // SUBLANES))
    table = kmean_scr[...]
    split = lambda a: (a.astype(BF16), (a - a.astype(BF16).astype(F32)).astype(BF16))
    t_hi, t_lo = split(table)
    q_hi, q_lo = split(jnp.concatenate(mq_rot_t, axis=0))
    gate_t = _dot(jnp.concatenate([t_hi, t_lo], axis=0), q_hi)
    gate_t = gate_t[0:nrow] + gate_t[nrow:] + _dot(t_hi, q_lo)
    kmean_scr[...] = jnp.where(put, jnp.broadcast_to(kmean_c, (nrow, _MH)), table)

    n_idx = lax.broadcasted_iota(jnp.int32, (SUBLANES, ROW_TILE), 0)
    valid = n_idx < c
    scale_b = LOG2E / math.sqrt(MOBA_HD)
    zero_rows = jnp.zeros((SLOT - HEAD_LANES - SUBLANES, ROW_TILE), F32)
    block_onehot = (lane == BIAS_LANE + c).astype(F32)
    for h in range(MOBA_HEADS):
        g = jnp.where(valid, gate_t[h * SUBLANES:(h + 1) * SUBLANES, :], -jnp.inf)
        rank = jnp.zeros((SUBLANES, ROW_TILE), jnp.int32)
        for k in range(1, SUBLANES):
            other = pltpu.roll(g, k, axis=0)
            other_n = pltpu.roll(n_idx, k, axis=0)
            beats = (other > g) | ((other == g) & (other_n < n_idx))
            rank = rank + beats.astype(jnp.int32)
        keep = (valid & (rank < MOBA_TOPK)) | (n_idx == c)
        bias = jnp.where(keep, 0.0, NEG_BIG)
        sl = slice(h * SLOT, (h + 1) * SLOT)
        mqt_ref[0, sl, :] = jnp.concatenate([mq_rot_t[h] * scale_b, bias, zero_rows], axis=0).astype(BF16)
        mvt_ref[0, sl, :] = jnp.concatenate([head(mv_t, h), ones_rows], axis=0).astype(BF16)
        j, hh = divmod(h, 2)
        k_h = mk_rot[j] if hh == 0 else pltpu.roll(mk_rot[j], HEAD_LANES, axis=1)
        mk_ref[:, sl] = jnp.where(head_lanes, k_h, block_onehot).astype(BF16)


def _prep(x2, win, wmt, wqt, wk, wvt, qg, kvg, tables, B, T):
    N, D = x2.shape
    nt = T // ROW_TILE
    row = lambda b, c: (b * nt + c, 0)
    col = lambda b, c: (0, b * nt + c)
    full = lambda b, c: (0, 0)
    cat, sat, cbt, sbt = tables
    width = MLA_HEADS * SLOT

    def rows(w):
        return pl.BlockSpec((ROW_TILE, w), row)

    def cols(a):
        return pl.BlockSpec((a.shape[0], ROW_TILE), col)

    def whole(a):
        return pl.BlockSpec(a.shape, full)

    rowmajor = jax.ShapeDtypeStruct((N, width), BF16)
    transposed = jax.ShapeDtypeStruct((B * nt, width, ROW_TILE), BF16)
    t_spec = pl.BlockSpec((1, width, ROW_TILE), lambda b, c: (b * nt + c, 0, 0))
    return pl.pallas_call(
        _prep_kernel,
        out_shape=(transposed, rowmajor, transposed, transposed, rowmajor, transposed),
        grid=(B, nt),
        in_specs=[rows(D), whole(win), whole(wmt), whole(wqt), whole(wk), whole(wvt), whole(qg), whole(kvg),
                  cols(cat), cols(sat), cols(cbt), cols(sbt)],
        out_specs=(t_spec, rows(width), t_spec, t_spec, rows(width), t_spec),
        scratch_shapes=[pltpu.VMEM((MOBA_HEADS * SUBLANES, _MH), F32)],
        compiler_params=pltpu.CompilerParams(dimension_semantics=("arbitrary", "arbitrary"),
                                             vmem_limit_bytes=VMEM_LIMIT_BYTES),
        name="prep",
    )(x2, win, wmt, wqt, wk, wvt, qg, kvg, cat, sat, cbt, sbt)


def _attn_kernel(qt_ref, k_ref, vt_ref, o_ref, s_scr, acc_scr):
    i = pl.program_id(2)
    nh = ATTN_HEADS_PER_STEP
    slot = lambda h: slice(h * SLOT, (h + 1) * SLOT)
    nblk = i + 1
    nquad = nblk // 4
    pair_end = 4 * nquad + 2 * ((nblk - 4 * nquad) // 2)
    q_pos = i * TQ + lax.broadcasted_iota(jnp.int32, (1, TQ), 1)

    def fold(s):
        out = s[0:SUBLANES]
        for t in range(1, s.shape[0] // SUBLANES):
            out = jnp.maximum(out, s[t * SUBLANES:(t + 1) * SUBLANES])
        return out

    def score_blocks(j, n, mrun):
        j0 = pl.multiple_of(j * TQ, TQ)
        visible = (j0 + lax.broadcasted_iota(jnp.int32, (n * TQ, TQ), 0)) <= q_pos
        out = []
        for h in range(nh):
            s = _dot(k_ref[pl.ds(j0, n * TQ), slot(h)], qt_ref[0, slot(h), :])
            s = jnp.where(visible, s, -jnp.inf)
            s_scr[h, pl.ds(j, n)] = s.reshape(n, TQ, TQ)
            out.append(jnp.maximum(mrun[h], fold(s)))
        return tuple(out)

    mrun = tuple(jnp.full((SUBLANES, TQ), -jnp.inf, F32) for _ in range(nh))
    mrun = lax.fori_loop(0, nquad, lambda jq, m: score_blocks(4 * jq, 4, m), mrun)
    mrun = lax.fori_loop(2 * nquad, pair_end // 2, lambda jj, m: score_blocks(2 * jj, 2, m), mrun)
    mrun = lax.fori_loop(pair_end, nblk, lambda j, m: score_blocks(j, 1, m), mrun)
    ms = [jnp.max(m, axis=0, keepdims=True) for m in mrun]

    def accumulate(j, n, carry):
        for h in range(nh):
            p = jnp.exp2(s_scr[h, pl.ds(j, n)].reshape(n * TQ, TQ) - ms[h]).astype(BF16)
            vt = jnp.concatenate([vt_ref[j + t, slot(h), :] for t in range(n)], axis=1)
            acc_scr[h] += _dot(vt, p)
        return carry

    acc_scr[...] = jnp.zeros_like(acc_scr)
    lax.fori_loop(0, nquad, lambda jq, c: accumulate(4 * jq, 4, c), 0)
    lax.fori_loop(2 * nquad, pair_end // 2, lambda jj, c: accumulate(2 * jj, 2, c), 0)
    lax.fori_loop(pair_end, nblk, lambda j, c: accumulate(j, 1, c), 0)
    outs = [acc_scr[h, 0:HEAD_LANES] / acc_scr[h, ONES_LANE:ONES_LANE + 1] for h in range(nh)]
    for h in range(nh):
        o_ref[0, h * HEAD_LANES:(h + 1) * HEAD_LANES, :] = outs[h].astype(o_ref.dtype)


def _attention(qt, k, vt, B, T, name):
    N = k.shape[0]
    nq = T // TQ
    nh = ATTN_HEADS_PER_STEP
    heads = k.shape[1] // SLOT
    return pl.pallas_call(
        _attn_kernel,
        out_shape=jax.ShapeDtypeStruct((B * nq, heads * HEAD_LANES, TQ), BF16),
        grid=(B, heads // nh, nq),
        in_specs=[pl.BlockSpec((1, nh * SLOT, TQ), lambda b, g, i: (b * nq + i, g, 0)),
                  pl.BlockSpec((T, nh * SLOT), lambda b, g, i: (b, g)),
                  pl.BlockSpec((nq, nh * SLOT, TQ), lambda b, g, i: (b, g, 0))],
        out_specs=pl.BlockSpec((1, nh * HEAD_LANES, TQ), lambda b, g, i: (b * nq + i, g, 0)),
        scratch_shapes=[pltpu.VMEM((nh, nq, TQ, TQ), F32), pltpu.VMEM((nh, SLOT, TQ), F32)],
        compiler_params=pltpu.CompilerParams(dimension_semantics=("arbitrary", "arbitrary", "arbitrary"),
                                             vmem_limit_bytes=VMEM_LIMIT_BYTES),
        name=name,
    )(qt, k, vt)


def _oproj_kernel(alpha, a_ref, m_ref, x_ref, wo_ref, g_ref, b_ref, wr_ref, br_ref,
                  x1_ref, x1t_ref, route_ref, gates_ref, cnt_ref, carry_scr):
    i = pl.program_id(0)

    @pl.when(i == 0)
    def _():
        carry_scr[...] = jnp.zeros_like(carry_scr)

    parts = []
    for t in range(a_ref.shape[0]):
        lhs_t = jnp.concatenate([a_ref[t], m_ref[t]], axis=0)
        parts.append(lax.dot_general(lhs_t, wo_ref[...], TN_DIMS, preferred_element_type=F32))
    mix = jnp.concatenate(parts, axis=0)
    x1 = _layer_norm(alpha * x_ref[...] + mix, g_ref[...], b_ref[...])
    x1_ref[...] = x1
    _pack_rows_to_tiles(x1t_ref, x1)

    ne = br_ref.shape[0]
    rows = x1.shape[0]
    x_hi = x1.astype(BF16)
    x_lo = (x1 - x_hi.astype(F32)).astype(BF16)
    both = _dot(x_hi, wr_ref[...])
    logits_rm = both[:, :LANES] + both[:, LANES:] + _dot(x_lo, wr_ref[:, :LANES])
    logits = logits_rm.T[0:ne] + br_ref[...]
    expert = lax.broadcasted_iota(jnp.int32, logits.shape, 0)
    expert_f = expert.astype(F32)
    vals, idxs = [], []
    work = logits
    for _ in range(MOE_TOPK):
        mx = jnp.max(work, axis=0, keepdims=True)
        ix = jnp.min(jnp.where(work == mx, expert_f, float(ne)), axis=0, keepdims=True).astype(jnp.int32)
        vals.append(mx)
        idxs.append(ix)
        work = jnp.where(expert == ix, -jnp.inf, work)
    exps = [jnp.exp(v - vals[0]) for v in vals]
    den = exps[0]
    for e in exps[1:]:
        den = den + e

    onehot = jnp.zeros(logits.shape, F32)
    for ix in idxs:
        onehot = onehot + (expert == ix).astype(F32)
    r = lax.broadcasted_iota(jnp.int32, (rows, rows), 0)
    cidx = lax.broadcasted_iota(jnp.int32, (rows, rows), 1)
    earlier = (r < cidx).astype(BF16)
    carry = carry_scr[:, 0:1]
    before = _dot(onehot.astype(BF16), earlier) + carry
    row8 = lax.broadcasted_iota(jnp.int32, (SUBLANES, rows), 0)
    route = jnp.zeros((SUBLANES, rows), jnp.int32)
    gates = jnp.zeros((SUBLANES, rows), F32)
    for k in range(MOE_TOPK):
        rank = jnp.sum(jnp.where(expert == idxs[k], before, 0.0), axis=0, keepdims=True).astype(jnp.int32)
        route = jnp.where(row8 == k, idxs[k], route)
        route = jnp.where(row8 == MOE_TOPK + k, rank, route)
        gates = jnp.where(row8 == k, exps[k] / den, gates)
    route_ref[...] = route
    gates_ref[...] = jnp.concatenate([gates, jnp.zeros((LANES - SUBLANES, rows), F32)], axis=0).T
    new_carry = carry + jnp.sum(onehot, axis=1, keepdims=True)
    carry_scr[...] = jnp.broadcast_to(new_carry, carry_scr.shape)
    cnt_ref[...] = jnp.broadcast_to(new_carry, cnt_ref.shape)


def _oproj(a, m, x2, wo, g1, b1, wr, br, alpha):
    N, D = x2.shape
    nt = N // OPROJ_TILE
    row = lambda i: (i, 0)
    full = lambda i: (0, 0)
    return pl.pallas_call(
        functools.partial(_oproj_kernel, alpha),
        out_shape=(jax.ShapeDtypeStruct((N, D), F32),
                   jax.ShapeDtypeStruct((N * PACKED_SUB, LANES), jnp.uint32),
                   jax.ShapeDtypeStruct((SUBLANES, N), jnp.int32),
                   jax.ShapeDtypeStruct((N, LANES), F32),
                   jax.ShapeDtypeStruct((br.shape[0], LANES), F32)),
        grid=(nt,),
        in_specs=[pl.BlockSpec((OPROJ_TILE // TQ,) + a.shape[1:], lambda i: (i, 0, 0)),
                  pl.BlockSpec((OPROJ_TILE // TQ,) + m.shape[1:], lambda i: (i, 0, 0)),
                  pl.BlockSpec((OPROJ_TILE, D), row), pl.BlockSpec(wo.shape, full),
                  pl.BlockSpec(g1.shape, full), pl.BlockSpec(b1.shape, full),
                  pl.BlockSpec(wr.shape, full), pl.BlockSpec(br.shape, full)],
        out_specs=(pl.BlockSpec((OPROJ_TILE, D), row), pl.BlockSpec((OPROJ_TILE * PACKED_SUB, LANES), row),
                   pl.BlockSpec((SUBLANES, OPROJ_TILE), lambda i: (0, i)),
                   pl.BlockSpec((OPROJ_TILE, LANES), row), pl.BlockSpec((br.shape[0], LANES), full)),
        scratch_shapes=[pltpu.VMEM((br.shape[0], LANES), F32)],
        compiler_params=pltpu.CompilerParams(dimension_semantics=("arbitrary",),
                                             vmem_limit_bytes=VMEM_LIMIT_BYTES),
        name="oproj_router",
    )(a, m, x2, wo, g1, b1, wr, br)


def _dispatch_kernel(dest_ref, fill_ref, x1_ref, xs_ref, zero_scr, sem, fill_sem):
    i = pl.program_id(0)
    fill_sub = MOE_HALF * PACKED_SUB
    nfill = xs_ref.shape[0] // fill_sub

    def fill_copy(blk, on):
        return pltpu.make_async_copy(zero_scr, xs_ref.at[pl.ds(pl.multiple_of(blk * fill_sub, fill_sub), fill_sub)], on)

    def for_blocks(kind, action):
        def body(blk, _):
            @pl.when(fill_ref[blk] == kind)
            def _():
                action(blk)
            return 0

        lax.fori_loop(0, nfill, body, 0)

    @pl.when(i == 0)
    def _():
        zero_scr[...] = jnp.zeros_like(zero_scr)
        for_blocks(FILL_TAIL, lambda blk: fill_copy(blk, sem).start())
        for_blocks(FILL_FREE, lambda blk: fill_copy(blk, fill_sem).start())
        for_blocks(FILL_TAIL, lambda blk: fill_copy(blk, sem).wait())

    base = i * DISPATCH_TILE
    n_tok = pl.num_programs(0) * DISPATCH_TILE

    def row_copy(t, k):
        d = dest_ref[k * n_tok + base + t]
        return pltpu.make_async_copy(x1_ref.at[pl.ds(pl.multiple_of(t * PACKED_SUB, PACKED_SUB), PACKED_SUB)],
                                     xs_ref.at[pl.ds(pl.multiple_of(d * PACKED_SUB, PACKED_SUB), PACKED_SUB)], sem)

    def start_rows(t, _):
        for k in range(MOE_TOPK):
            row_copy(t, k).start(priority=k % DMA_PRIORITIES)
        return 0

    lax.fori_loop(0, DISPATCH_TILE, start_rows, 0, unroll=ISSUE_UNROLL)
    for k in range(MOE_TOPK):
        pltpu.make_async_copy(x1_ref, xs_ref.at[pl.ds(0, DISPATCH_TILE * PACKED_SUB)], sem).wait()

    @pl.when(i == pl.num_programs(0) - 1)
    def _():
        for_blocks(FILL_FREE, lambda blk: fill_copy(blk, fill_sem).wait())


def _dispatch(dest, fill, x1t, n_rows):
    return pl.pallas_call(
        _dispatch_kernel,
        out_shape=jax.ShapeDtypeStruct((n_rows * PACKED_SUB, LANES), x1t.dtype),
        grid_spec=pltpu.PrefetchScalarGridSpec(
            num_scalar_prefetch=2,
            grid=(x1t.shape[0] // (DISPATCH_TILE * PACKED_SUB),),
            in_specs=[pl.BlockSpec((DISPATCH_TILE * PACKED_SUB, LANES), lambda i, d, t: (i, 0))],
            out_specs=pl.BlockSpec(memory_space=pl.ANY),
            scratch_shapes=[pltpu.VMEM((MOE_HALF * PACKED_SUB, LANES), x1t.dtype), pltpu.SemaphoreType.DMA(()),
                            pltpu.SemaphoreType.DMA(())],
        ),
        compiler_params=pltpu.CompilerParams(dimension_semantics=("arbitrary",),
                                             vmem_limit_bytes=VMEM_LIMIT_BYTES),
        name="dispatch",
    )(dest, fill, x1t)


def _experts_kernel(be_ref, slot_ref, nxt_ref, full_ref, nused_ref, x_ref, wg_hbm, bg_ref, wu_hbm, bu_ref, wd_hbm, bd_ref,
                    y_ref, wg_st, wu_st, wd_st, wg_bf, wu_bf, wd_bf, sems):
    i = pl.program_id(0)
    prev = be_ref[jnp.maximum(i - 1, 0)]
    changed = (i == 0) | (be_ref[i] != prev)
    active = i < nused_ref[0]
    slot = slot_ref[i]

    def weight_copies(expert, s):
        return (pltpu.make_async_copy(wg_hbm.at[expert], wg_st.at[s], sems.at[s, 0]),
                pltpu.make_async_copy(wu_hbm.at[expert], wu_st.at[s], sems.at[s, 1]),
                pltpu.make_async_copy(wd_hbm.at[expert], wd_st.at[s], sems.at[s, 2]))

    @pl.when(i == 0)
    def _():
        for cp in weight_copies(be_ref[0], slot):
            cp.start()

    @pl.when(active & changed)
    def _():
        for cp in weight_copies(be_ref[i], slot):
            cp.wait()

        @pl.when(nxt_ref[i] >= 0)
        def _():
            for cp in weight_copies(nxt_ref[i], 1 - slot):
                cp.start(priority=WEIGHT_DMA_PRIORITY)

        wg_bf[...] = wg_st[slot].astype(BF16)
        wu_bf[...] = wu_st[slot].astype(BF16)
        wd_bf[...] = wd_st[slot].astype(BF16)

    def mlp(nrows):
        rows = pl.ds(0, nrows * PACKED_SUB)
        xb = _unpack_tiles_to_rows(x_ref.at[rows], nrows)
        g = jnp.minimum(_dot(xb, wg_bf[...]) + bg_ref[0], SWIGLU_LIMIT)
        u = jnp.clip(_dot(xb, wu_bf[...]) + bu_ref[0], -SWIGLU_LIMIT, SWIGLU_LIMIT)
        h = g * (1.0 / (1.0 + jnp.exp(-SWIGLU_ALPHA * g))) * (u + 1.0)
        _pack_rows_to_tiles(y_ref.at[rows], _dot(h.astype(BF16), wd_bf[...]) + bd_ref[0])

    whole = active & (full_ref[i] > 0)

    @pl.when(whole)
    def _():
        mlp(MOE_GROUP)

    @pl.when(active & jnp.logical_not(whole))
    def _():
        mlp(MOE_HALF)

    @pl.when(jnp.logical_not(active))
    def _():
        y_ref[pl.ds(0, MOE_HALF * PACKED_SUB), :] = jnp.zeros((MOE_HALF * PACKED_SUB, LANES), y_ref.dtype)

    @pl.when(jnp.logical_not(whole))
    def _():
        y_ref[pl.ds(MOE_HALF * PACKED_SUB, MOE_HALF * PACKED_SUB), :] = jnp.zeros((MOE_HALF * PACKED_SUB, LANES), y_ref.dtype)


def _experts(blk_expert, blk_slot, blk_next, blk_full, n_used, xs, wg, bg, wu, bu, wd, bd):
    E, D, F = wg.shape
    nblk = xs.shape[0] // (MOE_GROUP * PACKED_SUB)

    def rowmap(i, be, sl, nx, fl, nu):
        return (jnp.minimum(i, nu[0] - 1), 0)

    def bmap(i, be, sl, nx, fl, nu):
        return (be[i], 0, 0)

    hbm = pl.BlockSpec(memory_space=pl.ANY)
    return pl.pallas_call(
        _experts_kernel,
        out_shape=jax.ShapeDtypeStruct(xs.shape, xs.dtype),
        grid_spec=pltpu.PrefetchScalarGridSpec(
            num_scalar_prefetch=5,
            grid=(nblk,),
            in_specs=[pl.BlockSpec((MOE_GROUP * PACKED_SUB, LANES), rowmap),
                      hbm, pl.BlockSpec((1, 1, F), bmap),
                      hbm, pl.BlockSpec((1, 1, F), bmap),
                      hbm, pl.BlockSpec((1, 1, D), bmap)],
            out_specs=pl.BlockSpec((MOE_GROUP * PACKED_SUB, LANES), lambda i, be, sl, nx, fl, nu: (i, 0)),
            scratch_shapes=[pltpu.VMEM((2, D, F), F32), pltpu.VMEM((2, D, F), F32), pltpu.VMEM((2, F, D), F32),
                            pltpu.VMEM((D, F), BF16), pltpu.VMEM((D, F), BF16), pltpu.VMEM((F, D), BF16),
                            pltpu.SemaphoreType.DMA((2, 3))],
        ),
        compiler_params=pltpu.CompilerParams(dimension_semantics=("arbitrary",),
                                             vmem_limit_bytes=VMEM_LIMIT_BYTES),
        name="experts",
    )(blk_expert, blk_slot, blk_next, blk_full, n_used, xs, wg, bg, wu, bu, wd, bd)


def _combine_kernel(alpha, dest_ref, gates_ref, x1_ref, g_ref, b_ref, y_ref, o_ref, ybuf, sems):
    i = pl.program_id(0)
    buf = i % 2

    n_tok = pl.num_programs(0) * COMBINE_TILE

    def gather_tile(tile, into):
        base = tile * COMBINE_TILE

        def start_rows(t, _):
            for k in range(MOE_TOPK):
                d = dest_ref[k * n_tok + base + t]
                pltpu.make_async_copy(y_ref.at[pl.ds(pl.multiple_of(d * PACKED_SUB, PACKED_SUB), PACKED_SUB)],
                                      ybuf.at[into, k, pl.ds(pl.multiple_of(t * PACKED_SUB, PACKED_SUB), PACKED_SUB)],
                                      sems.at[into]).start(priority=k % DMA_PRIORITIES)
            return 0

        lax.fori_loop(0, COMBINE_TILE, start_rows, 0, unroll=ISSUE_UNROLL)

    @pl.when(i == 0)
    def _():
        gather_tile(0, 0)

    @pl.when(i + 1 < pl.num_programs(0))
    def _():
        gather_tile(i + 1, 1 - buf)

    for k in range(MOE_TOPK):
        pltpu.make_async_copy(y_ref.at[pl.ds(0, COMBINE_TILE * PACKED_SUB)], ybuf.at[buf, k], sems.at[buf]).wait()

    gates = gates_ref[...]
    ffn = gates[:, 0:1] * _unpack_tiles_to_rows(ybuf.at[buf, 0], COMBINE_TILE).astype(F32)
    for k in range(1, MOE_TOPK):
        ffn = ffn + gates[:, k:k + 1] * _unpack_tiles_to_rows(ybuf.at[buf, k], COMBINE_TILE).astype(F32)
    o_ref[...] = _layer_norm(alpha * x1_ref[...] + ffn, g_ref[...], b_ref[...])


def _combine(dest, gates, x1, g2, b2, y_rows, alpha):
    N, D = x1.shape
    row = lambda i, d: (i, 0)
    full = lambda i, d: (0, 0)
    return pl.pallas_call(
        functools.partial(_combine_kernel, alpha),
        out_shape=jax.ShapeDtypeStruct((N, D), F32),
        grid_spec=pltpu.PrefetchScalarGridSpec(
            num_scalar_prefetch=1,
            grid=(N // COMBINE_TILE,),
            in_specs=[pl.BlockSpec((COMBINE_TILE, LANES), row), pl.BlockSpec((COMBINE_TILE, D), row),
                      pl.BlockSpec(g2.shape, full), pl.BlockSpec(b2.shape, full),
                      pl.BlockSpec(memory_space=pl.ANY)],
            out_specs=pl.BlockSpec((COMBINE_TILE, D), row),
            scratch_shapes=[pltpu.VMEM((2, MOE_TOPK, COMBINE_TILE * PACKED_SUB, LANES), y_rows.dtype),
                            pltpu.SemaphoreType.DMA((2,))],
        ),
        compiler_params=pltpu.CompilerParams(dimension_semantics=("arbitrary",),
                                             vmem_limit_bytes=VMEM_LIMIT_BYTES),
        name="combine",
    )(dest, gates, x1, g2, b2, y_rows)


def _rot_partner(w, half):
    return jnp.concatenate([-w[..., half:2 * half], w[..., :half]], axis=-1)


def _layer_weights(w_in, w_q_b, w_kv_b):
    D = w_in.shape[0]
    o1 = Q_LORA
    o2 = o1 + KV_LORA
    o3 = o2 + MLA_ROPE
    w_ql, w_kvl, w_kr = w_in[:, :o1], w_in[:, o1:o2], w_in[:, o2:o3]
    w_mq, w_mk, w_mv = w_in[:, o3:o3 + _MH], w_in[:, o3 + _MH:o3 + 2 * _MH], w_in[:, o3 + 2 * _MH:]
    zpad = lambda n: jnp.zeros((D, n), w_in.dtype)
    tail = SLOT - MLA_NOPE - MLA_ROPE
    assert tail == MLA_ROPE
    kr_slot = jnp.concatenate([zpad(MLA_NOPE), w_kr, _rot_partner(w_kr, MLA_ROPE // 2)], axis=1)

    win = jnp.concatenate([w_ql, w_kvl, kr_slot, w_mk], axis=1).astype(BF16)
    wmt = jnp.concatenate([w_mq, w_mv], axis=1).T.astype(BF16)

    wkv3 = w_kv_b.reshape(KV_LORA, MLA_HEADS, MLA_NOPE + MLA_V)
    wk = jnp.concatenate([wkv3[..., :MLA_NOPE], jnp.zeros((KV_LORA, MLA_HEADS, SLOT - MLA_NOPE), w_kv_b.dtype)],
                         axis=-1).reshape(KV_LORA, MLA_HEADS * SLOT).astype(BF16)
    wv = wkv3[..., MLA_NOPE:].reshape(KV_LORA, MLA_HEADS * MLA_V)
    return win, wmt, w_q_b.T.astype(BF16), wk, wv.T.astype(BF16)


def _rope_tables(positions):
    pos = positions.astype(F32).reshape(1, -1)

    def cs(d_rot):
        inv_freq = ROPE_THETA ** (-jnp.arange(0, d_rot, 2, dtype=F32) / d_rot)
        ang = inv_freq.reshape(d_rot // 2, 1) * pos
        return jnp.cos(ang), jnp.sin(ang)

    return cs(MLA_ROPE) + cs(MOBA_ROT)


def kernel(x, positions, w_in, q_a_norm, w_q_b, kv_a_norm, w_kv_b, w_o, ln1_g, ln1_b, w_router, b_router,
           w_gate, b_gate, w_up, b_up, w_down, b_down, ln2_g, ln2_b):
    B, T, D = x.shape
    depth = w_in.shape[0]
    alpha = (2.0 * depth) ** 0.25
    N = B * T
    assert T % MOBA_BLOCK == 0 and T // MOBA_BLOCK <= SUBLANES and N % OPROJ_TILE == 0
    assert D == SUBLANES * LANES
    n_asg = N * MOE_TOPK
    n_rows = n_asg + N_EXPERTS * MOE_GROUP
    tables = _rope_tables(positions)
    h = x.reshape(N, D)
    for l in range(depth):
        win, wmt, wqt, wk, wvt = _layer_weights(w_in[l], w_q_b[l], w_kv_b[l])
        qa, ka, va, mq, mk, mv = _prep(h, win, wmt, wqt, wk, wvt, q_a_norm[l].reshape(1, -1),
                                       kv_a_norm[l].reshape(1, -1), tables, B, T)
        a = _attention(qa, ka, va, B, T, "mla_attention")
        m = _attention(mq, mk, mv, B, T, "moba_attention")
        wr_pad = jnp.concatenate([w_router[l], jnp.zeros((D, LANES - N_EXPERTS), F32)], axis=1)
        wr_hi = wr_pad.astype(BF16)
        wr = jnp.concatenate([wr_hi, (wr_pad - wr_hi.astype(F32)).astype(BF16)], axis=1)
        br = b_router[l].reshape(N_EXPERTS, 1)
        x1, x1t, route, gates, cnt = _oproj(a, m, h, w_o[l].astype(BF16), ln1_g[l].reshape(1, D), ln1_b[l].reshape(1, D),
                                       wr, br, alpha)
        er = jnp.arange(N_EXPERTS, dtype=jnp.int32)
        counts = cnt[:, 0].astype(jnp.int32)
        padded = (counts + MOE_GROUP - 1) // MOE_GROUP * MOE_GROUP
        upto = er[None, :] <= er[:, None]
        pad_end = jnp.sum(jnp.where(upto, padded[None, :], 0), axis=1)
        pad_start = pad_end - padded
        e_idx = route[:MOE_TOPK]
        group_start = jnp.sum(jnp.where(e_idx[..., None] == er, pad_start, 0), axis=-1)
        dest = (group_start + route[MOE_TOPK:2 * MOE_TOPK]).reshape(n_asg)
        n_used = (pad_end[-1:] // MOE_GROUP).astype(jnp.int32)
        nblk = n_rows // MOE_GROUP
        blk_start = jnp.arange(nblk, dtype=jnp.int32) * MOE_GROUP
        blk_expert = jnp.minimum(jnp.sum((blk_start[:, None] >= pad_end[None, :]).astype(jnp.int32), axis=1),
                                 N_EXPERTS - 1)
        of_expert = blk_expert[:, None] == er[None, :]
        pick = lambda per_expert: jnp.sum(jnp.where(of_expert, per_expert[None, :], 0), axis=1)
        nonempty = padded > 0
        group_index = jnp.sum(jnp.where(upto & nonempty[None, :], 1, 0), axis=1) - 1
        later = (er[None, :] > er[:, None]) & nonempty[None, :]
        next_expert = jnp.min(jnp.where(later, er[None, :], N_EXPERTS), axis=1)
        next_expert = jnp.where(next_expert == N_EXPERTS, -1, next_expert)
        blk_slot = pick(group_index) % 2
        blk_next = pick(next_expert)
        blk_full = (pick(pad_start + counts) > blk_start + MOE_HALF).astype(jnp.int32)
        half_start = jnp.arange(n_rows // MOE_HALF, dtype=jnp.int32) * MOE_HALF
        half_expert = jnp.minimum(jnp.sum((half_start[:, None] >= pad_end[None, :]).astype(jnp.int32), axis=1),
                                  N_EXPERTS - 1)
        half_real_end = jnp.sum(jnp.where(half_expert[:, None] == er[None, :], (pad_start + counts)[None, :], 0), axis=1)
        no_real_rows = (half_start >= pad_end[-1]) | (half_start >= half_real_end)
        fill = jnp.where(no_real_rows, FILL_FREE,
                         jnp.where(half_start + MOE_HALF > half_real_end, FILL_TAIL, 0)).astype(jnp.int32)
        xs = _dispatch(dest, fill, x1t, n_rows)
        y_rows = _experts(blk_expert, blk_slot.astype(jnp.int32), blk_next, blk_full, n_used, xs, w_gate[l], b_gate[l].reshape(N_EXPERTS, 1, -1),
                          w_up[l], b_up[l].reshape(N_EXPERTS, 1, -1), w_down[l], b_down[l].reshape(N_EXPERTS, 1, -1))
        h = _combine(dest, gates, x1, ln2_g[l].reshape(1, D), ln2_b[l].reshape(1, D), y_rows, alpha)
    return h.reshape(B, T, D)
```

```python
import functools
import math

import jax
import jax.numpy as jnp
from jax import lax
from jax.experimental import pallas as pl
from jax.experimental.pallas import tpu as pltpu

ROPE_THETA = 500000.0
MLA_HEADS = 8
MLA_NOPE = 64
MLA_ROPE = 32
MLA_V = 64
Q_LORA = 256
KV_LORA = 128
MOBA_HEADS = 8
MOBA_HD = 64
MOBA_ROT = MOBA_HD // 4
MOBA_BLOCK = 256
MOBA_TOPK = 3
N_EXPERTS = 32
MOE_TOPK = 4
SWIGLU_LIMIT = 7.0
SWIGLU_ALPHA = 1.702
RMS_EPS = 1e-6
LN_EPS = 1e-5

LANES = 128
SUBLANES = 8
VMEM_LIMIT_BYTES = 56 * 1024 * 1024

SLOT = LANES
TQ = MOBA_BLOCK
ROW_TILE = 256
OPROJ_TILE = 1024
MOE_GROUP = 512
MOE_HALF = MOE_GROUP // 2
PACKED_SUB = SUBLANES // 2
FILL_TAIL, FILL_FREE = 1, 2
DISPATCH_TILE = 1024
COMBINE_TILE = 512
ISSUE_UNROLL = 4
DMA_PRIORITIES = 2
WEIGHT_DMA_PRIORITY = 1
HEAD_LANES = 64
ONES_LANE = 64
BIAS_LANE = 64
ATTN_HEADS_PER_STEP = 8
LOG2E = math.log2(math.e)
NEG_BIG = -(2.0 ** 100)

F32 = jnp.float32
BF16 = jnp.bfloat16
NT_DIMS = (((1,), (1,)), ((), ()))
TN_DIMS = (((0,), (0,)), ((), ()))


def _dot(a, b, precision=None):
    return jnp.dot(a, b, preferred_element_type=F32, precision=precision)


def _dot_nt(a, b, precision=None):
    return lax.dot_general(a, b, NT_DIMS, preferred_element_type=F32, precision=precision)


def _pack_rows_to_tiles(ref, x):
    rows, width = x.shape
    bits = lambda v: lax.bitcast_convert_type(v.astype(BF16).astype(F32), jnp.uint32)
    words = (bits(x[:, :width // 2]) >> 16) | (bits(x[:, width // 2:]) & jnp.uint32(0xFFFF0000))
    for c in range(PACKED_SUB):
        ref[pl.ds(c, rows, stride=PACKED_SUB), :] = words[:, c * LANES:(c + 1) * LANES]


def _unpack_tiles_to_rows(ref, rows):
    words = jnp.concatenate([ref[pl.ds(c, rows, stride=PACKED_SUB), :] for c in range(PACKED_SUB)], axis=1)
    low = lax.bitcast_convert_type(words << 16, F32).astype(BF16)
    high = lax.bitcast_convert_type(words & jnp.uint32(0xFFFF0000), F32).astype(BF16)
    return jnp.concatenate([low, high], axis=1)


def _rms(x, g):
    return x * lax.rsqrt(jnp.mean(x * x, axis=-1, keepdims=True) + RMS_EPS) * g


def _layer_norm(x, g, b):
    mu = jnp.mean(x, axis=-1, keepdims=True)
    xc = x - mu
    var = jnp.mean(xc * xc, axis=-1, keepdims=True)
    return xc * lax.rsqrt(var + LN_EPS) * g + b


_C_QL = 0
_C_KVL = _C_QL + Q_LORA
_C_KR = _C_KVL + KV_LORA
_C_MK = _C_KR + SLOT
_MH = MOBA_HEADS * MOBA_HD
_C_END = _C_MK + _MH


def _prep_kernel(x_ref, win_ref, wmt_ref, wqt_ref, wk_ref, wvt_ref, qg_ref, kvg_ref,
                 cat_ref, sat_ref, cbt_ref, sbt_ref,
                 qt_ref, k_ref, vt_ref, kmean_scr):
    c = pl.program_id(1)
    width = MLA_HEADS * SLOT
    qat_ref, mqt_ref = qt_ref.at[:, 0:width, :], qt_ref.at[:, width:2 * width, :]
    ka_ref, mk_ref = k_ref.at[:, 0:width], k_ref.at[:, width:2 * width]
    vat_ref, mvt_ref = vt_ref.at[:, 0:width, :], vt_ref.at[:, width:2 * width, :]

    @pl.when(c == 0)
    def _():
        kmean_scr[...] = jnp.zeros_like(kmean_scr)

    xb = x_ref[...].astype(BF16)
    both_halves = lambda t: jnp.concatenate([t, t], axis=0)
    cat = both_halves(cat_ref[...])
    sat = both_halves(sat_ref[...])
    cbt = both_halves(cbt_ref[...])
    sat_signed = jnp.concatenate([-sat_ref[...], sat_ref[...]], axis=0)
    sbt_signed = jnp.concatenate([-sbt_ref[...], sbt_ref[...]], axis=0)

    def rotated_t(r, cos, sin_signed):
        half = r.shape[0] // 2
        return r * cos + jnp.concatenate([r[half:], r[:half]], axis=0) * sin_signed

    const = lambda v, n: jnp.full((n, ROW_TILE), v, F32)
    tail = SLOT - MLA_NOPE - MLA_ROPE
    rest = MOBA_HD - MOBA_ROT
    ca = jnp.concatenate([const(1.0, MLA_NOPE), cat, const(0.0, tail)], axis=0).T
    sa = jnp.concatenate([const(0.0, MLA_NOPE), sat, const(0.0, tail)], axis=0).T
    cb = jnp.concatenate([cbt, const(1.0, rest), cbt, const(1.0, rest)], axis=0).T
    sb = jnp.concatenate([sbt_signed, const(0.0, rest), sbt_signed, const(0.0, rest)], axis=0).T
    lane = lax.broadcasted_iota(jnp.int32, (ROW_TILE, SLOT), 1)
    head_lanes = lane < HEAD_LANES
    first_half = (lane % MOBA_HD) < MOBA_ROT // 2
    ones_rows = (lax.broadcasted_iota(jnp.int32, (SLOT - HEAD_LANES, ROW_TILE), 0) == ONES_LANE - HEAD_LANES).astype(F32)

    ql = _dot(xb, win_ref[:, _C_QL:_C_KVL])
    kvl = _dot(xb, win_ref[:, _C_KVL:_C_KR])
    kr = _dot(xb, win_ref[:, _C_KR:_C_MK])
    krs = pltpu.roll(kr, SLOT - MLA_ROPE, axis=1)
    qn = _rms(ql, qg_ref[...]).astype(BF16)
    kvn = _rms(kvl, kvg_ref[...]).astype(BF16)
    q_t = _dot_nt(wqt_ref[...], qn)
    kn = _dot(kvn, wk_ref[...])
    v_t = _dot_nt(wvt_ref[...], kvn)
    scale_a = LOG2E / math.sqrt(MLA_NOPE + MLA_ROPE)
    kro = kr * ca + krs * sa
    qd = MLA_NOPE + MLA_ROPE
    q_pad = jnp.zeros((SLOT - qd, ROW_TILE), F32)
    for h in range(MLA_HEADS):
        sl = slice(h * SLOT, (h + 1) * SLOT)
        q_rope = rotated_t(q_t[h * qd + MLA_NOPE:(h + 1) * qd], cat, sat_signed)
        q_slot = jnp.concatenate([q_t[h * qd:h * qd + MLA_NOPE], q_rope, q_pad], axis=0)
        qat_ref[0, sl, :] = (q_slot * scale_a).astype(BF16)
        ka_ref[:, sl] = (kn[:, sl] + kro).astype(BF16)
        vat_ref[0, sl, :] = jnp.concatenate([v_t[h * MLA_V:(h + 1) * MLA_V], ones_rows], axis=0).astype(BF16)

    mk = _dot(xb, win_ref[:, _C_MK:_C_END])
    npair = _MH // LANES
    pair = lambda a, j: a[:, j * LANES:(j + 1) * LANES]

    def rotated(a):
        x2 = pltpu.roll(a, LANES - MOBA_ROT // 2, axis=1)
        x1 = pltpu.roll(a, MOBA_ROT // 2, axis=1)
        return a * cb + jnp.where(first_half, x2, x1) * sb

    mk_rot = [rotated(pair(mk, j)) for j in range(npair)]
    mk_all = jnp.concatenate(mk_rot, axis=1)

    mq_t = _dot_nt(wmt_ref[0:_MH, :], xb)
    mv_t = _dot_nt(wmt_ref[_MH:2 * _MH, :], xb)
    head = lambda a, h: a[h * MOBA_HD:(h + 1) * MOBA_HD]
    mq_rot_t = [jnp.concatenate([rotated_t(head(mq_t, h)[0:MOBA_ROT], cbt, sbt_signed),
                                 head(mq_t, h)[MOBA_ROT:]], axis=0) for h in range(MOBA_HEADS)]

    nrow = MOBA_HEADS * SUBLANES
    row_i = lax.broadcasted_iota(jnp.int32, (nrow, _MH), 0)
    lane_i = lax.broadcasted_iota(jnp.int32, (nrow, _MH), 1)
    kmean_c = jnp.mean(mk_all, axis=0, keepdims=True)
    put = ((row_i % SUBLANES) == c) & ((lane_i // MOBA_HD) == (row_i // SUBLANES))
    table = kmean_scr[...]
    split = lambda a: (a.astype(BF16), (a - a.astype(BF16).astype(F32)).astype(BF16))
    t_hi, t_lo = split(table)
    q_hi, q_lo = split(jnp.concatenate(mq_rot_t, axis=0))
    gate_t = _dot(jnp.concatenate([t_hi, t_lo], axis=0), q_hi)
    gate_t = gate_t[0:nrow] + gate_t[nrow:] + _dot(t_hi, q_lo)
    kmean_scr[...] = jnp.where(put, jnp.broadcast_to(kmean_c, (nrow, _MH)), table)

    n_idx = lax.broadcasted_iota(jnp.int32, (SUBLANES, ROW_TILE), 0)
    valid = n_idx < c
    scale_b = LOG2E / math.sqrt(MOBA_HD)
    zero_rows = jnp.zeros((SLOT - HEAD_LANES - SUBLANES, ROW_TILE), F32)
    block_onehot = (lane == BIAS_LANE + c).astype(F32)
    for h in range(MOBA_HEADS):
        g = jnp.where(valid, gate_t[h * SUBLANES:(h + 1) * SUBLANES, :], -jnp.inf)
        rank = jnp.zeros((SUBLANES, ROW_TILE), jnp.int32)
        for k in range(1, SUBLANES):
            other = pltpu.roll(g, k, axis=0)
            other_n = pltpu.roll(n_idx, k, axis=0)
            beats = (other > g) | ((other == g) & (other_n < n_idx))
            rank = rank + beats.astype(jnp.int32)
        keep = (valid & (rank < MOBA_TOPK)) | (n_idx == c)
        bias = jnp.where(keep, 0.0, NEG_BIG)
        sl = slice(h * SLOT, (h + 1) * SLOT)
        mqt_ref[0, sl, :] = jnp.concatenate([mq_rot_t[h] * scale_b, bias, zero_rows], axis=0).astype(BF16)
        mvt_ref[0, sl, :] = jnp.concatenate([head(mv_t, h), ones_rows], axis=0).astype(BF16)
        j, hh = divmod(h, 2)
        k_h = mk_rot[j] if hh == 0 else pltpu.roll(mk_rot[j], HEAD_LANES, axis=1)
        mk_ref[:, sl] = jnp.where(head_lanes, k_h, block_onehot).astype(BF16)


def _prep(x2, win, wmt, wqt, wk, wvt, qg, kvg, tables, B, T):
    N, D = x2.shape
    nt = T // ROW_TILE
    row = lambda b, c: (b * nt + c, 0)
    col = lambda b, c: (0, b * nt + c)
    full = lambda b, c: (0, 0)
    cat, sat, cbt, sbt = tables
    width = MLA_HEADS * SLOT

    def rows(w):
        return pl.BlockSpec((ROW_TILE, w), row)

    def cols(a):
        return pl.BlockSpec((a.shape[0], ROW_TILE), col)

    def whole(a):
        return pl.BlockSpec(a.shape, full)

    width = 2 * width
    rowmajor = jax.ShapeDtypeStruct((N, width), BF16)
    transposed = jax.ShapeDtypeStruct((B * nt, width, ROW_TILE), BF16)
    t_spec = pl.BlockSpec((1, width, ROW_TILE), lambda b, c: (b * nt + c, 0, 0))
    return pl.pallas_call(
        _prep_kernel,
        out_shape=(transposed, rowmajor, transposed),
        grid=(B, nt),
        in_specs=[rows(D), whole(win), whole(wmt), whole(wqt), whole(wk), whole(wvt), whole(qg), whole(kvg),
                  cols(cat), cols(sat), cols(cbt), cols(sbt)],
        out_specs=(t_spec, rows(width), t_spec),
        scratch_shapes=[pltpu.VMEM((MOBA_HEADS * SUBLANES, _MH), F32)],
        compiler_params=pltpu.CompilerParams(dimension_semantics=("arbitrary", "arbitrary"),
                                             vmem_limit_bytes=VMEM_LIMIT_BYTES),
        name="prep",
    )(x2, win, wmt, wqt, wk, wvt, qg, kvg, cat, sat, cbt, sbt)


def _attn_kernel(qt_ref, k_ref, vt_ref, o_ref, s_scr, acc_scr):
    i = pl.program_id(2)
    nh = ATTN_HEADS_PER_STEP
    slot = lambda h: slice(h * SLOT, (h + 1) * SLOT)
    nblk = i + 1
    nquad = nblk // 4
    pair_end = 4 * nquad + 2 * ((nblk - 4 * nquad) // 2)
    q_pos = i * TQ + lax.broadcasted_iota(jnp.int32, (1, TQ), 1)

    def fold(s):
        out = s[0:SUBLANES]
        for t in range(1, s.shape[0] // SUBLANES):
            out = jnp.maximum(out, s[t * SUBLANES:(t + 1) * SUBLANES])
        return out

    def score_blocks(j, n, mrun):
        j0 = pl.multiple_of(j * TQ, TQ)
        visible = (j0 + lax.broadcasted_iota(jnp.int32, (n * TQ, TQ), 0)) <= q_pos
        out = []
        for h in range(nh):
            s = _dot(k_ref[pl.ds(j0, n * TQ), slot(h)], qt_ref[0, slot(h), :])
            s = jnp.where(visible, s, -jnp.inf)
            s_scr[h, pl.ds(j, n)] = s.reshape(n, TQ, TQ)
            out.append(jnp.maximum(mrun[h], fold(s)))
        return tuple(out)

    mrun = tuple(jnp.full((SUBLANES, TQ), -jnp.inf, F32) for _ in range(nh))
    mrun = lax.fori_loop(0, nquad, lambda jq, m: score_blocks(4 * jq, 4, m), mrun)
    mrun = lax.fori_loop(2 * nquad, pair_end // 2, lambda jj, m: score_blocks(2 * jj, 2, m), mrun)
    mrun = lax.fori_loop(pair_end, nblk, lambda j, m: score_blocks(j, 1, m), mrun)
    ms = [jnp.max(m, axis=0, keepdims=True) for m in mrun]

    def accumulate(j, n, carry):
        for h in range(nh):
            p = jnp.exp2(s_scr[h, pl.ds(j, n)].reshape(n * TQ, TQ) - ms[h]).astype(BF16)
            vt = jnp.concatenate([vt_ref[j + t, slot(h), :] for t in range(n)], axis=1)
            acc_scr[h] += _dot(vt, p)
        return carry

    acc_scr[...] = jnp.zeros_like(acc_scr)
    lax.fori_loop(0, nquad, lambda jq, c: accumulate(4 * jq, 4, c), 0)
    lax.fori_loop(2 * nquad, pair_end // 2, lambda jj, c: accumulate(2 * jj, 2, c), 0)
    lax.fori_loop(pair_end, nblk, lambda j, c: accumulate(j, 1, c), 0)
    outs = [acc_scr[h, 0:HEAD_LANES] / acc_scr[h, ONES_LANE:ONES_LANE + 1] for h in range(nh)]
    for h in range(nh):
        o_ref[0, h * HEAD_LANES:(h + 1) * HEAD_LANES, :] = outs[h].astype(o_ref.dtype)


def _attention(qt, k, vt, B, T, name):
    N = k.shape[0]
    nq = T // TQ
    nh = ATTN_HEADS_PER_STEP
    heads = k.shape[1] // SLOT
    return pl.pallas_call(
        _attn_kernel,
        out_shape=jax.ShapeDtypeStruct((B * nq, heads * HEAD_LANES, TQ), BF16),
        grid=(B, heads // nh, nq),
        in_specs=[pl.BlockSpec((1, nh * SLOT, TQ), lambda b, g, i: (b * nq + i, g, 0)),
                  pl.BlockSpec((T, nh * SLOT), lambda b, g, i: (b, g)),
                  pl.BlockSpec((nq, nh * SLOT, TQ), lambda b, g, i: (b, g, 0))],
        out_specs=pl.BlockSpec((1, nh * HEAD_LANES, TQ), lambda b, g, i: (b * nq + i, g, 0)),
        scratch_shapes=[pltpu.VMEM((nh, nq, TQ, TQ), F32), pltpu.VMEM((nh, SLOT, TQ), F32)],
        compiler_params=pltpu.CompilerParams(dimension_semantics=("arbitrary", "arbitrary", "arbitrary"),
                                             vmem_limit_bytes=VMEM_LIMIT_BYTES),
        name=name,
    )(qt, k, vt)


def _oproj_kernel(alpha, a_ref, x_ref, wo_ref, g_ref, b_ref, wr_ref, br_ref,
                  x1_ref, x1t_ref, route_ref, gates_ref, cnt_ref, carry_scr):
    i = pl.program_id(0)

    @pl.when(i == 0)
    def _():
        carry_scr[...] = jnp.zeros_like(carry_scr)

    parts = []
    for t in range(a_ref.shape[0]):
        parts.append(lax.dot_general(a_ref[t], wo_ref[...], TN_DIMS, preferred_element_type=F32))
    mix = jnp.concatenate(parts, axis=0)
    x1 = _layer_norm(alpha * x_ref[...] + mix, g_ref[...], b_ref[...])
    x1_ref[...] = x1
    _pack_rows_to_tiles(x1t_ref, x1)

    ne = br_ref.shape[0]
    rows = x1.shape[0]
    x_hi = x1.astype(BF16)
    x_lo = (x1 - x_hi.astype(F32)).astype(BF16)
    both = _dot(x_hi, wr_ref[...])
    logits_rm = both[:, :LANES] + both[:, LANES:] + _dot(x_lo, wr_ref[:, :LANES])
    logits = logits_rm.T[0:ne] + br_ref[...]
    expert = lax.broadcasted_iota(jnp.int32, logits.shape, 0)
    expert_f = expert.astype(F32)
    vals, idxs = [], []
    work = logits
    for _ in range(MOE_TOPK):
        mx = jnp.max(work, axis=0, keepdims=True)
        ix = jnp.min(jnp.where(work == mx, expert_f, float(ne)), axis=0, keepdims=True).astype(jnp.int32)
        vals.append(mx)
        idxs.append(ix)
        work = jnp.where(expert == ix, -jnp.inf, work)
    exps = [jnp.exp(v - vals[0]) for v in vals]
    den = exps[0]
    for e in exps[1:]:
        den = den + e

    onehot = jnp.zeros(logits.shape, F32)
    for ix in idxs:
        onehot = onehot + (expert == ix).astype(F32)
    r = lax.broadcasted_iota(jnp.int32, (rows, rows), 0)
    cidx = lax.broadcasted_iota(jnp.int32, (rows, rows), 1)
    earlier = (r < cidx).astype(BF16)
    carry = carry_scr[:, 0:1]
    before = _dot(onehot.astype(BF16), earlier) + carry
    row8 = lax.broadcasted_iota(jnp.int32, (SUBLANES, rows), 0)
    route = jnp.zeros((SUBLANES, rows), jnp.int32)
    gates = jnp.zeros((SUBLANES, rows), F32)
    for k in range(MOE_TOPK):
        rank = jnp.sum(jnp.where(expert == idxs[k], before, 0.0), axis=0, keepdims=True).astype(jnp.int32)
        route = jnp.where(row8 == k, idxs[k], route)
        route = jnp.where(row8 == MOE_TOPK + k, rank, route)
        gates = jnp.where(row8 == k, exps[k] / den, gates)
    route_ref[...] = route
    gates_ref[...] = jnp.concatenate([gates, jnp.zeros((LANES - SUBLANES, rows), F32)], axis=0).T
    new_carry = carry + jnp.sum(onehot, axis=1, keepdims=True)
    carry_scr[...] = jnp.broadcast_to(new_carry, carry_scr.shape)
    cnt_ref[...] = jnp.broadcast_to(new_carry, cnt_ref.shape)


def _oproj(a, x2, wo, g1, b1, wr, br, alpha):
    N, D = x2.shape
    nt = N // OPROJ_TILE
    row = lambda i: (i, 0)
    full = lambda i: (0, 0)
    return pl.pallas_call(
        functools.partial(_oproj_kernel, alpha),
        out_shape=(jax.ShapeDtypeStruct((N, D), F32),
                   jax.ShapeDtypeStruct((N * PACKED_SUB, LANES), jnp.uint32),
                   jax.ShapeDtypeStruct((SUBLANES, N), jnp.int32),
                   jax.ShapeDtypeStruct((N, LANES), F32),
                   jax.ShapeDtypeStruct((br.shape[0], LANES), F32)),
        grid=(nt,),
        in_specs=[pl.BlockSpec((OPROJ_TILE // TQ,) + a.shape[1:], lambda i: (i, 0, 0)),
                  pl.BlockSpec((OPROJ_TILE, D), row), pl.BlockSpec(wo.shape, full),
                  pl.BlockSpec(g1.shape, full), pl.BlockSpec(b1.shape, full),
                  pl.BlockSpec(wr.shape, full), pl.BlockSpec(br.shape, full)],
        out_specs=(pl.BlockSpec((OPROJ_TILE, D), row), pl.BlockSpec((OPROJ_TILE * PACKED_SUB, LANES), row),
                   pl.BlockSpec((SUBLANES, OPROJ_TILE), lambda i: (0, i)),
                   pl.BlockSpec((OPROJ_TILE, LANES), row), pl.BlockSpec((br.shape[0], LANES), full)),
        scratch_shapes=[pltpu.VMEM((br.shape[0], LANES), F32)],
        compiler_params=pltpu.CompilerParams(dimension_semantics=("arbitrary",),
                                             vmem_limit_bytes=VMEM_LIMIT_BYTES),
        name="oproj_router",
    )(a, x2, wo, g1, b1, wr, br)


def _dispatch_kernel(dest_ref, fill_ref, x1_ref, xs_ref, zero_scr, sem, fill_sem):
    i = pl.program_id(0)
    fill_sub = MOE_HALF * PACKED_SUB
    nfill = xs_ref.shape[0] // fill_sub

    def fill_copy(blk, on):
        return pltpu.make_async_copy(zero_scr, xs_ref.at[pl.ds(pl.multiple_of(blk * fill_sub, fill_sub), fill_sub)], on)

    def for_blocks(kind, action):
        def body(blk, _):
            @pl.when(fill_ref[blk] == kind)
            def _():
                action(blk)
            return 0

        lax.fori_loop(0, nfill, body, 0)

    @pl.when(i == 0)
    def _():
        zero_scr[...] = jnp.zeros_like(zero_scr)
        for_blocks(FILL_TAIL, lambda blk: fill_copy(blk, sem).start())
        for_blocks(FILL_FREE, lambda blk: fill_copy(blk, fill_sem).start())
        for_blocks(FILL_TAIL, lambda blk: fill_copy(blk, sem).wait())

    base = i * DISPATCH_TILE
    n_tok = pl.num_programs(0) * DISPATCH_TILE

    def row_copy(t, k):
        d = dest_ref[k * n_tok + base + t]
        return pltpu.make_async_copy(x1_ref.at[pl.ds(pl.multiple_of(t * PACKED_SUB, PACKED_SUB), PACKED_SUB)],
                                     xs_ref.at[pl.ds(pl.multiple_of(d * PACKED_SUB, PACKED_SUB), PACKED_SUB)], sem)

    def start_rows(t, _):
        for k in range(MOE_TOPK):
            row_copy(t, k).start(priority=k % DMA_PRIORITIES)
        return 0

    lax.fori_loop(0, DISPATCH_TILE, start_rows, 0, unroll=ISSUE_UNROLL)
    for k in range(MOE_TOPK):
        pltpu.make_async_copy(x1_ref, xs_ref.at[pl.ds(0, DISPATCH_TILE * PACKED_SUB)], sem).wait()

    @pl.when(i == pl.num_programs(0) - 1)
    def _():
        for_blocks(FILL_FREE, lambda blk: fill_copy(blk, fill_sem).wait())


def _dispatch(dest, fill, x1t, n_rows):
    return pl.pallas_call(
        _dispatch_kernel,
        out_shape=jax.ShapeDtypeStruct((n_rows * PACKED_SUB, LANES), x1t.dtype),
        grid_spec=pltpu.PrefetchScalarGridSpec(
            num_scalar_prefetch=2,
            grid=(x1t.shape[0] // (DISPATCH_TILE * PACKED_SUB),),
            in_specs=[pl.BlockSpec((DISPATCH_TILE * PACKED_SUB, LANES), lambda i, d, t: (i, 0))],
            out_specs=pl.BlockSpec(memory_space=pl.ANY),
            scratch_shapes=[pltpu.VMEM((MOE_HALF * PACKED_SUB, LANES), x1t.dtype), pltpu.SemaphoreType.DMA(()),
                            pltpu.SemaphoreType.DMA(())],
        ),
        compiler_params=pltpu.CompilerParams(dimension_semantics=("arbitrary",),
                                             vmem_limit_bytes=VMEM_LIMIT_BYTES),
        name="dispatch",
    )(dest, fill, x1t)


def _experts_kernel(be_ref, slot_ref, nxt_ref, full_ref, nused_ref, x_ref, wg_hbm, bg_ref, wu_hbm, bu_ref, wd_hbm, bd_ref,
                    y_ref, wg_st, wu_st, wd_st, wg_bf, wu_bf, wd_bf, sems):
    i = pl.program_id(0)
    prev = be_ref[jnp.maximum(i - 1, 0)]
    changed = (i == 0) | (be_ref[i] != prev)
    active = i < nused_ref[0]
    slot = slot_ref[i]

    def weight_copies(expert, s):
        return (pltpu.make_async_copy(wg_hbm.at[expert], wg_st.at[s], sems.at[s, 0]),
                pltpu.make_async_copy(wu_hbm.at[expert], wu_st.at[s], sems.at[s, 1]),
                pltpu.make_async_copy(wd_hbm.at[expert], wd_st.at[s], sems.at[s, 2]))

    @pl.when(i == 0)
    def _():
        for cp in weight_copies(be_ref[0], slot):
            cp.start()

    @pl.when(active & changed)
    def _():
        for cp in weight_copies(be_ref[i], slot):
            cp.wait()

        @pl.when(nxt_ref[i] >= 0)
        def _():
            for cp in weight_copies(nxt_ref[i], 1 - slot):
                cp.start(priority=WEIGHT_DMA_PRIORITY)

        wg_bf[...] = wg_st[slot].astype(BF16)
        wu_bf[...] = wu_st[slot].astype(BF16)
        wd_bf[...] = wd_st[slot].astype(BF16)

    def mlp(nrows):
        rows = pl.ds(0, nrows * PACKED_SUB)
        xb = _unpack_tiles_to_rows(x_ref.at[rows], nrows)
        g = jnp.minimum(_dot(xb, wg_bf[...]) + bg_ref[0], SWIGLU_LIMIT)
        u = jnp.clip(_dot(xb, wu_bf[...]) + bu_ref[0], -SWIGLU_LIMIT, SWIGLU_LIMIT)
        h = g * (1.0 / (1.0 + jnp.exp(-SWIGLU_ALPHA * g))) * (u + 1.0)
        _pack_rows_to_tiles(y_ref.at[rows], _dot(h.astype(BF16), wd_bf[...]) + bd_ref[0])

    whole = active & (full_ref[i] > 0)

    @pl.when(whole)
    def _():
        mlp(MOE_GROUP)

    @pl.when(active & jnp.logical_not(whole))
    def _():
        mlp(MOE_HALF)

    @pl.when(jnp.logical_not(active))
    def _():
        y_ref[pl.ds(0, MOE_HALF * PACKED_SUB), :] = jnp.zeros((MOE_HALF * PACKED_SUB, LANES), y_ref.dtype)

    @pl.when(jnp.logical_not(whole))
    def _():
        y_ref[pl.ds(MOE_HALF * PACKED_SUB, MOE_HALF * PACKED_SUB), :] = jnp.zeros((MOE_HALF * PACKED_SUB, LANES), y_ref.dtype)


def _experts(blk_expert, blk_slot, blk_next, blk_full, n_used, xs, wg, bg, wu, bu, wd, bd):
    E, D, F = wg.shape
    nblk = xs.shape[0] // (MOE_GROUP * PACKED_SUB)

    def rowmap(i, be, sl, nx, fl, nu):
        return (jnp.minimum(i, nu[0] - 1), 0)

    def bmap(i, be, sl, nx, fl, nu):
        return (be[i], 0, 0)

    hbm = pl.BlockSpec(memory_space=pl.ANY)
    return pl.pallas_call(
        _experts_kernel,
        out_shape=jax.ShapeDtypeStruct(xs.shape, xs.dtype),
        grid_spec=pltpu.PrefetchScalarGridSpec(
            num_scalar_prefetch=5,
            grid=(nblk,),
            in_specs=[pl.BlockSpec((MOE_GROUP * PACKED_SUB, LANES), rowmap),
                      hbm, pl.BlockSpec((1, 1, F), bmap),
                      hbm, pl.BlockSpec((1, 1, F), bmap),
                      hbm, pl.BlockSpec((1, 1, D), bmap)],
            out_specs=pl.BlockSpec((MOE_GROUP * PACKED_SUB, LANES), lambda i, be, sl, nx, fl, nu: (i, 0)),
            scratch_shapes=[pltpu.VMEM((2, D, F), F32), pltpu.VMEM((2, D, F), F32), pltpu.VMEM((2, F, D), F32),
                            pltpu.VMEM((D, F), BF16), pltpu.VMEM((D, F), BF16), pltpu.VMEM((F, D), BF16),
                            pltpu.SemaphoreType.DMA((2, 3))],
        ),
        compiler_params=pltpu.CompilerParams(dimension_semantics=("arbitrary",),
                                             vmem_limit_bytes=VMEM_LIMIT_BYTES),
        name="experts",
    )(blk_expert, blk_slot, blk_next, blk_full, n_used, xs, wg, bg, wu, bu, wd, bd)


def _combine_kernel(alpha, dest_ref, gates_ref, x1_ref, g_ref, b_ref, y_ref, o_ref, ybuf, sems):
    i = pl.program_id(0)
    buf = i % 2

    n_tok = pl.num_programs(0) * COMBINE_TILE

    def gather_tile(tile, into):
        base = tile * COMBINE_TILE

        def start_rows(t, _):
            for k in range(MOE_TOPK):
                d = dest_ref[k * n_tok + base + t]
                pltpu.make_async_copy(y_ref.at[pl.ds(pl.multiple_of(d * PACKED_SUB, PACKED_SUB), PACKED_SUB)],
                                      ybuf.at[into, k, pl.ds(pl.multiple_of(t * PACKED_SUB, PACKED_SUB), PACKED_SUB)],
                                      sems.at[into]).start(priority=k % DMA_PRIORITIES)
            return 0

        lax.fori_loop(0, COMBINE_TILE, start_rows, 0, unroll=ISSUE_UNROLL)

    @pl.when(i == 0)
    def _():
        gather_tile(0, 0)

    @pl.when(i + 1 < pl.num_programs(0))
    def _():
        gather_tile(i + 1, 1 - buf)

    for k in range(MOE_TOPK):
        pltpu.make_async_copy(y_ref.at[pl.ds(0, COMBINE_TILE * PACKED_SUB)], ybuf.at[buf, k], sems.at[buf]).wait()

    gates = gates_ref[...]
    ffn = gates[:, 0:1] * _unpack_tiles_to_rows(ybuf.at[buf, 0], COMBINE_TILE).astype(F32)
    for k in range(1, MOE_TOPK):
        ffn = ffn + gates[:, k:k + 1] * _unpack_tiles_to_rows(ybuf.at[buf, k], COMBINE_TILE).astype(F32)
    o_ref[...] = _layer_norm(alpha * x1_ref[...] + ffn, g_ref[...], b_ref[...])


def _combine(dest, gates, x1, g2, b2, y_rows, alpha):
    N, D = x1.shape
    row = lambda i, d: (i, 0)
    full = lambda i, d: (0, 0)
    return pl.pallas_call(
        functools.partial(_combine_kernel, alpha),
        out_shape=jax.ShapeDtypeStruct((N, D), F32),
        grid_spec=pltpu.PrefetchScalarGridSpec(
            num_scalar_prefetch=1,
            grid=(N // COMBINE_TILE,),
            in_specs=[pl.BlockSpec((COMBINE_TILE, LANES), row), pl.BlockSpec((COMBINE_TILE, D), row),
                      pl.BlockSpec(g2.shape, full), pl.BlockSpec(b2.shape, full),
                      pl.BlockSpec(memory_space=pl.ANY)],
            out_specs=pl.BlockSpec((COMBINE_TILE, D), row),
            scratch_shapes=[pltpu.VMEM((2, MOE_TOPK, COMBINE_TILE * PACKED_SUB, LANES), y_rows.dtype),
                            pltpu.SemaphoreType.DMA((2,))],
        ),
        compiler_params=pltpu.CompilerParams(dimension_semantics=("arbitrary",),
                                             vmem_limit_bytes=VMEM_LIMIT_BYTES),
        name="combine",
    )(dest, gates, x1, g2, b2, y_rows)


def _rot_partner(w, half):
    return jnp.concatenate([-w[..., half:2 * half], w[..., :half]], axis=-1)


def _layer_weights(w_in, w_q_b, w_kv_b):
    D = w_in.shape[0]
    o1 = Q_LORA
    o2 = o1 + KV_LORA
    o3 = o2 + MLA_ROPE
    w_ql, w_kvl, w_kr = w_in[:, :o1], w_in[:, o1:o2], w_in[:, o2:o3]
    w_mq, w_mk, w_mv = w_in[:, o3:o3 + _MH], w_in[:, o3 + _MH:o3 + 2 * _MH], w_in[:, o3 + 2 * _MH:]
    zpad = lambda n: jnp.zeros((D, n), w_in.dtype)
    tail = SLOT - MLA_NOPE - MLA_ROPE
    assert tail == MLA_ROPE
    kr_slot = jnp.concatenate([zpad(MLA_NOPE), w_kr, _rot_partner(w_kr, MLA_ROPE // 2)], axis=1)

    win = jnp.concatenate([w_ql, w_kvl, kr_slot, w_mk], axis=1).astype(BF16)
    wmt = jnp.concatenate([w_mq, w_mv], axis=1).T.astype(BF16)

    wkv3 = w_kv_b.reshape(KV_LORA, MLA_HEADS, MLA_NOPE + MLA_V)
    wk = jnp.concatenate([wkv3[..., :MLA_NOPE], jnp.zeros((KV_LORA, MLA_HEADS, SLOT - MLA_NOPE), w_kv_b.dtype)],
                         axis=-1).reshape(KV_LORA, MLA_HEADS * SLOT).astype(BF16)
    wv = wkv3[..., MLA_NOPE:].reshape(KV_LORA, MLA_HEADS * MLA_V)
    return win, wmt, w_q_b.T.astype(BF16), wk, wv.T.astype(BF16)


def _rope_tables(positions):
    pos = positions.astype(F32).reshape(1, -1)

    def cs(d_rot):
        inv_freq = ROPE_THETA ** (-jnp.arange(0, d_rot, 2, dtype=F32) / d_rot)
        ang = inv_freq.reshape(d_rot // 2, 1) * pos
        return jnp.cos(ang), jnp.sin(ang)

    return cs(MLA_ROPE) + cs(MOBA_ROT)


def kernel(x, positions, w_in, q_a_norm, w_q_b, kv_a_norm, w_kv_b, w_o, ln1_g, ln1_b, w_router, b_router,
           w_gate, b_gate, w_up, b_up, w_down, b_down, ln2_g, ln2_b):
    B, T, D = x.shape
    depth = w_in.shape[0]
    alpha = (2.0 * depth) ** 0.25
    N = B * T
    assert T % MOBA_BLOCK == 0 and T // MOBA_BLOCK <= SUBLANES and N % OPROJ_TILE == 0
    assert D == SUBLANES * LANES
    n_asg = N * MOE_TOPK
    n_rows = n_asg + N_EXPERTS * MOE_GROUP
    tables = _rope_tables(positions)
    h = x.reshape(N, D)
    for l in range(depth):
        win, wmt, wqt, wk, wvt = _layer_weights(w_in[l], w_q_b[l], w_kv_b[l])
        qt, k, vt = _prep(h, win, wmt, wqt, wk, wvt, q_a_norm[l].reshape(1, -1), kv_a_norm[l].reshape(1, -1), tables, B, T)
        a = _attention(qt, k, vt, B, T, "attention")
        wr_pad = jnp.concatenate([w_router[l], jnp.zeros((D, LANES - N_EXPERTS), F32)], axis=1)
        wr_hi = wr_pad.astype(BF16)
        wr = jnp.concatenate([wr_hi, (wr_pad - wr_hi.astype(F32)).astype(BF16)], axis=1)
        br = b_router[l].reshape(N_EXPERTS, 1)
        x1, x1t, route, gates, cnt = _oproj(a, h, w_o[l].astype(BF16), ln1_g[l].reshape(1, D), ln1_b[l].reshape(1, D),
                                       wr, br, alpha)
        er = jnp.arange(N_EXPERTS, dtype=jnp.int32)
        counts = cnt[:, 0].astype(jnp.int32)
        padded = (counts + MOE_GROUP - 1) // MOE_GROUP * MOE_GROUP
        upto = er[None, :] <= er[:, None]
        pad_end = jnp.sum(jnp.where(upto, padded[None, :], 0), axis=1)
        pad_start = pad_end - padded
        e_idx = route[:MOE_TOPK]
        group_start = jnp.sum(jnp.where(e_idx[..., None] == er, pad_start, 0), axis=-1)
        dest = (group_start + route[MOE_TOPK:2 * MOE_TOPK]).reshape(n_asg)
        n_used = (pad_end[-1:] // MOE_GROUP).astype(jnp.int32)
        nblk = n_rows // MOE_GROUP
        blk_start = jnp.arange(nblk, dtype=jnp.int32) * MOE_GROUP
        blk_expert = jnp.minimum(jnp.sum((blk_start[:, None] >= pad_end[None, :]).astype(jnp.int32), axis=1),
                                 N_EXPERTS - 1)
        of_expert = blk_expert[:, None] == er[None, :]
        pick = lambda per_expert: jnp.sum(jnp.where(of_expert, per_expert[None, :], 0), axis=1)
        nonempty = padded > 0
        group_index = jnp.sum(jnp.where(upto & nonempty[None, :], 1, 0), axis=1) - 1
        later = (er[None, :] > er[:, None]) & nonempty[None, :]
        next_expert = jnp.min(jnp.where(later, er[None, :], N_EXPERTS), axis=1)
        next_expert = jnp.where(next_expert == N_EXPERTS, -1, next_expert)
        blk_slot = pick(group_index) % 2
        blk_next = pick(next_expert)
        blk_full = (pick(pad_start + counts) > blk_start + MOE_HALF).astype(jnp.int32)
        half_start = jnp.arange(n_rows // MOE_HALF, dtype=jnp.int32) * MOE_HALF
        half_expert = jnp.minimum(jnp.sum((half_start[:, None] >= pad_end[None, :]).astype(jnp.int32), axis=1),
                                  N_EXPERTS - 1)
        half_real_end = jnp.sum(jnp.where(half_expert[:, None] == er[None, :], (pad_start + counts)[None, :], 0), axis=1)
        no_real_rows = (half_start >= pad_end[-1]) | (half_start >= half_real_end)
        fill = jnp.where(no_real_rows, FILL_FREE,
                         jnp.where(half_start + MOE_HALF > half_real_end, FILL_TAIL, 0)).astype(jnp.int32)
        xs = _dispatch(dest, fill, x1t, n_rows)
        y_rows = _experts(blk_expert, blk_slot.astype(jnp.int32), blk_next, blk_full, n_used, xs, w_gate[l], b_gate[l].reshape(N_EXPERTS, 1, -1),
                          w_up[l], b_up[l].reshape(N_EXPERTS, 1, -1), w_down[l], b_down[l].reshape(N_EXPERTS, 1, -1))
        h = _combine(dest, gates, x1, ln2_g[l].reshape(1, D), ln2_b[l].reshape(1, D), y_rows, alpha)
    return h.reshape(B, T, D)
```

```python
import functools
import math

import jax
import jax.numpy as jnp
from jax import lax
from jax.experimental import pallas as pl
from jax.experimental.pallas import tpu as pltpu

ROPE_THETA = 500000.0
MLA_HEADS = 8
MLA_NOPE = 64
MLA_ROPE = 32
MLA_V = 64
Q_LORA = 256
KV_LORA = 128
MOBA_HEADS = 8
MOBA_HD = 64
MOBA_ROT = MOBA_HD // 4
MOBA_BLOCK = 256
MOBA_TOPK = 3
N_EXPERTS = 32
MOE_TOPK = 4
SWIGLU_LIMIT = 7.0
SWIGLU_ALPHA = 1.702
RMS_EPS = 1e-6
LN_EPS = 1e-5

LANES = 128
SUBLANES = 8
VMEM_LIMIT_BYTES = 56 * 1024 * 1024

SLOT = LANES
TQ = MOBA_BLOCK
ROW_TILE = 256
OPROJ_TILE = 1024
MOE_GROUP = 512
MOE_HALF = MOE_GROUP // 2
PACKED_SUB = SUBLANES // 2
FILL_TAIL, FILL_FREE = 1, 2
DISPATCH_TILE = 1024
COMBINE_TILE = 512
ISSUE_UNROLL = 4
DMA_PRIORITIES = 2
WEIGHT_DMA_PRIORITY = 1
HEAD_LANES = 64
ONES_LANE = 64
BIAS_LANE = 64
ATTN_HEADS_PER_STEP = 8
LOG2E = math.log2(math.e)
NEG_BIG = -(2.0 ** 100)

F32 = jnp.float32
BF16 = jnp.bfloat16
NT_DIMS = (((1,), (1,)), ((), ()))
TN_DIMS = (((0,), (0,)), ((), ()))


def _dot(a, b, precision=None):
    return jnp.dot(a, b, preferred_element_type=F32, precision=precision)


def _dot_nt(a, b, precision=None):
    return lax.dot_general(a, b, NT_DIMS, preferred_element_type=F32, precision=precision)


def _pack_rows_to_tiles(ref, x):
    rows, width = x.shape
    bits = lambda v: lax.bitcast_convert_type(v.astype(BF16).astype(F32), jnp.uint32)
    words = (bits(x[:, :width // 2]) >> 16) | (bits(x[:, width // 2:]) & jnp.uint32(0xFFFF0000))
    for c in range(PACKED_SUB):
        ref[pl.ds(c, rows, stride=PACKED_SUB), :] = words[:, c * LANES:(c + 1) * LANES]


def _unpack_tiles_to_rows(ref, rows):
    words = jnp.concatenate([ref[pl.ds(c, rows, stride=PACKED_SUB), :] for c in range(PACKED_SUB)], axis=1)
    low = lax.bitcast_convert_type(words << 16, F32).astype(BF16)
    high = lax.bitcast_convert_type(words & jnp.uint32(0xFFFF0000), F32).astype(BF16)
    return jnp.concatenate([low, high], axis=1)


def _rms(x, g):
    return x * lax.rsqrt(jnp.mean(x * x, axis=-1, keepdims=True) + RMS_EPS) * g


def _layer_norm(x, g, b):
    mu = jnp.mean(x, axis=-1, keepdims=True)
    xc = x - mu
    var = jnp.mean(xc * xc, axis=-1, keepdims=True)
    return xc * lax.rsqrt(var + LN_EPS) * g + b


_C_QL = 0
_C_KVL = _C_QL + Q_LORA
_C_KR = _C_KVL + KV_LORA
_C_MK = _C_KR + SLOT
_MH = MOBA_HEADS * MOBA_HD
_C_END = _C_MK + _MH


def _prep_kernel(x_ref, win_ref, wmt_ref, wqt_ref, wk_ref, wvt_ref, qg_ref, kvg_ref,
                 cat_ref, sat_ref, cbt_ref, sbt_ref,
                 qt_ref, k_ref, vt_ref, kmean_scr):
    c = pl.program_id(1)
    width = MLA_HEADS * SLOT
    qat_ref, mqt_ref = qt_ref.at[:, 0:width, :], qt_ref.at[:, width:2 * width, :]
    ka_ref, mk_ref = k_ref.at[:, 0:width], k_ref.at[:, width:2 * width]
    vat_ref, mvt_ref = vt_ref.at[:, 0:width, :], vt_ref.at[:, width:2 * width, :]

    @pl.when(c == 0)
    def _():
        kmean_scr[...] = jnp.zeros_like(kmean_scr)

    xb = x_ref[...].astype(BF16)
    both_halves = lambda t: jnp.concatenate([t, t], axis=0)
    cat = both_halves(cat_ref[...])
    sat = both_halves(sat_ref[...])
    cbt = both_halves(cbt_ref[...])
    sat_signed = jnp.concatenate([-sat_ref[...], sat_ref[...]], axis=0)
    sbt_signed = jnp.concatenate([-sbt_ref[...], sbt_ref[...]], axis=0)

    def rotated_t(r, cos, sin_signed):
        half = r.shape[0] // 2
        return r * cos + jnp.concatenate([r[half:], r[:half]], axis=0) * sin_signed

    const = lambda v, n: jnp.full((n, ROW_TILE), v, F32)
    tail = SLOT - MLA_NOPE - MLA_ROPE
    rest = MOBA_HD - MOBA_ROT
    ca = jnp.concatenate([const(1.0, MLA_NOPE), cat, const(0.0, tail)], axis=0).T
    sa = jnp.concatenate([const(0.0, MLA_NOPE), sat, const(0.0, tail)], axis=0).T
    cb = jnp.concatenate([cbt, const(1.0, rest), cbt, const(1.0, rest)], axis=0).T
    sb = jnp.concatenate([sbt_signed, const(0.0, rest), sbt_signed, const(0.0, rest)], axis=0).T
    lane = lax.broadcasted_iota(jnp.int32, (ROW_TILE, SLOT), 1)
    head_lanes = lane < HEAD_LANES
    first_half = (lane % MOBA_HD) < MOBA_ROT // 2
    ones_rows = (lax.broadcasted_iota(jnp.int32, (SLOT - HEAD_LANES, ROW_TILE), 0) == ONES_LANE - HEAD_LANES).astype(F32)

    ql = _dot(xb, win_ref[:, _C_QL:_C_KVL])
    kvl = _dot(xb, win_ref[:, _C_KVL:_C_KR])
    kr = _dot(xb, win_ref[:, _C_KR:_C_MK])
    krs = pltpu.roll(kr, SLOT - MLA_ROPE, axis=1)
    qn = _rms(ql, qg_ref[...]).astype(BF16)
    kvn = _rms(kvl, kvg_ref[...]).astype(BF16)
    q_t = _dot_nt(wqt_ref[...], qn)
    kn = _dot(kvn, wk_ref[...])
    v_t = _dot_nt(wvt_ref[...], kvn)
    scale_a = LOG2E / math.sqrt(MLA_NOPE + MLA_ROPE)
    kro = kr * ca + krs * sa
    qd = MLA_NOPE + MLA_ROPE
    q_pad = jnp.zeros((SLOT - qd, ROW_TILE), F32)
    for h in range(MLA_HEADS):
        sl = slice(h * SLOT, (h + 1) * SLOT)
        q_rope = rotated_t(q_t[h * qd + MLA_NOPE:(h + 1) * qd], cat, sat_signed)
        q_slot = jnp.concatenate([q_t[h * qd:h * qd + MLA_NOPE], q_rope, q_pad], axis=0)
        qat_ref[0, sl, :] = (q_slot * scale_a).astype(BF16)
        j, hh = divmod(h, 2)
        kn_h = kn[:, j * LANES:(j + 1) * LANES]
        kn_h = kn_h if hh == 0 else pltpu.roll(kn_h, HEAD_LANES, axis=1)
        ka_ref[:, sl] = jnp.where(head_lanes, kn_h, kro).astype(BF16)
        vat_ref[0, sl, :] = jnp.concatenate([v_t[h * MLA_V:(h + 1) * MLA_V], ones_rows], axis=0).astype(BF16)

    mk = _dot(xb, win_ref[:, _C_MK:_C_END])
    npair = _MH // LANES
    pair = lambda a, j: a[:, j * LANES:(j + 1) * LANES]

    def rotated(a):
        x2 = pltpu.roll(a, LANES - MOBA_ROT // 2, axis=1)
        x1 = pltpu.roll(a, MOBA_ROT // 2, axis=1)
        return a * cb + jnp.where(first_half, x2, x1) * sb

    mk_rot = [rotated(pair(mk, j)) for j in range(npair)]
    mk_all = jnp.concatenate(mk_rot, axis=1)

    mq_t = _dot_nt(wmt_ref[0:_MH, :], xb)
    mv_t = _dot_nt(wmt_ref[_MH:2 * _MH, :], xb)
    head = lambda a, h: a[h * MOBA_HD:(h + 1) * MOBA_HD]
    mq_rot_t = [jnp.concatenate([rotated_t(head(mq_t, h)[0:MOBA_ROT], cbt, sbt_signed),
                                 head(mq_t, h)[MOBA_ROT:]], axis=0) for h in range(MOBA_HEADS)]

    nrow = MOBA_HEADS * SUBLANES
    row_i = lax.broadcasted_iota(jnp.int32, (nrow, _MH), 0)
    lane_i = lax.broadcasted_iota(jnp.int32, (nrow, _MH), 1)
    kmean_c = jnp.mean(mk_all, axis=0, keepdims=True)
    put = ((row_i % SUBLANES) == c) & ((lane_i // MOBA_HD) == (row_i // SUBLANES))
    table = kmean_scr[...]
    split = lambda a: (a.astype(BF16), (a - a.astype(BF16).astype(F32)).astype(BF16))
    t_hi, t_lo = split(table)
    q_hi, q_lo = split(jnp.concatenate(mq_rot_t, axis=0))
    gate_t = _dot(jnp.concatenate([t_hi, t_lo], axis=0), q_hi)
    gate_t = gate_t[0:nrow] + gate_t[nrow:] + _dot(t_hi, q_lo)
    kmean_scr[...] = jnp.where(put, jnp.broadcast_to(kmean_c, (nrow, _MH)), table)

    n_idx = lax.broadcasted_iota(jnp.int32, (SUBLANES, ROW_TILE), 0)
    valid = n_idx < c
    scale_b = LOG2E / math.sqrt(MOBA_HD)
    zero_rows = jnp.zeros((SLOT - HEAD_LANES - SUBLANES, ROW_TILE), F32)
    block_onehot = (lane == BIAS_LANE + c).astype(F32)
    for h in range(MOBA_HEADS):
        g = jnp.where(valid, gate_t[h * SUBLANES:(h + 1) * SUBLANES, :], -jnp.inf)
        rank = jnp.zeros((SUBLANES, ROW_TILE), jnp.int32)
        for k in range(1, SUBLANES):
            other = pltpu.roll(g, k, axis=0)
            other_n = pltpu.roll(n_idx, k, axis=0)
            beats = (other > g) | ((other == g) & (other_n < n_idx))
            rank = rank + beats.astype(jnp.int32)
        keep = (valid & (rank < MOBA_TOPK)) | (n_idx == c)
        bias = jnp.where(keep, 0.0, NEG_BIG)
        sl = slice(h * SLOT, (h + 1) * SLOT)
        mqt_ref[0, sl, :] = jnp.concatenate([mq_rot_t[h] * scale_b, bias, zero_rows], axis=0).astype(BF16)
        mvt_ref[0, sl, :] = jnp.concatenate([head(mv_t, h), ones_rows], axis=0).astype(BF16)
        j, hh = divmod(h, 2)
        k_h = mk_rot[j] if hh == 0 else pltpu.roll(mk_rot[j], HEAD_LANES, axis=1)
        mk_ref[:, sl] = jnp.where(head_lanes, k_h, block_onehot).astype(BF16)


def _prep(x2, win, wmt, wqt, wk, wvt, qg, kvg, tables, B, T):
    N, D = x2.shape
    nt = T // ROW_TILE
    row = lambda b, c: (b * nt + c, 0)
    col = lambda b, c: (0, b * nt + c)
    full = lambda b, c: (0, 0)
    cat, sat, cbt, sbt = tables
    width = MLA_HEADS * SLOT

    def rows(w):
        return pl.BlockSpec((ROW_TILE, w), row)

    def cols(a):
        return pl.BlockSpec((a.shape[0], ROW_TILE), col)

    def whole(a):
        return pl.BlockSpec(a.shape, full)

    width = 2 * width
    rowmajor = jax.ShapeDtypeStruct((N, width), BF16)
    transposed = jax.ShapeDtypeStruct((B * nt, width, ROW_TILE), BF16)
    t_spec = pl.BlockSpec((1, width, ROW_TILE), lambda b, c: (b * nt + c, 0, 0))
    return pl.pallas_call(
        _prep_kernel,
        out_shape=(transposed, rowmajor, transposed),
        grid=(B, nt),
        in_specs=[rows(D), whole(win), whole(wmt), whole(wqt), whole(wk), whole(wvt), whole(qg), whole(kvg),
                  cols(cat), cols(sat), cols(cbt), cols(sbt)],
        out_specs=(t_spec, rows(width), t_spec),
        scratch_shapes=[pltpu.VMEM((MOBA_HEADS * SUBLANES, _MH), F32)],
        compiler_params=pltpu.CompilerParams(dimension_semantics=("arbitrary", "arbitrary"),
                                             vmem_limit_bytes=VMEM_LIMIT_BYTES),
        name="prep",
    )(x2, win, wmt, wqt, wk, wvt, qg, kvg, cat, sat, cbt, sbt)


def _attn_kernel(qt_ref, k_ref, vt_ref, o_ref, s_scr, acc_scr):
    i = pl.program_id(2)
    nh = ATTN_HEADS_PER_STEP
    slot = lambda h: slice(h * SLOT, (h + 1) * SLOT)
    nblk = i + 1
    nquad = nblk // 4
    pair_end = 4 * nquad + 2 * ((nblk - 4 * nquad) // 2)
    q_pos = i * TQ + lax.broadcasted_iota(jnp.int32, (1, TQ), 1)

    def fold(s):
        out = s[0:SUBLANES]
        for t in range(1, s.shape[0] // SUBLANES):
            out = jnp.maximum(out, s[t * SUBLANES:(t + 1) * SUBLANES])
        return out

    def score_blocks(j, n, mrun):
        j0 = pl.multiple_of(j * TQ, TQ)
        visible = (j0 + lax.broadcasted_iota(jnp.int32, (n * TQ, TQ), 0)) <= q_pos
        out = []
        for h in range(nh):
            s = _dot(k_ref[pl.ds(j0, n * TQ), slot(h)], qt_ref[0, slot(h), :])
            s = jnp.where(visible, s, -jnp.inf)
            s_scr[h, pl.ds(j, n)] = s.reshape(n, TQ, TQ)
            out.append(jnp.maximum(mrun[h], fold(s)))
        return tuple(out)

    mrun = tuple(jnp.full((SUBLANES, TQ), -jnp.inf, F32) for _ in range(nh))
    mrun = lax.fori_loop(0, nquad, lambda jq, m: score_blocks(4 * jq, 4, m), mrun)
    mrun = lax.fori_loop(2 * nquad, pair_end // 2, lambda jj, m: score_blocks(2 * jj, 2, m), mrun)
    mrun = lax.fori_loop(pair_end, nblk, lambda j, m: score_blocks(j, 1, m), mrun)
    ms = [jnp.max(m, axis=0, keepdims=True) for m in mrun]

    def accumulate(j, n, carry):
        for h in range(nh):
            p = jnp.exp2(s_scr[h, pl.ds(j, n)].reshape(n * TQ, TQ) - ms[h]).astype(BF16)
            vt = jnp.concatenate([vt_ref[j + t, slot(h), :] for t in range(n)], axis=1)
            acc_scr[h] += _dot(vt, p)
        return carry

    acc_scr[...] = jnp.zeros_like(acc_scr)
    lax.fori_loop(0, nquad, lambda jq, c: accumulate(4 * jq, 4, c), 0)
    lax.fori_loop(2 * nquad, pair_end // 2, lambda jj, c: accumulate(2 * jj, 2, c), 0)
    lax.fori_loop(pair_end, nblk, lambda j, c: accumulate(j, 1, c), 0)
    outs = [acc_scr[h, 0:HEAD_LANES] / acc_scr[h, ONES_LANE:ONES_LANE + 1] for h in range(nh)]
    for h in range(nh):
        o_ref[0, h * HEAD_LANES:(h + 1) * HEAD_LANES, :] = outs[h].astype(o_ref.dtype)


def _attention(qt, k, vt, B, T, name):
    N = k.shape[0]
    nq = T // TQ
    nh = ATTN_HEADS_PER_STEP
    heads = k.shape[1] // SLOT
    return pl.pallas_call(
        _attn_kernel,
        out_shape=jax.ShapeDtypeStruct((B * nq, heads * HEAD_LANES, TQ), BF16),
        grid=(B, heads // nh, nq),
        in_specs=[pl.BlockSpec((1, nh * SLOT, TQ), lambda b, g, i: (b * nq + i, g, 0)),
                  pl.BlockSpec((T, nh * SLOT), lambda b, g, i: (b, g)),
                  pl.BlockSpec((nq, nh * SLOT, TQ), lambda b, g, i: (b, g, 0))],
        out_specs=pl.BlockSpec((1, nh * HEAD_LANES, TQ), lambda b, g, i: (b * nq + i, g, 0)),
        scratch_shapes=[pltpu.VMEM((nh, nq, TQ, TQ), F32), pltpu.VMEM((nh, SLOT, TQ), F32)],
        compiler_params=pltpu.CompilerParams(dimension_semantics=("arbitrary", "arbitrary", "arbitrary"),
                                             vmem_limit_bytes=VMEM_LIMIT_BYTES),
        name=name,
    )(qt, k, vt)


def _oproj_kernel(alpha, a_ref, x_ref, wo_ref, g_ref, b_ref, wr_ref, br_ref,
                  x1_ref, x1t_ref, route_ref, gates_ref, cnt_ref, carry_scr):
    i = pl.program_id(0)

    @pl.when(i == 0)
    def _():
        carry_scr[...] = jnp.zeros_like(carry_scr)

    parts = []
    for t in range(a_ref.shape[0]):
        parts.append(lax.dot_general(a_ref[t], wo_ref[...], TN_DIMS, preferred_element_type=F32))
    mix = jnp.concatenate(parts, axis=0)
    x1 = _layer_norm(alpha * x_ref[...] + mix, g_ref[...], b_ref[...])
    x1_ref[...] = x1
    _pack_rows_to_tiles(x1t_ref, x1)

    ne = br_ref.shape[0]
    rows = x1.shape[0]
    x_hi = x1.astype(BF16)
    x_lo = (x1 - x_hi.astype(F32)).astype(BF16)
    both = _dot(x_hi, wr_ref[...])
    logits_rm = both[:, :LANES] + both[:, LANES:] + _dot(x_lo, wr_ref[:, :LANES])
    logits = logits_rm.T[0:ne] + br_ref[...]
    expert = lax.broadcasted_iota(jnp.int32, logits.shape, 0)
    expert_f = expert.astype(F32)
    vals, idxs = [], []
    work = logits
    for _ in range(MOE_TOPK):
        mx = jnp.max(work, axis=0, keepdims=True)
        ix = jnp.min(jnp.where(work == mx, expert_f, float(ne)), axis=0, keepdims=True).astype(jnp.int32)
        vals.append(mx)
        idxs.append(ix)
        work = jnp.where(expert == ix, -jnp.inf, work)
    exps = [jnp.exp(v - vals[0]) for v in vals]
    den = exps[0]
    for e in exps[1:]:
        den = den + e

    onehot = jnp.zeros(logits.shape, F32)
    for ix in idxs:
        onehot = onehot + (expert == ix).astype(F32)
    r = lax.broadcasted_iota(jnp.int32, (rows, rows), 0)
    cidx = lax.broadcasted_iota(jnp.int32, (rows, rows), 1)
    earlier = (r < cidx).astype(BF16)
    carry = carry_scr[:, 0:1]
    before = _dot(onehot.astype(BF16), earlier) + carry
    row8 = lax.broadcasted_iota(jnp.int32, (SUBLANES, rows), 0)
    route = jnp.zeros((SUBLANES, rows), jnp.int32)
    gates = jnp.zeros((SUBLANES, rows), F32)
    for k in range(MOE_TOPK):
        rank = jnp.sum(jnp.where(expert == idxs[k], before, 0.0), axis=0, keepdims=True).astype(jnp.int32)
        route = jnp.where(row8 == k, idxs[k], route)
        route = jnp.where(row8 == MOE_TOPK + k, rank, route)
        gates = jnp.where(row8 == k, exps[k] / den, gates)
    route_ref[...] = route
    gates_ref[...] = jnp.concatenate([gates, jnp.zeros((LANES - SUBLANES, rows), F32)], axis=0).T
    new_carry = carry + jnp.sum(onehot, axis=1, keepdims=True)
    carry_scr[...] = jnp.broadcast_to(new_carry, carry_scr.shape)
    cnt_ref[...] = jnp.broadcast_to(new_carry, cnt_ref.shape)


def _oproj(a, x2, wo, g1, b1, wr, br, alpha):
    N, D = x2.shape
    nt = N // OPROJ_TILE
    row = lambda i: (i, 0)
    full = lambda i: (0, 0)
    return pl.pallas_call(
        functools.partial(_oproj_kernel, alpha),
        out_shape=(jax.ShapeDtypeStruct((N, D), F32),
                   jax.ShapeDtypeStruct((N * PACKED_SUB, LANES), jnp.uint32),
                   jax.ShapeDtypeStruct((SUBLANES, N), jnp.int32),
                   jax.ShapeDtypeStruct((N, LANES), F32),
                   jax.ShapeDtypeStruct((br.shape[0], LANES), F32)),
        grid=(nt,),
        in_specs=[pl.BlockSpec((OPROJ_TILE // TQ,) + a.shape[1:], lambda i: (i, 0, 0)),
                  pl.BlockSpec((OPROJ_TILE, D), row), pl.BlockSpec(wo.shape, full),
                  pl.BlockSpec(g1.shape, full), pl.BlockSpec(b1.shape, full),
                  pl.BlockSpec(wr.shape, full), pl.BlockSpec(br.shape, full)],
        out_specs=(pl.BlockSpec((OPROJ_TILE, D), row), pl.BlockSpec((OPROJ_TILE * PACKED_SUB, LANES), row),
                   pl.BlockSpec((SUBLANES, OPROJ_TILE), lambda i: (0, i)),
                   pl.BlockSpec((OPROJ_TILE, LANES), row), pl.BlockSpec((br.shape[0], LANES), full)),
        scratch_shapes=[pltpu.VMEM((br.shape[0], LANES), F32)],
        compiler_params=pltpu.CompilerParams(dimension_semantics=("arbitrary",),
                                             vmem_limit_bytes=VMEM_LIMIT_BYTES),
        name="oproj_router",
    )(a, x2, wo, g1, b1, wr, br)


def _dispatch_kernel(dest_ref, fill_ref, x1_ref, xs_ref, zero_scr, sem, fill_sem):
    i = pl.program_id(0)
    fill_sub = MOE_HALF * PACKED_SUB
    nfill = xs_ref.shape[0] // fill_sub

    def fill_copy(blk, on):
        return pltpu.make_async_copy(zero_scr, xs_ref.at[pl.ds(pl.multiple_of(blk * fill_sub, fill_sub), fill_sub)], on)

    def for_blocks(kind, action):
        def body(blk, _):
            @pl.when(fill_ref[blk] == kind)
            def _():
                action(blk)
            return 0

        lax.fori_loop(0, nfill, body, 0)

    @pl.when(i == 0)
    def _():
        zero_scr[...] = jnp.zeros_like(zero_scr)
        for_blocks(FILL_TAIL, lambda blk: fill_copy(blk, sem).start())
        for_blocks(FILL_FREE, lambda blk: fill_copy(blk, fill_sem).start())
        for_blocks(FILL_TAIL, lambda blk: fill_copy(blk, sem).wait())

    base = i * DISPATCH_TILE
    n_tok = pl.num_programs(0) * DISPATCH_TILE

    def row_copy(t, k):
        d = dest_ref[k * n_tok + base + t]
        return pltpu.make_async_copy(x1_ref.at[pl.ds(pl.multiple_of(t * PACKED_SUB, PACKED_SUB), PACKED_SUB)],
                                     xs_ref.at[pl.ds(pl.multiple_of(d * PACKED_SUB, PACKED_SUB), PACKED_SUB)], sem)

    def start_rows(t, _):
        for k in range(MOE_TOPK):
            row_copy(t, k).start(priority=k % DMA_PRIORITIES)
        return 0

    lax.fori_loop(0, DISPATCH_TILE, start_rows, 0, unroll=ISSUE_UNROLL)
    for k in range(MOE_TOPK):
        pltpu.make_async_copy(x1_ref, xs_ref.at[pl.ds(0, DISPATCH_TILE * PACKED_SUB)], sem).wait()

    @pl.when(i == pl.num_programs(0) - 1)
    def _():
        for_blocks(FILL_FREE, lambda blk: fill_copy(blk, fill_sem).wait())


def _dispatch(dest, fill, x1t, n_rows):
    return pl.pallas_call(
        _dispatch_kernel,
        out_shape=jax.ShapeDtypeStruct((n_rows * PACKED_SUB, LANES), x1t.dtype),
        grid_spec=pltpu.PrefetchScalarGridSpec(
            num_scalar_prefetch=2,
            grid=(x1t.shape[0] // (DISPATCH_TILE * PACKED_SUB),),
            in_specs=[pl.BlockSpec((DISPATCH_TILE * PACKED_SUB, LANES), lambda i, d, t: (i, 0))],
            out_specs=pl.BlockSpec(memory_space=pl.ANY),
            scratch_shapes=[pltpu.VMEM((MOE_HALF * PACKED_SUB, LANES), x1t.dtype), pltpu.SemaphoreType.DMA(()),
                            pltpu.SemaphoreType.DMA(())],
        ),
        compiler_params=pltpu.CompilerParams(dimension_semantics=("arbitrary",),
                                             vmem_limit_bytes=VMEM_LIMIT_BYTES),
        name="dispatch",
    )(dest, fill, x1t)


def _experts_kernel(be_ref, slot_ref, nxt_ref, full_ref, nused_ref, x_ref, wg_hbm, bg_ref, wu_hbm, bu_ref, wd_hbm, bd_ref,
                    y_ref, wg_st, wu_st, wd_st, wg_bf, wu_bf, wd_bf, sems):
    i = pl.program_id(0)
    prev = be_ref[jnp.maximum(i - 1, 0)]
    changed = (i == 0) | (be_ref[i] != prev)
    active = i < nused_ref[0]
    slot = slot_ref[i]

    def weight_copies(expert, s):
        return (pltpu.make_async_copy(wg_hbm.at[expert], wg_st.at[s], sems.at[s, 0]),
                pltpu.make_async_copy(wu_hbm.at[expert], wu_st.at[s], sems.at[s, 1]),
                pltpu.make_async_copy(wd_hbm.at[expert], wd_st.at[s], sems.at[s, 2]))

    @pl.when(i == 0)
    def _():
        for cp in weight_copies(be_ref[0], slot):
            cp.start()

    @pl.when(active & changed)
    def _():
        for cp in weight_copies(be_ref[i], slot):
            cp.wait()

        @pl.when(nxt_ref[i] >= 0)
        def _():
            for cp in weight_copies(nxt_ref[i], 1 - slot):
                cp.start(priority=WEIGHT_DMA_PRIORITY)

        wg_bf[...] = wg_st[slot].astype(BF16)
        wu_bf[...] = wu_st[slot].astype(BF16)
        wd_bf[...] = wd_st[slot].astype(BF16)

    def mlp(nrows):
        rows = pl.ds(0, nrows * PACKED_SUB)
        xb = _unpack_tiles_to_rows(x_ref.at[rows], nrows)
        g = jnp.minimum(_dot(xb, wg_bf[...]) + bg_ref[0], SWIGLU_LIMIT)
        u = jnp.clip(_dot(xb, wu_bf[...]) + bu_ref[0], -SWIGLU_LIMIT, SWIGLU_LIMIT)
        h = g * (1.0 / (1.0 + jnp.exp(-SWIGLU_ALPHA * g))) * (u + 1.0)
        _pack_rows_to_tiles(y_ref.at[rows], _dot(h.astype(BF16), wd_bf[...]) + bd_ref[0])

    whole = active & (full_ref[i] > 0)

    @pl.when(whole)
    def _():
        mlp(MOE_GROUP)

    @pl.when(active & jnp.logical_not(whole))
    def _():
        mlp(MOE_HALF)

    @pl.when(jnp.logical_not(active))
    def _():
        y_ref[pl.ds(0, MOE_HALF * PACKED_SUB), :] = jnp.zeros((MOE_HALF * PACKED_SUB, LANES), y_ref.dtype)

    @pl.when(jnp.logical_not(whole))
    def _():
        y_ref[pl.ds(MOE_HALF * PACKED_SUB, MOE_HALF * PACKED_SUB), :] = jnp.zeros((MOE_HALF * PACKED_SUB, LANES), y_ref.dtype)


def _experts(blk_expert, blk_slot, blk_next, blk_full, n_used, xs, wg, bg, wu, bu, wd, bd):
    E, D, F = wg.shape
    nblk = xs.shape[0] // (MOE_GROUP * PACKED_SUB)

    def rowmap(i, be, sl, nx, fl, nu):
        return (jnp.minimum(i, nu[0] - 1), 0)

    def bmap(i, be, sl, nx, fl, nu):
        return (be[i], 0, 0)

    hbm = pl.BlockSpec(memory_space=pl.ANY)
    return pl.pallas_call(
        _experts_kernel,
        out_shape=jax.ShapeDtypeStruct(xs.shape, xs.dtype),
        grid_spec=pltpu.PrefetchScalarGridSpec(
            num_scalar_prefetch=5,
            grid=(nblk,),
            in_specs=[pl.BlockSpec((MOE_GROUP * PACKED_SUB, LANES), rowmap),
                      hbm, pl.BlockSpec((1, 1, F), bmap),
                      hbm, pl.BlockSpec((1, 1, F), bmap),
                      hbm, pl.BlockSpec((1, 1, D), bmap)],
            out_specs=pl.BlockSpec((MOE_GROUP * PACKED_SUB, LANES), lambda i, be, sl, nx, fl, nu: (i, 0)),
            scratch_shapes=[pltpu.VMEM((2, D, F), F32), pltpu.VMEM((2, D, F), F32), pltpu.VMEM((2, F, D), F32),
                            pltpu.VMEM((D, F), BF16), pltpu.VMEM((D, F), BF16), pltpu.VMEM((F, D), BF16),
                            pltpu.SemaphoreType.DMA((2, 3))],
        ),
        compiler_params=pltpu.CompilerParams(dimension_semantics=("arbitrary",),
                                             vmem_limit_bytes=VMEM_LIMIT_BYTES),
        name="experts",
    )(blk_expert, blk_slot, blk_next, blk_full, n_used, xs, wg, bg, wu, bu, wd, bd)


def _combine_kernel(alpha, dest_ref, gates_ref, x1_ref, g_ref, b_ref, y_ref, o_ref, ybuf, sems):
    i = pl.program_id(0)
    buf = i % 2

    n_tok = pl.num_programs(0) * COMBINE_TILE

    def gather_tile(tile, into):
        base = tile * COMBINE_TILE

        def start_rows(t, _):
            for k in range(MOE_TOPK):
                d = dest_ref[k * n_tok + base + t]
                pltpu.make_async_copy(y_ref.at[pl.ds(pl.multiple_of(d * PACKED_SUB, PACKED_SUB), PACKED_SUB)],
                                      ybuf.at[into, k, pl.ds(pl.multiple_of(t * PACKED_SUB, PACKED_SUB), PACKED_SUB)],
                                      sems.at[into]).start(priority=k % DMA_PRIORITIES)
            return 0

        lax.fori_loop(0, COMBINE_TILE, start_rows, 0, unroll=ISSUE_UNROLL)

    @pl.when(i == 0)
    def _():
        gather_tile(0, 0)

    @pl.when(i + 1 < pl.num_programs(0))
    def _():
        gather_tile(i + 1, 1 - buf)

    for k in range(MOE_TOPK):
        pltpu.make_async_copy(y_ref.at[pl.ds(0, COMBINE_TILE * PACKED_SUB)], ybuf.at[buf, k], sems.at[buf]).wait()

    gates = gates_ref[...]
    ffn = gates[:, 0:1] * _unpack_tiles_to_rows(ybuf.at[buf, 0], COMBINE_TILE).astype(F32)
    for k in range(1, MOE_TOPK):
        ffn = ffn + gates[:, k:k + 1] * _unpack_tiles_to_rows(ybuf.at[buf, k], COMBINE_TILE).astype(F32)
    o_ref[...] = _layer_norm(alpha * x1_ref[...] + ffn, g_ref[...], b_ref[...])


def _combine(dest, gates, x1, g2, b2, y_rows, alpha):
    N, D = x1.shape
    row = lambda i, d: (i, 0)
    full = lambda i, d: (0, 0)
    return pl.pallas_call(
        functools.partial(_combine_kernel, alpha),
        out_shape=jax.ShapeDtypeStruct((N, D), F32),
        grid_spec=pltpu.PrefetchScalarGridSpec(
            num_scalar_prefetch=1,
            grid=(N // COMBINE_TILE,),
            in_specs=[pl.BlockSpec((COMBINE_TILE, LANES), row), pl.BlockSpec((COMBINE_TILE, D), row),
                      pl.BlockSpec(g2.shape, full), pl.BlockSpec(b2.shape, full),
                      pl.BlockSpec(memory_space=pl.ANY)],
            out_specs=pl.BlockSpec((COMBINE_TILE, D), row),
            scratch_shapes=[pltpu.VMEM((2, MOE_TOPK, COMBINE_TILE * PACKED_SUB, LANES), y_rows.dtype),
                            pltpu.SemaphoreType.DMA((2,))],
        ),
        compiler_params=pltpu.CompilerParams(dimension_semantics=("arbitrary",),
                                             vmem_limit_bytes=VMEM_LIMIT_BYTES),
        name="combine",
    )(dest, gates, x1, g2, b2, y_rows)


def _rot_partner(w, half):
    return jnp.concatenate([-w[..., half:2 * half], w[..., :half]], axis=-1)


def _layer_weights(w_in, w_q_b, w_kv_b):
    D = w_in.shape[0]
    o1 = Q_LORA
    o2 = o1 + KV_LORA
    o3 = o2 + MLA_ROPE
    w_ql, w_kvl, w_kr = w_in[:, :o1], w_in[:, o1:o2], w_in[:, o2:o3]
    w_mq, w_mk, w_mv = w_in[:, o3:o3 + _MH], w_in[:, o3 + _MH:o3 + 2 * _MH], w_in[:, o3 + 2 * _MH:]
    zpad = lambda n: jnp.zeros((D, n), w_in.dtype)
    tail = SLOT - MLA_NOPE - MLA_ROPE
    assert tail == MLA_ROPE
    kr_slot = jnp.concatenate([zpad(MLA_NOPE), w_kr, _rot_partner(w_kr, MLA_ROPE // 2)], axis=1)

    win = jnp.concatenate([w_ql, w_kvl, kr_slot, w_mk], axis=1).astype(BF16)
    wmt = jnp.concatenate([w_mq, w_mv], axis=1).T.astype(BF16)

    wkv3 = w_kv_b.reshape(KV_LORA, MLA_HEADS, MLA_NOPE + MLA_V)
    wk = wkv3[..., :MLA_NOPE].reshape(KV_LORA, MLA_HEADS * MLA_NOPE).astype(BF16)
    wv = wkv3[..., MLA_NOPE:].reshape(KV_LORA, MLA_HEADS * MLA_V)
    return win, wmt, w_q_b.T.astype(BF16), wk, wv.T.astype(BF16)


def _rope_tables(positions):
    pos = positions.astype(F32).reshape(1, -1)

    def cs(d_rot):
        inv_freq = ROPE_THETA ** (-jnp.arange(0, d_rot, 2, dtype=F32) / d_rot)
        ang = inv_freq.reshape(d_rot // 2, 1) * pos
        return jnp.cos(ang), jnp.sin(ang)

    return cs(MLA_ROPE) + cs(MOBA_ROT)


def kernel(x, positions, w_in, q_a_norm, w_q_b, kv_a_norm, w_kv_b, w_o, ln1_g, ln1_b, w_router, b_router,
           w_gate, b_gate, w_up, b_up, w_down, b_down, ln2_g, ln2_b):
    B, T, D = x.shape
    depth = w_in.shape[0]
    alpha = (2.0 * depth) ** 0.25
    N = B * T
    assert T % MOBA_BLOCK == 0 and T // MOBA_BLOCK <= SUBLANES and N % OPROJ_TILE == 0
    assert D == SUBLANES * LANES
    n_asg = N * MOE_TOPK
    n_rows = n_asg + N_EXPERTS * MOE_GROUP
    tables = _rope_tables(positions)
    h = x.reshape(N, D)
    for l in range(depth):
        win, wmt, wqt, wk, wvt = _layer_weights(w_in[l], w_q_b[l], w_kv_b[l])
        qt, k, vt = _prep(h, win, wmt, wqt, wk, wvt, q_a_norm[l].reshape(1, -1), kv_a_norm[l].reshape(1, -1), tables, B, T)
        a = _attention(qt, k, vt, B, T, "attention")
        wr_pad = jnp.concatenate([w_router[l], jnp.zeros((D, LANES - N_EXPERTS), F32)], axis=1)
        wr_hi = wr_pad.astype(BF16)
        wr = jnp.concatenate([wr_hi, (wr_pad - wr_hi.astype(F32)).astype(BF16)], axis=1)
        br = b_router[l].reshape(N_EXPERTS, 1)
        x1, x1t, route, gates, cnt = _oproj(a, h, w_o[l].astype(BF16), ln1_g[l].reshape(1, D), ln1_b[l].reshape(1, D),
                                       wr, br, alpha)
        er = jnp.arange(N_EXPERTS, dtype=jnp.int32)
        counts = cnt[:, 0].astype(jnp.int32)
        padded = (counts + MOE_GROUP - 1) // MOE_GROUP * MOE_GROUP
        upto = er[None, :] <= er[:, None]
        pad_end = jnp.sum(jnp.where(upto, padded[None, :], 0), axis=1)
        pad_start = pad_end - padded
        e_idx = route[:MOE_TOPK]
        group_start = jnp.sum(jnp.where(e_idx[..., None] == er, pad_start, 0), axis=-1)
        dest = (group_start + route[MOE_TOPK:2 * MOE_TOPK]).reshape(n_asg)
        n_used = (pad_end[-1:] // MOE_GROUP).astype(jnp.int32)
        nblk = n_rows // MOE_GROUP
        blk_start = jnp.arange(nblk, dtype=jnp.int32) * MOE_GROUP
        blk_expert = jnp.minimum(jnp.sum((blk_start[:, None] >= pad_end[None, :]).astype(jnp.int32), axis=1),
                                 N_EXPERTS - 1)
        of_expert = blk_expert[:, None] == er[None, :]
        pick = lambda per_expert: jnp.sum(jnp.where(of_expert, per_expert[None, :], 0), axis=1)
        nonempty = padded > 0
        group_index = jnp.sum(jnp.where(upto & nonempty[None, :], 1, 0), axis=1) - 1
        later = (er[None, :] > er[:, None]) & nonempty[None, :]
        next_expert = jnp.min(jnp.where(later, er[None, :], N_EXPERTS), axis=1)
        next_expert = jnp.where(next_expert == N_EXPERTS, -1, next_expert)
        blk_slot = pick(group_index) % 2
        blk_next = pick(next_expert)
        blk_full = (pick(pad_start + counts) > blk_start + MOE_HALF).astype(jnp.int32)
        half_start = jnp.arange(n_rows // MOE_HALF, dtype=jnp.int32) * MOE_HALF
        half_expert = jnp.minimum(jnp.sum((half_start[:, None] >= pad_end[None, :]).astype(jnp.int32), axis=1),
                                  N_EXPERTS - 1)
        half_real_end = jnp.sum(jnp.where(half_expert[:, None] == er[None, :], (pad_start + counts)[None, :], 0), axis=1)
        no_real_rows = (half_start >= pad_end[-1]) | (half_start >= half_real_end)
        fill = jnp.where(no_real_rows, FILL_FREE,
                         jnp.where(half_start + MOE_HALF > half_real_end, FILL_TAIL, 0)).astype(jnp.int32)
        xs = _dispatch(dest, fill, x1t, n_rows)
        y_rows = _experts(blk_expert, blk_slot.astype(jnp.int32), blk_next, blk_full, n_used, xs, w_gate[l], b_gate[l].reshape(N_EXPERTS, 1, -1),
                          w_up[l], b_up[l].reshape(N_EXPERTS, 1, -1), w_down[l], b_down[l].reshape(N_EXPERTS, 1, -1))
        h = _combine(dest, gates, x1, ln2_g[l].reshape(1, D), ln2_b[l].reshape(1, D), y_rows, alpha)
    return h.reshape(B, T, D)
```
